```python
import math
import jax
import jax.numpy as jnp
from jax import lax
import numpy as np

D_MODEL = 1024
BATCH = 2
SEQ = 16384
DEPTH = 2

GRID_W = 64
CTX_LEN = 256
GLA_HEADS = 4
GLA_DK = 32
GLA_DV = 64
GLA_GATE_RANK = 16
GLA_TAU = 16.0
GLA_CHUNK = 64
HY_CH = 256
HY_ORDER = 2
HY_EMB = 33
HY_FILTER_HIDDEN = 64
HY_DECAY_TARGET = 1e-2
HY_FAST_DECAY = 0.3
HY_SLOW_DECAY = 1.5
SHORT_CONV = 3
MLA_HEADS = 8
MLA_Q_RANK = 256
MLA_KV_RANK = 128
MLA_NOPE = 64
MLA_ROPE = 32
MLA_V = 64
MLA_SCALE = (MLA_NOPE + MLA_ROPE) ** -0.5
ROPE_BASE = 10000.0
Q_BLOCK = 128
D_FF = 2816
N_EXPERTS = 8
TOP_K = 2
EXPERT_FF = 3584
D_MIX = GLA_HEADS * GLA_DV + HY_CH + MLA_HEADS * MLA_V
IN_SPLITS = (GLA_HEADS * GLA_DK, GLA_HEADS * GLA_DK, GLA_HEADS * GLA_DV, GLA_HEADS * GLA_DV,
             2 * GLA_GATE_RANK, 3 * HY_CH, MLA_Q_RANK, MLA_KV_RANK, MLA_ROPE)
IN_COLS = sum(IN_SPLITS)
F32 = jnp.float32

kernel_name = "hybrid_gla_hyena_mla_dit_block"


def _rms(x, g, eps=1e-6):
    xf = x.astype(F32)
    y = xf * lax.rsqrt(jnp.mean(xf * xf, axis=-1, keepdims=True) + eps)
    return (y * g.astype(F32)).astype(x.dtype)


def _layernorm(x, g, b, eps=1e-5):
    xf = x.astype(F32)
    mu = jnp.mean(xf, axis=-1, keepdims=True)
    var = jnp.mean(jnp.square(xf - mu), axis=-1, keepdims=True)
    return ((xf - mu) * lax.rsqrt(var + eps) * g.astype(F32) + b.astype(F32)).astype(x.dtype)


def _modulate(x, shift, scale):
    return x * (1 + scale) + shift


def _split_cols(p):
    cuts = [int(v) for v in np.cumsum(IN_SPLITS)[:-1]]
    return jnp.split(p, cuts, axis=-1)


def _heads(t, d):
    B, L, _ = t.shape
    return t.reshape(B, L, -1, d).transpose(0, 2, 1, 3).astype(F32)


def _flip(t):
    return t[:, :, ::-1]


def _gla_q(qa):
    return _heads(qa, GLA_DK) * GLA_DK ** -0.5


def _gla_kv(ka, va, alr, w_gate, b_gate):
    k = _heads(ka, GLA_DK)
    v = _heads(va, GLA_DV)
    log_a = []
    for d in range(2):
        z = alr[..., d * GLA_GATE_RANK:(d + 1) * GLA_GATE_RANK] @ w_gate[d] + b_gate[d]
        log_a.append(_heads(jax.nn.log_sigmoid(z.astype(F32)) / GLA_TAU, GLA_DK))
    return k, v, log_a[0], log_a[1]


def _gla_state(k, v, log_a):
    b = jnp.cumsum(log_a, axis=2)
    return jnp.einsum('bhtk,bhtv->bhkv', k * jnp.exp(b[:, :, -1:] - b), v)


def _gla_chunked(q, k, v, log_a, s0):
    B, H, L, DK = q.shape
    DV = v.shape[-1]
    n = L // GLA_CHUNK
    to_chunks = lambda t: jnp.moveaxis(t.reshape(B, H, n, GLA_CHUNK, t.shape[-1]), 2, 0)
    mask = jnp.tril(jnp.ones((GLA_CHUNK, GLA_CHUNK), dtype=bool))[:, :, None]

    def step(S, inp):
        qi, ki, vi, ai = inp
        b = jnp.cumsum(ai, axis=2)
        o_inter = jnp.einsum('bhtk,bhkv->bhtv', qi * jnp.exp(b), S)
        diff = b[:, :, :, None, :] - b[:, :, None, :, :]
        decay = jnp.where(mask, jnp.exp(jnp.minimum(diff, 0.0)), 0.0)
        att = jnp.einsum('bhtk,bhsk,bhtsk->bhts', qi, ki, decay)
        o = o_inter + jnp.einsum('bhts,bhsv->bhtv', att, vi)
        b_last = b[:, :, -1:, :]
        S_new = jnp.exp(b_last[:, :, 0, :])[..., None] * S + jnp.einsum(
            'bhsk,bhsv->bhkv', ki * jnp.exp(b_last - b), vi)
        return S_new, o

    _, o = lax.scan(step, s0, (to_chunks(q), to_chunks(k), to_chunks(v), to_chunks(log_a)))
    return jnp.moveaxis(o, 0, 2).reshape(B, H, L, DV)


def _gla_bidir(q, k, v, la_f, la_b, s_f, s_b):
    o_f = _gla_chunked(q, k, v, la_f, s_f)
    o_b = _flip(_gla_chunked(_flip(q), _flip(k), _flip(v), _flip(la_b), s_b))
    return o_f + o_b


def _gla_out(o, g, norm_g):
    B, H, L, DV = o.shape
    o = o.transpose(0, 2, 1, 3).astype(g.dtype)
    o = _rms(o, norm_g) * jax.nn.silu(g.reshape(B, L, H, DV))
    return o.reshape(B, L, H * DV)


def _short_conv(u, w, b):
    L = u.shape[1]
    pad = SHORT_CONV // 2
    up = jnp.pad(u, ((0, 0), (pad, pad), (0, 0)))
    y = b
    for j in range(SHORT_CONV):
        y = y + up[:, j:j + L] * w[j]
    return y


def _hyena_filters(L, fw1, fb1, ff1, fw2, fb2, ff2, fw3, fb3):
    pos = jnp.arange(L, dtype=F32)
    t = pos / (L - 1)
    bands = (HY_EMB - 1) // 2
    freqs = jnp.linspace(1e-4, bands - 1, bands, dtype=F32)
    ang = (2.0 * math.pi * pos / L)[:, None] * freqs
    z = jnp.concatenate([t[:, None], jnp.cos(ang), -jnp.sin(ang)], axis=-1)
    hid = jnp.sin(ff1.astype(F32) * (z @ fw1.astype(F32) + fb1.astype(F32)))
    hid = jnp.sin(ff2.astype(F32) * (hid @ fw2.astype(F32) + fb2.astype(F32)))
    h = (hid @ fw3.astype(F32) + fb3.astype(F32)).reshape(L, HY_ORDER, 2, HY_CH)
    deltas = jnp.abs(jnp.linspace(math.log(HY_DECAY_TARGET) / HY_SLOW_DECAY,
                                  math.log(HY_DECAY_TARGET) / HY_FAST_DECAY, HY_CH, dtype=F32))
    h = h * jnp.exp(-t[:, None, None, None] * deltas)
    return h * lax.rsqrt(jnp.sum(h * h, axis=(0, 2), keepdims=True) + 1e-6)


def _bidir_fftconv(u, h_fwd, h_bwd, d_skip):
    L = u.shape[1]
    h_circ = jnp.concatenate([h_fwd, jnp.zeros_like(h_fwd[:1]), h_bwd[:0:-1]], axis=0)
    hf = jnp.fft.rfft(h_circ, n=2 * L, axis=0)
    uf = jnp.fft.rfft(u.astype(F32), n=2 * L, axis=1)
    y = jnp.fft.irfft(uf * hf[None], n=2 * L, axis=1)[:, :L]
    return (y + u.astype(F32) * d_skip.astype(F32)).astype(u.dtype)


def _hyena(u, conv_w, conv_b, filt, skip):
    L = u.shape[1]
    u = _short_conv(u, conv_w, conv_b)
    v, x1, x2 = jnp.split(u, 3, axis=-1)
    h = _hyena_filters(L, *filt)
    z = x1 * _bidir_fftconv(v, h[:, 0, 0], h[:, 0, 1], skip[0])
    return x2 * _bidir_fftconv(z, h[:, 1, 0], h[:, 1, 1], skip[1])


def _rope_2d(x, row, col):
    half = x.shape[-1] // 2
    inv = ROPE_BASE ** (-jnp.arange(0, half, 2, dtype=F32) / half)

    def rot(xa, pos):
        ang = pos.astype(F32)[:, None, None] * inv
        cos = jnp.cos(ang).astype(x.dtype)
        sin = jnp.sin(ang).astype(x.dtype)
        x1, x2 = jnp.split(xa, 2, axis=-1)
        return jnp.concatenate([x1 * cos - x2 * sin, x1 * sin + x2 * cos], axis=-1)

    return jnp.concatenate([rot(x[..., :half], row), rot(x[..., half:], col)], axis=-1)


def _mla_q(cq, qn_g, w_uq, row, col):
    B, L, _ = cq.shape
    q = (_rms(cq, qn_g) @ w_uq).reshape(B, L, MLA_HEADS, MLA_NOPE + MLA_ROPE)
    if row is None:
        return q
    return jnp.concatenate([q[..., :MLA_NOPE], _rope_2d(q[..., MLA_NOPE:], row, col)], axis=-1)


def _mla_kv(ckv, kr, kvn_g, w_ukv, row, col):
    B, L, _ = ckv.shape
    kv = (_rms(ckv, kvn_g) @ w_ukv).reshape(B, L, MLA_HEADS, MLA_NOPE + MLA_V)
    k_nope, v = kv[..., :MLA_NOPE], kv[..., MLA_NOPE:]
    kr = kr[:, :, None, :]
    if row is not None:
        kr = _rope_2d(kr, row, col)
    k = jnp.concatenate([k_nope, jnp.broadcast_to(kr, (B, L, MLA_HEADS, MLA_ROPE))], axis=-1)
    return k, v


def _attend(q, k, v):
    s = jnp.einsum('bqhd,bkhd->bhqk', q, k).astype(F32) * MLA_SCALE
    p = jax.nn.softmax(s, axis=-1).astype(v.dtype)
    return jnp.einsum('bhqk,bkhd->bqhd', p, v)


def _attend_blocked(q, k, v):
    B, L, H, D = q.shape
    nb = L // Q_BLOCK
    qb = jnp.moveaxis(q.reshape(B, nb, Q_BLOCK, H, D), 1, 0)
    ob = lax.map(lambda t: _attend(t, k, v), qb)
    return jnp.moveaxis(ob, 0, 1).reshape(B, L, H, -1)


def _token_mixers(ph, pc, row, col, need_ctx, gla_p, hy_p, mla_p):
    B, L, _ = ph.shape
    qa, ka, va, ga, alr, hyu, cq, ckv, kr = _split_cols(ph)
    qa_c, ka_c, va_c, ga_c, alr_c, hyu_c, cq_c, ckv_c, kr_c = _split_cols(pc)
    w_gate, b_gate, gla_g = gla_p
    conv_w, conv_b, filt, skip, hy_g = hy_p
    qn_g, w_uq, kvn_g, w_ukv, mla_g = mla_p

    k_c, v_c, la_cf, la_cb = _gla_kv(ka_c, va_c, alr_c, w_gate, b_gate)
    s_f = _gla_state(k_c, v_c, la_cf)
    s_b = _gla_state(_flip(k_c), _flip(v_c), _flip(la_cb))
    km_c, vm_c = _mla_kv(ckv_c, kr_c, kvn_g, w_ukv, None, None)

    k, v, la_f, la_b = _gla_kv(ka, va, alr, w_gate, b_gate)
    y_a = _gla_out(_gla_bidir(_gla_q(qa), k, v, la_f, la_b, s_f, s_b), ga, gla_g)
    y_b = _rms(_hyena(hyu, conv_w, conv_b, filt, skip), hy_g)
    km, vm = _mla_kv(ckv, kr, kvn_g, w_ukv, row, col)
    om = _attend_blocked(_mla_q(cq, qn_g, w_uq, row, col),
                         jnp.concatenate([km_c, km], axis=1), jnp.concatenate([vm_c, vm], axis=1))
    y_c = _rms(om.reshape(B, L, -1), mla_g)
    y = jnp.concatenate([y_a, y_b, y_c], axis=-1)
    if not need_ctx:
        return y, None

    Lc = pc.shape[1]
    z0 = jnp.zeros_like(s_f)
    yc_a = _gla_out(_gla_bidir(_gla_q(qa_c), k_c, v_c, la_cf, la_cb, z0, z0), ga_c, gla_g)
    yc_b = _rms(_hyena(hyu_c, conv_w, conv_b, filt, skip), hy_g)
    omc = _attend(_mla_q(cq_c, qn_g, w_uq, None, None), km_c, vm_c)
    yc_c = _rms(omc.reshape(B, Lc, -1), mla_g)
    return y, jnp.concatenate([yc_a, yc_b, yc_c], axis=-1)


def _swiglu(h, w1, w3, w2):
    return (jax.nn.silu(h @ w1) * (h @ w3)) @ w2


def _moe(h, w_router, w1, w3, w2):
    logits = (h @ w_router).astype(F32)
    top_v, top_i = lax.top_k(logits, TOP_K)
    top_w = jax.nn.softmax(top_v, axis=-1)
    gates = jnp.sum(jax.nn.one_hot(top_i, N_EXPERTS, dtype=F32) * top_w[..., None], axis=-2).astype(h.dtype)
    out = jnp.zeros_like(h)
    for e in range(N_EXPERTS):
        out = out + gates[..., e:e + 1] * _swiglu(h, w1[e], w3[e], w2[e])
    return out


def setup_inputs(seed: int = 0) -> dict:
    key = jax.random.key(seed)
    ks = iter(jax.random.split(key, 48))
    nrm = lambda shape, scale=1.0: scale * jax.random.normal(next(ks), shape, F32)
    gain = lambda shape: 1.0 + 0.1 * jax.random.normal(next(ks), shape, F32)
    D = D_MODEL
    Hf = HY_FILTER_HIDDEN
    beta = (8.0 * DEPTH) ** -0.25
    n_dense = (DEPTH + 1) // 2
    n_moe = DEPTH // 2
    return {
        "x": nrm((BATCH, SEQ, D)),
        "c": nrm((BATCH, D)),
        "ctx": nrm((BATCH, CTX_LEN, D)),
        "c_ctx": nrm((D,)),
        "w_mod": nrm((DEPTH, D, 6 * D), 0.5 * D ** -0.5),
        "b_mod": nrm((DEPTH, 6 * D), 0.01),
        "w_in": nrm((DEPTH, D, IN_COLS), D ** -0.5),
        "gla_w_gate": nrm((DEPTH, 2, GLA_GATE_RANK, GLA_HEADS * GLA_DK), GLA_GATE_RANK ** -0.5),
        "gla_b_gate": nrm((DEPTH, 2, GLA_HEADS * GLA_DK), 0.1),
        "gla_norm_g": gain((DEPTH, GLA_DV)),
        "hy_conv_w": nrm((DEPTH, SHORT_CONV, 3 * HY_CH), SHORT_CONV ** -0.5),
        "hy_conv_b": nrm((DEPTH, 3 * HY_CH), 0.01),
        "hy_f_w1": nrm((DEPTH, HY_EMB, Hf), HY_EMB ** -0.5),
        "hy_f_b1": nrm((DEPTH, Hf), 0.1),
        "hy_f_freq1": gain((DEPTH, Hf)),
        "hy_f_w2": nrm((DEPTH, Hf, Hf), Hf ** -0.5),
        "hy_f_b2": nrm((DEPTH, Hf), 0.1),
        "hy_f_freq2": gain((DEPTH, Hf)),
        "hy_f_w3": nrm((DEPTH, Hf, HY_ORDER * 2 * HY_CH), Hf ** -0.5),
        "hy_f_b3": nrm((DEPTH, HY_ORDER * 2 * HY_CH), 0.1),
        "hy_skip": nrm((DEPTH, HY_ORDER, HY_CH)),
        "hy_norm_g": gain((DEPTH, HY_CH)),
        "mla_q_norm_g": gain((DEPTH, MLA_Q_RANK)),
        "mla_w_uq": nrm((DEPTH, MLA_Q_RANK, MLA_HEADS * (MLA_NOPE + MLA_ROPE)), MLA_Q_RANK ** -0.5),
        "mla_kv_norm_g": gain((DEPTH, MLA_KV_RANK)),
        "mla_w_ukv": nrm((DEPTH, MLA_KV_RANK, MLA_HEADS * (MLA_NOPE + MLA_V)), MLA_KV_RANK ** -0.5),
        "mla_norm_g": gain((DEPTH, MLA_HEADS * MLA_V)),
        "w_out": nrm((DEPTH, D_MIX, D), beta * D_MIX ** -0.5),
        "ln_g": gain((DEPTH, 2, D)),
        "ln_b": nrm((DEPTH, 2, D), 0.01),
        "ffn_w1": nrm((n_dense, D, D_FF), D ** -0.5),
        "ffn_w3": nrm((n_dense, D, D_FF), D ** -0.5),
        "ffn_w2": nrm((n_dense, D_FF, D), beta * D_FF ** -0.5),
        "moe_router": nrm((n_moe, D, N_EXPERTS), D ** -0.5),
        "moe_w1": nrm((n_moe, N_EXPERTS, D, EXPERT_FF), D ** -0.5),
        "moe_w3": nrm((n_moe, N_EXPERTS, D, EXPERT_FF), D ** -0.5),
        "moe_w2": nrm((n_moe, N_EXPERTS, EXPERT_FF, D), beta * EXPERT_FF ** -0.5),
    }


def reference(x, c, ctx, c_ctx, w_mod, b_mod, w_in, gla_w_gate, gla_b_gate, gla_norm_g,
              hy_conv_w, hy_conv_b, hy_f_w1, hy_f_b1, hy_f_freq1, hy_f_w2, hy_f_b2, hy_f_freq2,
              hy_f_w3, hy_f_b3, hy_skip, hy_norm_g, mla_q_norm_g, mla_w_uq, mla_kv_norm_g, mla_w_ukv,
              mla_norm_g, w_out, ln_g, ln_b, ffn_w1, ffn_w3, ffn_w2, moe_router, moe_w1, moe_w3, moe_w2):
    B, L, _ = x.shape
    ROWS = L // GRID_W
    row = jnp.broadcast_to(jnp.arange(ROWS, dtype=jnp.int32)[:, None], (ROWS, GRID_W)).reshape(-1)
    col = jnp.broadcast_to(jnp.arange(GRID_W, dtype=jnp.int32)[None, :], (ROWS, GRID_W)).reshape(-1)
    alpha = (2.0 * DEPTH) ** 0.25
    s_lat = jax.nn.silu(c)
    s_ctx = jax.nn.silu(c_ctx)
    xc = ctx
    for l in range(DEPTH):
        need_ctx = l < DEPTH - 1
        m = jnp.split((s_lat @ w_mod[l] + b_mod[l])[:, None, :], 6, axis=-1)
        mc = jnp.split(s_ctx @ w_mod[l] + b_mod[l], 6, axis=-1)
        gla_p = (gla_w_gate[l], gla_b_gate[l], gla_norm_g[l])
        filt = (hy_f_w1[l], hy_f_b1[l], hy_f_freq1[l], hy_f_w2[l], hy_f_b2[l], hy_f_freq2[l],
                hy_f_w3[l], hy_f_b3[l])
        hy_p = (hy_conv_w[l], hy_conv_b[l], filt, hy_skip[l], hy_norm_g[l])
        mla_p = (mla_q_norm_g[l], mla_w_uq[l], mla_kv_norm_g[l], mla_w_ukv[l], mla_norm_g[l])

        ph = _modulate(x, m[0], m[1]) @ w_in[l]
        pc = _modulate(xc, mc[0], mc[1]) @ w_in[l]
        y, yc = _token_mixers(ph, pc, row, col, need_ctx, gla_p, hy_p, mla_p)
        x = _layernorm(alpha * x + m[2] * (y @ w_out[l]), ln_g[l, 0], ln_b[l, 0])
        if need_ctx:
            xc = _layernorm(alpha * xc + mc[2] * (yc @ w_out[l]), ln_g[l, 0], ln_b[l, 0])

        def ffn(h, i=l // 2, dense=(l % 2 == 0)):
            if dense:
                return _swiglu(h, ffn_w1[i], ffn_w3[i], ffn_w2[i])
            return _moe(h, moe_router[i], moe_w1[i], moe_w3[i], moe_w2[i])

        x = _layernorm(alpha * x + m[5] * ffn(_modulate(x, m[3], m[4])), ln_g[l, 1], ln_b[l, 1])
        if need_ctx:
            xc = _layernorm(alpha * xc + mc[5] * ffn(_modulate(xc, mc[3], mc[4])), ln_g[l, 1], ln_b[l, 1])
    return x
```

```python
import functools
import math

import numpy as np
import jax
import jax.numpy as jnp
from jax import lax
from jax.experimental import pallas as pl
from jax.experimental.pallas import tpu as pltpu

F32 = jnp.float32
BF16 = jnp.bfloat16
HI = lax.Precision.HIGHEST

GRID_W = 64
GLA_HEADS, GLA_DK, GLA_DV, GLA_RANK, GLA_TAU = 4, 32, 64, 16, 16.0
HY_CH, HY_EMB = 256, 33
HY_DECAY_TARGET, HY_FAST_DECAY, HY_SLOW_DECAY = 1e-2, 0.3, 1.5
MLA_HEADS, MLA_Q_RANK, MLA_KV_RANK, MLA_NOPE, MLA_ROPE, MLA_V = 8, 256, 128, 64, 32, 64
MLA_SCALE = (MLA_NOPE + MLA_ROPE) ** -0.5
ROPE_BASE = 10000.0
N_EXPERTS = 8
IN_SPLITS = (128, 128, 256, 256, 32, 768, 256, 128, 32)

LANES = 128
VMEM_LIMIT = 56 * 1024 * 1024

GLA_CHUNK = 128
DFT_N2 = 256


def _cp(*sem):
    return pltpu.CompilerParams(dimension_semantics=sem, vmem_limit_bytes=VMEM_LIMIT)


def _full(shape):
    n = len(shape)
    return pl.BlockSpec(shape, lambda *_: (0,) * n)


def _idiv(x, d):
    assert d & (d - 1) == 0
    return lax.shift_right_logical(x, int(math.log2(d)))


INPROJ_WIDTHS = (768, 256, 512, 128, 256, 256)


def _arrange_w_in(w):
    cuts = np.cumsum(IN_SPLITS)[:-1]
    qa, ka, va, ga, alr, hyu, cq, ckv, kr = jnp.split(w, [int(c) for c in cuts], axis=1)
    z96 = jnp.zeros((w.shape[0], 96), w.dtype)
    return jnp.concatenate([hyu, qa, ka, va, ga, alr, z96, cq, ckv, kr, z96], axis=1).astype(BF16)


def _inproj_body(x_ref, sh_ref, sc_ref, w_ref, *out_refs):
    h = x_ref[0] * (1.0 + sc_ref[0]) + sh_ref[0]
    acc = jnp.dot(h.astype(BF16), w_ref[...], preferred_element_type=F32)
    off = 0
    for r in out_refs:
        w = r.shape[-1]
        r[0] = acc[:, off:off + w]
        off += w


def _inproj(x, shift, scale, w_arr):
    B, L, D = x.shape
    tm = min(512, L)
    n = w_arr.shape[1]
    row = lambda w: pl.BlockSpec((1, tm, w), lambda b, i: (b, i, 0))
    vec = pl.BlockSpec((1, 1, D), lambda b, i: (b, 0, 0))
    return pl.pallas_call(
        _inproj_body,
        grid=(B, L // tm),
        in_specs=[row(D), vec, vec, _full((D, n))],
        out_specs=[row(w) for w in INPROJ_WIDTHS],
        out_shape=[jax.ShapeDtypeStruct((B, L, w), F32) for w in INPROJ_WIDTHS],
        compiler_params=_cp("parallel", "parallel"),
        name="inproj",
    )(x, shift, scale, w_arr)


def _log_sigmoid(z):
    return jnp.minimum(z, 0.0) - jnp.log1p(jnp.exp(-jnp.abs(z)))


def _gla_dir(qk, v, la, s_ref, d, tri):
    C = qk.shape[0]
    KD = GLA_HEADS * GLA_DK
    VD = GLA_HEADS * GLA_DV
    q = qk[:, :KD] * (GLA_DK ** -0.5)
    k = qk[:, KD:]
    b = jnp.dot(tri, la, precision=HI, preferred_element_type=F32)
    tot = jnp.sum(la, axis=0, keepdims=True)
    qe = q * jnp.exp(b)
    ke = (k * jnp.exp(-b)).astype(BF16)
    kl = (k * jnp.exp(tot - b)).astype(BF16)
    s_old = s_ref[d]
    vb = v.astype(BF16)
    o = jnp.dot(qe.astype(BF16), s_old.astype(BF16), preferred_element_type=F32)
    lane_k = _idiv(lax.broadcasted_iota(jnp.int32, (1, KD), 1), GLA_DK)
    lane_v = _idiv(lax.broadcasted_iota(jnp.int32, (1, VD), 1), GLA_DV)
    for h in range(GLA_HEADS):
        qh = jnp.where(lane_k == h, qe, 0.0).astype(BF16)
        att = lax.dot_general(qh, ke, (((1,), (1,)), ((), ())), preferred_element_type=F32)
        att = (att * tri).astype(BF16)
        oh = jnp.dot(att, vb, preferred_element_type=F32)
        o = o + jnp.where(lane_v == h, oh, 0.0)
    ones = jnp.ones((C, VD), F32)
    tot_b = lax.dot_general(la, ones, (((0,), (0,)), ((), ())), precision=HI, preferred_element_type=F32)
    kv = lax.dot_general(kl, vb, (((0,), (0,)), ((), ())), preferred_element_type=F32)
    rk = _idiv(lax.broadcasted_iota(jnp.int32, (KD, VD), 0), GLA_DK)
    cv = _idiv(lax.broadcasted_iota(jnp.int32, (KD, VD), 1), GLA_DV)
    s_ref[d] = jnp.exp(tot_b) * s_old + jnp.where(rk == cv, kv, 0.0)
    return o


def _gla_body(qkf_ref, vf_ref, af_ref, qkb_ref, vb_ref, ab_ref, wg_ref, bg_ref, s0_ref,
              of_ref, ob_ref, sout_ref, s_ref):
    i = pl.program_id(1)
    C = qkf_ref.shape[1]
    KD = GLA_HEADS * GLA_DK

    @pl.when(i == 0)
    def _():
        s_ref[...] = s0_ref[0]

    r = lax.broadcasted_iota(jnp.int32, (C, C), 0)
    c = lax.broadcasted_iota(jnp.int32, (C, C), 1)
    tri_lo = (c <= r).astype(F32)
    tri_up = (c >= r).astype(F32)
    zf = jnp.dot(af_ref[0], wg_ref[...], precision=HI, preferred_element_type=F32) + bg_ref[...]
    zb = jnp.dot(ab_ref[0], wg_ref[...], precision=HI, preferred_element_type=F32) + bg_ref[...]
    la_f = _log_sigmoid(zf[:, :KD]) / GLA_TAU
    la_b = _log_sigmoid(zb[:, KD:]) / GLA_TAU
    of_ref[0] = _gla_dir(qkf_ref[0], vf_ref[0], la_f, s_ref, 0, tri_lo)
    ob_ref[0] = _gla_dir(qkb_ref[0], vb_ref[0], la_b, s_ref, 1, tri_up)

    @pl.when(i == pl.num_programs(1) - 1)
    def _():
        sout_ref[0] = s_ref[...]


def _gla(qk, vg, alr, wg, bg, s0):
    B, L, _ = qk.shape
    C = min(GLA_CHUNK, L)
    n = L // C
    KD, VD = GLA_HEADS * GLA_DK, GLA_HEADS * GLA_DV
    fwd = lambda w: pl.BlockSpec((1, C, w), lambda b, i: (b, i, 0))
    bwd = lambda w: pl.BlockSpec((1, C, w), lambda b, i: (b, n - 1 - i, 0))
    st = pl.BlockSpec((1, 2, KD, VD), lambda b, i: (b, 0, 0, 0))
    return pl.pallas_call(
        _gla_body,
        grid=(B, n),
        in_specs=[fwd(2 * KD), fwd(VD), fwd(LANES), bwd(2 * KD), bwd(VD), bwd(LANES),
                  _full((LANES, 2 * KD)), _full((1, 2 * KD)), st],
        out_specs=[fwd(VD), bwd(VD), st],
        out_shape=[jax.ShapeDtypeStruct((B, L, VD), F32), jax.ShapeDtypeStruct((B, L, VD), F32),
                   jax.ShapeDtypeStruct((B, 2, KD, VD), F32)],
        scratch_shapes=[pltpu.VMEM((2, KD, VD), F32)],
        compiler_params=_cp("parallel", "arbitrary"),
        name="gla",
    )(qk, vg, alr, qk, vg, alr, wg, bg, s0)


def _arrange_gate(w_gate, b_gate):
    KD = GLA_HEADS * GLA_DK
    wg = jnp.zeros((LANES, 2 * KD), F32)
    wg = wg.at[:GLA_RANK, :KD].set(w_gate[0]).at[GLA_RANK:2 * GLA_RANK, KD:].set(w_gate[1])
    return wg, jnp.concatenate([b_gate[0], b_gate[1]])[None, :]


def _shortconv_body(x_ref, p_ref, n_ref, w_ref, b_ref, v_ref, x1_ref, x2_ref):
    i = pl.program_id(1)
    last = pl.num_programs(1) - 1
    x = x_ref[0]
    tm = x.shape[0]
    prev = jnp.where(i > 0, p_ref[0][7:8, :], 0.0)
    nxt = jnp.where(i < last, n_ref[0][0:1, :], 0.0)
    rid = lax.broadcasted_iota(jnp.int32, x.shape, 0)
    dn = jnp.where(rid == 0, prev, pltpu.roll(x, 1, 0))
    up = jnp.where(rid == tm - 1, nxt, pltpu.roll(x, tm - 1, 0))
    w = w_ref[...]
    y = b_ref[...] + dn * w[0:1] + x * w[1:2] + up * w[2:3]
    v_ref[0] = y[:, :HY_CH]
    x1_ref[0] = y[:, HY_CH:2 * HY_CH]
    x2_ref[0] = y[:, 2 * HY_CH:]


def _shortconv(u, w, b):
    B, L, W = u.shape
    tm = min(512, L)
    nb = tm // 8
    row = pl.BlockSpec((1, tm, W), lambda b_, i: (b_, i, 0))
    prev = pl.BlockSpec((1, 8, W), lambda b_, i: (b_, jnp.maximum(i * nb - 1, 0), 0))
    nxt = pl.BlockSpec((1, 8, W), lambda b_, i: (b_, jnp.minimum((i + 1) * nb, L // 8 - 1), 0))
    o = pl.BlockSpec((1, tm, HY_CH), lambda b_, i: (b_, i, 0))
    return pl.pallas_call(
        _shortconv_body,
        grid=(B, L // tm),
        in_specs=[row, prev, nxt, _full((3, W)), _full((1, W))],
        out_specs=[o, o, o],
        out_shape=[jax.ShapeDtypeStruct((B, L, HY_CH), F32)] * 3,
        compiler_params=_cp("parallel", "parallel"),
        name="shortconv",
    )(u, u, u, w, b[None, :])


def _filter_feats(L):
    pos = jnp.arange(L, dtype=F32)
    t = pos / (L - 1)
    bands = (HY_EMB - 1) // 2
    freqs = jnp.linspace(1e-4, bands - 1, bands, dtype=F32)
    ang = (2.0 * math.pi * pos / L)[:, None] * freqs
    z = jnp.concatenate([t[:, None], jnp.cos(ang), -jnp.sin(ang)], axis=-1)
    z = jnp.pad(z, ((0, 0), (0, LANES - HY_EMB)))
    deltas = jnp.abs(jnp.linspace(math.log(HY_DECAY_TARGET) / HY_SLOW_DECAY,
                                  math.log(HY_DECAY_TARGET) / HY_FAST_DECAY, HY_CH, dtype=F32))
    return z, jnp.tile(deltas, 4)[None, :]


def _filter_body(z_ref, w1_ref, b1_ref, f1_ref, w2_ref, b2_ref, f2_ref, w3_ref, b3_ref, dl_ref,
                 h_ref, ss_ref, *, L):
    i = pl.program_id(0)
    z = z_ref[...]
    tm = z.shape[0]
    hid = jnp.sin(f1_ref[...] * (jnp.dot(z, w1_ref[...], precision=HI, preferred_element_type=F32) + b1_ref[...]))
    hid = jnp.sin(f2_ref[...] * (jnp.dot(hid, w2_ref[...], precision=HI, preferred_element_type=F32) + b2_ref[...]))
    h = jnp.dot(hid, w3_ref[...], precision=HI, preferred_element_type=F32) + b3_ref[...]
    pos = (lax.broadcasted_iota(jnp.int32, (tm, 1), 0) + i * tm).astype(F32)
    t = pos / (L - 1)
    h = h * jnp.exp(-t * dl_ref[...])

    @pl.when(i == 0)
    def _():
        ss_ref[...] = jnp.zeros_like(ss_ref)

    ss_ref[...] += jnp.sum(h * h, axis=0, keepdims=True)
    col = lax.broadcasted_iota(jnp.int32, h.shape, 1)
    is_bwd = (_idiv(col, HY_CH) & 1) == 1
    h_ref[...] = jnp.where(jnp.logical_and(is_bwd, pos == 0.0), 0.0, h)


def _filters(L, fw1, fb1, ff1, fw2, fb2, ff2, fw3, fb3):
    z, dl = _filter_feats(L)
    tm = min(1024, L)
    Hf = fw2.shape[0]
    w1 = jnp.pad(fw1, ((0, LANES - HY_EMB), (0, 0)))
    NC = fw3.shape[1]
    return pl.pallas_call(
        functools.partial(_filter_body, L=L),
        grid=(L // tm,),
        in_specs=[pl.BlockSpec((tm, LANES), lambda i: (i, 0)), _full((LANES, Hf)), _full((1, Hf)), _full((1, Hf)),
                  _full((Hf, Hf)), _full((1, Hf)), _full((1, Hf)), _full((Hf, NC)), _full((1, NC)), _full((1, NC))],
        out_specs=[pl.BlockSpec((tm, NC), lambda i: (i, 0)), _full((1, NC))],
        out_shape=[jax.ShapeDtypeStruct((L, NC), F32), jax.ShapeDtypeStruct((1, NC), F32)],
        compiler_params=_cp("arbitrary"),
        name="hy_filters",
    )(z, w1, fb1[None], ff1[None], fw2, fb2[None], ff2[None], fw3, fb3[None], dl)


def _dft_consts(L):
    N = 2 * L
    N2 = DFT_N2
    N1 = N // N2
    half = N1 // 2
    k1 = np.arange(N1)[:, None].astype(np.float64)
    n1 = np.arange(N1)[None, :].astype(np.float64)
    a1 = 2.0 * np.pi * k1 * n1 / N1
    f1r, f1i = np.cos(a1), -np.sin(a1)
    fa = np.concatenate([f1r[:, :half], f1i[:, :half]], axis=0)
    fb = np.concatenate([f1r[:half, :], f1i[:half, :]], axis=1) / N
    k2 = np.arange(N2)[:, None].astype(np.float64)
    n2 = np.arange(N2)[None, :].astype(np.float64)
    a2 = 2.0 * np.pi * k2 * n2 / N2
    f2r, f2i = np.cos(a2), -np.sin(a2)
    g = np.block([[f2r, -f2i], [f2i, f2r]])
    gc = np.block([[f2r, f2i], [-f2i, f2r]])
    at = 2.0 * np.pi * (np.arange(N1)[:, None] * np.arange(N2)[None, :] % N) / N
    twr, twi = np.cos(at), -np.sin(at)
    c = lambda a: jnp.asarray(a, dtype=F32)
    bc = lambda a: jnp.broadcast_to(c(a)[:, :, None], (N1, N2, LANES))
    return dict(N1=N1, N2=N2, half=half, fa=c(fa), fb=c(fb), g=c(g), gc=c(gc), twr=bc(twr), twi=bc(twi))


def _lanes(t, width):
    return jnp.concatenate([t] * (width // LANES), axis=-1)


def _dft1_body(f_ref, x_ref, o_ref):
    o_ref[0] = jnp.dot(f_ref[...], x_ref[0].astype(BF16), preferred_element_type=F32)


def _dft_stage1(fa, x):
    B, half, M = x.shape
    R = fa.shape[0]
    tn = min(4096, M)
    return pl.pallas_call(
        _dft1_body,
        grid=(B, M // tn),
        in_specs=[_full((R, half)), pl.BlockSpec((1, half, tn), lambda b, j: (b, 0, j))],
        out_specs=pl.BlockSpec((1, R, tn), lambda b, j: (b, 0, j)),
        out_shape=jax.ShapeDtypeStruct((B, R, M), F32),
        compiler_params=_cp("parallel", "parallel"),
        name="hy_dft1",
    )(fa.astype(BF16), x)


def _filter_spec_body(a_ref, twr_ref, twi_ref, g_ref, ss_ref, hf_ref):
    W = a_ref.shape[-1]
    ar, ai = a_ref[0, 0], a_ref[1, 0]
    twr, twi = _lanes(twr_ref[0], W), _lanes(twi_ref[0], W)
    xr = ar * twr - ai * twi
    xi = ar * twi + ai * twr
    z = jnp.dot(g_ref[...], jnp.concatenate([xr, xi], axis=0).astype(BF16), preferred_element_type=F32)
    n2 = z.shape[0] // 2
    zr, zi = z[:n2], z[n2:]
    ss = ss_ref[...]
    for o in range(2):
        f0, b0 = (2 * o) * HY_CH, (2 * o + 1) * HY_CH
        sc = lax.rsqrt(ss[:, f0:f0 + HY_CH] + ss[:, b0:b0 + HY_CH] + 1e-6)
        hf_ref[o, 0, 0] = (zr[:, f0:f0 + HY_CH] + zr[:, b0:b0 + HY_CH]) * sc
        hf_ref[o, 0, 1] = (zi[:, f0:f0 + HY_CH] - zi[:, b0:b0 + HY_CH]) * sc


def _filter_spectrum(h, ss, dc):
    L, NC = h.shape
    N1, N2, half = dc["N1"], dc["N2"], dc["half"]
    a = _dft_stage1(dc["fa"], h.reshape(1, half, N2 * NC))
    a = a.reshape(2, N1, N2, NC)
    return pl.pallas_call(
        _filter_spec_body,
        grid=(N1,),
        in_specs=[pl.BlockSpec((2, 1, N2, NC), lambda k: (0, k, 0, 0)),
                  pl.BlockSpec((1, N2, LANES), lambda k: (k, 0, 0)), pl.BlockSpec((1, N2, LANES), lambda k: (k, 0, 0)),
                  _full((2 * N2, 2 * N2)), _full((1, NC))],
        out_specs=pl.BlockSpec((2, 1, 2, N2, HY_CH), lambda k: (0, k, 0, 0, 0)),
        out_shape=jax.ShapeDtypeStruct((2, N1, 2, N2, HY_CH), F32),
        compiler_params=_cp("parallel"),
        name="hy_filter_spec",
    )(a, dc["twr"], dc["twi"], dc["g"].astype(BF16), ss)


def _spec_mul_body(a_ref, twr_ref, twi_ref, g_ref, gc_ref, hf_ref, o_ref):
    W = a_ref.shape[-1]
    ar, ai = a_ref[0, 0, 0], a_ref[0, 1, 0]
    twr, twi = _lanes(twr_ref[0], W), _lanes(twi_ref[0], W)
    xr = ar * twr - ai * twi
    xi = ar * twi + ai * twr
    z = jnp.dot(g_ref[...], jnp.concatenate([xr, xi], axis=0).astype(BF16), preferred_element_type=F32)
    n2 = z.shape[0] // 2
    zr, zi = z[:n2], z[n2:]
    hr, hi = hf_ref[0, 0], hf_ref[0, 1]
    yr = zr * hr - zi * hi
    yi = zr * hi + zi * hr
    b = jnp.dot(gc_ref[...], jnp.concatenate([yr, yi], axis=0).astype(BF16), preferred_element_type=F32)
    br, bi = b[:n2], b[n2:]
    o_ref[0, 0, 0] = br * twr + bi * twi
    o_ref[0, 1, 0] = bi * twr - br * twi


def _spec_mul(a, hf, dc):
    B = a.shape[0]
    N1, N2 = dc["N1"], dc["N2"]
    C = a.shape[-1]
    blk = pl.BlockSpec((1, 2, 1, N2, C), lambda b, k: (b, 0, k, 0, 0))
    tw = pl.BlockSpec((1, N2, LANES), lambda b, k: (k, 0, 0))
    return pl.pallas_call(
        _spec_mul_body,
        grid=(B, N1),
        in_specs=[blk, tw, tw, _full((2 * N2, 2 * N2)), _full((2 * N2, 2 * N2)),
                  pl.BlockSpec((1, 2, N2, C), lambda b, k: (k, 0, 0, 0))],
        out_specs=blk,
        out_shape=jax.ShapeDtypeStruct(a.shape, F32),
        compiler_params=_cp("parallel", "parallel"),
        name="hy_spec_mul",
    )(a, dc["twr"], dc["twi"], dc["g"].astype(BF16), dc["gc"].astype(BF16), hf)


def _dft3_body(f_ref, b_ref, u_ref, gate_ref, skip_ref, o_ref):
    y = jnp.dot(f_ref[...], b_ref[0].astype(BF16), preferred_element_type=F32)
    o_ref[0] = gate_ref[0] * (y + u_ref[0] * skip_ref[...])


def _dft_stage3(fb, bm, u, gate, skip_t):
    B, R, M = bm.shape
    half = fb.shape[0]
    tn = skip_t.shape[1]
    row = pl.BlockSpec((1, half, tn), lambda b, j: (b, 0, j))
    return pl.pallas_call(
        _dft3_body,
        grid=(B, M // tn),
        in_specs=[_full((half, R)), pl.BlockSpec((1, R, tn), lambda b, j: (b, 0, j)), row, row, _full((1, tn))],
        out_specs=row,
        out_shape=jax.ShapeDtypeStruct((B, half, M), F32),
        compiler_params=_cp("parallel", "parallel"),
        name="hy_dft3",
    )(fb.astype(BF16), bm, u, gate, skip_t)


def _longconv_gated(u, gate, hf, skip, dc):
    B, L, C = u.shape
    N1, N2, half = dc["N1"], dc["N2"], dc["half"]
    M = N2 * C
    uv = u.reshape(B, half, M)
    a = _dft_stage1(dc["fa"], uv).reshape(B, 2, N1, N2, C)
    bm = _spec_mul(a, hf, dc).reshape(B, 2 * N1, M)
    tn = min(4096, M)
    skip_t = jnp.tile(skip, tn // C)[None, :]
    return _dft_stage3(dc["fb"], bm, uv, gate.reshape(B, half, M), skip_t).reshape(B, L, C)


def _hyena(hyu, conv_w, conv_b, filt, skip):
    B, L, _ = hyu.shape
    v, x1, x2 = _shortconv(hyu, conv_w, conv_b)
    h, ss = _filters(L, *filt)
    dc = _dft_consts(L)
    hf = _filter_spectrum(h, ss, dc)
    z1 = _longconv_gated(v, x1, hf[0], skip[0], dc)
    return _longconv_gated(z1, x2, hf[1], skip[1], dc)


def _hyena_ctx_body(v_ref, x1_ref, x2_ref, h_ref, ss_ref, skip_ref, fc_ref, gc_ref, o_ref):
    fc, gc = fc_ref[...], gc_ref[...]
    n = fc.shape[0] // 2
    ss = ss_ref[...]
    h = h_ref[...]

    def conv(u, o):
        f0, b0 = (2 * o) * HY_CH, (2 * o + 1) * HY_CH
        sc = lax.rsqrt(ss[:, f0:f0 + HY_CH] + ss[:, b0:b0 + HY_CH] + 1e-6)
        x = jnp.dot(fc, u, precision=HI, preferred_element_type=F32)
        hf = jnp.dot(fc, h[:, f0:f0 + HY_CH], precision=HI, preferred_element_type=F32)
        hb = jnp.dot(fc, h[:, b0:b0 + HY_CH], precision=HI, preferred_element_type=F32)
        hr = (hf[:n] + hb[:n]) * sc
        hi = (hf[n:] - hb[n:]) * sc
        yr = x[:n] * hr - x[n:] * hi
        yi = x[:n] * hi + x[n:] * hr
        y = jnp.dot(gc, jnp.concatenate([yr, yi], axis=0), precision=HI, preferred_element_type=F32)
        return y + u * skip_ref[o:o + 1, :]

    z1 = x1_ref[0] * conv(v_ref[0], 0)
    o_ref[0] = x2_ref[0] * conv(z1, 1)


def _hyena_ctx(hyu, conv_w, conv_b, filt, skip):
    B, L, _ = hyu.shape
    v, x1, x2 = _shortconv(hyu, conv_w, conv_b)
    h, ss = _filters(L, *filt)
    N = 2 * L
    ang = 2.0 * np.pi * (np.arange(N)[:, None] * np.arange(L)[None, :] % N) / N
    fr, fi = np.cos(ang), -np.sin(ang)
    fc = jnp.asarray(np.concatenate([fr, fi], axis=0), dtype=F32)
    gc = jnp.asarray(np.concatenate([fr.T, fi.T], axis=1) / N, dtype=F32)
    row = pl.BlockSpec((1, L, HY_CH), lambda b: (b, 0, 0))
    return pl.pallas_call(
        _hyena_ctx_body,
        grid=(B,),
        in_specs=[row, row, row, _full(h.shape), _full(ss.shape), _full(skip.shape), _full(fc.shape), _full(gc.shape)],
        out_specs=row,
        out_shape=jax.ShapeDtypeStruct((B, L, HY_CH), F32),
        compiler_params=_cp("parallel"),
        name="hyena_ctx",
    )(v, x1, x2, h, ss, skip, fc, gc)


HEAD_PAD = 128


def _rope_swap(w):
    a, b, c, d = w[..., 0:8], w[..., 8:16], w[..., 16:24], w[..., 24:32]
    return jnp.concatenate([-b, a, -d, c], axis=-1)


def _arrange_wq(w_uq):
    R = w_uq.shape[0]
    w = w_uq.reshape(R, MLA_HEADS, MLA_NOPE + MLA_ROPE)
    rope = w[..., MLA_NOPE:]
    out = jnp.concatenate([w[..., :MLA_NOPE], rope, _rope_swap(rope)], axis=-1)
    return out.reshape(R, MLA_HEADS * HEAD_PAD).astype(BF16)


def _arrange_wkv(w_ukv):
    R = w_ukv.shape[0]
    w = w_ukv.reshape(R, MLA_HEADS, MLA_NOPE + MLA_V)
    wk = jnp.concatenate([w[..., :MLA_NOPE], jnp.zeros((R, MLA_HEADS, HEAD_PAD - MLA_NOPE), w.dtype)], axis=-1)
    wv = w[..., MLA_NOPE:]
    return wk.reshape(R, MLA_HEADS * HEAD_PAD).astype(BF16), wv.reshape(R, MLA_HEADS * MLA_V).astype(BF16)


def _kr_place():
    e = np.zeros((LANES, MLA_HEADS * HEAD_PAD), np.float32)
    es = np.zeros((LANES, MLA_HEADS * HEAD_PAD), np.float32)
    for h in range(MLA_HEADS):
        base = h * HEAD_PAD + MLA_NOPE
        for j in range(MLA_ROPE):
            e[j, base + j] = 1.0
            blk, r = divmod(j, 16)
            if r < 8:
                es[16 * blk + r + 8, base + j] = -1.0
            else:
                es[16 * blk + r - 8, base + j] = 1.0
    return jnp.asarray(e).astype(BF16), jnp.asarray(es).astype(BF16)


def _rope_tables(L, rope):
    if rope:
        t = np.arange(L)
        row, col = (t // GRID_W).astype(np.float32), (t % GRID_W).astype(np.float32)
        half = MLA_ROPE // 2
        inv = ROPE_BASE ** (-jnp.arange(0, half, 2, dtype=F32) / half)
        ar = jnp.asarray(row)[:, None] * inv
        ac = jnp.asarray(col)[:, None] * inv
        cos = jnp.concatenate([jnp.cos(ar), jnp.cos(ar), jnp.cos(ac), jnp.cos(ac)], axis=-1)
        sin = jnp.concatenate([jnp.sin(ar), jnp.sin(ar), jnp.sin(ac), jnp.sin(ac)], axis=-1)
    else:
        cos, sin = jnp.ones((L, MLA_ROPE), F32), jnp.zeros((L, MLA_ROPE), F32)
    return cos, sin


def _rms_rows(x, g, eps=1e-6):
    return x * lax.rsqrt(jnp.mean(x * x, axis=-1, keepdims=True) + eps) * g


def _qproj_body(cq_ref, g_ref, w_ref, t1_ref, t2_ref, q_ref):
    xn = _rms_rows(cq_ref[0], g_ref[...])
    acc = jnp.dot(xn.astype(BF16), w_ref[...], preferred_element_type=F32)
    W = acc.shape[1]
    t1, t2 = _lanes(t1_ref[...], W), _lanes(t2_ref[...], W)
    q_ref[0] = (acc * t1 + pltpu.roll(acc, W - MLA_ROPE, 1) * t2).astype(q_ref.dtype)


def _qproj(cq, g, wq, cos, sin):
    B, L, R = cq.shape
    tm = min(512, L)
    W = wq.shape[1]
    ones, zeros = jnp.ones((L, MLA_NOPE), F32), jnp.zeros((L, MLA_ROPE), F32)
    t1 = jnp.concatenate([ones, cos, zeros], axis=-1) * MLA_SCALE
    t2 = jnp.concatenate([jnp.zeros((L, MLA_NOPE), F32), sin, zeros], axis=-1) * MLA_SCALE
    tab = pl.BlockSpec((tm, HEAD_PAD), lambda b, i: (i, 0))
    return pl.pallas_call(
        _qproj_body,
        grid=(B, L // tm),
        in_specs=[pl.BlockSpec((1, tm, R), lambda b, i: (b, i, 0)), _full((1, R)), _full((R, W)), tab, tab],
        out_specs=pl.BlockSpec((1, tm, W), lambda b, i: (b, i, 0)),
        out_shape=jax.ShapeDtypeStruct((B, L, W), BF16),
        compiler_params=_cp("parallel", "parallel"),
        name="mla_qproj",
    )(cq, g[None, :], wq, t1, t2)


def _kvproj_body(c_ref, g_ref, wk_ref, wv_ref, e_ref, es_ref, cos_ref, sin_ref, k_ref, v_ref):
    c = c_ref[0]
    R = MLA_KV_RANK
    xn = _rms_rows(c[:, :R], g_ref[...]).astype(BF16)
    kr = c[:, R:]
    acc = jnp.dot(xn, wk_ref[...], preferred_element_type=F32)
    acc += jnp.dot((kr * cos_ref[...]).astype(BF16), e_ref[...], preferred_element_type=F32)
    acc += jnp.dot((kr * sin_ref[...]).astype(BF16), es_ref[...], preferred_element_type=F32)
    k_ref[0] = acc.astype(k_ref.dtype)
    v_ref[0] = jnp.dot(xn, wv_ref[...], preferred_element_type=F32).astype(v_ref.dtype)


def _kvproj(ckvr, g, wk, wv, cos, sin):
    B, L, Wc = ckvr.shape
    tm = min(512, L)
    pad = jnp.zeros((L, LANES - MLA_ROPE), F32)
    cos_p, sin_p = jnp.concatenate([cos, pad], axis=-1), jnp.concatenate([sin, pad], axis=-1)
    e, es = _kr_place()
    tab = pl.BlockSpec((tm, LANES), lambda b, i: (i, 0))
    Wk, Wv = wk.shape[1], wv.shape[1]
    return pl.pallas_call(
        _kvproj_body,
        grid=(B, L // tm),
        in_specs=[pl.BlockSpec((1, tm, Wc), lambda b, i: (b, i, 0)), _full((1, MLA_KV_RANK)),
                  _full(wk.shape), _full(wv.shape), _full(e.shape), _full(es.shape), tab, tab],
        out_specs=[pl.BlockSpec((1, tm, Wk), lambda b, i: (b, i, 0)), pl.BlockSpec((1, tm, Wv), lambda b, i: (b, i, 0))],
        out_shape=[jax.ShapeDtypeStruct((B, L, Wk), BF16), jax.ShapeDtypeStruct((B, L, Wv), BF16)],
        compiler_params=_cp("parallel", "parallel"),
        name="mla_kvproj",
    )(ckvr, g[None, :], wk, wv, e, es, cos_p, sin_p)


def _flash_body(q_ref, k_ref, v_ref, o_ref, m_ref, l_ref, acc_ref):
    j = pl.program_id(3)

    @pl.when(j == 0)
    def _():
        m_ref[...] = jnp.full_like(m_ref, -jnp.inf)
        l_ref[...] = jnp.zeros_like(l_ref)
        acc_ref[...] = jnp.zeros_like(acc_ref)

    v = v_ref[0]
    for a in range(2):
        q = q_ref[0][:, a * HEAD_PAD:(a + 1) * HEAD_PAD]
        k = k_ref[0][:, a * HEAD_PAD:(a + 1) * HEAD_PAD]
        s = lax.dot_general(q, k, (((1,), (1,)), ((), ())), preferred_element_type=F32)
        m_old = m_ref[a]
        m_new = jnp.maximum(m_old, jnp.max(s, axis=1, keepdims=True))
        p = jnp.exp(s - m_new)
        alpha = jnp.exp(m_old - m_new)
        l_ref[a] = alpha * l_ref[a] + jnp.sum(p, axis=1, keepdims=True)
        acc_ref[a] = alpha * acc_ref[a] + jnp.dot(p.astype(BF16), v, preferred_element_type=F32)
        m_ref[a] = m_new

    @pl.when(j == pl.num_programs(3) - 1)
    def _():
        lane = lax.broadcasted_iota(jnp.int32, acc_ref.shape[1:], 1)
        o_ref[0] = jnp.where(lane < MLA_V, acc_ref[0] / l_ref[0], acc_ref[1] / l_ref[1])


def _flash_tiles(Lq, Lk):
    tq = min(1024, Lq)
    tk = Lk
    for cand in (1280, 1024, 512, 256):
        if Lk % cand == 0:
            tk = cand
            break
    return tq, tk


def _flash(q, k, v):
    B, Lq, _ = q.shape
    Lk = k.shape[1]
    tq, tk = _flash_tiles(Lq, Lk)
    hp = MLA_HEADS // 2
    return pl.pallas_call(
        _flash_body,
        grid=(B, hp, Lq // tq, Lk // tk),
        in_specs=[pl.BlockSpec((1, tq, 2 * HEAD_PAD), lambda b, h, i, j: (b, i, h)),
                  pl.BlockSpec((1, tk, 2 * HEAD_PAD), lambda b, h, i, j: (b, j, h)),
                  pl.BlockSpec((1, tk, 2 * MLA_V), lambda b, h, i, j: (b, j, h))],
        out_specs=pl.BlockSpec((1, tq, 2 * MLA_V), lambda b, h, i, j: (b, i, h)),
        out_shape=jax.ShapeDtypeStruct((B, Lq, MLA_HEADS * MLA_V), F32),
        scratch_shapes=[pltpu.VMEM((2, tq, 1), F32), pltpu.VMEM((2, tq, 1), F32), pltpu.VMEM((2, tq, 2 * MLA_V), F32)],
        compiler_params=_cp("parallel", "parallel", "parallel", "arbitrary"),
        name="mla_flash",
    )(q, k, v)


def _layernorm_rows(x, g, b, eps=1e-5):
    mu = jnp.mean(x, axis=-1, keepdims=True)
    xc = x - mu
    var = jnp.mean(xc * xc, axis=-1, keepdims=True)
    return xc * lax.rsqrt(var + eps) * g + b


def _outproj_body(of_ref, ob_ref, g_ref, hy_ref, om_ref, x_ref, gate_ref, gg_ref, hg_ref, mg_ref,
                  w_ref, lg_ref, lb_ref, o_ref, *, alpha):
    VD = GLA_HEADS * GLA_DV
    o = of_ref[0] + ob_ref[0]
    r = _idiv(lax.broadcasted_iota(jnp.int32, (VD, VD), 0), GLA_DV)
    c = _idiv(lax.broadcasted_iota(jnp.int32, (VD, VD), 1), GLA_DV)
    grp = (r == c).astype(F32)
    ms = jnp.dot(o * o, grp, precision=HI, preferred_element_type=F32) * (1.0 / GLA_DV)
    g = g_ref[0]
    ya = o * lax.rsqrt(ms + 1e-6) * gg_ref[...] * (g * jax.nn.sigmoid(g))
    yb = _rms_rows(hy_ref[0], hg_ref[...])
    yc = _rms_rows(om_ref[0], mg_ref[...])
    acc = jnp.dot(ya.astype(BF16), w_ref[0:VD, :], preferred_element_type=F32)
    acc += jnp.dot(yb.astype(BF16), w_ref[VD:VD + HY_CH, :], preferred_element_type=F32)
    acc += jnp.dot(yc.astype(BF16), w_ref[VD + HY_CH:, :], preferred_element_type=F32)
    o_ref[0] = _layernorm_rows(alpha * x_ref[0] + gate_ref[0] * acc, lg_ref[...], lb_ref[...])


def _outproj(of, ob, vg, hy, om, x, gate, gla_g, hy_g, mla_g, w_out, ln_g, ln_b, alpha):
    B, L, D = x.shape
    tm = min(512, L)
    VD = GLA_HEADS * GLA_DV
    MD = MLA_HEADS * MLA_V
    row = lambda w: pl.BlockSpec((1, tm, w), lambda b, i: (b, i, 0))
    return pl.pallas_call(
        functools.partial(_outproj_body, alpha=alpha),
        grid=(B, L // tm),
        in_specs=[row(VD), row(VD), pl.BlockSpec((1, tm, VD), lambda b, i: (b, i, 1)), row(HY_CH), row(MD), row(D),
                  pl.BlockSpec((1, 1, D), lambda b, i: (b, 0, 0)), _full((1, VD)), _full((1, HY_CH)), _full((1, MD)),
                  _full(w_out.shape), _full((1, D)), _full((1, D))],
        out_specs=row(D),
        out_shape=jax.ShapeDtypeStruct((B, L, D), F32),
        compiler_params=_cp("parallel", "parallel"),
        name="outproj",
    )(of, ob, vg, hy, om, x, gate, jnp.tile(gla_g, GLA_HEADS)[None, :], hy_g[None, :], mla_g[None, :],
      w_out.astype(BF16), ln_g[None, :], ln_b[None, :])


def _ffn_body(x_ref, sh_ref, sc_ref, gate_ref, w1_ref, w3_ref, w2_ref, lg_ref, lb_ref, o_ref, h_ref, acc_ref, *, alpha):
    j = pl.program_id(2)

    @pl.when(j == 0)
    def _():
        h_ref[...] = (x_ref[0] * (1.0 + sc_ref[0]) + sh_ref[0]).astype(BF16)
        acc_ref[...] = jnp.zeros_like(acc_ref)

    h = h_ref[...]
    a = jnp.dot(h, w1_ref[...], preferred_element_type=F32)
    b = jnp.dot(h, w3_ref[...], preferred_element_type=F32)
    t = (a * jax.nn.sigmoid(a) * b).astype(BF16)
    acc_ref[...] += jnp.dot(t, w2_ref[...], preferred_element_type=F32)

    @pl.when(j == pl.num_programs(2) - 1)
    def _():
        o_ref[0] = _layernorm_rows(alpha * x_ref[0] + gate_ref[0] * acc_ref[...], lg_ref[...], lb_ref[...])


def _ffn_tile(F):
    for cand in (512, 256, 128):
        if F % cand == 0:
            return cand
    return F


def _ffn(x, shift, scale, gate, w1, w3, w2, ln_g, ln_b, alpha):
    B, L, D = x.shape
    F = w1.shape[1]
    tm = min(1024, L)
    tf = _ffn_tile(F)
    row = pl.BlockSpec((1, tm, D), lambda b, i, j: (b, i, 0))
    vec = pl.BlockSpec((1, 1, D), lambda b, i, j: (b, 0, 0))
    return pl.pallas_call(
        functools.partial(_ffn_body, alpha=alpha),
        grid=(B, L // tm, F // tf),
        in_specs=[row, vec, vec, vec,
                  pl.BlockSpec((D, tf), lambda b, i, j: (0, j)), pl.BlockSpec((D, tf), lambda b, i, j: (0, j)),
                  pl.BlockSpec((tf, D), lambda b, i, j: (j, 0)), _full((1, D)), _full((1, D))],
        out_specs=row,
        out_shape=jax.ShapeDtypeStruct((B, L, D), F32),
        scratch_shapes=[pltpu.VMEM((tm, D), BF16), pltpu.VMEM((tm, D), F32)],
        compiler_params=_cp("parallel", "parallel", "arbitrary"),
        name="ffn",
    )(x, shift, scale, gate, w1.astype(BF16), w3.astype(BF16), w2.astype(BF16), ln_g[None, :], ln_b[None, :])


def _router_body(x_ref, sh_ref, sc_ref, wr_ref, g_ref):
    h = x_ref[0] * (1.0 + sc_ref[0]) + sh_ref[0]
    logits = jnp.dot(h, wr_ref[...], precision=HI, preferred_element_type=F32)
    lane = lax.broadcasted_iota(jnp.int32, logits.shape, 1).astype(F32)
    logits = jnp.where(lane < N_EXPERTS, logits, -jnp.inf)
    m1 = jnp.max(logits, axis=1, keepdims=True)
    i1 = jnp.min(jnp.where(logits == m1, lane, float(LANES)), axis=1, keepdims=True)
    rest = jnp.where(lane == i1, -jnp.inf, logits)
    m2 = jnp.max(rest, axis=1, keepdims=True)
    i2 = jnp.min(jnp.where(rest == m2, lane, float(LANES)), axis=1, keepdims=True)
    e2 = jnp.exp(m2 - m1)
    w1 = 1.0 / (1.0 + e2)
    w2 = e2 / (1.0 + e2)
    g_ref[0] = jnp.where(lane == i1, w1, 0.0) + jnp.where(lane == i2, w2, 0.0)


def _router(x, shift, scale, w_router):
    B, L, D = x.shape
    tm = min(512, L)
    wr = jnp.pad(w_router, ((0, 0), (0, LANES - N_EXPERTS)))
    vec = pl.BlockSpec((1, 1, D), lambda b, i: (b, 0, 0))
    return pl.pallas_call(
        _router_body,
        grid=(B, L // tm),
        in_specs=[pl.BlockSpec((1, tm, D), lambda b, i: (b, i, 0)), vec, vec, _full((D, LANES))],
        out_specs=pl.BlockSpec((1, tm, LANES), lambda b, i: (b, i, 0)),
        out_shape=jax.ShapeDtypeStruct((B, L, LANES), F32),
        compiler_params=_cp("parallel", "parallel"),
        name="moe_router",
    )(x, shift, scale, wr)


def _moe_body(x_ref, sh_ref, sc_ref, gate_ref, gts_ref, w1_ref, w3_ref, w2_ref, lg_ref, lb_ref, o_ref,
              h_ref, acc_ref, *, alpha):
    e = pl.program_id(2)
    j = pl.program_id(3)

    @pl.when(jnp.logical_and(e == 0, j == 0))
    def _():
        h_ref[...] = (x_ref[0] * (1.0 + sc_ref[0]) + sh_ref[0]).astype(BF16)
        acc_ref[...] = jnp.zeros_like(acc_ref)

    gts = gts_ref[0]
    lane = lax.broadcasted_iota(jnp.int32, gts.shape, 1)
    ge = jnp.sum(jnp.where(lane == e, gts, 0.0), axis=1, keepdims=True)
    h = h_ref[...]
    a = jnp.dot(h, w1_ref[0], preferred_element_type=F32)
    b = jnp.dot(h, w3_ref[0], preferred_element_type=F32)
    t = (a * jax.nn.sigmoid(a) * b * ge).astype(BF16)
    acc_ref[...] += jnp.dot(t, w2_ref[0], preferred_element_type=F32)

    @pl.when(jnp.logical_and(e == pl.num_programs(2) - 1, j == pl.num_programs(3) - 1))
    def _():
        o_ref[0] = _layernorm_rows(alpha * x_ref[0] + gate_ref[0] * acc_ref[...], lg_ref[...], lb_ref[...])


def _moe(x, shift, scale, gate, w_router, w1, w3, w2, ln_g, ln_b, alpha):
    B, L, D = x.shape
    E, _, F = w1.shape
    gts = _router(x, shift, scale, w_router)
    tm = min(1024, L)
    tf = _ffn_tile(F)
    row = pl.BlockSpec((1, tm, D), lambda b, i, e, j: (b, i, 0))
    vec = pl.BlockSpec((1, 1, D), lambda b, i, e, j: (b, 0, 0))
    return pl.pallas_call(
        functools.partial(_moe_body, alpha=alpha),
        grid=(B, L // tm, E, F // tf),
        in_specs=[row, vec, vec, vec, pl.BlockSpec((1, tm, LANES), lambda b, i, e, j: (b, i, 0)),
                  pl.BlockSpec((1, D, tf), lambda b, i, e, j: (e, 0, j)), pl.BlockSpec((1, D, tf), lambda b, i, e, j: (e, 0, j)),
                  pl.BlockSpec((1, tf, D), lambda b, i, e, j: (e, j, 0)), _full((1, D)), _full((1, D))],
        out_specs=row,
        out_shape=jax.ShapeDtypeStruct((B, L, D), F32),
        scratch_shapes=[pltpu.VMEM((tm, D), BF16), pltpu.VMEM((tm, D), F32)],
        compiler_params=_cp("parallel", "parallel", "arbitrary", "arbitrary"),
        name="moe",
    )(x, shift, scale, gate, gts, w1.astype(BF16), w3.astype(BF16), w2.astype(BF16), ln_g[None, :], ln_b[None, :])


def _mod_body(c_ref, w_ref, b_ref, o_ref):
    c = c_ref[...]
    s = c * jax.nn.sigmoid(c)
    o_ref[...] = jnp.dot(s, w_ref[...], precision=HI, preferred_element_type=F32) + b_ref[...]


def _modulation(cc, w_mod, b_mod):
    R, D = cc.shape
    N = w_mod.shape[1]
    tn = 1024
    return pl.pallas_call(
        _mod_body,
        grid=(N // tn,),
        in_specs=[_full((R, D)), pl.BlockSpec((D, tn), lambda j: (0, j)), pl.BlockSpec((1, tn), lambda j: (0, j))],
        out_specs=pl.BlockSpec((R, tn), lambda j: (0, j)),
        out_shape=jax.ShapeDtypeStruct((R, N), F32),
        compiler_params=_cp("parallel"),
        name="modulation",
    )(cc, w_mod, b_mod[None, :])


def kernel(x, c, ctx, c_ctx, w_mod, b_mod, w_in, gla_w_gate, gla_b_gate, gla_norm_g, hy_conv_w, hy_conv_b, hy_f_w1, hy_f_b1, hy_f_freq1, hy_f_w2, hy_f_b2, hy_f_freq2, hy_f_w3, hy_f_b3, hy_skip, hy_norm_g, mla_q_norm_g, mla_w_uq, mla_kv_norm_g, mla_w_ukv, mla_norm_g, w_out, ln_g, ln_b, ffn_w1, ffn_w3, ffn_w2, moe_router, moe_w1, moe_w3, moe_w2):
    B, L, D = x.shape
    Lc = ctx.shape[1]
    depth = w_mod.shape[0]
    alpha = (2.0 * depth) ** 0.25
    cc = jnp.zeros((8, D), F32).at[:B].set(c).at[B].set(c_ctx)
    cos, sin = _rope_tables(L, True)
    cos_c, sin_c = _rope_tables(Lc, False)
    KD, VD = GLA_HEADS * GLA_DK, GLA_HEADS * GLA_DV
    xc = ctx
    for l in range(depth):
        need_ctx = l < depth - 1
        mods = _modulation(cc, w_mod[l], b_mod[l])
        m = [mods[:B, k * D:(k + 1) * D][:, None, :] for k in range(6)]
        mc = [jnp.broadcast_to(mods[B, k * D:(k + 1) * D][None, None, :], (B, 1, D)) for k in range(6)]
        w_arr = _arrange_w_in(w_in[l])
        wg, bg = _arrange_gate(gla_w_gate[l], gla_b_gate[l])
        filt = (hy_f_w1[l], hy_f_b1[l], hy_f_freq1[l], hy_f_w2[l], hy_f_b2[l], hy_f_freq2[l], hy_f_w3[l], hy_f_b3[l])
        wq = _arrange_wq(mla_w_uq[l])
        wk, wv = _arrange_wkv(mla_w_ukv[l])

        hyu, qk, vg, alr, cq, ckvr = _inproj(x, m[0], m[1], w_arr)
        hyu_c, qk_c, vg_c, alr_c, cq_c, ckvr_c = _inproj(xc, mc[0], mc[1], w_arr)

        of_c, ob_c, s_c = _gla(qk_c, vg_c, alr_c, wg, bg, jnp.zeros((B, 2, KD, VD), F32))
        of, ob, _ = _gla(qk, vg, alr, wg, bg, s_c)
        hy = _hyena(hyu, hy_conv_w[l], hy_conv_b[l], filt, hy_skip[l])
        k_c, v_c = _kvproj(ckvr_c, mla_kv_norm_g[l], wk, wv, cos_c, sin_c)
        k_m, v_m = _kvproj(ckvr, mla_kv_norm_g[l], wk, wv, cos, sin)
        q_m = _qproj(cq, mla_q_norm_g[l], wq, cos, sin)
        om = _flash(q_m, jnp.concatenate([k_c, k_m], axis=1), jnp.concatenate([v_c, v_m], axis=1))

        x = _outproj(of, ob, vg, hy, om, x, m[2], gla_norm_g[l], hy_norm_g[l], mla_norm_g[l], w_out[l],
                     ln_g[l, 0], ln_b[l, 0], alpha)
        if need_ctx:
            hy_c = _hyena_ctx(hyu_c, hy_conv_w[l], hy_conv_b[l], filt, hy_skip[l])
            q_c = _qproj(cq_c, mla_q_norm_g[l], wq, cos_c, sin_c)
            om_c = _flash(q_c, k_c, v_c)
            xc = _outproj(of_c, ob_c, vg_c, hy_c, om_c, xc, mc[2], gla_norm_g[l], hy_norm_g[l], mla_norm_g[l],
                          w_out[l], ln_g[l, 0], ln_b[l, 0], alpha)

        i = l // 2
        if l % 2 == 0:
            x = _ffn(x, m[3], m[4], m[5], ffn_w1[i], ffn_w3[i], ffn_w2[i], ln_g[l, 1], ln_b[l, 1], alpha)
            if need_ctx:
                xc = _ffn(xc, mc[3], mc[4], mc[5], ffn_w1[i], ffn_w3[i], ffn_w2[i], ln_g[l, 1], ln_b[l, 1], alpha)
        else:
            x = _moe(x, m[3], m[4], m[5], moe_router[i], moe_w1[i], moe_w3[i], moe_w2[i], ln_g[l, 1], ln_b[l, 1], alpha)
            if need_ctx:
                xc = _moe(xc, mc[3], mc[4], mc[5], moe_router[i], moe_w1[i], moe_w3[i], moe_w2[i], ln_g[l, 1],
                          ln_b[l, 1], alpha)
    return x
```

```python
import functools
import math

import numpy as np
import jax
import jax.numpy as jnp
from jax import lax
from jax.experimental import pallas as pl
from jax.experimental.pallas import tpu as pltpu

F32 = jnp.float32
BF16 = jnp.bfloat16
HI = lax.Precision.HIGHEST

GRID_W = 64
GLA_HEADS, GLA_DK, GLA_DV, GLA_RANK, GLA_TAU = 4, 32, 64, 16, 16.0
HY_CH, HY_EMB = 256, 33
HY_DECAY_TARGET, HY_FAST_DECAY, HY_SLOW_DECAY = 1e-2, 0.3, 1.5
MLA_HEADS, MLA_Q_RANK, MLA_KV_RANK, MLA_NOPE, MLA_ROPE, MLA_V = 8, 256, 128, 64, 32, 64
MLA_SCALE = (MLA_NOPE + MLA_ROPE) ** -0.5
ROPE_BASE = 10000.0
N_EXPERTS = 8
IN_SPLITS = (128, 128, 256, 256, 32, 768, 256, 128, 32)

LANES = 128
VMEM_LIMIT = 56 * 1024 * 1024

GLA_CHUNK = 128
DFT_N2 = 256


def _cp(*sem):
    return pltpu.CompilerParams(dimension_semantics=sem, vmem_limit_bytes=VMEM_LIMIT)


def _full(shape):
    n = len(shape)
    return pl.BlockSpec(shape, lambda *_: (0,) * n)


def _idiv(x, d):
    assert d & (d - 1) == 0
    return lax.shift_right_logical(x, int(math.log2(d)))


INPROJ_WIDTHS = (768, 256, 512, 128, 256, 256)


def _arrange_w_in(w):
    cuts = np.cumsum(IN_SPLITS)[:-1]
    qa, ka, va, ga, alr, hyu, cq, ckv, kr = jnp.split(w, [int(c) for c in cuts], axis=1)
    z96 = jnp.zeros((w.shape[0], 96), w.dtype)
    return jnp.concatenate([hyu, qa, ka, va, ga, alr, z96, cq, ckv, kr, z96], axis=1).astype(BF16)


def _inproj_body(x_ref, sh_ref, sc_ref, w_ref, *out_refs):
    h = x_ref[0] * (1.0 + sc_ref[0]) + sh_ref[0]
    acc = jnp.dot(h.astype(BF16), w_ref[...], preferred_element_type=F32)
    off = 0
    for r in out_refs:
        w = r.shape[-1]
        r[0] = acc[:, off:off + w]
        off += w


def _inproj(x, shift, scale, w_arr):
    B, L, D = x.shape
    tm = min(512, L)
    n = w_arr.shape[1]
    row = lambda w: pl.BlockSpec((1, tm, w), lambda b, i: (b, i, 0))
    vec = pl.BlockSpec((1, 1, D), lambda b, i: (b, 0, 0))
    return pl.pallas_call(
        _inproj_body,
        grid=(B, L // tm),
        in_specs=[row(D), vec, vec, _full((D, n))],
        out_specs=[row(w) for w in INPROJ_WIDTHS],
        out_shape=[jax.ShapeDtypeStruct((B, L, w), F32) for w in INPROJ_WIDTHS],
        compiler_params=_cp("parallel", "parallel"),
        name="inproj",
    )(x, shift, scale, w_arr)


def _log_sigmoid(z):
    return jnp.minimum(z, 0.0) - jnp.log1p(jnp.exp(-jnp.abs(z)))


def _gla_dir(qk, v, la, s_ref, d, tri):
    C = qk.shape[0]
    KD = GLA_HEADS * GLA_DK
    VD = GLA_HEADS * GLA_DV
    q = qk[:, :KD] * (GLA_DK ** -0.5)
    k = qk[:, KD:]
    b = jnp.dot(tri, la, precision=HI, preferred_element_type=F32)
    tot = jnp.sum(la, axis=0, keepdims=True)
    qe = q * jnp.exp(b)
    ke = (k * jnp.exp(-b)).astype(BF16)
    kl = (k * jnp.exp(tot - b)).astype(BF16)
    s_old = s_ref[d]
    vb = v.astype(BF16)
    o = jnp.dot(qe.astype(BF16), s_old.astype(BF16), preferred_element_type=F32)
    lane_k = _idiv(lax.broadcasted_iota(jnp.int32, (1, KD), 1), GLA_DK)
    lane_v = _idiv(lax.broadcasted_iota(jnp.int32, (1, VD), 1), GLA_DV)
    for h in range(GLA_HEADS):
        qh = jnp.where(lane_k == h, qe, 0.0).astype(BF16)
        att = lax.dot_general(qh, ke, (((1,), (1,)), ((), ())), preferred_element_type=F32)
        att = (att * tri).astype(BF16)
        oh = jnp.dot(att, vb, preferred_element_type=F32)
        o = o + jnp.where(lane_v == h, oh, 0.0)
    ones = jnp.ones((C, VD), F32)
    tot_b = lax.dot_general(la, ones, (((0,), (0,)), ((), ())), precision=HI, preferred_element_type=F32)
    kv = lax.dot_general(kl, vb, (((0,), (0,)), ((), ())), preferred_element_type=F32)
    rk = _idiv(lax.broadcasted_iota(jnp.int32, (KD, VD), 0), GLA_DK)
    cv = _idiv(lax.broadcasted_iota(jnp.int32, (KD, VD), 1), GLA_DV)
    s_ref[d] = jnp.exp(tot_b) * s_old + jnp.where(rk == cv, kv, 0.0)
    return o


def _gla_body(qkf_ref, vf_ref, af_ref, qkb_ref, vb_ref, ab_ref, wg_ref, bg_ref, s0_ref,
              of_ref, ob_ref, sout_ref, s_ref):
    i = pl.program_id(1)
    C = qkf_ref.shape[1]
    KD = GLA_HEADS * GLA_DK

    @pl.when(i == 0)
    def _():
        s_ref[...] = s0_ref[0]

    r = lax.broadcasted_iota(jnp.int32, (C, C), 0)
    c = lax.broadcasted_iota(jnp.int32, (C, C), 1)
    tri_lo = (c <= r).astype(F32)
    tri_up = (c >= r).astype(F32)
    zf = jnp.dot(af_ref[0], wg_ref[...], precision=HI, preferred_element_type=F32) + bg_ref[...]
    zb = jnp.dot(ab_ref[0], wg_ref[...], precision=HI, preferred_element_type=F32) + bg_ref[...]
    la_f = _log_sigmoid(zf[:, :KD]) / GLA_TAU
    la_b = _log_sigmoid(zb[:, KD:]) / GLA_TAU
    of_ref[0] = _gla_dir(qkf_ref[0], vf_ref[0], la_f, s_ref, 0, tri_lo)
    ob_ref[0] = _gla_dir(qkb_ref[0], vb_ref[0], la_b, s_ref, 1, tri_up)

    @pl.when(i == pl.num_programs(1) - 1)
    def _():
        sout_ref[0] = s_ref[...]


def _gla(qk, vg, alr, wg, bg, s0):
    B, L, _ = qk.shape
    C = min(GLA_CHUNK, L)
    n = L // C
    KD, VD = GLA_HEADS * GLA_DK, GLA_HEADS * GLA_DV
    fwd = lambda w: pl.BlockSpec((1, C, w), lambda b, i: (b, i, 0))
    bwd = lambda w: pl.BlockSpec((1, C, w), lambda b, i: (b, n - 1 - i, 0))
    st = pl.BlockSpec((1, 2, KD, VD), lambda b, i: (b, 0, 0, 0))
    return pl.pallas_call(
        _gla_body,
        grid=(B, n),
        in_specs=[fwd(2 * KD), fwd(VD), fwd(LANES), bwd(2 * KD), bwd(VD), bwd(LANES),
                  _full((LANES, 2 * KD)), _full((1, 2 * KD)), st],
        out_specs=[fwd(VD), bwd(VD), st],
        out_shape=[jax.ShapeDtypeStruct((B, L, VD), F32), jax.ShapeDtypeStruct((B, L, VD), F32),
                   jax.ShapeDtypeStruct((B, 2, KD, VD), F32)],
        scratch_shapes=[pltpu.VMEM((2, KD, VD), F32)],
        compiler_params=_cp("parallel", "arbitrary"),
        name="gla",
    )(qk, vg, alr, qk, vg, alr, wg, bg, s0)


def _arrange_gate(w_gate, b_gate):
    KD = GLA_HEADS * GLA_DK
    wg = jnp.zeros((LANES, 2 * KD), F32)
    wg = wg.at[:GLA_RANK, :KD].set(w_gate[0]).at[GLA_RANK:2 * GLA_RANK, KD:].set(w_gate[1])
    return wg, jnp.concatenate([b_gate[0], b_gate[1]])[None, :]


def _shortconv_body(x_ref, p_ref, n_ref, w_ref, b_ref, v_ref, x1_ref, x2_ref):
    i = pl.program_id(1)
    last = pl.num_programs(1) - 1
    x = x_ref[0]
    tm = x.shape[0]
    prev = jnp.where(i > 0, p_ref[0][7:8, :], 0.0)
    nxt = jnp.where(i < last, n_ref[0][0:1, :], 0.0)
    rid = lax.broadcasted_iota(jnp.int32, x.shape, 0)
    dn = jnp.where(rid == 0, prev, pltpu.roll(x, 1, 0))
    up = jnp.where(rid == tm - 1, nxt, pltpu.roll(x, tm - 1, 0))
    w = w_ref[...]
    y = b_ref[...] + dn * w[0:1] + x * w[1:2] + up * w[2:3]
    v_ref[0] = y[:, :HY_CH]
    x1_ref[0] = y[:, HY_CH:2 * HY_CH]
    x2_ref[0] = y[:, 2 * HY_CH:]


def _shortconv(u, w, b):
    B, L, W = u.shape
    tm = min(512, L)
    nb = tm // 8
    row = pl.BlockSpec((1, tm, W), lambda b_, i: (b_, i, 0))
    prev = pl.BlockSpec((1, 8, W), lambda b_, i: (b_, jnp.maximum(i * nb - 1, 0), 0))
    nxt = pl.BlockSpec((1, 8, W), lambda b_, i: (b_, jnp.minimum((i + 1) * nb, L // 8 - 1), 0))
    o = pl.BlockSpec((1, tm, HY_CH), lambda b_, i: (b_, i, 0))
    return pl.pallas_call(
        _shortconv_body,
        grid=(B, L // tm),
        in_specs=[row, prev, nxt, _full((3, W)), _full((1, W))],
        out_specs=[o, o, o],
        out_shape=[jax.ShapeDtypeStruct((B, L, HY_CH), F32)] * 3,
        compiler_params=_cp("parallel", "parallel"),
        name="shortconv",
    )(u, u, u, w, b[None, :])


def _filter_feats(L):
    pos = jnp.arange(L, dtype=F32)
    t = pos / (L - 1)
    bands = (HY_EMB - 1) // 2
    freqs = jnp.linspace(1e-4, bands - 1, bands, dtype=F32)
    ang = (2.0 * math.pi * pos / L)[:, None] * freqs
    z = jnp.concatenate([t[:, None], jnp.cos(ang), -jnp.sin(ang)], axis=-1)
    z = jnp.pad(z, ((0, 0), (0, LANES - HY_EMB)))
    deltas = jnp.abs(jnp.linspace(math.log(HY_DECAY_TARGET) / HY_SLOW_DECAY,
                                  math.log(HY_DECAY_TARGET) / HY_FAST_DECAY, HY_CH, dtype=F32))
    return z, jnp.tile(deltas, 4)[None, :]


def _filter_body(z_ref, w1_ref, b1_ref, f1_ref, w2_ref, b2_ref, f2_ref, w3_ref, b3_ref, dl_ref,
                 h_ref, ss_ref, *, L):
    i = pl.program_id(0)
    z = z_ref[...]
    tm = z.shape[0]
    hid = jnp.sin(f1_ref[...] * (jnp.dot(z, w1_ref[...], precision=HI, preferred_element_type=F32) + b1_ref[...]))
    hid = jnp.sin(f2_ref[...] * (jnp.dot(hid, w2_ref[...], precision=HI, preferred_element_type=F32) + b2_ref[...]))
    h = jnp.dot(hid, w3_ref[...], precision=HI, preferred_element_type=F32) + b3_ref[...]
    pos = (lax.broadcasted_iota(jnp.int32, (tm, 1), 0) + i * tm).astype(F32)
    t = pos / (L - 1)
    h = h * jnp.exp(-t * dl_ref[...])

    @pl.when(i == 0)
    def _():
        ss_ref[...] = jnp.zeros_like(ss_ref)

    ss_ref[...] += jnp.sum(h * h, axis=0, keepdims=True)
    col = lax.broadcasted_iota(jnp.int32, h.shape, 1)
    is_bwd = (_idiv(col, HY_CH) & 1) == 1
    h_ref[...] = jnp.where(jnp.logical_and(is_bwd, pos == 0.0), 0.0, h)


def _filters(L, fw1, fb1, ff1, fw2, fb2, ff2, fw3, fb3):
    z, dl = _filter_feats(L)
    tm = min(1024, L)
    Hf = fw2.shape[0]
    w1 = jnp.pad(fw1, ((0, LANES - HY_EMB), (0, 0)))
    NC = fw3.shape[1]
    return pl.pallas_call(
        functools.partial(_filter_body, L=L),
        grid=(L // tm,),
        in_specs=[pl.BlockSpec((tm, LANES), lambda i: (i, 0)), _full((LANES, Hf)), _full((1, Hf)), _full((1, Hf)),
                  _full((Hf, Hf)), _full((1, Hf)), _full((1, Hf)), _full((Hf, NC)), _full((1, NC)), _full((1, NC))],
        out_specs=[pl.BlockSpec((tm, NC), lambda i: (i, 0)), _full((1, NC))],
        out_shape=[jax.ShapeDtypeStruct((L, NC), F32), jax.ShapeDtypeStruct((1, NC), F32)],
        compiler_params=_cp("arbitrary"),
        name="hy_filters",
    )(z, w1, fb1[None], ff1[None], fw2, fb2[None], ff2[None], fw3, fb3[None], dl)


def _dft_consts(L):
    N = 2 * L
    N2 = DFT_N2
    N1 = N // N2
    half = N1 // 2
    k1 = np.arange(N1)[:, None].astype(np.float64)
    n1 = np.arange(N1)[None, :].astype(np.float64)
    a1 = 2.0 * np.pi * k1 * n1 / N1
    f1r, f1i = np.cos(a1), -np.sin(a1)
    fa = np.concatenate([f1r[:, :half], f1i[:, :half]], axis=0)
    fb = np.concatenate([f1r[:half, :], f1i[:half, :]], axis=1) / N
    k2 = np.arange(N2)[:, None].astype(np.float64)
    n2 = np.arange(N2)[None, :].astype(np.float64)
    a2 = 2.0 * np.pi * k2 * n2 / N2
    f2r, f2i = np.cos(a2), -np.sin(a2)
    g = np.block([[f2r, -f2i], [f2i, f2r]])
    gc = np.block([[f2r, f2i], [-f2i, f2r]])
    at = 2.0 * np.pi * (np.arange(N1)[:, None] * np.arange(N2)[None, :] % N) / N
    twr, twi = np.cos(at), -np.sin(at)
    c = lambda a: jnp.asarray(a, dtype=F32)
    bc = lambda a: jnp.broadcast_to(c(a)[:, :, None], (N1, N2, LANES))
    return dict(N1=N1, N2=N2, half=half, fa=c(fa), fb=c(fb), g=c(g), gc=c(gc), twr=bc(twr), twi=bc(twi))


def _lanes(t, width):
    return jnp.concatenate([t] * (width // LANES), axis=-1)


def _dft1_body(f_ref, x_ref, o_ref):
    o_ref[0] = jnp.dot(f_ref[...], x_ref[0].astype(BF16), preferred_element_type=F32)


def _dft_stage1(fa, x):
    B, half, M = x.shape
    R = fa.shape[0]
    tn = min(4096, M)
    return pl.pallas_call(
        _dft1_body,
        grid=(B, M // tn),
        in_specs=[_full((R, half)), pl.BlockSpec((1, half, tn), lambda b, j: (b, 0, j))],
        out_specs=pl.BlockSpec((1, R, tn), lambda b, j: (b, 0, j)),
        out_shape=jax.ShapeDtypeStruct((B, R, M), F32),
        compiler_params=_cp("parallel", "parallel"),
        name="hy_dft1",
    )(fa.astype(BF16), x)


def _filter_spec_body(a_ref, twr_ref, twi_ref, g_ref, ss_ref, hf_ref):
    W = a_ref.shape[-1]
    ar, ai = a_ref[0, 0], a_ref[1, 0]
    twr, twi = _lanes(twr_ref[0], W), _lanes(twi_ref[0], W)
    xr = ar * twr - ai * twi
    xi = ar * twi + ai * twr
    z = jnp.dot(g_ref[...], jnp.concatenate([xr, xi], axis=0).astype(BF16), preferred_element_type=F32)
    n2 = z.shape[0] // 2
    zr, zi = z[:n2], z[n2:]
    ss = ss_ref[...]
    for o in range(2):
        f0, b0 = (2 * o) * HY_CH, (2 * o + 1) * HY_CH
        sc = lax.rsqrt(ss[:, f0:f0 + HY_CH] + ss[:, b0:b0 + HY_CH] + 1e-6)
        hf_ref[o, 0, 0] = (zr[:, f0:f0 + HY_CH] + zr[:, b0:b0 + HY_CH]) * sc
        hf_ref[o, 0, 1] = (zi[:, f0:f0 + HY_CH] - zi[:, b0:b0 + HY_CH]) * sc


def _filter_spectrum(h, ss, dc):
    L, NC = h.shape
    N1, N2, half = dc["N1"], dc["N2"], dc["half"]
    a = _dft_stage1(dc["fa"], h.reshape(1, half, N2 * NC))
    a = a.reshape(2, N1, N2, NC)
    return pl.pallas_call(
        _filter_spec_body,
        grid=(N1,),
        in_specs=[pl.BlockSpec((2, 1, N2, NC), lambda k: (0, k, 0, 0)),
                  pl.BlockSpec((1, N2, LANES), lambda k: (k, 0, 0)), pl.BlockSpec((1, N2, LANES), lambda k: (k, 0, 0)),
                  _full((2 * N2, 2 * N2)), _full((1, NC))],
        out_specs=pl.BlockSpec((2, 1, 2, N2, HY_CH), lambda k: (0, k, 0, 0, 0)),
        out_shape=jax.ShapeDtypeStruct((2, N1, 2, N2, HY_CH), F32),
        compiler_params=_cp("parallel"),
        name="hy_filter_spec",
    )(a, dc["twr"], dc["twi"], dc["g"].astype(BF16), ss)


def _spec_mul_body(a_ref, twr_ref, twi_ref, g_ref, gc_ref, hf_ref, o_ref):
    W = a_ref.shape[-1]
    ar, ai = a_ref[0, 0, 0], a_ref[0, 1, 0]
    twr, twi = _lanes(twr_ref[0], W), _lanes(twi_ref[0], W)
    xr = ar * twr - ai * twi
    xi = ar * twi + ai * twr
    z = jnp.dot(g_ref[...], jnp.concatenate([xr, xi], axis=0).astype(BF16), preferred_element_type=F32)
    n2 = z.shape[0] // 2
    zr, zi = z[:n2], z[n2:]
    hr, hi = hf_ref[0, 0], hf_ref[0, 1]
    yr = zr * hr - zi * hi
    yi = zr * hi + zi * hr
    b = jnp.dot(gc_ref[...], jnp.concatenate([yr, yi], axis=0).astype(BF16), preferred_element_type=F32)
    br, bi = b[:n2], b[n2:]
    o_ref[0, 0, 0] = br * twr + bi * twi
    o_ref[0, 1, 0] = bi * twr - br * twi


def _spec_mul(a, hf, dc):
    B = a.shape[0]
    N1, N2 = dc["N1"], dc["N2"]
    C = a.shape[-1]
    blk = pl.BlockSpec((1, 2, 1, N2, C), lambda b, k: (b, 0, k, 0, 0))
    tw = pl.BlockSpec((1, N2, LANES), lambda b, k: (k, 0, 0))
    return pl.pallas_call(
        _spec_mul_body,
        grid=(B, N1),
        in_specs=[blk, tw, tw, _full((2 * N2, 2 * N2)), _full((2 * N2, 2 * N2)),
                  pl.BlockSpec((1, 2, N2, C), lambda b, k: (k, 0, 0, 0))],
        out_specs=blk,
        out_shape=jax.ShapeDtypeStruct(a.shape, F32),
        compiler_params=_cp("parallel", "parallel"),
        name="hy_spec_mul",
    )(a, dc["twr"], dc["twi"], dc["g"].astype(BF16), dc["gc"].astype(BF16), hf)


def _dft3_body(f_ref, b_ref, u_ref, gate_ref, skip_ref, o_ref):
    y = jnp.dot(f_ref[...], b_ref[0].astype(BF16), preferred_element_type=F32)
    o_ref[0] = gate_ref[0] * (y + u_ref[0] * skip_ref[...])


def _dft_stage3(fb, bm, u, gate, skip_t):
    B, R, M = bm.shape
    half = fb.shape[0]
    tn = skip_t.shape[1]
    row = pl.BlockSpec((1, half, tn), lambda b, j: (b, 0, j))
    return pl.pallas_call(
        _dft3_body,
        grid=(B, M // tn),
        in_specs=[_full((half, R)), pl.BlockSpec((1, R, tn), lambda b, j: (b, 0, j)), row, row, _full((1, tn))],
        out_specs=row,
        out_shape=jax.ShapeDtypeStruct((B, half, M), F32),
        compiler_params=_cp("parallel", "parallel"),
        name="hy_dft3",
    )(fb.astype(BF16), bm, u, gate, skip_t)


def _longconv_gated(u, gate, hf, skip, dc):
    B, L, C = u.shape
    N1, N2, half = dc["N1"], dc["N2"], dc["half"]
    M = N2 * C
    uv = u.reshape(B, half, M)
    a = _dft_stage1(dc["fa"], uv).reshape(B, 2, N1, N2, C)
    bm = _spec_mul(a, hf, dc).reshape(B, 2 * N1, M)
    tn = min(4096, M)
    skip_t = jnp.tile(skip, tn // C)[None, :]
    return _dft_stage3(dc["fb"], bm, uv, gate.reshape(B, half, M), skip_t).reshape(B, L, C)


def _hyena(hyu, conv_w, conv_b, filt, skip):
    B, L, _ = hyu.shape
    v, x1, x2 = _shortconv(hyu, conv_w, conv_b)
    h, ss = _filters(L, *filt)
    dc = _dft_consts(L)
    hf = _filter_spectrum(h, ss, dc)
    z1 = _longconv_gated(v, x1, hf[0], skip[0], dc)
    return _longconv_gated(z1, x2, hf[1], skip[1], dc)


def _hyena_ctx_body(v_ref, x1_ref, x2_ref, h_ref, ss_ref, skip_ref, fc_ref, gc_ref, o_ref):
    fc, gc = fc_ref[...], gc_ref[...]
    n = fc.shape[0] // 2
    ss = ss_ref[...]
    h = h_ref[...]

    def conv(u, o):
        f0, b0 = (2 * o) * HY_CH, (2 * o + 1) * HY_CH
        sc = lax.rsqrt(ss[:, f0:f0 + HY_CH] + ss[:, b0:b0 + HY_CH] + 1e-6)
        x = jnp.dot(fc, u, precision=HI, preferred_element_type=F32)
        hf = jnp.dot(fc, h[:, f0:f0 + HY_CH], precision=HI, preferred_element_type=F32)
        hb = jnp.dot(fc, h[:, b0:b0 + HY_CH], precision=HI, preferred_element_type=F32)
        hr = (hf[:n] + hb[:n]) * sc
        hi = (hf[n:] - hb[n:]) * sc
        yr = x[:n] * hr - x[n:] * hi
        yi = x[:n] * hi + x[n:] * hr
        y = jnp.dot(gc, jnp.concatenate([yr, yi], axis=0), precision=HI, preferred_element_type=F32)
        return y + u * skip_ref[o:o + 1, :]

    z1 = x1_ref[0] * conv(v_ref[0], 0)
    o_ref[0] = x2_ref[0] * conv(z1, 1)


def _hyena_ctx(hyu, conv_w, conv_b, filt, skip):
    B, L, _ = hyu.shape
    v, x1, x2 = _shortconv(hyu, conv_w, conv_b)
    h, ss = _filters(L, *filt)
    N = 2 * L
    ang = 2.0 * np.pi * (np.arange(N)[:, None] * np.arange(L)[None, :] % N) / N
    fr, fi = np.cos(ang), -np.sin(ang)
    fc = jnp.asarray(np.concatenate([fr, fi], axis=0), dtype=F32)
    gc = jnp.asarray(np.concatenate([fr.T, fi.T], axis=1) / N, dtype=F32)
    row = pl.BlockSpec((1, L, HY_CH), lambda b: (b, 0, 0))
    return pl.pallas_call(
        _hyena_ctx_body,
        grid=(B,),
        in_specs=[row, row, row, _full(h.shape), _full(ss.shape), _full(skip.shape), _full(fc.shape), _full(gc.shape)],
        out_specs=row,
        out_shape=jax.ShapeDtypeStruct((B, L, HY_CH), F32),
        compiler_params=_cp("parallel"),
        name="hyena_ctx",
    )(v, x1, x2, h, ss, skip, fc, gc)


HEAD_PAD = 128


def _rope_swap(w):
    a, b, c, d = w[..., 0:8], w[..., 8:16], w[..., 16:24], w[..., 24:32]
    return jnp.concatenate([-b, a, -d, c], axis=-1)


def _arrange_wq(w_uq):
    R = w_uq.shape[0]
    w = w_uq.reshape(R, MLA_HEADS, MLA_NOPE + MLA_ROPE)
    rope = w[..., MLA_NOPE:]
    out = jnp.concatenate([w[..., :MLA_NOPE], rope, _rope_swap(rope)], axis=-1)
    return out.reshape(R, MLA_HEADS * HEAD_PAD).astype(BF16)


def _arrange_wkv(w_ukv):
    R = w_ukv.shape[0]
    w = w_ukv.reshape(R, MLA_HEADS, MLA_NOPE + MLA_V)
    wk = jnp.concatenate([w[..., :MLA_NOPE], jnp.zeros((R, MLA_HEADS, HEAD_PAD - MLA_NOPE), w.dtype)], axis=-1)
    wv = w[..., MLA_NOPE:]
    return wk.reshape(R, MLA_HEADS * HEAD_PAD).astype(BF16), wv.reshape(R, MLA_HEADS * MLA_V).astype(BF16)


def _kr_place():
    e = np.zeros((LANES, MLA_HEADS * HEAD_PAD), np.float32)
    es = np.zeros((LANES, MLA_HEADS * HEAD_PAD), np.float32)
    for h in range(MLA_HEADS):
        base = h * HEAD_PAD + MLA_NOPE
        for j in range(MLA_ROPE):
            e[j, base + j] = 1.0
            blk, r = divmod(j, 16)
            if r < 8:
                es[16 * blk + r + 8, base + j] = -1.0
            else:
                es[16 * blk + r - 8, base + j] = 1.0
    return jnp.asarray(e).astype(BF16), jnp.asarray(es).astype(BF16)


def _rope_tables(L, rope):
    if rope:
        t = np.arange(L)
        row, col = (t // GRID_W).astype(np.float32), (t % GRID_W).astype(np.float32)
        half = MLA_ROPE // 2
        inv = ROPE_BASE ** (-jnp.arange(0, half, 2, dtype=F32) / half)
        ar = jnp.asarray(row)[:, None] * inv
        ac = jnp.asarray(col)[:, None] * inv
        cos = jnp.concatenate([jnp.cos(ar), jnp.cos(ar), jnp.cos(ac), jnp.cos(ac)], axis=-1)
        sin = jnp.concatenate([jnp.sin(ar), jnp.sin(ar), jnp.sin(ac), jnp.sin(ac)], axis=-1)
    else:
        cos, sin = jnp.ones((L, MLA_ROPE), F32), jnp.zeros((L, MLA_ROPE), F32)
    return cos, sin


def _rms_rows(x, g, eps=1e-6):
    return x * lax.rsqrt(jnp.mean(x * x, axis=-1, keepdims=True) + eps) * g


def _qproj_body(cq_ref, g_ref, w_ref, t1_ref, t2_ref, q_ref):
    xn = _rms_rows(cq_ref[0], g_ref[...])
    acc = jnp.dot(xn.astype(BF16), w_ref[...], preferred_element_type=F32)
    W = acc.shape[1]
    t1, t2 = _lanes(t1_ref[...], W), _lanes(t2_ref[...], W)
    q_ref[0] = (acc * t1 + pltpu.roll(acc, W - MLA_ROPE, 1) * t2).astype(q_ref.dtype)


def _qproj(cq, g, wq, cos, sin):
    B, L, R = cq.shape
    tm = min(512, L)
    W = wq.shape[1]
    ones, zeros = jnp.ones((L, MLA_NOPE), F32), jnp.zeros((L, MLA_ROPE), F32)
    qs = MLA_SCALE * math.log2(math.e)
    t1 = jnp.concatenate([ones, cos, zeros], axis=-1) * qs
    t2 = jnp.concatenate([jnp.zeros((L, MLA_NOPE), F32), sin, zeros], axis=-1) * qs
    tab = pl.BlockSpec((tm, HEAD_PAD), lambda b, i: (i, 0))
    return pl.pallas_call(
        _qproj_body,
        grid=(B, L // tm),
        in_specs=[pl.BlockSpec((1, tm, R), lambda b, i: (b, i, 0)), _full((1, R)), _full((R, W)), tab, tab],
        out_specs=pl.BlockSpec((1, tm, W), lambda b, i: (b, i, 0)),
        out_shape=jax.ShapeDtypeStruct((B, L, W), BF16),
        compiler_params=_cp("parallel", "parallel"),
        name="mla_qproj",
    )(cq, g[None, :], wq, t1, t2)


def _kvproj_body(c_ref, g_ref, wk_ref, wv_ref, e_ref, es_ref, cos_ref, sin_ref, k_ref, v_ref):
    c = c_ref[0]
    R = MLA_KV_RANK
    xn = _rms_rows(c[:, :R], g_ref[...]).astype(BF16)
    kr = c[:, R:]
    acc = jnp.dot(xn, wk_ref[...], preferred_element_type=F32)
    acc += jnp.dot((kr * cos_ref[...]).astype(BF16), e_ref[...], preferred_element_type=F32)
    acc += jnp.dot((kr * sin_ref[...]).astype(BF16), es_ref[...], preferred_element_type=F32)
    k_ref[0] = acc.astype(k_ref.dtype)
    v_ref[0] = jnp.dot(xn, wv_ref[...], preferred_element_type=F32).astype(v_ref.dtype)


def _kvproj(ckvr, g, wk, wv, cos, sin):
    B, L, Wc = ckvr.shape
    tm = min(512, L)
    pad = jnp.zeros((L, LANES - MLA_ROPE), F32)
    cos_p, sin_p = jnp.concatenate([cos, pad], axis=-1), jnp.concatenate([sin, pad], axis=-1)
    e, es = _kr_place()
    tab = pl.BlockSpec((tm, LANES), lambda b, i: (i, 0))
    Wk, Wv = wk.shape[1], wv.shape[1]
    return pl.pallas_call(
        _kvproj_body,
        grid=(B, L // tm),
        in_specs=[pl.BlockSpec((1, tm, Wc), lambda b, i: (b, i, 0)), _full((1, MLA_KV_RANK)),
                  _full(wk.shape), _full(wv.shape), _full(e.shape), _full(es.shape), tab, tab],
        out_specs=[pl.BlockSpec((1, tm, Wk), lambda b, i: (b, i, 0)), pl.BlockSpec((1, tm, Wv), lambda b, i: (b, i, 0))],
        out_shape=[jax.ShapeDtypeStruct((B, L, Wk), BF16), jax.ShapeDtypeStruct((B, L, Wv), BF16)],
        compiler_params=_cp("parallel", "parallel"),
        name="mla_kvproj",
    )(ckvr, g[None, :], wk, wv, e, es, cos_p, sin_p)


FLASH_ROWS = 256
FLASH_KEYS = 256


def _flash_body(q_ref, k_ref, v_ref, o_ref, m_ref, l_ref, acc_ref, s_ref, *, R):
    j = pl.program_id(3)
    tq, tk = q_ref.shape[1], k_ref.shape[1]
    CK = FLASH_KEYS
    npc = CK // LANES

    @pl.when(j == 0)
    def _():
        m_ref[...] = jnp.full_like(m_ref, -jnp.inf)
        l_ref[...] = jnp.zeros_like(l_ref)
        acc_ref[...] = jnp.zeros_like(acc_ref)

    def pass1(a, r):
        lo, r0 = a * HEAD_PAD, r * R
        q = q_ref[0, r0:r0 + R, lo:lo + HEAD_PAD]
        mp = None
        for c in range(tk // CK):
            kc = k_ref[0, c * CK:(c + 1) * CK, lo:lo + HEAD_PAD]
            s = lax.dot_general(q, kc, (((1,), (1,)), ((), ())), preferred_element_type=F32)
            s_ref[r0:r0 + R, c * CK:(c + 1) * CK] = s
            for w in range(npc):
                pc = s[:, w * LANES:(w + 1) * LANES]
                mp = pc if mp is None else jnp.maximum(mp, pc)
        m_old = m_ref[a, r0:r0 + R, :]
        return m_old, jnp.maximum(m_old, jnp.max(mp, axis=1, keepdims=True))

    def pass2(a, r, m_old, m_new):
        r0 = r * R
        alpha = jnp.exp2(m_old - m_new)
        lp = jnp.zeros((R, LANES), F32)
        pv = jnp.zeros((R, 2 * MLA_V), F32)
        for c in range(tk // CK):
            s = s_ref[r0:r0 + R, c * CK:(c + 1) * CK]
            ps = [jnp.exp2(s[:, w * LANES:(w + 1) * LANES] - m_new) for w in range(npc)]
            for p_ in ps:
                lp = lp + p_
            p = jnp.concatenate(ps, axis=1).astype(BF16)
            pv = pv + jnp.dot(p, v_ref[0, c * CK:(c + 1) * CK, :], preferred_element_type=F32)
        l_ref[a, r0:r0 + R, :] = alpha * l_ref[a, r0:r0 + R, :] + jnp.sum(lp, axis=1, keepdims=True)
        acc_ref[a, r0:r0 + R, :] = alpha * acc_ref[a, r0:r0 + R, :] + pv
        m_ref[a, r0:r0 + R, :] = m_new

    blocks = [(a, r) for a in range(2) for r in range(tq // R)]
    pend = pass1(*blocks[0])
    for i, blk in enumerate(blocks):
        nxt = pass1(*blocks[i + 1]) if i + 1 < len(blocks) else None
        pass2(*blk, *pend)
        pend = nxt

    @pl.when(j == pl.num_programs(3) - 1)
    def _():
        lane = lax.broadcasted_iota(jnp.int32, acc_ref.shape[1:], 1)
        o_ref[0] = jnp.where(lane < MLA_V, acc_ref[0] / l_ref[0], acc_ref[1] / l_ref[1])


def _flash_tiles(Lq, Lk):
    tq = min(1024, Lq)
    tk = Lk
    for cand in (3328, 1280, 1024, 512, 256):
        if Lk % cand == 0:
            tk = cand
            break
    return tq, tk


def _flash(q, k, v):
    B, Lq, _ = q.shape
    Lk = k.shape[1]
    tq, tk = _flash_tiles(Lq, Lk)
    hp = MLA_HEADS // 2
    return pl.pallas_call(
        functools.partial(_flash_body, R=min(FLASH_ROWS, tq)),
        grid=(B, hp, Lq // tq, Lk // tk),
        in_specs=[pl.BlockSpec((1, tq, 2 * HEAD_PAD), lambda b, h, i, j: (b, i, h)),
                  pl.BlockSpec((1, tk, 2 * HEAD_PAD), lambda b, h, i, j: (b, j, h)),
                  pl.BlockSpec((1, tk, 2 * MLA_V), lambda b, h, i, j: (b, j, h))],
        out_specs=pl.BlockSpec((1, tq, 2 * MLA_V), lambda b, h, i, j: (b, i, h)),
        out_shape=jax.ShapeDtypeStruct((B, Lq, MLA_HEADS * MLA_V), F32),
        scratch_shapes=[pltpu.VMEM((2, tq, LANES), F32), pltpu.VMEM((2, tq, LANES), F32),
                        pltpu.VMEM((2, tq, 2 * MLA_V), F32), pltpu.VMEM((tq, tk), F32)],
        compiler_params=_cp("parallel", "parallel", "parallel", "arbitrary"),
        name="mla_flash",
    )(q, k, v)


def _layernorm_rows(x, g, b, eps=1e-5):
    mu = jnp.mean(x, axis=-1, keepdims=True)
    xc = x - mu
    var = jnp.mean(xc * xc, axis=-1, keepdims=True)
    return xc * lax.rsqrt(var + eps) * g + b


def _outproj_body(of_ref, ob_ref, g_ref, hy_ref, om_ref, x_ref, gate_ref, gg_ref, hg_ref, mg_ref,
                  w_ref, lg_ref, lb_ref, o_ref, *, alpha):
    VD = GLA_HEADS * GLA_DV
    o = of_ref[0] + ob_ref[0]
    r = _idiv(lax.broadcasted_iota(jnp.int32, (VD, VD), 0), GLA_DV)
    c = _idiv(lax.broadcasted_iota(jnp.int32, (VD, VD), 1), GLA_DV)
    grp = (r == c).astype(F32)
    ms = jnp.dot(o * o, grp, precision=HI, preferred_element_type=F32) * (1.0 / GLA_DV)
    g = g_ref[0]
    ya = o * lax.rsqrt(ms + 1e-6) * gg_ref[...] * (g * jax.nn.sigmoid(g))
    yb = _rms_rows(hy_ref[0], hg_ref[...])
    yc = _rms_rows(om_ref[0], mg_ref[...])
    acc = jnp.dot(ya.astype(BF16), w_ref[0:VD, :], preferred_element_type=F32)
    acc += jnp.dot(yb.astype(BF16), w_ref[VD:VD + HY_CH, :], preferred_element_type=F32)
    acc += jnp.dot(yc.astype(BF16), w_ref[VD + HY_CH:, :], preferred_element_type=F32)
    o_ref[0] = _layernorm_rows(alpha * x_ref[0] + gate_ref[0] * acc, lg_ref[...], lb_ref[...])


def _outproj(of, ob, vg, hy, om, x, gate, gla_g, hy_g, mla_g, w_out, ln_g, ln_b, alpha):
    B, L, D = x.shape
    tm = min(512, L)
    VD = GLA_HEADS * GLA_DV
    MD = MLA_HEADS * MLA_V
    row = lambda w: pl.BlockSpec((1, tm, w), lambda b, i: (b, i, 0))
    return pl.pallas_call(
        functools.partial(_outproj_body, alpha=alpha),
        grid=(B, L // tm),
        in_specs=[row(VD), row(VD), pl.BlockSpec((1, tm, VD), lambda b, i: (b, i, 1)), row(HY_CH), row(MD), row(D),
                  pl.BlockSpec((1, 1, D), lambda b, i: (b, 0, 0)), _full((1, VD)), _full((1, HY_CH)), _full((1, MD)),
                  _full(w_out.shape), _full((1, D)), _full((1, D))],
        out_specs=row(D),
        out_shape=jax.ShapeDtypeStruct((B, L, D), F32),
        compiler_params=_cp("parallel", "parallel"),
        name="outproj",
    )(of, ob, vg, hy, om, x, gate, jnp.tile(gla_g, GLA_HEADS)[None, :], hy_g[None, :], mla_g[None, :],
      w_out.astype(BF16), ln_g[None, :], ln_b[None, :])


def _ffn_body(x_ref, sh_ref, sc_ref, gate_ref, w1_ref, w3_ref, w2_ref, lg_ref, lb_ref, o_ref, h_ref, acc_ref, *, alpha):
    j = pl.program_id(2)

    @pl.when(j == 0)
    def _():
        h_ref[...] = (x_ref[0] * (1.0 + sc_ref[0]) + sh_ref[0]).astype(BF16)
        acc_ref[...] = jnp.zeros_like(acc_ref)

    h = h_ref[...]
    a = jnp.dot(h, w1_ref[...], preferred_element_type=F32)
    b = jnp.dot(h, w3_ref[...], preferred_element_type=F32)
    t = (a * jax.nn.sigmoid(a) * b).astype(BF16)
    acc_ref[...] += jnp.dot(t, w2_ref[...], preferred_element_type=F32)

    @pl.when(j == pl.num_programs(2) - 1)
    def _():
        o_ref[0] = _layernorm_rows(alpha * x_ref[0] + gate_ref[0] * acc_ref[...], lg_ref[...], lb_ref[...])


def _ffn_tile(F):
    for cand in (512, 256, 128):
        if F % cand == 0:
            return cand
    return F


def _ffn(x, shift, scale, gate, w1, w3, w2, ln_g, ln_b, alpha):
    B, L, D = x.shape
    F = w1.shape[1]
    tm = min(1024, L)
    tf = _ffn_tile(F)
    row = pl.BlockSpec((1, tm, D), lambda b, i, j: (b, i, 0))
    vec = pl.BlockSpec((1, 1, D), lambda b, i, j: (b, 0, 0))
    return pl.pallas_call(
        functools.partial(_ffn_body, alpha=alpha),
        grid=(B, L // tm, F // tf),
        in_specs=[row, vec, vec, vec,
                  pl.BlockSpec((D, tf), lambda b, i, j: (0, j)), pl.BlockSpec((D, tf), lambda b, i, j: (0, j)),
                  pl.BlockSpec((tf, D), lambda b, i, j: (j, 0)), _full((1, D)), _full((1, D))],
        out_specs=row,
        out_shape=jax.ShapeDtypeStruct((B, L, D), F32),
        scratch_shapes=[pltpu.VMEM((tm, D), BF16), pltpu.VMEM((tm, D), F32)],
        compiler_params=_cp("parallel", "parallel", "arbitrary"),
        name="ffn",
    )(x, shift, scale, gate, w1.astype(BF16), w3.astype(BF16), w2.astype(BF16), ln_g[None, :], ln_b[None, :])


def _router_body(x_ref, sh_ref, sc_ref, wr_ref, g_ref):
    h = x_ref[0] * (1.0 + sc_ref[0]) + sh_ref[0]
    logits = jnp.dot(h, wr_ref[...], precision=HI, preferred_element_type=F32)
    lane = lax.broadcasted_iota(jnp.int32, logits.shape, 1).astype(F32)
    logits = jnp.where(lane < N_EXPERTS, logits, -jnp.inf)
    m1 = jnp.max(logits, axis=1, keepdims=True)
    i1 = jnp.min(jnp.where(logits == m1, lane, float(LANES)), axis=1, keepdims=True)
    rest = jnp.where(lane == i1, -jnp.inf, logits)
    m2 = jnp.max(rest, axis=1, keepdims=True)
    i2 = jnp.min(jnp.where(rest == m2, lane, float(LANES)), axis=1, keepdims=True)
    e2 = jnp.exp(m2 - m1)
    w1 = 1.0 / (1.0 + e2)
    w2 = e2 / (1.0 + e2)
    g_ref[0] = jnp.where(lane == i1, w1, 0.0) + jnp.where(lane == i2, w2, 0.0)


def _router(x, shift, scale, w_router):
    B, L, D = x.shape
    tm = min(512, L)
    wr = jnp.pad(w_router, ((0, 0), (0, LANES - N_EXPERTS)))
    vec = pl.BlockSpec((1, 1, D), lambda b, i: (b, 0, 0))
    return pl.pallas_call(
        _router_body,
        grid=(B, L // tm),
        in_specs=[pl.BlockSpec((1, tm, D), lambda b, i: (b, i, 0)), vec, vec, _full((D, LANES))],
        out_specs=pl.BlockSpec((1, tm, LANES), lambda b, i: (b, i, 0)),
        out_shape=jax.ShapeDtypeStruct((B, L, LANES), F32),
        compiler_params=_cp("parallel", "parallel"),
        name="moe_router",
    )(x, shift, scale, wr)


def _moe_body(x_ref, sh_ref, sc_ref, gate_ref, gts_ref, w1_ref, w3_ref, w2_ref, lg_ref, lb_ref, o_ref,
              h_ref, acc_ref, *, alpha):
    e = pl.program_id(2)
    j = pl.program_id(3)

    @pl.when(jnp.logical_and(e == 0, j == 0))
    def _():
        h_ref[...] = (x_ref[0] * (1.0 + sc_ref[0]) + sh_ref[0]).astype(BF16)
        acc_ref[...] = jnp.zeros_like(acc_ref)

    gts = gts_ref[0]
    lane = lax.broadcasted_iota(jnp.int32, gts.shape, 1)
    ge = jnp.sum(jnp.where(lane == e, gts, 0.0), axis=1, keepdims=True)
    h = h_ref[...]
    a = jnp.dot(h, w1_ref[0], preferred_element_type=F32)
    b = jnp.dot(h, w3_ref[0], preferred_element_type=F32)
    t = (a * jax.nn.sigmoid(a) * b * ge).astype(BF16)
    acc_ref[...] += jnp.dot(t, w2_ref[0], preferred_element_type=F32)

    @pl.when(jnp.logical_and(e == pl.num_programs(2) - 1, j == pl.num_programs(3) - 1))
    def _():
        o_ref[0] = _layernorm_rows(alpha * x_ref[0] + gate_ref[0] * acc_ref[...], lg_ref[...], lb_ref[...])


def _moe(x, shift, scale, gate, w_router, w1, w3, w2, ln_g, ln_b, alpha):
    B, L, D = x.shape
    E, _, F = w1.shape
    gts = _router(x, shift, scale, w_router)
    tm = min(1024, L)
    tf = _ffn_tile(F)
    row = pl.BlockSpec((1, tm, D), lambda b, i, e, j: (b, i, 0))
    vec = pl.BlockSpec((1, 1, D), lambda b, i, e, j: (b, 0, 0))
    return pl.pallas_call(
        functools.partial(_moe_body, alpha=alpha),
        grid=(B, L // tm, E, F // tf),
        in_specs=[row, vec, vec, vec, pl.BlockSpec((1, tm, LANES), lambda b, i, e, j: (b, i, 0)),
                  pl.BlockSpec((1, D, tf), lambda b, i, e, j: (e, 0, j)), pl.BlockSpec((1, D, tf), lambda b, i, e, j: (e, 0, j)),
                  pl.BlockSpec((1, tf, D), lambda b, i, e, j: (e, j, 0)), _full((1, D)), _full((1, D))],
        out_specs=row,
        out_shape=jax.ShapeDtypeStruct((B, L, D), F32),
        scratch_shapes=[pltpu.VMEM((tm, D), BF16), pltpu.VMEM((tm, D), F32)],
        compiler_params=_cp("parallel", "parallel", "arbitrary", "arbitrary"),
        name="moe",
    )(x, shift, scale, gate, gts, w1.astype(BF16), w3.astype(BF16), w2.astype(BF16), ln_g[None, :], ln_b[None, :])


def _mod_body(c_ref, w_ref, b_ref, o_ref):
    c = c_ref[...]
    s = c * jax.nn.sigmoid(c)
    o_ref[...] = jnp.dot(s, w_ref[...], precision=HI, preferred_element_type=F32) + b_ref[...]


def _modulation(cc, w_mod, b_mod):
    R, D = cc.shape
    N = w_mod.shape[1]
    tn = 1024
    return pl.pallas_call(
        _mod_body,
        grid=(N // tn,),
        in_specs=[_full((R, D)), pl.BlockSpec((D, tn), lambda j: (0, j)), pl.BlockSpec((1, tn), lambda j: (0, j))],
        out_specs=pl.BlockSpec((R, tn), lambda j: (0, j)),
        out_shape=jax.ShapeDtypeStruct((R, N), F32),
        compiler_params=_cp("parallel"),
        name="modulation",
    )(cc, w_mod, b_mod[None, :])


def kernel(x, c, ctx, c_ctx, w_mod, b_mod, w_in, gla_w_gate, gla_b_gate, gla_norm_g, hy_conv_w, hy_conv_b, hy_f_w1, hy_f_b1, hy_f_freq1, hy_f_w2, hy_f_b2, hy_f_freq2, hy_f_w3, hy_f_b3, hy_skip, hy_norm_g, mla_q_norm_g, mla_w_uq, mla_kv_norm_g, mla_w_ukv, mla_norm_g, w_out, ln_g, ln_b, ffn_w1, ffn_w3, ffn_w2, moe_router, moe_w1, moe_w3, moe_w2):
    B, L, D = x.shape
    Lc = ctx.shape[1]
    depth = w_mod.shape[0]
    alpha = (2.0 * depth) ** 0.25
    cc = jnp.zeros((8, D), F32).at[:B].set(c).at[B].set(c_ctx)
    cos, sin = _rope_tables(L, True)
    cos_c, sin_c = _rope_tables(Lc, False)
    KD, VD = GLA_HEADS * GLA_DK, GLA_HEADS * GLA_DV
    xc = ctx
    for l in range(depth):
        need_ctx = l < depth - 1
        mods = _modulation(cc, w_mod[l], b_mod[l])
        m = [mods[:B, k * D:(k + 1) * D][:, None, :] for k in range(6)]
        mc = [jnp.broadcast_to(mods[B, k * D:(k + 1) * D][None, None, :], (B, 1, D)) for k in range(6)]
        w_arr = _arrange_w_in(w_in[l])
        wg, bg = _arrange_gate(gla_w_gate[l], gla_b_gate[l])
        filt = (hy_f_w1[l], hy_f_b1[l], hy_f_freq1[l], hy_f_w2[l], hy_f_b2[l], hy_f_freq2[l], hy_f_w3[l], hy_f_b3[l])
        wq = _arrange_wq(mla_w_uq[l])
        wk, wv = _arrange_wkv(mla_w_ukv[l])

        hyu, qk, vg, alr, cq, ckvr = _inproj(x, m[0], m[1], w_arr)
        hyu_c, qk_c, vg_c, alr_c, cq_c, ckvr_c = _inproj(xc, mc[0], mc[1], w_arr)

        of_c, ob_c, s_c = _gla(qk_c, vg_c, alr_c, wg, bg, jnp.zeros((B, 2, KD, VD), F32))
        of, ob, _ = _gla(qk, vg, alr, wg, bg, s_c)
        hy = _hyena(hyu, hy_conv_w[l], hy_conv_b[l], filt, hy_skip[l])
        k_c, v_c = _kvproj(ckvr_c, mla_kv_norm_g[l], wk, wv, cos_c, sin_c)
        k_m, v_m = _kvproj(ckvr, mla_kv_norm_g[l], wk, wv, cos, sin)
        q_m = _qproj(cq, mla_q_norm_g[l], wq, cos, sin)
        om = _flash(q_m, jnp.concatenate([k_c, k_m], axis=1), jnp.concatenate([v_c, v_m], axis=1))

        x = _outproj(of, ob, vg, hy, om, x, m[2], gla_norm_g[l], hy_norm_g[l], mla_norm_g[l], w_out[l],
                     ln_g[l, 0], ln_b[l, 0], alpha)
        if need_ctx:
            hy_c = _hyena_ctx(hyu_c, hy_conv_w[l], hy_conv_b[l], filt, hy_skip[l])
            q_c = _qproj(cq_c, mla_q_norm_g[l], wq, cos_c, sin_c)
            om_c = _flash(q_c, k_c, v_c)
            xc = _outproj(of_c, ob_c, vg_c, hy_c, om_c, xc, mc[2], gla_norm_g[l], hy_norm_g[l], mla_norm_g[l],
                          w_out[l], ln_g[l, 0], ln_b[l, 0], alpha)

        i = l // 2
        if l % 2 == 0:
            x = _ffn(x, m[3], m[4], m[5], ffn_w1[i], ffn_w3[i], ffn_w2[i], ln_g[l, 1], ln_b[l, 1], alpha)
            if need_ctx:
                xc = _ffn(xc, mc[3], mc[4], mc[5], ffn_w1[i], ffn_w3[i], ffn_w2[i], ln_g[l, 1], ln_b[l, 1], alpha)
        else:
            x = _moe(x, m[3], m[4], m[5], moe_router[i], moe_w1[i], moe_w3[i], moe_w2[i], ln_g[l, 1], ln_b[l, 1], alpha)
            if need_ctx:
                xc = _moe(xc, mc[3], mc[4], mc[5], moe_router[i], moe_w1[i], moe_w3[i], moe_w2[i], ln_g[l, 1],
                          ln_b[l, 1], alpha)
    return x
```

```python
import functools
import math

import numpy as np
import jax
import jax.numpy as jnp
from jax import lax
from jax.experimental import pallas as pl
from jax.experimental.pallas import tpu as pltpu

F32 = jnp.float32
BF16 = jnp.bfloat16
HI = lax.Precision.HIGHEST

GRID_W = 64
GLA_HEADS, GLA_DK, GLA_DV, GLA_RANK, GLA_TAU = 4, 32, 64, 16, 16.0
HY_CH, HY_EMB = 256, 33
HY_DECAY_TARGET, HY_FAST_DECAY, HY_SLOW_DECAY = 1e-2, 0.3, 1.5
MLA_HEADS, MLA_Q_RANK, MLA_KV_RANK, MLA_NOPE, MLA_ROPE, MLA_V = 8, 256, 128, 64, 32, 64
MLA_SCALE = (MLA_NOPE + MLA_ROPE) ** -0.5
ROPE_BASE = 10000.0
N_EXPERTS = 8
IN_SPLITS = (128, 128, 256, 256, 32, 768, 256, 128, 32)

LANES = 128
VMEM_LIMIT = 56 * 1024 * 1024

GLA_CHUNK = 128
DFT_N2 = 256


def _cp(*sem):
    return pltpu.CompilerParams(dimension_semantics=sem, vmem_limit_bytes=VMEM_LIMIT)


def _full(shape):
    n = len(shape)
    return pl.BlockSpec(shape, lambda *_: (0,) * n)


def _idiv(x, d):
    assert d & (d - 1) == 0
    return lax.shift_right_logical(x, int(math.log2(d)))


INPROJ_WIDTHS = (768, 256, 512, 128, 256, 256)


def _arrange_w_in(w):
    cuts = np.cumsum(IN_SPLITS)[:-1]
    qa, ka, va, ga, alr, hyu, cq, ckv, kr = jnp.split(w, [int(c) for c in cuts], axis=1)
    z96 = jnp.zeros((w.shape[0], 96), w.dtype)
    return jnp.concatenate([hyu, qa, ka, va, ga, alr, z96, cq, ckv, kr, z96], axis=1).astype(BF16)


def _inproj_body(x_ref, sh_ref, sc_ref, w_ref, *out_refs):
    h = x_ref[0] * (1.0 + sc_ref[0]) + sh_ref[0]
    acc = jnp.dot(h.astype(BF16), w_ref[...], preferred_element_type=F32)
    off = 0
    for r in out_refs:
        w = r.shape[-1]
        r[0] = acc[:, off:off + w]
        off += w


def _inproj(x, shift, scale, w_arr):
    B, L, D = x.shape
    tm = min(512, L)
    n = w_arr.shape[1]
    row = lambda w: pl.BlockSpec((1, tm, w), lambda b, i: (b, i, 0))
    vec = pl.BlockSpec((1, 1, D), lambda b, i: (b, 0, 0))
    return pl.pallas_call(
        _inproj_body,
        grid=(B, L // tm),
        in_specs=[row(D), vec, vec, _full((D, n))],
        out_specs=[row(w) for w in INPROJ_WIDTHS],
        out_shape=[jax.ShapeDtypeStruct((B, L, w), F32) for w in INPROJ_WIDTHS],
        compiler_params=_cp("parallel", "parallel"),
        name="inproj",
    )(x, shift, scale, w_arr)


def _log_sigmoid(z):
    return jnp.minimum(z, 0.0) - jnp.log1p(jnp.exp(-jnp.abs(z)))


def _gla_dir(qk, v, la, s_ref, d, tri):
    C = qk.shape[0]
    KD = GLA_HEADS * GLA_DK
    VD = GLA_HEADS * GLA_DV
    q = qk[:, :KD] * (GLA_DK ** -0.5)
    k = qk[:, KD:]
    b = jnp.dot(tri, la, precision=HI, preferred_element_type=F32)
    tot = jnp.sum(la, axis=0, keepdims=True)
    qe = q * jnp.exp(b)
    ke = (k * jnp.exp(-b)).astype(BF16)
    kl = (k * jnp.exp(tot - b)).astype(BF16)
    s_old = s_ref[d]
    vb = v.astype(BF16)
    o = jnp.dot(qe.astype(BF16), s_old.astype(BF16), preferred_element_type=F32)
    lane_k = _idiv(lax.broadcasted_iota(jnp.int32, (1, KD), 1), GLA_DK)
    lane_v = _idiv(lax.broadcasted_iota(jnp.int32, (1, VD), 1), GLA_DV)
    for h in range(GLA_HEADS):
        qh = jnp.where(lane_k == h, qe, 0.0).astype(BF16)
        att = lax.dot_general(qh, ke, (((1,), (1,)), ((), ())), preferred_element_type=F32)
        att = (att * tri).astype(BF16)
        oh = jnp.dot(att, vb, preferred_element_type=F32)
        o = o + jnp.where(lane_v == h, oh, 0.0)
    ones = jnp.ones((C, VD), F32)
    tot_b = lax.dot_general(la, ones, (((0,), (0,)), ((), ())), precision=HI, preferred_element_type=F32)
    kv = lax.dot_general(kl, vb, (((0,), (0,)), ((), ())), preferred_element_type=F32)
    rk = _idiv(lax.broadcasted_iota(jnp.int32, (KD, VD), 0), GLA_DK)
    cv = _idiv(lax.broadcasted_iota(jnp.int32, (KD, VD), 1), GLA_DV)
    s_ref[d] = jnp.exp(tot_b) * s_old + jnp.where(rk == cv, kv, 0.0)
    return o


def _gla_body(qkf_ref, vf_ref, af_ref, qkb_ref, vb_ref, ab_ref, wg_ref, bg_ref, s0_ref,
              of_ref, ob_ref, sout_ref, s_ref):
    i = pl.program_id(1)
    C = qkf_ref.shape[1]
    KD = GLA_HEADS * GLA_DK

    @pl.when(i == 0)
    def _():
        s_ref[...] = s0_ref[0]

    r = lax.broadcasted_iota(jnp.int32, (C, C), 0)
    c = lax.broadcasted_iota(jnp.int32, (C, C), 1)
    tri_lo = (c <= r).astype(F32)
    tri_up = (c >= r).astype(F32)
    zf = jnp.dot(af_ref[0], wg_ref[...], precision=HI, preferred_element_type=F32) + bg_ref[...]
    zb = jnp.dot(ab_ref[0], wg_ref[...], precision=HI, preferred_element_type=F32) + bg_ref[...]
    la_f = _log_sigmoid(zf[:, :KD]) / GLA_TAU
    la_b = _log_sigmoid(zb[:, KD:]) / GLA_TAU
    of_ref[0] = _gla_dir(qkf_ref[0], vf_ref[0], la_f, s_ref, 0, tri_lo)
    ob_ref[0] = _gla_dir(qkb_ref[0], vb_ref[0], la_b, s_ref, 1, tri_up)

    @pl.when(i == pl.num_programs(1) - 1)
    def _():
        sout_ref[0] = s_ref[...]


def _gla(qk, vg, alr, wg, bg, s0):
    B, L, _ = qk.shape
    C = min(GLA_CHUNK, L)
    n = L // C
    KD, VD = GLA_HEADS * GLA_DK, GLA_HEADS * GLA_DV
    fwd = lambda w: pl.BlockSpec((1, C, w), lambda b, i: (b, i, 0))
    bwd = lambda w: pl.BlockSpec((1, C, w), lambda b, i: (b, n - 1 - i, 0))
    st = pl.BlockSpec((1, 2, KD, VD), lambda b, i: (b, 0, 0, 0))
    return pl.pallas_call(
        _gla_body,
        grid=(B, n),
        in_specs=[fwd(2 * KD), fwd(VD), fwd(LANES), bwd(2 * KD), bwd(VD), bwd(LANES),
                  _full((LANES, 2 * KD)), _full((1, 2 * KD)), st],
        out_specs=[fwd(VD), bwd(VD), st],
        out_shape=[jax.ShapeDtypeStruct((B, L, VD), F32), jax.ShapeDtypeStruct((B, L, VD), F32),
                   jax.ShapeDtypeStruct((B, 2, KD, VD), F32)],
        scratch_shapes=[pltpu.VMEM((2, KD, VD), F32)],
        compiler_params=_cp("parallel", "arbitrary"),
        name="gla",
    )(qk, vg, alr, qk, vg, alr, wg, bg, s0)


def _arrange_gate(w_gate, b_gate):
    KD = GLA_HEADS * GLA_DK
    wg = jnp.zeros((LANES, 2 * KD), F32)
    wg = wg.at[:GLA_RANK, :KD].set(w_gate[0]).at[GLA_RANK:2 * GLA_RANK, KD:].set(w_gate[1])
    return wg, jnp.concatenate([b_gate[0], b_gate[1]])[None, :]


def _shortconv_body(x_ref, p_ref, n_ref, w_ref, b_ref, v_ref, x1_ref, x2_ref):
    i = pl.program_id(1)
    last = pl.num_programs(1) - 1
    x = x_ref[0]
    tm = x.shape[0]
    prev = jnp.where(i > 0, p_ref[0][7:8, :], 0.0)
    nxt = jnp.where(i < last, n_ref[0][0:1, :], 0.0)
    rid = lax.broadcasted_iota(jnp.int32, x.shape, 0)
    dn = jnp.where(rid == 0, prev, pltpu.roll(x, 1, 0))
    up = jnp.where(rid == tm - 1, nxt, pltpu.roll(x, tm - 1, 0))
    w = w_ref[...]
    y = b_ref[...] + dn * w[0:1] + x * w[1:2] + up * w[2:3]
    v_ref[0] = y[:, :HY_CH]
    x1_ref[0] = y[:, HY_CH:2 * HY_CH]
    x2_ref[0] = y[:, 2 * HY_CH:]


def _shortconv(u, w, b):
    B, L, W = u.shape
    tm = min(512, L)
    nb = tm // 8
    row = pl.BlockSpec((1, tm, W), lambda b_, i: (b_, i, 0))
    prev = pl.BlockSpec((1, 8, W), lambda b_, i: (b_, jnp.maximum(i * nb - 1, 0), 0))
    nxt = pl.BlockSpec((1, 8, W), lambda b_, i: (b_, jnp.minimum((i + 1) * nb, L // 8 - 1), 0))
    o = pl.BlockSpec((1, tm, HY_CH), lambda b_, i: (b_, i, 0))
    return pl.pallas_call(
        _shortconv_body,
        grid=(B, L // tm),
        in_specs=[row, prev, nxt, _full((3, W)), _full((1, W))],
        out_specs=[o, o, o],
        out_shape=[jax.ShapeDtypeStruct((B, L, HY_CH), F32)] * 3,
        compiler_params=_cp("parallel", "parallel"),
        name="shortconv",
    )(u, u, u, w, b[None, :])


def _filter_feats(L):
    pos = jnp.arange(L, dtype=F32)
    t = pos / (L - 1)
    bands = (HY_EMB - 1) // 2
    freqs = jnp.linspace(1e-4, bands - 1, bands, dtype=F32)
    ang = (2.0 * math.pi * pos / L)[:, None] * freqs
    z = jnp.concatenate([t[:, None], jnp.cos(ang), -jnp.sin(ang)], axis=-1)
    z = jnp.pad(z, ((0, 0), (0, LANES - HY_EMB)))
    deltas = jnp.abs(jnp.linspace(math.log(HY_DECAY_TARGET) / HY_SLOW_DECAY,
                                  math.log(HY_DECAY_TARGET) / HY_FAST_DECAY, HY_CH, dtype=F32))
    return z, jnp.tile(deltas, 4)[None, :]


def _filter_body(z_ref, w1_ref, b1_ref, f1_ref, w2_ref, b2_ref, f2_ref, w3_ref, b3_ref, dl_ref,
                 h_ref, ss_ref, *, L):
    i = pl.program_id(0)
    z = z_ref[...]
    tm = z.shape[0]
    hid = jnp.sin(f1_ref[...] * (jnp.dot(z, w1_ref[...], precision=HI, preferred_element_type=F32) + b1_ref[...]))
    hid = jnp.sin(f2_ref[...] * (jnp.dot(hid, w2_ref[...], precision=HI, preferred_element_type=F32) + b2_ref[...]))
    h = jnp.dot(hid, w3_ref[...], precision=HI, preferred_element_type=F32) + b3_ref[...]
    pos = (lax.broadcasted_iota(jnp.int32, (tm, 1), 0) + i * tm).astype(F32)
    t = pos / (L - 1)
    h = h * jnp.exp(-t * dl_ref[...])

    @pl.when(i == 0)
    def _():
        ss_ref[...] = jnp.zeros_like(ss_ref)

    ss_ref[...] += jnp.sum(h * h, axis=0, keepdims=True)
    col = lax.broadcasted_iota(jnp.int32, h.shape, 1)
    is_bwd = (_idiv(col, HY_CH) & 1) == 1
    h_ref[...] = jnp.where(jnp.logical_and(is_bwd, pos == 0.0), 0.0, h)


def _filters(L, fw1, fb1, ff1, fw2, fb2, ff2, fw3, fb3):
    z, dl = _filter_feats(L)
    tm = min(1024, L)
    Hf = fw2.shape[0]
    w1 = jnp.pad(fw1, ((0, LANES - HY_EMB), (0, 0)))
    NC = fw3.shape[1]
    return pl.pallas_call(
        functools.partial(_filter_body, L=L),
        grid=(L // tm,),
        in_specs=[pl.BlockSpec((tm, LANES), lambda i: (i, 0)), _full((LANES, Hf)), _full((1, Hf)), _full((1, Hf)),
                  _full((Hf, Hf)), _full((1, Hf)), _full((1, Hf)), _full((Hf, NC)), _full((1, NC)), _full((1, NC))],
        out_specs=[pl.BlockSpec((tm, NC), lambda i: (i, 0)), _full((1, NC))],
        out_shape=[jax.ShapeDtypeStruct((L, NC), F32), jax.ShapeDtypeStruct((1, NC), F32)],
        compiler_params=_cp("arbitrary"),
        name="hy_filters",
    )(z, w1, fb1[None], ff1[None], fw2, fb2[None], ff2[None], fw3, fb3[None], dl)


def _dft_consts(L):
    N = 2 * L
    N2 = DFT_N2
    N1 = N // N2
    half = N1 // 2
    k1 = np.arange(N1)[:, None].astype(np.float64)
    n1 = np.arange(N1)[None, :].astype(np.float64)
    a1 = 2.0 * np.pi * k1 * n1 / N1
    f1r, f1i = np.cos(a1), -np.sin(a1)
    fa = np.concatenate([f1r[:, :half], f1i[:, :half]], axis=0)
    fb = np.concatenate([f1r[:half, :], f1i[:half, :]], axis=1) / N
    k2 = np.arange(N2)[:, None].astype(np.float64)
    n2 = np.arange(N2)[None, :].astype(np.float64)
    a2 = 2.0 * np.pi * k2 * n2 / N2
    f2r, f2i = np.cos(a2), -np.sin(a2)
    g = np.block([[f2r, -f2i], [f2i, f2r]])
    gc = np.block([[f2r, f2i], [-f2i, f2r]])
    at = 2.0 * np.pi * (np.arange(N1)[:, None] * np.arange(N2)[None, :] % N) / N
    twr, twi = np.cos(at), -np.sin(at)
    c = lambda a: jnp.asarray(a, dtype=F32)
    bc = lambda a: jnp.broadcast_to(c(a)[:, :, None], (N1, N2, LANES))
    return dict(N1=N1, N2=N2, half=half, fa=c(fa), fb=c(fb), g=c(g), gc=c(gc), twr=bc(twr), twi=bc(twi))


def _lanes(t, width):
    return jnp.concatenate([t] * (width // LANES), axis=-1)


def _dft1_body(f_ref, x_ref, o_ref):
    o_ref[0] = jnp.dot(f_ref[...], x_ref[0].astype(BF16), preferred_element_type=F32)


def _dft_stage1(fa, x):
    B, half, M = x.shape
    R = fa.shape[0]
    tn = min(4096, M)
    return pl.pallas_call(
        _dft1_body,
        grid=(B, M // tn),
        in_specs=[_full((R, half)), pl.BlockSpec((1, half, tn), lambda b, j: (b, 0, j))],
        out_specs=pl.BlockSpec((1, R, tn), lambda b, j: (b, 0, j)),
        out_shape=jax.ShapeDtypeStruct((B, R, M), F32),
        compiler_params=_cp("parallel", "parallel"),
        name="hy_dft1",
    )(fa.astype(BF16), x)


def _filter_spec_body(a_ref, twr_ref, twi_ref, g_ref, ss_ref, hf_ref):
    W = a_ref.shape[-1]
    ar, ai = a_ref[0, 0], a_ref[1, 0]
    twr, twi = _lanes(twr_ref[0], W), _lanes(twi_ref[0], W)
    xr = ar * twr - ai * twi
    xi = ar * twi + ai * twr
    z = jnp.dot(g_ref[...], jnp.concatenate([xr, xi], axis=0).astype(BF16), preferred_element_type=F32)
    n2 = z.shape[0] // 2
    zr, zi = z[:n2], z[n2:]
    ss = ss_ref[...]
    for o in range(2):
        f0, b0 = (2 * o) * HY_CH, (2 * o + 1) * HY_CH
        sc = lax.rsqrt(ss[:, f0:f0 + HY_CH] + ss[:, b0:b0 + HY_CH] + 1e-6)
        hf_ref[o, 0, 0] = (zr[:, f0:f0 + HY_CH] + zr[:, b0:b0 + HY_CH]) * sc
        hf_ref[o, 0, 1] = (zi[:, f0:f0 + HY_CH] - zi[:, b0:b0 + HY_CH]) * sc


def _filter_spectrum(h, ss, dc):
    L, NC = h.shape
    N1, N2, half = dc["N1"], dc["N2"], dc["half"]
    a = _dft_stage1(dc["fa"], h.reshape(1, half, N2 * NC))
    a = a.reshape(2, N1, N2, NC)
    return pl.pallas_call(
        _filter_spec_body,
        grid=(N1,),
        in_specs=[pl.BlockSpec((2, 1, N2, NC), lambda k: (0, k, 0, 0)),
                  pl.BlockSpec((1, N2, LANES), lambda k: (k, 0, 0)), pl.BlockSpec((1, N2, LANES), lambda k: (k, 0, 0)),
                  _full((2 * N2, 2 * N2)), _full((1, NC))],
        out_specs=pl.BlockSpec((2, 1, 2, N2, HY_CH), lambda k: (0, k, 0, 0, 0)),
        out_shape=jax.ShapeDtypeStruct((2, N1, 2, N2, HY_CH), F32),
        compiler_params=_cp("parallel"),
        name="hy_filter_spec",
    )(a, dc["twr"], dc["twi"], dc["g"].astype(BF16), ss)


def _spec_mul_body(a_ref, twr_ref, twi_ref, g_ref, gc_ref, hf_ref, o_ref):
    W = a_ref.shape[-1]
    ar, ai = a_ref[0, 0, 0], a_ref[0, 1, 0]
    twr, twi = _lanes(twr_ref[0], W), _lanes(twi_ref[0], W)
    xr = ar * twr - ai * twi
    xi = ar * twi + ai * twr
    z = jnp.dot(g_ref[...], jnp.concatenate([xr, xi], axis=0).astype(BF16), preferred_element_type=F32)
    n2 = z.shape[0] // 2
    zr, zi = z[:n2], z[n2:]
    hr, hi = hf_ref[0, 0], hf_ref[0, 1]
    yr = zr * hr - zi * hi
    yi = zr * hi + zi * hr
    b = jnp.dot(gc_ref[...], jnp.concatenate([yr, yi], axis=0).astype(BF16), preferred_element_type=F32)
    br, bi = b[:n2], b[n2:]
    o_ref[0, 0, 0] = br * twr + bi * twi
    o_ref[0, 1, 0] = bi * twr - br * twi


def _spec_mul(a, hf, dc):
    B = a.shape[0]
    N1, N2 = dc["N1"], dc["N2"]
    C = a.shape[-1]
    blk = pl.BlockSpec((1, 2, 1, N2, C), lambda b, k: (b, 0, k, 0, 0))
    tw = pl.BlockSpec((1, N2, LANES), lambda b, k: (k, 0, 0))
    return pl.pallas_call(
        _spec_mul_body,
        grid=(B, N1),
        in_specs=[blk, tw, tw, _full((2 * N2, 2 * N2)), _full((2 * N2, 2 * N2)),
                  pl.BlockSpec((1, 2, N2, C), lambda b, k: (k, 0, 0, 0))],
        out_specs=blk,
        out_shape=jax.ShapeDtypeStruct(a.shape, F32),
        compiler_params=_cp("parallel", "parallel"),
        name="hy_spec_mul",
    )(a, dc["twr"], dc["twi"], dc["g"].astype(BF16), dc["gc"].astype(BF16), hf)


def _dft3_body(f_ref, b_ref, u_ref, gate_ref, skip_ref, o_ref):
    y = jnp.dot(f_ref[...], b_ref[0].astype(BF16), preferred_element_type=F32)
    o_ref[0] = gate_ref[0] * (y + u_ref[0] * skip_ref[...])


def _dft_stage3(fb, bm, u, gate, skip_t):
    B, R, M = bm.shape
    half = fb.shape[0]
    tn = skip_t.shape[1]
    row = pl.BlockSpec((1, half, tn), lambda b, j: (b, 0, j))
    return pl.pallas_call(
        _dft3_body,
        grid=(B, M // tn),
        in_specs=[_full((half, R)), pl.BlockSpec((1, R, tn), lambda b, j: (b, 0, j)), row, row, _full((1, tn))],
        out_specs=row,
        out_shape=jax.ShapeDtypeStruct((B, half, M), F32),
        compiler_params=_cp("parallel", "parallel"),
        name="hy_dft3",
    )(fb.astype(BF16), bm, u, gate, skip_t)


def _longconv_gated(u, gate, hf, skip, dc):
    B, L, C = u.shape
    N1, N2, half = dc["N1"], dc["N2"], dc["half"]
    M = N2 * C
    uv = u.reshape(B, half, M)
    a = _dft_stage1(dc["fa"], uv).reshape(B, 2, N1, N2, C)
    bm = _spec_mul(a, hf, dc).reshape(B, 2 * N1, M)
    tn = min(4096, M)
    skip_t = jnp.tile(skip, tn // C)[None, :]
    return _dft_stage3(dc["fb"], bm, uv, gate.reshape(B, half, M), skip_t).reshape(B, L, C)


def _hyena(hyu, conv_w, conv_b, filt, skip):
    B, L, _ = hyu.shape
    v, x1, x2 = _shortconv(hyu, conv_w, conv_b)
    h, ss = _filters(L, *filt)
    dc = _dft_consts(L)
    hf = _filter_spectrum(h, ss, dc)
    z1 = _longconv_gated(v, x1, hf[0], skip[0], dc)
    return _longconv_gated(z1, x2, hf[1], skip[1], dc)


def _hyena_ctx_body(v_ref, x1_ref, x2_ref, h_ref, ss_ref, skip_ref, fc_ref, gc_ref, o_ref):
    fc, gc = fc_ref[...], gc_ref[...]
    n = fc.shape[0] // 2
    ss = ss_ref[...]
    h = h_ref[...]

    def conv(u, o):
        f0, b0 = (2 * o) * HY_CH, (2 * o + 1) * HY_CH
        sc = lax.rsqrt(ss[:, f0:f0 + HY_CH] + ss[:, b0:b0 + HY_CH] + 1e-6)
        x = jnp.dot(fc, u, precision=HI, preferred_element_type=F32)
        hf = jnp.dot(fc, h[:, f0:f0 + HY_CH], precision=HI, preferred_element_type=F32)
        hb = jnp.dot(fc, h[:, b0:b0 + HY_CH], precision=HI, preferred_element_type=F32)
        hr = (hf[:n] + hb[:n]) * sc
        hi = (hf[n:] - hb[n:]) * sc
        yr = x[:n] * hr - x[n:] * hi
        yi = x[:n] * hi + x[n:] * hr
        y = jnp.dot(gc, jnp.concatenate([yr, yi], axis=0), precision=HI, preferred_element_type=F32)
        return y + u * skip_ref[o:o + 1, :]

    z1 = x1_ref[0] * conv(v_ref[0], 0)
    o_ref[0] = x2_ref[0] * conv(z1, 1)


def _hyena_ctx(hyu, conv_w, conv_b, filt, skip):
    B, L, _ = hyu.shape
    v, x1, x2 = _shortconv(hyu, conv_w, conv_b)
    h, ss = _filters(L, *filt)
    N = 2 * L
    ang = 2.0 * np.pi * (np.arange(N)[:, None] * np.arange(L)[None, :] % N) / N
    fr, fi = np.cos(ang), -np.sin(ang)
    fc = jnp.asarray(np.concatenate([fr, fi], axis=0), dtype=F32)
    gc = jnp.asarray(np.concatenate([fr.T, fi.T], axis=1) / N, dtype=F32)
    row = pl.BlockSpec((1, L, HY_CH), lambda b: (b, 0, 0))
    return pl.pallas_call(
        _hyena_ctx_body,
        grid=(B,),
        in_specs=[row, row, row, _full(h.shape), _full(ss.shape), _full(skip.shape), _full(fc.shape), _full(gc.shape)],
        out_specs=row,
        out_shape=jax.ShapeDtypeStruct((B, L, HY_CH), F32),
        compiler_params=_cp("parallel"),
        name="hyena_ctx",
    )(v, x1, x2, h, ss, skip, fc, gc)


HEAD_PAD = 128


def _rope_swap(w):
    a, b, c, d = w[..., 0:8], w[..., 8:16], w[..., 16:24], w[..., 24:32]
    return jnp.concatenate([-b, a, -d, c], axis=-1)


def _arrange_wq(w_uq):
    R = w_uq.shape[0]
    w = w_uq.reshape(R, MLA_HEADS, MLA_NOPE + MLA_ROPE)
    rope = w[..., MLA_NOPE:]
    out = jnp.concatenate([w[..., :MLA_NOPE], rope, _rope_swap(rope)], axis=-1)
    return out.reshape(R, MLA_HEADS * HEAD_PAD).astype(BF16)


def _arrange_wkv(w_ukv):
    R = w_ukv.shape[0]
    w = w_ukv.reshape(R, MLA_HEADS, MLA_NOPE + MLA_V)
    wk = jnp.concatenate([w[..., :MLA_NOPE], jnp.zeros((R, MLA_HEADS, HEAD_PAD - MLA_NOPE), w.dtype)], axis=-1)
    wv = w[..., MLA_NOPE:]
    return wk.reshape(R, MLA_HEADS * HEAD_PAD).astype(BF16), wv.reshape(R, MLA_HEADS * MLA_V).astype(BF16)


def _kr_place():
    e = np.zeros((LANES, MLA_HEADS * HEAD_PAD), np.float32)
    es = np.zeros((LANES, MLA_HEADS * HEAD_PAD), np.float32)
    for h in range(MLA_HEADS):
        base = h * HEAD_PAD + MLA_NOPE
        for j in range(MLA_ROPE):
            e[j, base + j] = 1.0
            blk, r = divmod(j, 16)
            if r < 8:
                es[16 * blk + r + 8, base + j] = -1.0
            else:
                es[16 * blk + r - 8, base + j] = 1.0
    return jnp.asarray(e).astype(BF16), jnp.asarray(es).astype(BF16)


def _rope_tables(L, rope):
    if rope:
        t = np.arange(L)
        row, col = (t // GRID_W).astype(np.float32), (t % GRID_W).astype(np.float32)
        half = MLA_ROPE // 2
        inv = ROPE_BASE ** (-jnp.arange(0, half, 2, dtype=F32) / half)
        ar = jnp.asarray(row)[:, None] * inv
        ac = jnp.asarray(col)[:, None] * inv
        cos = jnp.concatenate([jnp.cos(ar), jnp.cos(ar), jnp.cos(ac), jnp.cos(ac)], axis=-1)
        sin = jnp.concatenate([jnp.sin(ar), jnp.sin(ar), jnp.sin(ac), jnp.sin(ac)], axis=-1)
    else:
        cos, sin = jnp.ones((L, MLA_ROPE), F32), jnp.zeros((L, MLA_ROPE), F32)
    return cos, sin


def _rms_rows(x, g, eps=1e-6):
    return x * lax.rsqrt(jnp.mean(x * x, axis=-1, keepdims=True) + eps) * g


def _qproj_body(cq_ref, g_ref, w_ref, t1_ref, t2_ref, q_ref):
    xn = _rms_rows(cq_ref[0], g_ref[...])
    acc = jnp.dot(xn.astype(BF16), w_ref[...], preferred_element_type=F32)
    W = acc.shape[1]
    t1, t2 = _lanes(t1_ref[...], W), _lanes(t2_ref[...], W)
    q_ref[0] = (acc * t1 + pltpu.roll(acc, W - MLA_ROPE, 1) * t2).astype(q_ref.dtype)


def _qproj(cq, g, wq, cos, sin):
    B, L, R = cq.shape
    tm = min(512, L)
    W = wq.shape[1]
    ones, zeros = jnp.ones((L, MLA_NOPE), F32), jnp.zeros((L, MLA_ROPE), F32)
    qs = MLA_SCALE * math.log2(math.e)
    t1 = jnp.concatenate([ones, cos, zeros], axis=-1) * qs
    t2 = jnp.concatenate([jnp.zeros((L, MLA_NOPE), F32), sin, zeros], axis=-1) * qs
    tab = pl.BlockSpec((tm, HEAD_PAD), lambda b, i: (i, 0))
    return pl.pallas_call(
        _qproj_body,
        grid=(B, L // tm),
        in_specs=[pl.BlockSpec((1, tm, R), lambda b, i: (b, i, 0)), _full((1, R)), _full((R, W)), tab, tab],
        out_specs=pl.BlockSpec((1, tm, W), lambda b, i: (b, i, 0)),
        out_shape=jax.ShapeDtypeStruct((B, L, W), BF16),
        compiler_params=_cp("parallel", "parallel"),
        name="mla_qproj",
    )(cq, g[None, :], wq, t1, t2)


def _kvproj_body(c_ref, g_ref, wk_ref, wv_ref, e_ref, es_ref, cos_ref, sin_ref, k_ref, v_ref):
    c = c_ref[0]
    R = MLA_KV_RANK
    xn = _rms_rows(c[:, :R], g_ref[...]).astype(BF16)
    kr = c[:, R:]
    acc = jnp.dot(xn, wk_ref[...], preferred_element_type=F32)
    acc += jnp.dot((kr * cos_ref[...]).astype(BF16), e_ref[...], preferred_element_type=F32)
    acc += jnp.dot((kr * sin_ref[...]).astype(BF16), es_ref[...], preferred_element_type=F32)
    k_ref[0] = acc.astype(k_ref.dtype)
    v_ref[0] = jnp.dot(xn, wv_ref[...], preferred_element_type=F32).astype(v_ref.dtype)


def _kvproj(ckvr, g, wk, wv, cos, sin):
    B, L, Wc = ckvr.shape
    tm = min(512, L)
    pad = jnp.zeros((L, LANES - MLA_ROPE), F32)
    cos_p, sin_p = jnp.concatenate([cos, pad], axis=-1), jnp.concatenate([sin, pad], axis=-1)
    e, es = _kr_place()
    tab = pl.BlockSpec((tm, LANES), lambda b, i: (i, 0))
    Wk, Wv = wk.shape[1], wv.shape[1]
    return pl.pallas_call(
        _kvproj_body,
        grid=(B, L // tm),
        in_specs=[pl.BlockSpec((1, tm, Wc), lambda b, i: (b, i, 0)), _full((1, MLA_KV_RANK)),
                  _full(wk.shape), _full(wv.shape), _full(e.shape), _full(es.shape), tab, tab],
        out_specs=[pl.BlockSpec((1, tm, Wk), lambda b, i: (b, i, 0)), pl.BlockSpec((1, tm, Wv), lambda b, i: (b, i, 0))],
        out_shape=[jax.ShapeDtypeStruct((B, L, Wk), BF16), jax.ShapeDtypeStruct((B, L, Wv), BF16)],
        compiler_params=_cp("parallel", "parallel"),
        name="mla_kvproj",
    )(ckvr, g[None, :], wk, wv, e, es, cos_p, sin_p)


FLASH_ROWS = 256
FLASH_KEYS = 256


def _flash_body(q_ref, k_ref, v_ref, o_ref, m_ref, l_ref, acc_ref, s_ref, *, R):
    j = pl.program_id(3)
    tq, tk = q_ref.shape[1], k_ref.shape[1]
    CK = FLASH_KEYS
    npc = CK // LANES

    @pl.when(j == 0)
    def _():
        m_ref[...] = jnp.full_like(m_ref, -jnp.inf)
        l_ref[...] = jnp.zeros_like(l_ref)
        acc_ref[...] = jnp.zeros_like(acc_ref)

    def pass1(a, r):
        lo, r0 = a * HEAD_PAD, r * R
        q = q_ref[0, r0:r0 + R, lo:lo + HEAD_PAD]
        mp = None
        for c in range(tk // CK):
            kc = k_ref[0, c * CK:(c + 1) * CK, lo:lo + HEAD_PAD]
            s = lax.dot_general(q, kc, (((1,), (1,)), ((), ())), preferred_element_type=F32)
            s_ref[r0:r0 + R, c * CK:(c + 1) * CK] = s
            for w in range(npc):
                pc = s[:, w * LANES:(w + 1) * LANES]
                mp = pc if mp is None else jnp.maximum(mp, pc)
        m_old = m_ref[a, r0:r0 + R, :]
        return m_old, jnp.maximum(m_old, jnp.max(mp, axis=1, keepdims=True))

    def pass2(a, r, m_old, m_new):
        r0 = r * R
        alpha = jnp.exp2(m_old - m_new)
        lp = jnp.zeros((R, LANES), F32)
        pv = jnp.zeros((R, 2 * MLA_V), F32)
        for c in range(tk // CK):
            s = s_ref[r0:r0 + R, c * CK:(c + 1) * CK]
            ps = [jnp.exp2(s[:, w * LANES:(w + 1) * LANES] - m_new) for w in range(npc)]
            for p_ in ps:
                lp = lp + p_
            p = jnp.concatenate(ps, axis=1).astype(BF16)
            pv = pv + jnp.dot(p, v_ref[0, c * CK:(c + 1) * CK, :], preferred_element_type=F32)
        l_ref[a, r0:r0 + R, :] = alpha * l_ref[a, r0:r0 + R, :] + jnp.sum(lp, axis=1, keepdims=True)
        acc_ref[a, r0:r0 + R, :] = alpha * acc_ref[a, r0:r0 + R, :] + pv
        m_ref[a, r0:r0 + R, :] = m_new

    blocks = [(a, r) for a in range(2) for r in range(tq // R)]
    pend = pass1(*blocks[0])
    for i, blk in enumerate(blocks):
        nxt = pass1(*blocks[i + 1]) if i + 1 < len(blocks) else None
        pass2(*blk, *pend)
        pend = nxt

    @pl.when(j == pl.num_programs(3) - 1)
    def _():
        lane = lax.broadcasted_iota(jnp.int32, acc_ref.shape[1:], 1)
        o_ref[0] = jnp.where(lane < MLA_V, acc_ref[0] / l_ref[0], acc_ref[1] / l_ref[1])


def _flash_tiles(Lq, Lk):
    tq = min(1024, Lq)
    tk = Lk
    for cand in (3328, 1280, 1024, 512, 256):
        if Lk % cand == 0:
            tk = cand
            break
    return tq, tk


def _flash(q, k, v):
    B, Lq, _ = q.shape
    Lk = k.shape[1]
    tq, tk = _flash_tiles(Lq, Lk)
    hp = MLA_HEADS // 2
    return pl.pallas_call(
        functools.partial(_flash_body, R=min(FLASH_ROWS, tq)),
        grid=(B, hp, Lq // tq, Lk // tk),
        in_specs=[pl.BlockSpec((1, tq, 2 * HEAD_PAD), lambda b, h, i, j: (b, i, h)),
                  pl.BlockSpec((1, tk, 2 * HEAD_PAD), lambda b, h, i, j: (b, j, h)),
                  pl.BlockSpec((1, tk, 2 * MLA_V), lambda b, h, i, j: (b, j, h))],
        out_specs=pl.BlockSpec((1, tq, 2 * MLA_V), lambda b, h, i, j: (b, i, h)),
        out_shape=jax.ShapeDtypeStruct((B, Lq, MLA_HEADS * MLA_V), F32),
        scratch_shapes=[pltpu.VMEM((2, tq, LANES), F32), pltpu.VMEM((2, tq, LANES), F32),
                        pltpu.VMEM((2, tq, 2 * MLA_V), F32), pltpu.VMEM((tq, tk), F32)],
        compiler_params=_cp("parallel", "parallel", "parallel", "arbitrary"),
        name="mla_flash",
    )(q, k, v)


def _layernorm_rows(x, g, b, eps=1e-5):
    mu = jnp.mean(x, axis=-1, keepdims=True)
    xc = x - mu
    var = jnp.mean(xc * xc, axis=-1, keepdims=True)
    return xc * lax.rsqrt(var + eps) * g + b


def _outproj_body(of_ref, ob_ref, g_ref, hy_ref, om_ref, x_ref, gate_ref, gg_ref, hg_ref, mg_ref,
                  w_ref, lg_ref, lb_ref, o_ref, *, alpha):
    VD = GLA_HEADS * GLA_DV
    o = of_ref[0] + ob_ref[0]
    r = _idiv(lax.broadcasted_iota(jnp.int32, (VD, VD), 0), GLA_DV)
    c = _idiv(lax.broadcasted_iota(jnp.int32, (VD, VD), 1), GLA_DV)
    grp = (r == c).astype(F32)
    ms = jnp.dot(o * o, grp, precision=HI, preferred_element_type=F32) * (1.0 / GLA_DV)
    g = g_ref[0]
    ya = o * lax.rsqrt(ms + 1e-6) * gg_ref[...] * (g * jax.nn.sigmoid(g))
    yb = _rms_rows(hy_ref[0], hg_ref[...])
    yc = _rms_rows(om_ref[0], mg_ref[...])
    acc = jnp.dot(ya.astype(BF16), w_ref[0:VD, :], preferred_element_type=F32)
    acc += jnp.dot(yb.astype(BF16), w_ref[VD:VD + HY_CH, :], preferred_element_type=F32)
    acc += jnp.dot(yc.astype(BF16), w_ref[VD + HY_CH:, :], preferred_element_type=F32)
    o_ref[0] = _layernorm_rows(alpha * x_ref[0] + gate_ref[0] * acc, lg_ref[...], lb_ref[...])


def _outproj(of, ob, vg, hy, om, x, gate, gla_g, hy_g, mla_g, w_out, ln_g, ln_b, alpha):
    B, L, D = x.shape
    tm = min(512, L)
    VD = GLA_HEADS * GLA_DV
    MD = MLA_HEADS * MLA_V
    row = lambda w: pl.BlockSpec((1, tm, w), lambda b, i: (b, i, 0))
    return pl.pallas_call(
        functools.partial(_outproj_body, alpha=alpha),
        grid=(B, L // tm),
        in_specs=[row(VD), row(VD), pl.BlockSpec((1, tm, VD), lambda b, i: (b, i, 1)), row(HY_CH), row(MD), row(D),
                  pl.BlockSpec((1, 1, D), lambda b, i: (b, 0, 0)), _full((1, VD)), _full((1, HY_CH)), _full((1, MD)),
                  _full(w_out.shape), _full((1, D)), _full((1, D))],
        out_specs=row(D),
        out_shape=jax.ShapeDtypeStruct((B, L, D), F32),
        compiler_params=_cp("parallel", "parallel"),
        name="outproj",
    )(of, ob, vg, hy, om, x, gate, jnp.tile(gla_g, GLA_HEADS)[None, :], hy_g[None, :], mla_g[None, :],
      w_out.astype(BF16), ln_g[None, :], ln_b[None, :])


def _ffn_body(x_ref, sh_ref, sc_ref, gate_ref, w1_ref, w3_ref, w2_ref, lg_ref, lb_ref, o_ref, h_ref, acc_ref, *, alpha):
    j = pl.program_id(2)

    @pl.when(j == 0)
    def _():
        h_ref[...] = (x_ref[0] * (1.0 + sc_ref[0]) + sh_ref[0]).astype(BF16)
        acc_ref[...] = jnp.zeros_like(acc_ref)

    h = h_ref[...]
    a = jnp.dot(h, w1_ref[...], preferred_element_type=F32)
    b = jnp.dot(h, w3_ref[...], preferred_element_type=F32)
    t = (a * jax.nn.sigmoid(a) * b).astype(BF16)
    acc_ref[...] += jnp.dot(t, w2_ref[...], preferred_element_type=F32)

    @pl.when(j == pl.num_programs(2) - 1)
    def _():
        o_ref[0] = _layernorm_rows(alpha * x_ref[0] + gate_ref[0] * acc_ref[...], lg_ref[...], lb_ref[...])


def _ffn_tile(F):
    for cand in (512, 256, 128):
        if F % cand == 0:
            return cand
    return F


def _ffn(x, shift, scale, gate, w1, w3, w2, ln_g, ln_b, alpha):
    B, L, D = x.shape
    F = w1.shape[1]
    tm = min(1024, L)
    tf = _ffn_tile(F)
    row = pl.BlockSpec((1, tm, D), lambda b, i, j: (b, i, 0))
    vec = pl.BlockSpec((1, 1, D), lambda b, i, j: (b, 0, 0))
    return pl.pallas_call(
        functools.partial(_ffn_body, alpha=alpha),
        grid=(B, L // tm, F // tf),
        in_specs=[row, vec, vec, vec,
                  pl.BlockSpec((D, tf), lambda b, i, j: (0, j)), pl.BlockSpec((D, tf), lambda b, i, j: (0, j)),
                  pl.BlockSpec((tf, D), lambda b, i, j: (j, 0)), _full((1, D)), _full((1, D))],
        out_specs=row,
        out_shape=jax.ShapeDtypeStruct((B, L, D), F32),
        scratch_shapes=[pltpu.VMEM((tm, D), BF16), pltpu.VMEM((tm, D), F32)],
        compiler_params=_cp("parallel", "parallel", "arbitrary"),
        name="ffn",
    )(x, shift, scale, gate, w1.astype(BF16), w3.astype(BF16), w2.astype(BF16), ln_g[None, :], ln_b[None, :])


MOE_TOKENS = 1024
MOE_ROWS = 304
RANK_CHUNK = 256


def _router_body(x_ref, sh_ref, sc_ref, wr_ref, h_ref, g_ref, rk_ref, rkt_ref, cnt_ref):
    h = x_ref[0] * (1.0 + sc_ref[0]) + sh_ref[0]
    h_ref[0] = h.astype(BF16)
    logits = jnp.dot(h, wr_ref[...], precision=HI, preferred_element_type=F32)
    lane = lax.broadcasted_iota(jnp.int32, logits.shape, 1).astype(F32)
    logits = jnp.where(lane < N_EXPERTS, logits, -jnp.inf)
    m1 = jnp.max(logits, axis=1, keepdims=True)
    i1 = jnp.min(jnp.where(logits == m1, lane, float(LANES)), axis=1, keepdims=True)
    rest = jnp.where(lane == i1, -jnp.inf, logits)
    m2 = jnp.max(rest, axis=1, keepdims=True)
    i2 = jnp.min(jnp.where(rest == m2, lane, float(LANES)), axis=1, keepdims=True)
    e2 = jnp.exp(m2 - m1)
    w1 = 1.0 / (1.0 + e2)
    w2 = e2 / (1.0 + e2)
    g_ref[0] = jnp.where(lane == i1, w1, 0.0) + jnp.where(lane == i2, w2, 0.0)
    sel = jnp.logical_or(lane == i1, lane == i2)
    self_ = sel.astype(F32)
    tm = h.shape[0]
    C = min(RANK_CHUNK, tm)
    r = lax.broadcasted_iota(jnp.int32, (C, C), 0)
    c = lax.broadcasted_iota(jnp.int32, (C, C), 1)
    tri = (c < r).astype(BF16)
    carry = jnp.zeros((1, LANES), F32)
    parts = []
    for k in range(tm // C):
        sk = self_[k * C:(k + 1) * C]
        parts.append(jnp.dot(tri, sk.astype(BF16), preferred_element_type=F32) + carry)
        carry = carry + jnp.sum(sk, axis=0, keepdims=True)
    rank = jnp.where(sel, jnp.concatenate(parts, axis=0), -1.0)
    rk_ref[0] = rank
    rkt_ref[0] = rank.T[:8]
    cnt_ref[0, 0] = carry


def _router(x, shift, scale, w_router):
    B, L, D = x.shape
    tm = min(MOE_TOKENS, L)
    nt = L // tm
    wr = jnp.pad(w_router, ((0, 0), (0, LANES - N_EXPERTS)))
    vec = pl.BlockSpec((1, 1, D), lambda b, i: (b, 0, 0))
    col = pl.BlockSpec((1, tm, LANES), lambda b, i: (b, i, 0))
    return pl.pallas_call(
        _router_body,
        grid=(B, nt),
        in_specs=[pl.BlockSpec((1, tm, D), lambda b, i: (b, i, 0)), vec, vec, _full((D, LANES))],
        out_specs=[pl.BlockSpec((1, tm, D), lambda b, i: (b, i, 0)), col, col,
                   pl.BlockSpec((1, 8, tm), lambda b, i: (b, 0, i)), pl.BlockSpec((1, 1, 1, LANES), lambda b, i: (b, i, 0, 0))],
        out_shape=[jax.ShapeDtypeStruct((B, L, D), BF16), jax.ShapeDtypeStruct((B, L, LANES), F32),
                   jax.ShapeDtypeStruct((B, L, LANES), F32), jax.ShapeDtypeStruct((B, 8, L), F32),
                   jax.ShapeDtypeStruct((B, nt, 1, LANES), F32)],
        compiler_params=_cp("parallel", "parallel"),
        name="moe_router",
    )(x, shift, scale, wr)


def _moe_body(cnt_ref, h_ref, g_ref, rk_ref, rkt_ref, x_ref, gate_ref, w1_ref, w3_ref, w2_ref, lg_ref, lb_ref,
              o_ref, xg_ref, y_ref, acc_ref, *, alpha, M):
    b, i, e, j = pl.program_id(0), pl.program_id(1), pl.program_id(2), pl.program_id(3)
    nt, ne, nj = pl.num_programs(1), pl.num_programs(2), pl.num_programs(3)
    tm = h_ref.shape[1]
    cnt = cnt_ref[(b * nt + i) * ne + e]
    n_ch = lax.div(cnt + (M - 1), M)

    @pl.when(jnp.logical_and(e == 0, j == 0))
    def _():
        acc_ref[...] = jnp.zeros_like(acc_ref)

    @pl.when(j == 0)
    def _():
        rkt = rkt_ref[0, pl.ds(e, 1), :]

        def gather(c, carry):
            r0 = pl.multiple_of(c * M, 16)
            rows = (lax.broadcasted_iota(jnp.int32, (M, 1), 0) + c * M).astype(F32)
            onehot = (rkt == rows).astype(BF16)
            xg_ref[pl.ds(r0, M), :] = jnp.dot(onehot, h_ref[0], preferred_element_type=F32).astype(BF16)
            return carry

        lax.fori_loop(0, n_ch, gather, 0)

    def expert(c, carry):
        r0 = pl.multiple_of(c * M, 16)
        xg = xg_ref[pl.ds(r0, M), :]
        a = jnp.dot(xg, w1_ref[0], preferred_element_type=F32)
        g = jnp.dot(xg, w3_ref[0], preferred_element_type=F32)
        t = (a * jax.nn.sigmoid(a) * g).astype(BF16)
        yv = jnp.dot(t, w2_ref[0], preferred_element_type=F32)

        @pl.when(j == 0)
        def _():
            y_ref[pl.ds(r0, M), :] = yv

        @pl.when(j > 0)
        def _():
            y_ref[pl.ds(r0, M), :] += yv

        return carry

    lax.fori_loop(0, n_ch, expert, 0)

    @pl.when(j == nj - 1)
    def _():
        lane = lax.broadcasted_iota(jnp.int32, (tm, LANES), 1)
        rke = jnp.sum(jnp.where(lane == e, rk_ref[0], 0.0), axis=1, keepdims=True)
        ge = jnp.sum(jnp.where(lane == e, g_ref[0], 0.0), axis=1, keepdims=True)

        def scatter(c, carry):
            r0 = pl.multiple_of(c * M, 16)
            cols = (lax.broadcasted_iota(jnp.int32, (1, M), 1) + c * M).astype(F32)
            onehot = (rke == cols).astype(BF16)
            yb = y_ref[pl.ds(r0, M), :].astype(BF16)
            acc_ref[...] += ge * jnp.dot(onehot, yb, preferred_element_type=F32)
            return carry

        lax.fori_loop(0, n_ch, scatter, 0)

    @pl.when(jnp.logical_and(e == ne - 1, j == nj - 1))
    def _():
        o_ref[0] = _layernorm_rows(alpha * x_ref[0] + gate_ref[0] * acc_ref[...], lg_ref[...], lb_ref[...])


def _moe(x, shift, scale, gate, w_router, w1, w3, w2, ln_g, ln_b, alpha):
    B, L, D = x.shape
    E, _, F = w1.shape
    hb, gts, rk, rkt, cnt = _router(x, shift, scale, w_router)
    tm = min(MOE_TOKENS, L)
    nt = L // tm
    M = MOE_ROWS
    rows_max = -(-tm // M) * M
    tf = _ffn_tile(F)
    counts = cnt[:, :, 0, :E].astype(jnp.int32).reshape(-1)
    row = lambda w: pl.BlockSpec((1, tm, w), lambda b, i, e, j, c: (b, i, 0))
    vec = pl.BlockSpec((1, 1, D), lambda b, i, e, j, c: (b, 0, 0))
    par = pl.BlockSpec((1, D), lambda b, i, e, j, c: (0, 0))
    return pl.pallas_call(
        functools.partial(_moe_body, alpha=alpha, M=M),
        grid_spec=pltpu.PrefetchScalarGridSpec(
            num_scalar_prefetch=1,
            grid=(B, nt, E, F // tf),
            in_specs=[row(D), row(LANES), row(LANES), pl.BlockSpec((1, 8, tm), lambda b, i, e, j, c: (b, 0, i)),
                      row(D), vec,
                      pl.BlockSpec((1, D, tf), lambda b, i, e, j, c: (e, 0, j)),
                      pl.BlockSpec((1, D, tf), lambda b, i, e, j, c: (e, 0, j)),
                      pl.BlockSpec((1, tf, D), lambda b, i, e, j, c: (e, j, 0)), par, par],
            out_specs=row(D),
            scratch_shapes=[pltpu.VMEM((rows_max, D), BF16), pltpu.VMEM((rows_max, D), F32), pltpu.VMEM((tm, D), F32)],
        ),
        out_shape=jax.ShapeDtypeStruct((B, L, D), F32),
        compiler_params=_cp("parallel", "parallel", "arbitrary", "arbitrary"),
        name="moe",
    )(counts, hb, gts, rk, rkt, x, gate, w1.astype(BF16), w3.astype(BF16), w2.astype(BF16), ln_g[None, :], ln_b[None, :])


def _mod_body(c_ref, w_ref, b_ref, o_ref):
    c = c_ref[...]
    s = c * jax.nn.sigmoid(c)
    o_ref[...] = jnp.dot(s, w_ref[...], precision=HI, preferred_element_type=F32) + b_ref[...]


def _modulation(cc, w_mod, b_mod):
    R, D = cc.shape
    N = w_mod.shape[1]
    tn = 1024
    return pl.pallas_call(
        _mod_body,
        grid=(N // tn,),
        in_specs=[_full((R, D)), pl.BlockSpec((D, tn), lambda j: (0, j)), pl.BlockSpec((1, tn), lambda j: (0, j))],
        out_specs=pl.BlockSpec((R, tn), lambda j: (0, j)),
        out_shape=jax.ShapeDtypeStruct((R, N), F32),
        compiler_params=_cp("parallel"),
        name="modulation",
    )(cc, w_mod, b_mod[None, :])


def kernel(x, c, ctx, c_ctx, w_mod, b_mod, w_in, gla_w_gate, gla_b_gate, gla_norm_g, hy_conv_w, hy_conv_b, hy_f_w1, hy_f_b1, hy_f_freq1, hy_f_w2, hy_f_b2, hy_f_freq2, hy_f_w3, hy_f_b3, hy_skip, hy_norm_g, mla_q_norm_g, mla_w_uq, mla_kv_norm_g, mla_w_ukv, mla_norm_g, w_out, ln_g, ln_b, ffn_w1, ffn_w3, ffn_w2, moe_router, moe_w1, moe_w3, moe_w2):
    B, L, D = x.shape
    Lc = ctx.shape[1]
    depth = w_mod.shape[0]
    alpha = (2.0 * depth) ** 0.25
    cc = jnp.zeros((8, D), F32).at[:B].set(c).at[B].set(c_ctx)
    cos, sin = _rope_tables(L, True)
    cos_c, sin_c = _rope_tables(Lc, False)
    KD, VD = GLA_HEADS * GLA_DK, GLA_HEADS * GLA_DV
    xc = ctx
    for l in range(depth):
        need_ctx = l < depth - 1
        mods = _modulation(cc, w_mod[l], b_mod[l])
        m = [mods[:B, k * D:(k + 1) * D][:, None, :] for k in range(6)]
        mc = [jnp.broadcast_to(mods[B, k * D:(k + 1) * D][None, None, :], (B, 1, D)) for k in range(6)]
        w_arr = _arrange_w_in(w_in[l])
        wg, bg = _arrange_gate(gla_w_gate[l], gla_b_gate[l])
        filt = (hy_f_w1[l], hy_f_b1[l], hy_f_freq1[l], hy_f_w2[l], hy_f_b2[l], hy_f_freq2[l], hy_f_w3[l], hy_f_b3[l])
        wq = _arrange_wq(mla_w_uq[l])
        wk, wv = _arrange_wkv(mla_w_ukv[l])

        hyu, qk, vg, alr, cq, ckvr = _inproj(x, m[0], m[1], w_arr)
        hyu_c, qk_c, vg_c, alr_c, cq_c, ckvr_c = _inproj(xc, mc[0], mc[1], w_arr)

        of_c, ob_c, s_c = _gla(qk_c, vg_c, alr_c, wg, bg, jnp.zeros((B, 2, KD, VD), F32))
        of, ob, _ = _gla(qk, vg, alr, wg, bg, s_c)
        hy = _hyena(hyu, hy_conv_w[l], hy_conv_b[l], filt, hy_skip[l])
        k_c, v_c = _kvproj(ckvr_c, mla_kv_norm_g[l], wk, wv, cos_c, sin_c)
        k_m, v_m = _kvproj(ckvr, mla_kv_norm_g[l], wk, wv, cos, sin)
        q_m = _qproj(cq, mla_q_norm_g[l], wq, cos, sin)
        om = _flash(q_m, jnp.concatenate([k_c, k_m], axis=1), jnp.concatenate([v_c, v_m], axis=1))

        x = _outproj(of, ob, vg, hy, om, x, m[2], gla_norm_g[l], hy_norm_g[l], mla_norm_g[l], w_out[l],
                     ln_g[l, 0], ln_b[l, 0], alpha)
        if need_ctx:
            hy_c = _hyena_ctx(hyu_c, hy_conv_w[l], hy_conv_b[l], filt, hy_skip[l])
            q_c = _qproj(cq_c, mla_q_norm_g[l], wq, cos_c, sin_c)
            om_c = _flash(q_c, k_c, v_c)
            xc = _outproj(of_c, ob_c, vg_c, hy_c, om_c, xc, mc[2], gla_norm_g[l], hy_norm_g[l], mla_norm_g[l],
                          w_out[l], ln_g[l, 0], ln_b[l, 0], alpha)

        i = l // 2
        if l % 2 == 0:
            x = _ffn(x, m[3], m[4], m[5], ffn_w1[i], ffn_w3[i], ffn_w2[i], ln_g[l, 1], ln_b[l, 1], alpha)
            if need_ctx:
                xc = _ffn(xc, mc[3], mc[4], mc[5], ffn_w1[i], ffn_w3[i], ffn_w2[i], ln_g[l, 1], ln_b[l, 1], alpha)
        else:
            x = _moe(x, m[3], m[4], m[5], moe_router[i], moe_w1[i], moe_w3[i], moe_w2[i], ln_g[l, 1], ln_b[l, 1], alpha)
            if need_ctx:
                xc = _moe(xc, mc[3], mc[4], mc[5], moe_router[i], moe_w1[i], moe_w3[i], moe_w2[i], ln_g[l, 1],
                          ln_b[l, 1], alpha)
    return x
```

```python
import functools
import math

import numpy as np
import jax
import jax.numpy as jnp
from jax import lax
from jax.experimental import pallas as pl
from jax.experimental.pallas import tpu as pltpu

F32 = jnp.float32
BF16 = jnp.bfloat16
HI = lax.Precision.HIGHEST

GRID_W = 64
GLA_HEADS, GLA_DK, GLA_DV, GLA_RANK, GLA_TAU = 4, 32, 64, 16, 16.0
HY_CH, HY_EMB = 256, 33
HY_DECAY_TARGET, HY_FAST_DECAY, HY_SLOW_DECAY = 1e-2, 0.3, 1.5
MLA_HEADS, MLA_Q_RANK, MLA_KV_RANK, MLA_NOPE, MLA_ROPE, MLA_V = 8, 256, 128, 64, 32, 64
MLA_SCALE = (MLA_NOPE + MLA_ROPE) ** -0.5
ROPE_BASE = 10000.0
N_EXPERTS = 8
IN_SPLITS = (128, 128, 256, 256, 32, 768, 256, 128, 32)

LANES = 128
SUBLANES = 8
VMEM_LIMIT = 56 * 1024 * 1024

GLA_CHUNK = 128
DFT_N2 = 256


def _cp(*sem):
    return pltpu.CompilerParams(dimension_semantics=sem, vmem_limit_bytes=VMEM_LIMIT)


def _full(shape):
    n = len(shape)
    return pl.BlockSpec(shape, lambda *_: (0,) * n)


def _idiv(x, d):
    assert d & (d - 1) == 0
    return lax.shift_right_logical(x, int(math.log2(d)))


INPROJ_WIDTHS = (768, 256, 512, 128, 256, 256)


def _arrange_w_in(w):
    cuts = np.cumsum(IN_SPLITS)[:-1]
    qa, ka, va, ga, alr, hyu, cq, ckv, kr = jnp.split(w, [int(c) for c in cuts], axis=1)
    z96 = jnp.zeros((w.shape[0], 96), w.dtype)
    return jnp.concatenate([hyu, qa, ka, va, ga, alr, z96, cq, ckv, kr, z96], axis=1).astype(BF16)


def _inproj_body(x_ref, sh_ref, sc_ref, w_ref, *out_refs):
    h = x_ref[0] * (1.0 + sc_ref[0]) + sh_ref[0]
    acc = jnp.dot(h.astype(BF16), w_ref[...], preferred_element_type=F32)
    off = 0
    for r in out_refs:
        w = r.shape[-1]
        r[0] = acc[:, off:off + w]
        off += w


def _inproj(x, shift, scale, w_arr):
    B, L, D = x.shape
    tm = min(512, L)
    n = w_arr.shape[1]
    row = lambda w: pl.BlockSpec((1, tm, w), lambda b, i: (b, i, 0))
    vec = pl.BlockSpec((1, 1, D), lambda b, i: (b, 0, 0))
    return pl.pallas_call(
        _inproj_body,
        grid=(B, L // tm),
        in_specs=[row(D), vec, vec, _full((D, n))],
        out_specs=[row(w) for w in INPROJ_WIDTHS],
        out_shape=[jax.ShapeDtypeStruct((B, L, w), F32) for w in INPROJ_WIDTHS],
        compiler_params=_cp("parallel", "parallel"),
        name="inproj",
    )(x, shift, scale, w_arr)


def _log_sigmoid(z):
    return jnp.minimum(z, 0.0) - jnp.log1p(jnp.exp(-jnp.abs(z)))


def _gla_dir(qk, v, la, s_ref, d, tri):
    C = qk.shape[0]
    KD = GLA_HEADS * GLA_DK
    VD = GLA_HEADS * GLA_DV
    q = qk[:, :KD] * (GLA_DK ** -0.5)
    k = qk[:, KD:]
    b = jnp.dot(tri, la, precision=HI, preferred_element_type=F32)
    tot = jnp.sum(la, axis=0, keepdims=True)
    qe = q * jnp.exp(b)
    ke = (k * jnp.exp(-b)).astype(BF16)
    kl = (k * jnp.exp(tot - b)).astype(BF16)
    s_old = s_ref[d]
    vb = v.astype(BF16)
    o = jnp.dot(qe.astype(BF16), s_old.astype(BF16), preferred_element_type=F32)
    lane_k = _idiv(lax.broadcasted_iota(jnp.int32, (1, KD), 1), GLA_DK)
    lane_v = _idiv(lax.broadcasted_iota(jnp.int32, (1, VD), 1), GLA_DV)
    for h in range(GLA_HEADS):
        qh = jnp.where(lane_k == h, qe, 0.0).astype(BF16)
        att = lax.dot_general(qh, ke, (((1,), (1,)), ((), ())), preferred_element_type=F32)
        att = (att * tri).astype(BF16)
        oh = jnp.dot(att, vb, preferred_element_type=F32)
        o = o + jnp.where(lane_v == h, oh, 0.0)
    ones = jnp.ones((C, VD), F32)
    tot_b = lax.dot_general(la, ones, (((0,), (0,)), ((), ())), precision=HI, preferred_element_type=F32)
    kv = lax.dot_general(kl, vb, (((0,), (0,)), ((), ())), preferred_element_type=F32)
    rk = _idiv(lax.broadcasted_iota(jnp.int32, (KD, VD), 0), GLA_DK)
    cv = _idiv(lax.broadcasted_iota(jnp.int32, (KD, VD), 1), GLA_DV)
    s_ref[d] = jnp.exp(tot_b) * s_old + jnp.where(rk == cv, kv, 0.0)
    return o


def _gla_body(qkf_ref, vf_ref, af_ref, qkb_ref, vb_ref, ab_ref, wg_ref, bg_ref, s0_ref,
              of_ref, ob_ref, sout_ref, s_ref):
    i = pl.program_id(1)
    C = qkf_ref.shape[1]
    KD = GLA_HEADS * GLA_DK

    @pl.when(i == 0)
    def _():
        s_ref[...] = s0_ref[0]

    r = lax.broadcasted_iota(jnp.int32, (C, C), 0)
    c = lax.broadcasted_iota(jnp.int32, (C, C), 1)
    tri_lo = (c <= r).astype(F32)
    tri_up = (c >= r).astype(F32)
    zf = jnp.dot(af_ref[0], wg_ref[...], precision=HI, preferred_element_type=F32) + bg_ref[...]
    zb = jnp.dot(ab_ref[0], wg_ref[...], precision=HI, preferred_element_type=F32) + bg_ref[...]
    la_f = _log_sigmoid(zf[:, :KD]) / GLA_TAU
    la_b = _log_sigmoid(zb[:, KD:]) / GLA_TAU
    of_ref[0] = _gla_dir(qkf_ref[0], vf_ref[0], la_f, s_ref, 0, tri_lo)
    ob_ref[0] = _gla_dir(qkb_ref[0], vb_ref[0], la_b, s_ref, 1, tri_up)

    @pl.when(i == pl.num_programs(1) - 1)
    def _():
        sout_ref[0] = s_ref[...]


def _gla(qk, vg, alr, wg, bg, s0):
    B, L, _ = qk.shape
    C = min(GLA_CHUNK, L)
    n = L // C
    KD, VD = GLA_HEADS * GLA_DK, GLA_HEADS * GLA_DV
    fwd = lambda w: pl.BlockSpec((1, C, w), lambda b, i: (b, i, 0))
    bwd = lambda w: pl.BlockSpec((1, C, w), lambda b, i: (b, n - 1 - i, 0))
    st = pl.BlockSpec((1, 2, KD, VD), lambda b, i: (b, 0, 0, 0))
    return pl.pallas_call(
        _gla_body,
        grid=(B, n),
        in_specs=[fwd(2 * KD), fwd(VD), fwd(LANES), bwd(2 * KD), bwd(VD), bwd(LANES),
                  _full((LANES, 2 * KD)), _full((1, 2 * KD)), st],
        out_specs=[fwd(VD), bwd(VD), st],
        out_shape=[jax.ShapeDtypeStruct((B, L, VD), F32), jax.ShapeDtypeStruct((B, L, VD), F32),
                   jax.ShapeDtypeStruct((B, 2, KD, VD), F32)],
        scratch_shapes=[pltpu.VMEM((2, KD, VD), F32)],
        compiler_params=_cp("parallel", "arbitrary"),
        name="gla",
    )(qk, vg, alr, qk, vg, alr, wg, bg, s0)


def _arrange_gate(w_gate, b_gate):
    KD = GLA_HEADS * GLA_DK
    wg = jnp.zeros((LANES, 2 * KD), F32)
    wg = wg.at[:GLA_RANK, :KD].set(w_gate[0]).at[GLA_RANK:2 * GLA_RANK, KD:].set(w_gate[1])
    return wg, jnp.concatenate([b_gate[0], b_gate[1]])[None, :]


def _shortconv_body(x_ref, p_ref, n_ref, w_ref, b_ref, v_ref, x1_ref, x2_ref):
    i = pl.program_id(1)
    last = pl.num_programs(1) - 1
    x = x_ref[0]
    tm = x.shape[0]
    prev = jnp.where(i > 0, p_ref[0][7:8, :], 0.0)
    nxt = jnp.where(i < last, n_ref[0][0:1, :], 0.0)
    rid = lax.broadcasted_iota(jnp.int32, x.shape, 0)
    dn = jnp.where(rid == 0, prev, pltpu.roll(x, 1, 0))
    up = jnp.where(rid == tm - 1, nxt, pltpu.roll(x, tm - 1, 0))
    w = w_ref[...]
    y = b_ref[...] + dn * w[0:1] + x * w[1:2] + up * w[2:3]
    v_ref[0] = y[:, :HY_CH]
    x1_ref[0] = y[:, HY_CH:2 * HY_CH]
    x2_ref[0] = y[:, 2 * HY_CH:]


def _shortconv(u, w, b):
    B, L, W = u.shape
    tm = min(512, L)
    nb = tm // 8
    row = pl.BlockSpec((1, tm, W), lambda b_, i: (b_, i, 0))
    prev = pl.BlockSpec((1, 8, W), lambda b_, i: (b_, jnp.maximum(i * nb - 1, 0), 0))
    nxt = pl.BlockSpec((1, 8, W), lambda b_, i: (b_, jnp.minimum((i + 1) * nb, L // 8 - 1), 0))
    o = pl.BlockSpec((1, tm, HY_CH), lambda b_, i: (b_, i, 0))
    return pl.pallas_call(
        _shortconv_body,
        grid=(B, L // tm),
        in_specs=[row, prev, nxt, _full((3, W)), _full((1, W))],
        out_specs=[o, o, o],
        out_shape=[jax.ShapeDtypeStruct((B, L, HY_CH), F32)] * 3,
        compiler_params=_cp("parallel", "parallel"),
        name="shortconv",
    )(u, u, u, w, b[None, :])


def _filter_feats(L):
    pos = jnp.arange(L, dtype=F32)
    t = pos / (L - 1)
    bands = (HY_EMB - 1) // 2
    freqs = jnp.linspace(1e-4, bands - 1, bands, dtype=F32)
    ang = (2.0 * math.pi * pos / L)[:, None] * freqs
    z = jnp.concatenate([t[:, None], jnp.cos(ang), -jnp.sin(ang)], axis=-1)
    z = jnp.pad(z, ((0, 0), (0, LANES - HY_EMB)))
    deltas = jnp.abs(jnp.linspace(math.log(HY_DECAY_TARGET) / HY_SLOW_DECAY,
                                  math.log(HY_DECAY_TARGET) / HY_FAST_DECAY, HY_CH, dtype=F32))
    return z, jnp.tile(deltas, 4)[None, :]


def _filter_body(z_ref, w1_ref, b1_ref, f1_ref, w2_ref, b2_ref, f2_ref, w3_ref, b3_ref, dl_ref,
                 h_ref, ss_ref, *, L):
    i = pl.program_id(0)
    z = z_ref[...]
    tm = z.shape[0]
    hid = jnp.sin(f1_ref[...] * (jnp.dot(z, w1_ref[...], precision=HI, preferred_element_type=F32) + b1_ref[...]))
    hid = jnp.sin(f2_ref[...] * (jnp.dot(hid, w2_ref[...], precision=HI, preferred_element_type=F32) + b2_ref[...]))
    h = jnp.dot(hid, w3_ref[...], precision=HI, preferred_element_type=F32) + b3_ref[...]
    pos = (lax.broadcasted_iota(jnp.int32, (tm, 1), 0) + i * tm).astype(F32)
    t = pos / (L - 1)
    h = h * jnp.exp(-t * dl_ref[...])

    @pl.when(i == 0)
    def _():
        ss_ref[...] = jnp.zeros_like(ss_ref)

    ss_ref[...] += jnp.sum(h * h, axis=0, keepdims=True)
    col = lax.broadcasted_iota(jnp.int32, h.shape, 1)
    is_bwd = (_idiv(col, HY_CH) & 1) == 1
    h_ref[...] = jnp.where(jnp.logical_and(is_bwd, pos == 0.0), 0.0, h)


def _filters(L, fw1, fb1, ff1, fw2, fb2, ff2, fw3, fb3):
    z, dl = _filter_feats(L)
    tm = min(1024, L)
    Hf = fw2.shape[0]
    w1 = jnp.pad(fw1, ((0, LANES - HY_EMB), (0, 0)))
    NC = fw3.shape[1]
    return pl.pallas_call(
        functools.partial(_filter_body, L=L),
        grid=(L // tm,),
        in_specs=[pl.BlockSpec((tm, LANES), lambda i: (i, 0)), _full((LANES, Hf)), _full((1, Hf)), _full((1, Hf)),
                  _full((Hf, Hf)), _full((1, Hf)), _full((1, Hf)), _full((Hf, NC)), _full((1, NC)), _full((1, NC))],
        out_specs=[pl.BlockSpec((tm, NC), lambda i: (i, 0)), _full((1, NC))],
        out_shape=[jax.ShapeDtypeStruct((L, NC), F32), jax.ShapeDtypeStruct((1, NC), F32)],
        compiler_params=_cp("arbitrary"),
        name="hy_filters",
    )(z, w1, fb1[None], ff1[None], fw2, fb2[None], ff2[None], fw3, fb3[None], dl)


def _dft_consts(L):
    N = 2 * L
    N2 = DFT_N2
    N1 = N // N2
    half = N1 // 2
    k1 = np.arange(N1)[:, None].astype(np.float64)
    n1 = np.arange(N1)[None, :].astype(np.float64)
    a1 = 2.0 * np.pi * k1 * n1 / N1
    f1r, f1i = np.cos(a1), -np.sin(a1)
    fa = np.concatenate([f1r[:, :half], f1i[:, :half]], axis=0)
    fb = np.concatenate([f1r[:half, :], f1i[:half, :]], axis=1) / N
    k2 = np.arange(N2)[:, None].astype(np.float64)
    n2 = np.arange(N2)[None, :].astype(np.float64)
    a2 = 2.0 * np.pi * k2 * n2 / N2
    f2r, f2i = np.cos(a2), -np.sin(a2)
    g = np.block([[f2r, -f2i], [f2i, f2r]])
    gc = np.block([[f2r, f2i], [-f2i, f2r]])
    at = 2.0 * np.pi * (np.arange(N1)[:, None] * np.arange(N2)[None, :] % N) / N
    twr, twi = np.cos(at), -np.sin(at)
    c = lambda a: jnp.asarray(a, dtype=F32)
    bc = lambda a: jnp.broadcast_to(c(a)[:, :, None], (N1, N2, LANES))
    eye = np.eye(SUBLANES)
    return dict(N1=N1, N2=N2, half=half, fa=c(np.kron(fa, eye)), fb=c(np.kron(fb, eye)), g=c(g), gc=c(gc),
                twr=bc(twr), twi=bc(twi))


def _lanes(t, width):
    return jnp.concatenate([t] * (width // LANES), axis=-1)


def _dft1_body(f_ref, x_ref, o_ref):
    x = x_ref[0]
    x2 = x.reshape(x.shape[0] * SUBLANES, x.shape[2]).astype(BF16)
    y = jnp.dot(f_ref[...], x2, preferred_element_type=F32)
    o_ref[0] = y.reshape(o_ref.shape[1], SUBLANES, y.shape[1])


def _dft_stage1(fa, x):
    B, half, N2, W = x.shape
    R = fa.shape[0] // SUBLANES
    return pl.pallas_call(
        _dft1_body,
        grid=(B, N2 // SUBLANES),
        in_specs=[_full(fa.shape), pl.BlockSpec((1, half, SUBLANES, W), lambda b, j: (b, 0, j, 0))],
        out_specs=pl.BlockSpec((1, R, SUBLANES, W), lambda b, j: (b, 0, j, 0)),
        out_shape=jax.ShapeDtypeStruct((B, R, N2, W), F32),
        compiler_params=_cp("parallel", "parallel"),
        name="hy_dft1",
    )(fa.astype(BF16), x)


def _filter_spec_body(a_ref, twr_ref, twi_ref, g_ref, ss_ref, hf_ref):
    W = a_ref.shape[-1]
    ar, ai = a_ref[0, 0], a_ref[1, 0]
    twr, twi = _lanes(twr_ref[0], W), _lanes(twi_ref[0], W)
    xr = ar * twr - ai * twi
    xi = ar * twi + ai * twr
    z = jnp.dot(g_ref[...], jnp.concatenate([xr, xi], axis=0).astype(BF16), preferred_element_type=F32)
    n2 = z.shape[0] // 2
    zr, zi = z[:n2], z[n2:]
    ss = ss_ref[...]
    for o in range(2):
        f0, b0 = (2 * o) * HY_CH, (2 * o + 1) * HY_CH
        sc = lax.rsqrt(ss[:, f0:f0 + HY_CH] + ss[:, b0:b0 + HY_CH] + 1e-6)
        hf_ref[o, 0, 0] = (zr[:, f0:f0 + HY_CH] + zr[:, b0:b0 + HY_CH]) * sc
        hf_ref[o, 0, 1] = (zi[:, f0:f0 + HY_CH] - zi[:, b0:b0 + HY_CH]) * sc


def _filter_spectrum(h, ss, dc):
    L, NC = h.shape
    N1, N2, half = dc["N1"], dc["N2"], dc["half"]
    a = _dft_stage1(dc["fa"], h.reshape(1, half, N2, NC))
    a = a.reshape(2, N1, N2, NC)
    return pl.pallas_call(
        _filter_spec_body,
        grid=(N1,),
        in_specs=[pl.BlockSpec((2, 1, N2, NC), lambda k: (0, k, 0, 0)),
                  pl.BlockSpec((1, N2, LANES), lambda k: (k, 0, 0)), pl.BlockSpec((1, N2, LANES), lambda k: (k, 0, 0)),
                  _full((2 * N2, 2 * N2)), _full((1, NC))],
        out_specs=pl.BlockSpec((2, 1, 2, N2, HY_CH), lambda k: (0, k, 0, 0, 0)),
        out_shape=jax.ShapeDtypeStruct((2, N1, 2, N2, HY_CH), F32),
        compiler_params=_cp("parallel"),
        name="hy_filter_spec",
    )(a, dc["twr"], dc["twi"], dc["g"].astype(BF16), ss)


SPEC_K1 = 4


def _spec_mul_body(a_ref, twr_ref, twi_ref, g_ref, gc_ref, hf_ref, o_ref):
    W = a_ref.shape[-1]
    for k in range(a_ref.shape[2]):
        ar, ai = a_ref[0, 0, k], a_ref[0, 1, k]
        twr, twi = _lanes(twr_ref[k], W), _lanes(twi_ref[k], W)
        xr = ar * twr - ai * twi
        xi = ar * twi + ai * twr
        z = jnp.dot(g_ref[...], jnp.concatenate([xr, xi], axis=0).astype(BF16), preferred_element_type=F32)
        n2 = z.shape[0] // 2
        zr, zi = z[:n2], z[n2:]
        hr, hi = hf_ref[k, 0], hf_ref[k, 1]
        yr = zr * hr - zi * hi
        yi = zr * hi + zi * hr
        b = jnp.dot(gc_ref[...], jnp.concatenate([yr, yi], axis=0).astype(BF16), preferred_element_type=F32)
        br, bi = b[:n2], b[n2:]
        o_ref[0, 0, k] = br * twr + bi * twi
        o_ref[0, 1, k] = bi * twr - br * twi


def _spec_mul(a, hf, dc):
    B = a.shape[0]
    N1, N2 = dc["N1"], dc["N2"]
    C = a.shape[-1]
    kb = min(SPEC_K1, N1)
    blk = pl.BlockSpec((1, 2, kb, N2, C), lambda b, k: (b, 0, k, 0, 0))
    tw = pl.BlockSpec((kb, N2, LANES), lambda b, k: (k, 0, 0))
    return pl.pallas_call(
        _spec_mul_body,
        grid=(B, N1 // kb),
        in_specs=[blk, tw, tw, _full((2 * N2, 2 * N2)), _full((2 * N2, 2 * N2)),
                  pl.BlockSpec((kb, 2, N2, C), lambda b, k: (k, 0, 0, 0))],
        out_specs=blk,
        out_shape=jax.ShapeDtypeStruct(a.shape, F32),
        compiler_params=_cp("parallel", "parallel"),
        name="hy_spec_mul",
    )(a, dc["twr"], dc["twi"], dc["g"].astype(BF16), dc["gc"].astype(BF16), hf)


def _dft3_body(f_ref, b_ref, u_ref, gate_ref, skip_ref, o_ref):
    bm = b_ref[0]
    b2 = bm.reshape(bm.shape[0] * SUBLANES, bm.shape[2]).astype(BF16)
    y = jnp.dot(f_ref[...], b2, preferred_element_type=F32)
    rows, C = y.shape
    u = u_ref[0].reshape(rows, C)
    gate = gate_ref[0].reshape(rows, C)
    o_ref[0] = (gate * (y + u * skip_ref[...])).reshape(o_ref.shape[1], SUBLANES, C)


def _dft_stage3(fb, bm, u, gate, skip):
    B, R, N2, C = bm.shape
    half = u.shape[1]
    row = pl.BlockSpec((1, half, SUBLANES, C), lambda b, j: (b, 0, j, 0))
    return pl.pallas_call(
        _dft3_body,
        grid=(B, N2 // SUBLANES),
        in_specs=[_full(fb.shape), pl.BlockSpec((1, R, SUBLANES, C), lambda b, j: (b, 0, j, 0)), row, row, _full((1, C))],
        out_specs=row,
        out_shape=jax.ShapeDtypeStruct((B, half, N2, C), F32),
        compiler_params=_cp("parallel", "parallel"),
        name="hy_dft3",
    )(fb.astype(BF16), bm, u, gate, skip[None, :])


def _longconv_gated(u, gate, hf, skip, dc):
    B, L, C = u.shape
    N1, N2, half = dc["N1"], dc["N2"], dc["half"]
    u4 = u.reshape(B, half, N2, C)
    a = _dft_stage1(dc["fa"], u4).reshape(B, 2, N1, N2, C)
    bm = _spec_mul(a, hf, dc).reshape(B, 2 * N1, N2, C)
    return _dft_stage3(dc["fb"], bm, u4, gate.reshape(B, half, N2, C), skip).reshape(B, L, C)


def _hyena(hyu, conv_w, conv_b, filt, skip):
    B, L, _ = hyu.shape
    v, x1, x2 = _shortconv(hyu, conv_w, conv_b)
    h, ss = _filters(L, *filt)
    dc = _dft_consts(L)
    hf = _filter_spectrum(h, ss, dc)
    z1 = _longconv_gated(v, x1, hf[0], skip[0], dc)
    return _longconv_gated(z1, x2, hf[1], skip[1], dc)


def _hyena_ctx_body(v_ref, x1_ref, x2_ref, h_ref, ss_ref, skip_ref, fc_ref, gc_ref, o_ref):
    fc, gc = fc_ref[...], gc_ref[...]
    n = fc.shape[0] // 2
    ss = ss_ref[...]
    h = h_ref[...]

    def conv(u, o):
        f0, b0 = (2 * o) * HY_CH, (2 * o + 1) * HY_CH
        sc = lax.rsqrt(ss[:, f0:f0 + HY_CH] + ss[:, b0:b0 + HY_CH] + 1e-6)
        x = jnp.dot(fc, u, precision=HI, preferred_element_type=F32)
        hf = jnp.dot(fc, h[:, f0:f0 + HY_CH], precision=HI, preferred_element_type=F32)
        hb = jnp.dot(fc, h[:, b0:b0 + HY_CH], precision=HI, preferred_element_type=F32)
        hr = (hf[:n] + hb[:n]) * sc
        hi = (hf[n:] - hb[n:]) * sc
        yr = x[:n] * hr - x[n:] * hi
        yi = x[:n] * hi + x[n:] * hr
        y = jnp.dot(gc, jnp.concatenate([yr, yi], axis=0), precision=HI, preferred_element_type=F32)
        return y + u * skip_ref[o:o + 1, :]

    z1 = x1_ref[0] * conv(v_ref[0], 0)
    o_ref[0] = x2_ref[0] * conv(z1, 1)


def _hyena_ctx(hyu, conv_w, conv_b, filt, skip):
    B, L, _ = hyu.shape
    v, x1, x2 = _shortconv(hyu, conv_w, conv_b)
    h, ss = _filters(L, *filt)
    N = 2 * L
    ang = 2.0 * np.pi * (np.arange(N)[:, None] * np.arange(L)[None, :] % N) / N
    fr, fi = np.cos(ang), -np.sin(ang)
    fc = jnp.asarray(np.concatenate([fr, fi], axis=0), dtype=F32)
    gc = jnp.asarray(np.concatenate([fr.T, fi.T], axis=1) / N, dtype=F32)
    row = pl.BlockSpec((1, L, HY_CH), lambda b: (b, 0, 0))
    return pl.pallas_call(
        _hyena_ctx_body,
        grid=(B,),
        in_specs=[row, row, row, _full(h.shape), _full(ss.shape), _full(skip.shape), _full(fc.shape), _full(gc.shape)],
        out_specs=row,
        out_shape=jax.ShapeDtypeStruct((B, L, HY_CH), F32),
        compiler_params=_cp("parallel"),
        name="hyena_ctx",
    )(v, x1, x2, h, ss, skip, fc, gc)


HEAD_PAD = 128


def _rope_swap(w):
    a, b, c, d = w[..., 0:8], w[..., 8:16], w[..., 16:24], w[..., 24:32]
    return jnp.concatenate([-b, a, -d, c], axis=-1)


def _arrange_wq(w_uq):
    R = w_uq.shape[0]
    w = w_uq.reshape(R, MLA_HEADS, MLA_NOPE + MLA_ROPE)
    rope = w[..., MLA_NOPE:]
    out = jnp.concatenate([w[..., :MLA_NOPE], rope, _rope_swap(rope)], axis=-1)
    return out.reshape(R, MLA_HEADS * HEAD_PAD).astype(BF16)


def _arrange_wkv(w_ukv):
    R = w_ukv.shape[0]
    w = w_ukv.reshape(R, MLA_HEADS, MLA_NOPE + MLA_V)
    wk = jnp.concatenate([w[..., :MLA_NOPE], jnp.zeros((R, MLA_HEADS, HEAD_PAD - MLA_NOPE), w.dtype)], axis=-1)
    wv = w[..., MLA_NOPE:]
    return wk.reshape(R, MLA_HEADS * HEAD_PAD).astype(BF16), wv.reshape(R, MLA_HEADS * MLA_V).astype(BF16)


def _kr_place():
    e = np.zeros((LANES, MLA_HEADS * HEAD_PAD), np.float32)
    es = np.zeros((LANES, MLA_HEADS * HEAD_PAD), np.float32)
    for h in range(MLA_HEADS):
        base = h * HEAD_PAD + MLA_NOPE
        for j in range(MLA_ROPE):
            e[j, base + j] = 1.0
            blk, r = divmod(j, 16)
            if r < 8:
                es[16 * blk + r + 8, base + j] = -1.0
            else:
                es[16 * blk + r - 8, base + j] = 1.0
    return jnp.asarray(e).astype(BF16), jnp.asarray(es).astype(BF16)


def _rope_tables(L, rope):
    if rope:
        t = np.arange(L)
        row, col = (t // GRID_W).astype(np.float32), (t % GRID_W).astype(np.float32)
        half = MLA_ROPE // 2
        inv = ROPE_BASE ** (-jnp.arange(0, half, 2, dtype=F32) / half)
        ar = jnp.asarray(row)[:, None] * inv
        ac = jnp.asarray(col)[:, None] * inv
        cos = jnp.concatenate([jnp.cos(ar), jnp.cos(ar), jnp.cos(ac), jnp.cos(ac)], axis=-1)
        sin = jnp.concatenate([jnp.sin(ar), jnp.sin(ar), jnp.sin(ac), jnp.sin(ac)], axis=-1)
    else:
        cos, sin = jnp.ones((L, MLA_ROPE), F32), jnp.zeros((L, MLA_ROPE), F32)
    return cos, sin


def _rms_rows(x, g, eps=1e-6):
    return x * lax.rsqrt(jnp.mean(x * x, axis=-1, keepdims=True) + eps) * g


def _qproj_body(cq_ref, g_ref, w_ref, t1_ref, t2_ref, q_ref):
    xn = _rms_rows(cq_ref[0], g_ref[...])
    acc = jnp.dot(xn.astype(BF16), w_ref[...], preferred_element_type=F32)
    W = acc.shape[1]
    t1, t2 = _lanes(t1_ref[...], W), _lanes(t2_ref[...], W)
    q_ref[0] = (acc * t1 + pltpu.roll(acc, W - MLA_ROPE, 1) * t2).astype(q_ref.dtype)


def _qproj(cq, g, wq, cos, sin):
    B, L, R = cq.shape
    tm = min(512, L)
    W = wq.shape[1]
    ones, zeros = jnp.ones((L, MLA_NOPE), F32), jnp.zeros((L, MLA_ROPE), F32)
    qs = MLA_SCALE * math.log2(math.e)
    t1 = jnp.concatenate([ones, cos, zeros], axis=-1) * qs
    t2 = jnp.concatenate([jnp.zeros((L, MLA_NOPE), F32), sin, zeros], axis=-1) * qs
    tab = pl.BlockSpec((tm, HEAD_PAD), lambda b, i: (i, 0))
    return pl.pallas_call(
        _qproj_body,
        grid=(B, L // tm),
        in_specs=[pl.BlockSpec((1, tm, R), lambda b, i: (b, i, 0)), _full((1, R)), _full((R, W)), tab, tab],
        out_specs=pl.BlockSpec((1, tm, W), lambda b, i: (b, i, 0)),
        out_shape=jax.ShapeDtypeStruct((B, L, W), BF16),
        compiler_params=_cp("parallel", "parallel"),
        name="mla_qproj",
    )(cq, g[None, :], wq, t1, t2)


def _kvproj_body(c_ref, g_ref, wk_ref, wv_ref, e_ref, es_ref, cos_ref, sin_ref, k_ref, v_ref):
    c = c_ref[0]
    R = MLA_KV_RANK
    xn = _rms_rows(c[:, :R], g_ref[...]).astype(BF16)
    kr = c[:, R:]
    acc = jnp.dot(xn, wk_ref[...], preferred_element_type=F32)
    acc += jnp.dot((kr * cos_ref[...]).astype(BF16), e_ref[...], preferred_element_type=F32)
    acc += jnp.dot((kr * sin_ref[...]).astype(BF16), es_ref[...], preferred_element_type=F32)
    k_ref[0] = acc.astype(k_ref.dtype)
    v_ref[0] = jnp.dot(xn, wv_ref[...], preferred_element_type=F32).astype(v_ref.dtype)


def _kvproj(ckvr, g, wk, wv, cos, sin):
    B, L, Wc = ckvr.shape
    tm = min(512, L)
    pad = jnp.zeros((L, LANES - MLA_ROPE), F32)
    cos_p, sin_p = jnp.concatenate([cos, pad], axis=-1), jnp.concatenate([sin, pad], axis=-1)
    e, es = _kr_place()
    tab = pl.BlockSpec((tm, LANES), lambda b, i: (i, 0))
    Wk, Wv = wk.shape[1], wv.shape[1]
    return pl.pallas_call(
        _kvproj_body,
        grid=(B, L // tm),
        in_specs=[pl.BlockSpec((1, tm, Wc), lambda b, i: (b, i, 0)), _full((1, MLA_KV_RANK)),
                  _full(wk.shape), _full(wv.shape), _full(e.shape), _full(es.shape), tab, tab],
        out_specs=[pl.BlockSpec((1, tm, Wk), lambda b, i: (b, i, 0)), pl.BlockSpec((1, tm, Wv), lambda b, i: (b, i, 0))],
        out_shape=[jax.ShapeDtypeStruct((B, L, Wk), BF16), jax.ShapeDtypeStruct((B, L, Wv), BF16)],
        compiler_params=_cp("parallel", "parallel"),
        name="mla_kvproj",
    )(ckvr, g[None, :], wk, wv, e, es, cos_p, sin_p)


FLASH_ROWS = 256
FLASH_KEYS = 256


def _flash_body(q_ref, k_ref, v_ref, o_ref, m_ref, l_ref, acc_ref, s_ref, *, R):
    j = pl.program_id(3)
    tq, tk = q_ref.shape[1], k_ref.shape[1]
    CK = FLASH_KEYS
    npc = CK // LANES

    @pl.when(j == 0)
    def _():
        m_ref[...] = jnp.full_like(m_ref, -jnp.inf)
        l_ref[...] = jnp.zeros_like(l_ref)
        acc_ref[...] = jnp.zeros_like(acc_ref)

    def pass1(a, r):
        lo, r0 = a * HEAD_PAD, r * R
        q = q_ref[0, r0:r0 + R, lo:lo + HEAD_PAD]
        mp = None
        for c in range(tk // CK):
            kc = k_ref[0, c * CK:(c + 1) * CK, lo:lo + HEAD_PAD]
            s = lax.dot_general(q, kc, (((1,), (1,)), ((), ())), preferred_element_type=F32)
            s_ref[r0:r0 + R, c * CK:(c + 1) * CK] = s
            for w in range(npc):
                pc = s[:, w * LANES:(w + 1) * LANES]
                mp = pc if mp is None else jnp.maximum(mp, pc)
        m_old = m_ref[a, r0:r0 + R, :]
        return m_old, jnp.maximum(m_old, jnp.max(mp, axis=1, keepdims=True))

    def pass2(a, r, m_old, m_new):
        r0 = r * R
        alpha = jnp.exp2(m_old - m_new)
        lp = jnp.zeros((R, LANES), F32)
        pv = jnp.zeros((R, 2 * MLA_V), F32)
        for c in range(tk // CK):
            s = s_ref[r0:r0 + R, c * CK:(c + 1) * CK]
            ps = [jnp.exp2(s[:, w * LANES:(w + 1) * LANES] - m_new) for w in range(npc)]
            for p_ in ps:
                lp = lp + p_
            p = jnp.concatenate(ps, axis=1).astype(BF16)
            pv = pv + jnp.dot(p, v_ref[0, c * CK:(c + 1) * CK, :], preferred_element_type=F32)
        l_ref[a, r0:r0 + R, :] = alpha * l_ref[a, r0:r0 + R, :] + jnp.sum(lp, axis=1, keepdims=True)
        acc_ref[a, r0:r0 + R, :] = alpha * acc_ref[a, r0:r0 + R, :] + pv
        m_ref[a, r0:r0 + R, :] = m_new

    blocks = [(a, r) for a in range(2) for r in range(tq // R)]
    pend = pass1(*blocks[0])
    for i, blk in enumerate(blocks):
        nxt = pass1(*blocks[i + 1]) if i + 1 < len(blocks) else None
        pass2(*blk, *pend)
        pend = nxt

    @pl.when(j == pl.num_programs(3) - 1)
    def _():
        lane = lax.broadcasted_iota(jnp.int32, acc_ref.shape[1:], 1)
        o_ref[0] = jnp.where(lane < MLA_V, acc_ref[0] / l_ref[0], acc_ref[1] / l_ref[1])


def _flash_tiles(Lq, Lk):
    tq = min(1024, Lq)
    tk = Lk
    for cand in (3328, 1280, 1024, 512, 256):
        if Lk % cand == 0:
            tk = cand
            break
    return tq, tk


def _flash(q, k, v):
    B, Lq, _ = q.shape
    Lk = k.shape[1]
    tq, tk = _flash_tiles(Lq, Lk)
    hp = MLA_HEADS // 2
    return pl.pallas_call(
        functools.partial(_flash_body, R=min(FLASH_ROWS, tq)),
        grid=(B, hp, Lq // tq, Lk // tk),
        in_specs=[pl.BlockSpec((1, tq, 2 * HEAD_PAD), lambda b, h, i, j: (b, i, h)),
                  pl.BlockSpec((1, tk, 2 * HEAD_PAD), lambda b, h, i, j: (b, j, h)),
                  pl.BlockSpec((1, tk, 2 * MLA_V), lambda b, h, i, j: (b, j, h))],
        out_specs=pl.BlockSpec((1, tq, 2 * MLA_V), lambda b, h, i, j: (b, i, h)),
        out_shape=jax.ShapeDtypeStruct((B, Lq, MLA_HEADS * MLA_V), F32),
        scratch_shapes=[pltpu.VMEM((2, tq, LANES), F32), pltpu.VMEM((2, tq, LANES), F32),
                        pltpu.VMEM((2, tq, 2 * MLA_V), F32), pltpu.VMEM((tq, tk), F32)],
        compiler_params=_cp("parallel", "parallel", "parallel", "arbitrary"),
        name="mla_flash",
    )(q, k, v)


def _layernorm_rows(x, g, b, eps=1e-5):
    mu = jnp.mean(x, axis=-1, keepdims=True)
    xc = x - mu
    var = jnp.mean(xc * xc, axis=-1, keepdims=True)
    return xc * lax.rsqrt(var + eps) * g + b


def _outproj_body(of_ref, ob_ref, g_ref, hy_ref, om_ref, x_ref, gate_ref, gg_ref, hg_ref, mg_ref,
                  w_ref, lg_ref, lb_ref, o_ref, *, alpha):
    VD = GLA_HEADS * GLA_DV
    o = of_ref[0] + ob_ref[0]
    r = _idiv(lax.broadcasted_iota(jnp.int32, (VD, VD), 0), GLA_DV)
    c = _idiv(lax.broadcasted_iota(jnp.int32, (VD, VD), 1), GLA_DV)
    grp = (r == c).astype(F32)
    ms = jnp.dot(o * o, grp, precision=HI, preferred_element_type=F32) * (1.0 / GLA_DV)
    g = g_ref[0]
    ya = o * lax.rsqrt(ms + 1e-6) * gg_ref[...] * (g * jax.nn.sigmoid(g))
    yb = _rms_rows(hy_ref[0], hg_ref[...])
    yc = _rms_rows(om_ref[0], mg_ref[...])
    acc = jnp.dot(ya.astype(BF16), w_ref[0:VD, :], preferred_element_type=F32)
    acc += jnp.dot(yb.astype(BF16), w_ref[VD:VD + HY_CH, :], preferred_element_type=F32)
    acc += jnp.dot(yc.astype(BF16), w_ref[VD + HY_CH:, :], preferred_element_type=F32)
    o_ref[0] = _layernorm_rows(alpha * x_ref[0] + gate_ref[0] * acc, lg_ref[...], lb_ref[...])


def _outproj(of, ob, vg, hy, om, x, gate, gla_g, hy_g, mla_g, w_out, ln_g, ln_b, alpha):
    B, L, D = x.shape
    tm = min(512, L)
    VD = GLA_HEADS * GLA_DV
    MD = MLA_HEADS * MLA_V
    row = lambda w: pl.BlockSpec((1, tm, w), lambda b, i: (b, i, 0))
    return pl.pallas_call(
        functools.partial(_outproj_body, alpha=alpha),
        grid=(B, L // tm),
        in_specs=[row(VD), row(VD), pl.BlockSpec((1, tm, VD), lambda b, i: (b, i, 1)), row(HY_CH), row(MD), row(D),
                  pl.BlockSpec((1, 1, D), lambda b, i: (b, 0, 0)), _full((1, VD)), _full((1, HY_CH)), _full((1, MD)),
                  _full(w_out.shape), _full((1, D)), _full((1, D))],
        out_specs=row(D),
        out_shape=jax.ShapeDtypeStruct((B, L, D), F32),
        compiler_params=_cp("parallel", "parallel"),
        name="outproj",
    )(of, ob, vg, hy, om, x, gate, jnp.tile(gla_g, GLA_HEADS)[None, :], hy_g[None, :], mla_g[None, :],
      w_out.astype(BF16), ln_g[None, :], ln_b[None, :])


def _ffn_body(x_ref, sh_ref, sc_ref, gate_ref, w1_ref, w3_ref, w2_ref, lg_ref, lb_ref, o_ref, h_ref, acc_ref, *, alpha):
    j = pl.program_id(2)

    @pl.when(j == 0)
    def _():
        h_ref[...] = (x_ref[0] * (1.0 + sc_ref[0]) + sh_ref[0]).astype(BF16)
        acc_ref[...] = jnp.zeros_like(acc_ref)

    h = h_ref[...]
    a = jnp.dot(h, w1_ref[...], preferred_element_type=F32)
    b = jnp.dot(h, w3_ref[...], preferred_element_type=F32)
    t = (a * jax.nn.sigmoid(a) * b).astype(BF16)
    acc_ref[...] += jnp.dot(t, w2_ref[...], preferred_element_type=F32)

    @pl.when(j == pl.num_programs(2) - 1)
    def _():
        o_ref[0] = _layernorm_rows(alpha * x_ref[0] + gate_ref[0] * acc_ref[...], lg_ref[...], lb_ref[...])


def _ffn_tile(F):
    for cand in (512, 256, 128):
        if F % cand == 0:
            return cand
    return F


def _ffn(x, shift, scale, gate, w1, w3, w2, ln_g, ln_b, alpha):
    B, L, D = x.shape
    F = w1.shape[1]
    tm = min(1024, L)
    tf = _ffn_tile(F)
    row = pl.BlockSpec((1, tm, D), lambda b, i, j: (b, i, 0))
    vec = pl.BlockSpec((1, 1, D), lambda b, i, j: (b, 0, 0))
    return pl.pallas_call(
        functools.partial(_ffn_body, alpha=alpha),
        grid=(B, L // tm, F // tf),
        in_specs=[row, vec, vec, vec,
                  pl.BlockSpec((D, tf), lambda b, i, j: (0, j)), pl.BlockSpec((D, tf), lambda b, i, j: (0, j)),
                  pl.BlockSpec((tf, D), lambda b, i, j: (j, 0)), _full((1, D)), _full((1, D))],
        out_specs=row,
        out_shape=jax.ShapeDtypeStruct((B, L, D), F32),
        scratch_shapes=[pltpu.VMEM((tm, D), BF16), pltpu.VMEM((tm, D), F32)],
        compiler_params=_cp("parallel", "parallel", "arbitrary"),
        name="ffn",
    )(x, shift, scale, gate, w1.astype(BF16), w3.astype(BF16), w2.astype(BF16), ln_g[None, :], ln_b[None, :])


MOE_TOKENS = 2048
MOE_ROWS = 256
RANK_CHUNK = 256


def _router_body(x_ref, sh_ref, sc_ref, wr_ref, h_ref, g_ref, rk_ref, rkt_ref, cnt_ref):
    h = x_ref[0] * (1.0 + sc_ref[0]) + sh_ref[0]
    h_ref[0] = h.astype(BF16)
    logits = jnp.dot(h, wr_ref[...], precision=HI, preferred_element_type=F32)
    lane = lax.broadcasted_iota(jnp.int32, logits.shape, 1).astype(F32)
    logits = jnp.where(lane < N_EXPERTS, logits, -jnp.inf)
    m1 = jnp.max(logits, axis=1, keepdims=True)
    i1 = jnp.min(jnp.where(logits == m1, lane, float(LANES)), axis=1, keepdims=True)
    rest = jnp.where(lane == i1, -jnp.inf, logits)
    m2 = jnp.max(rest, axis=1, keepdims=True)
    i2 = jnp.min(jnp.where(rest == m2, lane, float(LANES)), axis=1, keepdims=True)
    e2 = jnp.exp(m2 - m1)
    w1 = 1.0 / (1.0 + e2)
    w2 = e2 / (1.0 + e2)
    g_ref[0] = jnp.where(lane == i1, w1, 0.0) + jnp.where(lane == i2, w2, 0.0)
    sel = jnp.logical_or(lane == i1, lane == i2)
    self_ = sel.astype(F32)
    tm = h.shape[0]
    C = min(RANK_CHUNK, tm)
    r = lax.broadcasted_iota(jnp.int32, (C, C), 0)
    c = lax.broadcasted_iota(jnp.int32, (C, C), 1)
    tri = (c < r).astype(BF16)
    carry = jnp.zeros((1, LANES), F32)
    parts = []
    for k in range(tm // C):
        sk = self_[k * C:(k + 1) * C]
        parts.append(jnp.dot(tri, sk.astype(BF16), preferred_element_type=F32) + carry)
        carry = carry + jnp.sum(sk, axis=0, keepdims=True)
    rank = jnp.where(sel, jnp.concatenate(parts, axis=0), -1.0)
    rk_ref[0] = rank
    rkt_ref[0] = rank.T[:8]
    cnt_ref[0, 0] = carry


def _router(x, shift, scale, w_router):
    B, L, D = x.shape
    tm = min(MOE_TOKENS, L)
    nt = L // tm
    wr = jnp.pad(w_router, ((0, 0), (0, LANES - N_EXPERTS)))
    vec = pl.BlockSpec((1, 1, D), lambda b, i: (b, 0, 0))
    col = pl.BlockSpec((1, tm, LANES), lambda b, i: (b, i, 0))
    return pl.pallas_call(
        _router_body,
        grid=(B, nt),
        in_specs=[pl.BlockSpec((1, tm, D), lambda b, i: (b, i, 0)), vec, vec, _full((D, LANES))],
        out_specs=[pl.BlockSpec((1, tm, D), lambda b, i: (b, i, 0)), col, col,
                   pl.BlockSpec((1, 8, tm), lambda b, i: (b, 0, i)), pl.BlockSpec((1, 1, 1, LANES), lambda b, i: (b, i, 0, 0))],
        out_shape=[jax.ShapeDtypeStruct((B, L, D), BF16), jax.ShapeDtypeStruct((B, L, LANES), F32),
                   jax.ShapeDtypeStruct((B, L, LANES), F32), jax.ShapeDtypeStruct((B, 8, L), F32),
                   jax.ShapeDtypeStruct((B, nt, 1, LANES), F32)],
        compiler_params=_cp("parallel", "parallel"),
        name="moe_router",
    )(x, shift, scale, wr)


def _moe_body(cnt_ref, h_ref, g_ref, rk_ref, rkt_ref, w1_ref, w3_ref, w2_ref, o_ref, xg_ref, y_ref, *, M, P):
    b, i, e, j = pl.program_id(0), pl.program_id(1), pl.program_id(2), pl.program_id(3)
    nt, ne, nj = pl.num_programs(1), pl.num_programs(2), pl.num_programs(3)
    tm = h_ref.shape[1]
    cnt = cnt_ref[(b * nt + i) * ne + e]
    n_ch = lax.div(cnt + (M - 1), M)

    @pl.when(jnp.logical_and(e == 0, j == 0))
    def _():
        o_ref[...] = jnp.zeros_like(o_ref)

    @pl.when(j == 0)
    def _():
        rkt = rkt_ref[0, pl.ds(e, 1), :]

        def gather(c, carry):
            r0 = pl.multiple_of(c * M, 16)
            rows = (lax.broadcasted_iota(jnp.int32, (M, 1), 0) + c * M).astype(F32)
            onehot = (rkt == rows).astype(BF16)
            xg_ref[pl.ds(r0, M), :] = jnp.dot(onehot, h_ref[0], preferred_element_type=F32).astype(BF16)
            return carry

        lax.fori_loop(0, n_ch, gather, 0)

    def expert(c, carry):
        r0 = pl.multiple_of(c * M, 16)
        xg = xg_ref[pl.ds(r0, M), :]
        a = jnp.dot(xg, w1_ref[0], preferred_element_type=F32)
        g = jnp.dot(xg, w3_ref[0], preferred_element_type=F32)
        t = (a * jax.nn.sigmoid(a) * g).astype(BF16)
        yv = jnp.dot(t, w2_ref[0], preferred_element_type=F32)

        @pl.when(j == 0)
        def _():
            y_ref[pl.ds(r0, M), :] = yv

        @pl.when(j > 0)
        def _():
            y_ref[pl.ds(r0, M), :] += yv

        return carry

    lax.fori_loop(0, n_ch, expert, 0)

    @pl.when(j == nj - 1)
    def _():
        for p in range(tm // P):
            lane = lax.broadcasted_iota(jnp.int32, (P, LANES), 1)
            rke = jnp.sum(jnp.where(lane == e, rk_ref[0, p * P:(p + 1) * P, :], 0.0), axis=1, keepdims=True)
            ge = jnp.sum(jnp.where(lane == e, g_ref[0, p * P:(p + 1) * P, :], 0.0), axis=1, keepdims=True)

            def scatter(c, carry):
                r0 = pl.multiple_of(c * M, 16)
                cols = (lax.broadcasted_iota(jnp.int32, (1, M), 1) + c * M).astype(F32)
                onehot = (rke == cols).astype(BF16)
                yb = y_ref[pl.ds(r0, M), :].astype(BF16)
                o_ref[0, p * P:(p + 1) * P, :] += ge * jnp.dot(onehot, yb, preferred_element_type=F32)
                return carry

            lax.fori_loop(0, n_ch, scatter, 0)


def _res_ln_body(x_ref, y_ref, gate_ref, lg_ref, lb_ref, o_ref, *, alpha):
    o_ref[0] = _layernorm_rows(alpha * x_ref[0] + gate_ref[0] * y_ref[0], lg_ref[...], lb_ref[...])


def _res_ln(x, y, gate, ln_g, ln_b, alpha):
    B, L, D = x.shape
    tm = min(1024, L)
    row = pl.BlockSpec((1, tm, D), lambda b, i: (b, i, 0))
    return pl.pallas_call(
        functools.partial(_res_ln_body, alpha=alpha),
        grid=(B, L // tm),
        in_specs=[row, row, pl.BlockSpec((1, 1, D), lambda b, i: (b, 0, 0)), _full((1, D)), _full((1, D))],
        out_specs=row,
        out_shape=jax.ShapeDtypeStruct((B, L, D), F32),
        compiler_params=_cp("parallel", "parallel"),
        name="res_ln",
    )(x, y, gate, ln_g[None, :], ln_b[None, :])


def _moe(x, shift, scale, gate, w_router, w1, w3, w2, ln_g, ln_b, alpha):
    B, L, D = x.shape
    E, _, F = w1.shape
    hb, gts, rk, rkt, cnt = _router(x, shift, scale, w_router)
    tm = min(MOE_TOKENS, L)
    nt = L // tm
    M = MOE_ROWS
    rows_max = -(-tm // M) * M
    tf = _ffn_tile(F)
    counts = cnt[:, :, 0, :E].astype(jnp.int32).reshape(-1)
    row = lambda w: pl.BlockSpec((1, tm, w), lambda b, i, e, j, c: (b, i, 0))
    y = pl.pallas_call(
        functools.partial(_moe_body, M=M, P=min(512, tm)),
        grid_spec=pltpu.PrefetchScalarGridSpec(
            num_scalar_prefetch=1,
            grid=(B, nt, E, F // tf),
            in_specs=[row(D), row(LANES), row(LANES), pl.BlockSpec((1, 8, tm), lambda b, i, e, j, c: (b, 0, i)),
                      pl.BlockSpec((1, D, tf), lambda b, i, e, j, c: (e, 0, j)),
                      pl.BlockSpec((1, D, tf), lambda b, i, e, j, c: (e, 0, j)),
                      pl.BlockSpec((1, tf, D), lambda b, i, e, j, c: (e, j, 0))],
            out_specs=row(D),
            scratch_shapes=[pltpu.VMEM((rows_max, D), BF16), pltpu.VMEM((rows_max, D), F32)],
        ),
        out_shape=jax.ShapeDtypeStruct((B, L, D), F32),
        compiler_params=_cp("parallel", "parallel", "arbitrary", "arbitrary"),
        name="moe",
    )(counts, hb, gts, rk, rkt, w1.astype(BF16), w3.astype(BF16), w2.astype(BF16))
    return _res_ln(x, y, gate, ln_g, ln_b, alpha)


def _mod_body(c_ref, w_ref, b_ref, o_ref):
    c = c_ref[...]
    s = c * jax.nn.sigmoid(c)
    o_ref[...] = jnp.dot(s, w_ref[...], precision=HI, preferred_element_type=F32) + b_ref[...]


def _modulation(cc, w_mod, b_mod):
    R, D = cc.shape
    N = w_mod.shape[1]
    tn = 1024
    return pl.pallas_call(
        _mod_body,
        grid=(N // tn,),
        in_specs=[_full((R, D)), pl.BlockSpec((D, tn), lambda j: (0, j)), pl.BlockSpec((1, tn), lambda j: (0, j))],
        out_specs=pl.BlockSpec((R, tn), lambda j: (0, j)),
        out_shape=jax.ShapeDtypeStruct((R, N), F32),
        compiler_params=_cp("parallel"),
        name="modulation",
    )(cc, w_mod, b_mod[None, :])


def kernel(x, c, ctx, c_ctx, w_mod, b_mod, w_in, gla_w_gate, gla_b_gate, gla_norm_g, hy_conv_w, hy_conv_b, hy_f_w1, hy_f_b1, hy_f_freq1, hy_f_w2, hy_f_b2, hy_f_freq2, hy_f_w3, hy_f_b3, hy_skip, hy_norm_g, mla_q_norm_g, mla_w_uq, mla_kv_norm_g, mla_w_ukv, mla_norm_g, w_out, ln_g, ln_b, ffn_w1, ffn_w3, ffn_w2, moe_router, moe_w1, moe_w3, moe_w2):
    B, L, D = x.shape
    Lc = ctx.shape[1]
    depth = w_mod.shape[0]
    alpha = (2.0 * depth) ** 0.25
    cc = jnp.zeros((8, D), F32).at[:B].set(c).at[B].set(c_ctx)
    cos, sin = _rope_tables(L, True)
    cos_c, sin_c = _rope_tables(Lc, False)
    KD, VD = GLA_HEADS * GLA_DK, GLA_HEADS * GLA_DV
    xc = ctx
    for l in range(depth):
        need_ctx = l < depth - 1
        mods = _modulation(cc, w_mod[l], b_mod[l])
        m = [mods[:B, k * D:(k + 1) * D][:, None, :] for k in range(6)]
        mc = [jnp.broadcast_to(mods[B, k * D:(k + 1) * D][None, None, :], (B, 1, D)) for k in range(6)]
        w_arr = _arrange_w_in(w_in[l])
        wg, bg = _arrange_gate(gla_w_gate[l], gla_b_gate[l])
        filt = (hy_f_w1[l], hy_f_b1[l], hy_f_freq1[l], hy_f_w2[l], hy_f_b2[l], hy_f_freq2[l], hy_f_w3[l], hy_f_b3[l])
        wq = _arrange_wq(mla_w_uq[l])
        wk, wv = _arrange_wkv(mla_w_ukv[l])

        hyu, qk, vg, alr, cq, ckvr = _inproj(x, m[0], m[1], w_arr)
        hyu_c, qk_c, vg_c, alr_c, cq_c, ckvr_c = _inproj(xc, mc[0], mc[1], w_arr)

        of_c, ob_c, s_c = _gla(qk_c, vg_c, alr_c, wg, bg, jnp.zeros((B, 2, KD, VD), F32))
        of, ob, _ = _gla(qk, vg, alr, wg, bg, s_c)
        hy = _hyena(hyu, hy_conv_w[l], hy_conv_b[l], filt, hy_skip[l])
        k_c, v_c = _kvproj(ckvr_c, mla_kv_norm_g[l], wk, wv, cos_c, sin_c)
        k_m, v_m = _kvproj(ckvr, mla_kv_norm_g[l], wk, wv, cos, sin)
        q_m = _qproj(cq, mla_q_norm_g[l], wq, cos, sin)
        om = _flash(q_m, jnp.concatenate([k_c, k_m], axis=1), jnp.concatenate([v_c, v_m], axis=1))

        x = _outproj(of, ob, vg, hy, om, x, m[2], gla_norm_g[l], hy_norm_g[l], mla_norm_g[l], w_out[l],
                     ln_g[l, 0], ln_b[l, 0], alpha)
        if need_ctx:
            hy_c = _hyena_ctx(hyu_c, hy_conv_w[l], hy_conv_b[l], filt, hy_skip[l])
            q_c = _qproj(cq_c, mla_q_norm_g[l], wq, cos_c, sin_c)
            om_c = _flash(q_c, k_c, v_c)
            xc = _outproj(of_c, ob_c, vg_c, hy_c, om_c, xc, mc[2], gla_norm_g[l], hy_norm_g[l], mla_norm_g[l],
                          w_out[l], ln_g[l, 0], ln_b[l, 0], alpha)

        i = l // 2
        if l % 2 == 0:
            x = _ffn(x, m[3], m[4], m[5], ffn_w1[i], ffn_w3[i], ffn_w2[i], ln_g[l, 1], ln_b[l, 1], alpha)
            if need_ctx:
                xc = _ffn(xc, mc[3], mc[4], mc[5], ffn_w1[i], ffn_w3[i], ffn_w2[i], ln_g[l, 1], ln_b[l, 1], alpha)
        else:
            x = _moe(x, m[3], m[4], m[5], moe_router[i], moe_w1[i], moe_w3[i], moe_w2[i], ln_g[l, 1], ln_b[l, 1], alpha)
            if need_ctx:
                xc = _moe(xc, mc[3], mc[4], mc[5], moe_router[i], moe_w1[i], moe_w3[i], moe_w2[i], ln_g[l, 1],
                          ln_b[l, 1], alpha)
    return x
```

```python
import functools
import math

import numpy as np
import jax
import jax.numpy as jnp
from jax import lax
from jax.experimental import pallas as pl
from jax.experimental.pallas import tpu as pltpu

F32 = jnp.float32
BF16 = jnp.bfloat16
HI = lax.Precision.HIGHEST

GRID_W = 64
GLA_HEADS, GLA_DK, GLA_DV, GLA_RANK, GLA_TAU = 4, 32, 64, 16, 16.0
HY_CH, HY_EMB = 256, 33
HY_DECAY_TARGET, HY_FAST_DECAY, HY_SLOW_DECAY = 1e-2, 0.3, 1.5
MLA_HEADS, MLA_Q_RANK, MLA_KV_RANK, MLA_NOPE, MLA_ROPE, MLA_V = 8, 256, 128, 64, 32, 64
MLA_SCALE = (MLA_NOPE + MLA_ROPE) ** -0.5
ROPE_BASE = 10000.0
N_EXPERTS = 8
IN_SPLITS = (128, 128, 256, 256, 32, 768, 256, 128, 32)

LANES = 128
SUBLANES = 8
VMEM_LIMIT = 56 * 1024 * 1024

GLA_CHUNK = 128
DFT_N2 = 256


def _cp(*sem):
    return pltpu.CompilerParams(dimension_semantics=sem, vmem_limit_bytes=VMEM_LIMIT)


def _full(shape):
    n = len(shape)
    return pl.BlockSpec(shape, lambda *_: (0,) * n)


def _idiv(x, d):
    assert d & (d - 1) == 0
    return lax.shift_right_logical(x, int(math.log2(d)))


INPROJ_WIDTHS = (768, 256, 512, 128, 256, 256)


def _arrange_w_in(w):
    cuts = np.cumsum(IN_SPLITS)[:-1]
    qa, ka, va, ga, alr, hyu, cq, ckv, kr = jnp.split(w, [int(c) for c in cuts], axis=1)
    z96 = jnp.zeros((w.shape[0], 96), w.dtype)
    return jnp.concatenate([hyu, qa, ka, va, ga, alr, z96, cq, ckv, kr, z96], axis=1).astype(BF16)


def _inproj_body(x_ref, sh_ref, sc_ref, w_ref, *out_refs):
    h = x_ref[0] * (1.0 + sc_ref[0]) + sh_ref[0]
    acc = jnp.dot(h.astype(BF16), w_ref[...], preferred_element_type=F32)
    off = 0
    for r in out_refs:
        w = r.shape[-1]
        r[0] = acc[:, off:off + w]
        off += w


def _inproj(x, shift, scale, w_arr):
    B, L, D = x.shape
    tm = min(512, L)
    n = w_arr.shape[1]
    row = lambda w: pl.BlockSpec((1, tm, w), lambda b, i: (b, i, 0))
    vec = pl.BlockSpec((1, 1, D), lambda b, i: (b, 0, 0))
    return pl.pallas_call(
        _inproj_body,
        grid=(B, L // tm),
        in_specs=[row(D), vec, vec, _full((D, n))],
        out_specs=[row(w) for w in INPROJ_WIDTHS],
        out_shape=[jax.ShapeDtypeStruct((B, L, w), F32) for w in INPROJ_WIDTHS],
        compiler_params=_cp("parallel", "parallel"),
        name="inproj",
    )(x, shift, scale, w_arr)


def _log_sigmoid(z):
    return jnp.minimum(z, 0.0) - jnp.log1p(jnp.exp(-jnp.abs(z)))


def _gla_dir(qk, v, la, s_ref, d, tri):
    C = qk.shape[0]
    KD = GLA_HEADS * GLA_DK
    VD = GLA_HEADS * GLA_DV
    q = qk[:, :KD] * (GLA_DK ** -0.5)
    k = qk[:, KD:]
    b = jnp.dot(tri, la, precision=HI, preferred_element_type=F32)
    tot = jnp.sum(la, axis=0, keepdims=True)
    qe = q * jnp.exp(b)
    ke = (k * jnp.exp(-b)).astype(BF16)
    kl = (k * jnp.exp(tot - b)).astype(BF16)
    s_old = s_ref[d]
    vb = v.astype(BF16)
    o = jnp.dot(qe.astype(BF16), s_old.astype(BF16), preferred_element_type=F32)
    lane_k = _idiv(lax.broadcasted_iota(jnp.int32, (1, KD), 1), GLA_DK)
    lane_v = _idiv(lax.broadcasted_iota(jnp.int32, (1, VD), 1), GLA_DV)
    for h in range(GLA_HEADS):
        qh = jnp.where(lane_k == h, qe, 0.0).astype(BF16)
        att = lax.dot_general(qh, ke, (((1,), (1,)), ((), ())), preferred_element_type=F32)
        att = (att * tri).astype(BF16)
        oh = jnp.dot(att, vb, preferred_element_type=F32)
        o = o + jnp.where(lane_v == h, oh, 0.0)
    ones = jnp.ones((C, VD), F32)
    tot_b = lax.dot_general(la, ones, (((0,), (0,)), ((), ())), precision=HI, preferred_element_type=F32)
    kv = lax.dot_general(kl, vb, (((0,), (0,)), ((), ())), preferred_element_type=F32)
    rk = _idiv(lax.broadcasted_iota(jnp.int32, (KD, VD), 0), GLA_DK)
    cv = _idiv(lax.broadcasted_iota(jnp.int32, (KD, VD), 1), GLA_DV)
    s_ref[d] = jnp.exp(tot_b) * s_old + jnp.where(rk == cv, kv, 0.0)
    return o


def _gla_body(qkf_ref, vf_ref, af_ref, qkb_ref, vb_ref, ab_ref, wg_ref, bg_ref, s0_ref,
              of_ref, ob_ref, sout_ref, s_ref):
    i = pl.program_id(0)
    C = qkf_ref.shape[1]
    KD = GLA_HEADS * GLA_DK

    @pl.when(i == 0)
    def _():
        s_ref[...] = s0_ref[...]

    r = lax.broadcasted_iota(jnp.int32, (C, C), 0)
    c = lax.broadcasted_iota(jnp.int32, (C, C), 1)
    tri_lo = (c <= r).astype(F32)
    tri_up = (c >= r).astype(F32)
    for b in range(qkf_ref.shape[0]):
        zf = jnp.dot(af_ref[b], wg_ref[...], precision=HI, preferred_element_type=F32) + bg_ref[...]
        zb = jnp.dot(ab_ref[b], wg_ref[...], precision=HI, preferred_element_type=F32) + bg_ref[...]
        la_f = _log_sigmoid(zf[:, :KD]) / GLA_TAU
        la_b = _log_sigmoid(zb[:, KD:]) / GLA_TAU
        of_ref[b] = _gla_dir(qkf_ref[b], vf_ref[b], la_f, s_ref, 2 * b, tri_lo)
        ob_ref[b] = _gla_dir(qkb_ref[b], vb_ref[b], la_b, s_ref, 2 * b + 1, tri_up)

    @pl.when(i == pl.num_programs(0) - 1)
    def _():
        sout_ref[...] = s_ref[...]


def _gla(qk, vg, alr, wg, bg, s0):
    B, L, _ = qk.shape
    C = min(GLA_CHUNK, L)
    n = L // C
    KD, VD = GLA_HEADS * GLA_DK, GLA_HEADS * GLA_DV
    fwd = lambda w: pl.BlockSpec((B, C, w), lambda i: (0, i, 0))
    bwd = lambda w: pl.BlockSpec((B, C, w), lambda i: (0, n - 1 - i, 0))
    st = _full((2 * B, KD, VD))
    of, ob, s_out = pl.pallas_call(
        _gla_body,
        grid=(n,),
        in_specs=[fwd(2 * KD), fwd(VD), fwd(LANES), bwd(2 * KD), bwd(VD), bwd(LANES),
                  _full((LANES, 2 * KD)), _full((1, 2 * KD)), st],
        out_specs=[fwd(VD), bwd(VD), st],
        out_shape=[jax.ShapeDtypeStruct((B, L, VD), F32), jax.ShapeDtypeStruct((B, L, VD), F32),
                   jax.ShapeDtypeStruct((2 * B, KD, VD), F32)],
        scratch_shapes=[pltpu.VMEM((2 * B, KD, VD), F32)],
        compiler_params=_cp("arbitrary"),
        name="gla",
    )(qk, vg, alr, qk, vg, alr, wg, bg, s0.reshape(2 * B, KD, VD))
    return of, ob, s_out.reshape(B, 2, KD, VD)


def _arrange_gate(w_gate, b_gate):
    KD = GLA_HEADS * GLA_DK
    wg = jnp.zeros((LANES, 2 * KD), F32)
    wg = wg.at[:GLA_RANK, :KD].set(w_gate[0]).at[GLA_RANK:2 * GLA_RANK, KD:].set(w_gate[1])
    return wg, jnp.concatenate([b_gate[0], b_gate[1]])[None, :]


def _shortconv_body(x_ref, p_ref, n_ref, w_ref, b_ref, v_ref, x1_ref, x2_ref):
    i = pl.program_id(1)
    last = pl.num_programs(1) - 1
    x = x_ref[0]
    tm = x.shape[0]
    prev = jnp.where(i > 0, p_ref[0][7:8, :], 0.0)
    nxt = jnp.where(i < last, n_ref[0][0:1, :], 0.0)
    rid = lax.broadcasted_iota(jnp.int32, x.shape, 0)
    dn = jnp.where(rid == 0, prev, pltpu.roll(x, 1, 0))
    up = jnp.where(rid == tm - 1, nxt, pltpu.roll(x, tm - 1, 0))
    w = w_ref[...]
    y = b_ref[...] + dn * w[0:1] + x * w[1:2] + up * w[2:3]
    v_ref[0] = y[:, :HY_CH]
    x1_ref[0] = y[:, HY_CH:2 * HY_CH]
    x2_ref[0] = y[:, 2 * HY_CH:]


def _shortconv(u, w, b):
    B, L, W = u.shape
    tm = min(512, L)
    nb = tm // 8
    row = pl.BlockSpec((1, tm, W), lambda b_, i: (b_, i, 0))
    prev = pl.BlockSpec((1, 8, W), lambda b_, i: (b_, jnp.maximum(i * nb - 1, 0), 0))
    nxt = pl.BlockSpec((1, 8, W), lambda b_, i: (b_, jnp.minimum((i + 1) * nb, L // 8 - 1), 0))
    o = pl.BlockSpec((1, tm, HY_CH), lambda b_, i: (b_, i, 0))
    return pl.pallas_call(
        _shortconv_body,
        grid=(B, L // tm),
        in_specs=[row, prev, nxt, _full((3, W)), _full((1, W))],
        out_specs=[o, o, o],
        out_shape=[jax.ShapeDtypeStruct((B, L, HY_CH), F32)] * 3,
        compiler_params=_cp("parallel", "parallel"),
        name="shortconv",
    )(u, u, u, w, b[None, :])


def _filter_feats(L):
    pos = jnp.arange(L, dtype=F32)
    t = pos / (L - 1)
    bands = (HY_EMB - 1) // 2
    freqs = jnp.linspace(1e-4, bands - 1, bands, dtype=F32)
    ang = (2.0 * math.pi * pos / L)[:, None] * freqs
    z = jnp.concatenate([t[:, None], jnp.cos(ang), -jnp.sin(ang)], axis=-1)
    z = jnp.pad(z, ((0, 0), (0, LANES - HY_EMB)))
    deltas = jnp.abs(jnp.linspace(math.log(HY_DECAY_TARGET) / HY_SLOW_DECAY,
                                  math.log(HY_DECAY_TARGET) / HY_FAST_DECAY, HY_CH, dtype=F32))
    return z, jnp.tile(deltas, 4)[None, :]


def _filter_body(z_ref, w1_ref, b1_ref, f1_ref, w2_ref, b2_ref, f2_ref, w3_ref, b3_ref, dl_ref,
                 h_ref, ss_ref, *, L):
    i = pl.program_id(0)
    z = z_ref[...]
    tm = z.shape[0]
    hid = jnp.sin(f1_ref[...] * (jnp.dot(z, w1_ref[...], precision=HI, preferred_element_type=F32) + b1_ref[...]))
    hid = jnp.sin(f2_ref[...] * (jnp.dot(hid, w2_ref[...], precision=HI, preferred_element_type=F32) + b2_ref[...]))
    h = jnp.dot(hid, w3_ref[...], precision=HI, preferred_element_type=F32) + b3_ref[...]
    pos = (lax.broadcasted_iota(jnp.int32, (tm, 1), 0) + i * tm).astype(F32)
    t = pos / (L - 1)
    h = h * jnp.exp(-t * dl_ref[...])

    @pl.when(i == 0)
    def _():
        ss_ref[...] = jnp.zeros_like(ss_ref)

    ss_ref[...] += jnp.sum(h * h, axis=0, keepdims=True)
    col = lax.broadcasted_iota(jnp.int32, h.shape, 1)
    is_bwd = (_idiv(col, HY_CH) & 1) == 1
    h_ref[...] = jnp.where(jnp.logical_and(is_bwd, pos == 0.0), 0.0, h)


def _filters(L, fw1, fb1, ff1, fw2, fb2, ff2, fw3, fb3):
    z, dl = _filter_feats(L)
    tm = min(1024, L)
    Hf = fw2.shape[0]
    w1 = jnp.pad(fw1, ((0, LANES - HY_EMB), (0, 0)))
    NC = fw3.shape[1]
    return pl.pallas_call(
        functools.partial(_filter_body, L=L),
        grid=(L // tm,),
        in_specs=[pl.BlockSpec((tm, LANES), lambda i: (i, 0)), _full((LANES, Hf)), _full((1, Hf)), _full((1, Hf)),
                  _full((Hf, Hf)), _full((1, Hf)), _full((1, Hf)), _full((Hf, NC)), _full((1, NC)), _full((1, NC))],
        out_specs=[pl.BlockSpec((tm, NC), lambda i: (i, 0)), _full((1, NC))],
        out_shape=[jax.ShapeDtypeStruct((L, NC), F32), jax.ShapeDtypeStruct((1, NC), F32)],
        compiler_params=_cp("arbitrary"),
        name="hy_filters",
    )(z, w1, fb1[None], ff1[None], fw2, fb2[None], ff2[None], fw3, fb3[None], dl)


def _dft_consts(L):
    N = 2 * L
    N2 = DFT_N2
    N1 = N // N2
    half = N1 // 2
    k1 = np.arange(N1)[:, None].astype(np.float64)
    n1 = np.arange(N1)[None, :].astype(np.float64)
    a1 = 2.0 * np.pi * k1 * n1 / N1
    f1r, f1i = np.cos(a1), -np.sin(a1)
    fa = np.concatenate([f1r[:, :half], f1i[:, :half]], axis=0)
    fb = np.concatenate([f1r[:half, :], f1i[:half, :]], axis=1) / N
    k2 = np.arange(N2)[:, None].astype(np.float64)
    n2 = np.arange(N2)[None, :].astype(np.float64)
    a2 = 2.0 * np.pi * k2 * n2 / N2
    f2r, f2i = np.cos(a2), -np.sin(a2)
    g = np.block([[f2r, -f2i], [f2i, f2r]])
    gc = np.block([[f2r, f2i], [-f2i, f2r]])
    at = 2.0 * np.pi * (np.arange(N1)[:, None] * np.arange(N2)[None, :] % N) / N
    twr, twi = np.cos(at), -np.sin(at)
    c = lambda a: jnp.asarray(a, dtype=F32)
    bc = lambda a: jnp.broadcast_to(c(a)[:, :, None], (N1, N2, LANES))
    eye = np.eye(SUBLANES)
    return dict(N1=N1, N2=N2, half=half, fa=c(np.kron(fa, eye)), fb=c(np.kron(fb, eye)), g=c(g), gc=c(gc),
                twr=bc(twr), twi=bc(twi))


def _lanes(t, width):
    return jnp.concatenate([t] * (width // LANES), axis=-1)


def _dft1_body(f_ref, x_ref, o_ref):
    x = x_ref[0]
    x2 = x.reshape(x.shape[0] * SUBLANES, x.shape[2]).astype(BF16)
    y = jnp.dot(f_ref[...], x2, preferred_element_type=F32)
    o_ref[0] = y.reshape(o_ref.shape[1], SUBLANES, y.shape[1])


def _dft_stage1(fa, x):
    B, half, N2, W = x.shape
    R = fa.shape[0] // SUBLANES
    return pl.pallas_call(
        _dft1_body,
        grid=(B, N2 // SUBLANES),
        in_specs=[_full(fa.shape), pl.BlockSpec((1, half, SUBLANES, W), lambda b, j: (b, 0, j, 0))],
        out_specs=pl.BlockSpec((1, R, SUBLANES, W), lambda b, j: (b, 0, j, 0)),
        out_shape=jax.ShapeDtypeStruct((B, R, N2, W), F32),
        compiler_params=_cp("parallel", "parallel"),
        name="hy_dft1",
    )(fa.astype(BF16), x)


def _filter_spec_body(a_ref, twr_ref, twi_ref, g_ref, ss_ref, hf_ref):
    W = a_ref.shape[-1]
    ar, ai = a_ref[0, 0], a_ref[1, 0]
    twr, twi = _lanes(twr_ref[0], W), _lanes(twi_ref[0], W)
    xr = ar * twr - ai * twi
    xi = ar * twi + ai * twr
    z = jnp.dot(g_ref[...], jnp.concatenate([xr, xi], axis=0).astype(BF16), preferred_element_type=F32)
    n2 = z.shape[0] // 2
    zr, zi = z[:n2], z[n2:]
    ss = ss_ref[...]
    for o in range(2):
        f0, b0 = (2 * o) * HY_CH, (2 * o + 1) * HY_CH
        sc = lax.rsqrt(ss[:, f0:f0 + HY_CH] + ss[:, b0:b0 + HY_CH] + 1e-6)
        hf_ref[o, 0, 0] = (zr[:, f0:f0 + HY_CH] + zr[:, b0:b0 + HY_CH]) * sc
        hf_ref[o, 0, 1] = (zi[:, f0:f0 + HY_CH] - zi[:, b0:b0 + HY_CH]) * sc


def _filter_spectrum(h, ss, dc):
    L, NC = h.shape
    N1, N2, half = dc["N1"], dc["N2"], dc["half"]
    a = _dft_stage1(dc["fa"], h.reshape(1, half, N2, NC))
    a = a.reshape(2, N1, N2, NC)
    return pl.pallas_call(
        _filter_spec_body,
        grid=(N1,),
        in_specs=[pl.BlockSpec((2, 1, N2, NC), lambda k: (0, k, 0, 0)),
                  pl.BlockSpec((1, N2, LANES), lambda k: (k, 0, 0)), pl.BlockSpec((1, N2, LANES), lambda k: (k, 0, 0)),
                  _full((2 * N2, 2 * N2)), _full((1, NC))],
        out_specs=pl.BlockSpec((2, 1, 2, N2, HY_CH), lambda k: (0, k, 0, 0, 0)),
        out_shape=jax.ShapeDtypeStruct((2, N1, 2, N2, HY_CH), F32),
        compiler_params=_cp("parallel"),
        name="hy_filter_spec",
    )(a, dc["twr"], dc["twi"], dc["g"].astype(BF16), ss)


SPEC_K1 = 4


def _spec_mul_body(a_ref, twr_ref, twi_ref, g_ref, gc_ref, hf_ref, o_ref):
    W = a_ref.shape[-1]
    for k in range(a_ref.shape[2]):
        ar, ai = a_ref[0, 0, k], a_ref[0, 1, k]
        twr, twi = _lanes(twr_ref[k], W), _lanes(twi_ref[k], W)
        xr = ar * twr - ai * twi
        xi = ar * twi + ai * twr
        z = jnp.dot(g_ref[...], jnp.concatenate([xr, xi], axis=0).astype(BF16), preferred_element_type=F32)
        n2 = z.shape[0] // 2
        zr, zi = z[:n2], z[n2:]
        hr, hi = hf_ref[0, k, 0], hf_ref[0, k, 1]
        yr = zr * hr - zi * hi
        yi = zr * hi + zi * hr
        b = jnp.dot(gc_ref[...], jnp.concatenate([yr, yi], axis=0).astype(BF16), preferred_element_type=F32)
        br, bi = b[:n2], b[n2:]
        o_ref[0, 0, k] = br * twr + bi * twi
        o_ref[0, 1, k] = bi * twr - br * twi


def _spec_mul(a, hf, order, dc):
    B = a.shape[0]
    N1, N2 = dc["N1"], dc["N2"]
    C = a.shape[-1]
    kb = min(SPEC_K1, N1)
    blk = pl.BlockSpec((1, 2, kb, N2, C), lambda b, k: (b, 0, k, 0, 0))
    tw = pl.BlockSpec((kb, N2, LANES), lambda b, k: (k, 0, 0))
    return pl.pallas_call(
        _spec_mul_body,
        grid=(B, N1 // kb),
        in_specs=[blk, tw, tw, _full((2 * N2, 2 * N2)), _full((2 * N2, 2 * N2)),
                  pl.BlockSpec((1, kb, 2, N2, C), lambda b, k: (order, k, 0, 0, 0))],
        out_specs=blk,
        out_shape=jax.ShapeDtypeStruct(a.shape, F32),
        compiler_params=_cp("parallel", "parallel"),
        name="hy_spec_mul",
    )(a, dc["twr"], dc["twi"], dc["g"].astype(BF16), dc["gc"].astype(BF16), hf)


def _dft3_body(f_ref, b_ref, u_ref, gate_ref, skip_ref, o_ref):
    bm = b_ref[0]
    b2 = bm.reshape(bm.shape[0] * SUBLANES, bm.shape[2]).astype(BF16)
    y = jnp.dot(f_ref[...], b2, preferred_element_type=F32)
    rows, C = y.shape
    u = u_ref[0].reshape(rows, C)
    gate = gate_ref[0].reshape(rows, C)
    o_ref[0] = (gate * (y + u * skip_ref[...])).reshape(o_ref.shape[1], SUBLANES, C)


def _dft_stage3(fb, bm, u, gate, skip):
    B, R, N2, C = bm.shape
    half = u.shape[1]
    row = pl.BlockSpec((1, half, SUBLANES, C), lambda b, j: (b, 0, j, 0))
    return pl.pallas_call(
        _dft3_body,
        grid=(B, N2 // SUBLANES),
        in_specs=[_full(fb.shape), pl.BlockSpec((1, R, SUBLANES, C), lambda b, j: (b, 0, j, 0)), row, row, _full((1, C))],
        out_specs=row,
        out_shape=jax.ShapeDtypeStruct((B, half, N2, C), F32),
        compiler_params=_cp("parallel", "parallel"),
        name="hy_dft3",
    )(fb.astype(BF16), bm, u, gate, skip[None, :])


def _longconv_gated(u, gate, hf, order, skip, dc):
    B, L, C = u.shape
    N1, N2, half = dc["N1"], dc["N2"], dc["half"]
    u4 = u.reshape(B, half, N2, C)
    a = _dft_stage1(dc["fa"], u4).reshape(B, 2, N1, N2, C)
    bm = _spec_mul(a, hf, order, dc).reshape(B, 2 * N1, N2, C)
    return _dft_stage3(dc["fb"], bm, u4, gate.reshape(B, half, N2, C), skip).reshape(B, L, C)


def _hyena(hyu, conv_w, conv_b, filt, skip):
    B, L, _ = hyu.shape
    v, x1, x2 = _shortconv(hyu, conv_w, conv_b)
    h, ss = _filters(L, *filt)
    dc = _dft_consts(L)
    hf = _filter_spectrum(h, ss, dc)
    z1 = _longconv_gated(v, x1, hf, 0, skip[0], dc)
    return _longconv_gated(z1, x2, hf, 1, skip[1], dc)


def _hyena_ctx_body(v_ref, x1_ref, x2_ref, h_ref, ss_ref, skip_ref, fc_ref, gc_ref, o_ref):
    fc, gc = fc_ref[...], gc_ref[...]
    n = fc.shape[0] // 2
    ss = ss_ref[...]
    h = h_ref[...]

    def conv(u, o):
        f0, b0 = (2 * o) * HY_CH, (2 * o + 1) * HY_CH
        sc = lax.rsqrt(ss[:, f0:f0 + HY_CH] + ss[:, b0:b0 + HY_CH] + 1e-6)
        x = jnp.dot(fc, u, precision=HI, preferred_element_type=F32)
        hf = jnp.dot(fc, h[:, f0:f0 + HY_CH], precision=HI, preferred_element_type=F32)
        hb = jnp.dot(fc, h[:, b0:b0 + HY_CH], precision=HI, preferred_element_type=F32)
        hr = (hf[:n] + hb[:n]) * sc
        hi = (hf[n:] - hb[n:]) * sc
        yr = x[:n] * hr - x[n:] * hi
        yi = x[:n] * hi + x[n:] * hr
        y = jnp.dot(gc, jnp.concatenate([yr, yi], axis=0), precision=HI, preferred_element_type=F32)
        return y + u * skip_ref[o:o + 1, :]

    z1 = x1_ref[0] * conv(v_ref[0], 0)
    o_ref[0] = x2_ref[0] * conv(z1, 1)


def _hyena_ctx(hyu, conv_w, conv_b, filt, skip):
    B, L, _ = hyu.shape
    v, x1, x2 = _shortconv(hyu, conv_w, conv_b)
    h, ss = _filters(L, *filt)
    N = 2 * L
    ang = 2.0 * np.pi * (np.arange(N)[:, None] * np.arange(L)[None, :] % N) / N
    fr, fi = np.cos(ang), -np.sin(ang)
    fc = jnp.asarray(np.concatenate([fr, fi], axis=0), dtype=F32)
    gc = jnp.asarray(np.concatenate([fr.T, fi.T], axis=1) / N, dtype=F32)
    row = pl.BlockSpec((1, L, HY_CH), lambda b: (b, 0, 0))
    return pl.pallas_call(
        _hyena_ctx_body,
        grid=(B,),
        in_specs=[row, row, row, _full(h.shape), _full(ss.shape), _full(skip.shape), _full(fc.shape), _full(gc.shape)],
        out_specs=row,
        out_shape=jax.ShapeDtypeStruct((B, L, HY_CH), F32),
        compiler_params=_cp("parallel"),
        name="hyena_ctx",
    )(v, x1, x2, h, ss, skip, fc, gc)


HEAD_PAD = 128


def _rope_swap(w):
    a, b, c, d = w[..., 0:8], w[..., 8:16], w[..., 16:24], w[..., 24:32]
    return jnp.concatenate([-b, a, -d, c], axis=-1)


def _arrange_wq(w_uq):
    R = w_uq.shape[0]
    w = w_uq.reshape(R, MLA_HEADS, MLA_NOPE + MLA_ROPE)
    rope = w[..., MLA_NOPE:]
    out = jnp.concatenate([w[..., :MLA_NOPE], rope, _rope_swap(rope)], axis=-1)
    return out.reshape(R, MLA_HEADS * HEAD_PAD).astype(BF16)


def _arrange_wkv(w_ukv):
    R = w_ukv.shape[0]
    w = w_ukv.reshape(R, MLA_HEADS, MLA_NOPE + MLA_V)
    wk = jnp.concatenate([w[..., :MLA_NOPE], jnp.zeros((R, MLA_HEADS, HEAD_PAD - MLA_NOPE), w.dtype)], axis=-1)
    wv = w[..., MLA_NOPE:]
    return wk.reshape(R, MLA_HEADS * HEAD_PAD).astype(BF16), wv.reshape(R, MLA_HEADS * MLA_V).astype(BF16)


def _kr_place():
    e = np.zeros((LANES, MLA_HEADS * HEAD_PAD), np.float32)
    es = np.zeros((LANES, MLA_HEADS * HEAD_PAD), np.float32)
    for h in range(MLA_HEADS):
        base = h * HEAD_PAD + MLA_NOPE
        for j in range(MLA_ROPE):
            e[j, base + j] = 1.0
            blk, r = divmod(j, 16)
            if r < 8:
                es[16 * blk + r + 8, base + j] = -1.0
            else:
                es[16 * blk + r - 8, base + j] = 1.0
    return jnp.asarray(e).astype(BF16), jnp.asarray(es).astype(BF16)


def _rope_tables(L, rope):
    if rope:
        t = np.arange(L)
        row, col = (t // GRID_W).astype(np.float32), (t % GRID_W).astype(np.float32)
        half = MLA_ROPE // 2
        inv = ROPE_BASE ** (-jnp.arange(0, half, 2, dtype=F32) / half)
        ar = jnp.asarray(row)[:, None] * inv
        ac = jnp.asarray(col)[:, None] * inv
        cos = jnp.concatenate([jnp.cos(ar), jnp.cos(ar), jnp.cos(ac), jnp.cos(ac)], axis=-1)
        sin = jnp.concatenate([jnp.sin(ar), jnp.sin(ar), jnp.sin(ac), jnp.sin(ac)], axis=-1)
    else:
        cos, sin = jnp.ones((L, MLA_ROPE), F32), jnp.zeros((L, MLA_ROPE), F32)
    return cos, sin


def _rms_rows(x, g, eps=1e-6):
    return x * lax.rsqrt(jnp.mean(x * x, axis=-1, keepdims=True) + eps) * g


def _qproj_body(cq_ref, g_ref, w_ref, t1_ref, t2_ref, q_ref):
    xn = _rms_rows(cq_ref[0], g_ref[...])
    acc = jnp.dot(xn.astype(BF16), w_ref[...], preferred_element_type=F32)
    W = acc.shape[1]
    t1, t2 = _lanes(t1_ref[...], W), _lanes(t2_ref[...], W)
    q_ref[0] = (acc * t1 + pltpu.roll(acc, W - MLA_ROPE, 1) * t2).astype(q_ref.dtype)


def _qproj(cq, g, wq, cos, sin):
    B, L, R = cq.shape
    tm = min(512, L)
    W = wq.shape[1]
    ones, zeros = jnp.ones((L, MLA_NOPE), F32), jnp.zeros((L, MLA_ROPE), F32)
    qs = MLA_SCALE * math.log2(math.e)
    t1 = jnp.concatenate([ones, cos, zeros], axis=-1) * qs
    t2 = jnp.concatenate([jnp.zeros((L, MLA_NOPE), F32), sin, zeros], axis=-1) * qs
    tab = pl.BlockSpec((tm, HEAD_PAD), lambda b, i: (i, 0))
    return pl.pallas_call(
        _qproj_body,
        grid=(B, L // tm),
        in_specs=[pl.BlockSpec((1, tm, R), lambda b, i: (b, i, 0)), _full((1, R)), _full((R, W)), tab, tab],
        out_specs=pl.BlockSpec((1, tm, W), lambda b, i: (b, i, 0)),
        out_shape=jax.ShapeDtypeStruct((B, L, W), BF16),
        compiler_params=_cp("parallel", "parallel"),
        name="mla_qproj",
    )(cq, g[None, :], wq, t1, t2)


def _kvproj_body(c_ref, g_ref, wk_ref, wv_ref, e_ref, es_ref, cos_ref, sin_ref, k_ref, v_ref):
    c = c_ref[0]
    R = MLA_KV_RANK
    xn = _rms_rows(c[:, :R], g_ref[...]).astype(BF16)
    kr = c[:, R:]
    acc = jnp.dot(xn, wk_ref[...], preferred_element_type=F32)
    acc += jnp.dot((kr * cos_ref[...]).astype(BF16), e_ref[...], preferred_element_type=F32)
    acc += jnp.dot((kr * sin_ref[...]).astype(BF16), es_ref[...], preferred_element_type=F32)
    k_ref[0] = acc.astype(k_ref.dtype)
    v_ref[0] = jnp.dot(xn, wv_ref[...], preferred_element_type=F32).astype(v_ref.dtype)


def _kvproj(ckvr, g, wk, wv, cos, sin):
    B, L, Wc = ckvr.shape
    tm = next(t for t in (1280, 512, 256, L) if L % t == 0)
    pad = jnp.zeros((L, LANES - MLA_ROPE), F32)
    cos_p, sin_p = jnp.concatenate([cos, pad], axis=-1), jnp.concatenate([sin, pad], axis=-1)
    e, es = _kr_place()
    tab = pl.BlockSpec((tm, LANES), lambda b, i: (i, 0))
    Wk, Wv = wk.shape[1], wv.shape[1]
    return pl.pallas_call(
        _kvproj_body,
        grid=(B, L // tm),
        in_specs=[pl.BlockSpec((1, tm, Wc), lambda b, i: (b, i, 0)), _full((1, MLA_KV_RANK)),
                  _full(wk.shape), _full(wv.shape), _full(e.shape), _full(es.shape), tab, tab],
        out_specs=[pl.BlockSpec((1, tm, Wk), lambda b, i: (b, i, 0)), pl.BlockSpec((1, tm, Wv), lambda b, i: (b, i, 0))],
        out_shape=[jax.ShapeDtypeStruct((B, L, Wk), BF16), jax.ShapeDtypeStruct((B, L, Wv), BF16)],
        compiler_params=_cp("parallel", "parallel"),
        name="mla_kvproj",
    )(ckvr, g[None, :], wk, wv, e, es, cos_p, sin_p)


FLASH_ROWS = 256
FLASH_KEYS = 256


def _flash_body(q_ref, k_ref, v_ref, o_ref, m_ref, l_ref, acc_ref, s_ref, *, R):
    j = pl.program_id(3)
    tq, tk = q_ref.shape[1], k_ref.shape[1]
    CK = FLASH_KEYS
    npc = CK // LANES

    @pl.when(j == 0)
    def _():
        m_ref[...] = jnp.full_like(m_ref, -jnp.inf)
        l_ref[...] = jnp.zeros_like(l_ref)
        acc_ref[...] = jnp.zeros_like(acc_ref)

    def pass1(a, r):
        lo, r0 = a * HEAD_PAD, r * R
        q = q_ref[0, r0:r0 + R, lo:lo + HEAD_PAD]
        mp = None
        for c in range(tk // CK):
            kc = k_ref[0, c * CK:(c + 1) * CK, lo:lo + HEAD_PAD]
            s = lax.dot_general(q, kc, (((1,), (1,)), ((), ())), preferred_element_type=F32)
            s_ref[r0:r0 + R, c * CK:(c + 1) * CK] = s
            for w in range(npc):
                pc = s[:, w * LANES:(w + 1) * LANES]
                mp = pc if mp is None else jnp.maximum(mp, pc)
        m_old = m_ref[a, r0:r0 + R, :]
        return m_old, jnp.maximum(m_old, jnp.max(mp, axis=1, keepdims=True))

    def pass2(a, r, m_old, m_new):
        r0 = r * R
        alpha = jnp.exp2(m_old - m_new)
        lp = jnp.zeros((R, LANES), F32)
        pv = jnp.zeros((R, 2 * MLA_V), F32)
        for c in range(tk // CK):
            s = s_ref[r0:r0 + R, c * CK:(c + 1) * CK]
            ps = [jnp.exp2(s[:, w * LANES:(w + 1) * LANES] - m_new) for w in range(npc)]
            for p_ in ps:
                lp = lp + p_
            p = jnp.concatenate(ps, axis=1).astype(BF16)
            pv = pv + jnp.dot(p, v_ref[0, c * CK:(c + 1) * CK, :], preferred_element_type=F32)
        l_ref[a, r0:r0 + R, :] = alpha * l_ref[a, r0:r0 + R, :] + jnp.sum(lp, axis=1, keepdims=True)
        acc_ref[a, r0:r0 + R, :] = alpha * acc_ref[a, r0:r0 + R, :] + pv
        m_ref[a, r0:r0 + R, :] = m_new

    assert tq // R >= 2
    blocks = [(a, r) for a in range(2) for r in range(tq // R)]
    pend = pass1(*blocks[0])
    for i, blk in enumerate(blocks):
        nxt = pass1(*blocks[i + 1]) if i + 1 < len(blocks) else None
        pass2(*blk, *pend)
        pend = nxt

    @pl.when(j == pl.num_programs(3) - 1)
    def _():
        lane = lax.broadcasted_iota(jnp.int32, acc_ref.shape[1:], 1)
        o_ref[0] = jnp.where(lane < MLA_V, acc_ref[0] / l_ref[0], acc_ref[1] / l_ref[1])


def _flash_tiles(Lq, Lk):
    tq = min(1024, Lq)
    tk = next(t for t in (3328, 1280, 1024, 512, 256, Lk) if Lk % t == 0)
    return tq, tk


def _flash(q, k, v):
    B, Lq, _ = q.shape
    Lk = k.shape[1]
    tq, tk = _flash_tiles(Lq, Lk)
    hp = MLA_HEADS // 2
    return pl.pallas_call(
        functools.partial(_flash_body, R=min(FLASH_ROWS, tq // 2)),
        grid=(B, hp, Lq // tq, Lk // tk),
        in_specs=[pl.BlockSpec((1, tq, 2 * HEAD_PAD), lambda b, h, i, j: (b, i, h)),
                  pl.BlockSpec((1, tk, 2 * HEAD_PAD), lambda b, h, i, j: (b, j, h)),
                  pl.BlockSpec((1, tk, 2 * MLA_V), lambda b, h, i, j: (b, j, h))],
        out_specs=pl.BlockSpec((1, tq, 2 * MLA_V), lambda b, h, i, j: (b, i, h)),
        out_shape=jax.ShapeDtypeStruct((B, Lq, MLA_HEADS * MLA_V), F32),
        scratch_shapes=[pltpu.VMEM((2, tq, LANES), F32), pltpu.VMEM((2, tq, LANES), F32),
                        pltpu.VMEM((2, tq, 2 * MLA_V), F32), pltpu.VMEM((tq, tk), F32)],
        compiler_params=_cp("parallel", "parallel", "parallel", "arbitrary"),
        name="mla_flash",
    )(q, k, v)


def _layernorm_rows(x, g, b, eps=1e-5):
    mu = jnp.mean(x, axis=-1, keepdims=True)
    xc = x - mu
    var = jnp.mean(xc * xc, axis=-1, keepdims=True)
    return xc * lax.rsqrt(var + eps) * g + b


def _outproj_body(of_ref, ob_ref, g_ref, hy_ref, om_ref, x_ref, gate_ref, gg_ref, hg_ref, mg_ref,
                  w_ref, lg_ref, lb_ref, o_ref, *, alpha):
    VD = GLA_HEADS * GLA_DV
    o = of_ref[0] + ob_ref[0]
    r = _idiv(lax.broadcasted_iota(jnp.int32, (VD, VD), 0), GLA_DV)
    c = _idiv(lax.broadcasted_iota(jnp.int32, (VD, VD), 1), GLA_DV)
    grp = (r == c).astype(F32)
    ms = jnp.dot(o * o, grp, precision=HI, preferred_element_type=F32) * (1.0 / GLA_DV)
    g = g_ref[0]
    ya = o * lax.rsqrt(ms + 1e-6) * gg_ref[...] * (g * jax.nn.sigmoid(g))
    yb = _rms_rows(hy_ref[0], hg_ref[...])
    yc = _rms_rows(om_ref[0], mg_ref[...])
    acc = jnp.dot(ya.astype(BF16), w_ref[0:VD, :], preferred_element_type=F32)
    acc += jnp.dot(yb.astype(BF16), w_ref[VD:VD + HY_CH, :], preferred_element_type=F32)
    acc += jnp.dot(yc.astype(BF16), w_ref[VD + HY_CH:, :], preferred_element_type=F32)
    o_ref[0] = _layernorm_rows(alpha * x_ref[0] + gate_ref[0] * acc, lg_ref[...], lb_ref[...])


def _outproj(of, ob, vg, hy, om, x, gate, gla_g, hy_g, mla_g, w_out, ln_g, ln_b, alpha):
    B, L, D = x.shape
    tm = min(512, L)
    VD = GLA_HEADS * GLA_DV
    MD = MLA_HEADS * MLA_V
    row = lambda w: pl.BlockSpec((1, tm, w), lambda b, i: (b, i, 0))
    return pl.pallas_call(
        functools.partial(_outproj_body, alpha=alpha),
        grid=(B, L // tm),
        in_specs=[row(VD), row(VD), pl.BlockSpec((1, tm, VD), lambda b, i: (b, i, 1)), row(HY_CH), row(MD), row(D),
                  pl.BlockSpec((1, 1, D), lambda b, i: (b, 0, 0)), _full((1, VD)), _full((1, HY_CH)), _full((1, MD)),
                  _full(w_out.shape), _full((1, D)), _full((1, D))],
        out_specs=row(D),
        out_shape=jax.ShapeDtypeStruct((B, L, D), F32),
        compiler_params=_cp("parallel", "parallel"),
        name="outproj",
    )(of, ob, vg, hy, om, x, gate, jnp.tile(gla_g, GLA_HEADS)[None, :], hy_g[None, :], mla_g[None, :],
      w_out.astype(BF16), ln_g[None, :], ln_b[None, :])


def _ffn_body(x_ref, sh_ref, sc_ref, gate_ref, w1_ref, w3_ref, w2_ref, lg_ref, lb_ref, o_ref, h_ref, acc_ref, *, alpha):
    j = pl.program_id(2)

    @pl.when(j == 0)
    def _():
        h_ref[...] = (x_ref[0] * (1.0 + sc_ref[0]) + sh_ref[0]).astype(BF16)
        acc_ref[...] = jnp.zeros_like(acc_ref)

    h = h_ref[...]
    a = jnp.dot(h, w1_ref[...], preferred_element_type=F32)
    b = jnp.dot(h, w3_ref[...], preferred_element_type=F32)
    t = (a * jax.nn.sigmoid(a) * b).astype(BF16)
    acc_ref[...] += jnp.dot(t, w2_ref[...], preferred_element_type=F32)

    @pl.when(j == pl.num_programs(2) - 1)
    def _():
        o_ref[0] = _layernorm_rows(alpha * x_ref[0] + gate_ref[0] * acc_ref[...], lg_ref[...], lb_ref[...])


def _ffn_tile(F):
    for cand in (512, 256, 128):
        if F % cand == 0:
            return cand
    return F


def _ffn(x, shift, scale, gate, w1, w3, w2, ln_g, ln_b, alpha):
    B, L, D = x.shape
    F = w1.shape[1]
    tm = min(1024, L)
    tf = _ffn_tile(F)
    row = pl.BlockSpec((1, tm, D), lambda b, i, j: (b, i, 0))
    vec = pl.BlockSpec((1, 1, D), lambda b, i, j: (b, 0, 0))
    return pl.pallas_call(
        functools.partial(_ffn_body, alpha=alpha),
        grid=(B, L // tm, F // tf),
        in_specs=[row, vec, vec, vec,
                  pl.BlockSpec((D, tf), lambda b, i, j: (0, j)), pl.BlockSpec((D, tf), lambda b, i, j: (0, j)),
                  pl.BlockSpec((tf, D), lambda b, i, j: (j, 0)), _full((1, D)), _full((1, D))],
        out_specs=row,
        out_shape=jax.ShapeDtypeStruct((B, L, D), F32),
        scratch_shapes=[pltpu.VMEM((tm, D), BF16), pltpu.VMEM((tm, D), F32)],
        compiler_params=_cp("parallel", "parallel", "arbitrary"),
        name="ffn",
    )(x, shift, scale, gate, w1.astype(BF16), w3.astype(BF16), w2.astype(BF16), ln_g[None, :], ln_b[None, :])


MOE_TOKENS = 2048
MOE_ROWS = 256
RANK_CHUNK = 256


def _router_body(x_ref, sh_ref, sc_ref, wr_ref, h_ref, g_ref, rk_ref, rkt_ref, cnt_ref):
    h = x_ref[0] * (1.0 + sc_ref[0]) + sh_ref[0]
    h_ref[0] = h.astype(BF16)
    logits = jnp.dot(h, wr_ref[...], precision=HI, preferred_element_type=F32)
    lane = lax.broadcasted_iota(jnp.int32, logits.shape, 1).astype(F32)
    logits = jnp.where(lane < N_EXPERTS, logits, -jnp.inf)
    m1 = jnp.max(logits, axis=1, keepdims=True)
    i1 = jnp.min(jnp.where(logits == m1, lane, float(LANES)), axis=1, keepdims=True)
    rest = jnp.where(lane == i1, -jnp.inf, logits)
    m2 = jnp.max(rest, axis=1, keepdims=True)
    i2 = jnp.min(jnp.where(rest == m2, lane, float(LANES)), axis=1, keepdims=True)
    e2 = jnp.exp(m2 - m1)
    w1 = 1.0 / (1.0 + e2)
    w2 = e2 / (1.0 + e2)
    g_ref[0] = jnp.where(lane == i1, w1, 0.0) + jnp.where(lane == i2, w2, 0.0)
    sel = jnp.logical_or(lane == i1, lane == i2)
    self_ = sel.astype(F32)
    tm = h.shape[0]
    C = min(RANK_CHUNK, tm)
    r = lax.broadcasted_iota(jnp.int32, (C, C), 0)
    c = lax.broadcasted_iota(jnp.int32, (C, C), 1)
    tri = (c < r).astype(BF16)
    carry = jnp.zeros((1, LANES), F32)
    parts = []
    for k in range(tm // C):
        sk = self_[k * C:(k + 1) * C]
        parts.append(jnp.dot(tri, sk.astype(BF16), preferred_element_type=F32) + carry)
        carry = carry + jnp.sum(sk, axis=0, keepdims=True)
    rank = jnp.where(sel, jnp.concatenate(parts, axis=0), -1.0)
    rk_ref[0] = rank
    rkt_ref[0] = rank.T[:8]
    cnt_ref[0, 0] = carry


def _router(x, shift, scale, w_router):
    B, L, D = x.shape
    tm = min(MOE_TOKENS, L)
    nt = L // tm
    wr = jnp.pad(w_router, ((0, 0), (0, LANES - N_EXPERTS)))
    vec = pl.BlockSpec((1, 1, D), lambda b, i: (b, 0, 0))
    col = pl.BlockSpec((1, tm, LANES), lambda b, i: (b, i, 0))
    return pl.pallas_call(
        _router_body,
        grid=(B, nt),
        in_specs=[pl.BlockSpec((1, tm, D), lambda b, i: (b, i, 0)), vec, vec, _full((D, LANES))],
        out_specs=[pl.BlockSpec((1, tm, D), lambda b, i: (b, i, 0)), col, col,
                   pl.BlockSpec((1, 8, tm), lambda b, i: (b, 0, i)), pl.BlockSpec((1, 1, 1, LANES), lambda b, i: (b, i, 0, 0))],
        out_shape=[jax.ShapeDtypeStruct((B, L, D), BF16), jax.ShapeDtypeStruct((B, L, LANES), F32),
                   jax.ShapeDtypeStruct((B, L, LANES), F32), jax.ShapeDtypeStruct((B, 8, L), F32),
                   jax.ShapeDtypeStruct((B, nt, 1, LANES), F32)],
        compiler_params=_cp("parallel", "parallel"),
        name="moe_router",
    )(x, shift, scale, wr)


def _moe_body(cnt_ref, h_ref, g_ref, rk_ref, rkt_ref, w1_ref, w3_ref, w2_ref, o_ref, xg_ref, y_ref, *, M, P):
    b, i, e, j = pl.program_id(0), pl.program_id(1), pl.program_id(2), pl.program_id(3)
    nt, ne, nj = pl.num_programs(1), pl.num_programs(2), pl.num_programs(3)
    tm = h_ref.shape[1]
    cnt = cnt_ref[(b * nt + i) * ne + e]
    n_ch = lax.div(cnt + (M - 1), M)

    @pl.when(jnp.logical_and(e == 0, j == 0))
    def _():
        o_ref[...] = jnp.zeros_like(o_ref)

    @pl.when(j == 0)
    def _():
        rkt = rkt_ref[0, pl.ds(e, 1), :]

        def gather(c, carry):
            r0 = pl.multiple_of(c * M, 16)
            rows = (lax.broadcasted_iota(jnp.int32, (M, 1), 0) + c * M).astype(F32)
            onehot = (rkt == rows).astype(BF16)
            xg_ref[pl.ds(r0, M), :] = jnp.dot(onehot, h_ref[0], preferred_element_type=F32).astype(BF16)
            return carry

        lax.fori_loop(0, n_ch, gather, 0)

    def expert(c, carry):
        r0 = pl.multiple_of(c * M, 16)
        xg = xg_ref[pl.ds(r0, M), :]
        a = jnp.dot(xg, w1_ref[0], preferred_element_type=F32)
        g = jnp.dot(xg, w3_ref[0], preferred_element_type=F32)
        t = (a * jax.nn.sigmoid(a) * g).astype(BF16)
        yv = jnp.dot(t, w2_ref[0], preferred_element_type=F32)

        @pl.when(j == 0)
        def _():
            y_ref[pl.ds(r0, M), :] = yv

        @pl.when(j > 0)
        def _():
            y_ref[pl.ds(r0, M), :] += yv

        return carry

    lax.fori_loop(0, n_ch, expert, 0)

    @pl.when(j == nj - 1)
    def _():
        for p in range(tm // P):
            lane = lax.broadcasted_iota(jnp.int32, (P, LANES), 1)
            rke = jnp.sum(jnp.where(lane == e, rk_ref[0, p * P:(p + 1) * P, :], 0.0), axis=1, keepdims=True)
            ge = jnp.sum(jnp.where(lane == e, g_ref[0, p * P:(p + 1) * P, :], 0.0), axis=1, keepdims=True)

            def scatter(c, carry):
                r0 = pl.multiple_of(c * M, 16)
                cols = (lax.broadcasted_iota(jnp.int32, (1, M), 1) + c * M).astype(F32)
                onehot = (rke == cols).astype(BF16)
                yb = y_ref[pl.ds(r0, M), :].astype(BF16)
                o_ref[0, p * P:(p + 1) * P, :] += ge * jnp.dot(onehot, yb, preferred_element_type=F32)
                return carry

            lax.fori_loop(0, n_ch, scatter, 0)


def _res_ln_body(x_ref, y_ref, gate_ref, lg_ref, lb_ref, o_ref, *, alpha):
    o_ref[0] = _layernorm_rows(alpha * x_ref[0] + gate_ref[0] * y_ref[0], lg_ref[...], lb_ref[...])


def _res_ln(x, y, gate, ln_g, ln_b, alpha):
    B, L, D = x.shape
    tm = min(1024, L)
    row = pl.BlockSpec((1, tm, D), lambda b, i: (b, i, 0))
    return pl.pallas_call(
        functools.partial(_res_ln_body, alpha=alpha),
        grid=(B, L // tm),
        in_specs=[row, row, pl.BlockSpec((1, 1, D), lambda b, i: (b, 0, 0)), _full((1, D)), _full((1, D))],
        out_specs=row,
        out_shape=jax.ShapeDtypeStruct((B, L, D), F32),
        compiler_params=_cp("parallel", "parallel"),
        name="res_ln",
    )(x, y, gate, ln_g[None, :], ln_b[None, :])


def _moe(x, shift, scale, gate, w_router, w1, w3, w2, ln_g, ln_b, alpha):
    B, L, D = x.shape
    E, _, F = w1.shape
    hb, gts, rk, rkt, cnt = _router(x, shift, scale, w_router)
    tm = min(MOE_TOKENS, L)
    nt = L // tm
    M = MOE_ROWS
    rows_max = -(-tm // M) * M
    tf = _ffn_tile(F)
    counts = cnt[:, :, 0, :E].astype(jnp.int32).reshape(-1)
    row = lambda w: pl.BlockSpec((1, tm, w), lambda b, i, e, j, c: (b, i, 0))
    y = pl.pallas_call(
        functools.partial(_moe_body, M=M, P=min(512, tm)),
        grid_spec=pltpu.PrefetchScalarGridSpec(
            num_scalar_prefetch=1,
            grid=(B, nt, E, F // tf),
            in_specs=[row(D), row(LANES), row(LANES), pl.BlockSpec((1, 8, tm), lambda b, i, e, j, c: (b, 0, i)),
                      pl.BlockSpec((1, D, tf), lambda b, i, e, j, c: (e, 0, j)),
                      pl.BlockSpec((1, D, tf), lambda b, i, e, j, c: (e, 0, j)),
                      pl.BlockSpec((1, tf, D), lambda b, i, e, j, c: (e, j, 0))],
            out_specs=row(D),
            scratch_shapes=[pltpu.VMEM((rows_max, D), BF16), pltpu.VMEM((rows_max, D), F32)],
        ),
        out_shape=jax.ShapeDtypeStruct((B, L, D), F32),
        compiler_params=_cp("parallel", "parallel", "arbitrary", "arbitrary"),
        name="moe",
    )(counts, hb, gts, rk, rkt, w1.astype(BF16), w3.astype(BF16), w2.astype(BF16))
    return _res_ln(x, y, gate, ln_g, ln_b, alpha)


def _mod_body(c_ref, w_ref, b_ref, o_ref):
    c = c_ref[...]
    s = c * jax.nn.sigmoid(c)
    o_ref[...] = jnp.dot(s, w_ref[...], precision=HI, preferred_element_type=F32) + b_ref[...]


def _modulation(cc, w_mod, b_mod):
    R, D = cc.shape
    N = w_mod.shape[1]
    tn = 1024
    return pl.pallas_call(
        _mod_body,
        grid=(N // tn,),
        in_specs=[_full((R, D)), pl.BlockSpec((D, tn), lambda j: (0, j)), pl.BlockSpec((1, tn), lambda j: (0, j))],
        out_specs=pl.BlockSpec((R, tn), lambda j: (0, j)),
        out_shape=jax.ShapeDtypeStruct((R, N), F32),
        compiler_params=_cp("parallel"),
        name="modulation",
    )(cc, w_mod, b_mod[None, :])


def kernel(x, c, ctx, c_ctx, w_mod, b_mod, w_in, gla_w_gate, gla_b_gate, gla_norm_g, hy_conv_w, hy_conv_b, hy_f_w1, hy_f_b1, hy_f_freq1, hy_f_w2, hy_f_b2, hy_f_freq2, hy_f_w3, hy_f_b3, hy_skip, hy_norm_g, mla_q_norm_g, mla_w_uq, mla_kv_norm_g, mla_w_ukv, mla_norm_g, w_out, ln_g, ln_b, ffn_w1, ffn_w3, ffn_w2, moe_router, moe_w1, moe_w3, moe_w2):
    B, L, D = x.shape
    Lc = ctx.shape[1]
    depth = w_mod.shape[0]
    alpha = (2.0 * depth) ** 0.25
    cc = jnp.zeros((8, D), F32).at[:B].set(c).at[B].set(c_ctx)
    cos, sin = _rope_tables(L, True)
    cos_c, sin_c = _rope_tables(Lc, False)
    cos_all, sin_all = jnp.concatenate([cos_c, cos], axis=0), jnp.concatenate([sin_c, sin], axis=0)
    KD, VD = GLA_HEADS * GLA_DK, GLA_HEADS * GLA_DV
    xc = ctx
    for l in range(depth):
        need_ctx = l < depth - 1
        mods = _modulation(cc, w_mod[l], b_mod[l])
        m = [mods[:B, k * D:(k + 1) * D][:, None, :] for k in range(6)]
        mc = [jnp.broadcast_to(mods[B, k * D:(k + 1) * D][None, None, :], (B, 1, D)) for k in range(6)]
        w_arr = _arrange_w_in(w_in[l])
        wg, bg = _arrange_gate(gla_w_gate[l], gla_b_gate[l])
        filt = (hy_f_w1[l], hy_f_b1[l], hy_f_freq1[l], hy_f_w2[l], hy_f_b2[l], hy_f_freq2[l], hy_f_w3[l], hy_f_b3[l])
        wq = _arrange_wq(mla_w_uq[l])
        wk, wv = _arrange_wkv(mla_w_ukv[l])

        hyu, qk, vg, alr, cq, ckvr = _inproj(x, m[0], m[1], w_arr)
        hyu_c, qk_c, vg_c, alr_c, cq_c, ckvr_c = _inproj(xc, mc[0], mc[1], w_arr)

        of_c, ob_c, s_c = _gla(qk_c, vg_c, alr_c, wg, bg, jnp.zeros((B, 2, KD, VD), F32))
        of, ob, _ = _gla(qk, vg, alr, wg, bg, s_c)
        hy = _hyena(hyu, hy_conv_w[l], hy_conv_b[l], filt, hy_skip[l])
        k_all, v_all = _kvproj(jnp.concatenate([ckvr_c, ckvr], axis=1), mla_kv_norm_g[l], wk, wv, cos_all, sin_all)
        k_c, v_c = k_all[:, :Lc], v_all[:, :Lc]
        q_m = _qproj(cq, mla_q_norm_g[l], wq, cos, sin)
        om = _flash(q_m, k_all, v_all)

        x = _outproj(of, ob, vg, hy, om, x, m[2], gla_norm_g[l], hy_norm_g[l], mla_norm_g[l], w_out[l],
                     ln_g[l, 0], ln_b[l, 0], alpha)
        if need_ctx:
            hy_c = _hyena_ctx(hyu_c, hy_conv_w[l], hy_conv_b[l], filt, hy_skip[l])
            q_c = _qproj(cq_c, mla_q_norm_g[l], wq, cos_c, sin_c)
            om_c = _flash(q_c, k_c, v_c)
            xc = _outproj(of_c, ob_c, vg_c, hy_c, om_c, xc, mc[2], gla_norm_g[l], hy_norm_g[l], mla_norm_g[l],
                          w_out[l], ln_g[l, 0], ln_b[l, 0], alpha)

        i = l // 2
        if l % 2 == 0:
            x = _ffn(x, m[3], m[4], m[5], ffn_w1[i], ffn_w3[i], ffn_w2[i], ln_g[l, 1], ln_b[l, 1], alpha)
            if need_ctx:
                xc = _ffn(xc, mc[3], mc[4], mc[5], ffn_w1[i], ffn_w3[i], ffn_w2[i], ln_g[l, 1], ln_b[l, 1], alpha)
        else:
            x = _moe(x, m[3], m[4], m[5], moe_router[i], moe_w1[i], moe_w3[i], moe_w2[i], ln_g[l, 1], ln_b[l, 1], alpha)
            if need_ctx:
                xc = _moe(xc, mc[3], mc[4], mc[5], moe_router[i], moe_w1[i], moe_w3[i], moe_w2[i], ln_g[l, 1],
                          ln_b[l, 1], alpha)
    return x
```

```python
import functools
import math

import numpy as np
import jax
import jax.numpy as jnp
from jax import lax
from jax.experimental import pallas as pl
from jax.experimental.pallas import tpu as pltpu

F32 = jnp.float32
BF16 = jnp.bfloat16
HI = lax.Precision.HIGHEST

GRID_W = 64
GLA_HEADS, GLA_DK, GLA_DV, GLA_RANK, GLA_TAU = 4, 32, 64, 16, 16.0
HY_CH, HY_EMB = 256, 33
HY_DECAY_TARGET, HY_FAST_DECAY, HY_SLOW_DECAY = 1e-2, 0.3, 1.5
MLA_HEADS, MLA_Q_RANK, MLA_KV_RANK, MLA_NOPE, MLA_ROPE, MLA_V = 8, 256, 128, 64, 32, 64
MLA_SCALE = (MLA_NOPE + MLA_ROPE) ** -0.5
ROPE_BASE = 10000.0
N_EXPERTS = 8
IN_SPLITS = (128, 128, 256, 256, 32, 768, 256, 128, 32)

LANES = 128
SUBLANES = 8
VMEM_LIMIT = 56 * 1024 * 1024

GLA_CHUNK = 128
DFT_N2 = 256


def _cp(*sem):
    return pltpu.CompilerParams(dimension_semantics=sem, vmem_limit_bytes=VMEM_LIMIT)


def _full(shape):
    n = len(shape)
    return pl.BlockSpec(shape, lambda *_: (0,) * n)


def _idiv(x, d):
    assert d & (d - 1) == 0
    return lax.shift_right_logical(x, int(math.log2(d)))


INPROJ_WIDTHS = (768, 256, 512, 128, 256, 256)


def _arrange_w_in(w):
    cuts = np.cumsum(IN_SPLITS)[:-1]
    qa, ka, va, ga, alr, hyu, cq, ckv, kr = jnp.split(w, [int(c) for c in cuts], axis=1)
    z96 = jnp.zeros((w.shape[0], 96), w.dtype)
    return jnp.concatenate([hyu, qa, ka, va, ga, alr, z96, cq, ckv, kr, z96], axis=1).astype(BF16)


def _inproj_body(x_ref, sh_ref, sc_ref, w_ref, *out_refs):
    h = x_ref[0] * (1.0 + sc_ref[0]) + sh_ref[0]
    acc = jnp.dot(h.astype(BF16), w_ref[...], preferred_element_type=F32)
    off = 0
    for r in out_refs:
        w = r.shape[-1]
        r[0] = acc[:, off:off + w]
        off += w


def _inproj(x, shift, scale, w_arr):
    B, L, D = x.shape
    tm = min(512, L)
    n = w_arr.shape[1]
    row = lambda w: pl.BlockSpec((1, tm, w), lambda b, i: (b, i, 0))
    vec = pl.BlockSpec((1, 1, D), lambda b, i: (b, 0, 0))
    return pl.pallas_call(
        _inproj_body,
        grid=(B, L // tm),
        in_specs=[row(D), vec, vec, _full((D, n))],
        out_specs=[row(w) for w in INPROJ_WIDTHS],
        out_shape=[jax.ShapeDtypeStruct((B, L, w), F32) for w in INPROJ_WIDTHS],
        compiler_params=_cp("parallel", "parallel"),
        name="inproj",
    )(x, shift, scale, w_arr)


def _log_sigmoid(z):
    return jnp.minimum(z, 0.0) - jnp.log1p(jnp.exp(-jnp.abs(z)))


def _gla_body(qkf_ref, vf_ref, af_ref, qkb_ref, vb_ref, ab_ref, wg_ref, bg_ref, s0_ref,
              of_ref, ob_ref, sout_ref, s_ref):
    i = pl.program_id(0)
    C = qkf_ref.shape[1]
    KD = GLA_HEADS * GLA_DK
    VD = GLA_HEADS * GLA_DV

    @pl.when(i == 0)
    def _():
        s_ref[...] = s0_ref[...]

    r = lax.broadcasted_iota(jnp.int32, (C, C), 0)
    c = lax.broadcasted_iota(jnp.int32, (C, C), 1)
    tris = ((c <= r).astype(F32), (c >= r).astype(F32))
    lane_k = _idiv(lax.broadcasted_iota(jnp.int32, (1, KD), 1), GLA_DK)
    lane_v = _idiv(lax.broadcasted_iota(jnp.int32, (1, VD), 1), GLA_DV)
    rk = _idiv(lax.broadcasted_iota(jnp.int32, (KD, VD), 0), GLA_DK)
    cv = _idiv(lax.broadcasted_iota(jnp.int32, (KD, VD), 1), GLA_DV)
    ones = jnp.ones((C, VD), F32)
    refs = ((qkf_ref, vf_ref, af_ref, of_ref), (qkb_ref, vb_ref, ab_ref, ob_ref))
    chains = [(b, d) for b in range(qkf_ref.shape[0]) for d in range(2)]

    z = [jnp.dot(refs[d][2][b], wg_ref[...], precision=HI, preferred_element_type=F32) + bg_ref[...] for b, d in chains]
    la = [_log_sigmoid(zz[:, d * KD:(d + 1) * KD]) / GLA_TAU for zz, (b, d) in zip(z, chains)]
    bb = [jnp.dot(tris[d], l_, precision=HI, preferred_element_type=F32) for l_, (b, d) in zip(la, chains)]
    tot_b = [lax.dot_general(l_, ones, (((0,), (0,)), ((), ())), precision=HI, preferred_element_type=F32) for l_ in la]
    qe, ke, kl, vb, s_old = [], [], [], [], []
    for n, (b, d) in enumerate(chains):
        qk = refs[d][0][b]
        q = qk[:, :KD] * (GLA_DK ** -0.5)
        k = qk[:, KD:]
        tot = jnp.sum(la[n], axis=0, keepdims=True)
        qe.append(q * jnp.exp(bb[n]))
        ke.append((k * jnp.exp(-bb[n])).astype(BF16))
        kl.append((k * jnp.exp(tot - bb[n])).astype(BF16))
        vb.append(refs[d][1][b].astype(BF16))
        s_old.append(s_ref[2 * b + d])
    o = [jnp.dot(qe[n].astype(BF16), s_old[n].astype(BF16), preferred_element_type=F32) for n in range(len(chains))]
    att = [[lax.dot_general(jnp.where(lane_k == h, qe[n], 0.0).astype(BF16), ke[n], (((1,), (1,)), ((), ())),
                            preferred_element_type=F32) for h in range(GLA_HEADS)] for n in range(len(chains))]
    kv = [lax.dot_general(kl[n], vb[n], (((0,), (0,)), ((), ())), preferred_element_type=F32) for n in range(len(chains))]
    for n, (b, d) in enumerate(chains):
        on = o[n]
        for h in range(GLA_HEADS):
            oh = jnp.dot((att[n][h] * tris[d]).astype(BF16), vb[n], preferred_element_type=F32)
            on = on + jnp.where(lane_v == h, oh, 0.0)
        refs[d][3][b] = on
        s_ref[2 * b + d] = jnp.exp(tot_b[n]) * s_old[n] + jnp.where(rk == cv, kv[n], 0.0)

    @pl.when(i == pl.num_programs(0) - 1)
    def _():
        sout_ref[...] = s_ref[...]


def _gla(qk, vg, alr, wg, bg, s0):
    B, L, _ = qk.shape
    C = min(GLA_CHUNK, L)
    n = L // C
    KD, VD = GLA_HEADS * GLA_DK, GLA_HEADS * GLA_DV
    fwd = lambda w: pl.BlockSpec((B, C, w), lambda i: (0, i, 0))
    bwd = lambda w: pl.BlockSpec((B, C, w), lambda i: (0, n - 1 - i, 0))
    st = _full((2 * B, KD, VD))
    of, ob, s_out = pl.pallas_call(
        _gla_body,
        grid=(n,),
        in_specs=[fwd(2 * KD), fwd(VD), fwd(LANES), bwd(2 * KD), bwd(VD), bwd(LANES),
                  _full((LANES, 2 * KD)), _full((1, 2 * KD)), st],
        out_specs=[fwd(VD), bwd(VD), st],
        out_shape=[jax.ShapeDtypeStruct((B, L, VD), F32), jax.ShapeDtypeStruct((B, L, VD), F32),
                   jax.ShapeDtypeStruct((2 * B, KD, VD), F32)],
        scratch_shapes=[pltpu.VMEM((2 * B, KD, VD), F32)],
        compiler_params=_cp("arbitrary"),
        name="gla",
    )(qk, vg, alr, qk, vg, alr, wg, bg, s0.reshape(2 * B, KD, VD))
    return of, ob, s_out.reshape(B, 2, KD, VD)


def _arrange_gate(w_gate, b_gate):
    KD = GLA_HEADS * GLA_DK
    wg = jnp.zeros((LANES, 2 * KD), F32)
    wg = wg.at[:GLA_RANK, :KD].set(w_gate[0]).at[GLA_RANK:2 * GLA_RANK, KD:].set(w_gate[1])
    return wg, jnp.concatenate([b_gate[0], b_gate[1]])[None, :]


def _shortconv_body(x_ref, p_ref, n_ref, w_ref, b_ref, v_ref, x1_ref, x2_ref):
    i = pl.program_id(1)
    last = pl.num_programs(1) - 1
    x = x_ref[0]
    tm = x.shape[0]
    prev = jnp.where(i > 0, p_ref[0][7:8, :], 0.0)
    nxt = jnp.where(i < last, n_ref[0][0:1, :], 0.0)
    rid = lax.broadcasted_iota(jnp.int32, x.shape, 0)
    dn = jnp.where(rid == 0, prev, pltpu.roll(x, 1, 0))
    up = jnp.where(rid == tm - 1, nxt, pltpu.roll(x, tm - 1, 0))
    w = w_ref[...]
    y = b_ref[...] + dn * w[0:1] + x * w[1:2] + up * w[2:3]
    v_ref[0] = y[:, :HY_CH]
    x1_ref[0] = y[:, HY_CH:2 * HY_CH]
    x2_ref[0] = y[:, 2 * HY_CH:]


def _shortconv(u, w, b):
    B, L, W = u.shape
    tm = min(512, L)
    nb = tm // 8
    row = pl.BlockSpec((1, tm, W), lambda b_, i: (b_, i, 0))
    prev = pl.BlockSpec((1, 8, W), lambda b_, i: (b_, jnp.maximum(i * nb - 1, 0), 0))
    nxt = pl.BlockSpec((1, 8, W), lambda b_, i: (b_, jnp.minimum((i + 1) * nb, L // 8 - 1), 0))
    o = pl.BlockSpec((1, tm, HY_CH), lambda b_, i: (b_, i, 0))
    return pl.pallas_call(
        _shortconv_body,
        grid=(B, L // tm),
        in_specs=[row, prev, nxt, _full((3, W)), _full((1, W))],
        out_specs=[o, o, o],
        out_shape=[jax.ShapeDtypeStruct((B, L, HY_CH), F32)] * 3,
        compiler_params=_cp("parallel", "parallel"),
        name="shortconv",
    )(u, u, u, w, b[None, :])


def _filter_feats(L):
    pos = jnp.arange(L, dtype=F32)
    t = pos / (L - 1)
    bands = (HY_EMB - 1) // 2
    freqs = jnp.linspace(1e-4, bands - 1, bands, dtype=F32)
    ang = (2.0 * math.pi * pos / L)[:, None] * freqs
    z = jnp.concatenate([t[:, None], jnp.cos(ang), -jnp.sin(ang)], axis=-1)
    z = jnp.pad(z, ((0, 0), (0, LANES - HY_EMB)))
    deltas = jnp.abs(jnp.linspace(math.log(HY_DECAY_TARGET) / HY_SLOW_DECAY,
                                  math.log(HY_DECAY_TARGET) / HY_FAST_DECAY, HY_CH, dtype=F32))
    return z, jnp.tile(deltas, 4)[None, :]


def _filter_body(z_ref, w1_ref, b1_ref, f1_ref, w2_ref, b2_ref, f2_ref, w3_ref, b3_ref, dl_ref,
                 h_ref, ss_ref, *, L):
    i = pl.program_id(0)
    z = z_ref[...]
    tm = z.shape[0]
    hid = jnp.sin(f1_ref[...] * (jnp.dot(z, w1_ref[...], precision=HI, preferred_element_type=F32) + b1_ref[...]))
    hid = jnp.sin(f2_ref[...] * (jnp.dot(hid, w2_ref[...], precision=HI, preferred_element_type=F32) + b2_ref[...]))
    h = jnp.dot(hid, w3_ref[...], precision=HI, preferred_element_type=F32) + b3_ref[...]
    pos = (lax.broadcasted_iota(jnp.int32, (tm, 1), 0) + i * tm).astype(F32)
    t = pos / (L - 1)
    h = h * jnp.exp(-t * dl_ref[...])

    @pl.when(i == 0)
    def _():
        ss_ref[...] = jnp.zeros_like(ss_ref)

    ss_ref[...] += jnp.sum(h * h, axis=0, keepdims=True)
    col = lax.broadcasted_iota(jnp.int32, h.shape, 1)
    is_bwd = (_idiv(col, HY_CH) & 1) == 1
    h_ref[...] = jnp.where(jnp.logical_and(is_bwd, pos == 0.0), 0.0, h)


def _filters(L, fw1, fb1, ff1, fw2, fb2, ff2, fw3, fb3):
    z, dl = _filter_feats(L)
    tm = min(1024, L)
    Hf = fw2.shape[0]
    w1 = jnp.pad(fw1, ((0, LANES - HY_EMB), (0, 0)))
    NC = fw3.shape[1]
    return pl.pallas_call(
        functools.partial(_filter_body, L=L),
        grid=(L // tm,),
        in_specs=[pl.BlockSpec((tm, LANES), lambda i: (i, 0)), _full((LANES, Hf)), _full((1, Hf)), _full((1, Hf)),
                  _full((Hf, Hf)), _full((1, Hf)), _full((1, Hf)), _full((Hf, NC)), _full((1, NC)), _full((1, NC))],
        out_specs=[pl.BlockSpec((tm, NC), lambda i: (i, 0)), _full((1, NC))],
        out_shape=[jax.ShapeDtypeStruct((L, NC), F32), jax.ShapeDtypeStruct((1, NC), F32)],
        compiler_params=_cp("arbitrary"),
        name="hy_filters",
    )(z, w1, fb1[None], ff1[None], fw2, fb2[None], ff2[None], fw3, fb3[None], dl)


def _dft_consts(L):
    N = 2 * L
    N2 = DFT_N2
    N1 = N // N2
    half = N1 // 2
    k1 = np.arange(N1)[:, None].astype(np.float64)
    n1 = np.arange(N1)[None, :].astype(np.float64)
    a1 = 2.0 * np.pi * k1 * n1 / N1
    f1r, f1i = np.cos(a1), -np.sin(a1)
    fa = np.concatenate([f1r[:, :half], f1i[:, :half]], axis=0)
    fb = np.concatenate([f1r[:half, :], f1i[:half, :]], axis=1) / N
    k2 = np.arange(N2)[:, None].astype(np.float64)
    n2 = np.arange(N2)[None, :].astype(np.float64)
    a2 = 2.0 * np.pi * k2 * n2 / N2
    f2r, f2i = np.cos(a2), -np.sin(a2)
    g = np.block([[f2r, -f2i], [f2i, f2r]])
    gc = np.block([[f2r, f2i], [-f2i, f2r]])
    at = 2.0 * np.pi * (np.arange(N1)[:, None] * np.arange(N2)[None, :] % N) / N
    twr, twi = np.cos(at), -np.sin(at)
    c = lambda a: jnp.asarray(a, dtype=F32)
    bc = lambda a: jnp.broadcast_to(c(a)[:, :, None], (N1, N2, LANES))
    eye = np.eye(SUBLANES)
    return dict(N1=N1, N2=N2, half=half, fa=c(np.kron(fa, eye)), fb=c(np.kron(fb, eye)), g=c(g), gc=c(gc),
                twr=bc(twr), twi=bc(twi))


def _lanes(t, width):
    return jnp.concatenate([t] * (width // LANES), axis=-1)


def _dft1_body(f_ref, x_ref, o_ref):
    x = x_ref[0]
    x2 = x.reshape(x.shape[0] * SUBLANES, x.shape[2]).astype(BF16)
    y = jnp.dot(f_ref[...], x2, preferred_element_type=F32)
    o_ref[0] = y.reshape(o_ref.shape[1], SUBLANES, y.shape[1])


def _dft_stage1(fa, x):
    B, half, N2, W = x.shape
    R = fa.shape[0] // SUBLANES
    return pl.pallas_call(
        _dft1_body,
        grid=(B, N2 // SUBLANES),
        in_specs=[_full(fa.shape), pl.BlockSpec((1, half, SUBLANES, W), lambda b, j: (b, 0, j, 0))],
        out_specs=pl.BlockSpec((1, R, SUBLANES, W), lambda b, j: (b, 0, j, 0)),
        out_shape=jax.ShapeDtypeStruct((B, R, N2, W), F32),
        compiler_params=_cp("parallel", "parallel"),
        name="hy_dft1",
    )(fa.astype(BF16), x)


def _filter_spec_body(a_ref, twr_ref, twi_ref, g_ref, ss_ref, hf_ref):
    W = a_ref.shape[-1]
    ar, ai = a_ref[0, 0], a_ref[1, 0]
    twr, twi = _lanes(twr_ref[0], W), _lanes(twi_ref[0], W)
    xr = ar * twr - ai * twi
    xi = ar * twi + ai * twr
    z = jnp.dot(g_ref[...], jnp.concatenate([xr, xi], axis=0).astype(BF16), preferred_element_type=F32)
    n2 = z.shape[0] // 2
    zr, zi = z[:n2], z[n2:]
    ss = ss_ref[...]
    for o in range(2):
        f0, b0 = (2 * o) * HY_CH, (2 * o + 1) * HY_CH
        sc = lax.rsqrt(ss[:, f0:f0 + HY_CH] + ss[:, b0:b0 + HY_CH] + 1e-6)
        hf_ref[o, 0, 0] = (zr[:, f0:f0 + HY_CH] + zr[:, b0:b0 + HY_CH]) * sc
        hf_ref[o, 0, 1] = (zi[:, f0:f0 + HY_CH] - zi[:, b0:b0 + HY_CH]) * sc


def _filter_spectrum(h, ss, dc):
    L, NC = h.shape
    N1, N2, half = dc["N1"], dc["N2"], dc["half"]
    a = _dft_stage1(dc["fa"], h.reshape(1, half, N2, NC))
    a = a.reshape(2, N1, N2, NC)
    return pl.pallas_call(
        _filter_spec_body,
        grid=(N1,),
        in_specs=[pl.BlockSpec((2, 1, N2, NC), lambda k: (0, k, 0, 0)),
                  pl.BlockSpec((1, N2, LANES), lambda k: (k, 0, 0)), pl.BlockSpec((1, N2, LANES), lambda k: (k, 0, 0)),
                  _full((2 * N2, 2 * N2)), _full((1, NC))],
        out_specs=pl.BlockSpec((2, 1, 2, N2, HY_CH), lambda k: (0, k, 0, 0, 0)),
        out_shape=jax.ShapeDtypeStruct((2, N1, 2, N2, HY_CH), F32),
        compiler_params=_cp("parallel"),
        name="hy_filter_spec",
    )(a, dc["twr"], dc["twi"], dc["g"].astype(BF16), ss)


SPEC_K1 = 4


def _spec_mul_body(a_ref, twr_ref, twi_ref, g_ref, gc_ref, hf_ref, o_ref):
    W = a_ref.shape[-1]
    ks = range(a_ref.shape[2])
    n2 = g_ref.shape[0] // 2
    x = []
    for k in ks:
        ar, ai = a_ref[0, 0, k], a_ref[0, 1, k]
        twr, twi = _lanes(twr_ref[k], W), _lanes(twi_ref[k], W)
        x.append(jnp.concatenate([ar * twr - ai * twi, ar * twi + ai * twr], axis=0).astype(BF16))
    z = [jnp.dot(g_ref[...], x_, preferred_element_type=F32) for x_ in x]
    y = []
    for k, z_ in zip(ks, z):
        zr, zi = z_[:n2], z_[n2:]
        hr, hi = hf_ref[0, k, 0], hf_ref[0, k, 1]
        y.append(jnp.concatenate([zr * hr - zi * hi, zr * hi + zi * hr], axis=0).astype(BF16))
    b = [jnp.dot(gc_ref[...], y_, preferred_element_type=F32) for y_ in y]
    for k, b_ in zip(ks, b):
        br, bi = b_[:n2], b_[n2:]
        twr, twi = _lanes(twr_ref[k], W), _lanes(twi_ref[k], W)
        o_ref[0, 0, k] = br * twr + bi * twi
        o_ref[0, 1, k] = bi * twr - br * twi


def _spec_mul(a, hf, order, dc):
    B = a.shape[0]
    N1, N2 = dc["N1"], dc["N2"]
    C = a.shape[-1]
    kb = min(SPEC_K1, N1)
    blk = pl.BlockSpec((1, 2, kb, N2, C), lambda b, k: (b, 0, k, 0, 0))
    tw = pl.BlockSpec((kb, N2, LANES), lambda b, k: (k, 0, 0))
    return pl.pallas_call(
        _spec_mul_body,
        grid=(B, N1 // kb),
        in_specs=[blk, tw, tw, _full((2 * N2, 2 * N2)), _full((2 * N2, 2 * N2)),
                  pl.BlockSpec((1, kb, 2, N2, C), lambda b, k: (order, k, 0, 0, 0))],
        out_specs=blk,
        out_shape=jax.ShapeDtypeStruct(a.shape, F32),
        compiler_params=_cp("parallel", "parallel"),
        name="hy_spec_mul",
    )(a, dc["twr"], dc["twi"], dc["g"].astype(BF16), dc["gc"].astype(BF16), hf)


def _dft3_body(f_ref, b_ref, u_ref, gate_ref, skip_ref, o_ref):
    bm = b_ref[0]
    b2 = bm.reshape(bm.shape[0] * SUBLANES, bm.shape[2]).astype(BF16)
    y = jnp.dot(f_ref[...], b2, preferred_element_type=F32)
    rows, C = y.shape
    u = u_ref[0].reshape(rows, C)
    gate = gate_ref[0].reshape(rows, C)
    o_ref[0] = (gate * (y + u * skip_ref[...])).reshape(o_ref.shape[1], SUBLANES, C)


def _dft_stage3(fb, bm, u, gate, skip):
    B, R, N2, C = bm.shape
    half = u.shape[1]
    row = pl.BlockSpec((1, half, SUBLANES, C), lambda b, j: (b, 0, j, 0))
    return pl.pallas_call(
        _dft3_body,
        grid=(B, N2 // SUBLANES),
        in_specs=[_full(fb.shape), pl.BlockSpec((1, R, SUBLANES, C), lambda b, j: (b, 0, j, 0)), row, row, _full((1, C))],
        out_specs=row,
        out_shape=jax.ShapeDtypeStruct((B, half, N2, C), F32),
        compiler_params=_cp("parallel", "parallel"),
        name="hy_dft3",
    )(fb.astype(BF16), bm, u, gate, skip[None, :])


def _longconv_gated(u, gate, hf, order, skip, dc):
    B, L, C = u.shape
    N1, N2, half = dc["N1"], dc["N2"], dc["half"]
    u4 = u.reshape(B, half, N2, C)
    a = _dft_stage1(dc["fa"], u4).reshape(B, 2, N1, N2, C)
    bm = _spec_mul(a, hf, order, dc).reshape(B, 2 * N1, N2, C)
    return _dft_stage3(dc["fb"], bm, u4, gate.reshape(B, half, N2, C), skip).reshape(B, L, C)


def _hyena(hyu, conv_w, conv_b, filt, skip):
    B, L, _ = hyu.shape
    v, x1, x2 = _shortconv(hyu, conv_w, conv_b)
    h, ss = _filters(L, *filt)
    dc = _dft_consts(L)
    hf = _filter_spectrum(h, ss, dc)
    z1 = _longconv_gated(v, x1, hf, 0, skip[0], dc)
    return _longconv_gated(z1, x2, hf, 1, skip[1], dc)


def _hyena_ctx_body(v_ref, x1_ref, x2_ref, h_ref, ss_ref, skip_ref, fc_ref, gc_ref, o_ref):
    fc, gc = fc_ref[...], gc_ref[...]
    n = fc.shape[0] // 2
    ss = ss_ref[...]
    h = h_ref[...]

    def conv(u, o):
        f0, b0 = (2 * o) * HY_CH, (2 * o + 1) * HY_CH
        sc = lax.rsqrt(ss[:, f0:f0 + HY_CH] + ss[:, b0:b0 + HY_CH] + 1e-6)
        x = jnp.dot(fc, u, precision=HI, preferred_element_type=F32)
        hf = jnp.dot(fc, h[:, f0:f0 + HY_CH], precision=HI, preferred_element_type=F32)
        hb = jnp.dot(fc, h[:, b0:b0 + HY_CH], precision=HI, preferred_element_type=F32)
        hr = (hf[:n] + hb[:n]) * sc
        hi = (hf[n:] - hb[n:]) * sc
        yr = x[:n] * hr - x[n:] * hi
        yi = x[:n] * hi + x[n:] * hr
        y = jnp.dot(gc, jnp.concatenate([yr, yi], axis=0), precision=HI, preferred_element_type=F32)
        return y + u * skip_ref[o:o + 1, :]

    z1 = x1_ref[0] * conv(v_ref[0], 0)
    o_ref[0] = x2_ref[0] * conv(z1, 1)


def _hyena_ctx(hyu, conv_w, conv_b, filt, skip):
    B, L, _ = hyu.shape
    v, x1, x2 = _shortconv(hyu, conv_w, conv_b)
    h, ss = _filters(L, *filt)
    N = 2 * L
    ang = 2.0 * np.pi * (np.arange(N)[:, None] * np.arange(L)[None, :] % N) / N
    fr, fi = np.cos(ang), -np.sin(ang)
    fc = jnp.asarray(np.concatenate([fr, fi], axis=0), dtype=F32)
    gc = jnp.asarray(np.concatenate([fr.T, fi.T], axis=1) / N, dtype=F32)
    row = pl.BlockSpec((1, L, HY_CH), lambda b: (b, 0, 0))
    return pl.pallas_call(
        _hyena_ctx_body,
        grid=(B,),
        in_specs=[row, row, row, _full(h.shape), _full(ss.shape), _full(skip.shape), _full(fc.shape), _full(gc.shape)],
        out_specs=row,
        out_shape=jax.ShapeDtypeStruct((B, L, HY_CH), F32),
        compiler_params=_cp("parallel"),
        name="hyena_ctx",
    )(v, x1, x2, h, ss, skip, fc, gc)


HEAD_PAD = 128


def _rope_swap(w):
    a, b, c, d = w[..., 0:8], w[..., 8:16], w[..., 16:24], w[..., 24:32]
    return jnp.concatenate([-b, a, -d, c], axis=-1)


def _arrange_wq(w_uq):
    R = w_uq.shape[0]
    w = w_uq.reshape(R, MLA_HEADS, MLA_NOPE + MLA_ROPE)
    rope = w[..., MLA_NOPE:]
    out = jnp.concatenate([w[..., :MLA_NOPE], rope, _rope_swap(rope)], axis=-1)
    return out.reshape(R, MLA_HEADS * HEAD_PAD).astype(BF16)


def _arrange_wkv(w_ukv):
    R = w_ukv.shape[0]
    w = w_ukv.reshape(R, MLA_HEADS, MLA_NOPE + MLA_V)
    wk = jnp.concatenate([w[..., :MLA_NOPE], jnp.zeros((R, MLA_HEADS, HEAD_PAD - MLA_NOPE), w.dtype)], axis=-1)
    wv = w[..., MLA_NOPE:]
    return wk.reshape(R, MLA_HEADS * HEAD_PAD).astype(BF16), wv.reshape(R, MLA_HEADS * MLA_V).astype(BF16)


def _kr_place():
    e = np.zeros((LANES, MLA_HEADS * HEAD_PAD), np.float32)
    es = np.zeros((LANES, MLA_HEADS * HEAD_PAD), np.float32)
    for h in range(MLA_HEADS):
        base = h * HEAD_PAD + MLA_NOPE
        for j in range(MLA_ROPE):
            e[j, base + j] = 1.0
            blk, r = divmod(j, 16)
            if r < 8:
                es[16 * blk + r + 8, base + j] = -1.0
            else:
                es[16 * blk + r - 8, base + j] = 1.0
    return jnp.asarray(e).astype(BF16), jnp.asarray(es).astype(BF16)


def _rope_tables(L, rope):
    if rope:
        t = np.arange(L)
        row, col = (t // GRID_W).astype(np.float32), (t % GRID_W).astype(np.float32)
        half = MLA_ROPE // 2
        inv = ROPE_BASE ** (-jnp.arange(0, half, 2, dtype=F32) / half)
        ar = jnp.asarray(row)[:, None] * inv
        ac = jnp.asarray(col)[:, None] * inv
        cos = jnp.concatenate([jnp.cos(ar), jnp.cos(ar), jnp.cos(ac), jnp.cos(ac)], axis=-1)
        sin = jnp.concatenate([jnp.sin(ar), jnp.sin(ar), jnp.sin(ac), jnp.sin(ac)], axis=-1)
    else:
        cos, sin = jnp.ones((L, MLA_ROPE), F32), jnp.zeros((L, MLA_ROPE), F32)
    return cos, sin


def _rms_rows(x, g, eps=1e-6):
    return x * lax.rsqrt(jnp.mean(x * x, axis=-1, keepdims=True) + eps) * g


def _qproj_body(cq_ref, g_ref, w_ref, t1_ref, t2_ref, q_ref):
    xn = _rms_rows(cq_ref[0], g_ref[...])
    acc = jnp.dot(xn.astype(BF16), w_ref[...], preferred_element_type=F32)
    W = acc.shape[1]
    t1, t2 = _lanes(t1_ref[...], W), _lanes(t2_ref[...], W)
    q_ref[0] = (acc * t1 + pltpu.roll(acc, W - MLA_ROPE, 1) * t2).astype(q_ref.dtype)


def _qproj(cq, g, wq, cos, sin):
    B, L, R = cq.shape
    tm = min(512, L)
    W = wq.shape[1]
    ones, zeros = jnp.ones((L, MLA_NOPE), F32), jnp.zeros((L, MLA_ROPE), F32)
    qs = MLA_SCALE * math.log2(math.e)
    t1 = jnp.concatenate([ones, cos, zeros], axis=-1) * qs
    t2 = jnp.concatenate([jnp.zeros((L, MLA_NOPE), F32), sin, zeros], axis=-1) * qs
    tab = pl.BlockSpec((tm, HEAD_PAD), lambda b, i: (i, 0))
    return pl.pallas_call(
        _qproj_body,
        grid=(B, L // tm),
        in_specs=[pl.BlockSpec((1, tm, R), lambda b, i: (b, i, 0)), _full((1, R)), _full((R, W)), tab, tab],
        out_specs=pl.BlockSpec((1, tm, W), lambda b, i: (b, i, 0)),
        out_shape=jax.ShapeDtypeStruct((B, L, W), BF16),
        compiler_params=_cp("parallel", "parallel"),
        name="mla_qproj",
    )(cq, g[None, :], wq, t1, t2)


def _kvproj_body(c_ref, g_ref, wk_ref, wv_ref, e_ref, es_ref, cos_ref, sin_ref, k_ref, v_ref):
    c = c_ref[0]
    R = MLA_KV_RANK
    xn = _rms_rows(c[:, :R], g_ref[...]).astype(BF16)
    kr = c[:, R:]
    acc = jnp.dot(xn, wk_ref[...], preferred_element_type=F32)
    acc += jnp.dot((kr * cos_ref[...]).astype(BF16), e_ref[...], preferred_element_type=F32)
    acc += jnp.dot((kr * sin_ref[...]).astype(BF16), es_ref[...], preferred_element_type=F32)
    k_ref[0] = acc.astype(k_ref.dtype)
    v_ref[0] = jnp.dot(xn, wv_ref[...], preferred_element_type=F32).astype(v_ref.dtype)


def _kvproj(ckvr, g, wk, wv, cos, sin):
    B, L, Wc = ckvr.shape
    tm = next(t for t in (1280, 512, 256, L) if L % t == 0)
    pad = jnp.zeros((L, LANES - MLA_ROPE), F32)
    cos_p, sin_p = jnp.concatenate([cos, pad], axis=-1), jnp.concatenate([sin, pad], axis=-1)
    e, es = _kr_place()
    tab = pl.BlockSpec((tm, LANES), lambda b, i: (i, 0))
    Wk, Wv = wk.shape[1], wv.shape[1]
    return pl.pallas_call(
        _kvproj_body,
        grid=(B, L // tm),
        in_specs=[pl.BlockSpec((1, tm, Wc), lambda b, i: (b, i, 0)), _full((1, MLA_KV_RANK)),
                  _full(wk.shape), _full(wv.shape), _full(e.shape), _full(es.shape), tab, tab],
        out_specs=[pl.BlockSpec((1, tm, Wk), lambda b, i: (b, i, 0)), pl.BlockSpec((1, tm, Wv), lambda b, i: (b, i, 0))],
        out_shape=[jax.ShapeDtypeStruct((B, L, Wk), BF16), jax.ShapeDtypeStruct((B, L, Wv), BF16)],
        compiler_params=_cp("parallel", "parallel"),
        name="mla_kvproj",
    )(ckvr, g[None, :], wk, wv, e, es, cos_p, sin_p)


FLASH_ROWS = 256
FLASH_KEYS = 256


def _flash_body(q_ref, k_ref, v_ref, o_ref, m_ref, l_ref, acc_ref, s_ref, *, R):
    j = pl.program_id(3)
    tq, tk = q_ref.shape[1], k_ref.shape[1]
    CK = FLASH_KEYS
    npc = CK // LANES

    @pl.when(j == 0)
    def _():
        m_ref[...] = jnp.full_like(m_ref, -jnp.inf)
        l_ref[...] = jnp.zeros_like(l_ref)
        acc_ref[...] = jnp.zeros_like(acc_ref)

    def pass1(a, r):
        lo, r0 = a * HEAD_PAD, r * R
        q = q_ref[0, r0:r0 + R, lo:lo + HEAD_PAD]
        mp = None
        for c in range(tk // CK):
            kc = k_ref[0, c * CK:(c + 1) * CK, lo:lo + HEAD_PAD]
            s = lax.dot_general(q, kc, (((1,), (1,)), ((), ())), preferred_element_type=F32)
            s_ref[r0:r0 + R, c * CK:(c + 1) * CK] = s
            for w in range(npc):
                pc = s[:, w * LANES:(w + 1) * LANES]
                mp = pc if mp is None else jnp.maximum(mp, pc)
        m_old = m_ref[a, r0:r0 + R, :]
        return m_old, jnp.maximum(m_old, jnp.max(mp, axis=1, keepdims=True))

    def pass2(a, r, m_old, m_new):
        r0 = r * R
        alpha = jnp.exp2(m_old - m_new)
        lp = jnp.zeros((R, LANES), F32)
        pv = jnp.zeros((R, 2 * MLA_V), F32)
        for c in range(tk // CK):
            s = s_ref[r0:r0 + R, c * CK:(c + 1) * CK]
            ps = [jnp.exp2(s[:, w * LANES:(w + 1) * LANES] - m_new) for w in range(npc)]
            for p_ in ps:
                lp = lp + p_
            p = jnp.concatenate(ps, axis=1).astype(BF16)
            pv = pv + jnp.dot(p, v_ref[0, c * CK:(c + 1) * CK, :], preferred_element_type=F32)
        l_ref[a, r0:r0 + R, :] = alpha * l_ref[a, r0:r0 + R, :] + jnp.sum(lp, axis=1, keepdims=True)
        acc_ref[a, r0:r0 + R, :] = alpha * acc_ref[a, r0:r0 + R, :] + pv
        m_ref[a, r0:r0 + R, :] = m_new

    assert tq // R >= 2
    blocks = [(a, r) for a in range(2) for r in range(tq // R)]
    pend = pass1(*blocks[0])
    for i, blk in enumerate(blocks):
        nxt = pass1(*blocks[i + 1]) if i + 1 < len(blocks) else None
        pass2(*blk, *pend)
        pend = nxt

    @pl.when(j == pl.num_programs(3) - 1)
    def _():
        lane = lax.broadcasted_iota(jnp.int32, acc_ref.shape[1:], 1)
        o_ref[0] = jnp.where(lane < MLA_V, acc_ref[0] / l_ref[0], acc_ref[1] / l_ref[1])


def _flash_tiles(Lq, Lk):
    tq = min(1024, Lq)
    tk = next(t for t in (3328, 1280, 1024, 512, 256, Lk) if Lk % t == 0)
    return tq, tk


def _flash(q, k, v):
    B, Lq, _ = q.shape
    Lk = k.shape[1]
    tq, tk = _flash_tiles(Lq, Lk)
    hp = MLA_HEADS // 2
    return pl.pallas_call(
        functools.partial(_flash_body, R=min(FLASH_ROWS, tq // 2)),
        grid=(B, hp, Lq // tq, Lk // tk),
        in_specs=[pl.BlockSpec((1, tq, 2 * HEAD_PAD), lambda b, h, i, j: (b, i, h)),
                  pl.BlockSpec((1, tk, 2 * HEAD_PAD), lambda b, h, i, j: (b, j, h)),
                  pl.BlockSpec((1, tk, 2 * MLA_V), lambda b, h, i, j: (b, j, h))],
        out_specs=pl.BlockSpec((1, tq, 2 * MLA_V), lambda b, h, i, j: (b, i, h)),
        out_shape=jax.ShapeDtypeStruct((B, Lq, MLA_HEADS * MLA_V), F32),
        scratch_shapes=[pltpu.VMEM((2, tq, LANES), F32), pltpu.VMEM((2, tq, LANES), F32),
                        pltpu.VMEM((2, tq, 2 * MLA_V), F32), pltpu.VMEM((tq, tk), F32)],
        compiler_params=_cp("parallel", "parallel", "parallel", "arbitrary"),
        name="mla_flash",
    )(q, k, v)


def _layernorm_rows(x, g, b, eps=1e-5):
    mu = jnp.mean(x, axis=-1, keepdims=True)
    xc = x - mu
    var = jnp.mean(xc * xc, axis=-1, keepdims=True)
    return xc * lax.rsqrt(var + eps) * g + b


def _outproj_body(of_ref, ob_ref, g_ref, hy_ref, om_ref, x_ref, gate_ref, gg_ref, hg_ref, mg_ref,
                  w_ref, lg_ref, lb_ref, o_ref, *, alpha):
    VD = GLA_HEADS * GLA_DV
    o = of_ref[0] + ob_ref[0]
    r = _idiv(lax.broadcasted_iota(jnp.int32, (VD, VD), 0), GLA_DV)
    c = _idiv(lax.broadcasted_iota(jnp.int32, (VD, VD), 1), GLA_DV)
    grp = (r == c).astype(F32)
    ms = jnp.dot(o * o, grp, precision=HI, preferred_element_type=F32) * (1.0 / GLA_DV)
    g = g_ref[0]
    ya = o * lax.rsqrt(ms + 1e-6) * gg_ref[...] * (g * jax.nn.sigmoid(g))
    yb = _rms_rows(hy_ref[0], hg_ref[...])
    yc = _rms_rows(om_ref[0], mg_ref[...])
    acc = jnp.dot(ya.astype(BF16), w_ref[0:VD, :], preferred_element_type=F32)
    acc += jnp.dot(yb.astype(BF16), w_ref[VD:VD + HY_CH, :], preferred_element_type=F32)
    acc += jnp.dot(yc.astype(BF16), w_ref[VD + HY_CH:, :], preferred_element_type=F32)
    o_ref[0] = _layernorm_rows(alpha * x_ref[0] + gate_ref[0] * acc, lg_ref[...], lb_ref[...])


def _outproj(of, ob, vg, hy, om, x, gate, gla_g, hy_g, mla_g, w_out, ln_g, ln_b, alpha):
    B, L, D = x.shape
    tm = min(512, L)
    VD = GLA_HEADS * GLA_DV
    MD = MLA_HEADS * MLA_V
    row = lambda w: pl.BlockSpec((1, tm, w), lambda b, i: (b, i, 0))
    return pl.pallas_call(
        functools.partial(_outproj_body, alpha=alpha),
        grid=(B, L // tm),
        in_specs=[row(VD), row(VD), pl.BlockSpec((1, tm, VD), lambda b, i: (b, i, 1)), row(HY_CH), row(MD), row(D),
                  pl.BlockSpec((1, 1, D), lambda b, i: (b, 0, 0)), _full((1, VD)), _full((1, HY_CH)), _full((1, MD)),
                  _full(w_out.shape), _full((1, D)), _full((1, D))],
        out_specs=row(D),
        out_shape=jax.ShapeDtypeStruct((B, L, D), F32),
        compiler_params=_cp("parallel", "parallel"),
        name="outproj",
    )(of, ob, vg, hy, om, x, gate, jnp.tile(gla_g, GLA_HEADS)[None, :], hy_g[None, :], mla_g[None, :],
      w_out.astype(BF16), ln_g[None, :], ln_b[None, :])


def _ffn_body(x_ref, sh_ref, sc_ref, gate_ref, w1_ref, w3_ref, w2_ref, lg_ref, lb_ref, o_ref, h_ref, acc_ref, *, alpha):
    j = pl.program_id(2)

    @pl.when(j == 0)
    def _():
        h_ref[...] = (x_ref[0] * (1.0 + sc_ref[0]) + sh_ref[0]).astype(BF16)
        acc_ref[...] = jnp.zeros_like(acc_ref)

    tm = h_ref.shape[0]
    parts = [slice(0, tm // 2), slice(tm // 2, tm)] if tm % 32 == 0 else [slice(0, tm)]
    hs = [h_ref[p, :] for p in parts]
    a = [jnp.dot(h, w1_ref[...], preferred_element_type=F32) for h in hs]
    b = [jnp.dot(h, w3_ref[...], preferred_element_type=F32) for h in hs]
    t = [(a_ * jax.nn.sigmoid(a_) * b_).astype(BF16) for a_, b_ in zip(a, b)]
    for p, t_ in zip(parts, t):
        acc_ref[p, :] += jnp.dot(t_, w2_ref[...], preferred_element_type=F32)

    @pl.when(j == pl.num_programs(2) - 1)
    def _():
        o_ref[0] = _layernorm_rows(alpha * x_ref[0] + gate_ref[0] * acc_ref[...], lg_ref[...], lb_ref[...])


def _ffn_tile(F):
    for cand in (512, 256, 128):
        if F % cand == 0:
            return cand
    return F


def _ffn(x, shift, scale, gate, w1, w3, w2, ln_g, ln_b, alpha):
    B, L, D = x.shape
    F = w1.shape[1]
    tm = min(1024, L)
    tf = _ffn_tile(F)
    row = pl.BlockSpec((1, tm, D), lambda b, i, j: (b, i, 0))
    vec = pl.BlockSpec((1, 1, D), lambda b, i, j: (b, 0, 0))
    return pl.pallas_call(
        functools.partial(_ffn_body, alpha=alpha),
        grid=(B, L // tm, F // tf),
        in_specs=[row, vec, vec, vec,
                  pl.BlockSpec((D, tf), lambda b, i, j: (0, j)), pl.BlockSpec((D, tf), lambda b, i, j: (0, j)),
                  pl.BlockSpec((tf, D), lambda b, i, j: (j, 0)), _full((1, D)), _full((1, D))],
        out_specs=row,
        out_shape=jax.ShapeDtypeStruct((B, L, D), F32),
        scratch_shapes=[pltpu.VMEM((tm, D), BF16), pltpu.VMEM((tm, D), F32)],
        compiler_params=_cp("parallel", "parallel", "arbitrary"),
        name="ffn",
    )(x, shift, scale, gate, w1.astype(BF16), w3.astype(BF16), w2.astype(BF16), ln_g[None, :], ln_b[None, :])


MOE_TOKENS = 2048
MOE_ROWS = 256
RANK_CHUNK = 256


def _router_body(x_ref, sh_ref, sc_ref, wr_ref, h_ref, g_ref, rk_ref, rkt_ref, cnt_ref):
    h = x_ref[0] * (1.0 + sc_ref[0]) + sh_ref[0]
    h_ref[0] = h.astype(BF16)
    logits = jnp.dot(h, wr_ref[...], precision=HI, preferred_element_type=F32)
    lane = lax.broadcasted_iota(jnp.int32, logits.shape, 1).astype(F32)
    logits = jnp.where(lane < N_EXPERTS, logits, -jnp.inf)
    m1 = jnp.max(logits, axis=1, keepdims=True)
    i1 = jnp.min(jnp.where(logits == m1, lane, float(LANES)), axis=1, keepdims=True)
    rest = jnp.where(lane == i1, -jnp.inf, logits)
    m2 = jnp.max(rest, axis=1, keepdims=True)
    i2 = jnp.min(jnp.where(rest == m2, lane, float(LANES)), axis=1, keepdims=True)
    e2 = jnp.exp(m2 - m1)
    w1 = 1.0 / (1.0 + e2)
    w2 = e2 / (1.0 + e2)
    g_ref[0] = jnp.where(lane == i1, w1, 0.0) + jnp.where(lane == i2, w2, 0.0)
    sel = jnp.logical_or(lane == i1, lane == i2)
    self_ = sel.astype(F32)
    tm = h.shape[0]
    C = min(RANK_CHUNK, tm)
    r = lax.broadcasted_iota(jnp.int32, (C, C), 0)
    c = lax.broadcasted_iota(jnp.int32, (C, C), 1)
    tri = (c < r).astype(BF16)
    carry = jnp.zeros((1, LANES), F32)
    parts = []
    for k in range(tm // C):
        sk = self_[k * C:(k + 1) * C]
        parts.append(jnp.dot(tri, sk.astype(BF16), preferred_element_type=F32) + carry)
        carry = carry + jnp.sum(sk, axis=0, keepdims=True)
    rank = jnp.where(sel, jnp.concatenate(parts, axis=0), -1.0)
    rk_ref[0] = rank
    rkt_ref[0] = rank.T[:8]
    cnt_ref[0, 0] = carry


def _router(x, shift, scale, w_router):
    B, L, D = x.shape
    tm = min(MOE_TOKENS, L)
    nt = L // tm
    wr = jnp.pad(w_router, ((0, 0), (0, LANES - N_EXPERTS)))
    vec = pl.BlockSpec((1, 1, D), lambda b, i: (b, 0, 0))
    col = pl.BlockSpec((1, tm, LANES), lambda b, i: (b, i, 0))
    return pl.pallas_call(
        _router_body,
        grid=(B, nt),
        in_specs=[pl.BlockSpec((1, tm, D), lambda b, i: (b, i, 0)), vec, vec, _full((D, LANES))],
        out_specs=[pl.BlockSpec((1, tm, D), lambda b, i: (b, i, 0)), col, col,
                   pl.BlockSpec((1, 8, tm), lambda b, i: (b, 0, i)), pl.BlockSpec((1, 1, 1, LANES), lambda b, i: (b, i, 0, 0))],
        out_shape=[jax.ShapeDtypeStruct((B, L, D), BF16), jax.ShapeDtypeStruct((B, L, LANES), F32),
                   jax.ShapeDtypeStruct((B, L, LANES), F32), jax.ShapeDtypeStruct((B, 8, L), F32),
                   jax.ShapeDtypeStruct((B, nt, 1, LANES), F32)],
        compiler_params=_cp("parallel", "parallel"),
        name="moe_router",
    )(x, shift, scale, wr)


def _moe_body(cnt_ref, h_ref, g_ref, rk_ref, rkt_ref, w1_ref, w3_ref, w2_ref, o_ref, xg_ref, y_ref, *, M, P):
    b, i, e, j = pl.program_id(0), pl.program_id(1), pl.program_id(2), pl.program_id(3)
    nt, ne, nj = pl.num_programs(1), pl.num_programs(2), pl.num_programs(3)
    tm = h_ref.shape[1]
    cnt = cnt_ref[(b * nt + i) * ne + e]
    n_ch = lax.div(cnt + (M - 1), M)

    @pl.when(jnp.logical_and(e == 0, j == 0))
    def _():
        o_ref[...] = jnp.zeros_like(o_ref)

    @pl.when(j == 0)
    def _():
        rkt = rkt_ref[0, pl.ds(e, 1), :]

        def gather(c, carry):
            r0 = pl.multiple_of(c * M, 16)
            rows = (lax.broadcasted_iota(jnp.int32, (M, 1), 0) + c * M).astype(F32)
            onehot = (rkt == rows).astype(BF16)
            xg_ref[pl.ds(r0, M), :] = jnp.dot(onehot, h_ref[0], preferred_element_type=F32).astype(BF16)
            return carry

        lax.fori_loop(0, n_ch, gather, 0)

    def expert(chunks):
        r0 = [pl.multiple_of(c * M, 16) for c in chunks]
        xg = [xg_ref[pl.ds(r, M), :] for r in r0]
        a = [jnp.dot(x_, w1_ref[0], preferred_element_type=F32) for x_ in xg]
        g = [jnp.dot(x_, w3_ref[0], preferred_element_type=F32) for x_ in xg]
        t = [(a_ * jax.nn.sigmoid(a_) * g_).astype(BF16) for a_, g_ in zip(a, g)]
        yv = [jnp.dot(t_, w2_ref[0], preferred_element_type=F32) for t_ in t]

        @pl.when(j == 0)
        def _():
            for r, y_ in zip(r0, yv):
                y_ref[pl.ds(r, M), :] = y_

        @pl.when(j > 0)
        def _():
            for r, y_ in zip(r0, yv):
                y_ref[pl.ds(r, M), :] += y_

    def expert_pair(c2, carry):
        expert([2 * c2, 2 * c2 + 1])
        return carry

    lax.fori_loop(0, lax.div(n_ch, 2), expert_pair, 0)

    @pl.when(lax.rem(n_ch, 2) == 1)
    def _():
        expert([n_ch - 1])

    @pl.when(j == nj - 1)
    def _():
        for p in range(tm // P):
            lane = lax.broadcasted_iota(jnp.int32, (P, LANES), 1)
            rke = jnp.sum(jnp.where(lane == e, rk_ref[0, p * P:(p + 1) * P, :], 0.0), axis=1, keepdims=True)
            ge = jnp.sum(jnp.where(lane == e, g_ref[0, p * P:(p + 1) * P, :], 0.0), axis=1, keepdims=True)

            def scatter(c, carry):
                r0 = pl.multiple_of(c * M, 16)
                cols = (lax.broadcasted_iota(jnp.int32, (1, M), 1) + c * M).astype(F32)
                onehot = (rke == cols).astype(BF16)
                yb = y_ref[pl.ds(r0, M), :].astype(BF16)
                o_ref[0, p * P:(p + 1) * P, :] += ge * jnp.dot(onehot, yb, preferred_element_type=F32)
                return carry

            lax.fori_loop(0, n_ch, scatter, 0)


def _res_ln_body(x_ref, y_ref, gate_ref, lg_ref, lb_ref, o_ref, *, alpha):
    o_ref[0] = _layernorm_rows(alpha * x_ref[0] + gate_ref[0] * y_ref[0], lg_ref[...], lb_ref[...])


def _res_ln(x, y, gate, ln_g, ln_b, alpha):
    B, L, D = x.shape
    tm = min(1024, L)
    row = pl.BlockSpec((1, tm, D), lambda b, i: (b, i, 0))
    return pl.pallas_call(
        functools.partial(_res_ln_body, alpha=alpha),
        grid=(B, L // tm),
        in_specs=[row, row, pl.BlockSpec((1, 1, D), lambda b, i: (b, 0, 0)), _full((1, D)), _full((1, D))],
        out_specs=row,
        out_shape=jax.ShapeDtypeStruct((B, L, D), F32),
        compiler_params=_cp("parallel", "parallel"),
        name="res_ln",
    )(x, y, gate, ln_g[None, :], ln_b[None, :])


def _moe(x, shift, scale, gate, w_router, w1, w3, w2, ln_g, ln_b, alpha):
    B, L, D = x.shape
    E, _, F = w1.shape
    hb, gts, rk, rkt, cnt = _router(x, shift, scale, w_router)
    tm = min(MOE_TOKENS, L)
    nt = L // tm
    M = MOE_ROWS
    rows_max = -(-tm // M) * M
    tf = _ffn_tile(F)
    counts = cnt[:, :, 0, :E].astype(jnp.int32).reshape(-1)
    row = lambda w: pl.BlockSpec((1, tm, w), lambda b, i, e, j, c: (b, i, 0))
    y = pl.pallas_call(
        functools.partial(_moe_body, M=M, P=min(512, tm)),
        grid_spec=pltpu.PrefetchScalarGridSpec(
            num_scalar_prefetch=1,
            grid=(B, nt, E, F // tf),
            in_specs=[row(D), row(LANES), row(LANES), pl.BlockSpec((1, 8, tm), lambda b, i, e, j, c: (b, 0, i)),
                      pl.BlockSpec((1, D, tf), lambda b, i, e, j, c: (e, 0, j)),
                      pl.BlockSpec((1, D, tf), lambda b, i, e, j, c: (e, 0, j)),
                      pl.BlockSpec((1, tf, D), lambda b, i, e, j, c: (e, j, 0))],
            out_specs=row(D),
            scratch_shapes=[pltpu.VMEM((rows_max, D), BF16), pltpu.VMEM((rows_max, D), F32)],
        ),
        out_shape=jax.ShapeDtypeStruct((B, L, D), F32),
        compiler_params=_cp("parallel", "parallel", "arbitrary", "arbitrary"),
        name="moe",
    )(counts, hb, gts, rk, rkt, w1.astype(BF16), w3.astype(BF16), w2.astype(BF16))
    return _res_ln(x, y, gate, ln_g, ln_b, alpha)


def _mod_body(c_ref, w_ref, b_ref, o_ref):
    c = c_ref[...]
    s = c * jax.nn.sigmoid(c)
    o_ref[...] = jnp.dot(s, w_ref[...], precision=HI, preferred_element_type=F32) + b_ref[...]


def _modulation(cc, w_mod, b_mod):
    R, D = cc.shape
    N = w_mod.shape[1]
    tn = 1024
    return pl.pallas_call(
        _mod_body,
        grid=(N // tn,),
        in_specs=[_full((R, D)), pl.BlockSpec((D, tn), lambda j: (0, j)), pl.BlockSpec((1, tn), lambda j: (0, j))],
        out_specs=pl.BlockSpec((R, tn), lambda j: (0, j)),
        out_shape=jax.ShapeDtypeStruct((R, N), F32),
        compiler_params=_cp("parallel"),
        name="modulation",
    )(cc, w_mod, b_mod[None, :])


def _streams(x, c, ctx, c_ctx, w_mod, b_mod, w_in, gla_w_gate, gla_b_gate, gla_norm_g, hy_conv_w, hy_conv_b, hy_f_w1, hy_f_b1, hy_f_freq1, hy_f_w2, hy_f_b2, hy_f_freq2, hy_f_w3, hy_f_b3, hy_skip, hy_norm_g, mla_q_norm_g, mla_w_uq, mla_kv_norm_g, mla_w_ukv, mla_norm_g, w_out, ln_g, ln_b, ffn_w1, ffn_w3, ffn_w2, moe_router, moe_w1, moe_w3, moe_w2):
    B, L, D = x.shape
    Lc = ctx.shape[1]
    depth = w_mod.shape[0]
    alpha = (2.0 * depth) ** 0.25
    cc = jnp.zeros((8, D), F32).at[:B].set(c).at[B].set(c_ctx)
    cos, sin = _rope_tables(L, True)
    cos_c, sin_c = _rope_tables(Lc, False)
    cos_all, sin_all = jnp.concatenate([cos_c, cos], axis=0), jnp.concatenate([sin_c, sin], axis=0)
    KD, VD = GLA_HEADS * GLA_DK, GLA_HEADS * GLA_DV
    xc = ctx
    for l in range(depth):
        need_ctx = l < depth - 1
        mods = _modulation(cc, w_mod[l], b_mod[l])
        m = [mods[:B, k * D:(k + 1) * D][:, None, :] for k in range(6)]
        mc = [jnp.broadcast_to(mods[B, k * D:(k + 1) * D][None, None, :], (B, 1, D)) for k in range(6)]
        w_arr = _arrange_w_in(w_in[l])
        wg, bg = _arrange_gate(gla_w_gate[l], gla_b_gate[l])
        filt = (hy_f_w1[l], hy_f_b1[l], hy_f_freq1[l], hy_f_w2[l], hy_f_b2[l], hy_f_freq2[l], hy_f_w3[l], hy_f_b3[l])
        wq = _arrange_wq(mla_w_uq[l])
        wk, wv = _arrange_wkv(mla_w_ukv[l])

        hyu, qk, vg, alr, cq, ckvr = _inproj(x, m[0], m[1], w_arr)
        hyu_c, qk_c, vg_c, alr_c, cq_c, ckvr_c = _inproj(xc, mc[0], mc[1], w_arr)

        of_c, ob_c, s_c = _gla(qk_c, vg_c, alr_c, wg, bg, jnp.zeros((B, 2, KD, VD), F32))
        of, ob, _ = _gla(qk, vg, alr, wg, bg, s_c)
        hy = _hyena(hyu, hy_conv_w[l], hy_conv_b[l], filt, hy_skip[l])
        k_all, v_all = _kvproj(jnp.concatenate([ckvr_c, ckvr], axis=1), mla_kv_norm_g[l], wk, wv, cos_all, sin_all)
        k_c, v_c = k_all[:, :Lc], v_all[:, :Lc]
        q_m = _qproj(cq, mla_q_norm_g[l], wq, cos, sin)
        om = _flash(q_m, k_all, v_all)

        x = _outproj(of, ob, vg, hy, om, x, m[2], gla_norm_g[l], hy_norm_g[l], mla_norm_g[l], w_out[l],
                     ln_g[l, 0], ln_b[l, 0], alpha)
        if need_ctx:
            hy_c = _hyena_ctx(hyu_c, hy_conv_w[l], hy_conv_b[l], filt, hy_skip[l])
            q_c = _qproj(cq_c, mla_q_norm_g[l], wq, cos_c, sin_c)
            om_c = _flash(q_c, k_c, v_c)
            xc = _outproj(of_c, ob_c, vg_c, hy_c, om_c, xc, mc[2], gla_norm_g[l], hy_norm_g[l], mla_norm_g[l],
                          w_out[l], ln_g[l, 0], ln_b[l, 0], alpha)

        i = l // 2
        if l % 2 == 0:
            x = _ffn(x, m[3], m[4], m[5], ffn_w1[i], ffn_w3[i], ffn_w2[i], ln_g[l, 1], ln_b[l, 1], alpha)
            if need_ctx:
                xc = _ffn(xc, mc[3], mc[4], mc[5], ffn_w1[i], ffn_w3[i], ffn_w2[i], ln_g[l, 1], ln_b[l, 1], alpha)
        else:
            x = _moe(x, m[3], m[4], m[5], moe_router[i], moe_w1[i], moe_w3[i], moe_w2[i], ln_g[l, 1], ln_b[l, 1], alpha)
            if need_ctx:
                xc = _moe(xc, mc[3], mc[4], mc[5], moe_router[i], moe_w1[i], moe_w3[i], moe_w2[i], ln_g[l, 1],
                          ln_b[l, 1], alpha)
    return x, xc


def kernel(x, c, ctx, c_ctx, w_mod, b_mod, w_in, gla_w_gate, gla_b_gate, gla_norm_g, hy_conv_w, hy_conv_b, hy_f_w1, hy_f_b1, hy_f_freq1, hy_f_w2, hy_f_b2, hy_f_freq2, hy_f_w3, hy_f_b3, hy_skip, hy_norm_g, mla_q_norm_g, mla_w_uq, mla_kv_norm_g, mla_w_ukv, mla_norm_g, w_out, ln_g, ln_b, ffn_w1, ffn_w3, ffn_w2, moe_router, moe_w1, moe_w3, moe_w2):
    return _streams(x, c, ctx, c_ctx, w_mod, b_mod, w_in, gla_w_gate, gla_b_gate, gla_norm_g, hy_conv_w, hy_conv_b, hy_f_w1, hy_f_b1, hy_f_freq1, hy_f_w2, hy_f_b2, hy_f_freq2, hy_f_w3, hy_f_b3, hy_skip, hy_norm_g, mla_q_norm_g, mla_w_uq, mla_kv_norm_g, mla_w_ukv, mla_norm_g, w_out, ln_g, ln_b, ffn_w1, ffn_w3, ffn_w2, moe_router, moe_w1, moe_w3, moe_w2)[0]
```

```python
import functools
import math

import numpy as np
import jax
import jax.numpy as jnp
from jax import lax
from jax.experimental import pallas as pl
from jax.experimental.pallas import tpu as pltpu

F32 = jnp.float32
BF16 = jnp.bfloat16
HI = lax.Precision.HIGHEST

GRID_W = 64
GLA_HEADS, GLA_DK, GLA_DV, GLA_RANK, GLA_TAU = 4, 32, 64, 16, 16.0
HY_CH, HY_EMB = 256, 33
HY_DECAY_TARGET, HY_FAST_DECAY, HY_SLOW_DECAY = 1e-2, 0.3, 1.5
MLA_HEADS, MLA_Q_RANK, MLA_KV_RANK, MLA_NOPE, MLA_ROPE, MLA_V = 8, 256, 128, 64, 32, 64
MLA_SCALE = (MLA_NOPE + MLA_ROPE) ** -0.5
ROPE_BASE = 10000.0
N_EXPERTS = 8
IN_SPLITS = (128, 128, 256, 256, 32, 768, 256, 128, 32)

LANES = 128
SUBLANES = 8
VMEM_LIMIT = 56 * 1024 * 1024

GLA_CHUNK = 128
DFT_N2 = 256


def _cp(*sem):
    return pltpu.CompilerParams(dimension_semantics=sem, vmem_limit_bytes=VMEM_LIMIT)


def _full(shape):
    n = len(shape)
    return pl.BlockSpec(shape, lambda *_: (0,) * n)


def _idiv(x, d):
    assert d & (d - 1) == 0
    return lax.shift_right_logical(x, int(math.log2(d)))


INPROJ_WIDTHS = (768, 256, 512, 128, 256, 256)


def _arrange_w_in(w):
    cuts = np.cumsum(IN_SPLITS)[:-1]
    qa, ka, va, ga, alr, hyu, cq, ckv, kr = jnp.split(w, [int(c) for c in cuts], axis=1)
    z96 = jnp.zeros((w.shape[0], 96), w.dtype)
    return jnp.concatenate([hyu, qa, ka, va, ga, alr, z96, cq, ckv, kr, z96], axis=1).astype(BF16)


def _inproj_body(x_ref, sh_ref, sc_ref, w_ref, *out_refs):
    h = x_ref[0] * (1.0 + sc_ref[0]) + sh_ref[0]
    acc = jnp.dot(h.astype(BF16), w_ref[...], preferred_element_type=F32)
    off = 0
    for r in out_refs:
        w = r.shape[-1]
        r[0] = acc[:, off:off + w]
        off += w


def _inproj(x, shift, scale, w_arr):
    B, L, D = x.shape
    tm = min(512, L)
    n = w_arr.shape[1]
    row = lambda w: pl.BlockSpec((1, tm, w), lambda b, i: (b, i, 0))
    vec = pl.BlockSpec((1, 1, D), lambda b, i: (b, 0, 0))
    return pl.pallas_call(
        _inproj_body,
        grid=(B, L // tm),
        in_specs=[row(D), vec, vec, _full((D, n))],
        out_specs=[row(w) for w in INPROJ_WIDTHS],
        out_shape=[jax.ShapeDtypeStruct((B, L, w), F32) for w in INPROJ_WIDTHS],
        compiler_params=_cp("parallel", "parallel"),
        name="inproj",
    )(x, shift, scale, w_arr)


def _log_sigmoid(z):
    return jnp.minimum(z, 0.0) - jnp.log1p(jnp.exp(-jnp.abs(z)))


def _gla_body(qkf_ref, vf_ref, af_ref, qkb_ref, vb_ref, ab_ref, wg_ref, bg_ref, s0_ref,
              of_ref, ob_ref, sout_ref, s_ref):
    i = pl.program_id(0)
    C = qkf_ref.shape[1]
    KD = GLA_HEADS * GLA_DK
    VD = GLA_HEADS * GLA_DV

    @pl.when(i == 0)
    def _():
        s_ref[...] = s0_ref[...]

    r = lax.broadcasted_iota(jnp.int32, (C, C), 0)
    c = lax.broadcasted_iota(jnp.int32, (C, C), 1)
    tris = ((c <= r).astype(F32), (c >= r).astype(F32))
    lane_k = _idiv(lax.broadcasted_iota(jnp.int32, (1, KD), 1), GLA_DK)
    lane_v = _idiv(lax.broadcasted_iota(jnp.int32, (1, VD), 1), GLA_DV)
    rk = _idiv(lax.broadcasted_iota(jnp.int32, (KD, VD), 0), GLA_DK)
    cv = _idiv(lax.broadcasted_iota(jnp.int32, (KD, VD), 1), GLA_DV)
    ones = jnp.ones((C, VD), F32)
    refs = ((qkf_ref, vf_ref, af_ref, of_ref), (qkb_ref, vb_ref, ab_ref, ob_ref))
    chains = [(b, d) for b in range(qkf_ref.shape[0]) for d in range(2)]

    z = [jnp.dot(refs[d][2][b], wg_ref[...], precision=HI, preferred_element_type=F32) + bg_ref[...] for b, d in chains]
    la = [_log_sigmoid(zz[:, d * KD:(d + 1) * KD]) / GLA_TAU for zz, (b, d) in zip(z, chains)]
    bb = [jnp.dot(tris[d], l_, precision=HI, preferred_element_type=F32) for l_, (b, d) in zip(la, chains)]
    tot_b = [lax.dot_general(l_, ones, (((0,), (0,)), ((), ())), precision=HI, preferred_element_type=F32) for l_ in la]
    qe, ke, kl, vb, s_old = [], [], [], [], []
    for n, (b, d) in enumerate(chains):
        qk = refs[d][0][b]
        q = qk[:, :KD] * (GLA_DK ** -0.5)
        k = qk[:, KD:]
        tot = jnp.sum(la[n], axis=0, keepdims=True)
        qe.append(q * jnp.exp(bb[n]))
        ke.append((k * jnp.exp(-bb[n])).astype(BF16))
        kl.append((k * jnp.exp(tot - bb[n])).astype(BF16))
        vb.append(refs[d][1][b].astype(BF16))
        s_old.append(s_ref[2 * b + d])
    o = [jnp.dot(qe[n].astype(BF16), s_old[n].astype(BF16), preferred_element_type=F32) for n in range(len(chains))]
    att = [[lax.dot_general(jnp.where(lane_k == h, qe[n], 0.0).astype(BF16), ke[n], (((1,), (1,)), ((), ())),
                            preferred_element_type=F32) for h in range(GLA_HEADS)] for n in range(len(chains))]
    kv = [lax.dot_general(kl[n], vb[n], (((0,), (0,)), ((), ())), preferred_element_type=F32) for n in range(len(chains))]
    for n, (b, d) in enumerate(chains):
        on = o[n]
        for h in range(GLA_HEADS):
            oh = jnp.dot((att[n][h] * tris[d]).astype(BF16), vb[n], preferred_element_type=F32)
            on = on + jnp.where(lane_v == h, oh, 0.0)
        refs[d][3][b] = on
        s_ref[2 * b + d] = jnp.exp(tot_b[n]) * s_old[n] + jnp.where(rk == cv, kv[n], 0.0)

    @pl.when(i == pl.num_programs(0) - 1)
    def _():
        sout_ref[...] = s_ref[...]


def _gla(qk, vg, alr, wg, bg, s0):
    B, L, _ = qk.shape
    C = min(GLA_CHUNK, L)
    n = L // C
    KD, VD = GLA_HEADS * GLA_DK, GLA_HEADS * GLA_DV
    fwd = lambda w: pl.BlockSpec((B, C, w), lambda i: (0, i, 0))
    bwd = lambda w: pl.BlockSpec((B, C, w), lambda i: (0, n - 1 - i, 0))
    st = _full((2 * B, KD, VD))
    of, ob, s_out = pl.pallas_call(
        _gla_body,
        grid=(n,),
        in_specs=[fwd(2 * KD), fwd(VD), fwd(LANES), bwd(2 * KD), bwd(VD), bwd(LANES),
                  _full((LANES, 2 * KD)), _full((1, 2 * KD)), st],
        out_specs=[fwd(VD), bwd(VD), st],
        out_shape=[jax.ShapeDtypeStruct((B, L, VD), F32), jax.ShapeDtypeStruct((B, L, VD), F32),
                   jax.ShapeDtypeStruct((2 * B, KD, VD), F32)],
        scratch_shapes=[pltpu.VMEM((2 * B, KD, VD), F32)],
        compiler_params=_cp("arbitrary"),
        name="gla",
    )(qk, vg, alr, qk, vg, alr, wg, bg, s0.reshape(2 * B, KD, VD))
    return of, ob, s_out.reshape(B, 2, KD, VD)


def _arrange_gate(w_gate, b_gate):
    KD = GLA_HEADS * GLA_DK
    wg = jnp.zeros((LANES, 2 * KD), F32)
    wg = wg.at[:GLA_RANK, :KD].set(w_gate[0]).at[GLA_RANK:2 * GLA_RANK, KD:].set(w_gate[1])
    return wg, jnp.concatenate([b_gate[0], b_gate[1]])[None, :]


def _shortconv_body(x_ref, p_ref, n_ref, w_ref, b_ref, v_ref, x1_ref, x2_ref):
    i = pl.program_id(1)
    last = pl.num_programs(1) - 1
    x = x_ref[0]
    tm = x.shape[0]
    prev = jnp.where(i > 0, p_ref[0][7:8, :], 0.0)
    nxt = jnp.where(i < last, n_ref[0][0:1, :], 0.0)
    rid = lax.broadcasted_iota(jnp.int32, x.shape, 0)
    dn = jnp.where(rid == 0, prev, pltpu.roll(x, 1, 0))
    up = jnp.where(rid == tm - 1, nxt, pltpu.roll(x, tm - 1, 0))
    w = w_ref[...]
    y = b_ref[...] + dn * w[0:1] + x * w[1:2] + up * w[2:3]
    v_ref[0] = y[:, :HY_CH]
    x1_ref[0] = y[:, HY_CH:2 * HY_CH]
    x2_ref[0] = y[:, 2 * HY_CH:]


def _shortconv(u, w, b):
    B, L, W = u.shape
    tm = min(512, L)
    nb = tm // 8
    row = pl.BlockSpec((1, tm, W), lambda b_, i: (b_, i, 0))
    prev = pl.BlockSpec((1, 8, W), lambda b_, i: (b_, jnp.maximum(i * nb - 1, 0), 0))
    nxt = pl.BlockSpec((1, 8, W), lambda b_, i: (b_, jnp.minimum((i + 1) * nb, L // 8 - 1), 0))
    o = pl.BlockSpec((1, tm, HY_CH), lambda b_, i: (b_, i, 0))
    return pl.pallas_call(
        _shortconv_body,
        grid=(B, L // tm),
        in_specs=[row, prev, nxt, _full((3, W)), _full((1, W))],
        out_specs=[o, o, o],
        out_shape=[jax.ShapeDtypeStruct((B, L, HY_CH), F32)] * 3,
        compiler_params=_cp("parallel", "parallel"),
        name="shortconv",
    )(u, u, u, w, b[None, :])


def _filter_feats(L):
    pos = jnp.arange(L, dtype=F32)
    t = pos / (L - 1)
    bands = (HY_EMB - 1) // 2
    freqs = jnp.linspace(1e-4, bands - 1, bands, dtype=F32)
    ang = (2.0 * math.pi * pos / L)[:, None] * freqs
    z = jnp.concatenate([t[:, None], jnp.cos(ang), -jnp.sin(ang)], axis=-1)
    z = jnp.pad(z, ((0, 0), (0, LANES - HY_EMB)))
    deltas = jnp.abs(jnp.linspace(math.log(HY_DECAY_TARGET) / HY_SLOW_DECAY,
                                  math.log(HY_DECAY_TARGET) / HY_FAST_DECAY, HY_CH, dtype=F32))
    return z, jnp.tile(deltas, 4)[None, :]


def _filter_body(z_ref, w1_ref, b1_ref, f1_ref, w2_ref, b2_ref, f2_ref, w3_ref, b3_ref, dl_ref,
                 h_ref, ss_ref, *, L):
    i = pl.program_id(0)
    z = z_ref[...]
    tm = z.shape[0]
    hid = jnp.sin(f1_ref[...] * (jnp.dot(z, w1_ref[...], precision=HI, preferred_element_type=F32) + b1_ref[...]))
    hid = jnp.sin(f2_ref[...] * (jnp.dot(hid, w2_ref[...], precision=HI, preferred_element_type=F32) + b2_ref[...]))
    h = jnp.dot(hid, w3_ref[...], precision=HI, preferred_element_type=F32) + b3_ref[...]
    pos = (lax.broadcasted_iota(jnp.int32, (tm, 1), 0) + i * tm).astype(F32)
    t = pos / (L - 1)
    h = h * jnp.exp(-t * dl_ref[...])

    @pl.when(i == 0)
    def _():
        ss_ref[...] = jnp.zeros_like(ss_ref)

    ss_ref[...] += jnp.sum(h * h, axis=0, keepdims=True)
    col = lax.broadcasted_iota(jnp.int32, h.shape, 1)
    is_bwd = (_idiv(col, HY_CH) & 1) == 1
    h_ref[...] = jnp.where(jnp.logical_and(is_bwd, pos == 0.0), 0.0, h)


def _filters(L, fw1, fb1, ff1, fw2, fb2, ff2, fw3, fb3):
    z, dl = _filter_feats(L)
    tm = min(1024, L)
    Hf = fw2.shape[0]
    w1 = jnp.pad(fw1, ((0, LANES - HY_EMB), (0, 0)))
    NC = fw3.shape[1]
    return pl.pallas_call(
        functools.partial(_filter_body, L=L),
        grid=(L // tm,),
        in_specs=[pl.BlockSpec((tm, LANES), lambda i: (i, 0)), _full((LANES, Hf)), _full((1, Hf)), _full((1, Hf)),
                  _full((Hf, Hf)), _full((1, Hf)), _full((1, Hf)), _full((Hf, NC)), _full((1, NC)), _full((1, NC))],
        out_specs=[pl.BlockSpec((tm, NC), lambda i: (i, 0)), _full((1, NC))],
        out_shape=[jax.ShapeDtypeStruct((L, NC), F32), jax.ShapeDtypeStruct((1, NC), F32)],
        compiler_params=_cp("arbitrary"),
        name="hy_filters",
    )(z, w1, fb1[None], ff1[None], fw2, fb2[None], ff2[None], fw3, fb3[None], dl)


def _dft_consts(L):
    N = 2 * L
    N2 = DFT_N2
    N1 = N // N2
    half = N1 // 2
    k1 = np.arange(N1)[:, None].astype(np.float64)
    n1 = np.arange(N1)[None, :].astype(np.float64)
    a1 = 2.0 * np.pi * k1 * n1 / N1
    f1r, f1i = np.cos(a1), -np.sin(a1)
    fa = np.concatenate([f1r[:, :half], f1i[:, :half]], axis=0)
    fb = np.concatenate([f1r[:half, :], f1i[:half, :]], axis=1) / N
    k2 = np.arange(N2)[:, None].astype(np.float64)
    n2 = np.arange(N2)[None, :].astype(np.float64)
    a2 = 2.0 * np.pi * k2 * n2 / N2
    f2r, f2i = np.cos(a2), -np.sin(a2)
    g = np.block([[f2r, -f2i], [f2i, f2r]])
    gc = np.block([[f2r, f2i], [-f2i, f2r]])
    at = 2.0 * np.pi * (np.arange(N1)[:, None] * np.arange(N2)[None, :] % N) / N
    twr, twi = np.cos(at), -np.sin(at)
    c = lambda a: jnp.asarray(a, dtype=F32)
    bc = lambda a: jnp.broadcast_to(c(a)[:, :, None], (N1, N2, LANES))
    eye = np.eye(SUBLANES)
    return dict(N1=N1, N2=N2, half=half, fa=c(np.kron(fa, eye)), fb=c(np.kron(fb, eye)), g=c(g), gc=c(gc),
                twr=bc(twr), twi=bc(twi))


def _lanes(t, width):
    return jnp.concatenate([t] * (width // LANES), axis=-1)


def _dft1_body(f_ref, x_ref, o_ref):
    x = x_ref[0]
    x2 = x.reshape(x.shape[0] * SUBLANES, x.shape[2]).astype(BF16)
    y = jnp.dot(f_ref[...], x2, preferred_element_type=F32)
    o_ref[0] = y.reshape(o_ref.shape[1], SUBLANES, y.shape[1])


def _dft_stage1(fa, x):
    B, half, N2, W = x.shape
    R = fa.shape[0] // SUBLANES
    return pl.pallas_call(
        _dft1_body,
        grid=(B, N2 // SUBLANES),
        in_specs=[_full(fa.shape), pl.BlockSpec((1, half, SUBLANES, W), lambda b, j: (b, 0, j, 0))],
        out_specs=pl.BlockSpec((1, R, SUBLANES, W), lambda b, j: (b, 0, j, 0)),
        out_shape=jax.ShapeDtypeStruct((B, R, N2, W), F32),
        compiler_params=_cp("parallel", "parallel"),
        name="hy_dft1",
    )(fa.astype(BF16), x)


def _filter_spec_body(a_ref, twr_ref, twi_ref, g_ref, ss_ref, hf_ref):
    W = a_ref.shape[-1]
    ar, ai = a_ref[0, 0], a_ref[1, 0]
    twr, twi = _lanes(twr_ref[0], W), _lanes(twi_ref[0], W)
    xr = ar * twr - ai * twi
    xi = ar * twi + ai * twr
    z = jnp.dot(g_ref[...], jnp.concatenate([xr, xi], axis=0).astype(BF16), preferred_element_type=F32)
    n2 = z.shape[0] // 2
    zr, zi = z[:n2], z[n2:]
    ss = ss_ref[...]
    for o in range(2):
        f0, b0 = (2 * o) * HY_CH, (2 * o + 1) * HY_CH
        sc = lax.rsqrt(ss[:, f0:f0 + HY_CH] + ss[:, b0:b0 + HY_CH] + 1e-6)
        hf_ref[o, 0, 0] = (zr[:, f0:f0 + HY_CH] + zr[:, b0:b0 + HY_CH]) * sc
        hf_ref[o, 0, 1] = (zi[:, f0:f0 + HY_CH] - zi[:, b0:b0 + HY_CH]) * sc


def _filter_spectrum(h, ss, dc):
    L, NC = h.shape
    N1, N2, half = dc["N1"], dc["N2"], dc["half"]
    a = _dft_stage1(dc["fa"], h.reshape(1, half, N2, NC))
    a = a.reshape(2, N1, N2, NC)
    return pl.pallas_call(
        _filter_spec_body,
        grid=(N1,),
        in_specs=[pl.BlockSpec((2, 1, N2, NC), lambda k: (0, k, 0, 0)),
                  pl.BlockSpec((1, N2, LANES), lambda k: (k, 0, 0)), pl.BlockSpec((1, N2, LANES), lambda k: (k, 0, 0)),
                  _full((2 * N2, 2 * N2)), _full((1, NC))],
        out_specs=pl.BlockSpec((2, 1, 2, N2, HY_CH), lambda k: (0, k, 0, 0, 0)),
        out_shape=jax.ShapeDtypeStruct((2, N1, 2, N2, HY_CH), F32),
        compiler_params=_cp("parallel"),
        name="hy_filter_spec",
    )(a, dc["twr"], dc["twi"], dc["g"].astype(BF16), ss)


SPEC_K1 = 4


def _spec_mul_body(a_ref, twr_ref, twi_ref, g_ref, gc_ref, hf_ref, o_ref):
    W = a_ref.shape[-1]
    ks = range(a_ref.shape[2])
    n2 = g_ref.shape[0] // 2
    x = []
    for k in ks:
        ar, ai = a_ref[0, 0, k], a_ref[0, 1, k]
        twr, twi = _lanes(twr_ref[k], W), _lanes(twi_ref[k], W)
        x.append(jnp.concatenate([ar * twr - ai * twi, ar * twi + ai * twr], axis=0).astype(BF16))
    z = [jnp.dot(g_ref[...], x_, preferred_element_type=F32) for x_ in x]
    y = []
    for k, z_ in zip(ks, z):
        zr, zi = z_[:n2], z_[n2:]
        hr, hi = hf_ref[0, k, 0], hf_ref[0, k, 1]
        y.append(jnp.concatenate([zr * hr - zi * hi, zr * hi + zi * hr], axis=0).astype(BF16))
    b = [jnp.dot(gc_ref[...], y_, preferred_element_type=F32) for y_ in y]
    for k, b_ in zip(ks, b):
        br, bi = b_[:n2], b_[n2:]
        twr, twi = _lanes(twr_ref[k], W), _lanes(twi_ref[k], W)
        o_ref[0, 0, k] = br * twr + bi * twi
        o_ref[0, 1, k] = bi * twr - br * twi


def _spec_mul(a, hf, order, dc):
    B = a.shape[0]
    N1, N2 = dc["N1"], dc["N2"]
    C = a.shape[-1]
    kb = min(SPEC_K1, N1)
    blk = pl.BlockSpec((1, 2, kb, N2, C), lambda b, k: (b, 0, k, 0, 0))
    tw = pl.BlockSpec((kb, N2, LANES), lambda b, k: (k, 0, 0))
    return pl.pallas_call(
        _spec_mul_body,
        grid=(B, N1 // kb),
        in_specs=[blk, tw, tw, _full((2 * N2, 2 * N2)), _full((2 * N2, 2 * N2)),
                  pl.BlockSpec((1, kb, 2, N2, C), lambda b, k: (order, k, 0, 0, 0))],
        out_specs=blk,
        out_shape=jax.ShapeDtypeStruct(a.shape, F32),
        compiler_params=_cp("parallel", "parallel"),
        name="hy_spec_mul",
    )(a, dc["twr"], dc["twi"], dc["g"].astype(BF16), dc["gc"].astype(BF16), hf)


def _dft3_body(f_ref, b_ref, u_ref, gate_ref, skip_ref, o_ref):
    bm = b_ref[0]
    b2 = bm.reshape(bm.shape[0] * SUBLANES, bm.shape[2]).astype(BF16)
    y = jnp.dot(f_ref[...], b2, preferred_element_type=F32)
    rows, C = y.shape
    u = u_ref[0].reshape(rows, C)
    gate = gate_ref[0].reshape(rows, C)
    o_ref[0] = (gate * (y + u * skip_ref[...])).reshape(o_ref.shape[1], SUBLANES, C)


def _dft_stage3(fb, bm, u, gate, skip):
    B, R, N2, C = bm.shape
    half = u.shape[1]
    row = pl.BlockSpec((1, half, SUBLANES, C), lambda b, j: (b, 0, j, 0))
    return pl.pallas_call(
        _dft3_body,
        grid=(B, N2 // SUBLANES),
        in_specs=[_full(fb.shape), pl.BlockSpec((1, R, SUBLANES, C), lambda b, j: (b, 0, j, 0)), row, row, _full((1, C))],
        out_specs=row,
        out_shape=jax.ShapeDtypeStruct((B, half, N2, C), F32),
        compiler_params=_cp("parallel", "parallel"),
        name="hy_dft3",
    )(fb.astype(BF16), bm, u, gate, skip[None, :])


def _longconv_gated(u, gate, hf, order, skip, dc):
    B, L, C = u.shape
    N1, N2, half = dc["N1"], dc["N2"], dc["half"]
    u4 = u.reshape(B, half, N2, C)
    a = _dft_stage1(dc["fa"], u4).reshape(B, 2, N1, N2, C)
    bm = _spec_mul(a, hf, order, dc).reshape(B, 2 * N1, N2, C)
    return _dft_stage3(dc["fb"], bm, u4, gate.reshape(B, half, N2, C), skip).reshape(B, L, C)


def _hyena(hyu, conv_w, conv_b, filt, skip):
    B, L, _ = hyu.shape
    v, x1, x2 = _shortconv(hyu, conv_w, conv_b)
    h, ss = _filters(L, *filt)
    dc = _dft_consts(L)
    hf = _filter_spectrum(h, ss, dc)
    z1 = _longconv_gated(v, x1, hf, 0, skip[0], dc)
    return _longconv_gated(z1, x2, hf, 1, skip[1], dc)


def _hyena_ctx_body(v_ref, x1_ref, x2_ref, h_ref, ss_ref, skip_ref, fc_ref, gc_ref, o_ref):
    fc, gc = fc_ref[...], gc_ref[...]
    n = fc.shape[0] // 2
    ss = ss_ref[...]
    h = h_ref[...]

    def conv(u, o):
        f0, b0 = (2 * o) * HY_CH, (2 * o + 1) * HY_CH
        sc = lax.rsqrt(ss[:, f0:f0 + HY_CH] + ss[:, b0:b0 + HY_CH] + 1e-6)
        x = jnp.dot(fc, u, precision=HI, preferred_element_type=F32)
        hf = jnp.dot(fc, h[:, f0:f0 + HY_CH], precision=HI, preferred_element_type=F32)
        hb = jnp.dot(fc, h[:, b0:b0 + HY_CH], precision=HI, preferred_element_type=F32)
        hr = (hf[:n] + hb[:n]) * sc
        hi = (hf[n:] - hb[n:]) * sc
        yr = x[:n] * hr - x[n:] * hi
        yi = x[:n] * hi + x[n:] * hr
        y = jnp.dot(gc, jnp.concatenate([yr, yi], axis=0), precision=HI, preferred_element_type=F32)
        return y + u * skip_ref[o:o + 1, :]

    z1 = x1_ref[0] * conv(v_ref[0], 0)
    o_ref[0] = x2_ref[0] * conv(z1, 1)


def _hyena_ctx(hyu, conv_w, conv_b, filt, skip):
    B, L, _ = hyu.shape
    v, x1, x2 = _shortconv(hyu, conv_w, conv_b)
    h, ss = _filters(L, *filt)
    N = 2 * L
    ang = 2.0 * np.pi * (np.arange(N)[:, None] * np.arange(L)[None, :] % N) / N
    fr, fi = np.cos(ang), -np.sin(ang)
    fc = jnp.asarray(np.concatenate([fr, fi], axis=0), dtype=F32)
    gc = jnp.asarray(np.concatenate([fr.T, fi.T], axis=1) / N, dtype=F32)
    row = pl.BlockSpec((1, L, HY_CH), lambda b: (b, 0, 0))
    return pl.pallas_call(
        _hyena_ctx_body,
        grid=(B,),
        in_specs=[row, row, row, _full(h.shape), _full(ss.shape), _full(skip.shape), _full(fc.shape), _full(gc.shape)],
        out_specs=row,
        out_shape=jax.ShapeDtypeStruct((B, L, HY_CH), F32),
        compiler_params=_cp("parallel"),
        name="hyena_ctx",
    )(v, x1, x2, h, ss, skip, fc, gc)


HEAD_PAD = 128


def _rope_swap(w):
    a, b, c, d = w[..., 0:8], w[..., 8:16], w[..., 16:24], w[..., 24:32]
    return jnp.concatenate([-b, a, -d, c], axis=-1)


def _arrange_wq(w_uq):
    R = w_uq.shape[0]
    w = w_uq.reshape(R, MLA_HEADS, MLA_NOPE + MLA_ROPE)
    rope = w[..., MLA_NOPE:]
    out = jnp.concatenate([w[..., :MLA_NOPE], rope, _rope_swap(rope)], axis=-1)
    return out.reshape(R, MLA_HEADS * HEAD_PAD).astype(BF16)


def _arrange_wkv(w_ukv):
    R = w_ukv.shape[0]
    w = w_ukv.reshape(R, MLA_HEADS, MLA_NOPE + MLA_V)
    wk = jnp.concatenate([w[..., :MLA_NOPE], jnp.zeros((R, MLA_HEADS, HEAD_PAD - MLA_NOPE), w.dtype)], axis=-1)
    wv = w[..., MLA_NOPE:]
    return wk.reshape(R, MLA_HEADS * HEAD_PAD).astype(BF16), wv.reshape(R, MLA_HEADS * MLA_V).astype(BF16)


def _kr_place():
    e = np.zeros((LANES, MLA_HEADS * HEAD_PAD), np.float32)
    es = np.zeros((LANES, MLA_HEADS * HEAD_PAD), np.float32)
    for h in range(MLA_HEADS):
        base = h * HEAD_PAD + MLA_NOPE
        for j in range(MLA_ROPE):
            e[j, base + j] = 1.0
            blk, r = divmod(j, 16)
            if r < 8:
                es[16 * blk + r + 8, base + j] = -1.0
            else:
                es[16 * blk + r - 8, base + j] = 1.0
    return jnp.asarray(e).astype(BF16), jnp.asarray(es).astype(BF16)


def _rope_tables(L, rope):
    if rope:
        t = np.arange(L)
        row, col = (t // GRID_W).astype(np.float32), (t % GRID_W).astype(np.float32)
        half = MLA_ROPE // 2
        inv = ROPE_BASE ** (-jnp.arange(0, half, 2, dtype=F32) / half)
        ar = jnp.asarray(row)[:, None] * inv
        ac = jnp.asarray(col)[:, None] * inv
        cos = jnp.concatenate([jnp.cos(ar), jnp.cos(ar), jnp.cos(ac), jnp.cos(ac)], axis=-1)
        sin = jnp.concatenate([jnp.sin(ar), jnp.sin(ar), jnp.sin(ac), jnp.sin(ac)], axis=-1)
    else:
        cos, sin = jnp.ones((L, MLA_ROPE), F32), jnp.zeros((L, MLA_ROPE), F32)
    return cos, sin


def _rms_rows(x, g, eps=1e-6):
    return x * lax.rsqrt(jnp.mean(x * x, axis=-1, keepdims=True) + eps) * g


def _qproj_body(cq_ref, g_ref, w_ref, t1_ref, t2_ref, q_ref):
    xn = _rms_rows(cq_ref[0], g_ref[...])
    acc = jnp.dot(xn.astype(BF16), w_ref[...], preferred_element_type=F32)
    W = acc.shape[1]
    t1, t2 = _lanes(t1_ref[...], W), _lanes(t2_ref[...], W)
    q_ref[0] = (acc * t1 + pltpu.roll(acc, W - MLA_ROPE, 1) * t2).astype(q_ref.dtype)


def _qproj(cq, g, wq, cos, sin):
    B, L, R = cq.shape
    tm = min(512, L)
    W = wq.shape[1]
    ones, zeros = jnp.ones((L, MLA_NOPE), F32), jnp.zeros((L, MLA_ROPE), F32)
    qs = MLA_SCALE * math.log2(math.e)
    t1 = jnp.concatenate([ones, cos, zeros], axis=-1) * qs
    t2 = jnp.concatenate([jnp.zeros((L, MLA_NOPE), F32), sin, zeros], axis=-1) * qs
    tab = pl.BlockSpec((tm, HEAD_PAD), lambda b, i: (i, 0))
    return pl.pallas_call(
        _qproj_body,
        grid=(B, L // tm),
        in_specs=[pl.BlockSpec((1, tm, R), lambda b, i: (b, i, 0)), _full((1, R)), _full((R, W)), tab, tab],
        out_specs=pl.BlockSpec((1, tm, W), lambda b, i: (b, i, 0)),
        out_shape=jax.ShapeDtypeStruct((B, L, W), BF16),
        compiler_params=_cp("parallel", "parallel"),
        name="mla_qproj",
    )(cq, g[None, :], wq, t1, t2)


def _kvproj_body(c_ref, g_ref, wk_ref, wv_ref, e_ref, es_ref, cos_ref, sin_ref, k_ref, v_ref):
    c = c_ref[0]
    R = MLA_KV_RANK
    xn = _rms_rows(c[:, :R], g_ref[...]).astype(BF16)
    kr = c[:, R:]
    acc = jnp.dot(xn, wk_ref[...], preferred_element_type=F32)
    acc += jnp.dot((kr * cos_ref[...]).astype(BF16), e_ref[...], preferred_element_type=F32)
    acc += jnp.dot((kr * sin_ref[...]).astype(BF16), es_ref[...], preferred_element_type=F32)
    k_ref[0] = acc.astype(k_ref.dtype)
    v_ref[0] = jnp.dot(xn, wv_ref[...], preferred_element_type=F32).astype(v_ref.dtype)


def _kvproj(ckvr, g, wk, wv, cos, sin):
    B, L, Wc = ckvr.shape
    tm = next(t for t in (1280, 512, 256, L) if L % t == 0)
    pad = jnp.zeros((L, LANES - MLA_ROPE), F32)
    cos_p, sin_p = jnp.concatenate([cos, pad], axis=-1), jnp.concatenate([sin, pad], axis=-1)
    e, es = _kr_place()
    tab = pl.BlockSpec((tm, LANES), lambda b, i: (i, 0))
    Wk, Wv = wk.shape[1], wv.shape[1]
    return pl.pallas_call(
        _kvproj_body,
        grid=(B, L // tm),
        in_specs=[pl.BlockSpec((1, tm, Wc), lambda b, i: (b, i, 0)), _full((1, MLA_KV_RANK)),
                  _full(wk.shape), _full(wv.shape), _full(e.shape), _full(es.shape), tab, tab],
        out_specs=[pl.BlockSpec((1, tm, Wk), lambda b, i: (b, i, 0)), pl.BlockSpec((1, tm, Wv), lambda b, i: (b, i, 0))],
        out_shape=[jax.ShapeDtypeStruct((B, L, Wk), BF16), jax.ShapeDtypeStruct((B, L, Wv), BF16)],
        compiler_params=_cp("parallel", "parallel"),
        name="mla_kvproj",
    )(ckvr, g[None, :], wk, wv, e, es, cos_p, sin_p)


FLASH_ROWS = 256
FLASH_KEYS = 256


def _flash_body(q_ref, k_ref, v_ref, o_ref, m_ref, l_ref, acc_ref, s_ref, *, R):
    j = pl.program_id(3)
    tq, tk = q_ref.shape[1], k_ref.shape[1]
    CK = FLASH_KEYS
    npc = CK // LANES

    @pl.when(j == 0)
    def _():
        m_ref[...] = jnp.full_like(m_ref, -jnp.inf)
        l_ref[...] = jnp.zeros_like(l_ref)
        acc_ref[...] = jnp.zeros_like(acc_ref)

    def pass1(a, r):
        lo, r0 = a * HEAD_PAD, r * R
        q = q_ref[0, r0:r0 + R, lo:lo + HEAD_PAD]
        mp = None
        for c in range(tk // CK):
            kc = k_ref[0, c * CK:(c + 1) * CK, lo:lo + HEAD_PAD]
            s = lax.dot_general(q, kc, (((1,), (1,)), ((), ())), preferred_element_type=F32)
            s_ref[r0:r0 + R, c * CK:(c + 1) * CK] = s
            for w in range(npc):
                pc = s[:, w * LANES:(w + 1) * LANES]
                mp = pc if mp is None else jnp.maximum(mp, pc)
        m_old = m_ref[a, r0:r0 + R, :]
        return m_old, jnp.maximum(m_old, jnp.max(mp, axis=1, keepdims=True))

    def pass2(a, r, m_old, m_new):
        r0 = r * R
        alpha = jnp.exp2(m_old - m_new)
        lp = jnp.zeros((R, LANES), F32)
        pv = jnp.zeros((R, 2 * MLA_V), F32)
        for c in range(tk // CK):
            s = s_ref[r0:r0 + R, c * CK:(c + 1) * CK]
            ps = [jnp.exp2(s[:, w * LANES:(w + 1) * LANES] - m_new) for w in range(npc)]
            for p_ in ps:
                lp = lp + p_
            p = jnp.concatenate(ps, axis=1).astype(BF16)
            pv = pv + jnp.dot(p, v_ref[0, c * CK:(c + 1) * CK, :], preferred_element_type=F32)
        l_ref[a, r0:r0 + R, :] = alpha * l_ref[a, r0:r0 + R, :] + jnp.sum(lp, axis=1, keepdims=True)
        acc_ref[a, r0:r0 + R, :] = alpha * acc_ref[a, r0:r0 + R, :] + pv
        m_ref[a, r0:r0 + R, :] = m_new

    assert tq // R >= 2
    blocks = [(a, r) for a in range(2) for r in range(tq // R)]
    pend = pass1(*blocks[0])
    for i, blk in enumerate(blocks):
        nxt = pass1(*blocks[i + 1]) if i + 1 < len(blocks) else None
        pass2(*blk, *pend)
        pend = nxt

    @pl.when(j == pl.num_programs(3) - 1)
    def _():
        lane = lax.broadcasted_iota(jnp.int32, acc_ref.shape[1:], 1)
        o_ref[0] = jnp.where(lane < MLA_V, acc_ref[0] / l_ref[0], acc_ref[1] / l_ref[1])


def _flash_tiles(Lq, Lk):
    tq = min(1024, Lq)
    tk = next(t for t in (3328, 1280, 1024, 512, 256, Lk) if Lk % t == 0)
    return tq, tk


def _flash(q, k, v):
    B, Lq, _ = q.shape
    Lk = k.shape[1]
    tq, tk = _flash_tiles(Lq, Lk)
    hp = MLA_HEADS // 2
    return pl.pallas_call(
        functools.partial(_flash_body, R=min(FLASH_ROWS, tq // 2)),
        grid=(B, hp, Lq // tq, Lk // tk),
        in_specs=[pl.BlockSpec((1, tq, 2 * HEAD_PAD), lambda b, h, i, j: (b, i, h)),
                  pl.BlockSpec((1, tk, 2 * HEAD_PAD), lambda b, h, i, j: (b, j, h)),
                  pl.BlockSpec((1, tk, 2 * MLA_V), lambda b, h, i, j: (b, j, h))],
        out_specs=pl.BlockSpec((1, tq, 2 * MLA_V), lambda b, h, i, j: (b, i, h)),
        out_shape=jax.ShapeDtypeStruct((B, Lq, MLA_HEADS * MLA_V), F32),
        scratch_shapes=[pltpu.VMEM((2, tq, LANES), F32), pltpu.VMEM((2, tq, LANES), F32),
                        pltpu.VMEM((2, tq, 2 * MLA_V), F32), pltpu.VMEM((tq, tk), F32)],
        compiler_params=_cp("parallel", "parallel", "parallel", "arbitrary"),
        name="mla_flash",
    )(q, k, v)


def _layernorm_rows(x, g, b, eps=1e-5):
    mu = jnp.mean(x, axis=-1, keepdims=True)
    xc = x - mu
    var = jnp.mean(xc * xc, axis=-1, keepdims=True)
    return xc * lax.rsqrt(var + eps) * g + b


def _outproj_body(of_ref, ob_ref, g_ref, hy_ref, om_ref, x_ref, gate_ref, gg_ref, hg_ref, mg_ref,
                  w_ref, lg_ref, lb_ref, o_ref, *, alpha):
    VD = GLA_HEADS * GLA_DV
    o = of_ref[0] + ob_ref[0]
    r = _idiv(lax.broadcasted_iota(jnp.int32, (VD, VD), 0), GLA_DV)
    c = _idiv(lax.broadcasted_iota(jnp.int32, (VD, VD), 1), GLA_DV)
    grp = (r == c).astype(F32)
    ms = jnp.dot(o * o, grp, precision=HI, preferred_element_type=F32) * (1.0 / GLA_DV)
    g = g_ref[0]
    ya = o * lax.rsqrt(ms + 1e-6) * gg_ref[...] * (g * jax.nn.sigmoid(g))
    yb = _rms_rows(hy_ref[0], hg_ref[...])
    yc = _rms_rows(om_ref[0], mg_ref[...])
    acc = jnp.dot(ya.astype(BF16), w_ref[0:VD, :], preferred_element_type=F32)
    acc += jnp.dot(yb.astype(BF16), w_ref[VD:VD + HY_CH, :], preferred_element_type=F32)
    acc += jnp.dot(yc.astype(BF16), w_ref[VD + HY_CH:, :], preferred_element_type=F32)
    o_ref[0] = _layernorm_rows(alpha * x_ref[0] + gate_ref[0] * acc, lg_ref[...], lb_ref[...])


def _outproj(of, ob, vg, hy, om, x, gate, gla_g, hy_g, mla_g, w_out, ln_g, ln_b, alpha):
    B, L, D = x.shape
    tm = min(512, L)
    VD = GLA_HEADS * GLA_DV
    MD = MLA_HEADS * MLA_V
    row = lambda w: pl.BlockSpec((1, tm, w), lambda b, i: (b, i, 0))
    return pl.pallas_call(
        functools.partial(_outproj_body, alpha=alpha),
        grid=(B, L // tm),
        in_specs=[row(VD), row(VD), pl.BlockSpec((1, tm, VD), lambda b, i: (b, i, 1)), row(HY_CH), row(MD), row(D),
                  pl.BlockSpec((1, 1, D), lambda b, i: (b, 0, 0)), _full((1, VD)), _full((1, HY_CH)), _full((1, MD)),
                  _full(w_out.shape), _full((1, D)), _full((1, D))],
        out_specs=row(D),
        out_shape=jax.ShapeDtypeStruct((B, L, D), F32),
        compiler_params=_cp("parallel", "parallel"),
        name="outproj",
    )(of, ob, vg, hy, om, x, gate, jnp.tile(gla_g, GLA_HEADS)[None, :], hy_g[None, :], mla_g[None, :],
      w_out.astype(BF16), ln_g[None, :], ln_b[None, :])


def _ffn_body(x_ref, sh_ref, sc_ref, gate_ref, w1_ref, w3_ref, w2_ref, lg_ref, lb_ref, o_ref, h_ref, acc_ref, *, alpha):
    j = pl.program_id(2)

    @pl.when(j == 0)
    def _():
        h_ref[...] = (x_ref[0] * (1.0 + sc_ref[0]) + sh_ref[0]).astype(BF16)
        acc_ref[...] = jnp.zeros_like(acc_ref)

    h = h_ref[...]
    a = jnp.dot(h, w1_ref[...], preferred_element_type=F32)
    b = jnp.dot(h, w3_ref[...], preferred_element_type=F32)
    t = (a * jax.nn.sigmoid(a) * b).astype(BF16)
    acc_ref[...] += jnp.dot(t, w2_ref[...], preferred_element_type=F32)

    @pl.when(j == pl.num_programs(2) - 1)
    def _():
        o_ref[0] = _layernorm_rows(alpha * x_ref[0] + gate_ref[0] * acc_ref[...], lg_ref[...], lb_ref[...])


def _ffn_tile(F):
    for cand in (512, 256, 128):
        if F % cand == 0:
            return cand
    return F


def _ffn(x, shift, scale, gate, w1, w3, w2, ln_g, ln_b, alpha):
    B, L, D = x.shape
    F = w1.shape[1]
    tm = min(1024, L)
    tf = _ffn_tile(F)
    row = pl.BlockSpec((1, tm, D), lambda b, i, j: (b, i, 0))
    vec = pl.BlockSpec((1, 1, D), lambda b, i, j: (b, 0, 0))
    return pl.pallas_call(
        functools.partial(_ffn_body, alpha=alpha),
        grid=(B, L // tm, F // tf),
        in_specs=[row, vec, vec, vec,
                  pl.BlockSpec((D, tf), lambda b, i, j: (0, j)), pl.BlockSpec((D, tf), lambda b, i, j: (0, j)),
                  pl.BlockSpec((tf, D), lambda b, i, j: (j, 0)), _full((1, D)), _full((1, D))],
        out_specs=row,
        out_shape=jax.ShapeDtypeStruct((B, L, D), F32),
        scratch_shapes=[pltpu.VMEM((tm, D), BF16), pltpu.VMEM((tm, D), F32)],
        compiler_params=_cp("parallel", "parallel", "arbitrary"),
        name="ffn",
    )(x, shift, scale, gate, w1.astype(BF16), w3.astype(BF16), w2.astype(BF16), ln_g[None, :], ln_b[None, :])


MOE_TOKENS = 2048
MOE_ROWS = 256
RANK_CHUNK = 256


def _router_body(x_ref, sh_ref, sc_ref, wr_ref, h_ref, g_ref, rk_ref, rkt_ref, cnt_ref):
    h = x_ref[0] * (1.0 + sc_ref[0]) + sh_ref[0]
    h_ref[0] = h.astype(BF16)
    logits = jnp.dot(h, wr_ref[...], precision=HI, preferred_element_type=F32)
    lane = lax.broadcasted_iota(jnp.int32, logits.shape, 1).astype(F32)
    logits = jnp.where(lane < N_EXPERTS, logits, -jnp.inf)
    m1 = jnp.max(logits, axis=1, keepdims=True)
    i1 = jnp.min(jnp.where(logits == m1, lane, float(LANES)), axis=1, keepdims=True)
    rest = jnp.where(lane == i1, -jnp.inf, logits)
    m2 = jnp.max(rest, axis=1, keepdims=True)
    i2 = jnp.min(jnp.where(rest == m2, lane, float(LANES)), axis=1, keepdims=True)
    e2 = jnp.exp(m2 - m1)
    w1 = 1.0 / (1.0 + e2)
    w2 = e2 / (1.0 + e2)
    g_ref[0] = jnp.where(lane == i1, w1, 0.0) + jnp.where(lane == i2, w2, 0.0)
    sel = jnp.logical_or(lane == i1, lane == i2)
    self_ = sel.astype(F32)
    tm = h.shape[0]
    C = min(RANK_CHUNK, tm)
    r = lax.broadcasted_iota(jnp.int32, (C, C), 0)
    c = lax.broadcasted_iota(jnp.int32, (C, C), 1)
    tri = (c < r).astype(BF16)
    carry = jnp.zeros((1, LANES), F32)
    parts = []
    for k in range(tm // C):
        sk = self_[k * C:(k + 1) * C]
        parts.append(jnp.dot(tri, sk.astype(BF16), preferred_element_type=F32) + carry)
        carry = carry + jnp.sum(sk, axis=0, keepdims=True)
    rank = jnp.where(sel, jnp.concatenate(parts, axis=0), -1.0)
    rk_ref[0] = rank
    rkt_ref[0] = rank.T[:8]
    cnt_ref[0, 0] = carry


def _router(x, shift, scale, w_router):
    B, L, D = x.shape
    tm = min(MOE_TOKENS, L)
    nt = L // tm
    wr = jnp.pad(w_router, ((0, 0), (0, LANES - N_EXPERTS)))
    vec = pl.BlockSpec((1, 1, D), lambda b, i: (b, 0, 0))
    col = pl.BlockSpec((1, tm, LANES), lambda b, i: (b, i, 0))
    return pl.pallas_call(
        _router_body,
        grid=(B, nt),
        in_specs=[pl.BlockSpec((1, tm, D), lambda b, i: (b, i, 0)), vec, vec, _full((D, LANES))],
        out_specs=[pl.BlockSpec((1, tm, D), lambda b, i: (b, i, 0)), col, col,
                   pl.BlockSpec((1, 8, tm), lambda b, i: (b, 0, i)), pl.BlockSpec((1, 1, 1, LANES), lambda b, i: (b, i, 0, 0))],
        out_shape=[jax.ShapeDtypeStruct((B, L, D), BF16), jax.ShapeDtypeStruct((B, L, LANES), F32),
                   jax.ShapeDtypeStruct((B, L, LANES), F32), jax.ShapeDtypeStruct((B, 8, L), F32),
                   jax.ShapeDtypeStruct((B, nt, 1, LANES), F32)],
        compiler_params=_cp("parallel", "parallel"),
        name="moe_router",
    )(x, shift, scale, wr)


def _moe_body(cnt_ref, h_ref, g_ref, rk_ref, rkt_ref, w1_ref, w3_ref, w2_ref, o_ref, xg_ref, y_ref, *, M, P):
    b, i, e, j = pl.program_id(0), pl.program_id(1), pl.program_id(2), pl.program_id(3)
    nt, ne, nj = pl.num_programs(1), pl.num_programs(2), pl.num_programs(3)
    tm = h_ref.shape[1]
    cnt = cnt_ref[(b * nt + i) * ne + e]
    n_ch = lax.div(cnt + (M - 1), M)

    @pl.when(jnp.logical_and(e == 0, j == 0))
    def _():
        o_ref[...] = jnp.zeros_like(o_ref)

    @pl.when(j == 0)
    def _():
        rkt = rkt_ref[0, pl.ds(e, 1), :]

        def gather(c, carry):
            r0 = pl.multiple_of(c * M, 16)
            rows = (lax.broadcasted_iota(jnp.int32, (M, 1), 0) + c * M).astype(F32)
            onehot = (rkt == rows).astype(BF16)
            xg_ref[pl.ds(r0, M), :] = jnp.dot(onehot, h_ref[0], preferred_element_type=F32).astype(BF16)
            return carry

        lax.fori_loop(0, n_ch, gather, 0)

    def expert(chunks):
        r0 = [pl.multiple_of(c * M, 16) for c in chunks]
        xg = [xg_ref[pl.ds(r, M), :] for r in r0]
        a = [jnp.dot(x_, w1_ref[0], preferred_element_type=F32) for x_ in xg]
        g = [jnp.dot(x_, w3_ref[0], preferred_element_type=F32) for x_ in xg]
        t = [(a_ * jax.nn.sigmoid(a_) * g_).astype(BF16) for a_, g_ in zip(a, g)]
        yv = [jnp.dot(t_, w2_ref[0], preferred_element_type=F32) for t_ in t]

        @pl.when(j == 0)
        def _():
            for r, y_ in zip(r0, yv):
                y_ref[pl.ds(r, M), :] = y_

        @pl.when(j > 0)
        def _():
            for r, y_ in zip(r0, yv):
                y_ref[pl.ds(r, M), :] += y_

    def expert_pair(c2, carry):
        expert([2 * c2, 2 * c2 + 1])
        return carry

    lax.fori_loop(0, lax.div(n_ch, 2), expert_pair, 0)

    @pl.when(lax.rem(n_ch, 2) == 1)
    def _():
        expert([n_ch - 1])

    @pl.when(j == nj - 1)
    def _():
        for p in range(tm // P):
            lane = lax.broadcasted_iota(jnp.int32, (P, LANES), 1)
            rke = jnp.sum(jnp.where(lane == e, rk_ref[0, p * P:(p + 1) * P, :], 0.0), axis=1, keepdims=True)
            ge = jnp.sum(jnp.where(lane == e, g_ref[0, p * P:(p + 1) * P, :], 0.0), axis=1, keepdims=True)

            def scatter(c, carry):
                r0 = pl.multiple_of(c * M, 16)
                cols = (lax.broadcasted_iota(jnp.int32, (1, M), 1) + c * M).astype(F32)
                onehot = (rke == cols).astype(BF16)
                yb = y_ref[pl.ds(r0, M), :].astype(BF16)
                o_ref[0, p * P:(p + 1) * P, :] += ge * jnp.dot(onehot, yb, preferred_element_type=F32)
                return carry

            lax.fori_loop(0, n_ch, scatter, 0)


def _res_ln_body(x_ref, y_ref, gate_ref, lg_ref, lb_ref, o_ref, *, alpha):
    o_ref[0] = _layernorm_rows(alpha * x_ref[0] + gate_ref[0] * y_ref[0], lg_ref[...], lb_ref[...])


def _res_ln(x, y, gate, ln_g, ln_b, alpha):
    B, L, D = x.shape
    tm = min(1024, L)
    row = pl.BlockSpec((1, tm, D), lambda b, i: (b, i, 0))
    return pl.pallas_call(
        functools.partial(_res_ln_body, alpha=alpha),
        grid=(B, L // tm),
        in_specs=[row, row, pl.BlockSpec((1, 1, D), lambda b, i: (b, 0, 0)), _full((1, D)), _full((1, D))],
        out_specs=row,
        out_shape=jax.ShapeDtypeStruct((B, L, D), F32),
        compiler_params=_cp("parallel", "parallel"),
        name="res_ln",
    )(x, y, gate, ln_g[None, :], ln_b[None, :])


def _moe(x, shift, scale, gate, w_router, w1, w3, w2, ln_g, ln_b, alpha):
    B, L, D = x.shape
    E, _, F = w1.shape
    hb, gts, rk, rkt, cnt = _router(x, shift, scale, w_router)
    tm = min(MOE_TOKENS, L)
    nt = L // tm
    M = MOE_ROWS
    rows_max = -(-tm // M) * M
    tf = next(t for t in (896, 512, 256, 128, F) if F % t == 0)
    counts = cnt[:, :, 0, :E].astype(jnp.int32).reshape(-1)
    row = lambda w: pl.BlockSpec((1, tm, w), lambda b, i, e, j, c: (b, i, 0))
    y = pl.pallas_call(
        functools.partial(_moe_body, M=M, P=min(512, tm)),
        grid_spec=pltpu.PrefetchScalarGridSpec(
            num_scalar_prefetch=1,
            grid=(B, nt, E, F // tf),
            in_specs=[row(D), row(LANES), row(LANES), pl.BlockSpec((1, 8, tm), lambda b, i, e, j, c: (b, 0, i)),
                      pl.BlockSpec((1, D, tf), lambda b, i, e, j, c: (e, 0, j)),
                      pl.BlockSpec((1, D, tf), lambda b, i, e, j, c: (e, 0, j)),
                      pl.BlockSpec((1, tf, D), lambda b, i, e, j, c: (e, j, 0))],
            out_specs=row(D),
            scratch_shapes=[pltpu.VMEM((rows_max, D), BF16), pltpu.VMEM((rows_max, D), F32)],
        ),
        out_shape=jax.ShapeDtypeStruct((B, L, D), F32),
        compiler_params=_cp("parallel", "parallel", "arbitrary", "arbitrary"),
        name="moe",
    )(counts, hb, gts, rk, rkt, w1.astype(BF16), w3.astype(BF16), w2.astype(BF16))
    return _res_ln(x, y, gate, ln_g, ln_b, alpha)


def _mod_body(c_ref, w_ref, b_ref, o_ref):
    c = c_ref[...]
    s = c * jax.nn.sigmoid(c)
    o_ref[...] = jnp.dot(s, w_ref[...], precision=HI, preferred_element_type=F32) + b_ref[...]


def _modulation(cc, w_mod, b_mod):
    R, D = cc.shape
    N = w_mod.shape[1]
    tn = 1024
    return pl.pallas_call(
        _mod_body,
        grid=(N // tn,),
        in_specs=[_full((R, D)), pl.BlockSpec((D, tn), lambda j: (0, j)), pl.BlockSpec((1, tn), lambda j: (0, j))],
        out_specs=pl.BlockSpec((R, tn), lambda j: (0, j)),
        out_shape=jax.ShapeDtypeStruct((R, N), F32),
        compiler_params=_cp("parallel"),
        name="modulation",
    )(cc, w_mod, b_mod[None, :])


def _streams(x, c, ctx, c_ctx, w_mod, b_mod, w_in, gla_w_gate, gla_b_gate, gla_norm_g, hy_conv_w, hy_conv_b, hy_f_w1, hy_f_b1, hy_f_freq1, hy_f_w2, hy_f_b2, hy_f_freq2, hy_f_w3, hy_f_b3, hy_skip, hy_norm_g, mla_q_norm_g, mla_w_uq, mla_kv_norm_g, mla_w_ukv, mla_norm_g, w_out, ln_g, ln_b, ffn_w1, ffn_w3, ffn_w2, moe_router, moe_w1, moe_w3, moe_w2):
    B, L, D = x.shape
    Lc = ctx.shape[1]
    depth = w_mod.shape[0]
    alpha = (2.0 * depth) ** 0.25
    cc = jnp.zeros((8, D), F32).at[:B].set(c).at[B].set(c_ctx)
    cos, sin = _rope_tables(L, True)
    cos_c, sin_c = _rope_tables(Lc, False)
    cos_all, sin_all = jnp.concatenate([cos_c, cos], axis=0), jnp.concatenate([sin_c, sin], axis=0)
    KD, VD = GLA_HEADS * GLA_DK, GLA_HEADS * GLA_DV
    xc = ctx
    for l in range(depth):
        need_ctx = l < depth - 1
        mods = _modulation(cc, w_mod[l], b_mod[l])
        m = [mods[:B, k * D:(k + 1) * D][:, None, :] for k in range(6)]
        mc = [jnp.broadcast_to(mods[B, k * D:(k + 1) * D][None, None, :], (B, 1, D)) for k in range(6)]
        w_arr = _arrange_w_in(w_in[l])
        wg, bg = _arrange_gate(gla_w_gate[l], gla_b_gate[l])
        filt = (hy_f_w1[l], hy_f_b1[l], hy_f_freq1[l], hy_f_w2[l], hy_f_b2[l], hy_f_freq2[l], hy_f_w3[l], hy_f_b3[l])
        wq = _arrange_wq(mla_w_uq[l])
        wk, wv = _arrange_wkv(mla_w_ukv[l])

        hyu, qk, vg, alr, cq, ckvr = _inproj(x, m[0], m[1], w_arr)
        hyu_c, qk_c, vg_c, alr_c, cq_c, ckvr_c = _inproj(xc, mc[0], mc[1], w_arr)

        of_c, ob_c, s_c = _gla(qk_c, vg_c, alr_c, wg, bg, jnp.zeros((B, 2, KD, VD), F32))
        of, ob, _ = _gla(qk, vg, alr, wg, bg, s_c)
        hy = _hyena(hyu, hy_conv_w[l], hy_conv_b[l], filt, hy_skip[l])
        k_all, v_all = _kvproj(jnp.concatenate([ckvr_c, ckvr], axis=1), mla_kv_norm_g[l], wk, wv, cos_all, sin_all)
        k_c, v_c = k_all[:, :Lc], v_all[:, :Lc]
        q_m = _qproj(cq, mla_q_norm_g[l], wq, cos, sin)
        om = _flash(q_m, k_all, v_all)

        x = _outproj(of, ob, vg, hy, om, x, m[2], gla_norm_g[l], hy_norm_g[l], mla_norm_g[l], w_out[l],
                     ln_g[l, 0], ln_b[l, 0], alpha)
        if need_ctx:
            hy_c = _hyena_ctx(hyu_c, hy_conv_w[l], hy_conv_b[l], filt, hy_skip[l])
            q_c = _qproj(cq_c, mla_q_norm_g[l], wq, cos_c, sin_c)
            om_c = _flash(q_c, k_c, v_c)
            xc = _outproj(of_c, ob_c, vg_c, hy_c, om_c, xc, mc[2], gla_norm_g[l], hy_norm_g[l], mla_norm_g[l],
                          w_out[l], ln_g[l, 0], ln_b[l, 0], alpha)

        i = l // 2
        if l % 2 == 0:
            x = _ffn(x, m[3], m[4], m[5], ffn_w1[i], ffn_w3[i], ffn_w2[i], ln_g[l, 1], ln_b[l, 1], alpha)
            if need_ctx:
                xc = _ffn(xc, mc[3], mc[4], mc[5], ffn_w1[i], ffn_w3[i], ffn_w2[i], ln_g[l, 1], ln_b[l, 1], alpha)
        else:
            x = _moe(x, m[3], m[4], m[5], moe_router[i], moe_w1[i], moe_w3[i], moe_w2[i], ln_g[l, 1], ln_b[l, 1], alpha)
            if need_ctx:
                xc = _moe(xc, mc[3], mc[4], mc[5], moe_router[i], moe_w1[i], moe_w3[i], moe_w2[i], ln_g[l, 1],
                          ln_b[l, 1], alpha)
    return x, xc


def kernel(x, c, ctx, c_ctx, w_mod, b_mod, w_in, gla_w_gate, gla_b_gate, gla_norm_g, hy_conv_w, hy_conv_b, hy_f_w1, hy_f_b1, hy_f_freq1, hy_f_w2, hy_f_b2, hy_f_freq2, hy_f_w3, hy_f_b3, hy_skip, hy_norm_g, mla_q_norm_g, mla_w_uq, mla_kv_norm_g, mla_w_ukv, mla_norm_g, w_out, ln_g, ln_b, ffn_w1, ffn_w3, ffn_w2, moe_router, moe_w1, moe_w3, moe_w2):
    return _streams(x, c, ctx, c_ctx, w_mod, b_mod, w_in, gla_w_gate, gla_b_gate, gla_norm_g, hy_conv_w, hy_conv_b, hy_f_w1, hy_f_b1, hy_f_freq1, hy_f_w2, hy_f_b2, hy_f_freq2, hy_f_w3, hy_f_b3, hy_skip, hy_norm_g, mla_q_norm_g, mla_w_uq, mla_kv_norm_g, mla_w_ukv, mla_norm_g, w_out, ln_g, ln_b, ffn_w1, ffn_w3, ffn_w2, moe_router, moe_w1, moe_w3, moe_w2)[0]
```

```python
import functools
import math

import numpy as np
import jax
import jax.numpy as jnp
from jax import lax
from jax.experimental import pallas as pl
from jax.experimental.pallas import tpu as pltpu

F32 = jnp.float32
BF16 = jnp.bfloat16
HI = lax.Precision.HIGHEST

GRID_W = 64
GLA_HEADS, GLA_DK, GLA_DV, GLA_RANK, GLA_TAU = 4, 32, 64, 16, 16.0
HY_CH, HY_EMB = 256, 33
HY_DECAY_TARGET, HY_FAST_DECAY, HY_SLOW_DECAY = 1e-2, 0.3, 1.5
MLA_HEADS, MLA_Q_RANK, MLA_KV_RANK, MLA_NOPE, MLA_ROPE, MLA_V = 8, 256, 128, 64, 32, 64
MLA_SCALE = (MLA_NOPE + MLA_ROPE) ** -0.5
ROPE_BASE = 10000.0
N_EXPERTS = 8
IN_SPLITS = (128, 128, 256, 256, 32, 768, 256, 128, 32)

LANES = 128
SUBLANES = 8
VMEM_LIMIT = 56 * 1024 * 1024

ROW_TILE = 512
WIDE_ROW_TILE = 1024
GLA_CHUNK = 128
DFT_N2 = 256


def _cp(*sem):
    return pltpu.CompilerParams(dimension_semantics=sem, vmem_limit_bytes=VMEM_LIMIT)


def _full(shape):
    n = len(shape)
    return pl.BlockSpec(shape, lambda *_: (0,) * n)


def _idiv(x, d):
    assert d & (d - 1) == 0
    return lax.shift_right_logical(x, int(math.log2(d)))


INPROJ_WIDTHS = (768, 256, 512, 128, 256, 256)


def _arrange_w_in(w):
    cuts = np.cumsum(IN_SPLITS)[:-1]
    qa, ka, va, ga, alr, hyu, cq, ckv, kr = jnp.split(w, [int(c) for c in cuts], axis=1)
    z96 = jnp.zeros((w.shape[0], 96), w.dtype)
    return jnp.concatenate([hyu, qa, ka, va, ga, alr, z96, cq, ckv, kr, z96], axis=1).astype(BF16)


def _inproj_body(x_ref, sh_ref, sc_ref, w_ref, *out_refs):
    h = x_ref[0] * (1.0 + sc_ref[0]) + sh_ref[0]
    acc = jnp.dot(h.astype(BF16), w_ref[...], preferred_element_type=F32)
    off = 0
    for r in out_refs:
        w = r.shape[-1]
        r[0] = acc[:, off:off + w]
        off += w


def _inproj(x, shift, scale, w_arr):
    B, L, D = x.shape
    tm = min(ROW_TILE, L)
    n = w_arr.shape[1]
    row = lambda w: pl.BlockSpec((1, tm, w), lambda b, i: (b, i, 0))
    vec = pl.BlockSpec((1, 1, D), lambda b, i: (b, 0, 0))
    return pl.pallas_call(
        _inproj_body,
        grid=(B, L // tm),
        in_specs=[row(D), vec, vec, _full((D, n))],
        out_specs=[row(w) for w in INPROJ_WIDTHS],
        out_shape=[jax.ShapeDtypeStruct((B, L, w), F32) for w in INPROJ_WIDTHS],
        compiler_params=_cp("parallel", "parallel"),
        name="inproj",
    )(x, shift, scale, w_arr)


def _log_sigmoid(z):
    return jnp.minimum(z, 0.0) - jnp.log1p(jnp.exp(-jnp.abs(z)))


def _gla_body(qkf_ref, vf_ref, af_ref, qkb_ref, vb_ref, ab_ref, wg_ref, bg_ref, s0_ref,
              of_ref, ob_ref, sout_ref, s_ref):
    i = pl.program_id(0)
    C = qkf_ref.shape[1]
    KD = GLA_HEADS * GLA_DK
    VD = GLA_HEADS * GLA_DV

    @pl.when(i == 0)
    def _():
        s_ref[...] = s0_ref[...]

    r = lax.broadcasted_iota(jnp.int32, (C, C), 0)
    c = lax.broadcasted_iota(jnp.int32, (C, C), 1)
    tris = ((c <= r).astype(F32), (c >= r).astype(F32))
    lane_k = _idiv(lax.broadcasted_iota(jnp.int32, (1, KD), 1), GLA_DK)
    lane_v = _idiv(lax.broadcasted_iota(jnp.int32, (1, VD), 1), GLA_DV)
    rk = _idiv(lax.broadcasted_iota(jnp.int32, (KD, VD), 0), GLA_DK)
    cv = _idiv(lax.broadcasted_iota(jnp.int32, (KD, VD), 1), GLA_DV)
    ones = jnp.ones((C, VD), F32)
    refs = ((qkf_ref, vf_ref, af_ref, of_ref), (qkb_ref, vb_ref, ab_ref, ob_ref))
    chains = [(b, d) for b in range(qkf_ref.shape[0]) for d in range(2)]

    z = [jnp.dot(refs[d][2][b], wg_ref[...], precision=HI, preferred_element_type=F32) + bg_ref[...] for b, d in chains]
    la = [_log_sigmoid(zz[:, d * KD:(d + 1) * KD]) / GLA_TAU for zz, (b, d) in zip(z, chains)]
    bb = [jnp.dot(tris[d], l_, precision=HI, preferred_element_type=F32) for l_, (b, d) in zip(la, chains)]
    tot_b = [lax.dot_general(l_, ones, (((0,), (0,)), ((), ())), precision=HI, preferred_element_type=F32) for l_ in la]
    qe, ke, kl, vb, s_old = [], [], [], [], []
    for n, (b, d) in enumerate(chains):
        qk = refs[d][0][b]
        q = qk[:, :KD] * (GLA_DK ** -0.5)
        k = qk[:, KD:]
        tot = jnp.sum(la[n], axis=0, keepdims=True)
        qe.append(q * jnp.exp(bb[n]))
        ke.append((k * jnp.exp(-bb[n])).astype(BF16))
        kl.append((k * jnp.exp(tot - bb[n])).astype(BF16))
        vb.append(refs[d][1][b].astype(BF16))
        s_old.append(s_ref[2 * b + d])
    o = [jnp.dot(qe[n].astype(BF16), s_old[n].astype(BF16), preferred_element_type=F32) for n in range(len(chains))]
    att = [[lax.dot_general(jnp.where(lane_k == h, qe[n], 0.0).astype(BF16), ke[n], (((1,), (1,)), ((), ())),
                            preferred_element_type=F32) for h in range(GLA_HEADS)] for n in range(len(chains))]
    kv = [lax.dot_general(kl[n], vb[n], (((0,), (0,)), ((), ())), preferred_element_type=F32) for n in range(len(chains))]
    for n, (b, d) in enumerate(chains):
        on = o[n]
        for h in range(GLA_HEADS):
            oh = jnp.dot((att[n][h] * tris[d]).astype(BF16), vb[n], preferred_element_type=F32)
            on = on + jnp.where(lane_v == h, oh, 0.0)
        refs[d][3][b] = on
        s_ref[2 * b + d] = jnp.exp(tot_b[n]) * s_old[n] + jnp.where(rk == cv, kv[n], 0.0)

    @pl.when(i == pl.num_programs(0) - 1)
    def _():
        sout_ref[...] = s_ref[...]


def _gla(qk, vg, alr, wg, bg, s0):
    B, L, _ = qk.shape
    C = min(GLA_CHUNK, L)
    n = L // C
    KD, VD = GLA_HEADS * GLA_DK, GLA_HEADS * GLA_DV
    fwd = lambda w: pl.BlockSpec((B, C, w), lambda i: (0, i, 0))
    bwd = lambda w: pl.BlockSpec((B, C, w), lambda i: (0, n - 1 - i, 0))
    st = _full((2 * B, KD, VD))
    of, ob, s_out = pl.pallas_call(
        _gla_body,
        grid=(n,),
        in_specs=[fwd(2 * KD), fwd(VD), fwd(LANES), bwd(2 * KD), bwd(VD), bwd(LANES),
                  _full((LANES, 2 * KD)), _full((1, 2 * KD)), st],
        out_specs=[fwd(VD), bwd(VD), st],
        out_shape=[jax.ShapeDtypeStruct((B, L, VD), F32), jax.ShapeDtypeStruct((B, L, VD), F32),
                   jax.ShapeDtypeStruct((2 * B, KD, VD), F32)],
        scratch_shapes=[pltpu.VMEM((2 * B, KD, VD), F32)],
        compiler_params=_cp("arbitrary"),
        name="gla",
    )(qk, vg, alr, qk, vg, alr, wg, bg, s0.reshape(2 * B, KD, VD))
    return of, ob, s_out.reshape(B, 2, KD, VD)


def _arrange_gate(w_gate, b_gate):
    KD = GLA_HEADS * GLA_DK
    wg = jnp.zeros((LANES, 2 * KD), F32)
    wg = wg.at[:GLA_RANK, :KD].set(w_gate[0]).at[GLA_RANK:2 * GLA_RANK, KD:].set(w_gate[1])
    return wg, jnp.concatenate([b_gate[0], b_gate[1]])[None, :]


def _shortconv_body(x_ref, p_ref, n_ref, w_ref, b_ref, v_ref, x1_ref, x2_ref):
    i = pl.program_id(1)
    last = pl.num_programs(1) - 1
    x = x_ref[0]
    tm = x.shape[0]
    prev = jnp.where(i > 0, p_ref[0][7:8, :], 0.0)
    nxt = jnp.where(i < last, n_ref[0][0:1, :], 0.0)
    rid = lax.broadcasted_iota(jnp.int32, x.shape, 0)
    dn = jnp.where(rid == 0, prev, pltpu.roll(x, 1, 0))
    up = jnp.where(rid == tm - 1, nxt, pltpu.roll(x, tm - 1, 0))
    w = w_ref[...]
    y = b_ref[...] + dn * w[0:1] + x * w[1:2] + up * w[2:3]
    v_ref[0] = y[:, :HY_CH]
    x1_ref[0] = y[:, HY_CH:2 * HY_CH]
    x2_ref[0] = y[:, 2 * HY_CH:]


def _shortconv(u, w, b):
    B, L, W = u.shape
    tm = min(ROW_TILE, L)
    nb = tm // 8
    row = pl.BlockSpec((1, tm, W), lambda b_, i: (b_, i, 0))
    prev = pl.BlockSpec((1, 8, W), lambda b_, i: (b_, jnp.maximum(i * nb - 1, 0), 0))
    nxt = pl.BlockSpec((1, 8, W), lambda b_, i: (b_, jnp.minimum((i + 1) * nb, L // 8 - 1), 0))
    o = pl.BlockSpec((1, tm, HY_CH), lambda b_, i: (b_, i, 0))
    return pl.pallas_call(
        _shortconv_body,
        grid=(B, L // tm),
        in_specs=[row, prev, nxt, _full((3, W)), _full((1, W))],
        out_specs=[o, o, o],
        out_shape=[jax.ShapeDtypeStruct((B, L, HY_CH), F32)] * 3,
        compiler_params=_cp("parallel", "parallel"),
        name="shortconv",
    )(u, u, u, w, b[None, :])


def _filter_feats(L):
    pos = jnp.arange(L, dtype=F32)
    t = pos / (L - 1)
    bands = (HY_EMB - 1) // 2
    freqs = jnp.linspace(1e-4, bands - 1, bands, dtype=F32)
    ang = (2.0 * math.pi * pos / L)[:, None] * freqs
    z = jnp.concatenate([t[:, None], jnp.cos(ang), -jnp.sin(ang)], axis=-1)
    z = jnp.pad(z, ((0, 0), (0, LANES - HY_EMB)))
    deltas = jnp.abs(jnp.linspace(math.log(HY_DECAY_TARGET) / HY_SLOW_DECAY,
                                  math.log(HY_DECAY_TARGET) / HY_FAST_DECAY, HY_CH, dtype=F32))
    return z, jnp.tile(deltas, 4)[None, :]


def _filter_body(z_ref, w1_ref, b1_ref, f1_ref, w2_ref, b2_ref, f2_ref, w3_ref, b3_ref, dl_ref,
                 h_ref, ss_ref, *, L):
    i = pl.program_id(0)
    z = z_ref[...]
    tm = z.shape[0]
    hid = jnp.sin(f1_ref[...] * (jnp.dot(z, w1_ref[...], precision=HI, preferred_element_type=F32) + b1_ref[...]))
    hid = jnp.sin(f2_ref[...] * (jnp.dot(hid, w2_ref[...], precision=HI, preferred_element_type=F32) + b2_ref[...]))
    h = jnp.dot(hid, w3_ref[...], precision=HI, preferred_element_type=F32) + b3_ref[...]
    pos = (lax.broadcasted_iota(jnp.int32, (tm, 1), 0) + i * tm).astype(F32)
    t = pos / (L - 1)
    h = h * jnp.exp(-t * dl_ref[...])

    @pl.when(i == 0)
    def _():
        ss_ref[...] = jnp.zeros_like(ss_ref)

    ss_ref[...] += jnp.sum(h * h, axis=0, keepdims=True)
    col = lax.broadcasted_iota(jnp.int32, h.shape, 1)
    is_bwd = (_idiv(col, HY_CH) & 1) == 1
    h_ref[...] = jnp.where(jnp.logical_and(is_bwd, pos == 0.0), 0.0, h)


def _filters(L, fw1, fb1, ff1, fw2, fb2, ff2, fw3, fb3):
    z, dl = _filter_feats(L)
    tm = min(WIDE_ROW_TILE, L)
    Hf = fw2.shape[0]
    w1 = jnp.pad(fw1, ((0, LANES - HY_EMB), (0, 0)))
    NC = fw3.shape[1]
    return pl.pallas_call(
        functools.partial(_filter_body, L=L),
        grid=(L // tm,),
        in_specs=[pl.BlockSpec((tm, LANES), lambda i: (i, 0)), _full((LANES, Hf)), _full((1, Hf)), _full((1, Hf)),
                  _full((Hf, Hf)), _full((1, Hf)), _full((1, Hf)), _full((Hf, NC)), _full((1, NC)), _full((1, NC))],
        out_specs=[pl.BlockSpec((tm, NC), lambda i: (i, 0)), _full((1, NC))],
        out_shape=[jax.ShapeDtypeStruct((L, NC), F32), jax.ShapeDtypeStruct((1, NC), F32)],
        compiler_params=_cp("arbitrary"),
        name="hy_filters",
    )(z, w1, fb1[None], ff1[None], fw2, fb2[None], ff2[None], fw3, fb3[None], dl)


def _dft_consts(L):
    N = 2 * L
    N2 = DFT_N2
    N1 = N // N2
    half = N1 // 2
    k1 = np.arange(N1)[:, None].astype(np.float64)
    n1 = np.arange(N1)[None, :].astype(np.float64)
    a1 = 2.0 * np.pi * k1 * n1 / N1
    f1r, f1i = np.cos(a1), -np.sin(a1)
    fa = np.concatenate([f1r[:, :half], f1i[:, :half]], axis=0)
    fb = np.concatenate([f1r[:half, :], f1i[:half, :]], axis=1) / N
    k2 = np.arange(N2)[:, None].astype(np.float64)
    n2 = np.arange(N2)[None, :].astype(np.float64)
    a2 = 2.0 * np.pi * k2 * n2 / N2
    f2r, f2i = np.cos(a2), -np.sin(a2)
    g = np.block([[f2r, -f2i], [f2i, f2r]])
    gc = np.block([[f2r, f2i], [-f2i, f2r]])
    at = 2.0 * np.pi * (np.arange(N1)[:, None] * np.arange(N2)[None, :] % N) / N
    twr, twi = np.cos(at), -np.sin(at)
    c = lambda a: jnp.asarray(a, dtype=F32)
    bc = lambda a: jnp.broadcast_to(c(a)[:, :, None], (N1, N2, LANES))
    eye = np.eye(SUBLANES)
    return dict(N1=N1, N2=N2, half=half, fa=c(np.kron(fa, eye)), fb=c(np.kron(fb, eye)), g=c(g), gc=c(gc),
                twr=bc(twr), twi=bc(twi))


def _lanes(t, width):
    return jnp.concatenate([t] * (width // LANES), axis=-1)


def _dft1_body(f_ref, x_ref, o_ref):
    x = x_ref[0]
    x2 = x.reshape(x.shape[0] * SUBLANES, x.shape[2]).astype(BF16)
    y = jnp.dot(f_ref[...], x2, preferred_element_type=F32)
    o_ref[0] = y.reshape(o_ref.shape[1], SUBLANES, y.shape[1])


def _dft_stage1(fa, x):
    B, half, N2, W = x.shape
    R = fa.shape[0] // SUBLANES
    return pl.pallas_call(
        _dft1_body,
        grid=(B, N2 // SUBLANES),
        in_specs=[_full(fa.shape), pl.BlockSpec((1, half, SUBLANES, W), lambda b, j: (b, 0, j, 0))],
        out_specs=pl.BlockSpec((1, R, SUBLANES, W), lambda b, j: (b, 0, j, 0)),
        out_shape=jax.ShapeDtypeStruct((B, R, N2, W), F32),
        compiler_params=_cp("parallel", "parallel"),
        name="hy_dft1",
    )(fa.astype(BF16), x)


def _filter_spec_body(a_ref, twr_ref, twi_ref, g_ref, ss_ref, hf_ref):
    W = a_ref.shape[-1]
    ar, ai = a_ref[0, 0], a_ref[1, 0]
    twr, twi = _lanes(twr_ref[0], W), _lanes(twi_ref[0], W)
    xr = ar * twr - ai * twi
    xi = ar * twi + ai * twr
    z = jnp.dot(g_ref[...], jnp.concatenate([xr, xi], axis=0).astype(BF16), preferred_element_type=F32)
    n2 = z.shape[0] // 2
    zr, zi = z[:n2], z[n2:]
    ss = ss_ref[...]
    for o in range(2):
        f0, b0 = (2 * o) * HY_CH, (2 * o + 1) * HY_CH
        sc = lax.rsqrt(ss[:, f0:f0 + HY_CH] + ss[:, b0:b0 + HY_CH] + 1e-6)
        hf_ref[o, 0, 0] = ((zr[:, f0:f0 + HY_CH] + zr[:, b0:b0 + HY_CH]) * sc).astype(hf_ref.dtype)
        hf_ref[o, 0, 1] = ((zi[:, f0:f0 + HY_CH] - zi[:, b0:b0 + HY_CH]) * sc).astype(hf_ref.dtype)


def _filter_spectrum(h, ss, dc):
    L, NC = h.shape
    N1, N2, half = dc["N1"], dc["N2"], dc["half"]
    a = _dft_stage1(dc["fa"], h.reshape(1, half, N2, NC))
    a = a.reshape(2, N1, N2, NC)
    return pl.pallas_call(
        _filter_spec_body,
        grid=(N1,),
        in_specs=[pl.BlockSpec((2, 1, N2, NC), lambda k: (0, k, 0, 0)),
                  pl.BlockSpec((1, N2, LANES), lambda k: (k, 0, 0)), pl.BlockSpec((1, N2, LANES), lambda k: (k, 0, 0)),
                  _full((2 * N2, 2 * N2)), _full((1, NC))],
        out_specs=pl.BlockSpec((2, 1, 2, N2, HY_CH), lambda k: (0, k, 0, 0, 0)),
        out_shape=jax.ShapeDtypeStruct((2, N1, 2, N2, HY_CH), BF16),
        compiler_params=_cp("parallel"),
        name="hy_filter_spec",
    )(a, dc["twr"], dc["twi"], dc["g"].astype(BF16), ss)


SPEC_K1 = 4


def _spec_mul_body(a_ref, twr_ref, twi_ref, g_ref, gc_ref, hf_ref, o_ref):
    W = a_ref.shape[-1]
    ks = range(a_ref.shape[2])
    n2 = g_ref.shape[0] // 2
    x = []
    for k in ks:
        ar, ai = a_ref[0, 0, k], a_ref[0, 1, k]
        twr, twi = _lanes(twr_ref[k], W), _lanes(twi_ref[k], W)
        x.append(jnp.concatenate([ar * twr - ai * twi, ar * twi + ai * twr], axis=0).astype(BF16))
    z = [jnp.dot(g_ref[...], x_, preferred_element_type=F32) for x_ in x]
    y = []
    for k, z_ in zip(ks, z):
        zr, zi = z_[:n2], z_[n2:]
        hr, hi = hf_ref[0, k, 0].astype(F32), hf_ref[0, k, 1].astype(F32)
        y.append(jnp.concatenate([zr * hr - zi * hi, zr * hi + zi * hr], axis=0).astype(BF16))
    b = [jnp.dot(gc_ref[...], y_, preferred_element_type=F32) for y_ in y]
    for k, b_ in zip(ks, b):
        br, bi = b_[:n2], b_[n2:]
        twr, twi = _lanes(twr_ref[k], W), _lanes(twi_ref[k], W)
        o_ref[0, 0, k] = br * twr + bi * twi
        o_ref[0, 1, k] = bi * twr - br * twi


def _spec_mul(a, hf, order, dc):
    B = a.shape[0]
    N1, N2 = dc["N1"], dc["N2"]
    C = a.shape[-1]
    kb = min(SPEC_K1, N1)
    blk = pl.BlockSpec((1, 2, kb, N2, C), lambda k, b: (b, 0, k, 0, 0))
    tw = pl.BlockSpec((kb, N2, LANES), lambda k, b: (k, 0, 0))
    return pl.pallas_call(
        _spec_mul_body,
        grid=(N1 // kb, B),
        in_specs=[blk, tw, tw, _full((2 * N2, 2 * N2)), _full((2 * N2, 2 * N2)),
                  pl.BlockSpec((1, kb, 2, N2, C), lambda k, b: (order, k, 0, 0, 0))],
        out_specs=blk,
        out_shape=jax.ShapeDtypeStruct(a.shape, F32),
        compiler_params=_cp("parallel", "parallel"),
        name="hy_spec_mul",
    )(a, dc["twr"], dc["twi"], dc["g"].astype(BF16), dc["gc"].astype(BF16), hf)


def _dft3_body(f_ref, b_ref, u_ref, gate_ref, skip_ref, o_ref):
    bm = b_ref[0]
    b2 = bm.reshape(bm.shape[0] * SUBLANES, bm.shape[2]).astype(BF16)
    y = jnp.dot(f_ref[...], b2, preferred_element_type=F32)
    rows, C = y.shape
    u = u_ref[0].reshape(rows, C)
    gate = gate_ref[0].reshape(rows, C)
    o_ref[0] = (gate * (y + u * skip_ref[...])).reshape(o_ref.shape[1], SUBLANES, C)


def _dft_stage3(fb, bm, u, gate, skip):
    B, R, N2, C = bm.shape
    half = u.shape[1]
    row = pl.BlockSpec((1, half, SUBLANES, C), lambda b, j: (b, 0, j, 0))
    return pl.pallas_call(
        _dft3_body,
        grid=(B, N2 // SUBLANES),
        in_specs=[_full(fb.shape), pl.BlockSpec((1, R, SUBLANES, C), lambda b, j: (b, 0, j, 0)), row, row, _full((1, C))],
        out_specs=row,
        out_shape=jax.ShapeDtypeStruct((B, half, N2, C), F32),
        compiler_params=_cp("parallel", "parallel"),
        name="hy_dft3",
    )(fb.astype(BF16), bm, u, gate, skip[None, :])


def _longconv_gated(u, gate, hf, order, skip, dc):
    B, L, C = u.shape
    N1, N2, half = dc["N1"], dc["N2"], dc["half"]
    u4 = u.reshape(B, half, N2, C)
    a = _dft_stage1(dc["fa"], u4).reshape(B, 2, N1, N2, C)
    bm = _spec_mul(a, hf, order, dc).reshape(B, 2 * N1, N2, C)
    return _dft_stage3(dc["fb"], bm, u4, gate.reshape(B, half, N2, C), skip).reshape(B, L, C)


def _hyena(hyu, conv_w, conv_b, filt, skip):
    B, L, _ = hyu.shape
    v, x1, x2 = _shortconv(hyu, conv_w, conv_b)
    h, ss = _filters(L, *filt)
    dc = _dft_consts(L)
    hf = _filter_spectrum(h, ss, dc)
    z1 = _longconv_gated(v, x1, hf, 0, skip[0], dc)
    return _longconv_gated(z1, x2, hf, 1, skip[1], dc)


def _hyena_ctx_body(v_ref, x1_ref, x2_ref, h_ref, ss_ref, skip_ref, fc_ref, gc_ref, o_ref):
    fc, gc = fc_ref[...], gc_ref[...]
    n = fc.shape[0] // 2
    ss = ss_ref[...]
    h = h_ref[...]

    def conv(u, o):
        f0, b0 = (2 * o) * HY_CH, (2 * o + 1) * HY_CH
        sc = lax.rsqrt(ss[:, f0:f0 + HY_CH] + ss[:, b0:b0 + HY_CH] + 1e-6)
        x = jnp.dot(fc, u, precision=HI, preferred_element_type=F32)
        hf = jnp.dot(fc, h[:, f0:f0 + HY_CH], precision=HI, preferred_element_type=F32)
        hb = jnp.dot(fc, h[:, b0:b0 + HY_CH], precision=HI, preferred_element_type=F32)
        hr = (hf[:n] + hb[:n]) * sc
        hi = (hf[n:] - hb[n:]) * sc
        yr = x[:n] * hr - x[n:] * hi
        yi = x[:n] * hi + x[n:] * hr
        y = jnp.dot(gc, jnp.concatenate([yr, yi], axis=0), precision=HI, preferred_element_type=F32)
        return y + u * skip_ref[o:o + 1, :]

    z1 = x1_ref[0] * conv(v_ref[0], 0)
    o_ref[0] = x2_ref[0] * conv(z1, 1)


def _hyena_ctx(hyu, conv_w, conv_b, filt, skip):
    B, L, _ = hyu.shape
    v, x1, x2 = _shortconv(hyu, conv_w, conv_b)
    h, ss = _filters(L, *filt)
    N = 2 * L
    ang = 2.0 * np.pi * (np.arange(N)[:, None] * np.arange(L)[None, :] % N) / N
    fr, fi = np.cos(ang), -np.sin(ang)
    fc = jnp.asarray(np.concatenate([fr, fi], axis=0), dtype=F32)
    gc = jnp.asarray(np.concatenate([fr.T, fi.T], axis=1) / N, dtype=F32)
    row = pl.BlockSpec((1, L, HY_CH), lambda b: (b, 0, 0))
    return pl.pallas_call(
        _hyena_ctx_body,
        grid=(B,),
        in_specs=[row, row, row, _full(h.shape), _full(ss.shape), _full(skip.shape), _full(fc.shape), _full(gc.shape)],
        out_specs=row,
        out_shape=jax.ShapeDtypeStruct((B, L, HY_CH), F32),
        compiler_params=_cp("parallel"),
        name="hyena_ctx",
    )(v, x1, x2, h, ss, skip, fc, gc)


HEAD_PAD = 128


def _rope_swap(w):
    a, b, c, d = w[..., 0:8], w[..., 8:16], w[..., 16:24], w[..., 24:32]
    return jnp.concatenate([-b, a, -d, c], axis=-1)


def _arrange_wq(w_uq):
    R = w_uq.shape[0]
    w = w_uq.reshape(R, MLA_HEADS, MLA_NOPE + MLA_ROPE)
    rope = w[..., MLA_NOPE:]
    out = jnp.concatenate([w[..., :MLA_NOPE], rope, _rope_swap(rope)], axis=-1)
    return out.reshape(R, MLA_HEADS * HEAD_PAD).astype(BF16)


def _arrange_wkv(w_ukv):
    R = w_ukv.shape[0]
    w = w_ukv.reshape(R, MLA_HEADS, MLA_NOPE + MLA_V)
    wk = jnp.concatenate([w[..., :MLA_NOPE], jnp.zeros((R, MLA_HEADS, HEAD_PAD - MLA_NOPE), w.dtype)], axis=-1)
    wv = w[..., MLA_NOPE:]
    return wk.reshape(R, MLA_HEADS * HEAD_PAD).astype(BF16), wv.reshape(R, MLA_HEADS * MLA_V).astype(BF16)


def _kr_place():
    e = np.zeros((LANES, MLA_HEADS * HEAD_PAD), np.float32)
    es = np.zeros((LANES, MLA_HEADS * HEAD_PAD), np.float32)
    for h in range(MLA_HEADS):
        base = h * HEAD_PAD + MLA_NOPE
        for j in range(MLA_ROPE):
            e[j, base + j] = 1.0
            blk, r = divmod(j, 16)
            if r < 8:
                es[16 * blk + r + 8, base + j] = -1.0
            else:
                es[16 * blk + r - 8, base + j] = 1.0
    return jnp.asarray(e).astype(BF16), jnp.asarray(es).astype(BF16)


def _rope_tables(L, rope):
    if rope:
        t = np.arange(L)
        row, col = (t // GRID_W).astype(np.float32), (t % GRID_W).astype(np.float32)
        half = MLA_ROPE // 2
        inv = ROPE_BASE ** (-jnp.arange(0, half, 2, dtype=F32) / half)
        ar = jnp.asarray(row)[:, None] * inv
        ac = jnp.asarray(col)[:, None] * inv
        cos = jnp.concatenate([jnp.cos(ar), jnp.cos(ar), jnp.cos(ac), jnp.cos(ac)], axis=-1)
        sin = jnp.concatenate([jnp.sin(ar), jnp.sin(ar), jnp.sin(ac), jnp.sin(ac)], axis=-1)
    else:
        cos, sin = jnp.ones((L, MLA_ROPE), F32), jnp.zeros((L, MLA_ROPE), F32)
    return cos, sin


def _rms_rows(x, g, eps=1e-6):
    return x * lax.rsqrt(jnp.mean(x * x, axis=-1, keepdims=True) + eps) * g


def _qproj_body(cq_ref, g_ref, w_ref, t1_ref, t2_ref, q_ref):
    xn = _rms_rows(cq_ref[0], g_ref[...])
    acc = jnp.dot(xn.astype(BF16), w_ref[...], preferred_element_type=F32)
    W = acc.shape[1]
    t1, t2 = _lanes(t1_ref[...], W), _lanes(t2_ref[...], W)
    q_ref[0] = (acc * t1 + pltpu.roll(acc, W - MLA_ROPE, 1) * t2).astype(q_ref.dtype)


def _qproj(cq, g, wq, cos, sin):
    B, L, R = cq.shape
    tm = min(ROW_TILE, L)
    W = wq.shape[1]
    ones, zeros = jnp.ones((L, MLA_NOPE), F32), jnp.zeros((L, MLA_ROPE), F32)
    qs = MLA_SCALE * math.log2(math.e)
    t1 = jnp.concatenate([ones, cos, zeros], axis=-1) * qs
    t2 = jnp.concatenate([jnp.zeros((L, MLA_NOPE), F32), sin, zeros], axis=-1) * qs
    tab = pl.BlockSpec((tm, HEAD_PAD), lambda b, i: (i, 0))
    return pl.pallas_call(
        _qproj_body,
        grid=(B, L // tm),
        in_specs=[pl.BlockSpec((1, tm, R), lambda b, i: (b, i, 0)), _full((1, R)), _full((R, W)), tab, tab],
        out_specs=pl.BlockSpec((1, tm, W), lambda b, i: (b, i, 0)),
        out_shape=jax.ShapeDtypeStruct((B, L, W), BF16),
        compiler_params=_cp("parallel", "parallel"),
        name="mla_qproj",
    )(cq, g[None, :], wq, t1, t2)


def _kvproj_body(c_ref, g_ref, wk_ref, wv_ref, e_ref, es_ref, cos_ref, sin_ref, k_ref, v_ref):
    c = c_ref[0]
    R = MLA_KV_RANK
    xn = _rms_rows(c[:, :R], g_ref[...]).astype(BF16)
    kr = c[:, R:]
    acc = jnp.dot(xn, wk_ref[...], preferred_element_type=F32)
    acc += jnp.dot((kr * cos_ref[...]).astype(BF16), e_ref[...], preferred_element_type=F32)
    acc += jnp.dot((kr * sin_ref[...]).astype(BF16), es_ref[...], preferred_element_type=F32)
    k_ref[0] = acc.astype(k_ref.dtype)
    v_ref[0] = jnp.dot(xn, wv_ref[...], preferred_element_type=F32).astype(v_ref.dtype)


def _kvproj(ckvr, g, wk, wv, cos, sin):
    B, L, Wc = ckvr.shape
    tm = next(t for t in (1280, 512, 256, L) if L % t == 0)
    pad = jnp.zeros((L, LANES - MLA_ROPE), F32)
    cos_p, sin_p = jnp.concatenate([cos, pad], axis=-1), jnp.concatenate([sin, pad], axis=-1)
    e, es = _kr_place()
    tab = pl.BlockSpec((tm, LANES), lambda b, i: (i, 0))
    Wk, Wv = wk.shape[1], wv.shape[1]
    return pl.pallas_call(
        _kvproj_body,
        grid=(B, L // tm),
        in_specs=[pl.BlockSpec((1, tm, Wc), lambda b, i: (b, i, 0)), _full((1, MLA_KV_RANK)),
                  _full(wk.shape), _full(wv.shape), _full(e.shape), _full(es.shape), tab, tab],
        out_specs=[pl.BlockSpec((1, tm, Wk), lambda b, i: (b, i, 0)), pl.BlockSpec((1, tm, Wv), lambda b, i: (b, i, 0))],
        out_shape=[jax.ShapeDtypeStruct((B, L, Wk), BF16), jax.ShapeDtypeStruct((B, L, Wv), BF16)],
        compiler_params=_cp("parallel", "parallel"),
        name="mla_kvproj",
    )(ckvr, g[None, :], wk, wv, e, es, cos_p, sin_p)


FLASH_ROWS = 256
FLASH_KEYS = 256


def _flash_body(q_ref, k_ref, v_ref, o_ref, m_ref, l_ref, acc_ref, s_ref, *, R):
    j = pl.program_id(3)
    tq, tk = q_ref.shape[1], k_ref.shape[1]
    CK = FLASH_KEYS
    npc = CK // LANES

    @pl.when(j == 0)
    def _():
        m_ref[...] = jnp.full_like(m_ref, -jnp.inf)
        l_ref[...] = jnp.zeros_like(l_ref)
        acc_ref[...] = jnp.zeros_like(acc_ref)

    def pass1(a, r):
        lo, r0 = a * HEAD_PAD, r * R
        q = q_ref[0, r0:r0 + R, lo:lo + HEAD_PAD]
        mp = None
        for c in range(tk // CK):
            kc = k_ref[0, c * CK:(c + 1) * CK, lo:lo + HEAD_PAD]
            s = lax.dot_general(q, kc, (((1,), (1,)), ((), ())), preferred_element_type=F32)
            s_ref[r0:r0 + R, c * CK:(c + 1) * CK] = s
            for w in range(npc):
                pc = s[:, w * LANES:(w + 1) * LANES]
                mp = pc if mp is None else jnp.maximum(mp, pc)
        m_old = m_ref[a, r0:r0 + R, :]
        return m_old, jnp.maximum(m_old, jnp.max(mp, axis=1, keepdims=True))

    def pass2(a, r, m_old, m_new):
        r0 = r * R
        alpha = jnp.exp2(m_old - m_new)
        lp = jnp.zeros((R, LANES), F32)
        pv = jnp.zeros((R, 2 * MLA_V), F32)
        for c in range(tk // CK):
            s = s_ref[r0:r0 + R, c * CK:(c + 1) * CK]
            ps = [jnp.exp2(s[:, w * LANES:(w + 1) * LANES] - m_new) for w in range(npc)]
            for p_ in ps:
                lp = lp + p_
            p = jnp.concatenate(ps, axis=1).astype(BF16)
            pv = pv + jnp.dot(p, v_ref[0, c * CK:(c + 1) * CK, :], preferred_element_type=F32)
        l_ref[a, r0:r0 + R, :] = alpha * l_ref[a, r0:r0 + R, :] + jnp.sum(lp, axis=1, keepdims=True)
        acc_ref[a, r0:r0 + R, :] = alpha * acc_ref[a, r0:r0 + R, :] + pv
        m_ref[a, r0:r0 + R, :] = m_new

    assert tq // R >= 2
    blocks = [(a, r) for a in range(2) for r in range(tq // R)]
    pend = pass1(*blocks[0])
    for i, blk in enumerate(blocks):
        nxt = pass1(*blocks[i + 1]) if i + 1 < len(blocks) else None
        pass2(*blk, *pend)
        pend = nxt

    @pl.when(j == pl.num_programs(3) - 1)
    def _():
        lane = lax.broadcasted_iota(jnp.int32, acc_ref.shape[1:], 1)
        o_ref[0] = jnp.where(lane < MLA_V, acc_ref[0] / l_ref[0], acc_ref[1] / l_ref[1])


def _flash_tiles(Lq, Lk):
    tq = min(1024, Lq)
    tk = next(t for t in (3328, 1280, 1024, 512, 256, Lk) if Lk % t == 0)
    return tq, tk


def _flash(q, k, v):
    B, Lq, _ = q.shape
    Lk = k.shape[1]
    tq, tk = _flash_tiles(Lq, Lk)
    hp = MLA_HEADS // 2
    return pl.pallas_call(
        functools.partial(_flash_body, R=min(FLASH_ROWS, tq // 2)),
        grid=(B, hp, Lq // tq, Lk // tk),
        in_specs=[pl.BlockSpec((1, tq, 2 * HEAD_PAD), lambda b, h, i, j: (b, i, h)),
                  pl.BlockSpec((1, tk, 2 * HEAD_PAD), lambda b, h, i, j: (b, j, h)),
                  pl.BlockSpec((1, tk, 2 * MLA_V), lambda b, h, i, j: (b, j, h))],
        out_specs=pl.BlockSpec((1, tq, 2 * MLA_V), lambda b, h, i, j: (b, i, h)),
        out_shape=jax.ShapeDtypeStruct((B, Lq, MLA_HEADS * MLA_V), F32),
        scratch_shapes=[pltpu.VMEM((2, tq, LANES), F32), pltpu.VMEM((2, tq, LANES), F32),
                        pltpu.VMEM((2, tq, 2 * MLA_V), F32), pltpu.VMEM((tq, tk), F32)],
        compiler_params=_cp("parallel", "parallel", "parallel", "arbitrary"),
        name="mla_flash",
    )(q, k, v)


def _layernorm_rows(x, g, b, eps=1e-5):
    mu = jnp.mean(x, axis=-1, keepdims=True)
    xc = x - mu
    var = jnp.mean(xc * xc, axis=-1, keepdims=True)
    return xc * lax.rsqrt(var + eps) * g + b


def _outproj_body(of_ref, ob_ref, g_ref, hy_ref, om_ref, x_ref, gate_ref, gg_ref, hg_ref, mg_ref,
                  w_ref, lg_ref, lb_ref, o_ref, *, alpha):
    VD = GLA_HEADS * GLA_DV
    o = of_ref[0] + ob_ref[0]
    r = _idiv(lax.broadcasted_iota(jnp.int32, (VD, VD), 0), GLA_DV)
    c = _idiv(lax.broadcasted_iota(jnp.int32, (VD, VD), 1), GLA_DV)
    grp = (r == c).astype(F32)
    ms = jnp.dot(o * o, grp, precision=HI, preferred_element_type=F32) * (1.0 / GLA_DV)
    g = g_ref[0]
    ya = o * lax.rsqrt(ms + 1e-6) * gg_ref[...] * (g * jax.nn.sigmoid(g))
    yb = _rms_rows(hy_ref[0], hg_ref[...])
    yc = _rms_rows(om_ref[0], mg_ref[...])
    acc = jnp.dot(ya.astype(BF16), w_ref[0:VD, :], preferred_element_type=F32)
    acc += jnp.dot(yb.astype(BF16), w_ref[VD:VD + HY_CH, :], preferred_element_type=F32)
    acc += jnp.dot(yc.astype(BF16), w_ref[VD + HY_CH:, :], preferred_element_type=F32)
    o_ref[0] = _layernorm_rows(alpha * x_ref[0] + gate_ref[0] * acc, lg_ref[...], lb_ref[...])


def _outproj(of, ob, vg, hy, om, x, gate, gla_g, hy_g, mla_g, w_out, ln_g, ln_b, alpha):
    B, L, D = x.shape
    tm = min(ROW_TILE, L)
    VD = GLA_HEADS * GLA_DV
    MD = MLA_HEADS * MLA_V
    row = lambda w: pl.BlockSpec((1, tm, w), lambda b, i: (b, i, 0))
    return pl.pallas_call(
        functools.partial(_outproj_body, alpha=alpha),
        grid=(B, L // tm),
        in_specs=[row(VD), row(VD), pl.BlockSpec((1, tm, VD), lambda b, i: (b, i, 1)), row(HY_CH), row(MD), row(D),
                  pl.BlockSpec((1, 1, D), lambda b, i: (b, 0, 0)), _full((1, VD)), _full((1, HY_CH)), _full((1, MD)),
                  _full(w_out.shape), _full((1, D)), _full((1, D))],
        out_specs=row(D),
        out_shape=jax.ShapeDtypeStruct((B, L, D), F32),
        compiler_params=_cp("parallel", "parallel"),
        name="outproj",
    )(of, ob, vg, hy, om, x, gate, jnp.tile(gla_g, GLA_HEADS)[None, :], hy_g[None, :], mla_g[None, :],
      w_out.astype(BF16), ln_g[None, :], ln_b[None, :])


def _ffn_body(x_ref, sh_ref, sc_ref, gate_ref, w1_ref, w3_ref, w2_ref, lg_ref, lb_ref, o_ref, h_ref, acc_ref, *, alpha):
    j = pl.program_id(2)

    @pl.when(j == 0)
    def _():
        h_ref[...] = (x_ref[0] * (1.0 + sc_ref[0]) + sh_ref[0]).astype(BF16)
        acc_ref[...] = jnp.zeros_like(acc_ref)

    h = h_ref[...]
    a = jnp.dot(h, w1_ref[...], preferred_element_type=F32)
    b = jnp.dot(h, w3_ref[...], preferred_element_type=F32)
    t = (a * jax.nn.sigmoid(a) * b).astype(BF16)
    acc_ref[...] += jnp.dot(t, w2_ref[...], preferred_element_type=F32)

    @pl.when(j == pl.num_programs(2) - 1)
    def _():
        o_ref[0] = _layernorm_rows(alpha * x_ref[0] + gate_ref[0] * acc_ref[...], lg_ref[...], lb_ref[...])


def _ffn_tile(F):
    for cand in (512, 256, 128):
        if F % cand == 0:
            return cand
    return F


def _ffn(x, shift, scale, gate, w1, w3, w2, ln_g, ln_b, alpha):
    B, L, D = x.shape
    F = w1.shape[1]
    tm = min(WIDE_ROW_TILE, L)
    tf = _ffn_tile(F)
    row = pl.BlockSpec((1, tm, D), lambda b, i, j: (b, i, 0))
    vec = pl.BlockSpec((1, 1, D), lambda b, i, j: (b, 0, 0))
    return pl.pallas_call(
        functools.partial(_ffn_body, alpha=alpha),
        grid=(B, L // tm, F // tf),
        in_specs=[row, vec, vec, vec,
                  pl.BlockSpec((D, tf), lambda b, i, j: (0, j)), pl.BlockSpec((D, tf), lambda b, i, j: (0, j)),
                  pl.BlockSpec((tf, D), lambda b, i, j: (j, 0)), _full((1, D)), _full((1, D))],
        out_specs=row,
        out_shape=jax.ShapeDtypeStruct((B, L, D), F32),
        scratch_shapes=[pltpu.VMEM((tm, D), BF16), pltpu.VMEM((tm, D), F32)],
        compiler_params=_cp("parallel", "parallel", "arbitrary"),
        name="ffn",
    )(x, shift, scale, gate, w1.astype(BF16), w3.astype(BF16), w2.astype(BF16), ln_g[None, :], ln_b[None, :])


MOE_TOKENS = 2048
MOE_ROWS = 256
RANK_CHUNK = 256


def _router_body(x_ref, sh_ref, sc_ref, wr_ref, h_ref, g_ref, rk_ref, rkt_ref, cnt_ref):
    h = x_ref[0] * (1.0 + sc_ref[0]) + sh_ref[0]
    h_ref[0] = h.astype(BF16)
    logits = jnp.dot(h, wr_ref[...], precision=HI, preferred_element_type=F32)
    lane = lax.broadcasted_iota(jnp.int32, logits.shape, 1).astype(F32)
    logits = jnp.where(lane < N_EXPERTS, logits, -jnp.inf)
    m1 = jnp.max(logits, axis=1, keepdims=True)
    i1 = jnp.min(jnp.where(logits == m1, lane, float(LANES)), axis=1, keepdims=True)
    rest = jnp.where(lane == i1, -jnp.inf, logits)
    m2 = jnp.max(rest, axis=1, keepdims=True)
    i2 = jnp.min(jnp.where(rest == m2, lane, float(LANES)), axis=1, keepdims=True)
    e2 = jnp.exp(m2 - m1)
    w1 = 1.0 / (1.0 + e2)
    w2 = e2 / (1.0 + e2)
    g_ref[0] = jnp.where(lane == i1, w1, 0.0) + jnp.where(lane == i2, w2, 0.0)
    sel = jnp.logical_or(lane == i1, lane == i2)
    self_ = sel.astype(F32)
    tm = h.shape[0]
    C = min(RANK_CHUNK, tm)
    r = lax.broadcasted_iota(jnp.int32, (C, C), 0)
    c = lax.broadcasted_iota(jnp.int32, (C, C), 1)
    tri = (c < r).astype(BF16)
    carry = jnp.zeros((1, LANES), F32)
    parts = []
    for k in range(tm // C):
        sk = self_[k * C:(k + 1) * C]
        parts.append(jnp.dot(tri, sk.astype(BF16), preferred_element_type=F32) + carry)
        carry = carry + jnp.sum(sk, axis=0, keepdims=True)
    rank = jnp.where(sel, jnp.concatenate(parts, axis=0), -1.0)
    rk_ref[0] = rank
    rkt_ref[0] = rank.T[:8]
    cnt_ref[0, 0] = carry


def _router(x, shift, scale, w_router):
    B, L, D = x.shape
    tm = min(MOE_TOKENS, L)
    nt = L // tm
    wr = jnp.pad(w_router, ((0, 0), (0, LANES - N_EXPERTS)))
    vec = pl.BlockSpec((1, 1, D), lambda b, i: (b, 0, 0))
    col = pl.BlockSpec((1, tm, LANES), lambda b, i: (b, i, 0))
    return pl.pallas_call(
        _router_body,
        grid=(B, nt),
        in_specs=[pl.BlockSpec((1, tm, D), lambda b, i: (b, i, 0)), vec, vec, _full((D, LANES))],
        out_specs=[pl.BlockSpec((1, tm, D), lambda b, i: (b, i, 0)), col, col,
                   pl.BlockSpec((1, 8, tm), lambda b, i: (b, 0, i)), pl.BlockSpec((1, 1, 1, LANES), lambda b, i: (b, i, 0, 0))],
        out_shape=[jax.ShapeDtypeStruct((B, L, D), BF16), jax.ShapeDtypeStruct((B, L, LANES), F32),
                   jax.ShapeDtypeStruct((B, L, LANES), F32), jax.ShapeDtypeStruct((B, 8, L), F32),
                   jax.ShapeDtypeStruct((B, nt, 1, LANES), F32)],
        compiler_params=_cp("parallel", "parallel"),
        name="moe_router",
    )(x, shift, scale, wr)


def _moe_body(cnt_ref, h_ref, g_ref, rk_ref, rkt_ref, w1_ref, w3_ref, w2_ref, o_ref, xg_ref, y_ref, *, M, P):
    b, i, e, j = pl.program_id(0), pl.program_id(1), pl.program_id(2), pl.program_id(3)
    nt, ne, nj = pl.num_programs(1), pl.num_programs(2), pl.num_programs(3)
    tm = h_ref.shape[1]
    cnt = cnt_ref[(b * nt + i) * ne + e]
    n_ch = lax.div(cnt + (M - 1), M)

    @pl.when(jnp.logical_and(e == 0, j == 0))
    def _():
        o_ref[...] = jnp.zeros_like(o_ref)

    @pl.when(j == 0)
    def _():
        rkt = rkt_ref[0, pl.ds(e, 1), :]

        def gather(c, carry):
            r0 = pl.multiple_of(c * M, 16)
            rows = (lax.broadcasted_iota(jnp.int32, (M, 1), 0) + c * M).astype(F32)
            onehot = (rkt == rows).astype(BF16)
            xg_ref[pl.ds(r0, M), :] = jnp.dot(onehot, h_ref[0], preferred_element_type=F32).astype(BF16)
            return carry

        lax.fori_loop(0, n_ch, gather, 0)

    def expert(chunks):
        r0 = [pl.multiple_of(c * M, 16) for c in chunks]
        xg = [xg_ref[pl.ds(r, M), :] for r in r0]
        a = [jnp.dot(x_, w1_ref[0], preferred_element_type=F32) for x_ in xg]
        g = [jnp.dot(x_, w3_ref[0], preferred_element_type=F32) for x_ in xg]
        t = [(a_ * jax.nn.sigmoid(a_) * g_).astype(BF16) for a_, g_ in zip(a, g)]
        yv = [jnp.dot(t_, w2_ref[0], preferred_element_type=F32) for t_ in t]

        @pl.when(j == 0)
        def _():
            for r, y_ in zip(r0, yv):
                y_ref[pl.ds(r, M), :] = y_

        @pl.when(j > 0)
        def _():
            for r, y_ in zip(r0, yv):
                y_ref[pl.ds(r, M), :] += y_

    def expert_pair(c2, carry):
        expert([2 * c2, 2 * c2 + 1])
        return carry

    lax.fori_loop(0, lax.div(n_ch, 2), expert_pair, 0)

    @pl.when(lax.rem(n_ch, 2) == 1)
    def _():
        expert([n_ch - 1])

    @pl.when(j == nj - 1)
    def _():
        for p in range(tm // P):
            lane = lax.broadcasted_iota(jnp.int32, (P, LANES), 1)
            rke = jnp.sum(jnp.where(lane == e, rk_ref[0, p * P:(p + 1) * P, :], 0.0), axis=1, keepdims=True)
            ge = jnp.sum(jnp.where(lane == e, g_ref[0, p * P:(p + 1) * P, :], 0.0), axis=1, keepdims=True)

            def scatter(c, carry):
                r0 = pl.multiple_of(c * M, 16)
                cols = (lax.broadcasted_iota(jnp.int32, (1, M), 1) + c * M).astype(F32)
                onehot = (rke == cols).astype(BF16)
                yb = y_ref[pl.ds(r0, M), :].astype(BF16)
                o_ref[0, p * P:(p + 1) * P, :] += ge * jnp.dot(onehot, yb, preferred_element_type=F32)
                return carry

            lax.fori_loop(0, n_ch, scatter, 0)


def _res_ln_body(x_ref, y_ref, gate_ref, lg_ref, lb_ref, o_ref, *, alpha):
    o_ref[0] = _layernorm_rows(alpha * x_ref[0] + gate_ref[0] * y_ref[0], lg_ref[...], lb_ref[...])


def _res_ln(x, y, gate, ln_g, ln_b, alpha):
    B, L, D = x.shape
    tm = min(WIDE_ROW_TILE, L)
    row = pl.BlockSpec((1, tm, D), lambda b, i: (b, i, 0))
    return pl.pallas_call(
        functools.partial(_res_ln_body, alpha=alpha),
        grid=(B, L // tm),
        in_specs=[row, row, pl.BlockSpec((1, 1, D), lambda b, i: (b, 0, 0)), _full((1, D)), _full((1, D))],
        out_specs=row,
        out_shape=jax.ShapeDtypeStruct((B, L, D), F32),
        compiler_params=_cp("parallel", "parallel"),
        name="res_ln",
    )(x, y, gate, ln_g[None, :], ln_b[None, :])


def _moe(x, shift, scale, gate, w_router, w1, w3, w2, ln_g, ln_b, alpha):
    B, L, D = x.shape
    E, _, F = w1.shape
    hb, gts, rk, rkt, cnt = _router(x, shift, scale, w_router)
    tm = min(MOE_TOKENS, L)
    nt = L // tm
    M = MOE_ROWS
    rows_max = -(-tm // M) * M
    tf = next(t for t in (896, 512, 256, 128, F) if F % t == 0)
    counts = cnt[:, :, 0, :E].astype(jnp.int32).reshape(-1)
    row = lambda w: pl.BlockSpec((1, tm, w), lambda b, i, e, j, c: (b, i, 0))
    y = pl.pallas_call(
        functools.partial(_moe_body, M=M, P=min(512, tm)),
        grid_spec=pltpu.PrefetchScalarGridSpec(
            num_scalar_prefetch=1,
            grid=(B, nt, E, F // tf),
            in_specs=[row(D), row(LANES), row(LANES), pl.BlockSpec((1, 8, tm), lambda b, i, e, j, c: (b, 0, i)),
                      pl.BlockSpec((1, D, tf), lambda b, i, e, j, c: (e, 0, j)),
                      pl.BlockSpec((1, D, tf), lambda b, i, e, j, c: (e, 0, j)),
                      pl.BlockSpec((1, tf, D), lambda b, i, e, j, c: (e, j, 0))],
            out_specs=row(D),
            scratch_shapes=[pltpu.VMEM((rows_max, D), BF16), pltpu.VMEM((rows_max, D), F32)],
        ),
        out_shape=jax.ShapeDtypeStruct((B, L, D), F32),
        compiler_params=_cp("parallel", "parallel", "arbitrary", "arbitrary"),
        name="moe",
    )(counts, hb, gts, rk, rkt, w1.astype(BF16), w3.astype(BF16), w2.astype(BF16))
    return _res_ln(x, y, gate, ln_g, ln_b, alpha)


def _mod_body(c_ref, w_ref, b_ref, o_ref):
    c = c_ref[...]
    s = c * jax.nn.sigmoid(c)
    o_ref[...] = jnp.dot(s, w_ref[...], precision=HI, preferred_element_type=F32) + b_ref[...]


def _modulation(cc, w_mod, b_mod):
    R, D = cc.shape
    N = w_mod.shape[1]
    tn = 1024
    return pl.pallas_call(
        _mod_body,
        grid=(N // tn,),
        in_specs=[_full((R, D)), pl.BlockSpec((D, tn), lambda j: (0, j)), pl.BlockSpec((1, tn), lambda j: (0, j))],
        out_specs=pl.BlockSpec((R, tn), lambda j: (0, j)),
        out_shape=jax.ShapeDtypeStruct((R, N), F32),
        compiler_params=_cp("parallel"),
        name="modulation",
    )(cc, w_mod, b_mod[None, :])


def _streams(x, c, ctx, c_ctx, w_mod, b_mod, w_in, gla_w_gate, gla_b_gate, gla_norm_g, hy_conv_w, hy_conv_b, hy_f_w1, hy_f_b1, hy_f_freq1, hy_f_w2, hy_f_b2, hy_f_freq2, hy_f_w3, hy_f_b3, hy_skip, hy_norm_g, mla_q_norm_g, mla_w_uq, mla_kv_norm_g, mla_w_ukv, mla_norm_g, w_out, ln_g, ln_b, ffn_w1, ffn_w3, ffn_w2, moe_router, moe_w1, moe_w3, moe_w2):
    B, L, D = x.shape
    Lc = ctx.shape[1]
    depth = w_mod.shape[0]
    alpha = (2.0 * depth) ** 0.25
    cc = jnp.zeros((8, D), F32).at[:B].set(c).at[B].set(c_ctx)
    cos, sin = _rope_tables(L, True)
    cos_c, sin_c = _rope_tables(Lc, False)
    cos_all, sin_all = jnp.concatenate([cos_c, cos], axis=0), jnp.concatenate([sin_c, sin], axis=0)
    KD, VD = GLA_HEADS * GLA_DK, GLA_HEADS * GLA_DV
    xc = ctx
    for l in range(depth):
        need_ctx = l < depth - 1
        mods = _modulation(cc, w_mod[l], b_mod[l])
        m = [mods[:B, k * D:(k + 1) * D][:, None, :] for k in range(6)]
        mc = [jnp.broadcast_to(mods[B, k * D:(k + 1) * D][None, None, :], (B, 1, D)) for k in range(6)]
        w_arr = _arrange_w_in(w_in[l])
        wg, bg = _arrange_gate(gla_w_gate[l], gla_b_gate[l])
        filt = (hy_f_w1[l], hy_f_b1[l], hy_f_freq1[l], hy_f_w2[l], hy_f_b2[l], hy_f_freq2[l], hy_f_w3[l], hy_f_b3[l])
        wq = _arrange_wq(mla_w_uq[l])
        wk, wv = _arrange_wkv(mla_w_ukv[l])

        hyu, qk, vg, alr, cq, ckvr = _inproj(x, m[0], m[1], w_arr)
        hyu_c, qk_c, vg_c, alr_c, cq_c, ckvr_c = _inproj(xc, mc[0], mc[1], w_arr)

        of_c, ob_c, s_c = _gla(qk_c, vg_c, alr_c, wg, bg, jnp.zeros((B, 2, KD, VD), F32))
        of, ob, _ = _gla(qk, vg, alr, wg, bg, s_c)
        hy = _hyena(hyu, hy_conv_w[l], hy_conv_b[l], filt, hy_skip[l])
        k_all, v_all = _kvproj(jnp.concatenate([ckvr_c, ckvr], axis=1), mla_kv_norm_g[l], wk, wv, cos_all, sin_all)
        k_c, v_c = k_all[:, :Lc], v_all[:, :Lc]
        q_m = _qproj(cq, mla_q_norm_g[l], wq, cos, sin)
        om = _flash(q_m, k_all, v_all)

        x = _outproj(of, ob, vg, hy, om, x, m[2], gla_norm_g[l], hy_norm_g[l], mla_norm_g[l], w_out[l],
                     ln_g[l, 0], ln_b[l, 0], alpha)
        if need_ctx:
            hy_c = _hyena_ctx(hyu_c, hy_conv_w[l], hy_conv_b[l], filt, hy_skip[l])
            q_c = _qproj(cq_c, mla_q_norm_g[l], wq, cos_c, sin_c)
            om_c = _flash(q_c, k_c, v_c)
            xc = _outproj(of_c, ob_c, vg_c, hy_c, om_c, xc, mc[2], gla_norm_g[l], hy_norm_g[l], mla_norm_g[l],
                          w_out[l], ln_g[l, 0], ln_b[l, 0], alpha)

        i = l // 2
        if l % 2 == 0:
            x = _ffn(x, m[3], m[4], m[5], ffn_w1[i], ffn_w3[i], ffn_w2[i], ln_g[l, 1], ln_b[l, 1], alpha)
            if need_ctx:
                xc = _ffn(xc, mc[3], mc[4], mc[5], ffn_w1[i], ffn_w3[i], ffn_w2[i], ln_g[l, 1], ln_b[l, 1], alpha)
        else:
            x = _moe(x, m[3], m[4], m[5], moe_router[i], moe_w1[i], moe_w3[i], moe_w2[i], ln_g[l, 1], ln_b[l, 1], alpha)
            if need_ctx:
                xc = _moe(xc, mc[3], mc[4], mc[5], moe_router[i], moe_w1[i], moe_w3[i], moe_w2[i], ln_g[l, 1],
                          ln_b[l, 1], alpha)
    return x, xc


def kernel(x, c, ctx, c_ctx, w_mod, b_mod, w_in, gla_w_gate, gla_b_gate, gla_norm_g, hy_conv_w, hy_conv_b, hy_f_w1, hy_f_b1, hy_f_freq1, hy_f_w2, hy_f_b2, hy_f_freq2, hy_f_w3, hy_f_b3, hy_skip, hy_norm_g, mla_q_norm_g, mla_w_uq, mla_kv_norm_g, mla_w_ukv, mla_norm_g, w_out, ln_g, ln_b, ffn_w1, ffn_w3, ffn_w2, moe_router, moe_w1, moe_w3, moe_w2):
    return _streams(x, c, ctx, c_ctx, w_mod, b_mod, w_in, gla_w_gate, gla_b_gate, gla_norm_g, hy_conv_w, hy_conv_b, hy_f_w1, hy_f_b1, hy_f_freq1, hy_f_w2, hy_f_b2, hy_f_freq2, hy_f_w3, hy_f_b3, hy_skip, hy_norm_g, mla_q_norm_g, mla_w_uq, mla_kv_norm_g, mla_w_ukv, mla_norm_g, w_out, ln_g, ln_b, ffn_w1, ffn_w3, ffn_w2, moe_router, moe_w1, moe_w3, moe_w2)[0]
```

```python
import functools
import math

import numpy as np
import jax
import jax.numpy as jnp
from jax import lax
from jax.experimental import pallas as pl
from jax.experimental.pallas import tpu as pltpu

F32 = jnp.float32
BF16 = jnp.bfloat16
HI = lax.Precision.HIGHEST

GRID_W = 64
GLA_HEADS, GLA_DK, GLA_DV, GLA_RANK, GLA_TAU = 4, 32, 64, 16, 16.0
HY_CH, HY_EMB = 256, 33
HY_DECAY_TARGET, HY_FAST_DECAY, HY_SLOW_DECAY = 1e-2, 0.3, 1.5
MLA_HEADS, MLA_Q_RANK, MLA_KV_RANK, MLA_NOPE, MLA_ROPE, MLA_V = 8, 256, 128, 64, 32, 64
MLA_SCALE = (MLA_NOPE + MLA_ROPE) ** -0.5
ROPE_BASE = 10000.0
N_EXPERTS = 8
IN_SPLITS = (128, 128, 256, 256, 32, 768, 256, 128, 32)

LANES = 128
SUBLANES = 8
VMEM_LIMIT = 56 * 1024 * 1024

ROW_TILE = 512
WIDE_ROW_TILE = 1024
GLA_CHUNK = 128
DFT_N2 = 256


def _cp(*sem):
    return pltpu.CompilerParams(dimension_semantics=sem, vmem_limit_bytes=VMEM_LIMIT)


def _full(shape):
    n = len(shape)
    return pl.BlockSpec(shape, lambda *_: (0,) * n)


def _idiv(x, d):
    assert d & (d - 1) == 0
    return lax.shift_right_logical(x, int(math.log2(d)))


INPROJ_WIDTHS = (768, 256, 512, 128, 256, 256)


def _arrange_w_in(w):
    cuts = np.cumsum(IN_SPLITS)[:-1]
    qa, ka, va, ga, alr, hyu, cq, ckv, kr = jnp.split(w, [int(c) for c in cuts], axis=1)
    z96 = jnp.zeros((w.shape[0], 96), w.dtype)
    return jnp.concatenate([hyu, qa, ka, va, ga, alr, z96, cq, ckv, kr, z96], axis=1).astype(BF16)


def _inproj_body(x_ref, sh_ref, sc_ref, w_ref, *out_refs):
    h = x_ref[0] * (1.0 + sc_ref[0]) + sh_ref[0]
    acc = jnp.dot(h.astype(BF16), w_ref[...], preferred_element_type=F32)
    off = 0
    for r in out_refs:
        w = r.shape[-1]
        r[0] = acc[:, off:off + w]
        off += w


def _inproj(x, shift, scale, w_arr):
    B, L, D = x.shape
    tm = min(ROW_TILE, L)
    n = w_arr.shape[1]
    row = lambda w: pl.BlockSpec((1, tm, w), lambda b, i: (b, i, 0))
    vec = pl.BlockSpec((1, 1, D), lambda b, i: (b, 0, 0))
    return pl.pallas_call(
        _inproj_body,
        grid=(B, L // tm),
        in_specs=[row(D), vec, vec, _full((D, n))],
        out_specs=[row(w) for w in INPROJ_WIDTHS],
        out_shape=[jax.ShapeDtypeStruct((B, L, w), F32) for w in INPROJ_WIDTHS],
        compiler_params=_cp("parallel", "parallel"),
        name="inproj",
    )(x, shift, scale, w_arr)


def _log_sigmoid(z):
    return jnp.minimum(z, 0.0) - jnp.log1p(jnp.exp(-jnp.abs(z)))


def _gla_body(qkf_ref, vf_ref, af_ref, qkb_ref, vb_ref, ab_ref, wg_ref, bg_ref, s0_ref,
              of_ref, ob_ref, sout_ref, s_ref):
    i = pl.program_id(0)
    C = qkf_ref.shape[1]
    KD = GLA_HEADS * GLA_DK
    VD = GLA_HEADS * GLA_DV

    @pl.when(i == 0)
    def _():
        s_ref[...] = s0_ref[...]

    r = lax.broadcasted_iota(jnp.int32, (C, C), 0)
    c = lax.broadcasted_iota(jnp.int32, (C, C), 1)
    tris = ((c <= r).astype(F32), (c >= r).astype(F32))
    lane_k = _idiv(lax.broadcasted_iota(jnp.int32, (1, KD), 1), GLA_DK)
    lane_v = _idiv(lax.broadcasted_iota(jnp.int32, (1, VD), 1), GLA_DV)
    rk = _idiv(lax.broadcasted_iota(jnp.int32, (KD, VD), 0), GLA_DK)
    cv = _idiv(lax.broadcasted_iota(jnp.int32, (KD, VD), 1), GLA_DV)
    ones = jnp.ones((C, VD), F32)
    refs = ((qkf_ref, vf_ref, af_ref, of_ref), (qkb_ref, vb_ref, ab_ref, ob_ref))
    chains = [(b, d) for b in range(qkf_ref.shape[0]) for d in range(2)]

    z = [jnp.dot(refs[d][2][b], wg_ref[...], precision=HI, preferred_element_type=F32) + bg_ref[...] for b, d in chains]
    la = [_log_sigmoid(zz[:, d * KD:(d + 1) * KD]) / GLA_TAU for zz, (b, d) in zip(z, chains)]
    bb = [jnp.dot(tris[d], l_, precision=HI, preferred_element_type=F32) for l_, (b, d) in zip(la, chains)]
    tot_b = [lax.dot_general(l_, ones, (((0,), (0,)), ((), ())), precision=HI, preferred_element_type=F32) for l_ in la]
    qe, ke, kl, vb, s_old = [], [], [], [], []
    for n, (b, d) in enumerate(chains):
        qk = refs[d][0][b]
        q = qk[:, :KD] * (GLA_DK ** -0.5)
        k = qk[:, KD:]
        tot = jnp.sum(la[n], axis=0, keepdims=True)
        qe.append(q * jnp.exp(bb[n]))
        ke.append((k * jnp.exp(-bb[n])).astype(BF16))
        kl.append((k * jnp.exp(tot - bb[n])).astype(BF16))
        vb.append(refs[d][1][b].astype(BF16))
        s_old.append(s_ref[2 * b + d])
    o = [jnp.dot(qe[n].astype(BF16), s_old[n].astype(BF16), preferred_element_type=F32) for n in range(len(chains))]
    att = [[lax.dot_general(jnp.where(lane_k == h, qe[n], 0.0).astype(BF16), ke[n], (((1,), (1,)), ((), ())),
                            preferred_element_type=F32) for h in range(GLA_HEADS)] for n in range(len(chains))]
    kv = [lax.dot_general(kl[n], vb[n], (((0,), (0,)), ((), ())), preferred_element_type=F32) for n in range(len(chains))]
    for n, (b, d) in enumerate(chains):
        on = o[n]
        for h in range(GLA_HEADS):
            oh = jnp.dot((att[n][h] * tris[d]).astype(BF16), vb[n], preferred_element_type=F32)
            on = on + jnp.where(lane_v == h, oh, 0.0)
        refs[d][3][b] = on
        s_ref[2 * b + d] = jnp.exp(tot_b[n]) * s_old[n] + jnp.where(rk == cv, kv[n], 0.0)

    @pl.when(i == pl.num_programs(0) - 1)
    def _():
        sout_ref[...] = s_ref[...]


def _gla(qk, vg, alr, wg, bg, s0):
    B, L, _ = qk.shape
    C = min(GLA_CHUNK, L)
    n = L // C
    KD, VD = GLA_HEADS * GLA_DK, GLA_HEADS * GLA_DV
    fwd = lambda w: pl.BlockSpec((B, C, w), lambda i: (0, i, 0))
    bwd = lambda w: pl.BlockSpec((B, C, w), lambda i: (0, n - 1 - i, 0))
    st = _full((2 * B, KD, VD))
    of, ob, s_out = pl.pallas_call(
        _gla_body,
        grid=(n,),
        in_specs=[fwd(2 * KD), fwd(VD), fwd(LANES), bwd(2 * KD), bwd(VD), bwd(LANES),
                  _full((LANES, 2 * KD)), _full((1, 2 * KD)), st],
        out_specs=[fwd(VD), bwd(VD), st],
        out_shape=[jax.ShapeDtypeStruct((B, L, VD), F32), jax.ShapeDtypeStruct((B, L, VD), F32),
                   jax.ShapeDtypeStruct((2 * B, KD, VD), F32)],
        scratch_shapes=[pltpu.VMEM((2 * B, KD, VD), F32)],
        compiler_params=_cp("arbitrary"),
        name="gla",
    )(qk, vg, alr, qk, vg, alr, wg, bg, s0.reshape(2 * B, KD, VD))
    return of, ob, s_out.reshape(B, 2, KD, VD)


def _arrange_gate(w_gate, b_gate):
    KD = GLA_HEADS * GLA_DK
    wg = jnp.zeros((LANES, 2 * KD), F32)
    wg = wg.at[:GLA_RANK, :KD].set(w_gate[0]).at[GLA_RANK:2 * GLA_RANK, KD:].set(w_gate[1])
    return wg, jnp.concatenate([b_gate[0], b_gate[1]])[None, :]


def _shortconv_body(x_ref, p_ref, n_ref, w_ref, b_ref, v_ref, x1_ref, x2_ref):
    i = pl.program_id(1)
    last = pl.num_programs(1) - 1
    x = x_ref[0]
    tm = x.shape[0]
    prev = jnp.where(i > 0, p_ref[0][7:8, :], 0.0)
    nxt = jnp.where(i < last, n_ref[0][0:1, :], 0.0)
    rid = lax.broadcasted_iota(jnp.int32, x.shape, 0)
    dn = jnp.where(rid == 0, prev, pltpu.roll(x, 1, 0))
    up = jnp.where(rid == tm - 1, nxt, pltpu.roll(x, tm - 1, 0))
    w = w_ref[...]
    y = b_ref[...] + dn * w[0:1] + x * w[1:2] + up * w[2:3]
    v_ref[0] = y[:, :HY_CH]
    x1_ref[0] = y[:, HY_CH:2 * HY_CH]
    x2_ref[0] = y[:, 2 * HY_CH:]


def _shortconv(u, w, b):
    B, L, W = u.shape
    tm = min(ROW_TILE, L)
    nb = tm // 8
    row = pl.BlockSpec((1, tm, W), lambda b_, i: (b_, i, 0))
    prev = pl.BlockSpec((1, 8, W), lambda b_, i: (b_, jnp.maximum(i * nb - 1, 0), 0))
    nxt = pl.BlockSpec((1, 8, W), lambda b_, i: (b_, jnp.minimum((i + 1) * nb, L // 8 - 1), 0))
    o = pl.BlockSpec((1, tm, HY_CH), lambda b_, i: (b_, i, 0))
    return pl.pallas_call(
        _shortconv_body,
        grid=(B, L // tm),
        in_specs=[row, prev, nxt, _full((3, W)), _full((1, W))],
        out_specs=[o, o, o],
        out_shape=[jax.ShapeDtypeStruct((B, L, HY_CH), F32)] * 3,
        compiler_params=_cp("parallel", "parallel"),
        name="shortconv",
    )(u, u, u, w, b[None, :])


def _filter_feats(L):
    pos = jnp.arange(L, dtype=F32)
    t = pos / (L - 1)
    bands = (HY_EMB - 1) // 2
    freqs = jnp.linspace(1e-4, bands - 1, bands, dtype=F32)
    ang = (2.0 * math.pi * pos / L)[:, None] * freqs
    z = jnp.concatenate([t[:, None], jnp.cos(ang), -jnp.sin(ang)], axis=-1)
    z = jnp.pad(z, ((0, 0), (0, LANES - HY_EMB)))
    deltas = jnp.abs(jnp.linspace(math.log(HY_DECAY_TARGET) / HY_SLOW_DECAY,
                                  math.log(HY_DECAY_TARGET) / HY_FAST_DECAY, HY_CH, dtype=F32))
    return z, jnp.tile(deltas, 4)[None, :]


def _filter_body(z_ref, w1_ref, b1_ref, f1_ref, w2_ref, b2_ref, f2_ref, w3_ref, b3_ref, dl_ref,
                 h_ref, ss_ref, *, L):
    i = pl.program_id(0)
    z = z_ref[...]
    tm = z.shape[0]
    hid = jnp.sin(f1_ref[...] * (jnp.dot(z, w1_ref[...], precision=HI, preferred_element_type=F32) + b1_ref[...]))
    hid = jnp.sin(f2_ref[...] * (jnp.dot(hid, w2_ref[...], precision=HI, preferred_element_type=F32) + b2_ref[...]))
    h = jnp.dot(hid, w3_ref[...], precision=HI, preferred_element_type=F32) + b3_ref[...]
    pos = (lax.broadcasted_iota(jnp.int32, (tm, 1), 0) + i * tm).astype(F32)
    t = pos / (L - 1)
    h = h * jnp.exp(-t * dl_ref[...])

    @pl.when(i == 0)
    def _():
        ss_ref[...] = jnp.zeros_like(ss_ref)

    ss_ref[...] += jnp.sum(h * h, axis=0, keepdims=True)
    col = lax.broadcasted_iota(jnp.int32, h.shape, 1)
    is_bwd = (_idiv(col, HY_CH) & 1) == 1
    h_ref[...] = jnp.where(jnp.logical_and(is_bwd, pos == 0.0), 0.0, h)


def _filters(L, fw1, fb1, ff1, fw2, fb2, ff2, fw3, fb3):
    z, dl = _filter_feats(L)
    tm = min(WIDE_ROW_TILE, L)
    Hf = fw2.shape[0]
    w1 = jnp.pad(fw1, ((0, LANES - HY_EMB), (0, 0)))
    NC = fw3.shape[1]
    return pl.pallas_call(
        functools.partial(_filter_body, L=L),
        grid=(L // tm,),
        in_specs=[pl.BlockSpec((tm, LANES), lambda i: (i, 0)), _full((LANES, Hf)), _full((1, Hf)), _full((1, Hf)),
                  _full((Hf, Hf)), _full((1, Hf)), _full((1, Hf)), _full((Hf, NC)), _full((1, NC)), _full((1, NC))],
        out_specs=[pl.BlockSpec((tm, NC), lambda i: (i, 0)), _full((1, NC))],
        out_shape=[jax.ShapeDtypeStruct((L, NC), F32), jax.ShapeDtypeStruct((1, NC), F32)],
        compiler_params=_cp("arbitrary"),
        name="hy_filters",
    )(z, w1, fb1[None], ff1[None], fw2, fb2[None], ff2[None], fw3, fb3[None], dl)


def _dft_consts(L):
    N = 2 * L
    N2 = DFT_N2
    N1 = N // N2
    half = N1 // 2
    k1 = np.arange(N1)[:, None].astype(np.float64)
    n1 = np.arange(N1)[None, :].astype(np.float64)
    a1 = 2.0 * np.pi * k1 * n1 / N1
    f1r, f1i = np.cos(a1), -np.sin(a1)
    fa = np.concatenate([f1r[:, :half], f1i[:, :half]], axis=0)
    fb = np.concatenate([f1r[:half, :], f1i[:half, :]], axis=1) / N
    k2 = np.arange(N2)[:, None].astype(np.float64)
    n2 = np.arange(N2)[None, :].astype(np.float64)
    a2 = 2.0 * np.pi * k2 * n2 / N2
    f2r, f2i = np.cos(a2), -np.sin(a2)
    g = np.block([[f2r, -f2i], [f2i, f2r]])
    gc = np.block([[f2r, f2i], [-f2i, f2r]])
    at = 2.0 * np.pi * (np.arange(N1)[:, None] * np.arange(N2)[None, :] % N) / N
    twr, twi = np.cos(at), -np.sin(at)
    c = lambda a: jnp.asarray(a, dtype=F32)
    bc = lambda a: jnp.broadcast_to(c(a)[:, :, None], (N1, N2, LANES))
    eye = np.eye(SUBLANES)
    return dict(N1=N1, N2=N2, half=half, fa=c(np.kron(fa, eye)), fb=c(np.kron(fb, eye)), g=c(g), gc=c(gc),
                twr=bc(twr), twi=bc(twi))


def _lanes(t, width):
    return jnp.concatenate([t] * (width // LANES), axis=-1)


def _dft1_body(f_ref, x_ref, o_ref):
    x = x_ref[0]
    x2 = x.reshape(x.shape[0] * SUBLANES, x.shape[2]).astype(BF16)
    y = jnp.dot(f_ref[...], x2, preferred_element_type=F32)
    o_ref[0] = y.reshape(o_ref.shape[1], SUBLANES, y.shape[1])


def _dft_stage1(fa, x):
    B, half, N2, W = x.shape
    R = fa.shape[0] // SUBLANES
    return pl.pallas_call(
        _dft1_body,
        grid=(B, N2 // SUBLANES),
        in_specs=[_full(fa.shape), pl.BlockSpec((1, half, SUBLANES, W), lambda b, j: (b, 0, j, 0))],
        out_specs=pl.BlockSpec((1, R, SUBLANES, W), lambda b, j: (b, 0, j, 0)),
        out_shape=jax.ShapeDtypeStruct((B, R, N2, W), F32),
        compiler_params=_cp("parallel", "parallel"),
        name="hy_dft1",
    )(fa.astype(BF16), x)


def _filter_spec_body(a_ref, twr_ref, twi_ref, g_ref, ss_ref, hf_ref):
    W = a_ref.shape[-1]
    ar, ai = a_ref[0, 0], a_ref[1, 0]
    twr, twi = _lanes(twr_ref[0], W), _lanes(twi_ref[0], W)
    xr = ar * twr - ai * twi
    xi = ar * twi + ai * twr
    z = jnp.dot(g_ref[...], jnp.concatenate([xr, xi], axis=0).astype(BF16), preferred_element_type=F32)
    n2 = z.shape[0] // 2
    zr, zi = z[:n2], z[n2:]
    ss = ss_ref[...]
    for o in range(2):
        f0, b0 = (2 * o) * HY_CH, (2 * o + 1) * HY_CH
        sc = lax.rsqrt(ss[:, f0:f0 + HY_CH] + ss[:, b0:b0 + HY_CH] + 1e-6)
        hf_ref[o, 0, 0] = ((zr[:, f0:f0 + HY_CH] + zr[:, b0:b0 + HY_CH]) * sc).astype(hf_ref.dtype)
        hf_ref[o, 0, 1] = ((zi[:, f0:f0 + HY_CH] - zi[:, b0:b0 + HY_CH]) * sc).astype(hf_ref.dtype)


def _filter_spectrum(h, ss, dc):
    L, NC = h.shape
    N1, N2, half = dc["N1"], dc["N2"], dc["half"]
    a = _dft_stage1(dc["fa"], h.reshape(1, half, N2, NC))
    a = a.reshape(2, N1, N2, NC)
    return pl.pallas_call(
        _filter_spec_body,
        grid=(N1,),
        in_specs=[pl.BlockSpec((2, 1, N2, NC), lambda k: (0, k, 0, 0)),
                  pl.BlockSpec((1, N2, LANES), lambda k: (k, 0, 0)), pl.BlockSpec((1, N2, LANES), lambda k: (k, 0, 0)),
                  _full((2 * N2, 2 * N2)), _full((1, NC))],
        out_specs=pl.BlockSpec((2, 1, 2, N2, HY_CH), lambda k: (0, k, 0, 0, 0)),
        out_shape=jax.ShapeDtypeStruct((2, N1, 2, N2, HY_CH), BF16),
        compiler_params=_cp("parallel"),
        name="hy_filter_spec",
    )(a, dc["twr"], dc["twi"], dc["g"].astype(BF16), ss)


SPEC_K1 = 4


def _spec_mul_body(a_ref, twr_ref, twi_ref, g_ref, gc_ref, hf_ref, o_ref):
    W = a_ref.shape[-1]
    ks = range(a_ref.shape[2])
    n2 = g_ref.shape[0] // 2
    x = []
    for k in ks:
        ar, ai = a_ref[0, 0, k], a_ref[0, 1, k]
        twr, twi = _lanes(twr_ref[k], W), _lanes(twi_ref[k], W)
        x.append(jnp.concatenate([ar * twr - ai * twi, ar * twi + ai * twr], axis=0).astype(BF16))
    z = [jnp.dot(g_ref[...], x_, preferred_element_type=F32) for x_ in x]
    y = []
    for k, z_ in zip(ks, z):
        zr, zi = z_[:n2], z_[n2:]
        hr, hi = hf_ref[0, k, 0].astype(F32), hf_ref[0, k, 1].astype(F32)
        y.append(jnp.concatenate([zr * hr - zi * hi, zr * hi + zi * hr], axis=0).astype(BF16))
    b = [jnp.dot(gc_ref[...], y_, preferred_element_type=F32) for y_ in y]
    for k, b_ in zip(ks, b):
        br, bi = b_[:n2], b_[n2:]
        twr, twi = _lanes(twr_ref[k], W), _lanes(twi_ref[k], W)
        o_ref[0, 0, k] = br * twr + bi * twi
        o_ref[0, 1, k] = bi * twr - br * twi


def _spec_mul(a, hf, order, dc):
    B = a.shape[0]
    N1, N2 = dc["N1"], dc["N2"]
    C = a.shape[-1]
    kb = min(SPEC_K1, N1)
    blk = pl.BlockSpec((1, 2, kb, N2, C), lambda k, b: (b, 0, k, 0, 0))
    tw = pl.BlockSpec((kb, N2, LANES), lambda k, b: (k, 0, 0))
    return pl.pallas_call(
        _spec_mul_body,
        grid=(N1 // kb, B),
        in_specs=[blk, tw, tw, _full((2 * N2, 2 * N2)), _full((2 * N2, 2 * N2)),
                  pl.BlockSpec((1, kb, 2, N2, C), lambda k, b: (order, k, 0, 0, 0))],
        out_specs=blk,
        out_shape=jax.ShapeDtypeStruct(a.shape, F32),
        compiler_params=_cp("parallel", "parallel"),
        name="hy_spec_mul",
    )(a, dc["twr"], dc["twi"], dc["g"].astype(BF16), dc["gc"].astype(BF16), hf)


def _dft3_body(f_ref, b_ref, u_ref, gate_ref, skip_ref, o_ref):
    bm = b_ref[0]
    b2 = bm.reshape(bm.shape[0] * SUBLANES, bm.shape[2]).astype(BF16)
    y = jnp.dot(f_ref[...], b2, preferred_element_type=F32)
    rows, C = y.shape
    u = u_ref[0].reshape(rows, C)
    gate = gate_ref[0].reshape(rows, C)
    o_ref[0] = (gate * (y + u * skip_ref[...])).reshape(o_ref.shape[1], SUBLANES, C)


def _dft_stage3(fb, bm, u, gate, skip):
    B, R, N2, C = bm.shape
    half = u.shape[1]
    row = pl.BlockSpec((1, half, SUBLANES, C), lambda b, j: (b, 0, j, 0))
    return pl.pallas_call(
        _dft3_body,
        grid=(B, N2 // SUBLANES),
        in_specs=[_full(fb.shape), pl.BlockSpec((1, R, SUBLANES, C), lambda b, j: (b, 0, j, 0)), row, row, _full((1, C))],
        out_specs=row,
        out_shape=jax.ShapeDtypeStruct((B, half, N2, C), F32),
        compiler_params=_cp("parallel", "parallel"),
        name="hy_dft3",
    )(fb.astype(BF16), bm, u, gate, skip[None, :])


def _longconv_gated(u, gate, hf, order, skip, dc):
    B, L, C = u.shape
    N1, N2, half = dc["N1"], dc["N2"], dc["half"]
    u4 = u.reshape(B, half, N2, C)
    a = _dft_stage1(dc["fa"], u4).reshape(B, 2, N1, N2, C)
    bm = _spec_mul(a, hf, order, dc).reshape(B, 2 * N1, N2, C)
    return _dft_stage3(dc["fb"], bm, u4, gate.reshape(B, half, N2, C), skip).reshape(B, L, C)


def _hyena(hyu, conv_w, conv_b, filt, skip):
    B, L, _ = hyu.shape
    v, x1, x2 = _shortconv(hyu, conv_w, conv_b)
    h, ss = _filters(L, *filt)
    dc = _dft_consts(L)
    hf = _filter_spectrum(h, ss, dc)
    z1 = _longconv_gated(v, x1, hf, 0, skip[0], dc)
    return _longconv_gated(z1, x2, hf, 1, skip[1], dc)


def _hyena_ctx_body(v_ref, x1_ref, x2_ref, h_ref, ss_ref, skip_ref, fc_ref, gc_ref, o_ref):
    fc, gc = fc_ref[...], gc_ref[...]
    n = fc.shape[0] // 2
    ss = ss_ref[...]
    h = h_ref[...]

    def conv(u, o):
        f0, b0 = (2 * o) * HY_CH, (2 * o + 1) * HY_CH
        sc = lax.rsqrt(ss[:, f0:f0 + HY_CH] + ss[:, b0:b0 + HY_CH] + 1e-6)
        x = jnp.dot(fc, u, precision=HI, preferred_element_type=F32)
        hf = jnp.dot(fc, h[:, f0:f0 + HY_CH], precision=HI, preferred_element_type=F32)
        hb = jnp.dot(fc, h[:, b0:b0 + HY_CH], precision=HI, preferred_element_type=F32)
        hr = (hf[:n] + hb[:n]) * sc
        hi = (hf[n:] - hb[n:]) * sc
        yr = x[:n] * hr - x[n:] * hi
        yi = x[:n] * hi + x[n:] * hr
        y = jnp.dot(gc, jnp.concatenate([yr, yi], axis=0), precision=HI, preferred_element_type=F32)
        return y + u * skip_ref[o:o + 1, :]

    z1 = x1_ref[0] * conv(v_ref[0], 0)
    o_ref[0] = x2_ref[0] * conv(z1, 1)


def _hyena_ctx(hyu, conv_w, conv_b, filt, skip):
    B, L, _ = hyu.shape
    v, x1, x2 = _shortconv(hyu, conv_w, conv_b)
    h, ss = _filters(L, *filt)
    N = 2 * L
    ang = 2.0 * np.pi * (np.arange(N)[:, None] * np.arange(L)[None, :] % N) / N
    fr, fi = np.cos(ang), -np.sin(ang)
    fc = jnp.asarray(np.concatenate([fr, fi], axis=0), dtype=F32)
    gc = jnp.asarray(np.concatenate([fr.T, fi.T], axis=1) / N, dtype=F32)
    row = pl.BlockSpec((1, L, HY_CH), lambda b: (b, 0, 0))
    return pl.pallas_call(
        _hyena_ctx_body,
        grid=(B,),
        in_specs=[row, row, row, _full(h.shape), _full(ss.shape), _full(skip.shape), _full(fc.shape), _full(gc.shape)],
        out_specs=row,
        out_shape=jax.ShapeDtypeStruct((B, L, HY_CH), F32),
        compiler_params=_cp("parallel"),
        name="hyena_ctx",
    )(v, x1, x2, h, ss, skip, fc, gc)


HEAD_PAD = 128


def _rope_swap(w):
    a, b, c, d = w[..., 0:8], w[..., 8:16], w[..., 16:24], w[..., 24:32]
    return jnp.concatenate([-b, a, -d, c], axis=-1)


def _arrange_wq(w_uq):
    R = w_uq.shape[0]
    w = w_uq.reshape(R, MLA_HEADS, MLA_NOPE + MLA_ROPE)
    rope = w[..., MLA_NOPE:]
    out = jnp.concatenate([w[..., :MLA_NOPE], rope, _rope_swap(rope)], axis=-1)
    return out.reshape(R, MLA_HEADS * HEAD_PAD).astype(BF16)


def _arrange_wkv(w_ukv):
    R = w_ukv.shape[0]
    w = w_ukv.reshape(R, MLA_HEADS, MLA_NOPE + MLA_V)
    wk = jnp.concatenate([w[..., :MLA_NOPE], jnp.zeros((R, MLA_HEADS, HEAD_PAD - MLA_NOPE), w.dtype)], axis=-1)
    wv = w[..., MLA_NOPE:]
    return wk.reshape(R, MLA_HEADS * HEAD_PAD).astype(BF16), wv.reshape(R, MLA_HEADS * MLA_V).astype(BF16)


def _kr_place():
    e = np.zeros((LANES, MLA_HEADS * HEAD_PAD), np.float32)
    es = np.zeros((LANES, MLA_HEADS * HEAD_PAD), np.float32)
    for h in range(MLA_HEADS):
        base = h * HEAD_PAD + MLA_NOPE
        for j in range(MLA_ROPE):
            e[j, base + j] = 1.0
            blk, r = divmod(j, 16)
            if r < 8:
                es[16 * blk + r + 8, base + j] = -1.0
            else:
                es[16 * blk + r - 8, base + j] = 1.0
    return jnp.asarray(e).astype(BF16), jnp.asarray(es).astype(BF16)


def _rope_tables(L, rope):
    if rope:
        t = np.arange(L)
        row, col = (t // GRID_W).astype(np.float32), (t % GRID_W).astype(np.float32)
        half = MLA_ROPE // 2
        inv = ROPE_BASE ** (-jnp.arange(0, half, 2, dtype=F32) / half)
        ar = jnp.asarray(row)[:, None] * inv
        ac = jnp.asarray(col)[:, None] * inv
        cos = jnp.concatenate([jnp.cos(ar), jnp.cos(ar), jnp.cos(ac), jnp.cos(ac)], axis=-1)
        sin = jnp.concatenate([jnp.sin(ar), jnp.sin(ar), jnp.sin(ac), jnp.sin(ac)], axis=-1)
    else:
        cos, sin = jnp.ones((L, MLA_ROPE), F32), jnp.zeros((L, MLA_ROPE), F32)
    return cos, sin


def _rms_rows(x, g, eps=1e-6):
    return x * lax.rsqrt(jnp.mean(x * x, axis=-1, keepdims=True) + eps) * g


def _qproj_body(cq_ref, g_ref, w_ref, t1_ref, t2_ref, q_ref):
    xn = _rms_rows(cq_ref[0], g_ref[...])
    acc = jnp.dot(xn.astype(BF16), w_ref[...], preferred_element_type=F32)
    W = acc.shape[1]
    t1, t2 = _lanes(t1_ref[...], W), _lanes(t2_ref[...], W)
    q_ref[0] = (acc * t1 + pltpu.roll(acc, W - MLA_ROPE, 1) * t2).astype(q_ref.dtype)


def _qproj(cq, g, wq, cos, sin):
    B, L, R = cq.shape
    tm = min(ROW_TILE, L)
    W = wq.shape[1]
    ones, zeros = jnp.ones((L, MLA_NOPE), F32), jnp.zeros((L, MLA_ROPE), F32)
    qs = MLA_SCALE * math.log2(math.e)
    t1 = jnp.concatenate([ones, cos, zeros], axis=-1) * qs
    t2 = jnp.concatenate([jnp.zeros((L, MLA_NOPE), F32), sin, zeros], axis=-1) * qs
    tab = pl.BlockSpec((tm, HEAD_PAD), lambda b, i: (i, 0))
    return pl.pallas_call(
        _qproj_body,
        grid=(B, L // tm),
        in_specs=[pl.BlockSpec((1, tm, R), lambda b, i: (b, i, 0)), _full((1, R)), _full((R, W)), tab, tab],
        out_specs=pl.BlockSpec((1, tm, W), lambda b, i: (b, i, 0)),
        out_shape=jax.ShapeDtypeStruct((B, L, W), BF16),
        compiler_params=_cp("parallel", "parallel"),
        name="mla_qproj",
    )(cq, g[None, :], wq, t1, t2)


def _kvproj_body(c_ref, g_ref, wk_ref, wv_ref, e_ref, es_ref, cos_ref, sin_ref, k_ref, v_ref):
    c = c_ref[0]
    R = MLA_KV_RANK
    xn = _rms_rows(c[:, :R], g_ref[...]).astype(BF16)
    kr = c[:, R:]
    acc = jnp.dot(xn, wk_ref[...], preferred_element_type=F32)
    acc += jnp.dot((kr * cos_ref[...]).astype(BF16), e_ref[...], preferred_element_type=F32)
    acc += jnp.dot((kr * sin_ref[...]).astype(BF16), es_ref[...], preferred_element_type=F32)
    k_ref[0] = acc.astype(k_ref.dtype)
    v_ref[0] = jnp.dot(xn, wv_ref[...], preferred_element_type=F32).astype(v_ref.dtype)


def _kvproj(ckvr, g, wk, wv, cos, sin):
    B, L, Wc = ckvr.shape
    tm = next(t for t in (1280, 512, 256, L) if L % t == 0)
    pad = jnp.zeros((L, LANES - MLA_ROPE), F32)
    cos_p, sin_p = jnp.concatenate([cos, pad], axis=-1), jnp.concatenate([sin, pad], axis=-1)
    e, es = _kr_place()
    tab = pl.BlockSpec((tm, LANES), lambda b, i: (i, 0))
    Wk, Wv = wk.shape[1], wv.shape[1]
    return pl.pallas_call(
        _kvproj_body,
        grid=(B, L // tm),
        in_specs=[pl.BlockSpec((1, tm, Wc), lambda b, i: (b, i, 0)), _full((1, MLA_KV_RANK)),
                  _full(wk.shape), _full(wv.shape), _full(e.shape), _full(es.shape), tab, tab],
        out_specs=[pl.BlockSpec((1, tm, Wk), lambda b, i: (b, i, 0)), pl.BlockSpec((1, tm, Wv), lambda b, i: (b, i, 0))],
        out_shape=[jax.ShapeDtypeStruct((B, L, Wk), BF16), jax.ShapeDtypeStruct((B, L, Wv), BF16)],
        compiler_params=_cp("parallel", "parallel"),
        name="mla_kvproj",
    )(ckvr, g[None, :], wk, wv, e, es, cos_p, sin_p)


FLASH_Q_TILE = 2048
FLASH_ROWS = 256
FLASH_KEYS = 256


def _flash_body(q_ref, k_ref, v_ref, o_ref, m_ref, l_ref, acc_ref, s_ref, *, R):
    j = pl.program_id(3)
    tq, tk = q_ref.shape[1], k_ref.shape[1]
    CK = FLASH_KEYS
    npc = CK // LANES

    @pl.when(j == 0)
    def _():
        m_ref[...] = jnp.full_like(m_ref, -jnp.inf)
        l_ref[...] = jnp.zeros_like(l_ref)
        acc_ref[...] = jnp.zeros_like(acc_ref)

    def pass1(a, r):
        lo, r0 = a * HEAD_PAD, r * R
        q = q_ref[0, r0:r0 + R, lo:lo + HEAD_PAD]
        mp = None
        for c in range(tk // CK):
            kc = k_ref[0, c * CK:(c + 1) * CK, lo:lo + HEAD_PAD]
            s = lax.dot_general(q, kc, (((1,), (1,)), ((), ())), preferred_element_type=F32)
            s_ref[r0:r0 + R, c * CK:(c + 1) * CK] = s
            for w in range(npc):
                pc = s[:, w * LANES:(w + 1) * LANES]
                mp = pc if mp is None else jnp.maximum(mp, pc)
        m_old = m_ref[a, r0:r0 + R, :]
        return m_old, jnp.maximum(m_old, jnp.max(mp, axis=1, keepdims=True))

    def pass2(a, r, m_old, m_new):
        r0 = r * R
        alpha = jnp.exp2(m_old - m_new)
        lp = jnp.zeros((R, LANES), F32)
        pv = jnp.zeros((R, 2 * MLA_V), F32)
        for c in range(tk // CK):
            s = s_ref[r0:r0 + R, c * CK:(c + 1) * CK]
            ps = [jnp.exp2(s[:, w * LANES:(w + 1) * LANES] - m_new) for w in range(npc)]
            for p_ in ps:
                lp = lp + p_
            p = jnp.concatenate(ps, axis=1).astype(BF16)
            pv = pv + jnp.dot(p, v_ref[0, c * CK:(c + 1) * CK, :], preferred_element_type=F32)
        l_ref[a, r0:r0 + R, :] = alpha * l_ref[a, r0:r0 + R, :] + jnp.sum(lp, axis=1, keepdims=True)
        acc_ref[a, r0:r0 + R, :] = alpha * acc_ref[a, r0:r0 + R, :] + pv
        m_ref[a, r0:r0 + R, :] = m_new

    assert tq // R >= 2
    blocks = [(a, r) for a in range(2) for r in range(tq // R)]
    pend = pass1(*blocks[0])
    for i, blk in enumerate(blocks):
        nxt = pass1(*blocks[i + 1]) if i + 1 < len(blocks) else None
        pass2(*blk, *pend)
        pend = nxt

    @pl.when(j == pl.num_programs(3) - 1)
    def _():
        lane = lax.broadcasted_iota(jnp.int32, acc_ref.shape[1:], 1)
        o_ref[0] = jnp.where(lane < MLA_V, acc_ref[0] / l_ref[0], acc_ref[1] / l_ref[1])


def _flash_tiles(Lq, Lk):
    tq = min(FLASH_Q_TILE, Lq)
    tk = next(t for t in (3328, 1280, 1024, 512, 256, Lk) if Lk % t == 0)
    return tq, tk


def _flash(q, k, v):
    B, Lq, _ = q.shape
    Lk = k.shape[1]
    tq, tk = _flash_tiles(Lq, Lk)
    hp = MLA_HEADS // 2
    return pl.pallas_call(
        functools.partial(_flash_body, R=min(FLASH_ROWS, tq // 2)),
        grid=(B, hp, Lq // tq, Lk // tk),
        in_specs=[pl.BlockSpec((1, tq, 2 * HEAD_PAD), lambda b, h, i, j: (b, i, h)),
                  pl.BlockSpec((1, tk, 2 * HEAD_PAD), lambda b, h, i, j: (b, j, h)),
                  pl.BlockSpec((1, tk, 2 * MLA_V), lambda b, h, i, j: (b, j, h))],
        out_specs=pl.BlockSpec((1, tq, 2 * MLA_V), lambda b, h, i, j: (b, i, h)),
        out_shape=jax.ShapeDtypeStruct((B, Lq, MLA_HEADS * MLA_V), F32),
        scratch_shapes=[pltpu.VMEM((2, tq, LANES), F32), pltpu.VMEM((2, tq, LANES), F32),
                        pltpu.VMEM((2, tq, 2 * MLA_V), F32), pltpu.VMEM((tq, tk), F32)],
        compiler_params=_cp("parallel", "parallel", "parallel", "arbitrary"),
        name="mla_flash",
    )(q, k, v)


def _layernorm_rows(x, g, b, eps=1e-5):
    mu = jnp.mean(x, axis=-1, keepdims=True)
    xc = x - mu
    var = jnp.mean(xc * xc, axis=-1, keepdims=True)
    return xc * lax.rsqrt(var + eps) * g + b


def _outproj_body(of_ref, ob_ref, g_ref, hy_ref, om_ref, x_ref, gate_ref, gg_ref, hg_ref, mg_ref,
                  w_ref, lg_ref, lb_ref, o_ref, *, alpha):
    VD = GLA_HEADS * GLA_DV
    o = of_ref[0] + ob_ref[0]
    r = _idiv(lax.broadcasted_iota(jnp.int32, (VD, VD), 0), GLA_DV)
    c = _idiv(lax.broadcasted_iota(jnp.int32, (VD, VD), 1), GLA_DV)
    grp = (r == c).astype(F32)
    ms = jnp.dot(o * o, grp, precision=HI, preferred_element_type=F32) * (1.0 / GLA_DV)
    g = g_ref[0]
    ya = o * lax.rsqrt(ms + 1e-6) * gg_ref[...] * (g * jax.nn.sigmoid(g))
    yb = _rms_rows(hy_ref[0], hg_ref[...])
    yc = _rms_rows(om_ref[0], mg_ref[...])
    acc = jnp.dot(ya.astype(BF16), w_ref[0:VD, :], preferred_element_type=F32)
    acc += jnp.dot(yb.astype(BF16), w_ref[VD:VD + HY_CH, :], preferred_element_type=F32)
    acc += jnp.dot(yc.astype(BF16), w_ref[VD + HY_CH:, :], preferred_element_type=F32)
    o_ref[0] = _layernorm_rows(alpha * x_ref[0] + gate_ref[0] * acc, lg_ref[...], lb_ref[...])


def _outproj(of, ob, vg, hy, om, x, gate, gla_g, hy_g, mla_g, w_out, ln_g, ln_b, alpha):
    B, L, D = x.shape
    tm = min(ROW_TILE, L)
    VD = GLA_HEADS * GLA_DV
    MD = MLA_HEADS * MLA_V
    row = lambda w: pl.BlockSpec((1, tm, w), lambda b, i: (b, i, 0))
    return pl.pallas_call(
        functools.partial(_outproj_body, alpha=alpha),
        grid=(B, L // tm),
        in_specs=[row(VD), row(VD), pl.BlockSpec((1, tm, VD), lambda b, i: (b, i, 1)), row(HY_CH), row(MD), row(D),
                  pl.BlockSpec((1, 1, D), lambda b, i: (b, 0, 0)), _full((1, VD)), _full((1, HY_CH)), _full((1, MD)),
                  _full(w_out.shape), _full((1, D)), _full((1, D))],
        out_specs=row(D),
        out_shape=jax.ShapeDtypeStruct((B, L, D), F32),
        compiler_params=_cp("parallel", "parallel"),
        name="outproj",
    )(of, ob, vg, hy, om, x, gate, jnp.tile(gla_g, GLA_HEADS)[None, :], hy_g[None, :], mla_g[None, :],
      w_out.astype(BF16), ln_g[None, :], ln_b[None, :])


def _ffn_body(x_ref, sh_ref, sc_ref, gate_ref, w1_ref, w3_ref, w2_ref, lg_ref, lb_ref, o_ref, h_ref, acc_ref, *, alpha):
    j = pl.program_id(2)

    @pl.when(j == 0)
    def _():
        h_ref[...] = (x_ref[0] * (1.0 + sc_ref[0]) + sh_ref[0]).astype(BF16)
        acc_ref[...] = jnp.zeros_like(acc_ref)

    h = h_ref[...]
    a = jnp.dot(h, w1_ref[...], preferred_element_type=F32)
    b = jnp.dot(h, w3_ref[...], preferred_element_type=F32)
    t = (a * jax.nn.sigmoid(a) * b).astype(BF16)
    acc_ref[...] += jnp.dot(t, w2_ref[...], preferred_element_type=F32)

    @pl.when(j == pl.num_programs(2) - 1)
    def _():
        o_ref[0] = _layernorm_rows(alpha * x_ref[0] + gate_ref[0] * acc_ref[...], lg_ref[...], lb_ref[...])


def _ffn_tile(F):
    for cand in (512, 256, 128):
        if F % cand == 0:
            return cand
    return F


def _ffn(x, shift, scale, gate, w1, w3, w2, ln_g, ln_b, alpha):
    B, L, D = x.shape
    F = w1.shape[1]
    tm = min(WIDE_ROW_TILE, L)
    tf = _ffn_tile(F)
    row = pl.BlockSpec((1, tm, D), lambda b, i, j: (b, i, 0))
    vec = pl.BlockSpec((1, 1, D), lambda b, i, j: (b, 0, 0))
    return pl.pallas_call(
        functools.partial(_ffn_body, alpha=alpha),
        grid=(B, L // tm, F // tf),
        in_specs=[row, vec, vec, vec,
                  pl.BlockSpec((D, tf), lambda b, i, j: (0, j)), pl.BlockSpec((D, tf), lambda b, i, j: (0, j)),
                  pl.BlockSpec((tf, D), lambda b, i, j: (j, 0)), _full((1, D)), _full((1, D))],
        out_specs=row,
        out_shape=jax.ShapeDtypeStruct((B, L, D), F32),
        scratch_shapes=[pltpu.VMEM((tm, D), BF16), pltpu.VMEM((tm, D), F32)],
        compiler_params=_cp("parallel", "parallel", "arbitrary"),
        name="ffn",
    )(x, shift, scale, gate, w1.astype(BF16), w3.astype(BF16), w2.astype(BF16), ln_g[None, :], ln_b[None, :])


MOE_TOKENS = 2048
MOE_ROWS = 256
RANK_CHUNK = 256


def _router_body(x_ref, sh_ref, sc_ref, wr_ref, h_ref, g_ref, rk_ref, rkt_ref, cnt_ref):
    h = x_ref[0] * (1.0 + sc_ref[0]) + sh_ref[0]
    h_ref[0] = h.astype(BF16)
    logits = jnp.dot(h, wr_ref[...], precision=HI, preferred_element_type=F32)
    lane = lax.broadcasted_iota(jnp.int32, logits.shape, 1).astype(F32)
    logits = jnp.where(lane < N_EXPERTS, logits, -jnp.inf)
    m1 = jnp.max(logits, axis=1, keepdims=True)
    i1 = jnp.min(jnp.where(logits == m1, lane, float(LANES)), axis=1, keepdims=True)
    rest = jnp.where(lane == i1, -jnp.inf, logits)
    m2 = jnp.max(rest, axis=1, keepdims=True)
    i2 = jnp.min(jnp.where(rest == m2, lane, float(LANES)), axis=1, keepdims=True)
    e2 = jnp.exp(m2 - m1)
    w1 = 1.0 / (1.0 + e2)
    w2 = e2 / (1.0 + e2)
    g_ref[0] = jnp.where(lane == i1, w1, 0.0) + jnp.where(lane == i2, w2, 0.0)
    sel = jnp.logical_or(lane == i1, lane == i2)
    self_ = sel.astype(F32)
    tm = h.shape[0]
    C = min(RANK_CHUNK, tm)
    r = lax.broadcasted_iota(jnp.int32, (C, C), 0)
    c = lax.broadcasted_iota(jnp.int32, (C, C), 1)
    tri = (c < r).astype(BF16)
    carry = jnp.zeros((1, LANES), F32)
    parts = []
    for k in range(tm // C):
        sk = self_[k * C:(k + 1) * C]
        parts.append(jnp.dot(tri, sk.astype(BF16), preferred_element_type=F32) + carry)
        carry = carry + jnp.sum(sk, axis=0, keepdims=True)
    rank = jnp.where(sel, jnp.concatenate(parts, axis=0), -1.0)
    rk_ref[0] = rank
    rkt_ref[0] = rank.T[:8]
    cnt_ref[0, 0] = carry


def _router(x, shift, scale, w_router):
    B, L, D = x.shape
    tm = min(MOE_TOKENS, L)
    nt = L // tm
    wr = jnp.pad(w_router, ((0, 0), (0, LANES - N_EXPERTS)))
    vec = pl.BlockSpec((1, 1, D), lambda b, i: (b, 0, 0))
    col = pl.BlockSpec((1, tm, LANES), lambda b, i: (b, i, 0))
    return pl.pallas_call(
        _router_body,
        grid=(B, nt),
        in_specs=[pl.BlockSpec((1, tm, D), lambda b, i: (b, i, 0)), vec, vec, _full((D, LANES))],
        out_specs=[pl.BlockSpec((1, tm, D), lambda b, i: (b, i, 0)), col, col,
                   pl.BlockSpec((1, 8, tm), lambda b, i: (b, 0, i)), pl.BlockSpec((1, 1, 1, LANES), lambda b, i: (b, i, 0, 0))],
        out_shape=[jax.ShapeDtypeStruct((B, L, D), BF16), jax.ShapeDtypeStruct((B, L, LANES), F32),
                   jax.ShapeDtypeStruct((B, L, LANES), F32), jax.ShapeDtypeStruct((B, 8, L), F32),
                   jax.ShapeDtypeStruct((B, nt, 1, LANES), F32)],
        compiler_params=_cp("parallel", "parallel"),
        name="moe_router",
    )(x, shift, scale, wr)


def _moe_body(cnt_ref, h_ref, g_ref, rk_ref, rkt_ref, w1_ref, w3_ref, w2_ref, o_ref, xg_ref, y_ref, *, M, P):
    b, i, e, j = pl.program_id(0), pl.program_id(1), pl.program_id(2), pl.program_id(3)
    nt, ne, nj = pl.num_programs(1), pl.num_programs(2), pl.num_programs(3)
    tm = h_ref.shape[1]
    cnt = cnt_ref[(b * nt + i) * ne + e]
    n_ch = lax.div(cnt + (M - 1), M)

    @pl.when(jnp.logical_and(e == 0, j == 0))
    def _():
        o_ref[...] = jnp.zeros_like(o_ref)

    @pl.when(j == 0)
    def _():
        rkt = rkt_ref[0, pl.ds(e, 1), :]

        def gather(c, carry):
            r0 = pl.multiple_of(c * M, 16)
            rows = (lax.broadcasted_iota(jnp.int32, (M, 1), 0) + c * M).astype(F32)
            onehot = (rkt == rows).astype(BF16)
            xg_ref[pl.ds(r0, M), :] = jnp.dot(onehot, h_ref[0], preferred_element_type=F32).astype(BF16)
            return carry

        lax.fori_loop(0, n_ch, gather, 0)

    def expert(chunks):
        r0 = [pl.multiple_of(c * M, 16) for c in chunks]
        xg = [xg_ref[pl.ds(r, M), :] for r in r0]
        a = [jnp.dot(x_, w1_ref[0], preferred_element_type=F32) for x_ in xg]
        g = [jnp.dot(x_, w3_ref[0], preferred_element_type=F32) for x_ in xg]
        t = [(a_ * jax.nn.sigmoid(a_) * g_).astype(BF16) for a_, g_ in zip(a, g)]
        yv = [jnp.dot(t_, w2_ref[0], preferred_element_type=F32) for t_ in t]

        @pl.when(j == 0)
        def _():
            for r, y_ in zip(r0, yv):
                y_ref[pl.ds(r, M), :] = y_

        @pl.when(j > 0)
        def _():
            for r, y_ in zip(r0, yv):
                y_ref[pl.ds(r, M), :] += y_

    def expert_pair(c2, carry):
        expert([2 * c2, 2 * c2 + 1])
        return carry

    lax.fori_loop(0, lax.div(n_ch, 2), expert_pair, 0)

    @pl.when(lax.rem(n_ch, 2) == 1)
    def _():
        expert([n_ch - 1])

    @pl.when(j == nj - 1)
    def _():
        for p in range(tm // P):
            lane = lax.broadcasted_iota(jnp.int32, (P, LANES), 1)
            rke = jnp.sum(jnp.where(lane == e, rk_ref[0, p * P:(p + 1) * P, :], 0.0), axis=1, keepdims=True)
            ge = jnp.sum(jnp.where(lane == e, g_ref[0, p * P:(p + 1) * P, :], 0.0), axis=1, keepdims=True)

            def scatter(c, carry):
                r0 = pl.multiple_of(c * M, 16)
                cols = (lax.broadcasted_iota(jnp.int32, (1, M), 1) + c * M).astype(F32)
                onehot = (rke == cols).astype(BF16)
                yb = y_ref[pl.ds(r0, M), :].astype(BF16)
                o_ref[0, p * P:(p + 1) * P, :] += ge * jnp.dot(onehot, yb, preferred_element_type=F32)
                return carry

            lax.fori_loop(0, n_ch, scatter, 0)


def _res_ln_body(x_ref, y_ref, gate_ref, lg_ref, lb_ref, o_ref, *, alpha):
    o_ref[0] = _layernorm_rows(alpha * x_ref[0] + gate_ref[0] * y_ref[0], lg_ref[...], lb_ref[...])


def _res_ln(x, y, gate, ln_g, ln_b, alpha):
    B, L, D = x.shape
    tm = min(WIDE_ROW_TILE, L)
    row = pl.BlockSpec((1, tm, D), lambda b, i: (b, i, 0))
    return pl.pallas_call(
        functools.partial(_res_ln_body, alpha=alpha),
        grid=(B, L // tm),
        in_specs=[row, row, pl.BlockSpec((1, 1, D), lambda b, i: (b, 0, 0)), _full((1, D)), _full((1, D))],
        out_specs=row,
        out_shape=jax.ShapeDtypeStruct((B, L, D), F32),
        compiler_params=_cp("parallel", "parallel"),
        name="res_ln",
    )(x, y, gate, ln_g[None, :], ln_b[None, :])


def _moe(x, shift, scale, gate, w_router, w1, w3, w2, ln_g, ln_b, alpha):
    B, L, D = x.shape
    E, _, F = w1.shape
    hb, gts, rk, rkt, cnt = _router(x, shift, scale, w_router)
    tm = min(MOE_TOKENS, L)
    nt = L // tm
    M = MOE_ROWS
    rows_max = -(-tm // M) * M
    tf = next(t for t in (896, 512, 256, 128, F) if F % t == 0)
    counts = cnt[:, :, 0, :E].astype(jnp.int32).reshape(-1)
    row = lambda w: pl.BlockSpec((1, tm, w), lambda b, i, e, j, c: (b, i, 0))
    y = pl.pallas_call(
        functools.partial(_moe_body, M=M, P=min(512, tm)),
        grid_spec=pltpu.PrefetchScalarGridSpec(
            num_scalar_prefetch=1,
            grid=(B, nt, E, F // tf),
            in_specs=[row(D), row(LANES), row(LANES), pl.BlockSpec((1, 8, tm), lambda b, i, e, j, c: (b, 0, i)),
                      pl.BlockSpec((1, D, tf), lambda b, i, e, j, c: (e, 0, j)),
                      pl.BlockSpec((1, D, tf), lambda b, i, e, j, c: (e, 0, j)),
                      pl.BlockSpec((1, tf, D), lambda b, i, e, j, c: (e, j, 0))],
            out_specs=row(D),
            scratch_shapes=[pltpu.VMEM((rows_max, D), BF16), pltpu.VMEM((rows_max, D), F32)],
        ),
        out_shape=jax.ShapeDtypeStruct((B, L, D), F32),
        compiler_params=_cp("parallel", "parallel", "arbitrary", "arbitrary"),
        name="moe",
    )(counts, hb, gts, rk, rkt, w1.astype(BF16), w3.astype(BF16), w2.astype(BF16))
    return _res_ln(x, y, gate, ln_g, ln_b, alpha)


def _mod_body(c_ref, w_ref, b_ref, o_ref):
    c = c_ref[...]
    s = c * jax.nn.sigmoid(c)
    o_ref[...] = jnp.dot(s, w_ref[...], precision=HI, preferred_element_type=F32) + b_ref[...]


def _modulation(cc, w_mod, b_mod):
    R, D = cc.shape
    N = w_mod.shape[1]
    tn = 1024
    return pl.pallas_call(
        _mod_body,
        grid=(N // tn,),
        in_specs=[_full((R, D)), pl.BlockSpec((D, tn), lambda j: (0, j)), pl.BlockSpec((1, tn), lambda j: (0, j))],
        out_specs=pl.BlockSpec((R, tn), lambda j: (0, j)),
        out_shape=jax.ShapeDtypeStruct((R, N), F32),
        compiler_params=_cp("parallel"),
        name="modulation",
    )(cc, w_mod, b_mod[None, :])


def _streams(x, c, ctx, c_ctx, w_mod, b_mod, w_in, gla_w_gate, gla_b_gate, gla_norm_g, hy_conv_w, hy_conv_b, hy_f_w1, hy_f_b1, hy_f_freq1, hy_f_w2, hy_f_b2, hy_f_freq2, hy_f_w3, hy_f_b3, hy_skip, hy_norm_g, mla_q_norm_g, mla_w_uq, mla_kv_norm_g, mla_w_ukv, mla_norm_g, w_out, ln_g, ln_b, ffn_w1, ffn_w3, ffn_w2, moe_router, moe_w1, moe_w3, moe_w2):
    B, L, D = x.shape
    Lc = ctx.shape[1]
    depth = w_mod.shape[0]
    alpha = (2.0 * depth) ** 0.25
    cc = jnp.zeros((8, D), F32).at[:B].set(c).at[B].set(c_ctx)
    cos, sin = _rope_tables(L, True)
    cos_c, sin_c = _rope_tables(Lc, False)
    cos_all, sin_all = jnp.concatenate([cos_c, cos], axis=0), jnp.concatenate([sin_c, sin], axis=0)
    KD, VD = GLA_HEADS * GLA_DK, GLA_HEADS * GLA_DV
    xc = ctx
    for l in range(depth):
        need_ctx = l < depth - 1
        mods = _modulation(cc, w_mod[l], b_mod[l])
        m = [mods[:B, k * D:(k + 1) * D][:, None, :] for k in range(6)]
        mc = [jnp.broadcast_to(mods[B, k * D:(k + 1) * D][None, None, :], (B, 1, D)) for k in range(6)]
        w_arr = _arrange_w_in(w_in[l])
        wg, bg = _arrange_gate(gla_w_gate[l], gla_b_gate[l])
        filt = (hy_f_w1[l], hy_f_b1[l], hy_f_freq1[l], hy_f_w2[l], hy_f_b2[l], hy_f_freq2[l], hy_f_w3[l], hy_f_b3[l])
        wq = _arrange_wq(mla_w_uq[l])
        wk, wv = _arrange_wkv(mla_w_ukv[l])

        hyu, qk, vg, alr, cq, ckvr = _inproj(x, m[0], m[1], w_arr)
        hyu_c, qk_c, vg_c, alr_c, cq_c, ckvr_c = _inproj(xc, mc[0], mc[1], w_arr)

        of_c, ob_c, s_c = _gla(qk_c, vg_c, alr_c, wg, bg, jnp.zeros((B, 2, KD, VD), F32))
        of, ob, _ = _gla(qk, vg, alr, wg, bg, s_c)
        hy = _hyena(hyu, hy_conv_w[l], hy_conv_b[l], filt, hy_skip[l])
        k_all, v_all = _kvproj(jnp.concatenate([ckvr_c, ckvr], axis=1), mla_kv_norm_g[l], wk, wv, cos_all, sin_all)
        k_c, v_c = k_all[:, :Lc], v_all[:, :Lc]
        q_m = _qproj(cq, mla_q_norm_g[l], wq, cos, sin)
        om = _flash(q_m, k_all, v_all)

        x = _outproj(of, ob, vg, hy, om, x, m[2], gla_norm_g[l], hy_norm_g[l], mla_norm_g[l], w_out[l],
                     ln_g[l, 0], ln_b[l, 0], alpha)
        if need_ctx:
            hy_c = _hyena_ctx(hyu_c, hy_conv_w[l], hy_conv_b[l], filt, hy_skip[l])
            q_c = _qproj(cq_c, mla_q_norm_g[l], wq, cos_c, sin_c)
            om_c = _flash(q_c, k_c, v_c)
            xc = _outproj(of_c, ob_c, vg_c, hy_c, om_c, xc, mc[2], gla_norm_g[l], hy_norm_g[l], mla_norm_g[l],
                          w_out[l], ln_g[l, 0], ln_b[l, 0], alpha)

        i = l // 2
        if l % 2 == 0:
            x = _ffn(x, m[3], m[4], m[5], ffn_w1[i], ffn_w3[i], ffn_w2[i], ln_g[l, 1], ln_b[l, 1], alpha)
            if need_ctx:
                xc = _ffn(xc, mc[3], mc[4], mc[5], ffn_w1[i], ffn_w3[i], ffn_w2[i], ln_g[l, 1], ln_b[l, 1], alpha)
        else:
            x = _moe(x, m[3], m[4], m[5], moe_router[i], moe_w1[i], moe_w3[i], moe_w2[i], ln_g[l, 1], ln_b[l, 1], alpha)
            if need_ctx:
                xc = _moe(xc, mc[3], mc[4], mc[5], moe_router[i], moe_w1[i], moe_w3[i], moe_w2[i], ln_g[l, 1],
                          ln_b[l, 1], alpha)
    return x, xc


def kernel(x, c, ctx, c_ctx, w_mod, b_mod, w_in, gla_w_gate, gla_b_gate, gla_norm_g, hy_conv_w, hy_conv_b, hy_f_w1, hy_f_b1, hy_f_freq1, hy_f_w2, hy_f_b2, hy_f_freq2, hy_f_w3, hy_f_b3, hy_skip, hy_norm_g, mla_q_norm_g, mla_w_uq, mla_kv_norm_g, mla_w_ukv, mla_norm_g, w_out, ln_g, ln_b, ffn_w1, ffn_w3, ffn_w2, moe_router, moe_w1, moe_w3, moe_w2):
    return _streams(x, c, ctx, c_ctx, w_mod, b_mod, w_in, gla_w_gate, gla_b_gate, gla_norm_g, hy_conv_w, hy_conv_b, hy_f_w1, hy_f_b1, hy_f_freq1, hy_f_w2, hy_f_b2, hy_f_freq2, hy_f_w3, hy_f_b3, hy_skip, hy_norm_g, mla_q_norm_g, mla_w_uq, mla_kv_norm_g, mla_w_ukv, mla_norm_g, w_out, ln_g, ln_b, ffn_w1, ffn_w3, ffn_w2, moe_router, moe_w1, moe_w3, moe_w2)[0]
```

```python
import functools
import math

import numpy as np
import jax
import jax.numpy as jnp
from jax import lax
from jax.experimental import pallas as pl
from jax.experimental.pallas import tpu as pltpu

F32 = jnp.float32
BF16 = jnp.bfloat16
HI = lax.Precision.HIGHEST

GRID_W = 64
GLA_HEADS, GLA_DK, GLA_DV, GLA_RANK, GLA_TAU = 4, 32, 64, 16, 16.0
HY_CH, HY_EMB = 256, 33
HY_DECAY_TARGET, HY_FAST_DECAY, HY_SLOW_DECAY = 1e-2, 0.3, 1.5
MLA_HEADS, MLA_Q_RANK, MLA_KV_RANK, MLA_NOPE, MLA_ROPE, MLA_V = 8, 256, 128, 64, 32, 64
MLA_SCALE = (MLA_NOPE + MLA_ROPE) ** -0.5
ROPE_BASE = 10000.0
N_EXPERTS = 8
IN_SPLITS = (128, 128, 256, 256, 32, 768, 256, 128, 32)

LANES = 128
SUBLANES = 8
VMEM_LIMIT = 56 * 1024 * 1024

ROW_TILE = 512
WIDE_ROW_TILE = 1024
GLA_CHUNK = 128
DFT_N2 = 256


def _cp(*sem):
    return pltpu.CompilerParams(dimension_semantics=sem, vmem_limit_bytes=VMEM_LIMIT)


def _full(shape):
    n = len(shape)
    return pl.BlockSpec(shape, lambda *_: (0,) * n)


def _idiv(x, d):
    assert d & (d - 1) == 0
    return lax.shift_right_logical(x, int(math.log2(d)))


INPROJ_WIDTHS = (768, 256, 512, 128, 256, 256)


def _arrange_w_in(w):
    cuts = np.cumsum(IN_SPLITS)[:-1]
    qa, ka, va, ga, alr, hyu, cq, ckv, kr = jnp.split(w, [int(c) for c in cuts], axis=1)
    z96 = jnp.zeros((w.shape[0], 96), w.dtype)
    return jnp.concatenate([hyu, qa, ka, va, ga, alr, z96, cq, ckv, kr, z96], axis=1).astype(BF16)


def _inproj_body(x_ref, sh_ref, sc_ref, w_ref, *out_refs):
    h = x_ref[0] * (1.0 + sc_ref[0]) + sh_ref[0]
    acc = jnp.dot(h.astype(BF16), w_ref[...], preferred_element_type=F32)
    off = 0
    for r in out_refs:
        w = r.shape[-1]
        r[0] = acc[:, off:off + w]
        off += w


def _inproj(x, shift, scale, w_arr):
    B, L, D = x.shape
    tm = min(ROW_TILE, L)
    n = w_arr.shape[1]
    row = lambda w: pl.BlockSpec((1, tm, w), lambda b, i: (b, i, 0))
    vec = pl.BlockSpec((1, 1, D), lambda b, i: (b, 0, 0))
    return pl.pallas_call(
        _inproj_body,
        grid=(B, L // tm),
        in_specs=[row(D), vec, vec, _full((D, n))],
        out_specs=[row(w) for w in INPROJ_WIDTHS],
        out_shape=[jax.ShapeDtypeStruct((B, L, w), F32) for w in INPROJ_WIDTHS],
        compiler_params=_cp("parallel", "parallel"),
        name="inproj",
    )(x, shift, scale, w_arr)


def _log_sigmoid(z):
    return jnp.minimum(z, 0.0) - jnp.log1p(jnp.exp(-jnp.abs(z)))


def _gla_body(qkf_ref, vf_ref, af_ref, qkb_ref, vb_ref, ab_ref, wg_ref, bg_ref, s0_ref,
              of_ref, ob_ref, sout_ref, s_ref):
    i = pl.program_id(0)
    C = qkf_ref.shape[1]
    KD = GLA_HEADS * GLA_DK
    VD = GLA_HEADS * GLA_DV

    @pl.when(i == 0)
    def _():
        s_ref[...] = s0_ref[...]

    r = lax.broadcasted_iota(jnp.int32, (C, C), 0)
    c = lax.broadcasted_iota(jnp.int32, (C, C), 1)
    tris = ((c <= r).astype(F32), (c >= r).astype(F32))
    lane_k = _idiv(lax.broadcasted_iota(jnp.int32, (1, KD), 1), GLA_DK)
    lane_v = _idiv(lax.broadcasted_iota(jnp.int32, (1, VD), 1), GLA_DV)
    rk = _idiv(lax.broadcasted_iota(jnp.int32, (KD, VD), 0), GLA_DK)
    cv = _idiv(lax.broadcasted_iota(jnp.int32, (KD, VD), 1), GLA_DV)
    ones = jnp.ones((C, VD), F32)
    refs = ((qkf_ref, vf_ref, af_ref, of_ref), (qkb_ref, vb_ref, ab_ref, ob_ref))
    chains = [(b, d) for b in range(qkf_ref.shape[0]) for d in range(2)]

    z = [jnp.dot(refs[d][2][b], wg_ref[...], precision=HI, preferred_element_type=F32) + bg_ref[...] for b, d in chains]
    la = [_log_sigmoid(zz[:, d * KD:(d + 1) * KD]) / GLA_TAU for zz, (b, d) in zip(z, chains)]
    bb = [jnp.dot(tris[d], l_, precision=HI, preferred_element_type=F32) for l_, (b, d) in zip(la, chains)]
    tot_b = [lax.dot_general(l_, ones, (((0,), (0,)), ((), ())), precision=HI, preferred_element_type=F32) for l_ in la]
    qe, ke, kl, vb, s_old = [], [], [], [], []
    for n, (b, d) in enumerate(chains):
        qk = refs[d][0][b]
        q = qk[:, :KD] * (GLA_DK ** -0.5)
        k = qk[:, KD:]
        tot = jnp.sum(la[n], axis=0, keepdims=True)
        qe.append(q * jnp.exp(bb[n]))
        ke.append((k * jnp.exp(-bb[n])).astype(BF16))
        kl.append((k * jnp.exp(tot - bb[n])).astype(BF16))
        vb.append(refs[d][1][b].astype(BF16))
        s_old.append(s_ref[2 * b + d])
    o = [jnp.dot(qe[n].astype(BF16), s_old[n].astype(BF16), preferred_element_type=F32) for n in range(len(chains))]
    att = [[lax.dot_general(jnp.where(lane_k == h, qe[n], 0.0).astype(BF16), ke[n], (((1,), (1,)), ((), ())),
                            preferred_element_type=F32) for h in range(GLA_HEADS)] for n in range(len(chains))]
    kv = [lax.dot_general(kl[n], vb[n], (((0,), (0,)), ((), ())), preferred_element_type=F32) for n in range(len(chains))]
    for n, (b, d) in enumerate(chains):
        on = o[n]
        for h in range(GLA_HEADS):
            oh = jnp.dot((att[n][h] * tris[d]).astype(BF16), vb[n], preferred_element_type=F32)
            on = on + jnp.where(lane_v == h, oh, 0.0)
        refs[d][3][b] = on
        s_ref[2 * b + d] = jnp.exp(tot_b[n]) * s_old[n] + jnp.where(rk == cv, kv[n], 0.0)

    @pl.when(i == pl.num_programs(0) - 1)
    def _():
        sout_ref[...] = s_ref[...]


def _gla(qk, vg, alr, wg, bg, s0):
    B, L, _ = qk.shape
    C = min(GLA_CHUNK, L)
    n = L // C
    KD, VD = GLA_HEADS * GLA_DK, GLA_HEADS * GLA_DV
    fwd = lambda w: pl.BlockSpec((B, C, w), lambda i: (0, i, 0))
    bwd = lambda w: pl.BlockSpec((B, C, w), lambda i: (0, n - 1 - i, 0))
    st = _full((2 * B, KD, VD))
    of, ob, s_out = pl.pallas_call(
        _gla_body,
        grid=(n,),
        in_specs=[fwd(2 * KD), fwd(VD), fwd(LANES), bwd(2 * KD), bwd(VD), bwd(LANES),
                  _full((LANES, 2 * KD)), _full((1, 2 * KD)), st],
        out_specs=[fwd(VD), bwd(VD), st],
        out_shape=[jax.ShapeDtypeStruct((B, L, VD), F32), jax.ShapeDtypeStruct((B, L, VD), F32),
                   jax.ShapeDtypeStruct((2 * B, KD, VD), F32)],
        scratch_shapes=[pltpu.VMEM((2 * B, KD, VD), F32)],
        compiler_params=_cp("arbitrary"),
        name="gla",
    )(qk, vg, alr, qk, vg, alr, wg, bg, s0.reshape(2 * B, KD, VD))
    return of, ob, s_out.reshape(B, 2, KD, VD)


def _arrange_gate(w_gate, b_gate):
    KD = GLA_HEADS * GLA_DK
    wg = jnp.zeros((LANES, 2 * KD), F32)
    wg = wg.at[:GLA_RANK, :KD].set(w_gate[0]).at[GLA_RANK:2 * GLA_RANK, KD:].set(w_gate[1])
    return wg, jnp.concatenate([b_gate[0], b_gate[1]])[None, :]


def _shortconv_body(x_ref, p_ref, n_ref, w_ref, b_ref, v_ref, x1_ref, x2_ref):
    i = pl.program_id(1)
    last = pl.num_programs(1) - 1
    x = x_ref[0]
    tm = x.shape[0]
    prev = jnp.where(i > 0, p_ref[0][7:8, :], 0.0)
    nxt = jnp.where(i < last, n_ref[0][0:1, :], 0.0)
    rid = lax.broadcasted_iota(jnp.int32, x.shape, 0)
    dn = jnp.where(rid == 0, prev, pltpu.roll(x, 1, 0))
    up = jnp.where(rid == tm - 1, nxt, pltpu.roll(x, tm - 1, 0))
    w = w_ref[...]
    y = b_ref[...] + dn * w[0:1] + x * w[1:2] + up * w[2:3]
    v_ref[0] = y[:, :HY_CH]
    x1_ref[0] = y[:, HY_CH:2 * HY_CH]
    x2_ref[0] = y[:, 2 * HY_CH:]


def _shortconv(u, w, b):
    B, L, W = u.shape
    tm = min(ROW_TILE, L)
    nb = tm // 8
    row = pl.BlockSpec((1, tm, W), lambda b_, i: (b_, i, 0))
    prev = pl.BlockSpec((1, 8, W), lambda b_, i: (b_, jnp.maximum(i * nb - 1, 0), 0))
    nxt = pl.BlockSpec((1, 8, W), lambda b_, i: (b_, jnp.minimum((i + 1) * nb, L // 8 - 1), 0))
    o = pl.BlockSpec((1, tm, HY_CH), lambda b_, i: (b_, i, 0))
    return pl.pallas_call(
        _shortconv_body,
        grid=(B, L // tm),
        in_specs=[row, prev, nxt, _full((3, W)), _full((1, W))],
        out_specs=[o, o, o],
        out_shape=[jax.ShapeDtypeStruct((B, L, HY_CH), F32)] * 3,
        compiler_params=_cp("parallel", "parallel"),
        name="shortconv",
    )(u, u, u, w, b[None, :])


def _filter_feats(L):
    pos = jnp.arange(L, dtype=F32)
    t = pos / (L - 1)
    bands = (HY_EMB - 1) // 2
    freqs = jnp.linspace(1e-4, bands - 1, bands, dtype=F32)
    ang = (2.0 * math.pi * pos / L)[:, None] * freqs
    z = jnp.concatenate([t[:, None], jnp.cos(ang), -jnp.sin(ang)], axis=-1)
    z = jnp.pad(z, ((0, 0), (0, LANES - HY_EMB)))
    deltas = jnp.abs(jnp.linspace(math.log(HY_DECAY_TARGET) / HY_SLOW_DECAY,
                                  math.log(HY_DECAY_TARGET) / HY_FAST_DECAY, HY_CH, dtype=F32))
    return z, jnp.tile(deltas, 4)[None, :]


def _filter_body(z_ref, w1_ref, b1_ref, f1_ref, w2_ref, b2_ref, f2_ref, w3_ref, b3_ref, dl_ref,
                 h_ref, ss_ref, *, L):
    i = pl.program_id(0)
    z = z_ref[...]
    tm = z.shape[0]
    hid = jnp.sin(f1_ref[...] * (jnp.dot(z, w1_ref[...], precision=HI, preferred_element_type=F32) + b1_ref[...]))
    hid = jnp.sin(f2_ref[...] * (jnp.dot(hid, w2_ref[...], precision=HI, preferred_element_type=F32) + b2_ref[...]))
    h = jnp.dot(hid, w3_ref[...], precision=HI, preferred_element_type=F32) + b3_ref[...]
    pos = (lax.broadcasted_iota(jnp.int32, (tm, 1), 0) + i * tm).astype(F32)
    t = pos / (L - 1)
    h = h * jnp.exp(-t * dl_ref[...])

    @pl.when(i == 0)
    def _():
        ss_ref[...] = jnp.zeros_like(ss_ref)

    ss_ref[...] += jnp.sum(h * h, axis=0, keepdims=True)
    col = lax.broadcasted_iota(jnp.int32, h.shape, 1)
    is_bwd = (_idiv(col, HY_CH) & 1) == 1
    h_ref[...] = jnp.where(jnp.logical_and(is_bwd, pos == 0.0), 0.0, h)


def _filters(L, fw1, fb1, ff1, fw2, fb2, ff2, fw3, fb3):
    z, dl = _filter_feats(L)
    tm = min(WIDE_ROW_TILE, L)
    Hf = fw2.shape[0]
    w1 = jnp.pad(fw1, ((0, LANES - HY_EMB), (0, 0)))
    NC = fw3.shape[1]
    return pl.pallas_call(
        functools.partial(_filter_body, L=L),
        grid=(L // tm,),
        in_specs=[pl.BlockSpec((tm, LANES), lambda i: (i, 0)), _full((LANES, Hf)), _full((1, Hf)), _full((1, Hf)),
                  _full((Hf, Hf)), _full((1, Hf)), _full((1, Hf)), _full((Hf, NC)), _full((1, NC)), _full((1, NC))],
        out_specs=[pl.BlockSpec((tm, NC), lambda i: (i, 0)), _full((1, NC))],
        out_shape=[jax.ShapeDtypeStruct((L, NC), F32), jax.ShapeDtypeStruct((1, NC), F32)],
        compiler_params=_cp("arbitrary"),
        name="hy_filters",
    )(z, w1, fb1[None], ff1[None], fw2, fb2[None], ff2[None], fw3, fb3[None], dl)


def _dft_consts(L):
    N = 2 * L
    N2 = DFT_N2
    N1 = N // N2
    half = N1 // 2
    k1 = np.arange(N1)[:, None].astype(np.float64)
    n1 = np.arange(N1)[None, :].astype(np.float64)
    a1 = 2.0 * np.pi * k1 * n1 / N1
    f1r, f1i = np.cos(a1), -np.sin(a1)
    fa = np.concatenate([f1r[:, :half], f1i[:, :half]], axis=0)
    fb = np.concatenate([f1r[:half, :], f1i[:half, :]], axis=1) / N
    k2 = np.arange(N2)[:, None].astype(np.float64)
    n2 = np.arange(N2)[None, :].astype(np.float64)
    a2 = 2.0 * np.pi * k2 * n2 / N2
    f2r, f2i = np.cos(a2), -np.sin(a2)
    g = np.block([[f2r, -f2i], [f2i, f2r]])
    gc = np.block([[f2r, f2i], [-f2i, f2r]])
    at = 2.0 * np.pi * (np.arange(N1)[:, None] * np.arange(N2)[None, :] % N) / N
    twr, twi = np.cos(at), -np.sin(at)
    c = lambda a: jnp.asarray(a, dtype=F32)
    bc = lambda a: jnp.broadcast_to(c(a)[:, :, None], (N1, N2, LANES))
    eye = np.eye(SUBLANES)
    return dict(N1=N1, N2=N2, half=half, fa=c(np.kron(fa, eye)), fb=c(np.kron(fb, eye)), g=c(g), gc=c(gc),
                twr=bc(twr), twi=bc(twi))


def _lanes(t, width):
    return jnp.concatenate([t] * (width // LANES), axis=-1)


def _dft1_body(f_ref, x_ref, o_ref):
    x = x_ref[0]
    x2 = x.reshape(x.shape[0] * SUBLANES, x.shape[2]).astype(BF16)
    y = jnp.dot(f_ref[...], x2, preferred_element_type=F32)
    o_ref[0] = y.reshape(o_ref.shape[1], SUBLANES, y.shape[1])


def _dft_stage1(fa, x):
    B, half, N2, W = x.shape
    R = fa.shape[0] // SUBLANES
    return pl.pallas_call(
        _dft1_body,
        grid=(B, N2 // SUBLANES),
        in_specs=[_full(fa.shape), pl.BlockSpec((1, half, SUBLANES, W), lambda b, j: (b, 0, j, 0))],
        out_specs=pl.BlockSpec((1, R, SUBLANES, W), lambda b, j: (b, 0, j, 0)),
        out_shape=jax.ShapeDtypeStruct((B, R, N2, W), F32),
        compiler_params=_cp("parallel", "parallel"),
        name="hy_dft1",
    )(fa.astype(BF16), x)


def _filter_spec_body(a_ref, twr_ref, twi_ref, g_ref, ss_ref, hf_ref):
    W = a_ref.shape[-1]
    ar, ai = a_ref[0, 0], a_ref[1, 0]
    twr, twi = _lanes(twr_ref[0], W), _lanes(twi_ref[0], W)
    xr = ar * twr - ai * twi
    xi = ar * twi + ai * twr
    z = jnp.dot(g_ref[...], jnp.concatenate([xr, xi], axis=0).astype(BF16), preferred_element_type=F32)
    n2 = z.shape[0] // 2
    zr, zi = z[:n2], z[n2:]
    ss = ss_ref[...]
    for o in range(2):
        f0, b0 = (2 * o) * HY_CH, (2 * o + 1) * HY_CH
        sc = lax.rsqrt(ss[:, f0:f0 + HY_CH] + ss[:, b0:b0 + HY_CH] + 1e-6)
        hf_ref[o, 0, 0] = ((zr[:, f0:f0 + HY_CH] + zr[:, b0:b0 + HY_CH]) * sc).astype(hf_ref.dtype)
        hf_ref[o, 0, 1] = ((zi[:, f0:f0 + HY_CH] - zi[:, b0:b0 + HY_CH]) * sc).astype(hf_ref.dtype)


def _filter_spectrum(h, ss, dc):
    L, NC = h.shape
    N1, N2, half = dc["N1"], dc["N2"], dc["half"]
    a = _dft_stage1(dc["fa"], h.reshape(1, half, N2, NC))
    a = a.reshape(2, N1, N2, NC)
    return pl.pallas_call(
        _filter_spec_body,
        grid=(N1,),
        in_specs=[pl.BlockSpec((2, 1, N2, NC), lambda k: (0, k, 0, 0)),
                  pl.BlockSpec((1, N2, LANES), lambda k: (k, 0, 0)), pl.BlockSpec((1, N2, LANES), lambda k: (k, 0, 0)),
                  _full((2 * N2, 2 * N2)), _full((1, NC))],
        out_specs=pl.BlockSpec((2, 1, 2, N2, HY_CH), lambda k: (0, k, 0, 0, 0)),
        out_shape=jax.ShapeDtypeStruct((2, N1, 2, N2, HY_CH), BF16),
        compiler_params=_cp("parallel"),
        name="hy_filter_spec",
    )(a, dc["twr"], dc["twi"], dc["g"].astype(BF16), ss)


SPEC_K1 = 4


def _spec_mul_body(a_ref, twr_ref, twi_ref, g_ref, gc_ref, hf_ref, o_ref):
    W = a_ref.shape[-1]
    ks = range(a_ref.shape[2])
    n2 = g_ref.shape[0] // 2
    x = []
    for k in ks:
        ar, ai = a_ref[0, 0, k], a_ref[0, 1, k]
        twr, twi = _lanes(twr_ref[k], W), _lanes(twi_ref[k], W)
        x.append(jnp.concatenate([ar * twr - ai * twi, ar * twi + ai * twr], axis=0).astype(BF16))
    z = [jnp.dot(g_ref[...], x_, preferred_element_type=F32) for x_ in x]
    y = []
    for k, z_ in zip(ks, z):
        zr, zi = z_[:n2], z_[n2:]
        hr, hi = hf_ref[0, k, 0].astype(F32), hf_ref[0, k, 1].astype(F32)
        y.append(jnp.concatenate([zr * hr - zi * hi, zr * hi + zi * hr], axis=0).astype(BF16))
    b = [jnp.dot(gc_ref[...], y_, preferred_element_type=F32) for y_ in y]
    for k, b_ in zip(ks, b):
        br, bi = b_[:n2], b_[n2:]
        twr, twi = _lanes(twr_ref[k], W), _lanes(twi_ref[k], W)
        o_ref[0, 0, k] = br * twr + bi * twi
        o_ref[0, 1, k] = bi * twr - br * twi


def _spec_mul(a, hf, order, dc):
    B = a.shape[0]
    N1, N2 = dc["N1"], dc["N2"]
    C = a.shape[-1]
    kb = min(SPEC_K1, N1)
    blk = pl.BlockSpec((1, 2, kb, N2, C), lambda k, b: (b, 0, k, 0, 0))
    tw = pl.BlockSpec((kb, N2, LANES), lambda k, b: (k, 0, 0))
    return pl.pallas_call(
        _spec_mul_body,
        grid=(N1 // kb, B),
        in_specs=[blk, tw, tw, _full((2 * N2, 2 * N2)), _full((2 * N2, 2 * N2)),
                  pl.BlockSpec((1, kb, 2, N2, C), lambda k, b: (order, k, 0, 0, 0))],
        out_specs=blk,
        out_shape=jax.ShapeDtypeStruct(a.shape, F32),
        compiler_params=_cp("parallel", "parallel"),
        name="hy_spec_mul",
    )(a, dc["twr"], dc["twi"], dc["g"].astype(BF16), dc["gc"].astype(BF16), hf)


def _dft3_body(f_ref, b_ref, u_ref, gate_ref, skip_ref, o_ref):
    bm = b_ref[0]
    b2 = bm.reshape(bm.shape[0] * SUBLANES, bm.shape[2]).astype(BF16)
    y = jnp.dot(f_ref[...], b2, preferred_element_type=F32)
    rows, C = y.shape
    u = u_ref[0].reshape(rows, C)
    gate = gate_ref[0].reshape(rows, C)
    o_ref[0] = (gate * (y + u * skip_ref[...])).reshape(o_ref.shape[1], SUBLANES, C)


def _dft_stage3(fb, bm, u, gate, skip):
    B, R, N2, C = bm.shape
    half = u.shape[1]
    row = pl.BlockSpec((1, half, SUBLANES, C), lambda b, j: (b, 0, j, 0))
    return pl.pallas_call(
        _dft3_body,
        grid=(B, N2 // SUBLANES),
        in_specs=[_full(fb.shape), pl.BlockSpec((1, R, SUBLANES, C), lambda b, j: (b, 0, j, 0)), row, row, _full((1, C))],
        out_specs=row,
        out_shape=jax.ShapeDtypeStruct((B, half, N2, C), F32),
        compiler_params=_cp("parallel", "parallel"),
        name="hy_dft3",
    )(fb.astype(BF16), bm, u, gate, skip[None, :])


def _longconv_gated(u, gate, hf, order, skip, dc):
    B, L, C = u.shape
    N1, N2, half = dc["N1"], dc["N2"], dc["half"]
    u4 = u.reshape(B, half, N2, C)
    a = _dft_stage1(dc["fa"], u4).reshape(B, 2, N1, N2, C)
    bm = _spec_mul(a, hf, order, dc).reshape(B, 2 * N1, N2, C)
    return _dft_stage3(dc["fb"], bm, u4, gate.reshape(B, half, N2, C), skip).reshape(B, L, C)


def _hyena(hyu, conv_w, conv_b, filt, skip):
    B, L, _ = hyu.shape
    v, x1, x2 = _shortconv(hyu, conv_w, conv_b)
    h, ss = _filters(L, *filt)
    dc = _dft_consts(L)
    hf = _filter_spectrum(h, ss, dc)
    z1 = _longconv_gated(v, x1, hf, 0, skip[0], dc)
    return _longconv_gated(z1, x2, hf, 1, skip[1], dc)


def _hyena_ctx_body(v_ref, x1_ref, x2_ref, h_ref, ss_ref, skip_ref, fc_ref, gc_ref, o_ref):
    fc, gc = fc_ref[...], gc_ref[...]
    n = fc.shape[0] // 2
    ss = ss_ref[...]
    h = h_ref[...]

    def conv(u, o):
        f0, b0 = (2 * o) * HY_CH, (2 * o + 1) * HY_CH
        sc = lax.rsqrt(ss[:, f0:f0 + HY_CH] + ss[:, b0:b0 + HY_CH] + 1e-6)
        x = jnp.dot(fc, u, precision=HI, preferred_element_type=F32)
        hf = jnp.dot(fc, h[:, f0:f0 + HY_CH], precision=HI, preferred_element_type=F32)
        hb = jnp.dot(fc, h[:, b0:b0 + HY_CH], precision=HI, preferred_element_type=F32)
        hr = (hf[:n] + hb[:n]) * sc
        hi = (hf[n:] - hb[n:]) * sc
        yr = x[:n] * hr - x[n:] * hi
        yi = x[:n] * hi + x[n:] * hr
        y = jnp.dot(gc, jnp.concatenate([yr, yi], axis=0), precision=HI, preferred_element_type=F32)
        return y + u * skip_ref[o:o + 1, :]

    z1 = x1_ref[0] * conv(v_ref[0], 0)
    o_ref[0] = x2_ref[0] * conv(z1, 1)


def _hyena_ctx(hyu, conv_w, conv_b, filt, skip):
    B, L, _ = hyu.shape
    v, x1, x2 = _shortconv(hyu, conv_w, conv_b)
    h, ss = _filters(L, *filt)
    N = 2 * L
    ang = 2.0 * np.pi * (np.arange(N)[:, None] * np.arange(L)[None, :] % N) / N
    fr, fi = np.cos(ang), -np.sin(ang)
    fc = jnp.asarray(np.concatenate([fr, fi], axis=0), dtype=F32)
    gc = jnp.asarray(np.concatenate([fr.T, fi.T], axis=1) / N, dtype=F32)
    row = pl.BlockSpec((1, L, HY_CH), lambda b: (b, 0, 0))
    return pl.pallas_call(
        _hyena_ctx_body,
        grid=(B,),
        in_specs=[row, row, row, _full(h.shape), _full(ss.shape), _full(skip.shape), _full(fc.shape), _full(gc.shape)],
        out_specs=row,
        out_shape=jax.ShapeDtypeStruct((B, L, HY_CH), F32),
        compiler_params=_cp("parallel"),
        name="hyena_ctx",
    )(v, x1, x2, h, ss, skip, fc, gc)


HEAD_PAD = 128


def _rope_swap(w):
    a, b, c, d = w[..., 0:8], w[..., 8:16], w[..., 16:24], w[..., 24:32]
    return jnp.concatenate([-b, a, -d, c], axis=-1)


def _arrange_wq(w_uq):
    R = w_uq.shape[0]
    w = w_uq.reshape(R, MLA_HEADS, MLA_NOPE + MLA_ROPE)
    rope = w[..., MLA_NOPE:]
    out = jnp.concatenate([w[..., :MLA_NOPE], rope, _rope_swap(rope)], axis=-1)
    return out.reshape(R, MLA_HEADS * HEAD_PAD).astype(BF16)


def _arrange_wkv(w_ukv):
    R = w_ukv.shape[0]
    w = w_ukv.reshape(R, MLA_HEADS, MLA_NOPE + MLA_V)
    wk = jnp.concatenate([w[..., :MLA_NOPE], jnp.zeros((R, MLA_HEADS, HEAD_PAD - MLA_NOPE), w.dtype)], axis=-1)
    wv = w[..., MLA_NOPE:]
    return wk.reshape(R, MLA_HEADS * HEAD_PAD).astype(BF16), wv.reshape(R, MLA_HEADS * MLA_V).astype(BF16)


def _kr_place():
    e = np.zeros((LANES, MLA_HEADS * HEAD_PAD), np.float32)
    es = np.zeros((LANES, MLA_HEADS * HEAD_PAD), np.float32)
    for h in range(MLA_HEADS):
        base = h * HEAD_PAD + MLA_NOPE
        for j in range(MLA_ROPE):
            e[j, base + j] = 1.0
            blk, r = divmod(j, 16)
            if r < 8:
                es[16 * blk + r + 8, base + j] = -1.0
            else:
                es[16 * blk + r - 8, base + j] = 1.0
    return jnp.asarray(e).astype(BF16), jnp.asarray(es).astype(BF16)


def _rope_tables(L, rope):
    if rope:
        t = np.arange(L)
        row, col = (t // GRID_W).astype(np.float32), (t % GRID_W).astype(np.float32)
        half = MLA_ROPE // 2
        inv = ROPE_BASE ** (-jnp.arange(0, half, 2, dtype=F32) / half)
        ar = jnp.asarray(row)[:, None] * inv
        ac = jnp.asarray(col)[:, None] * inv
        cos = jnp.concatenate([jnp.cos(ar), jnp.cos(ar), jnp.cos(ac), jnp.cos(ac)], axis=-1)
        sin = jnp.concatenate([jnp.sin(ar), jnp.sin(ar), jnp.sin(ac), jnp.sin(ac)], axis=-1)
    else:
        cos, sin = jnp.ones((L, MLA_ROPE), F32), jnp.zeros((L, MLA_ROPE), F32)
    return cos, sin


def _rms_rows(x, g, eps=1e-6):
    return x * lax.rsqrt(jnp.mean(x * x, axis=-1, keepdims=True) + eps) * g


def _qproj_body(cq_ref, g_ref, w_ref, t1_ref, t2_ref, q_ref):
    xn = _rms_rows(cq_ref[0], g_ref[...])
    acc = jnp.dot(xn.astype(BF16), w_ref[...], preferred_element_type=F32)
    W = acc.shape[1]
    t1, t2 = _lanes(t1_ref[...], W), _lanes(t2_ref[...], W)
    q_ref[0] = (acc * t1 + pltpu.roll(acc, W - MLA_ROPE, 1) * t2).astype(q_ref.dtype)


def _qproj(cq, g, wq, cos, sin):
    B, L, R = cq.shape
    tm = min(ROW_TILE, L)
    W = wq.shape[1]
    ones, zeros = jnp.ones((L, MLA_NOPE), F32), jnp.zeros((L, MLA_ROPE), F32)
    qs = MLA_SCALE * math.log2(math.e)
    t1 = jnp.concatenate([ones, cos, zeros], axis=-1) * qs
    t2 = jnp.concatenate([jnp.zeros((L, MLA_NOPE), F32), sin, zeros], axis=-1) * qs
    tab = pl.BlockSpec((tm, HEAD_PAD), lambda b, i: (i, 0))
    return pl.pallas_call(
        _qproj_body,
        grid=(B, L // tm),
        in_specs=[pl.BlockSpec((1, tm, R), lambda b, i: (b, i, 0)), _full((1, R)), _full((R, W)), tab, tab],
        out_specs=pl.BlockSpec((1, tm, W), lambda b, i: (b, i, 0)),
        out_shape=jax.ShapeDtypeStruct((B, L, W), BF16),
        compiler_params=_cp("parallel", "parallel"),
        name="mla_qproj",
    )(cq, g[None, :], wq, t1, t2)


def _kvproj_body(c_ref, g_ref, wk_ref, wv_ref, e_ref, es_ref, cos_ref, sin_ref, k_ref, v_ref):
    c = c_ref[0]
    R = MLA_KV_RANK
    xn = _rms_rows(c[:, :R], g_ref[...]).astype(BF16)
    kr = c[:, R:]
    acc = jnp.dot(xn, wk_ref[...], preferred_element_type=F32)
    acc += jnp.dot((kr * cos_ref[...]).astype(BF16), e_ref[...], preferred_element_type=F32)
    acc += jnp.dot((kr * sin_ref[...]).astype(BF16), es_ref[...], preferred_element_type=F32)
    k_ref[0] = acc.astype(k_ref.dtype)
    v_ref[0] = jnp.dot(xn, wv_ref[...], preferred_element_type=F32).astype(v_ref.dtype)


def _kvproj(ckvr, g, wk, wv, cos, sin):
    B, L, Wc = ckvr.shape
    tm = next(t for t in (1280, 512, 256, L) if L % t == 0)
    pad = jnp.zeros((L, LANES - MLA_ROPE), F32)
    cos_p, sin_p = jnp.concatenate([cos, pad], axis=-1), jnp.concatenate([sin, pad], axis=-1)
    e, es = _kr_place()
    tab = pl.BlockSpec((tm, LANES), lambda b, i: (i, 0))
    Wk, Wv = wk.shape[1], wv.shape[1]
    return pl.pallas_call(
        _kvproj_body,
        grid=(B, L // tm),
        in_specs=[pl.BlockSpec((1, tm, Wc), lambda b, i: (b, i, 0)), _full((1, MLA_KV_RANK)),
                  _full(wk.shape), _full(wv.shape), _full(e.shape), _full(es.shape), tab, tab],
        out_specs=[pl.BlockSpec((1, tm, Wk), lambda b, i: (b, i, 0)), pl.BlockSpec((1, tm, Wv), lambda b, i: (b, i, 0))],
        out_shape=[jax.ShapeDtypeStruct((B, L, Wk), BF16), jax.ShapeDtypeStruct((B, L, Wv), BF16)],
        compiler_params=_cp("parallel", "parallel"),
        name="mla_kvproj",
    )(ckvr, g[None, :], wk, wv, e, es, cos_p, sin_p)


FLASH_Q_TILE = 2048
FLASH_ROWS = 256
FLASH_KEYS = 256


def _flash_body(q_ref, k_ref, v_ref, o_ref, m_ref, l_ref, acc_ref, s_ref, *, R):
    j = pl.program_id(3)
    tq, tk = q_ref.shape[1], k_ref.shape[1]
    CK = FLASH_KEYS
    npc = CK // LANES

    @pl.when(j == 0)
    def _():
        m_ref[...] = jnp.full_like(m_ref, -jnp.inf)
        l_ref[...] = jnp.zeros_like(l_ref)
        acc_ref[...] = jnp.zeros_like(acc_ref)

    def pass1(a, r):
        lo, r0 = a * HEAD_PAD, r * R
        q = q_ref[0, r0:r0 + R, lo:lo + HEAD_PAD]
        mp = None
        for c in range(tk // CK):
            kc = k_ref[0, c * CK:(c + 1) * CK, lo:lo + HEAD_PAD]
            s = lax.dot_general(q, kc, (((1,), (1,)), ((), ())), preferred_element_type=F32)
            s_ref[r0:r0 + R, c * CK:(c + 1) * CK] = s
            for w in range(npc):
                pc = s[:, w * LANES:(w + 1) * LANES]
                mp = pc if mp is None else jnp.maximum(mp, pc)
        m_old = m_ref[a, r0:r0 + R, :]
        return m_old, jnp.maximum(m_old, jnp.max(mp, axis=1, keepdims=True))

    def pass2(a, r, m_old, m_new):
        r0 = r * R
        alpha = jnp.exp2(m_old - m_new)
        lp = jnp.zeros((R, LANES), F32)
        pv = jnp.zeros((R, 2 * MLA_V), F32)
        for c in range(tk // CK):
            s = s_ref[r0:r0 + R, c * CK:(c + 1) * CK]
            ps = [jnp.exp2(s[:, w * LANES:(w + 1) * LANES] - m_new) for w in range(npc)]
            for p_ in ps:
                lp = lp + p_
            p = jnp.concatenate(ps, axis=1).astype(BF16)
            pv = pv + jnp.dot(p, v_ref[0, c * CK:(c + 1) * CK, :], preferred_element_type=F32)
        l_ref[a, r0:r0 + R, :] = alpha * l_ref[a, r0:r0 + R, :] + jnp.sum(lp, axis=1, keepdims=True)
        acc_ref[a, r0:r0 + R, :] = alpha * acc_ref[a, r0:r0 + R, :] + pv
        m_ref[a, r0:r0 + R, :] = m_new

    assert tq // R >= 2
    blocks = [(a, r) for a in range(2) for r in range(tq // R)]
    pend = pass1(*blocks[0])
    for i, blk in enumerate(blocks):
        nxt = pass1(*blocks[i + 1]) if i + 1 < len(blocks) else None
        pass2(*blk, *pend)
        pend = nxt

    @pl.when(j == pl.num_programs(3) - 1)
    def _():
        lane = lax.broadcasted_iota(jnp.int32, acc_ref.shape[1:], 1)
        o_ref[0] = jnp.where(lane < MLA_V, acc_ref[0] / l_ref[0], acc_ref[1] / l_ref[1])


def _flash_tiles(Lq, Lk):
    tq = min(FLASH_Q_TILE, Lq)
    tk = next(t for t in (3328, 1280, 1024, 512, 256, Lk) if Lk % t == 0)
    return tq, tk


def _flash(q, k, v):
    B, Lq, _ = q.shape
    Lk = k.shape[1]
    tq, tk = _flash_tiles(Lq, Lk)
    hp = MLA_HEADS // 2
    return pl.pallas_call(
        functools.partial(_flash_body, R=min(FLASH_ROWS, tq // 2)),
        grid=(B, hp, Lq // tq, Lk // tk),
        in_specs=[pl.BlockSpec((1, tq, 2 * HEAD_PAD), lambda b, h, i, j: (b, i, h)),
                  pl.BlockSpec((1, tk, 2 * HEAD_PAD), lambda b, h, i, j: (b, j, h)),
                  pl.BlockSpec((1, tk, 2 * MLA_V), lambda b, h, i, j: (b, j, h))],
        out_specs=pl.BlockSpec((1, tq, 2 * MLA_V), lambda b, h, i, j: (b, i, h)),
        out_shape=jax.ShapeDtypeStruct((B, Lq, MLA_HEADS * MLA_V), F32),
        scratch_shapes=[pltpu.VMEM((2, tq, LANES), F32), pltpu.VMEM((2, tq, LANES), F32),
                        pltpu.VMEM((2, tq, 2 * MLA_V), F32), pltpu.VMEM((tq, tk), F32)],
        compiler_params=_cp("parallel", "parallel", "parallel", "arbitrary"),
        name="mla_flash",
    )(q, k, v)


def _layernorm_rows(x, g, b, eps=1e-5):
    mu = jnp.mean(x, axis=-1, keepdims=True)
    xc = x - mu
    var = jnp.mean(xc * xc, axis=-1, keepdims=True)
    return xc * lax.rsqrt(var + eps) * g + b


def _outproj_body(of_ref, ob_ref, g_ref, hy_ref, om_ref, x_ref, gate_ref, gg_ref, hg_ref, mg_ref,
                  w_ref, lg_ref, lb_ref, o_ref, *, alpha):
    VD = GLA_HEADS * GLA_DV
    o = of_ref[0] + ob_ref[0]
    r = _idiv(lax.broadcasted_iota(jnp.int32, (VD, VD), 0), GLA_DV)
    c = _idiv(lax.broadcasted_iota(jnp.int32, (VD, VD), 1), GLA_DV)
    grp = (r == c).astype(F32)
    ms = jnp.dot(o * o, grp, precision=HI, preferred_element_type=F32) * (1.0 / GLA_DV)
    g = g_ref[0]
    ya = o * lax.rsqrt(ms + 1e-6) * gg_ref[...] * (g * jax.nn.sigmoid(g))
    yb = _rms_rows(hy_ref[0], hg_ref[...])
    yc = _rms_rows(om_ref[0], mg_ref[...])
    acc = jnp.dot(ya.astype(BF16), w_ref[0:VD, :], preferred_element_type=F32)
    acc += jnp.dot(yb.astype(BF16), w_ref[VD:VD + HY_CH, :], preferred_element_type=F32)
    acc += jnp.dot(yc.astype(BF16), w_ref[VD + HY_CH:, :], preferred_element_type=F32)
    o_ref[0] = _layernorm_rows(alpha * x_ref[0] + gate_ref[0] * acc, lg_ref[...], lb_ref[...])


def _outproj(of, ob, vg, hy, om, x, gate, gla_g, hy_g, mla_g, w_out, ln_g, ln_b, alpha):
    B, L, D = x.shape
    tm = min(ROW_TILE, L)
    VD = GLA_HEADS * GLA_DV
    MD = MLA_HEADS * MLA_V
    row = lambda w: pl.BlockSpec((1, tm, w), lambda b, i: (b, i, 0))
    return pl.pallas_call(
        functools.partial(_outproj_body, alpha=alpha),
        grid=(B, L // tm),
        in_specs=[row(VD), row(VD), pl.BlockSpec((1, tm, VD), lambda b, i: (b, i, 1)), row(HY_CH), row(MD), row(D),
                  pl.BlockSpec((1, 1, D), lambda b, i: (b, 0, 0)), _full((1, VD)), _full((1, HY_CH)), _full((1, MD)),
                  _full(w_out.shape), _full((1, D)), _full((1, D))],
        out_specs=row(D),
        out_shape=jax.ShapeDtypeStruct((B, L, D), F32),
        compiler_params=_cp("parallel", "parallel"),
        name="outproj",
    )(of, ob, vg, hy, om, x, gate, jnp.tile(gla_g, GLA_HEADS)[None, :], hy_g[None, :], mla_g[None, :],
      w_out.astype(BF16), ln_g[None, :], ln_b[None, :])


def _ffn_body(x_ref, sh_ref, sc_ref, gate_ref, w1_ref, w3_ref, w2_ref, lg_ref, lb_ref, o_ref, h_ref, acc_ref, *, alpha):
    j = pl.program_id(2)

    @pl.when(j == 0)
    def _():
        h_ref[...] = (x_ref[0] * (1.0 + sc_ref[0]) + sh_ref[0]).astype(BF16)
        acc_ref[...] = jnp.zeros_like(acc_ref)

    h = h_ref[...]
    a = jnp.dot(h, w1_ref[...], preferred_element_type=F32)
    b = jnp.dot(h, w3_ref[...], preferred_element_type=F32)
    t = (a * jax.nn.sigmoid(a) * b).astype(BF16)
    acc_ref[...] += jnp.dot(t, w2_ref[...], preferred_element_type=F32)

    @pl.when(j == pl.num_programs(2) - 1)
    def _():
        o_ref[0] = _layernorm_rows(alpha * x_ref[0] + gate_ref[0] * acc_ref[...], lg_ref[...], lb_ref[...])


def _ffn(x, shift, scale, gate, w1, w3, w2, ln_g, ln_b, alpha):
    B, L, D = x.shape
    F = w1.shape[1]
    tf = next(t for t in (1408, 512, 256, 128, F) if F % t == 0)
    tm = min(ROW_TILE if tf > 512 else WIDE_ROW_TILE, L)
    row = pl.BlockSpec((1, tm, D), lambda b, i, j: (b, i, 0))
    vec = pl.BlockSpec((1, 1, D), lambda b, i, j: (b, 0, 0))
    return pl.pallas_call(
        functools.partial(_ffn_body, alpha=alpha),
        grid=(B, L // tm, F // tf),
        in_specs=[row, vec, vec, vec,
                  pl.BlockSpec((D, tf), lambda b, i, j: (0, j)), pl.BlockSpec((D, tf), lambda b, i, j: (0, j)),
                  pl.BlockSpec((tf, D), lambda b, i, j: (j, 0)), _full((1, D)), _full((1, D))],
        out_specs=row,
        out_shape=jax.ShapeDtypeStruct((B, L, D), F32),
        scratch_shapes=[pltpu.VMEM((tm, D), BF16), pltpu.VMEM((tm, D), F32)],
        compiler_params=_cp("parallel", "parallel", "arbitrary"),
        name="ffn",
    )(x, shift, scale, gate, w1.astype(BF16), w3.astype(BF16), w2.astype(BF16), ln_g[None, :], ln_b[None, :])


MOE_TOKENS = 2048
MOE_ROWS = 256
RANK_CHUNK = 256


def _router_body(x_ref, sh_ref, sc_ref, wr_ref, h_ref, g_ref, rk_ref, rkt_ref, cnt_ref):
    h = x_ref[0] * (1.0 + sc_ref[0]) + sh_ref[0]
    h_ref[0] = h.astype(BF16)
    logits = jnp.dot(h, wr_ref[...], precision=HI, preferred_element_type=F32)
    lane = lax.broadcasted_iota(jnp.int32, logits.shape, 1).astype(F32)
    logits = jnp.where(lane < N_EXPERTS, logits, -jnp.inf)
    m1 = jnp.max(logits, axis=1, keepdims=True)
    i1 = jnp.min(jnp.where(logits == m1, lane, float(LANES)), axis=1, keepdims=True)
    rest = jnp.where(lane == i1, -jnp.inf, logits)
    m2 = jnp.max(rest, axis=1, keepdims=True)
    i2 = jnp.min(jnp.where(rest == m2, lane, float(LANES)), axis=1, keepdims=True)
    e2 = jnp.exp(m2 - m1)
    w1 = 1.0 / (1.0 + e2)
    w2 = e2 / (1.0 + e2)
    g_ref[0] = jnp.where(lane == i1, w1, 0.0) + jnp.where(lane == i2, w2, 0.0)
    sel = jnp.logical_or(lane == i1, lane == i2)
    self_ = sel.astype(F32)
    tm = h.shape[0]
    C = min(RANK_CHUNK, tm)
    r = lax.broadcasted_iota(jnp.int32, (C, C), 0)
    c = lax.broadcasted_iota(jnp.int32, (C, C), 1)
    tri = (c < r).astype(BF16)
    carry = jnp.zeros((1, LANES), F32)
    parts = []
    for k in range(tm // C):
        sk = self_[k * C:(k + 1) * C]
        parts.append(jnp.dot(tri, sk.astype(BF16), preferred_element_type=F32) + carry)
        carry = carry + jnp.sum(sk, axis=0, keepdims=True)
    rank = jnp.where(sel, jnp.concatenate(parts, axis=0), -1.0)
    rk_ref[0] = rank
    rkt_ref[0] = rank.T[:8]
    cnt_ref[0, 0] = carry


def _router(x, shift, scale, w_router):
    B, L, D = x.shape
    tm = min(MOE_TOKENS, L)
    nt = L // tm
    wr = jnp.pad(w_router, ((0, 0), (0, LANES - N_EXPERTS)))
    vec = pl.BlockSpec((1, 1, D), lambda b, i: (b, 0, 0))
    col = pl.BlockSpec((1, tm, LANES), lambda b, i: (b, i, 0))
    return pl.pallas_call(
        _router_body,
        grid=(B, nt),
        in_specs=[pl.BlockSpec((1, tm, D), lambda b, i: (b, i, 0)), vec, vec, _full((D, LANES))],
        out_specs=[pl.BlockSpec((1, tm, D), lambda b, i: (b, i, 0)), col, col,
                   pl.BlockSpec((1, 8, tm), lambda b, i: (b, 0, i)), pl.BlockSpec((1, 1, 1, LANES), lambda b, i: (b, i, 0, 0))],
        out_shape=[jax.ShapeDtypeStruct((B, L, D), BF16), jax.ShapeDtypeStruct((B, L, LANES), F32),
                   jax.ShapeDtypeStruct((B, L, LANES), F32), jax.ShapeDtypeStruct((B, 8, L), F32),
                   jax.ShapeDtypeStruct((B, nt, 1, LANES), F32)],
        compiler_params=_cp("parallel", "parallel"),
        name="moe_router",
    )(x, shift, scale, wr)


def _moe_body(cnt_ref, h_ref, g_ref, rk_ref, rkt_ref, w1_ref, w3_ref, w2_ref, o_ref, xg_ref, y_ref, *, M, P):
    b, i, e, j = pl.program_id(0), pl.program_id(1), pl.program_id(2), pl.program_id(3)
    nt, ne, nj = pl.num_programs(1), pl.num_programs(2), pl.num_programs(3)
    tm = h_ref.shape[1]
    cnt = cnt_ref[(b * nt + i) * ne + e]
    n_ch = lax.div(cnt + (M - 1), M)

    @pl.when(jnp.logical_and(e == 0, j == 0))
    def _():
        o_ref[...] = jnp.zeros_like(o_ref)

    @pl.when(j == 0)
    def _():
        rkt = rkt_ref[0, pl.ds(e, 1), :]

        def gather(c, carry):
            r0 = pl.multiple_of(c * M, 16)
            rows = (lax.broadcasted_iota(jnp.int32, (M, 1), 0) + c * M).astype(F32)
            onehot = (rkt == rows).astype(BF16)
            xg_ref[pl.ds(r0, M), :] = jnp.dot(onehot, h_ref[0], preferred_element_type=F32).astype(BF16)
            return carry

        lax.fori_loop(0, n_ch, gather, 0)

    def expert(chunks):
        r0 = [pl.multiple_of(c * M, 16) for c in chunks]
        xg = [xg_ref[pl.ds(r, M), :] for r in r0]
        a = [jnp.dot(x_, w1_ref[0], preferred_element_type=F32) for x_ in xg]
        g = [jnp.dot(x_, w3_ref[0], preferred_element_type=F32) for x_ in xg]
        t = [(a_ * jax.nn.sigmoid(a_) * g_).astype(BF16) for a_, g_ in zip(a, g)]
        yv = [jnp.dot(t_, w2_ref[0], preferred_element_type=F32) for t_ in t]

        @pl.when(j == 0)
        def _():
            for r, y_ in zip(r0, yv):
                y_ref[pl.ds(r, M), :] = y_

        @pl.when(j > 0)
        def _():
            for r, y_ in zip(r0, yv):
                y_ref[pl.ds(r, M), :] += y_

    def expert_pair(c2, carry):
        expert([2 * c2, 2 * c2 + 1])
        return carry

    lax.fori_loop(0, lax.div(n_ch, 2), expert_pair, 0)

    @pl.when(lax.rem(n_ch, 2) == 1)
    def _():
        expert([n_ch - 1])

    @pl.when(j == nj - 1)
    def _():
        for p in range(tm // P):
            lane = lax.broadcasted_iota(jnp.int32, (P, LANES), 1)
            rke = jnp.sum(jnp.where(lane == e, rk_ref[0, p * P:(p + 1) * P, :], 0.0), axis=1, keepdims=True)
            ge = jnp.sum(jnp.where(lane == e, g_ref[0, p * P:(p + 1) * P, :], 0.0), axis=1, keepdims=True)

            def scatter(c, carry):
                r0 = pl.multiple_of(c * M, 16)
                cols = (lax.broadcasted_iota(jnp.int32, (1, M), 1) + c * M).astype(F32)
                onehot = (rke == cols).astype(BF16)
                yb = y_ref[pl.ds(r0, M), :].astype(BF16)
                o_ref[0, p * P:(p + 1) * P, :] += ge * jnp.dot(onehot, yb, preferred_element_type=F32)
                return carry

            lax.fori_loop(0, n_ch, scatter, 0)


def _res_ln_body(x_ref, y_ref, gate_ref, lg_ref, lb_ref, o_ref, *, alpha):
    o_ref[0] = _layernorm_rows(alpha * x_ref[0] + gate_ref[0] * y_ref[0], lg_ref[...], lb_ref[...])


def _res_ln(x, y, gate, ln_g, ln_b, alpha):
    B, L, D = x.shape
    tm = min(WIDE_ROW_TILE, L)
    row = pl.BlockSpec((1, tm, D), lambda b, i: (b, i, 0))
    return pl.pallas_call(
        functools.partial(_res_ln_body, alpha=alpha),
        grid=(B, L // tm),
        in_specs=[row, row, pl.BlockSpec((1, 1, D), lambda b, i: (b, 0, 0)), _full((1, D)), _full((1, D))],
        out_specs=row,
        out_shape=jax.ShapeDtypeStruct((B, L, D), F32),
        compiler_params=_cp("parallel", "parallel"),
        name="res_ln",
    )(x, y, gate, ln_g[None, :], ln_b[None, :])


def _moe(x, shift, scale, gate, w_router, w1, w3, w2, ln_g, ln_b, alpha):
    B, L, D = x.shape
    E, _, F = w1.shape
    hb, gts, rk, rkt, cnt = _router(x, shift, scale, w_router)
    tm = min(MOE_TOKENS, L)
    nt = L // tm
    M = MOE_ROWS
    rows_max = -(-tm // M) * M
    tf = next(t for t in (896, 512, 256, 128, F) if F % t == 0)
    counts = cnt[:, :, 0, :E].astype(jnp.int32).reshape(-1)
    row = lambda w: pl.BlockSpec((1, tm, w), lambda b, i, e, j, c: (b, i, 0))
    y = pl.pallas_call(
        functools.partial(_moe_body, M=M, P=min(512, tm)),
        grid_spec=pltpu.PrefetchScalarGridSpec(
            num_scalar_prefetch=1,
            grid=(B, nt, E, F // tf),
            in_specs=[row(D), row(LANES), row(LANES), pl.BlockSpec((1, 8, tm), lambda b, i, e, j, c: (b, 0, i)),
                      pl.BlockSpec((1, D, tf), lambda b, i, e, j, c: (e, 0, j)),
                      pl.BlockSpec((1, D, tf), lambda b, i, e, j, c: (e, 0, j)),
                      pl.BlockSpec((1, tf, D), lambda b, i, e, j, c: (e, j, 0))],
            out_specs=row(D),
            scratch_shapes=[pltpu.VMEM((rows_max, D), BF16), pltpu.VMEM((rows_max, D), F32)],
        ),
        out_shape=jax.ShapeDtypeStruct((B, L, D), F32),
        compiler_params=_cp("parallel", "parallel", "arbitrary", "arbitrary"),
        name="moe",
    )(counts, hb, gts, rk, rkt, w1.astype(BF16), w3.astype(BF16), w2.astype(BF16))
    return _res_ln(x, y, gate, ln_g, ln_b, alpha)


def _mod_body(c_ref, w_ref, b_ref, o_ref):
    c = c_ref[...]
    s = c * jax.nn.sigmoid(c)
    o_ref[...] = jnp.dot(s, w_ref[...], precision=HI, preferred_element_type=F32) + b_ref[...]


def _modulation(cc, w_mod, b_mod):
    R, D = cc.shape
    N = w_mod.shape[1]
    tn = 1024
    return pl.pallas_call(
        _mod_body,
        grid=(N // tn,),
        in_specs=[_full((R, D)), pl.BlockSpec((D, tn), lambda j: (0, j)), pl.BlockSpec((1, tn), lambda j: (0, j))],
        out_specs=pl.BlockSpec((R, tn), lambda j: (0, j)),
        out_shape=jax.ShapeDtypeStruct((R, N), F32),
        compiler_params=_cp("parallel"),
        name="modulation",
    )(cc, w_mod, b_mod[None, :])


def _streams(x, c, ctx, c_ctx, w_mod, b_mod, w_in, gla_w_gate, gla_b_gate, gla_norm_g, hy_conv_w, hy_conv_b, hy_f_w1, hy_f_b1, hy_f_freq1, hy_f_w2, hy_f_b2, hy_f_freq2, hy_f_w3, hy_f_b3, hy_skip, hy_norm_g, mla_q_norm_g, mla_w_uq, mla_kv_norm_g, mla_w_ukv, mla_norm_g, w_out, ln_g, ln_b, ffn_w1, ffn_w3, ffn_w2, moe_router, moe_w1, moe_w3, moe_w2):
    B, L, D = x.shape
    Lc = ctx.shape[1]
    depth = w_mod.shape[0]
    alpha = (2.0 * depth) ** 0.25
    cc = jnp.zeros((8, D), F32).at[:B].set(c).at[B].set(c_ctx)
    cos, sin = _rope_tables(L, True)
    cos_c, sin_c = _rope_tables(Lc, False)
    cos_all, sin_all = jnp.concatenate([cos_c, cos], axis=0), jnp.concatenate([sin_c, sin], axis=0)
    KD, VD = GLA_HEADS * GLA_DK, GLA_HEADS * GLA_DV
    xc = ctx
    for l in range(depth):
        need_ctx = l < depth - 1
        mods = _modulation(cc, w_mod[l], b_mod[l])
        m = [mods[:B, k * D:(k + 1) * D][:, None, :] for k in range(6)]
        mc = [jnp.broadcast_to(mods[B, k * D:(k + 1) * D][None, None, :], (B, 1, D)) for k in range(6)]
        w_arr = _arrange_w_in(w_in[l])
        wg, bg = _arrange_gate(gla_w_gate[l], gla_b_gate[l])
        filt = (hy_f_w1[l], hy_f_b1[l], hy_f_freq1[l], hy_f_w2[l], hy_f_b2[l], hy_f_freq2[l], hy_f_w3[l], hy_f_b3[l])
        wq = _arrange_wq(mla_w_uq[l])
        wk, wv = _arrange_wkv(mla_w_ukv[l])

        hyu, qk, vg, alr, cq, ckvr = _inproj(x, m[0], m[1], w_arr)
        hyu_c, qk_c, vg_c, alr_c, cq_c, ckvr_c = _inproj(xc, mc[0], mc[1], w_arr)

        of_c, ob_c, s_c = _gla(qk_c, vg_c, alr_c, wg, bg, jnp.zeros((B, 2, KD, VD), F32))
        of, ob, _ = _gla(qk, vg, alr, wg, bg, s_c)
        hy = _hyena(hyu, hy_conv_w[l], hy_conv_b[l], filt, hy_skip[l])
        k_all, v_all = _kvproj(jnp.concatenate([ckvr_c, ckvr], axis=1), mla_kv_norm_g[l], wk, wv, cos_all, sin_all)
        k_c, v_c = k_all[:, :Lc], v_all[:, :Lc]
        q_m = _qproj(cq, mla_q_norm_g[l], wq, cos, sin)
        om = _flash(q_m, k_all, v_all)

        x = _outproj(of, ob, vg, hy, om, x, m[2], gla_norm_g[l], hy_norm_g[l], mla_norm_g[l], w_out[l],
                     ln_g[l, 0], ln_b[l, 0], alpha)
        if need_ctx:
            hy_c = _hyena_ctx(hyu_c, hy_conv_w[l], hy_conv_b[l], filt, hy_skip[l])
            q_c = _qproj(cq_c, mla_q_norm_g[l], wq, cos_c, sin_c)
            om_c = _flash(q_c, k_c, v_c)
            xc = _outproj(of_c, ob_c, vg_c, hy_c, om_c, xc, mc[2], gla_norm_g[l], hy_norm_g[l], mla_norm_g[l],
                          w_out[l], ln_g[l, 0], ln_b[l, 0], alpha)

        i = l // 2
        if l % 2 == 0:
            x = _ffn(x, m[3], m[4], m[5], ffn_w1[i], ffn_w3[i], ffn_w2[i], ln_g[l, 1], ln_b[l, 1], alpha)
            if need_ctx:
                xc = _ffn(xc, mc[3], mc[4], mc[5], ffn_w1[i], ffn_w3[i], ffn_w2[i], ln_g[l, 1], ln_b[l, 1], alpha)
        else:
            x = _moe(x, m[3], m[4], m[5], moe_router[i], moe_w1[i], moe_w3[i], moe_w2[i], ln_g[l, 1], ln_b[l, 1], alpha)
            if need_ctx:
                xc = _moe(xc, mc[3], mc[4], mc[5], moe_router[i], moe_w1[i], moe_w3[i], moe_w2[i], ln_g[l, 1],
                          ln_b[l, 1], alpha)
    return x, xc


def kernel(x, c, ctx, c_ctx, w_mod, b_mod, w_in, gla_w_gate, gla_b_gate, gla_norm_g, hy_conv_w, hy_conv_b, hy_f_w1, hy_f_b1, hy_f_freq1, hy_f_w2, hy_f_b2, hy_f_freq2, hy_f_w3, hy_f_b3, hy_skip, hy_norm_g, mla_q_norm_g, mla_w_uq, mla_kv_norm_g, mla_w_ukv, mla_norm_g, w_out, ln_g, ln_b, ffn_w1, ffn_w3, ffn_w2, moe_router, moe_w1, moe_w3, moe_w2):
    return _streams(x, c, ctx, c_ctx, w_mod, b_mod, w_in, gla_w_gate, gla_b_gate, gla_norm_g, hy_conv_w, hy_conv_b, hy_f_w1, hy_f_b1, hy_f_freq1, hy_f_w2, hy_f_b2, hy_f_freq2, hy_f_w3, hy_f_b3, hy_skip, hy_norm_g, mla_q_norm_g, mla_w_uq, mla_kv_norm_g, mla_w_ukv, mla_norm_g, w_out, ln_g, ln_b, ffn_w1, ffn_w3, ffn_w2, moe_router, moe_w1, moe_w3, moe_w2)[0]
```

```python
import functools
import math

import numpy as np
import jax
import jax.numpy as jnp
from jax import lax
from jax.experimental import pallas as pl
from jax.experimental.pallas import tpu as pltpu

F32 = jnp.float32
BF16 = jnp.bfloat16
HI = lax.Precision.HIGHEST

GRID_W = 64
GLA_HEADS, GLA_DK, GLA_DV, GLA_RANK, GLA_TAU = 4, 32, 64, 16, 16.0
HY_CH, HY_EMB = 256, 33
HY_DECAY_TARGET, HY_FAST_DECAY, HY_SLOW_DECAY = 1e-2, 0.3, 1.5
MLA_HEADS, MLA_Q_RANK, MLA_KV_RANK, MLA_NOPE, MLA_ROPE, MLA_V = 8, 256, 128, 64, 32, 64
MLA_SCALE = (MLA_NOPE + MLA_ROPE) ** -0.5
ROPE_BASE = 10000.0
N_EXPERTS = 8
IN_SPLITS = (128, 128, 256, 256, 32, 768, 256, 128, 32)

LANES = 128
SUBLANES = 8
VMEM_LIMIT = 56 * 1024 * 1024

ROW_TILE = 512
WIDE_ROW_TILE = 1024
GLA_CHUNK = 128
DFT_N2 = 256


def _cp(*sem):
    return pltpu.CompilerParams(dimension_semantics=sem, vmem_limit_bytes=VMEM_LIMIT)


def _full(shape):
    n = len(shape)
    return pl.BlockSpec(shape, lambda *_: (0,) * n)


def _idiv(x, d):
    assert d & (d - 1) == 0
    return lax.shift_right_logical(x, int(math.log2(d)))


INPROJ_WIDTHS = (768, 256, 512, 128, 256, 256)


def _arrange_w_in(w):
    cuts = np.cumsum(IN_SPLITS)[:-1]
    qa, ka, va, ga, alr, hyu, cq, ckv, kr = jnp.split(w, [int(c) for c in cuts], axis=1)
    z96 = jnp.zeros((w.shape[0], 96), w.dtype)
    return jnp.concatenate([hyu, qa, ka, va, ga, alr, z96, cq, ckv, kr, z96], axis=1).astype(BF16)


def _inproj_body(x_ref, sh_ref, sc_ref, w_ref, *out_refs):
    h = x_ref[0] * (1.0 + sc_ref[0]) + sh_ref[0]
    acc = jnp.dot(h.astype(BF16), w_ref[...], preferred_element_type=F32)
    off = 0
    for r in out_refs:
        w = r.shape[-1]
        r[0] = acc[:, off:off + w].astype(r.dtype)
        off += w


def _inproj(x, shift, scale, w_arr):
    B, L, D = x.shape
    tm = min(ROW_TILE, L)
    n = w_arr.shape[1]
    row = lambda w: pl.BlockSpec((1, tm, w), lambda b, i: (b, i, 0))
    vec = pl.BlockSpec((1, 1, D), lambda b, i: (b, 0, 0))
    return pl.pallas_call(
        _inproj_body,
        grid=(B, L // tm),
        in_specs=[row(D), vec, vec, _full((D, n))],
        out_specs=[row(w) for w in INPROJ_WIDTHS],
        out_shape=[jax.ShapeDtypeStruct((B, L, w), BF16) for w in INPROJ_WIDTHS],
        compiler_params=_cp("parallel", "parallel"),
        name="inproj",
    )(x, shift, scale, w_arr)


def _log_sigmoid(z):
    return jnp.minimum(z, 0.0) - jnp.log1p(jnp.exp(-jnp.abs(z)))


def _gla_body(qkf_ref, vf_ref, af_ref, qkb_ref, vb_ref, ab_ref, wg_ref, bg_ref, s0_ref,
              of_ref, ob_ref, sout_ref, s_ref):
    i = pl.program_id(0)
    C = qkf_ref.shape[1]
    KD = GLA_HEADS * GLA_DK
    VD = GLA_HEADS * GLA_DV

    @pl.when(i == 0)
    def _():
        s_ref[...] = s0_ref[...]

    r = lax.broadcasted_iota(jnp.int32, (C, C), 0)
    c = lax.broadcasted_iota(jnp.int32, (C, C), 1)
    tris = ((c <= r).astype(F32), (c >= r).astype(F32))
    lane_k = _idiv(lax.broadcasted_iota(jnp.int32, (1, KD), 1), GLA_DK)
    lane_v = _idiv(lax.broadcasted_iota(jnp.int32, (1, VD), 1), GLA_DV)
    rk = _idiv(lax.broadcasted_iota(jnp.int32, (KD, VD), 0), GLA_DK)
    cv = _idiv(lax.broadcasted_iota(jnp.int32, (KD, VD), 1), GLA_DV)
    ones = jnp.ones((C, VD), F32)
    refs = ((qkf_ref, vf_ref, af_ref, of_ref), (qkb_ref, vb_ref, ab_ref, ob_ref))
    chains = [(b, d) for b in range(qkf_ref.shape[0]) for d in range(2)]

    z = [jnp.dot(refs[d][2][b].astype(F32), wg_ref[...], precision=HI, preferred_element_type=F32) + bg_ref[...]
         for b, d in chains]
    la = [_log_sigmoid(zz[:, d * KD:(d + 1) * KD]) / GLA_TAU for zz, (b, d) in zip(z, chains)]
    bb = [jnp.dot(tris[d], l_, precision=HI, preferred_element_type=F32) for l_, (b, d) in zip(la, chains)]
    tot_b = [lax.dot_general(l_, ones, (((0,), (0,)), ((), ())), precision=HI, preferred_element_type=F32) for l_ in la]
    qe, ke, kl, vb, s_old = [], [], [], [], []
    for n, (b, d) in enumerate(chains):
        qk = refs[d][0][b].astype(F32)
        q = qk[:, :KD] * (GLA_DK ** -0.5)
        k = qk[:, KD:]
        tot = jnp.sum(la[n], axis=0, keepdims=True)
        qe.append(q * jnp.exp(bb[n]))
        ke.append((k * jnp.exp(-bb[n])).astype(BF16))
        kl.append((k * jnp.exp(tot - bb[n])).astype(BF16))
        vb.append(refs[d][1][b].astype(BF16))
        s_old.append(s_ref[2 * b + d])
    o = [jnp.dot(qe[n].astype(BF16), s_old[n].astype(BF16), preferred_element_type=F32) for n in range(len(chains))]
    att = [[lax.dot_general(jnp.where(lane_k == h, qe[n], 0.0).astype(BF16), ke[n], (((1,), (1,)), ((), ())),
                            preferred_element_type=F32) for h in range(GLA_HEADS)] for n in range(len(chains))]
    kv = [lax.dot_general(kl[n], vb[n], (((0,), (0,)), ((), ())), preferred_element_type=F32) for n in range(len(chains))]
    for n, (b, d) in enumerate(chains):
        on = o[n]
        for h in range(GLA_HEADS):
            oh = jnp.dot((att[n][h] * tris[d]).astype(BF16), vb[n], preferred_element_type=F32)
            on = on + jnp.where(lane_v == h, oh, 0.0)
        refs[d][3][b] = on.astype(refs[d][3].dtype)
        s_ref[2 * b + d] = jnp.exp(tot_b[n]) * s_old[n] + jnp.where(rk == cv, kv[n], 0.0)

    @pl.when(i == pl.num_programs(0) - 1)
    def _():
        sout_ref[...] = s_ref[...]


def _gla(qk, vg, alr, wg, bg, s0):
    B, L, _ = qk.shape
    C = min(GLA_CHUNK, L)
    n = L // C
    KD, VD = GLA_HEADS * GLA_DK, GLA_HEADS * GLA_DV
    fwd = lambda w: pl.BlockSpec((B, C, w), lambda i: (0, i, 0))
    bwd = lambda w: pl.BlockSpec((B, C, w), lambda i: (0, n - 1 - i, 0))
    st = _full((2 * B, KD, VD))
    of, ob, s_out = pl.pallas_call(
        _gla_body,
        grid=(n,),
        in_specs=[fwd(2 * KD), fwd(VD), fwd(LANES), bwd(2 * KD), bwd(VD), bwd(LANES),
                  _full((LANES, 2 * KD)), _full((1, 2 * KD)), st],
        out_specs=[fwd(VD), bwd(VD), st],
        out_shape=[jax.ShapeDtypeStruct((B, L, VD), BF16), jax.ShapeDtypeStruct((B, L, VD), BF16),
                   jax.ShapeDtypeStruct((2 * B, KD, VD), F32)],
        scratch_shapes=[pltpu.VMEM((2 * B, KD, VD), F32)],
        compiler_params=_cp("arbitrary"),
        name="gla",
    )(qk, vg, alr, qk, vg, alr, wg, bg, s0.reshape(2 * B, KD, VD))
    return of, ob, s_out.reshape(B, 2, KD, VD)


def _arrange_gate(w_gate, b_gate):
    KD = GLA_HEADS * GLA_DK
    wg = jnp.zeros((LANES, 2 * KD), F32)
    wg = wg.at[:GLA_RANK, :KD].set(w_gate[0]).at[GLA_RANK:2 * GLA_RANK, KD:].set(w_gate[1])
    return wg, jnp.concatenate([b_gate[0], b_gate[1]])[None, :]


def _shortconv_body(x_ref, p_ref, n_ref, w_ref, b_ref, v_ref, x1_ref, x2_ref):
    i = pl.program_id(1)
    last = pl.num_programs(1) - 1
    x = x_ref[0].astype(F32)
    tm = x.shape[0]
    hb = p_ref.shape[1]
    prev = jnp.where(i > 0, p_ref[0].astype(F32)[hb - 1:hb, :], 0.0)
    nxt = jnp.where(i < last, n_ref[0].astype(F32)[0:1, :], 0.0)
    rid = lax.broadcasted_iota(jnp.int32, x.shape, 0)
    dn = jnp.where(rid == 0, prev, pltpu.roll(x, 1, 0))
    up = jnp.where(rid == tm - 1, nxt, pltpu.roll(x, tm - 1, 0))
    w = w_ref[...]
    y = b_ref[...] + dn * w[0:1] + x * w[1:2] + up * w[2:3]
    v_ref[0] = y[:, :HY_CH]
    x1_ref[0] = y[:, HY_CH:2 * HY_CH]
    x2_ref[0] = y[:, 2 * HY_CH:]


def _shortconv(u, w, b):
    B, L, W = u.shape
    tm = min(ROW_TILE, L)
    hb = 2 * SUBLANES
    nb = tm // hb
    row = pl.BlockSpec((1, tm, W), lambda b_, i: (b_, i, 0))
    prev = pl.BlockSpec((1, hb, W), lambda b_, i: (b_, jnp.maximum(i * nb - 1, 0), 0))
    nxt = pl.BlockSpec((1, hb, W), lambda b_, i: (b_, jnp.minimum((i + 1) * nb, L // hb - 1), 0))
    o = pl.BlockSpec((1, tm, HY_CH), lambda b_, i: (b_, i, 0))
    return pl.pallas_call(
        _shortconv_body,
        grid=(B, L // tm),
        in_specs=[row, prev, nxt, _full((3, W)), _full((1, W))],
        out_specs=[o, o, o],
        out_shape=[jax.ShapeDtypeStruct((B, L, HY_CH), F32)] * 3,
        compiler_params=_cp("parallel", "parallel"),
        name="shortconv",
    )(u, u, u, w, b[None, :])


def _filter_feats(L):
    pos = np.arange(L, dtype=np.float64)
    t = pos / (L - 1)
    bands = (HY_EMB - 1) // 2
    freqs = np.linspace(1e-4, bands - 1, bands)
    ang = (2.0 * math.pi * pos / L)[:, None] * freqs
    z = jnp.asarray(np.concatenate([t[:, None], np.cos(ang), -np.sin(ang)], axis=-1), dtype=F32)
    z = jnp.pad(z, ((0, 0), (0, LANES - HY_EMB)))
    deltas = np.abs(np.linspace(math.log(HY_DECAY_TARGET) / HY_SLOW_DECAY,
                                math.log(HY_DECAY_TARGET) / HY_FAST_DECAY, HY_CH))
    return z, jnp.asarray(np.tile(deltas, 4)[None, :], dtype=F32)


def _filter_body(z_ref, w1_ref, b1_ref, f1_ref, w2_ref, b2_ref, f2_ref, w3_ref, b3_ref, dl_ref,
                 h_ref, ss_ref, *, L):
    i = pl.program_id(0)
    z = z_ref[...]
    tm = z.shape[0]
    hid = jnp.sin(f1_ref[...] * (jnp.dot(z, w1_ref[...], precision=HI, preferred_element_type=F32) + b1_ref[...]))
    hid = jnp.sin(f2_ref[...] * (jnp.dot(hid, w2_ref[...], precision=HI, preferred_element_type=F32) + b2_ref[...]))
    h = jnp.dot(hid, w3_ref[...], precision=HI, preferred_element_type=F32) + b3_ref[...]
    pos = (lax.broadcasted_iota(jnp.int32, (tm, 1), 0) + i * tm).astype(F32)
    t = pos / (L - 1)
    h = h * jnp.exp(-t * dl_ref[...])

    @pl.when(i == 0)
    def _():
        ss_ref[...] = jnp.zeros_like(ss_ref)

    ss_ref[...] += jnp.sum(h * h, axis=0, keepdims=True)
    col = lax.broadcasted_iota(jnp.int32, h.shape, 1)
    is_bwd = (_idiv(col, HY_CH) & 1) == 1
    h_ref[...] = jnp.where(jnp.logical_and(is_bwd, pos == 0.0), 0.0, h)


def _filters(L, fw1, fb1, ff1, fw2, fb2, ff2, fw3, fb3):
    z, dl = _filter_feats(L)
    tm = min(WIDE_ROW_TILE, L)
    Hf = fw2.shape[0]
    w1 = jnp.pad(fw1, ((0, LANES - HY_EMB), (0, 0)))
    NC = fw3.shape[1]
    return pl.pallas_call(
        functools.partial(_filter_body, L=L),
        grid=(L // tm,),
        in_specs=[pl.BlockSpec((tm, LANES), lambda i: (i, 0)), _full((LANES, Hf)), _full((1, Hf)), _full((1, Hf)),
                  _full((Hf, Hf)), _full((1, Hf)), _full((1, Hf)), _full((Hf, NC)), _full((1, NC)), _full((1, NC))],
        out_specs=[pl.BlockSpec((tm, NC), lambda i: (i, 0)), _full((1, NC))],
        out_shape=[jax.ShapeDtypeStruct((L, NC), F32), jax.ShapeDtypeStruct((1, NC), F32)],
        compiler_params=_cp("arbitrary"),
        name="hy_filters",
    )(z, w1, fb1[None], ff1[None], fw2, fb2[None], ff2[None], fw3, fb3[None], dl)


def _dft_consts(L):
    N = 2 * L
    N2 = DFT_N2
    N1 = N // N2
    half = N1 // 2
    k1 = np.arange(N1)[:, None].astype(np.float64)
    n1 = np.arange(N1)[None, :].astype(np.float64)
    a1 = 2.0 * np.pi * k1 * n1 / N1
    f1r, f1i = np.cos(a1), -np.sin(a1)
    fa = np.concatenate([f1r[:, :half], f1i[:, :half]], axis=0)
    fb = np.concatenate([f1r[:half, :], f1i[:half, :]], axis=1) / N
    k2 = np.arange(N2)[:, None].astype(np.float64)
    n2 = np.arange(N2)[None, :].astype(np.float64)
    a2 = 2.0 * np.pi * k2 * n2 / N2
    f2r, f2i = np.cos(a2), -np.sin(a2)
    g = np.block([[f2r, -f2i], [f2i, f2r]])
    gc = np.block([[f2r, f2i], [-f2i, f2r]])
    at = 2.0 * np.pi * (np.arange(N1)[:, None] * np.arange(N2)[None, :] % N) / N
    twr, twi = np.cos(at), -np.sin(at)
    c = lambda a: jnp.asarray(a, dtype=F32)
    bc = lambda a: jnp.broadcast_to(c(a)[:, :, None], (N1, N2, LANES))
    eye = np.eye(SUBLANES)
    return dict(N1=N1, N2=N2, half=half, fa=c(np.kron(fa, eye)), fb=c(np.kron(fb, eye)), g=c(g), gc=c(gc),
                twr=bc(twr), twi=bc(twi))


def _lanes(t, width):
    return jnp.concatenate([t] * (width // LANES), axis=-1)


def _dft1_body(f_ref, x_ref, o_ref):
    x = x_ref[0]
    x2 = x.reshape(x.shape[0] * SUBLANES, x.shape[2]).astype(BF16)
    y = jnp.dot(f_ref[...], x2, preferred_element_type=F32)
    o_ref[0] = y.reshape(o_ref.shape[1], SUBLANES, y.shape[1])


def _dft_stage1(fa, x):
    B, half, N2, W = x.shape
    R = fa.shape[0] // SUBLANES
    return pl.pallas_call(
        _dft1_body,
        grid=(B, N2 // SUBLANES),
        in_specs=[_full(fa.shape), pl.BlockSpec((1, half, SUBLANES, W), lambda b, j: (b, 0, j, 0))],
        out_specs=pl.BlockSpec((1, R, SUBLANES, W), lambda b, j: (b, 0, j, 0)),
        out_shape=jax.ShapeDtypeStruct((B, R, N2, W), F32),
        compiler_params=_cp("parallel", "parallel"),
        name="hy_dft1",
    )(fa.astype(BF16), x)


def _filter_spec_body(a_ref, twr_ref, twi_ref, g_ref, ss_ref, hf_ref):
    W = a_ref.shape[-1]
    ar, ai = a_ref[0, 0], a_ref[1, 0]
    twr, twi = _lanes(twr_ref[0], W), _lanes(twi_ref[0], W)
    xr = ar * twr - ai * twi
    xi = ar * twi + ai * twr
    z = jnp.dot(g_ref[...], jnp.concatenate([xr, xi], axis=0).astype(BF16), preferred_element_type=F32)
    n2 = z.shape[0] // 2
    zr, zi = z[:n2], z[n2:]
    ss = ss_ref[...]
    for o in range(2):
        f0, b0 = (2 * o) * HY_CH, (2 * o + 1) * HY_CH
        sc = lax.rsqrt(ss[:, f0:f0 + HY_CH] + ss[:, b0:b0 + HY_CH] + 1e-6)
        hf_ref[o, 0, 0] = ((zr[:, f0:f0 + HY_CH] + zr[:, b0:b0 + HY_CH]) * sc).astype(hf_ref.dtype)
        hf_ref[o, 0, 1] = ((zi[:, f0:f0 + HY_CH] - zi[:, b0:b0 + HY_CH]) * sc).astype(hf_ref.dtype)


def _filter_spectrum(h, ss, dc):
    L, NC = h.shape
    N1, N2, half = dc["N1"], dc["N2"], dc["half"]
    a = _dft_stage1(dc["fa"], h.reshape(1, half, N2, NC))
    a = a.reshape(2, N1, N2, NC)
    return pl.pallas_call(
        _filter_spec_body,
        grid=(N1,),
        in_specs=[pl.BlockSpec((2, 1, N2, NC), lambda k: (0, k, 0, 0)),
                  pl.BlockSpec((1, N2, LANES), lambda k: (k, 0, 0)), pl.BlockSpec((1, N2, LANES), lambda k: (k, 0, 0)),
                  _full((2 * N2, 2 * N2)), _full((1, NC))],
        out_specs=pl.BlockSpec((2, 1, 2, N2, HY_CH), lambda k: (0, k, 0, 0, 0)),
        out_shape=jax.ShapeDtypeStruct((2, N1, 2, N2, HY_CH), BF16),
        compiler_params=_cp("parallel"),
        name="hy_filter_spec",
    )(a, dc["twr"], dc["twi"], dc["g"].astype(BF16), ss)


SPEC_K1 = 4


def _spec_mul_body(a_ref, twr_ref, twi_ref, g_ref, gc_ref, hf_ref, o_ref):
    W = a_ref.shape[-1]
    ks = range(a_ref.shape[2])
    n2 = g_ref.shape[0] // 2
    x = []
    for k in ks:
        ar, ai = a_ref[0, 0, k], a_ref[0, 1, k]
        twr, twi = _lanes(twr_ref[k], W), _lanes(twi_ref[k], W)
        x.append(jnp.concatenate([ar * twr - ai * twi, ar * twi + ai * twr], axis=0).astype(BF16))
    z = [jnp.dot(g_ref[...], x_, preferred_element_type=F32) for x_ in x]
    y = []
    for k, z_ in zip(ks, z):
        zr, zi = z_[:n2], z_[n2:]
        hr, hi = hf_ref[0, k, 0].astype(F32), hf_ref[0, k, 1].astype(F32)
        y.append(jnp.concatenate([zr * hr - zi * hi, zr * hi + zi * hr], axis=0).astype(BF16))
    b = [jnp.dot(gc_ref[...], y_, preferred_element_type=F32) for y_ in y]
    for k, b_ in zip(ks, b):
        br, bi = b_[:n2], b_[n2:]
        twr, twi = _lanes(twr_ref[k], W), _lanes(twi_ref[k], W)
        o_ref[0, 0, k] = br * twr + bi * twi
        o_ref[0, 1, k] = bi * twr - br * twi


def _spec_mul(a, hf, order, dc):
    B = a.shape[0]
    N1, N2 = dc["N1"], dc["N2"]
    C = a.shape[-1]
    kb = min(SPEC_K1, N1)
    blk = pl.BlockSpec((1, 2, kb, N2, C), lambda k, b: (b, 0, k, 0, 0))
    tw = pl.BlockSpec((kb, N2, LANES), lambda k, b: (k, 0, 0))
    return pl.pallas_call(
        _spec_mul_body,
        grid=(N1 // kb, B),
        in_specs=[blk, tw, tw, _full((2 * N2, 2 * N2)), _full((2 * N2, 2 * N2)),
                  pl.BlockSpec((1, kb, 2, N2, C), lambda k, b: (order, k, 0, 0, 0))],
        out_specs=blk,
        out_shape=jax.ShapeDtypeStruct(a.shape, F32),
        compiler_params=_cp("parallel", "parallel"),
        name="hy_spec_mul",
    )(a, dc["twr"], dc["twi"], dc["g"].astype(BF16), dc["gc"].astype(BF16), hf)


def _dft3_body(f_ref, b_ref, u_ref, gate_ref, skip_ref, o_ref):
    bm = b_ref[0]
    b2 = bm.reshape(bm.shape[0] * SUBLANES, bm.shape[2]).astype(BF16)
    y = jnp.dot(f_ref[...], b2, preferred_element_type=F32)
    rows, C = y.shape
    u = u_ref[0].reshape(rows, C)
    gate = gate_ref[0].reshape(rows, C)
    o_ref[0] = (gate * (y + u * skip_ref[...])).reshape(o_ref.shape[1], SUBLANES, C)


def _dft_stage3(fb, bm, u, gate, skip):
    B, R, N2, C = bm.shape
    half = u.shape[1]
    row = pl.BlockSpec((1, half, SUBLANES, C), lambda b, j: (b, 0, j, 0))
    return pl.pallas_call(
        _dft3_body,
        grid=(B, N2 // SUBLANES),
        in_specs=[_full(fb.shape), pl.BlockSpec((1, R, SUBLANES, C), lambda b, j: (b, 0, j, 0)), row, row, _full((1, C))],
        out_specs=row,
        out_shape=jax.ShapeDtypeStruct((B, half, N2, C), F32),
        compiler_params=_cp("parallel", "parallel"),
        name="hy_dft3",
    )(fb.astype(BF16), bm, u, gate, skip[None, :])


def _longconv_gated(u, gate, hf, order, skip, dc):
    B, L, C = u.shape
    N1, N2, half = dc["N1"], dc["N2"], dc["half"]
    u4 = u.reshape(B, half, N2, C)
    a = _dft_stage1(dc["fa"], u4).reshape(B, 2, N1, N2, C)
    bm = _spec_mul(a, hf, order, dc).reshape(B, 2 * N1, N2, C)
    return _dft_stage3(dc["fb"], bm, u4, gate.reshape(B, half, N2, C), skip).reshape(B, L, C)


def _hyena(hyu, conv_w, conv_b, filt, skip):
    B, L, _ = hyu.shape
    v, x1, x2 = _shortconv(hyu, conv_w, conv_b)
    h, ss = _filters(L, *filt)
    dc = _dft_consts(L)
    hf = _filter_spectrum(h, ss, dc)
    z1 = _longconv_gated(v, x1, hf, 0, skip[0], dc)
    return _longconv_gated(z1, x2, hf, 1, skip[1], dc)


def _hyena_ctx_body(v_ref, x1_ref, x2_ref, h_ref, ss_ref, skip_ref, fc_ref, gc_ref, o_ref):
    fc, gc = fc_ref[...], gc_ref[...]
    n = fc.shape[0] // 2
    ss = ss_ref[...]
    h = h_ref[...]

    def conv(u, o):
        f0, b0 = (2 * o) * HY_CH, (2 * o + 1) * HY_CH
        sc = lax.rsqrt(ss[:, f0:f0 + HY_CH] + ss[:, b0:b0 + HY_CH] + 1e-6)
        x = jnp.dot(fc, u, precision=HI, preferred_element_type=F32)
        hf = jnp.dot(fc, h[:, f0:f0 + HY_CH], precision=HI, preferred_element_type=F32)
        hb = jnp.dot(fc, h[:, b0:b0 + HY_CH], precision=HI, preferred_element_type=F32)
        hr = (hf[:n] + hb[:n]) * sc
        hi = (hf[n:] - hb[n:]) * sc
        yr = x[:n] * hr - x[n:] * hi
        yi = x[:n] * hi + x[n:] * hr
        y = jnp.dot(gc, jnp.concatenate([yr, yi], axis=0), precision=HI, preferred_element_type=F32)
        return y + u * skip_ref[o:o + 1, :]

    z1 = x1_ref[0] * conv(v_ref[0], 0)
    o_ref[0] = x2_ref[0] * conv(z1, 1)


def _hyena_ctx(hyu, conv_w, conv_b, filt, skip):
    B, L, _ = hyu.shape
    v, x1, x2 = _shortconv(hyu, conv_w, conv_b)
    h, ss = _filters(L, *filt)
    N = 2 * L
    ang = 2.0 * np.pi * (np.arange(N)[:, None] * np.arange(L)[None, :] % N) / N
    fr, fi = np.cos(ang), -np.sin(ang)
    fc = jnp.asarray(np.concatenate([fr, fi], axis=0), dtype=F32)
    gc = jnp.asarray(np.concatenate([fr.T, fi.T], axis=1) / N, dtype=F32)
    row = pl.BlockSpec((1, L, HY_CH), lambda b: (b, 0, 0))
    return pl.pallas_call(
        _hyena_ctx_body,
        grid=(B,),
        in_specs=[row, row, row, _full(h.shape), _full(ss.shape), _full(skip.shape), _full(fc.shape), _full(gc.shape)],
        out_specs=row,
        out_shape=jax.ShapeDtypeStruct((B, L, HY_CH), F32),
        compiler_params=_cp("parallel"),
        name="hyena_ctx",
    )(v, x1, x2, h, ss, skip, fc, gc)


HEAD_PAD = 128


def _rope_swap(w):
    a, b, c, d = w[..., 0:8], w[..., 8:16], w[..., 16:24], w[..., 24:32]
    return jnp.concatenate([-b, a, -d, c], axis=-1)


def _arrange_wq(w_uq):
    R = w_uq.shape[0]
    w = w_uq.reshape(R, MLA_HEADS, MLA_NOPE + MLA_ROPE)
    rope = w[..., MLA_NOPE:]
    out = jnp.concatenate([w[..., :MLA_NOPE], rope, _rope_swap(rope)], axis=-1)
    return out.reshape(R, MLA_HEADS * HEAD_PAD).astype(BF16)


def _arrange_wkv(w_ukv):
    R = w_ukv.shape[0]
    w = w_ukv.reshape(R, MLA_HEADS, MLA_NOPE + MLA_V)
    wk = jnp.concatenate([w[..., :MLA_NOPE], jnp.zeros((R, MLA_HEADS, HEAD_PAD - MLA_NOPE), w.dtype)], axis=-1)
    wv = w[..., MLA_NOPE:]
    return wk.reshape(R, MLA_HEADS * HEAD_PAD).astype(BF16), wv.reshape(R, MLA_HEADS * MLA_V).astype(BF16)


def _kr_place():
    e = np.zeros((LANES, MLA_HEADS * HEAD_PAD), np.float32)
    es = np.zeros((LANES, MLA_HEADS * HEAD_PAD), np.float32)
    for h in range(MLA_HEADS):
        base = h * HEAD_PAD + MLA_NOPE
        for j in range(MLA_ROPE):
            e[j, base + j] = 1.0
            blk, r = divmod(j, 16)
            if r < 8:
                es[16 * blk + r + 8, base + j] = -1.0
            else:
                es[16 * blk + r - 8, base + j] = 1.0
    return jnp.asarray(e).astype(BF16), jnp.asarray(es).astype(BF16)


def _rope_tables(L, rope):
    if rope:
        t = np.arange(L)
        row, col = (t // GRID_W).astype(np.float64), (t % GRID_W).astype(np.float64)
        half = MLA_ROPE // 2
        inv = ROPE_BASE ** (-np.arange(0, half, 2, dtype=np.float64) / half)
        ar, ac = row[:, None] * inv, col[:, None] * inv
        cos = jnp.asarray(np.concatenate([np.cos(ar), np.cos(ar), np.cos(ac), np.cos(ac)], axis=-1), dtype=F32)
        sin = jnp.asarray(np.concatenate([np.sin(ar), np.sin(ar), np.sin(ac), np.sin(ac)], axis=-1), dtype=F32)
    else:
        cos, sin = jnp.ones((L, MLA_ROPE), F32), jnp.zeros((L, MLA_ROPE), F32)
    return cos, sin


def _rms_rows(x, g, eps=1e-6):
    return x * lax.rsqrt(jnp.mean(x * x, axis=-1, keepdims=True) + eps) * g


def _qproj_body(cq_ref, g_ref, w_ref, t1_ref, t2_ref, q_ref):
    xn = _rms_rows(cq_ref[0].astype(F32), g_ref[...])
    acc = jnp.dot(xn.astype(BF16), w_ref[...], preferred_element_type=F32)
    W = acc.shape[1]
    t1, t2 = _lanes(t1_ref[...], W), _lanes(t2_ref[...], W)
    q_ref[0] = (acc * t1 + pltpu.roll(acc, W - MLA_ROPE, 1) * t2).astype(q_ref.dtype)


def _qproj(cq, g, wq, cos, sin):
    B, L, R = cq.shape
    tm = min(ROW_TILE, L)
    W = wq.shape[1]
    ones, zeros = jnp.ones((L, MLA_NOPE), F32), jnp.zeros((L, MLA_ROPE), F32)
    qs = MLA_SCALE * math.log2(math.e)
    t1 = jnp.concatenate([ones, cos, zeros], axis=-1) * qs
    t2 = jnp.concatenate([jnp.zeros((L, MLA_NOPE), F32), sin, zeros], axis=-1) * qs
    tab = pl.BlockSpec((tm, HEAD_PAD), lambda b, i: (i, 0))
    return pl.pallas_call(
        _qproj_body,
        grid=(B, L // tm),
        in_specs=[pl.BlockSpec((1, tm, R), lambda b, i: (b, i, 0)), _full((1, R)), _full((R, W)), tab, tab],
        out_specs=pl.BlockSpec((1, tm, W), lambda b, i: (b, i, 0)),
        out_shape=jax.ShapeDtypeStruct((B, L, W), BF16),
        compiler_params=_cp("parallel", "parallel"),
        name="mla_qproj",
    )(cq, g[None, :], wq, t1, t2)


def _kvproj_body(c_ref, g_ref, wk_ref, wv_ref, e_ref, es_ref, cos_ref, sin_ref, k_ref, v_ref):
    c = c_ref[0].astype(F32)
    R = MLA_KV_RANK
    xn = _rms_rows(c[:, :R], g_ref[...]).astype(BF16)
    kr = c[:, R:]
    acc = jnp.dot(xn, wk_ref[...], preferred_element_type=F32)
    acc += jnp.dot((kr * cos_ref[...]).astype(BF16), e_ref[...], preferred_element_type=F32)
    acc += jnp.dot((kr * sin_ref[...]).astype(BF16), es_ref[...], preferred_element_type=F32)
    k_ref[0] = acc.astype(k_ref.dtype)
    v_ref[0] = jnp.dot(xn, wv_ref[...], preferred_element_type=F32).astype(v_ref.dtype)


def _kvproj(ckvr, g, wk, wv, cos, sin):
    B, L, Wc = ckvr.shape
    tm = next(t for t in (1280, 512, 256, L) if L % t == 0)
    pad = jnp.zeros((L, LANES - MLA_ROPE), F32)
    cos_p, sin_p = jnp.concatenate([cos, pad], axis=-1), jnp.concatenate([sin, pad], axis=-1)
    e, es = _kr_place()
    tab = pl.BlockSpec((tm, LANES), lambda b, i: (i, 0))
    Wk, Wv = wk.shape[1], wv.shape[1]
    return pl.pallas_call(
        _kvproj_body,
        grid=(B, L // tm),
        in_specs=[pl.BlockSpec((1, tm, Wc), lambda b, i: (b, i, 0)), _full((1, MLA_KV_RANK)),
                  _full(wk.shape), _full(wv.shape), _full(e.shape), _full(es.shape), tab, tab],
        out_specs=[pl.BlockSpec((1, tm, Wk), lambda b, i: (b, i, 0)), pl.BlockSpec((1, tm, Wv), lambda b, i: (b, i, 0))],
        out_shape=[jax.ShapeDtypeStruct((B, L, Wk), BF16), jax.ShapeDtypeStruct((B, L, Wv), BF16)],
        compiler_params=_cp("parallel", "parallel"),
        name="mla_kvproj",
    )(ckvr, g[None, :], wk, wv, e, es, cos_p, sin_p)


FLASH_Q_TILE = 2048
FLASH_ROWS = 256
FLASH_KEYS = 256


def _flash_body(q_ref, k_ref, v_ref, o_ref, m_ref, l_ref, acc_ref, s_ref, *, R):
    j = pl.program_id(3)
    tq, tk = q_ref.shape[1], k_ref.shape[1]
    CK = FLASH_KEYS
    npc = CK // LANES

    @pl.when(j == 0)
    def _():
        m_ref[...] = jnp.full_like(m_ref, -jnp.inf)
        l_ref[...] = jnp.zeros_like(l_ref)
        acc_ref[...] = jnp.zeros_like(acc_ref)

    def pass1(a, r):
        lo, r0 = a * HEAD_PAD, r * R
        q = q_ref[0, r0:r0 + R, lo:lo + HEAD_PAD]
        mp = None
        for c in range(tk // CK):
            kc = k_ref[0, c * CK:(c + 1) * CK, lo:lo + HEAD_PAD]
            s = lax.dot_general(q, kc, (((1,), (1,)), ((), ())), preferred_element_type=F32)
            s_ref[r0:r0 + R, c * CK:(c + 1) * CK] = s
            for w in range(npc):
                pc = s[:, w * LANES:(w + 1) * LANES]
                mp = pc if mp is None else jnp.maximum(mp, pc)
        m_old = m_ref[a, r0:r0 + R, :]
        return m_old, jnp.maximum(m_old, jnp.max(mp, axis=1, keepdims=True))

    def pass2(a, r, m_old, m_new):
        r0 = r * R
        alpha = jnp.exp2(m_old - m_new)
        lp = jnp.zeros((R, LANES), F32)
        pv = jnp.zeros((R, 2 * MLA_V), F32)
        for c in range(tk // CK):
            s = s_ref[r0:r0 + R, c * CK:(c + 1) * CK]
            ps = [jnp.exp2(s[:, w * LANES:(w + 1) * LANES] - m_new) for w in range(npc)]
            for p_ in ps:
                lp = lp + p_
            p = jnp.concatenate(ps, axis=1).astype(BF16)
            pv = pv + jnp.dot(p, v_ref[0, c * CK:(c + 1) * CK, :], preferred_element_type=F32)
        l_ref[a, r0:r0 + R, :] = alpha * l_ref[a, r0:r0 + R, :] + jnp.sum(lp, axis=1, keepdims=True)
        acc_ref[a, r0:r0 + R, :] = alpha * acc_ref[a, r0:r0 + R, :] + pv
        m_ref[a, r0:r0 + R, :] = m_new

    assert tq // R >= 2
    blocks = [(a, r) for a in range(2) for r in range(tq // R)]
    pend = pass1(*blocks[0])
    for i, blk in enumerate(blocks):
        nxt = pass1(*blocks[i + 1]) if i + 1 < len(blocks) else None
        pass2(*blk, *pend)
        pend = nxt

    @pl.when(j == pl.num_programs(3) - 1)
    def _():
        lane = lax.broadcasted_iota(jnp.int32, acc_ref.shape[1:], 1)
        o_ref[0] = jnp.where(lane < MLA_V, acc_ref[0] / l_ref[0], acc_ref[1] / l_ref[1]).astype(o_ref.dtype)


def _flash_tiles(Lq, Lk):
    tq = min(FLASH_Q_TILE, Lq)
    tk = next(t for t in (3328, 1280, 1024, 512, 256, Lk) if Lk % t == 0)
    return tq, tk


def _flash(q, k, v):
    B, Lq, _ = q.shape
    Lk = k.shape[1]
    tq, tk = _flash_tiles(Lq, Lk)
    hp = MLA_HEADS // 2
    return pl.pallas_call(
        functools.partial(_flash_body, R=min(FLASH_ROWS, tq // 2)),
        grid=(B, hp, Lq // tq, Lk // tk),
        in_specs=[pl.BlockSpec((1, tq, 2 * HEAD_PAD), lambda b, h, i, j: (b, i, h)),
                  pl.BlockSpec((1, tk, 2 * HEAD_PAD), lambda b, h, i, j: (b, j, h)),
                  pl.BlockSpec((1, tk, 2 * MLA_V), lambda b, h, i, j: (b, j, h))],
        out_specs=pl.BlockSpec((1, tq, 2 * MLA_V), lambda b, h, i, j: (b, i, h)),
        out_shape=jax.ShapeDtypeStruct((B, Lq, MLA_HEADS * MLA_V), BF16),
        scratch_shapes=[pltpu.VMEM((2, tq, LANES), F32), pltpu.VMEM((2, tq, LANES), F32),
                        pltpu.VMEM((2, tq, 2 * MLA_V), F32), pltpu.VMEM((tq, tk), F32)],
        compiler_params=_cp("parallel", "parallel", "parallel", "arbitrary"),
        name="mla_flash",
    )(q, k, v)


def _layernorm_rows(x, g, b, eps=1e-5):
    mu = jnp.mean(x, axis=-1, keepdims=True)
    xc = x - mu
    var = jnp.mean(xc * xc, axis=-1, keepdims=True)
    return xc * lax.rsqrt(var + eps) * g + b


def _outproj_body(of_ref, ob_ref, g_ref, hy_ref, om_ref, x_ref, gate_ref, gg_ref, hg_ref, mg_ref,
                  w_ref, lg_ref, lb_ref, o_ref, *, alpha):
    VD = GLA_HEADS * GLA_DV
    o = of_ref[0].astype(F32) + ob_ref[0].astype(F32)
    r = _idiv(lax.broadcasted_iota(jnp.int32, (VD, VD), 0), GLA_DV)
    c = _idiv(lax.broadcasted_iota(jnp.int32, (VD, VD), 1), GLA_DV)
    grp = (r == c).astype(F32)
    ms = jnp.dot(o * o, grp, precision=HI, preferred_element_type=F32) * (1.0 / GLA_DV)
    g = g_ref[0].astype(F32)
    ya = o * lax.rsqrt(ms + 1e-6) * gg_ref[...] * (g * jax.nn.sigmoid(g))
    yb = _rms_rows(hy_ref[0], hg_ref[...])
    yc = _rms_rows(om_ref[0].astype(F32), mg_ref[...])
    acc = jnp.dot(ya.astype(BF16), w_ref[0:VD, :], preferred_element_type=F32)
    acc += jnp.dot(yb.astype(BF16), w_ref[VD:VD + HY_CH, :], preferred_element_type=F32)
    acc += jnp.dot(yc.astype(BF16), w_ref[VD + HY_CH:, :], preferred_element_type=F32)
    o_ref[0] = _layernorm_rows(alpha * x_ref[0] + gate_ref[0] * acc, lg_ref[...], lb_ref[...])


def _outproj(of, ob, vg, hy, om, x, gate, gla_g, hy_g, mla_g, w_out, ln_g, ln_b, alpha):
    B, L, D = x.shape
    tm = min(ROW_TILE, L)
    VD = GLA_HEADS * GLA_DV
    MD = MLA_HEADS * MLA_V
    row = lambda w: pl.BlockSpec((1, tm, w), lambda b, i: (b, i, 0))
    return pl.pallas_call(
        functools.partial(_outproj_body, alpha=alpha),
        grid=(B, L // tm),
        in_specs=[row(VD), row(VD), pl.BlockSpec((1, tm, VD), lambda b, i: (b, i, 1)), row(HY_CH), row(MD), row(D),
                  pl.BlockSpec((1, 1, D), lambda b, i: (b, 0, 0)), _full((1, VD)), _full((1, HY_CH)), _full((1, MD)),
                  _full(w_out.shape), _full((1, D)), _full((1, D))],
        out_specs=row(D),
        out_shape=jax.ShapeDtypeStruct((B, L, D), F32),
        compiler_params=_cp("parallel", "parallel"),
        name="outproj",
    )(of, ob, vg, hy, om, x, gate, jnp.tile(gla_g, GLA_HEADS)[None, :], hy_g[None, :], mla_g[None, :],
      w_out.astype(BF16), ln_g[None, :], ln_b[None, :])


def _ffn_body(x_ref, sh_ref, sc_ref, gate_ref, w1_ref, w3_ref, w2_ref, lg_ref, lb_ref, o_ref, h_ref, acc_ref, *, alpha):
    j = pl.program_id(2)

    @pl.when(j == 0)
    def _():
        h_ref[...] = (x_ref[0] * (1.0 + sc_ref[0]) + sh_ref[0]).astype(BF16)
        acc_ref[...] = jnp.zeros_like(acc_ref)

    h = h_ref[...]
    a = jnp.dot(h, w1_ref[...], preferred_element_type=F32)
    b = jnp.dot(h, w3_ref[...], preferred_element_type=F32)
    t = (a * jax.nn.sigmoid(a) * b).astype(BF16)
    acc_ref[...] += jnp.dot(t, w2_ref[...], preferred_element_type=F32)

    @pl.when(j == pl.num_programs(2) - 1)
    def _():
        o_ref[0] = _layernorm_rows(alpha * x_ref[0] + gate_ref[0] * acc_ref[...], lg_ref[...], lb_ref[...])


def _ffn(x, shift, scale, gate, w1, w3, w2, ln_g, ln_b, alpha):
    B, L, D = x.shape
    F = w1.shape[1]
    tf = next(t for t in (1408, 512, 256, 128, F) if F % t == 0)
    tm = min(ROW_TILE if tf > 512 else WIDE_ROW_TILE, L)
    row = pl.BlockSpec((1, tm, D), lambda b, i, j: (b, i, 0))
    vec = pl.BlockSpec((1, 1, D), lambda b, i, j: (b, 0, 0))
    return pl.pallas_call(
        functools.partial(_ffn_body, alpha=alpha),
        grid=(B, L // tm, F // tf),
        in_specs=[row, vec, vec, vec,
                  pl.BlockSpec((D, tf), lambda b, i, j: (0, j)), pl.BlockSpec((D, tf), lambda b, i, j: (0, j)),
                  pl.BlockSpec((tf, D), lambda b, i, j: (j, 0)), _full((1, D)), _full((1, D))],
        out_specs=row,
        out_shape=jax.ShapeDtypeStruct((B, L, D), F32),
        scratch_shapes=[pltpu.VMEM((tm, D), BF16), pltpu.VMEM((tm, D), F32)],
        compiler_params=_cp("parallel", "parallel", "arbitrary"),
        name="ffn",
    )(x, shift, scale, gate, w1.astype(BF16), w3.astype(BF16), w2.astype(BF16), ln_g[None, :], ln_b[None, :])


MOE_TOKENS = 2048
MOE_ROWS = 256
RANK_CHUNK = 256


def _router_body(x_ref, sh_ref, sc_ref, wr_ref, h_ref, g_ref, rk_ref, rkt_ref, cnt_ref):
    h = x_ref[0] * (1.0 + sc_ref[0]) + sh_ref[0]
    h_ref[0] = h.astype(BF16)
    logits = jnp.dot(h, wr_ref[...], precision=HI, preferred_element_type=F32)
    lane = lax.broadcasted_iota(jnp.int32, logits.shape, 1).astype(F32)
    logits = jnp.where(lane < N_EXPERTS, logits, -jnp.inf)
    m1 = jnp.max(logits, axis=1, keepdims=True)
    i1 = jnp.min(jnp.where(logits == m1, lane, float(LANES)), axis=1, keepdims=True)
    rest = jnp.where(lane == i1, -jnp.inf, logits)
    m2 = jnp.max(rest, axis=1, keepdims=True)
    i2 = jnp.min(jnp.where(rest == m2, lane, float(LANES)), axis=1, keepdims=True)
    e2 = jnp.exp(m2 - m1)
    w1 = 1.0 / (1.0 + e2)
    w2 = e2 / (1.0 + e2)
    g_ref[0] = jnp.where(lane == i1, w1, 0.0) + jnp.where(lane == i2, w2, 0.0)
    sel = jnp.logical_or(lane == i1, lane == i2)
    self_ = sel.astype(F32)
    tm = h.shape[0]
    C = min(RANK_CHUNK, tm)
    r = lax.broadcasted_iota(jnp.int32, (C, C), 0)
    c = lax.broadcasted_iota(jnp.int32, (C, C), 1)
    tri = (c < r).astype(BF16)
    carry = jnp.zeros((1, LANES), F32)
    parts = []
    for k in range(tm // C):
        sk = self_[k * C:(k + 1) * C]
        parts.append(jnp.dot(tri, sk.astype(BF16), preferred_element_type=F32) + carry)
        carry = carry + jnp.sum(sk, axis=0, keepdims=True)
    rank = jnp.where(sel, jnp.concatenate(parts, axis=0), -1.0)
    rk_ref[0] = rank
    rkt_ref[0] = rank.T[:8]
    cnt_ref[0, 0] = carry


def _router(x, shift, scale, w_router):
    B, L, D = x.shape
    tm = min(MOE_TOKENS, L)
    nt = L // tm
    wr = jnp.pad(w_router, ((0, 0), (0, LANES - N_EXPERTS)))
    vec = pl.BlockSpec((1, 1, D), lambda b, i: (b, 0, 0))
    col = pl.BlockSpec((1, tm, LANES), lambda b, i: (b, i, 0))
    return pl.pallas_call(
        _router_body,
        grid=(B, nt),
        in_specs=[pl.BlockSpec((1, tm, D), lambda b, i: (b, i, 0)), vec, vec, _full((D, LANES))],
        out_specs=[pl.BlockSpec((1, tm, D), lambda b, i: (b, i, 0)), col, col,
                   pl.BlockSpec((1, 8, tm), lambda b, i: (b, 0, i)), pl.BlockSpec((1, 1, 1, LANES), lambda b, i: (b, i, 0, 0))],
        out_shape=[jax.ShapeDtypeStruct((B, L, D), BF16), jax.ShapeDtypeStruct((B, L, LANES), F32),
                   jax.ShapeDtypeStruct((B, L, LANES), F32), jax.ShapeDtypeStruct((B, 8, L), F32),
                   jax.ShapeDtypeStruct((B, nt, 1, LANES), F32)],
        compiler_params=_cp("parallel", "parallel"),
        name="moe_router",
    )(x, shift, scale, wr)


def _moe_body(cnt_ref, h_ref, g_ref, rk_ref, rkt_ref, w1_ref, w3_ref, w2_ref, o_ref, xg_ref, y_ref, *, M, P):
    b, i, e, j = pl.program_id(0), pl.program_id(1), pl.program_id(2), pl.program_id(3)
    nt, ne, nj = pl.num_programs(1), pl.num_programs(2), pl.num_programs(3)
    tm = h_ref.shape[1]
    cnt = cnt_ref[(b * nt + i) * ne + e]
    n_ch = lax.div(cnt + (M - 1), M)

    @pl.when(jnp.logical_and(e == 0, j == 0))
    def _():
        o_ref[...] = jnp.zeros_like(o_ref)

    @pl.when(j == 0)
    def _():
        rkt = rkt_ref[0, pl.ds(e, 1), :]

        def gather(c, carry):
            r0 = pl.multiple_of(c * M, 16)
            rows = (lax.broadcasted_iota(jnp.int32, (M, 1), 0) + c * M).astype(F32)
            onehot = (rkt == rows).astype(BF16)
            xg_ref[pl.ds(r0, M), :] = jnp.dot(onehot, h_ref[0], preferred_element_type=F32).astype(BF16)
            return carry

        lax.fori_loop(0, n_ch, gather, 0)

    def expert(chunks):
        r0 = [pl.multiple_of(c * M, 16) for c in chunks]
        xg = [xg_ref[pl.ds(r, M), :] for r in r0]
        a = [jnp.dot(x_, w1_ref[0], preferred_element_type=F32) for x_ in xg]
        g = [jnp.dot(x_, w3_ref[0], preferred_element_type=F32) for x_ in xg]
        t = [(a_ * jax.nn.sigmoid(a_) * g_).astype(BF16) for a_, g_ in zip(a, g)]
        yv = [jnp.dot(t_, w2_ref[0], preferred_element_type=F32) for t_ in t]

        @pl.when(j == 0)
        def _():
            for r, y_ in zip(r0, yv):
                y_ref[pl.ds(r, M), :] = y_

        @pl.when(j > 0)
        def _():
            for r, y_ in zip(r0, yv):
                y_ref[pl.ds(r, M), :] += y_

    def expert_pair(c2, carry):
        expert([2 * c2, 2 * c2 + 1])
        return carry

    lax.fori_loop(0, lax.div(n_ch, 2), expert_pair, 0)

    @pl.when(lax.rem(n_ch, 2) == 1)
    def _():
        expert([n_ch - 1])

    @pl.when(j == nj - 1)
    def _():
        for p in range(tm // P):
            lane = lax.broadcasted_iota(jnp.int32, (P, LANES), 1)
            rke = jnp.sum(jnp.where(lane == e, rk_ref[0, p * P:(p + 1) * P, :], 0.0), axis=1, keepdims=True)
            ge = jnp.sum(jnp.where(lane == e, g_ref[0, p * P:(p + 1) * P, :], 0.0), axis=1, keepdims=True)

            def scatter(c, carry):
                r0 = pl.multiple_of(c * M, 16)
                cols = (lax.broadcasted_iota(jnp.int32, (1, M), 1) + c * M).astype(F32)
                onehot = (rke == cols).astype(BF16)
                yb = y_ref[pl.ds(r0, M), :].astype(BF16)
                o_ref[0, p * P:(p + 1) * P, :] += ge * jnp.dot(onehot, yb, preferred_element_type=F32)
                return carry

            lax.fori_loop(0, n_ch, scatter, 0)


def _res_ln_body(x_ref, y_ref, gate_ref, lg_ref, lb_ref, o_ref, *, alpha):
    o_ref[0] = _layernorm_rows(alpha * x_ref[0] + gate_ref[0] * y_ref[0], lg_ref[...], lb_ref[...])


def _res_ln(x, y, gate, ln_g, ln_b, alpha):
    B, L, D = x.shape
    tm = min(WIDE_ROW_TILE, L)
    row = pl.BlockSpec((1, tm, D), lambda b, i: (b, i, 0))
    return pl.pallas_call(
        functools.partial(_res_ln_body, alpha=alpha),
        grid=(B, L // tm),
        in_specs=[row, row, pl.BlockSpec((1, 1, D), lambda b, i: (b, 0, 0)), _full((1, D)), _full((1, D))],
        out_specs=row,
        out_shape=jax.ShapeDtypeStruct((B, L, D), F32),
        compiler_params=_cp("parallel", "parallel"),
        name="res_ln",
    )(x, y, gate, ln_g[None, :], ln_b[None, :])


def _moe(x, shift, scale, gate, w_router, w1, w3, w2, ln_g, ln_b, alpha):
    B, L, D = x.shape
    E, _, F = w1.shape
    hb, gts, rk, rkt, cnt = _router(x, shift, scale, w_router)
    tm = min(MOE_TOKENS, L)
    nt = L // tm
    M = MOE_ROWS
    rows_max = -(-tm // M) * M
    tf = next(t for t in (896, 512, 256, 128, F) if F % t == 0)
    counts = cnt[:, :, 0, :E].astype(jnp.int32).reshape(-1)
    row = lambda w: pl.BlockSpec((1, tm, w), lambda b, i, e, j, c: (b, i, 0))
    y = pl.pallas_call(
        functools.partial(_moe_body, M=M, P=min(512, tm)),
        grid_spec=pltpu.PrefetchScalarGridSpec(
            num_scalar_prefetch=1,
            grid=(B, nt, E, F // tf),
            in_specs=[row(D), row(LANES), row(LANES), pl.BlockSpec((1, 8, tm), lambda b, i, e, j, c: (b, 0, i)),
                      pl.BlockSpec((1, D, tf), lambda b, i, e, j, c: (e, 0, j)),
                      pl.BlockSpec((1, D, tf), lambda b, i, e, j, c: (e, 0, j)),
                      pl.BlockSpec((1, tf, D), lambda b, i, e, j, c: (e, j, 0))],
            out_specs=row(D),
            scratch_shapes=[pltpu.VMEM((rows_max, D), BF16), pltpu.VMEM((rows_max, D), F32)],
        ),
        out_shape=jax.ShapeDtypeStruct((B, L, D), F32),
        compiler_params=_cp("parallel", "parallel", "arbitrary", "arbitrary"),
        name="moe",
    )(counts, hb, gts, rk, rkt, w1.astype(BF16), w3.astype(BF16), w2.astype(BF16))
    return _res_ln(x, y, gate, ln_g, ln_b, alpha)


def _mod_body(c_ref, w_ref, b_ref, o_ref):
    c = c_ref[...]
    s = c * jax.nn.sigmoid(c)
    o_ref[...] = jnp.dot(s, w_ref[...], precision=HI, preferred_element_type=F32) + b_ref[...]


def _modulation(cc, w_mod, b_mod):
    R, D = cc.shape
    N = w_mod.shape[1]
    tn = 1024
    return pl.pallas_call(
        _mod_body,
        grid=(N // tn,),
        in_specs=[_full((R, D)), pl.BlockSpec((D, tn), lambda j: (0, j)), pl.BlockSpec((1, tn), lambda j: (0, j))],
        out_specs=pl.BlockSpec((R, tn), lambda j: (0, j)),
        out_shape=jax.ShapeDtypeStruct((R, N), F32),
        compiler_params=_cp("parallel"),
        name="modulation",
    )(cc, w_mod, b_mod[None, :])


def _streams(x, c, ctx, c_ctx, w_mod, b_mod, w_in, gla_w_gate, gla_b_gate, gla_norm_g, hy_conv_w, hy_conv_b, hy_f_w1, hy_f_b1, hy_f_freq1, hy_f_w2, hy_f_b2, hy_f_freq2, hy_f_w3, hy_f_b3, hy_skip, hy_norm_g, mla_q_norm_g, mla_w_uq, mla_kv_norm_g, mla_w_ukv, mla_norm_g, w_out, ln_g, ln_b, ffn_w1, ffn_w3, ffn_w2, moe_router, moe_w1, moe_w3, moe_w2):
    B, L, D = x.shape
    Lc = ctx.shape[1]
    depth = w_mod.shape[0]
    alpha = (2.0 * depth) ** 0.25
    cc = jnp.zeros((8, D), F32).at[:B].set(c).at[B].set(c_ctx)
    cos, sin = _rope_tables(L, True)
    cos_c, sin_c = _rope_tables(Lc, False)
    cos_all, sin_all = jnp.concatenate([cos_c, cos], axis=0), jnp.concatenate([sin_c, sin], axis=0)
    KD, VD = GLA_HEADS * GLA_DK, GLA_HEADS * GLA_DV
    xc = ctx
    for l in range(depth):
        need_ctx = l < depth - 1
        mods = _modulation(cc, w_mod[l], b_mod[l])
        m = [mods[:B, k * D:(k + 1) * D][:, None, :] for k in range(6)]
        mc = [jnp.broadcast_to(mods[B, k * D:(k + 1) * D][None, None, :], (B, 1, D)) for k in range(6)]
        w_arr = _arrange_w_in(w_in[l])
        wg, bg = _arrange_gate(gla_w_gate[l], gla_b_gate[l])
        filt = (hy_f_w1[l], hy_f_b1[l], hy_f_freq1[l], hy_f_w2[l], hy_f_b2[l], hy_f_freq2[l], hy_f_w3[l], hy_f_b3[l])
        wq = _arrange_wq(mla_w_uq[l])
        wk, wv = _arrange_wkv(mla_w_ukv[l])

        hyu, qk, vg, alr, cq, ckvr = _inproj(x, m[0], m[1], w_arr)
        hyu_c, qk_c, vg_c, alr_c, cq_c, ckvr_c = _inproj(xc, mc[0], mc[1], w_arr)

        of_c, ob_c, s_c = _gla(qk_c, vg_c, alr_c, wg, bg, jnp.zeros((B, 2, KD, VD), F32))
        of, ob, _ = _gla(qk, vg, alr, wg, bg, s_c)
        hy = _hyena(hyu, hy_conv_w[l], hy_conv_b[l], filt, hy_skip[l])
        k_all, v_all = _kvproj(jnp.concatenate([ckvr_c, ckvr], axis=1), mla_kv_norm_g[l], wk, wv, cos_all, sin_all)
        k_c, v_c = k_all[:, :Lc], v_all[:, :Lc]
        q_m = _qproj(cq, mla_q_norm_g[l], wq, cos, sin)
        om = _flash(q_m, k_all, v_all)

        x = _outproj(of, ob, vg, hy, om, x, m[2], gla_norm_g[l], hy_norm_g[l], mla_norm_g[l], w_out[l],
                     ln_g[l, 0], ln_b[l, 0], alpha)
        if need_ctx:
            hy_c = _hyena_ctx(hyu_c, hy_conv_w[l], hy_conv_b[l], filt, hy_skip[l])
            q_c = _qproj(cq_c, mla_q_norm_g[l], wq, cos_c, sin_c)
            om_c = _flash(q_c, k_c, v_c)
            xc = _outproj(of_c, ob_c, vg_c, hy_c, om_c, xc, mc[2], gla_norm_g[l], hy_norm_g[l], mla_norm_g[l],
                          w_out[l], ln_g[l, 0], ln_b[l, 0], alpha)

        i = l // 2
        if l % 2 == 0:
            x = _ffn(x, m[3], m[4], m[5], ffn_w1[i], ffn_w3[i], ffn_w2[i], ln_g[l, 1], ln_b[l, 1], alpha)
            if need_ctx:
                xc = _ffn(xc, mc[3], mc[4], mc[5], ffn_w1[i], ffn_w3[i], ffn_w2[i], ln_g[l, 1], ln_b[l, 1], alpha)
        else:
            x = _moe(x, m[3], m[4], m[5], moe_router[i], moe_w1[i], moe_w3[i], moe_w2[i], ln_g[l, 1], ln_b[l, 1], alpha)
            if need_ctx:
                xc = _moe(xc, mc[3], mc[4], mc[5], moe_router[i], moe_w1[i], moe_w3[i], moe_w2[i], ln_g[l, 1],
                          ln_b[l, 1], alpha)
    return x, xc


def kernel(x, c, ctx, c_ctx, w_mod, b_mod, w_in, gla_w_gate, gla_b_gate, gla_norm_g, hy_conv_w, hy_conv_b, hy_f_w1, hy_f_b1, hy_f_freq1, hy_f_w2, hy_f_b2, hy_f_freq2, hy_f_w3, hy_f_b3, hy_skip, hy_norm_g, mla_q_norm_g, mla_w_uq, mla_kv_norm_g, mla_w_ukv, mla_norm_g, w_out, ln_g, ln_b, ffn_w1, ffn_w3, ffn_w2, moe_router, moe_w1, moe_w3, moe_w2):
    return _streams(x, c, ctx, c_ctx, w_mod, b_mod, w_in, gla_w_gate, gla_b_gate, gla_norm_g, hy_conv_w, hy_conv_b, hy_f_w1, hy_f_b1, hy_f_freq1, hy_f_w2, hy_f_b2, hy_f_freq2, hy_f_w3, hy_f_b3, hy_skip, hy_norm_g, mla_q_norm_g, mla_w_uq, mla_kv_norm_g, mla_w_ukv, mla_norm_g, w_out, ln_g, ln_b, ffn_w1, ffn_w3, ffn_w2, moe_router, moe_w1, moe_w3, moe_w2)[0]
```

```python
import functools
import math

import numpy as np
import jax
import jax.numpy as jnp
from jax import lax
from jax.experimental import pallas as pl
from jax.experimental.pallas import tpu as pltpu

F32 = jnp.float32
BF16 = jnp.bfloat16
HI = lax.Precision.HIGHEST

GRID_W = 64
GLA_HEADS, GLA_DK, GLA_DV, GLA_RANK, GLA_TAU = 4, 32, 64, 16, 16.0
HY_CH, HY_EMB = 256, 33
HY_DECAY_TARGET, HY_FAST_DECAY, HY_SLOW_DECAY = 1e-2, 0.3, 1.5
MLA_HEADS, MLA_Q_RANK, MLA_KV_RANK, MLA_NOPE, MLA_ROPE, MLA_V = 8, 256, 128, 64, 32, 64
MLA_SCALE = (MLA_NOPE + MLA_ROPE) ** -0.5
ROPE_BASE = 10000.0
N_EXPERTS = 8
IN_SPLITS = (128, 128, 256, 256, 32, 768, 256, 128, 32)

LANES = 128
SUBLANES = 8
VMEM_LIMIT = 56 * 1024 * 1024

ROW_TILE = 512
WIDE_ROW_TILE = 1024
GLA_CHUNK = 128
DFT_N2 = 256


def _cp(*sem):
    return pltpu.CompilerParams(dimension_semantics=sem, vmem_limit_bytes=VMEM_LIMIT)


def _full(shape):
    n = len(shape)
    return pl.BlockSpec(shape, lambda *_: (0,) * n)


def _idiv(x, d):
    assert d & (d - 1) == 0
    return lax.shift_right_logical(x, int(math.log2(d)))


INPROJ_WIDTHS = (768, 256, 512, 128, 256, 256)


def _arrange_w_in(w):
    cuts = np.cumsum(IN_SPLITS)[:-1]
    qa, ka, va, ga, alr, hyu, cq, ckv, kr = jnp.split(w, [int(c) for c in cuts], axis=1)
    z96 = jnp.zeros((w.shape[0], 96), w.dtype)
    return jnp.concatenate([hyu, qa, ka, va, ga, alr, z96, cq, ckv, kr, z96], axis=1).astype(BF16)


def _inproj_body(x_ref, sh_ref, sc_ref, w_ref, *out_refs):
    h = x_ref[0] * (1.0 + sc_ref[0]) + sh_ref[0]
    acc = jnp.dot(h.astype(BF16), w_ref[...], preferred_element_type=F32)
    off = 0
    for r in out_refs:
        w = r.shape[-1]
        r[0] = acc[:, off:off + w].astype(r.dtype)
        off += w


def _inproj(x, shift, scale, w_arr):
    B, L, D = x.shape
    tm = min(ROW_TILE, L)
    n = w_arr.shape[1]
    row = lambda w: pl.BlockSpec((1, tm, w), lambda b, i: (b, i, 0))
    vec = pl.BlockSpec((1, 1, D), lambda b, i: (b, 0, 0))
    return pl.pallas_call(
        _inproj_body,
        grid=(B, L // tm),
        in_specs=[row(D), vec, vec, _full((D, n))],
        out_specs=[row(w) for w in INPROJ_WIDTHS],
        out_shape=[jax.ShapeDtypeStruct((B, L, w), BF16) for w in INPROJ_WIDTHS],
        compiler_params=_cp("parallel", "parallel"),
        name="inproj",
    )(x, shift, scale, w_arr)


def _log_sigmoid(z):
    return jnp.minimum(z, 0.0) - jnp.log1p(jnp.exp(-jnp.abs(z)))


def _gla_body(qkf_ref, vf_ref, af_ref, qkb_ref, vb_ref, ab_ref, wg_ref, bg_ref, s0_ref,
              of_ref, ob_ref, sout_ref, s_ref):
    i = pl.program_id(0)
    C = qkf_ref.shape[1]
    KD = GLA_HEADS * GLA_DK
    VD = GLA_HEADS * GLA_DV

    @pl.when(i == 0)
    def _():
        s_ref[...] = s0_ref[...]

    r = lax.broadcasted_iota(jnp.int32, (C, C), 0)
    c = lax.broadcasted_iota(jnp.int32, (C, C), 1)
    tris = ((c <= r).astype(F32), (c >= r).astype(F32))
    lane_k = _idiv(lax.broadcasted_iota(jnp.int32, (1, KD), 1), GLA_DK)
    lane_v = _idiv(lax.broadcasted_iota(jnp.int32, (1, VD), 1), GLA_DV)
    rk = _idiv(lax.broadcasted_iota(jnp.int32, (KD, VD), 0), GLA_DK)
    cv = _idiv(lax.broadcasted_iota(jnp.int32, (KD, VD), 1), GLA_DV)
    ones = jnp.ones((C, VD), F32)
    refs = ((qkf_ref, vf_ref, af_ref, of_ref), (qkb_ref, vb_ref, ab_ref, ob_ref))
    chains = [(b, d) for b in range(qkf_ref.shape[0]) for d in range(2)]

    z = [jnp.dot(refs[d][2][b].astype(F32), wg_ref[...], precision=HI, preferred_element_type=F32) + bg_ref[...]
         for b, d in chains]
    la = [_log_sigmoid(zz[:, d * KD:(d + 1) * KD]) / GLA_TAU for zz, (b, d) in zip(z, chains)]
    bb = [jnp.dot(tris[d], l_, precision=HI, preferred_element_type=F32) for l_, (b, d) in zip(la, chains)]
    tot_b = [lax.dot_general(l_, ones, (((0,), (0,)), ((), ())), precision=HI, preferred_element_type=F32) for l_ in la]
    qe, ke, kl, vb, s_old = [], [], [], [], []
    for n, (b, d) in enumerate(chains):
        qk = refs[d][0][b].astype(F32)
        q = qk[:, :KD] * (GLA_DK ** -0.5)
        k = qk[:, KD:]
        tot = jnp.sum(la[n], axis=0, keepdims=True)
        qe.append(q * jnp.exp(bb[n]))
        ke.append((k * jnp.exp(-bb[n])).astype(BF16))
        kl.append((k * jnp.exp(tot - bb[n])).astype(BF16))
        vb.append(refs[d][1][b].astype(BF16))
        s_old.append(s_ref[2 * b + d])
    o = [jnp.dot(qe[n].astype(BF16), s_old[n].astype(BF16), preferred_element_type=F32) for n in range(len(chains))]
    att = [[lax.dot_general(jnp.where(lane_k == h, qe[n], 0.0).astype(BF16), ke[n], (((1,), (1,)), ((), ())),
                            preferred_element_type=F32) for h in range(GLA_HEADS)] for n in range(len(chains))]
    kv = [lax.dot_general(kl[n], vb[n], (((0,), (0,)), ((), ())), preferred_element_type=F32) for n in range(len(chains))]
    for n, (b, d) in enumerate(chains):
        on = o[n]
        for h in range(GLA_HEADS):
            oh = jnp.dot((att[n][h] * tris[d]).astype(BF16), vb[n], preferred_element_type=F32)
            on = on + jnp.where(lane_v == h, oh, 0.0)
        refs[d][3][b] = on.astype(refs[d][3].dtype)
        s_ref[2 * b + d] = jnp.exp(tot_b[n]) * s_old[n] + jnp.where(rk == cv, kv[n], 0.0)

    @pl.when(i == pl.num_programs(0) - 1)
    def _():
        sout_ref[...] = s_ref[...]


def _gla(qk, vg, alr, wg, bg, s0):
    B, L, _ = qk.shape
    C = min(GLA_CHUNK, L)
    n = L // C
    KD, VD = GLA_HEADS * GLA_DK, GLA_HEADS * GLA_DV
    fwd = lambda w: pl.BlockSpec((B, C, w), lambda i: (0, i, 0))
    bwd = lambda w: pl.BlockSpec((B, C, w), lambda i: (0, n - 1 - i, 0))
    st = _full((2 * B, KD, VD))
    of, ob, s_out = pl.pallas_call(
        _gla_body,
        grid=(n,),
        in_specs=[fwd(2 * KD), fwd(VD), fwd(LANES), bwd(2 * KD), bwd(VD), bwd(LANES),
                  _full((LANES, 2 * KD)), _full((1, 2 * KD)), st],
        out_specs=[fwd(VD), bwd(VD), st],
        out_shape=[jax.ShapeDtypeStruct((B, L, VD), BF16), jax.ShapeDtypeStruct((B, L, VD), BF16),
                   jax.ShapeDtypeStruct((2 * B, KD, VD), F32)],
        scratch_shapes=[pltpu.VMEM((2 * B, KD, VD), F32)],
        compiler_params=_cp("arbitrary"),
        name="gla",
    )(qk, vg, alr, qk, vg, alr, wg, bg, s0.reshape(2 * B, KD, VD))
    return of, ob, s_out.reshape(B, 2, KD, VD)


def _arrange_gate(w_gate, b_gate):
    KD = GLA_HEADS * GLA_DK
    wg = jnp.zeros((LANES, 2 * KD), F32)
    wg = wg.at[:GLA_RANK, :KD].set(w_gate[0]).at[GLA_RANK:2 * GLA_RANK, KD:].set(w_gate[1])
    return wg, jnp.concatenate([b_gate[0], b_gate[1]])[None, :]


def _shortconv_body(x_ref, p_ref, n_ref, w_ref, b_ref, v_ref, x1_ref, x2_ref):
    i = pl.program_id(1)
    last = pl.num_programs(1) - 1
    x = x_ref[0].astype(F32)
    tm = x.shape[0]
    hb = p_ref.shape[1]
    prev = jnp.where(i > 0, p_ref[0].astype(F32)[hb - 1:hb, :], 0.0)
    nxt = jnp.where(i < last, n_ref[0].astype(F32)[0:1, :], 0.0)
    rid = lax.broadcasted_iota(jnp.int32, x.shape, 0)
    dn = jnp.where(rid == 0, prev, pltpu.roll(x, 1, 0))
    up = jnp.where(rid == tm - 1, nxt, pltpu.roll(x, tm - 1, 0))
    w = w_ref[...]
    y = b_ref[...] + dn * w[0:1] + x * w[1:2] + up * w[2:3]
    v_ref[0] = y[:, :HY_CH]
    x1_ref[0] = y[:, HY_CH:2 * HY_CH]
    x2_ref[0] = y[:, 2 * HY_CH:]


def _shortconv(u, w, b):
    B, L, W = u.shape
    tm = min(ROW_TILE, L)
    hb = 2 * SUBLANES
    nb = tm // hb
    row = pl.BlockSpec((1, tm, W), lambda b_, i: (b_, i, 0))
    prev = pl.BlockSpec((1, hb, W), lambda b_, i: (b_, jnp.maximum(i * nb - 1, 0), 0))
    nxt = pl.BlockSpec((1, hb, W), lambda b_, i: (b_, jnp.minimum((i + 1) * nb, L // hb - 1), 0))
    o = pl.BlockSpec((1, tm, HY_CH), lambda b_, i: (b_, i, 0))
    return pl.pallas_call(
        _shortconv_body,
        grid=(B, L // tm),
        in_specs=[row, prev, nxt, _full((3, W)), _full((1, W))],
        out_specs=[o, o, o],
        out_shape=[jax.ShapeDtypeStruct((B, L, HY_CH), F32)] * 3,
        compiler_params=_cp("parallel", "parallel"),
        name="shortconv",
    )(u, u, u, w, b[None, :])


def _filter_feats(L):
    pos = np.arange(L, dtype=np.float64)
    t = pos / (L - 1)
    bands = (HY_EMB - 1) // 2
    freqs = np.linspace(1e-4, bands - 1, bands)
    ang = (2.0 * math.pi * pos / L)[:, None] * freqs
    z = jnp.asarray(np.concatenate([t[:, None], np.cos(ang), -np.sin(ang)], axis=-1), dtype=F32)
    z = jnp.pad(z, ((0, 0), (0, LANES - HY_EMB)))
    deltas = np.abs(np.linspace(math.log(HY_DECAY_TARGET) / HY_SLOW_DECAY,
                                math.log(HY_DECAY_TARGET) / HY_FAST_DECAY, HY_CH))
    return z, jnp.asarray(np.tile(deltas, 4)[None, :], dtype=F32)


def _filter_body(z_ref, w1_ref, b1_ref, f1_ref, w2_ref, b2_ref, f2_ref, w3_ref, b3_ref, dl_ref,
                 h_ref, ss_ref, *, L):
    i = pl.program_id(0)
    z = z_ref[...]
    tm = z.shape[0]
    hid = jnp.sin(f1_ref[...] * (jnp.dot(z, w1_ref[...], precision=HI, preferred_element_type=F32) + b1_ref[...]))
    hid = jnp.sin(f2_ref[...] * (jnp.dot(hid, w2_ref[...], precision=HI, preferred_element_type=F32) + b2_ref[...]))
    h = jnp.dot(hid, w3_ref[...], precision=HI, preferred_element_type=F32) + b3_ref[...]
    pos = (lax.broadcasted_iota(jnp.int32, (tm, 1), 0) + i * tm).astype(F32)
    t = pos / (L - 1)
    h = h * jnp.exp(-t * dl_ref[...])

    @pl.when(i == 0)
    def _():
        ss_ref[...] = jnp.zeros_like(ss_ref)

    ss_ref[...] += jnp.sum(h * h, axis=0, keepdims=True)
    col = lax.broadcasted_iota(jnp.int32, h.shape, 1)
    is_bwd = (_idiv(col, HY_CH) & 1) == 1
    h_ref[...] = jnp.where(jnp.logical_and(is_bwd, pos == 0.0), 0.0, h)


def _filters(L, fw1, fb1, ff1, fw2, fb2, ff2, fw3, fb3):
    z, dl = _filter_feats(L)
    tm = min(WIDE_ROW_TILE, L)
    Hf = fw2.shape[0]
    w1 = jnp.pad(fw1, ((0, LANES - HY_EMB), (0, 0)))
    NC = fw3.shape[1]
    return pl.pallas_call(
        functools.partial(_filter_body, L=L),
        grid=(L // tm,),
        in_specs=[pl.BlockSpec((tm, LANES), lambda i: (i, 0)), _full((LANES, Hf)), _full((1, Hf)), _full((1, Hf)),
                  _full((Hf, Hf)), _full((1, Hf)), _full((1, Hf)), _full((Hf, NC)), _full((1, NC)), _full((1, NC))],
        out_specs=[pl.BlockSpec((tm, NC), lambda i: (i, 0)), _full((1, NC))],
        out_shape=[jax.ShapeDtypeStruct((L, NC), F32), jax.ShapeDtypeStruct((1, NC), F32)],
        compiler_params=_cp("arbitrary"),
        name="hy_filters",
    )(z, w1, fb1[None], ff1[None], fw2, fb2[None], ff2[None], fw3, fb3[None], dl)


def _dft_consts(L):
    N = 2 * L
    N2 = DFT_N2
    N1 = N // N2
    half = N1 // 2
    k1 = np.arange(N1)[:, None].astype(np.float64)
    n1 = np.arange(N1)[None, :].astype(np.float64)
    a1 = 2.0 * np.pi * k1 * n1 / N1
    f1r, f1i = np.cos(a1), -np.sin(a1)
    fa = np.concatenate([f1r[:, :half], f1i[:, :half]], axis=0)
    fb = np.concatenate([f1r[:half, :], f1i[:half, :]], axis=1) / N
    k2 = np.arange(N2)[:, None].astype(np.float64)
    n2 = np.arange(N2)[None, :].astype(np.float64)
    a2 = 2.0 * np.pi * k2 * n2 / N2
    f2r, f2i = np.cos(a2), -np.sin(a2)
    g = np.block([[f2r, -f2i], [f2i, f2r]])
    gc = np.block([[f2r, f2i], [-f2i, f2r]])
    at = 2.0 * np.pi * (np.arange(N1)[:, None] * np.arange(N2)[None, :] % N) / N
    twr, twi = np.cos(at), -np.sin(at)
    c = lambda a: jnp.asarray(a, dtype=F32)
    bc = lambda a: jnp.broadcast_to(c(a)[:, :, None], (N1, N2, LANES))
    eye = np.eye(SUBLANES)
    return dict(N1=N1, N2=N2, half=half, fa=c(np.kron(fa, eye)), fb=c(np.kron(fb, eye)), g=c(g), gc=c(gc),
                twr=bc(twr), twi=bc(twi))


def _lanes(t, width):
    return jnp.concatenate([t] * (width // LANES), axis=-1)


def _dft1_body(f_ref, x_ref, o_ref):
    x = x_ref[0]
    x2 = x.reshape(x.shape[0] * SUBLANES, x.shape[2]).astype(BF16)
    y = jnp.dot(f_ref[...], x2, preferred_element_type=F32)
    o_ref[0] = y.reshape(o_ref.shape[1], SUBLANES, y.shape[1])


def _dft_stage1(fa, x):
    B, half, N2, W = x.shape
    R = fa.shape[0] // SUBLANES
    tw = min(W, HY_CH)
    return pl.pallas_call(
        _dft1_body,
        grid=(B, N2 // SUBLANES, W // tw),
        in_specs=[_full(fa.shape), pl.BlockSpec((1, half, SUBLANES, tw), lambda b, j, c: (b, 0, j, c))],
        out_specs=pl.BlockSpec((1, R, SUBLANES, tw), lambda b, j, c: (b, 0, j, c)),
        out_shape=jax.ShapeDtypeStruct((B, R, N2, W), F32),
        compiler_params=_cp("parallel", "parallel", "parallel"),
        name="hy_dft1",
    )(fa.astype(BF16), x)


def _filter_spec_body(a_ref, twr_ref, twi_ref, g_ref, ss_ref, hf_ref):
    W = a_ref.shape[-1]
    ar, ai = a_ref[0, 0], a_ref[1, 0]
    twr, twi = _lanes(twr_ref[0], W), _lanes(twi_ref[0], W)
    xr = ar * twr - ai * twi
    xi = ar * twi + ai * twr
    z = jnp.dot(g_ref[...], jnp.concatenate([xr, xi], axis=0).astype(BF16), preferred_element_type=F32)
    n2 = z.shape[0] // 2
    zr, zi = z[:n2], z[n2:]
    ss = ss_ref[...]
    for o in range(2):
        f0, b0 = (2 * o) * HY_CH, (2 * o + 1) * HY_CH
        sc = lax.rsqrt(ss[:, f0:f0 + HY_CH] + ss[:, b0:b0 + HY_CH] + 1e-6)
        hf_ref[o, 0, 0] = ((zr[:, f0:f0 + HY_CH] + zr[:, b0:b0 + HY_CH]) * sc).astype(hf_ref.dtype)
        hf_ref[o, 0, 1] = ((zi[:, f0:f0 + HY_CH] - zi[:, b0:b0 + HY_CH]) * sc).astype(hf_ref.dtype)


def _filter_spectrum(h, ss, dc):
    L, NC = h.shape
    N1, N2, half = dc["N1"], dc["N2"], dc["half"]
    a = _dft_stage1(dc["fa"], h.reshape(1, half, N2, NC))
    a = a.reshape(2, N1, N2, NC)
    return pl.pallas_call(
        _filter_spec_body,
        grid=(N1,),
        in_specs=[pl.BlockSpec((2, 1, N2, NC), lambda k: (0, k, 0, 0)),
                  pl.BlockSpec((1, N2, LANES), lambda k: (k, 0, 0)), pl.BlockSpec((1, N2, LANES), lambda k: (k, 0, 0)),
                  _full((2 * N2, 2 * N2)), _full((1, NC))],
        out_specs=pl.BlockSpec((2, 1, 2, N2, HY_CH), lambda k: (0, k, 0, 0, 0)),
        out_shape=jax.ShapeDtypeStruct((2, N1, 2, N2, HY_CH), BF16),
        compiler_params=_cp("parallel"),
        name="hy_filter_spec",
    )(a, dc["twr"], dc["twi"], dc["g"].astype(BF16), ss)


SPEC_K1 = 4


def _spec_mul_body(a_ref, twr_ref, twi_ref, g_ref, gc_ref, hf_ref, o_ref):
    W = a_ref.shape[-1]
    ks = range(a_ref.shape[2])
    n2 = g_ref.shape[0] // 2
    x = []
    for k in ks:
        ar, ai = a_ref[0, 0, k], a_ref[0, 1, k]
        twr, twi = _lanes(twr_ref[k], W), _lanes(twi_ref[k], W)
        x.append(jnp.concatenate([ar * twr - ai * twi, ar * twi + ai * twr], axis=0).astype(BF16))
    z = [jnp.dot(g_ref[...], x_, preferred_element_type=F32) for x_ in x]
    y = []
    for k, z_ in zip(ks, z):
        zr, zi = z_[:n2], z_[n2:]
        hr, hi = hf_ref[0, k, 0].astype(F32), hf_ref[0, k, 1].astype(F32)
        y.append(jnp.concatenate([zr * hr - zi * hi, zr * hi + zi * hr], axis=0).astype(BF16))
    b = [jnp.dot(gc_ref[...], y_, preferred_element_type=F32) for y_ in y]
    for k, b_ in zip(ks, b):
        br, bi = b_[:n2], b_[n2:]
        twr, twi = _lanes(twr_ref[k], W), _lanes(twi_ref[k], W)
        o_ref[0, 0, k] = br * twr + bi * twi
        o_ref[0, 1, k] = bi * twr - br * twi


def _spec_mul(a, hf, order, dc):
    B = a.shape[0]
    N1, N2 = dc["N1"], dc["N2"]
    C = a.shape[-1]
    kb = min(SPEC_K1, N1)
    blk = pl.BlockSpec((1, 2, kb, N2, C), lambda k, b: (b, 0, k, 0, 0))
    tw = pl.BlockSpec((kb, N2, LANES), lambda k, b: (k, 0, 0))
    return pl.pallas_call(
        _spec_mul_body,
        grid=(N1 // kb, B),
        in_specs=[blk, tw, tw, _full((2 * N2, 2 * N2)), _full((2 * N2, 2 * N2)),
                  pl.BlockSpec((1, kb, 2, N2, C), lambda k, b: (order, k, 0, 0, 0))],
        out_specs=blk,
        out_shape=jax.ShapeDtypeStruct(a.shape, F32),
        compiler_params=_cp("parallel", "parallel"),
        name="hy_spec_mul",
    )(a, dc["twr"], dc["twi"], dc["g"].astype(BF16), dc["gc"].astype(BF16), hf)


def _dft3_body(f_ref, b_ref, u_ref, gate_ref, skip_ref, o_ref):
    bm = b_ref[0]
    b2 = bm.reshape(bm.shape[0] * SUBLANES, bm.shape[2]).astype(BF16)
    y = jnp.dot(f_ref[...], b2, preferred_element_type=F32)
    rows, C = y.shape
    u = u_ref[0].reshape(rows, C)
    gate = gate_ref[0].reshape(rows, C)
    o_ref[0] = (gate * (y + u * skip_ref[...])).reshape(o_ref.shape[1], SUBLANES, C)


def _dft_stage3(fb, bm, u, gate, skip):
    B, R, N2, C = bm.shape
    half = u.shape[1]
    row = pl.BlockSpec((1, half, SUBLANES, C), lambda b, j: (b, 0, j, 0))
    return pl.pallas_call(
        _dft3_body,
        grid=(B, N2 // SUBLANES),
        in_specs=[_full(fb.shape), pl.BlockSpec((1, R, SUBLANES, C), lambda b, j: (b, 0, j, 0)), row, row, _full((1, C))],
        out_specs=row,
        out_shape=jax.ShapeDtypeStruct((B, half, N2, C), F32),
        compiler_params=_cp("parallel", "parallel"),
        name="hy_dft3",
    )(fb.astype(BF16), bm, u, gate, skip[None, :])


def _longconv_gated(u, gate, hf, order, skip, dc):
    B, L, C = u.shape
    N1, N2, half = dc["N1"], dc["N2"], dc["half"]
    u4 = u.reshape(B, half, N2, C)
    a = _dft_stage1(dc["fa"], u4).reshape(B, 2, N1, N2, C)
    bm = _spec_mul(a, hf, order, dc).reshape(B, 2 * N1, N2, C)
    return _dft_stage3(dc["fb"], bm, u4, gate.reshape(B, half, N2, C), skip).reshape(B, L, C)


def _hyena(hyu, conv_w, conv_b, filt, skip):
    B, L, _ = hyu.shape
    v, x1, x2 = _shortconv(hyu, conv_w, conv_b)
    h, ss = _filters(L, *filt)
    dc = _dft_consts(L)
    hf = _filter_spectrum(h, ss, dc)
    z1 = _longconv_gated(v, x1, hf, 0, skip[0], dc)
    return _longconv_gated(z1, x2, hf, 1, skip[1], dc)


def _hyena_ctx_body(v_ref, x1_ref, x2_ref, h_ref, ss_ref, skip_ref, fc_ref, gc_ref, o_ref):
    fc, gc = fc_ref[...], gc_ref[...]
    n = fc.shape[0] // 2
    ss = ss_ref[...]
    h = h_ref[...]

    def conv(u, o):
        f0, b0 = (2 * o) * HY_CH, (2 * o + 1) * HY_CH
        sc = lax.rsqrt(ss[:, f0:f0 + HY_CH] + ss[:, b0:b0 + HY_CH] + 1e-6)
        x = jnp.dot(fc, u, precision=HI, preferred_element_type=F32)
        hf = jnp.dot(fc, h[:, f0:f0 + HY_CH], precision=HI, preferred_element_type=F32)
        hb = jnp.dot(fc, h[:, b0:b0 + HY_CH], precision=HI, preferred_element_type=F32)
        hr = (hf[:n] + hb[:n]) * sc
        hi = (hf[n:] - hb[n:]) * sc
        yr = x[:n] * hr - x[n:] * hi
        yi = x[:n] * hi + x[n:] * hr
        y = jnp.dot(gc, jnp.concatenate([yr, yi], axis=0), precision=HI, preferred_element_type=F32)
        return y + u * skip_ref[o:o + 1, :]

    z1 = x1_ref[0] * conv(v_ref[0], 0)
    o_ref[0] = x2_ref[0] * conv(z1, 1)


def _hyena_ctx(hyu, conv_w, conv_b, filt, skip):
    B, L, _ = hyu.shape
    v, x1, x2 = _shortconv(hyu, conv_w, conv_b)
    h, ss = _filters(L, *filt)
    N = 2 * L
    ang = 2.0 * np.pi * (np.arange(N)[:, None] * np.arange(L)[None, :] % N) / N
    fr, fi = np.cos(ang), -np.sin(ang)
    fc = jnp.asarray(np.concatenate([fr, fi], axis=0), dtype=F32)
    gc = jnp.asarray(np.concatenate([fr.T, fi.T], axis=1) / N, dtype=F32)
    row = pl.BlockSpec((1, L, HY_CH), lambda b: (b, 0, 0))
    return pl.pallas_call(
        _hyena_ctx_body,
        grid=(B,),
        in_specs=[row, row, row, _full(h.shape), _full(ss.shape), _full(skip.shape), _full(fc.shape), _full(gc.shape)],
        out_specs=row,
        out_shape=jax.ShapeDtypeStruct((B, L, HY_CH), F32),
        compiler_params=_cp("parallel"),
        name="hyena_ctx",
    )(v, x1, x2, h, ss, skip, fc, gc)


HEAD_PAD = 128


def _rope_swap(w):
    a, b, c, d = w[..., 0:8], w[..., 8:16], w[..., 16:24], w[..., 24:32]
    return jnp.concatenate([-b, a, -d, c], axis=-1)


def _arrange_wq(w_uq):
    R = w_uq.shape[0]
    w = w_uq.reshape(R, MLA_HEADS, MLA_NOPE + MLA_ROPE)
    rope = w[..., MLA_NOPE:]
    out = jnp.concatenate([w[..., :MLA_NOPE], rope, _rope_swap(rope)], axis=-1)
    return out.reshape(R, MLA_HEADS * HEAD_PAD).astype(BF16)


def _arrange_wkv(w_ukv):
    R = w_ukv.shape[0]
    w = w_ukv.reshape(R, MLA_HEADS, MLA_NOPE + MLA_V)
    wk = jnp.concatenate([w[..., :MLA_NOPE], jnp.zeros((R, MLA_HEADS, HEAD_PAD - MLA_NOPE), w.dtype)], axis=-1)
    wv = w[..., MLA_NOPE:]
    return wk.reshape(R, MLA_HEADS * HEAD_PAD).astype(BF16), wv.reshape(R, MLA_HEADS * MLA_V).astype(BF16)


def _kr_place():
    e = np.zeros((LANES, MLA_HEADS * HEAD_PAD), np.float32)
    es = np.zeros((LANES, MLA_HEADS * HEAD_PAD), np.float32)
    for h in range(MLA_HEADS):
        base = h * HEAD_PAD + MLA_NOPE
        for j in range(MLA_ROPE):
            e[j, base + j] = 1.0
            blk, r = divmod(j, 16)
            if r < 8:
                es[16 * blk + r + 8, base + j] = -1.0
            else:
                es[16 * blk + r - 8, base + j] = 1.0
    return jnp.asarray(e).astype(BF16), jnp.asarray(es).astype(BF16)


def _rope_tables(L, rope):
    if rope:
        t = np.arange(L)
        row, col = (t // GRID_W).astype(np.float64), (t % GRID_W).astype(np.float64)
        half = MLA_ROPE // 2
        inv = ROPE_BASE ** (-np.arange(0, half, 2, dtype=np.float64) / half)
        ar, ac = row[:, None] * inv, col[:, None] * inv
        cos = jnp.asarray(np.concatenate([np.cos(ar), np.cos(ar), np.cos(ac), np.cos(ac)], axis=-1), dtype=F32)
        sin = jnp.asarray(np.concatenate([np.sin(ar), np.sin(ar), np.sin(ac), np.sin(ac)], axis=-1), dtype=F32)
    else:
        cos, sin = jnp.ones((L, MLA_ROPE), F32), jnp.zeros((L, MLA_ROPE), F32)
    return cos, sin


def _rms_rows(x, g, eps=1e-6):
    return x * lax.rsqrt(jnp.mean(x * x, axis=-1, keepdims=True) + eps) * g


def _qproj_body(cq_ref, g_ref, w_ref, t1_ref, t2_ref, q_ref):
    xn = _rms_rows(cq_ref[0].astype(F32), g_ref[...])
    acc = jnp.dot(xn.astype(BF16), w_ref[...], preferred_element_type=F32)
    W = acc.shape[1]
    t1, t2 = _lanes(t1_ref[...], W), _lanes(t2_ref[...], W)
    q_ref[0] = (acc * t1 + pltpu.roll(acc, W - MLA_ROPE, 1) * t2).astype(q_ref.dtype)


def _qproj(cq, g, wq, cos, sin):
    B, L, R = cq.shape
    tm = min(ROW_TILE, L)
    W = wq.shape[1]
    ones, zeros = jnp.ones((L, MLA_NOPE), F32), jnp.zeros((L, MLA_ROPE), F32)
    qs = MLA_SCALE * math.log2(math.e)
    t1 = jnp.concatenate([ones, cos, zeros], axis=-1) * qs
    t2 = jnp.concatenate([jnp.zeros((L, MLA_NOPE), F32), sin, zeros], axis=-1) * qs
    tab = pl.BlockSpec((tm, HEAD_PAD), lambda b, i: (i, 0))
    return pl.pallas_call(
        _qproj_body,
        grid=(B, L // tm),
        in_specs=[pl.BlockSpec((1, tm, R), lambda b, i: (b, i, 0)), _full((1, R)), _full((R, W)), tab, tab],
        out_specs=pl.BlockSpec((1, tm, W), lambda b, i: (b, i, 0)),
        out_shape=jax.ShapeDtypeStruct((B, L, W), BF16),
        compiler_params=_cp("parallel", "parallel"),
        name="mla_qproj",
    )(cq, g[None, :], wq, t1, t2)


def _kvproj_body(c_ref, g_ref, wk_ref, wv_ref, e_ref, es_ref, cos_ref, sin_ref, k_ref, v_ref):
    c = c_ref[0].astype(F32)
    R = MLA_KV_RANK
    xn = _rms_rows(c[:, :R], g_ref[...]).astype(BF16)
    kr = c[:, R:]
    acc = jnp.dot(xn, wk_ref[...], preferred_element_type=F32)
    acc += jnp.dot((kr * cos_ref[...]).astype(BF16), e_ref[...], preferred_element_type=F32)
    acc += jnp.dot((kr * sin_ref[...]).astype(BF16), es_ref[...], preferred_element_type=F32)
    k_ref[0] = acc.astype(k_ref.dtype)
    v_ref[0] = jnp.dot(xn, wv_ref[...], preferred_element_type=F32).astype(v_ref.dtype)


def _kvproj(ckvr, g, wk, wv, cos, sin):
    B, L, Wc = ckvr.shape
    tm = next(t for t in (1280, 512, 256, L) if L % t == 0)
    pad = jnp.zeros((L, LANES - MLA_ROPE), F32)
    cos_p, sin_p = jnp.concatenate([cos, pad], axis=-1), jnp.concatenate([sin, pad], axis=-1)
    e, es = _kr_place()
    tab = pl.BlockSpec((tm, LANES), lambda b, i: (i, 0))
    Wk, Wv = wk.shape[1], wv.shape[1]
    return pl.pallas_call(
        _kvproj_body,
        grid=(B, L // tm),
        in_specs=[pl.BlockSpec((1, tm, Wc), lambda b, i: (b, i, 0)), _full((1, MLA_KV_RANK)),
                  _full(wk.shape), _full(wv.shape), _full(e.shape), _full(es.shape), tab, tab],
        out_specs=[pl.BlockSpec((1, tm, Wk), lambda b, i: (b, i, 0)), pl.BlockSpec((1, tm, Wv), lambda b, i: (b, i, 0))],
        out_shape=[jax.ShapeDtypeStruct((B, L, Wk), BF16), jax.ShapeDtypeStruct((B, L, Wv), BF16)],
        compiler_params=_cp("parallel", "parallel"),
        name="mla_kvproj",
    )(ckvr, g[None, :], wk, wv, e, es, cos_p, sin_p)


FLASH_Q_TILE = 2048
FLASH_ROWS = 256
FLASH_KEYS = 256


def _flash_body(q_ref, k_ref, v_ref, o_ref, m_ref, l_ref, acc_ref, s_ref, *, R):
    j = pl.program_id(3)
    tq, tk = q_ref.shape[1], k_ref.shape[1]
    CK = FLASH_KEYS
    npc = CK // LANES

    @pl.when(j == 0)
    def _():
        m_ref[...] = jnp.full_like(m_ref, -jnp.inf)
        l_ref[...] = jnp.zeros_like(l_ref)
        acc_ref[...] = jnp.zeros_like(acc_ref)

    def pass1(a, r):
        lo, r0 = a * HEAD_PAD, r * R
        q = q_ref[0, r0:r0 + R, lo:lo + HEAD_PAD]
        mp = None
        for c in range(tk // CK):
            kc = k_ref[0, c * CK:(c + 1) * CK, lo:lo + HEAD_PAD]
            s = lax.dot_general(q, kc, (((1,), (1,)), ((), ())), preferred_element_type=F32)
            s_ref[r0:r0 + R, c * CK:(c + 1) * CK] = s
            for w in range(npc):
                pc = s[:, w * LANES:(w + 1) * LANES]
                mp = pc if mp is None else jnp.maximum(mp, pc)
        m_old = m_ref[a, r0:r0 + R, :]
        return m_old, jnp.maximum(m_old, jnp.max(mp, axis=1, keepdims=True))

    def pass2(a, r, m_old, m_new):
        r0 = r * R
        alpha = jnp.exp2(m_old - m_new)
        lp = jnp.zeros((R, LANES), F32)
        pv = jnp.zeros((R, 2 * MLA_V), F32)
        for c in range(tk // CK):
            s = s_ref[r0:r0 + R, c * CK:(c + 1) * CK]
            ps = [jnp.exp2(s[:, w * LANES:(w + 1) * LANES] - m_new) for w in range(npc)]
            for p_ in ps:
                lp = lp + p_
            p = jnp.concatenate(ps, axis=1).astype(BF16)
            pv = pv + jnp.dot(p, v_ref[0, c * CK:(c + 1) * CK, :], preferred_element_type=F32)
        l_ref[a, r0:r0 + R, :] = alpha * l_ref[a, r0:r0 + R, :] + jnp.sum(lp, axis=1, keepdims=True)
        acc_ref[a, r0:r0 + R, :] = alpha * acc_ref[a, r0:r0 + R, :] + pv
        m_ref[a, r0:r0 + R, :] = m_new

    assert tq // R >= 2
    blocks = [(a, r) for a in range(2) for r in range(tq // R)]
    pend = pass1(*blocks[0])
    for i, blk in enumerate(blocks):
        nxt = pass1(*blocks[i + 1]) if i + 1 < len(blocks) else None
        pass2(*blk, *pend)
        pend = nxt

    @pl.when(j == pl.num_programs(3) - 1)
    def _():
        lane = lax.broadcasted_iota(jnp.int32, acc_ref.shape[1:], 1)
        o_ref[0] = jnp.where(lane < MLA_V, acc_ref[0] / l_ref[0], acc_ref[1] / l_ref[1]).astype(o_ref.dtype)


def _flash_tiles(Lq, Lk):
    tq = min(FLASH_Q_TILE, Lq)
    tk = next(t for t in (3328, 1280, 1024, 512, 256, Lk) if Lk % t == 0)
    return tq, tk


def _flash(q, k, v):
    B, Lq, _ = q.shape
    Lk = k.shape[1]
    tq, tk = _flash_tiles(Lq, Lk)
    hp = MLA_HEADS // 2
    return pl.pallas_call(
        functools.partial(_flash_body, R=min(FLASH_ROWS, tq // 2)),
        grid=(B, hp, Lq // tq, Lk // tk),
        in_specs=[pl.BlockSpec((1, tq, 2 * HEAD_PAD), lambda b, h, i, j: (b, i, h)),
                  pl.BlockSpec((1, tk, 2 * HEAD_PAD), lambda b, h, i, j: (b, j, h)),
                  pl.BlockSpec((1, tk, 2 * MLA_V), lambda b, h, i, j: (b, j, h))],
        out_specs=pl.BlockSpec((1, tq, 2 * MLA_V), lambda b, h, i, j: (b, i, h)),
        out_shape=jax.ShapeDtypeStruct((B, Lq, MLA_HEADS * MLA_V), BF16),
        scratch_shapes=[pltpu.VMEM((2, tq, LANES), F32), pltpu.VMEM((2, tq, LANES), F32),
                        pltpu.VMEM((2, tq, 2 * MLA_V), F32), pltpu.VMEM((tq, tk), F32)],
        compiler_params=_cp("parallel", "parallel", "parallel", "arbitrary"),
        name="mla_flash",
    )(q, k, v)


def _layernorm_rows(x, g, b, eps=1e-5):
    mu = jnp.mean(x, axis=-1, keepdims=True)
    xc = x - mu
    var = jnp.mean(xc * xc, axis=-1, keepdims=True)
    return xc * lax.rsqrt(var + eps) * g + b


def _outproj_body(of_ref, ob_ref, g_ref, hy_ref, om_ref, x_ref, gate_ref, gg_ref, hg_ref, mg_ref,
                  w_ref, lg_ref, lb_ref, o_ref, *, alpha):
    VD = GLA_HEADS * GLA_DV
    tm = x_ref.shape[1]
    r = _idiv(lax.broadcasted_iota(jnp.int32, (VD, VD), 0), GLA_DV)
    c = _idiv(lax.broadcasted_iota(jnp.int32, (VD, VD), 1), GLA_DV)
    grp = (r == c).astype(F32)
    parts = [slice(0, tm // 2), slice(tm // 2, tm)] if tm % 32 == 0 else [slice(0, tm)]
    o = [of_ref[0, p, :].astype(F32) + ob_ref[0, p, :].astype(F32) for p in parts]
    ms = [jnp.dot(o_ * o_, grp, precision=HI, preferred_element_type=F32) * (1.0 / GLA_DV) for o_ in o]
    ys = []
    for p, o_, ms_ in zip(parts, o, ms):
        g = g_ref[0, p, :].astype(F32)
        ya = o_ * lax.rsqrt(ms_ + 1e-6) * gg_ref[...] * (g * jax.nn.sigmoid(g))
        yb = _rms_rows(hy_ref[0, p, :], hg_ref[...])
        yc = _rms_rows(om_ref[0, p, :].astype(F32), mg_ref[...])
        ys.append((ya.astype(BF16), yb.astype(BF16), yc.astype(BF16)))
    accs = []
    for ya, yb, yc in ys:
        acc = jnp.dot(ya, w_ref[0:VD, :], preferred_element_type=F32)
        acc += jnp.dot(yb, w_ref[VD:VD + HY_CH, :], preferred_element_type=F32)
        acc += jnp.dot(yc, w_ref[VD + HY_CH:, :], preferred_element_type=F32)
        accs.append(acc)
    for p, acc in zip(parts, accs):
        o_ref[0, p, :] = _layernorm_rows(alpha * x_ref[0, p, :] + gate_ref[0] * acc, lg_ref[...], lb_ref[...])


def _outproj(of, ob, vg, hy, om, x, gate, gla_g, hy_g, mla_g, w_out, ln_g, ln_b, alpha):
    B, L, D = x.shape
    tm = min(ROW_TILE, L)
    VD = GLA_HEADS * GLA_DV
    MD = MLA_HEADS * MLA_V
    row = lambda w: pl.BlockSpec((1, tm, w), lambda b, i: (b, i, 0))
    return pl.pallas_call(
        functools.partial(_outproj_body, alpha=alpha),
        grid=(B, L // tm),
        in_specs=[row(VD), row(VD), pl.BlockSpec((1, tm, VD), lambda b, i: (b, i, 1)), row(HY_CH), row(MD), row(D),
                  pl.BlockSpec((1, 1, D), lambda b, i: (b, 0, 0)), _full((1, VD)), _full((1, HY_CH)), _full((1, MD)),
                  _full(w_out.shape), _full((1, D)), _full((1, D))],
        out_specs=row(D),
        out_shape=jax.ShapeDtypeStruct((B, L, D), F32),
        compiler_params=_cp("parallel", "parallel"),
        name="outproj",
    )(of, ob, vg, hy, om, x, gate, jnp.tile(gla_g, GLA_HEADS)[None, :], hy_g[None, :], mla_g[None, :],
      w_out.astype(BF16), ln_g[None, :], ln_b[None, :])


def _ffn_body(x_ref, sh_ref, sc_ref, gate_ref, w1_ref, w3_ref, w2_ref, lg_ref, lb_ref, o_ref, h_ref, acc_ref, *, alpha):
    j = pl.program_id(2)

    @pl.when(j == 0)
    def _():
        h_ref[...] = (x_ref[0] * (1.0 + sc_ref[0]) + sh_ref[0]).astype(BF16)
        acc_ref[...] = jnp.zeros_like(acc_ref)

    h = h_ref[...]
    a = jnp.dot(h, w1_ref[...], preferred_element_type=F32)
    b = jnp.dot(h, w3_ref[...], preferred_element_type=F32)
    t = (a * jax.nn.sigmoid(a) * b).astype(BF16)
    acc_ref[...] += jnp.dot(t, w2_ref[...], preferred_element_type=F32)

    @pl.when(j == pl.num_programs(2) - 1)
    def _():
        o_ref[0] = _layernorm_rows(alpha * x_ref[0] + gate_ref[0] * acc_ref[...], lg_ref[...], lb_ref[...])


def _ffn(x, shift, scale, gate, w1, w3, w2, ln_g, ln_b, alpha):
    B, L, D = x.shape
    F = w1.shape[1]
    tf = next(t for t in (1408, 512, 256, 128, F) if F % t == 0)
    tm = min(ROW_TILE if tf > 512 else WIDE_ROW_TILE, L)
    row = pl.BlockSpec((1, tm, D), lambda b, i, j: (b, i, 0))
    vec = pl.BlockSpec((1, 1, D), lambda b, i, j: (b, 0, 0))
    return pl.pallas_call(
        functools.partial(_ffn_body, alpha=alpha),
        grid=(B, L // tm, F // tf),
        in_specs=[row, vec, vec, vec,
                  pl.BlockSpec((D, tf), lambda b, i, j: (0, j)), pl.BlockSpec((D, tf), lambda b, i, j: (0, j)),
                  pl.BlockSpec((tf, D), lambda b, i, j: (j, 0)), _full((1, D)), _full((1, D))],
        out_specs=row,
        out_shape=jax.ShapeDtypeStruct((B, L, D), F32),
        scratch_shapes=[pltpu.VMEM((tm, D), BF16), pltpu.VMEM((tm, D), F32)],
        compiler_params=_cp("parallel", "parallel", "arbitrary"),
        name="ffn",
    )(x, shift, scale, gate, w1.astype(BF16), w3.astype(BF16), w2.astype(BF16), ln_g[None, :], ln_b[None, :])


MOE_TOKENS = 2048
MOE_ROWS = 256
RANK_CHUNK = 256


def _router_body(x_ref, sh_ref, sc_ref, wr_ref, h_ref, g_ref, rk_ref, rkt_ref, cnt_ref):
    h = x_ref[0] * (1.0 + sc_ref[0]) + sh_ref[0]
    h_ref[0] = h.astype(BF16)
    logits = jnp.dot(h, wr_ref[...], precision=HI, preferred_element_type=F32)
    lane = lax.broadcasted_iota(jnp.int32, logits.shape, 1).astype(F32)
    logits = jnp.where(lane < N_EXPERTS, logits, -jnp.inf)
    m1 = jnp.max(logits, axis=1, keepdims=True)
    i1 = jnp.min(jnp.where(logits == m1, lane, float(LANES)), axis=1, keepdims=True)
    rest = jnp.where(lane == i1, -jnp.inf, logits)
    m2 = jnp.max(rest, axis=1, keepdims=True)
    i2 = jnp.min(jnp.where(rest == m2, lane, float(LANES)), axis=1, keepdims=True)
    e2 = jnp.exp(m2 - m1)
    w1 = 1.0 / (1.0 + e2)
    w2 = e2 / (1.0 + e2)
    g_ref[0] = jnp.where(lane == i1, w1, 0.0) + jnp.where(lane == i2, w2, 0.0)
    sel = jnp.logical_or(lane == i1, lane == i2)
    self_ = sel.astype(F32)
    tm = h.shape[0]
    C = min(RANK_CHUNK, tm)
    r = lax.broadcasted_iota(jnp.int32, (C, C), 0)
    c = lax.broadcasted_iota(jnp.int32, (C, C), 1)
    tri = (c < r).astype(BF16)
    carry = jnp.zeros((1, LANES), F32)
    parts = []
    for k in range(tm // C):
        sk = self_[k * C:(k + 1) * C]
        parts.append(jnp.dot(tri, sk.astype(BF16), preferred_element_type=F32) + carry)
        carry = carry + jnp.sum(sk, axis=0, keepdims=True)
    rank = jnp.where(sel, jnp.concatenate(parts, axis=0), -1.0)
    rk_ref[0] = rank
    rkt_ref[0] = rank.T[:8]
    cnt_ref[0, 0] = carry


def _router(x, shift, scale, w_router):
    B, L, D = x.shape
    tm = min(MOE_TOKENS, L)
    nt = L // tm
    wr = jnp.pad(w_router, ((0, 0), (0, LANES - N_EXPERTS)))
    vec = pl.BlockSpec((1, 1, D), lambda b, i: (b, 0, 0))
    col = pl.BlockSpec((1, tm, LANES), lambda b, i: (b, i, 0))
    return pl.pallas_call(
        _router_body,
        grid=(B, nt),
        in_specs=[pl.BlockSpec((1, tm, D), lambda b, i: (b, i, 0)), vec, vec, _full((D, LANES))],
        out_specs=[pl.BlockSpec((1, tm, D), lambda b, i: (b, i, 0)), col, col,
                   pl.BlockSpec((1, 8, tm), lambda b, i: (b, 0, i)), pl.BlockSpec((1, 1, 1, LANES), lambda b, i: (b, i, 0, 0))],
        out_shape=[jax.ShapeDtypeStruct((B, L, D), BF16), jax.ShapeDtypeStruct((B, L, LANES), F32),
                   jax.ShapeDtypeStruct((B, L, LANES), F32), jax.ShapeDtypeStruct((B, 8, L), F32),
                   jax.ShapeDtypeStruct((B, nt, 1, LANES), F32)],
        compiler_params=_cp("parallel", "parallel"),
        name="moe_router",
    )(x, shift, scale, wr)


def _moe_body(cnt_ref, h_ref, g_ref, rk_ref, rkt_ref, w1_ref, w3_ref, w2_ref, o_ref, xg_ref, y_ref, *, M, P):
    b, i, e, j = pl.program_id(0), pl.program_id(1), pl.program_id(2), pl.program_id(3)
    nt, ne, nj = pl.num_programs(1), pl.num_programs(2), pl.num_programs(3)
    tm = h_ref.shape[1]
    cnt = cnt_ref[(b * nt + i) * ne + e]
    n_ch = lax.div(cnt + (M - 1), M)

    @pl.when(jnp.logical_and(e == 0, j == 0))
    def _():
        o_ref[...] = jnp.zeros_like(o_ref)

    @pl.when(j == 0)
    def _():
        rkt = rkt_ref[0, pl.ds(e, 1), :]

        def gather(c, carry):
            r0 = pl.multiple_of(c * M, 16)
            rows = (lax.broadcasted_iota(jnp.int32, (M, 1), 0) + c * M).astype(F32)
            onehot = (rkt == rows).astype(BF16)
            xg_ref[pl.ds(r0, M), :] = jnp.dot(onehot, h_ref[0], preferred_element_type=F32).astype(BF16)
            return carry

        lax.fori_loop(0, n_ch, gather, 0)

    def expert(chunks):
        r0 = [pl.multiple_of(c * M, 16) for c in chunks]
        xg = [xg_ref[pl.ds(r, M), :] for r in r0]
        a = [jnp.dot(x_, w1_ref[0], preferred_element_type=F32) for x_ in xg]
        g = [jnp.dot(x_, w3_ref[0], preferred_element_type=F32) for x_ in xg]
        t = [(a_ * jax.nn.sigmoid(a_) * g_).astype(BF16) for a_, g_ in zip(a, g)]
        yv = [jnp.dot(t_, w2_ref[0], preferred_element_type=F32) for t_ in t]

        @pl.when(j == 0)
        def _():
            for r, y_ in zip(r0, yv):
                y_ref[pl.ds(r, M), :] = y_

        @pl.when(j > 0)
        def _():
            for r, y_ in zip(r0, yv):
                y_ref[pl.ds(r, M), :] += y_

    def expert_pair(c2, carry):
        expert([2 * c2, 2 * c2 + 1])
        return carry

    lax.fori_loop(0, lax.div(n_ch, 2), expert_pair, 0)

    @pl.when(lax.rem(n_ch, 2) == 1)
    def _():
        expert([n_ch - 1])

    @pl.when(j == nj - 1)
    def _():
        for p in range(tm // P):
            lane = lax.broadcasted_iota(jnp.int32, (P, LANES), 1)
            rke = jnp.sum(jnp.where(lane == e, rk_ref[0, p * P:(p + 1) * P, :], 0.0), axis=1, keepdims=True)
            ge = jnp.sum(jnp.where(lane == e, g_ref[0, p * P:(p + 1) * P, :], 0.0), axis=1, keepdims=True)

            def scatter(c, carry):
                r0 = pl.multiple_of(c * M, 16)
                cols = (lax.broadcasted_iota(jnp.int32, (1, M), 1) + c * M).astype(F32)
                onehot = (rke == cols).astype(BF16)
                yb = y_ref[pl.ds(r0, M), :].astype(BF16)
                o_ref[0, p * P:(p + 1) * P, :] += ge * jnp.dot(onehot, yb, preferred_element_type=F32)
                return carry

            lax.fori_loop(0, n_ch, scatter, 0)


def _res_ln_body(x_ref, y_ref, gate_ref, lg_ref, lb_ref, o_ref, *, alpha):
    o_ref[0] = _layernorm_rows(alpha * x_ref[0] + gate_ref[0] * y_ref[0], lg_ref[...], lb_ref[...])


def _res_ln(x, y, gate, ln_g, ln_b, alpha):
    B, L, D = x.shape
    tm = min(WIDE_ROW_TILE, L)
    row = pl.BlockSpec((1, tm, D), lambda b, i: (b, i, 0))
    return pl.pallas_call(
        functools.partial(_res_ln_body, alpha=alpha),
        grid=(B, L // tm),
        in_specs=[row, row, pl.BlockSpec((1, 1, D), lambda b, i: (b, 0, 0)), _full((1, D)), _full((1, D))],
        out_specs=row,
        out_shape=jax.ShapeDtypeStruct((B, L, D), F32),
        compiler_params=_cp("parallel", "parallel"),
        name="res_ln",
    )(x, y, gate, ln_g[None, :], ln_b[None, :])


def _moe(x, shift, scale, gate, w_router, w1, w3, w2, ln_g, ln_b, alpha):
    B, L, D = x.shape
    E, _, F = w1.shape
    hb, gts, rk, rkt, cnt = _router(x, shift, scale, w_router)
    tm = min(MOE_TOKENS, L)
    nt = L // tm
    M = MOE_ROWS
    rows_max = -(-tm // M) * M
    tf = next(t for t in (896, 512, 256, 128, F) if F % t == 0)
    counts = cnt[:, :, 0, :E].astype(jnp.int32).reshape(-1)
    row = lambda w: pl.BlockSpec((1, tm, w), lambda b, i, e, j, c: (b, i, 0))
    y = pl.pallas_call(
        functools.partial(_moe_body, M=M, P=min(512, tm)),
        grid_spec=pltpu.PrefetchScalarGridSpec(
            num_scalar_prefetch=1,
            grid=(B, nt, E, F // tf),
            in_specs=[row(D), row(LANES), row(LANES), pl.BlockSpec((1, 8, tm), lambda b, i, e, j, c: (b, 0, i)),
                      pl.BlockSpec((1, D, tf), lambda b, i, e, j, c: (e, 0, j)),
                      pl.BlockSpec((1, D, tf), lambda b, i, e, j, c: (e, 0, j)),
                      pl.BlockSpec((1, tf, D), lambda b, i, e, j, c: (e, j, 0))],
            out_specs=row(D),
            scratch_shapes=[pltpu.VMEM((rows_max, D), BF16), pltpu.VMEM((rows_max, D), F32)],
        ),
        out_shape=jax.ShapeDtypeStruct((B, L, D), F32),
        compiler_params=_cp("parallel", "parallel", "arbitrary", "arbitrary"),
        name="moe",
    )(counts, hb, gts, rk, rkt, w1.astype(BF16), w3.astype(BF16), w2.astype(BF16))
    return _res_ln(x, y, gate, ln_g, ln_b, alpha)


def _mod_body(c_ref, w_ref, b_ref, o_ref):
    c = c_ref[...]
    s = c * jax.nn.sigmoid(c)
    o_ref[...] = jnp.dot(s, w_ref[...], precision=HI, preferred_element_type=F32) + b_ref[...]


def _modulation(cc, w_mod, b_mod):
    R, D = cc.shape
    N = w_mod.shape[1]
    tn = 1024
    return pl.pallas_call(
        _mod_body,
        grid=(N // tn,),
        in_specs=[_full((R, D)), pl.BlockSpec((D, tn), lambda j: (0, j)), pl.BlockSpec((1, tn), lambda j: (0, j))],
        out_specs=pl.BlockSpec((R, tn), lambda j: (0, j)),
        out_shape=jax.ShapeDtypeStruct((R, N), F32),
        compiler_params=_cp("parallel"),
        name="modulation",
    )(cc, w_mod, b_mod[None, :])


def _streams(x, c, ctx, c_ctx, w_mod, b_mod, w_in, gla_w_gate, gla_b_gate, gla_norm_g, hy_conv_w, hy_conv_b, hy_f_w1, hy_f_b1, hy_f_freq1, hy_f_w2, hy_f_b2, hy_f_freq2, hy_f_w3, hy_f_b3, hy_skip, hy_norm_g, mla_q_norm_g, mla_w_uq, mla_kv_norm_g, mla_w_ukv, mla_norm_g, w_out, ln_g, ln_b, ffn_w1, ffn_w3, ffn_w2, moe_router, moe_w1, moe_w3, moe_w2):
    B, L, D = x.shape
    Lc = ctx.shape[1]
    depth = w_mod.shape[0]
    alpha = (2.0 * depth) ** 0.25
    cc = jnp.zeros((8, D), F32).at[:B].set(c).at[B].set(c_ctx)
    cos, sin = _rope_tables(L, True)
    cos_c, sin_c = _rope_tables(Lc, False)
    cos_all, sin_all = jnp.concatenate([cos_c, cos], axis=0), jnp.concatenate([sin_c, sin], axis=0)
    KD, VD = GLA_HEADS * GLA_DK, GLA_HEADS * GLA_DV
    xc = ctx
    for l in range(depth):
        need_ctx = l < depth - 1
        mods = _modulation(cc, w_mod[l], b_mod[l])
        m = [mods[:B, k * D:(k + 1) * D][:, None, :] for k in range(6)]
        mc = [jnp.broadcast_to(mods[B, k * D:(k + 1) * D][None, None, :], (B, 1, D)) for k in range(6)]
        w_arr = _arrange_w_in(w_in[l])
        wg, bg = _arrange_gate(gla_w_gate[l], gla_b_gate[l])
        filt = (hy_f_w1[l], hy_f_b1[l], hy_f_freq1[l], hy_f_w2[l], hy_f_b2[l], hy_f_freq2[l], hy_f_w3[l], hy_f_b3[l])
        wq = _arrange_wq(mla_w_uq[l])
        wk, wv = _arrange_wkv(mla_w_ukv[l])

        hyu, qk, vg, alr, cq, ckvr = _inproj(x, m[0], m[1], w_arr)
        hyu_c, qk_c, vg_c, alr_c, cq_c, ckvr_c = _inproj(xc, mc[0], mc[1], w_arr)

        of_c, ob_c, s_c = _gla(qk_c, vg_c, alr_c, wg, bg, jnp.zeros((B, 2, KD, VD), F32))
        of, ob, _ = _gla(qk, vg, alr, wg, bg, s_c)
        hy = _hyena(hyu, hy_conv_w[l], hy_conv_b[l], filt, hy_skip[l])
        k_all, v_all = _kvproj(jnp.concatenate([ckvr_c, ckvr], axis=1), mla_kv_norm_g[l], wk, wv, cos_all, sin_all)
        k_c, v_c = k_all[:, :Lc], v_all[:, :Lc]
        q_m = _qproj(cq, mla_q_norm_g[l], wq, cos, sin)
        om = _flash(q_m, k_all, v_all)

        x = _outproj(of, ob, vg, hy, om, x, m[2], gla_norm_g[l], hy_norm_g[l], mla_norm_g[l], w_out[l],
                     ln_g[l, 0], ln_b[l, 0], alpha)
        if need_ctx:
            hy_c = _hyena_ctx(hyu_c, hy_conv_w[l], hy_conv_b[l], filt, hy_skip[l])
            q_c = _qproj(cq_c, mla_q_norm_g[l], wq, cos_c, sin_c)
            om_c = _flash(q_c, k_c, v_c)
            xc = _outproj(of_c, ob_c, vg_c, hy_c, om_c, xc, mc[2], gla_norm_g[l], hy_norm_g[l], mla_norm_g[l],
                          w_out[l], ln_g[l, 0], ln_b[l, 0], alpha)

        i = l // 2
        if l % 2 == 0:
            x = _ffn(x, m[3], m[4], m[5], ffn_w1[i], ffn_w3[i], ffn_w2[i], ln_g[l, 1], ln_b[l, 1], alpha)
            if need_ctx:
                xc = _ffn(xc, mc[3], mc[4], mc[5], ffn_w1[i], ffn_w3[i], ffn_w2[i], ln_g[l, 1], ln_b[l, 1], alpha)
        else:
            x = _moe(x, m[3], m[4], m[5], moe_router[i], moe_w1[i], moe_w3[i], moe_w2[i], ln_g[l, 1], ln_b[l, 1], alpha)
            if need_ctx:
                xc = _moe(xc, mc[3], mc[4], mc[5], moe_router[i], moe_w1[i], moe_w3[i], moe_w2[i], ln_g[l, 1],
                          ln_b[l, 1], alpha)
    return x, xc


def kernel(x, c, ctx, c_ctx, w_mod, b_mod, w_in, gla_w_gate, gla_b_gate, gla_norm_g, hy_conv_w, hy_conv_b, hy_f_w1, hy_f_b1, hy_f_freq1, hy_f_w2, hy_f_b2, hy_f_freq2, hy_f_w3, hy_f_b3, hy_skip, hy_norm_g, mla_q_norm_g, mla_w_uq, mla_kv_norm_g, mla_w_ukv, mla_norm_g, w_out, ln_g, ln_b, ffn_w1, ffn_w3, ffn_w2, moe_router, moe_w1, moe_w3, moe_w2):
    return _streams(x, c, ctx, c_ctx, w_mod, b_mod, w_in, gla_w_gate, gla_b_gate, gla_norm_g, hy_conv_w, hy_conv_b, hy_f_w1, hy_f_b1, hy_f_freq1, hy_f_w2, hy_f_b2, hy_f_freq2, hy_f_w3, hy_f_b3, hy_skip, hy_norm_g, mla_q_norm_g, mla_w_uq, mla_kv_norm_g, mla_w_ukv, mla_norm_g, w_out, ln_g, ln_b, ffn_w1, ffn_w3, ffn_w2, moe_router, moe_w1, moe_w3, moe_w2)[0]
```

```python
import functools
import math

import numpy as np
import jax
import jax.numpy as jnp
from jax import lax
from jax.experimental import pallas as pl
from jax.experimental.pallas import tpu as pltpu

F32 = jnp.float32
BF16 = jnp.bfloat16
HI = lax.Precision.HIGHEST

GRID_W = 64
GLA_HEADS, GLA_DK, GLA_DV, GLA_RANK, GLA_TAU = 4, 32, 64, 16, 16.0
HY_CH, HY_EMB = 256, 33
HY_DECAY_TARGET, HY_FAST_DECAY, HY_SLOW_DECAY = 1e-2, 0.3, 1.5
MLA_HEADS, MLA_Q_RANK, MLA_KV_RANK, MLA_NOPE, MLA_ROPE, MLA_V = 8, 256, 128, 64, 32, 64
MLA_SCALE = (MLA_NOPE + MLA_ROPE) ** -0.5
ROPE_BASE = 10000.0
N_EXPERTS = 8
IN_SPLITS = (128, 128, 256, 256, 32, 768, 256, 128, 32)

LANES = 128
SUBLANES = 8
VMEM_LIMIT = 56 * 1024 * 1024

ROW_TILE = 512
WIDE_ROW_TILE = 1024
GLA_CHUNK = 128
DFT_N2 = 256


def _cp(*sem):
    return pltpu.CompilerParams(dimension_semantics=sem, vmem_limit_bytes=VMEM_LIMIT)


def _full(shape):
    n = len(shape)
    return pl.BlockSpec(shape, lambda *_: (0,) * n)


def _idiv(x, d):
    assert d & (d - 1) == 0
    return lax.shift_right_logical(x, int(math.log2(d)))


INPROJ_WIDTHS = (768, 256, 512, 128, 256, 256)


def _arrange_w_in(w):
    cuts = np.cumsum(IN_SPLITS)[:-1]
    qa, ka, va, ga, alr, hyu, cq, ckv, kr = jnp.split(w, [int(c) for c in cuts], axis=1)
    z96 = jnp.zeros((w.shape[0], 96), w.dtype)
    return jnp.concatenate([hyu, qa, ka, va, ga, alr, z96, cq, ckv, kr, z96], axis=1).astype(BF16)


def _inproj_body(x_ref, sh_ref, sc_ref, w_ref, *out_refs):
    h = x_ref[0] * (1.0 + sc_ref[0]) + sh_ref[0]
    acc = jnp.dot(h.astype(BF16), w_ref[...], preferred_element_type=F32)
    off = 0
    for r in out_refs:
        w = r.shape[-1]
        r[0] = acc[:, off:off + w].astype(r.dtype)
        off += w


def _inproj(x, shift, scale, w_arr):
    B, L, D = x.shape
    tm = min(ROW_TILE, L)
    n = w_arr.shape[1]
    row = lambda w: pl.BlockSpec((1, tm, w), lambda b, i: (b, i, 0))
    vec = pl.BlockSpec((1, 1, D), lambda b, i: (b, 0, 0))
    return pl.pallas_call(
        _inproj_body,
        grid=(B, L // tm),
        in_specs=[row(D), vec, vec, _full((D, n))],
        out_specs=[row(w) for w in INPROJ_WIDTHS],
        out_shape=[jax.ShapeDtypeStruct((B, L, w), BF16) for w in INPROJ_WIDTHS],
        compiler_params=_cp("parallel", "parallel"),
        name="inproj",
    )(x, shift, scale, w_arr)


def _log_sigmoid(z):
    return jnp.minimum(z, 0.0) - jnp.log1p(jnp.exp(-jnp.abs(z)))


def _gla_body(qkf_ref, vf_ref, af_ref, qkb_ref, vb_ref, ab_ref, wg_ref, bg_ref, s0_ref,
              of_ref, ob_ref, sout_ref, s_ref):
    i = pl.program_id(0)
    C = qkf_ref.shape[1]
    KD = GLA_HEADS * GLA_DK
    VD = GLA_HEADS * GLA_DV

    @pl.when(i == 0)
    def _():
        s_ref[...] = s0_ref[...]

    r = lax.broadcasted_iota(jnp.int32, (C, C), 0)
    c = lax.broadcasted_iota(jnp.int32, (C, C), 1)
    tris = ((c <= r).astype(F32), (c >= r).astype(F32))
    lane_k = _idiv(lax.broadcasted_iota(jnp.int32, (1, KD), 1), GLA_DK)
    lane_v = _idiv(lax.broadcasted_iota(jnp.int32, (1, VD), 1), GLA_DV)
    rk = _idiv(lax.broadcasted_iota(jnp.int32, (KD, VD), 0), GLA_DK)
    cv = _idiv(lax.broadcasted_iota(jnp.int32, (KD, VD), 1), GLA_DV)
    ones = jnp.ones((C, VD), F32)
    refs = ((qkf_ref, vf_ref, af_ref, of_ref), (qkb_ref, vb_ref, ab_ref, ob_ref))
    chains = [(b, d) for b in range(qkf_ref.shape[0]) for d in range(2)]

    z = [jnp.dot(refs[d][2][b].astype(F32), wg_ref[...], precision=HI, preferred_element_type=F32) + bg_ref[...]
         for b, d in chains]
    la = [_log_sigmoid(zz[:, d * KD:(d + 1) * KD]) / GLA_TAU for zz, (b, d) in zip(z, chains)]
    bb = [jnp.dot(tris[d], l_, precision=HI, preferred_element_type=F32) for l_, (b, d) in zip(la, chains)]
    tot_b = [lax.dot_general(l_, ones, (((0,), (0,)), ((), ())), precision=HI, preferred_element_type=F32) for l_ in la]
    qe, ke, kl, vb, s_old = [], [], [], [], []
    for n, (b, d) in enumerate(chains):
        qk = refs[d][0][b].astype(F32)
        q = qk[:, :KD] * (GLA_DK ** -0.5)
        k = qk[:, KD:]
        tot = jnp.sum(la[n], axis=0, keepdims=True)
        qe.append(q * jnp.exp(bb[n]))
        ke.append((k * jnp.exp(-bb[n])).astype(BF16))
        kl.append((k * jnp.exp(tot - bb[n])).astype(BF16))
        vb.append(refs[d][1][b].astype(BF16))
        s_old.append(s_ref[2 * b + d])
    o = [jnp.dot(qe[n].astype(BF16), s_old[n].astype(BF16), preferred_element_type=F32) for n in range(len(chains))]
    att = [[lax.dot_general(jnp.where(lane_k == h, qe[n], 0.0).astype(BF16), ke[n], (((1,), (1,)), ((), ())),
                            preferred_element_type=F32) for h in range(GLA_HEADS)] for n in range(len(chains))]
    kv = [lax.dot_general(kl[n], vb[n], (((0,), (0,)), ((), ())), preferred_element_type=F32) for n in range(len(chains))]
    for n, (b, d) in enumerate(chains):
        on = o[n]
        for h in range(GLA_HEADS):
            oh = jnp.dot((att[n][h] * tris[d]).astype(BF16), vb[n], preferred_element_type=F32)
            on = on + jnp.where(lane_v == h, oh, 0.0)
        refs[d][3][b] = on.astype(refs[d][3].dtype)
        s_ref[2 * b + d] = jnp.exp(tot_b[n]) * s_old[n] + jnp.where(rk == cv, kv[n], 0.0)

    @pl.when(i == pl.num_programs(0) - 1)
    def _():
        sout_ref[...] = s_ref[...]


def _gla(qk, vg, alr, wg, bg, s0):
    B, L, _ = qk.shape
    C = min(GLA_CHUNK, L)
    n = L // C
    KD, VD = GLA_HEADS * GLA_DK, GLA_HEADS * GLA_DV
    fwd = lambda w: pl.BlockSpec((B, C, w), lambda i: (0, i, 0))
    bwd = lambda w: pl.BlockSpec((B, C, w), lambda i: (0, n - 1 - i, 0))
    st = _full((2 * B, KD, VD))
    of, ob, s_out = pl.pallas_call(
        _gla_body,
        grid=(n,),
        in_specs=[fwd(2 * KD), fwd(VD), fwd(LANES), bwd(2 * KD), bwd(VD), bwd(LANES),
                  _full((LANES, 2 * KD)), _full((1, 2 * KD)), st],
        out_specs=[fwd(VD), bwd(VD), st],
        out_shape=[jax.ShapeDtypeStruct((B, L, VD), BF16), jax.ShapeDtypeStruct((B, L, VD), BF16),
                   jax.ShapeDtypeStruct((2 * B, KD, VD), F32)],
        scratch_shapes=[pltpu.VMEM((2 * B, KD, VD), F32)],
        compiler_params=_cp("arbitrary"),
        name="gla",
    )(qk, vg, alr, qk, vg, alr, wg, bg, s0.reshape(2 * B, KD, VD))
    return of, ob, s_out.reshape(B, 2, KD, VD)


def _arrange_gate(w_gate, b_gate):
    KD = GLA_HEADS * GLA_DK
    wg = jnp.zeros((LANES, 2 * KD), F32)
    wg = wg.at[:GLA_RANK, :KD].set(w_gate[0]).at[GLA_RANK:2 * GLA_RANK, KD:].set(w_gate[1])
    return wg, jnp.concatenate([b_gate[0], b_gate[1]])[None, :]


def _shortconv_body(x_ref, p_ref, n_ref, w_ref, b_ref, v_ref, x1_ref, x2_ref):
    i = pl.program_id(1)
    last = pl.num_programs(1) - 1
    x = x_ref[0].astype(F32)
    tm = x.shape[0]
    hb = p_ref.shape[1]
    prev = jnp.where(i > 0, p_ref[0].astype(F32)[hb - 1:hb, :], 0.0)
    nxt = jnp.where(i < last, n_ref[0].astype(F32)[0:1, :], 0.0)
    rid = lax.broadcasted_iota(jnp.int32, x.shape, 0)
    dn = jnp.where(rid == 0, prev, pltpu.roll(x, 1, 0))
    up = jnp.where(rid == tm - 1, nxt, pltpu.roll(x, tm - 1, 0))
    w = w_ref[...]
    y = b_ref[...] + dn * w[0:1] + x * w[1:2] + up * w[2:3]
    v_ref[0] = y[:, :HY_CH]
    x1_ref[0] = y[:, HY_CH:2 * HY_CH]
    x2_ref[0] = y[:, 2 * HY_CH:]


def _shortconv(u, w, b):
    B, L, W = u.shape
    tm = min(ROW_TILE, L)
    hb = 2 * SUBLANES
    nb = tm // hb
    row = pl.BlockSpec((1, tm, W), lambda b_, i: (b_, i, 0))
    prev = pl.BlockSpec((1, hb, W), lambda b_, i: (b_, jnp.maximum(i * nb - 1, 0), 0))
    nxt = pl.BlockSpec((1, hb, W), lambda b_, i: (b_, jnp.minimum((i + 1) * nb, L // hb - 1), 0))
    o = pl.BlockSpec((1, tm, HY_CH), lambda b_, i: (b_, i, 0))
    return pl.pallas_call(
        _shortconv_body,
        grid=(B, L // tm),
        in_specs=[row, prev, nxt, _full((3, W)), _full((1, W))],
        out_specs=[o, o, o],
        out_shape=[jax.ShapeDtypeStruct((B, L, HY_CH), F32)] * 3,
        compiler_params=_cp("parallel", "parallel"),
        name="shortconv",
    )(u, u, u, w, b[None, :])


def _filter_feats(L):
    pos = np.arange(L, dtype=np.float64)
    t = pos / (L - 1)
    bands = (HY_EMB - 1) // 2
    freqs = np.linspace(1e-4, bands - 1, bands)
    ang = (2.0 * math.pi * pos / L)[:, None] * freqs
    z = jnp.asarray(np.concatenate([t[:, None], np.cos(ang), -np.sin(ang)], axis=-1), dtype=F32)
    z = jnp.pad(z, ((0, 0), (0, LANES - HY_EMB)))
    deltas = np.abs(np.linspace(math.log(HY_DECAY_TARGET) / HY_SLOW_DECAY,
                                math.log(HY_DECAY_TARGET) / HY_FAST_DECAY, HY_CH))
    return z, jnp.asarray(np.tile(deltas, 4)[None, :], dtype=F32)


def _filter_body(z_ref, w1_ref, b1_ref, f1_ref, w2_ref, b2_ref, f2_ref, w3_ref, b3_ref, dl_ref,
                 h_ref, ss_ref, *, L):
    i = pl.program_id(0)
    z = z_ref[...]
    tm = z.shape[0]
    hid = jnp.sin(f1_ref[...] * (jnp.dot(z, w1_ref[...], precision=HI, preferred_element_type=F32) + b1_ref[...]))
    hid = jnp.sin(f2_ref[...] * (jnp.dot(hid, w2_ref[...], precision=HI, preferred_element_type=F32) + b2_ref[...]))
    h = jnp.dot(hid, w3_ref[...], precision=HI, preferred_element_type=F32) + b3_ref[...]
    pos = (lax.broadcasted_iota(jnp.int32, (tm, 1), 0) + i * tm).astype(F32)
    t = pos / (L - 1)
    h = h * jnp.exp(-t * dl_ref[...])

    @pl.when(i == 0)
    def _():
        ss_ref[...] = jnp.zeros_like(ss_ref)

    ss_ref[...] += jnp.sum(h * h, axis=0, keepdims=True)
    col = lax.broadcasted_iota(jnp.int32, h.shape, 1)
    is_bwd = (_idiv(col, HY_CH) & 1) == 1
    h_ref[...] = jnp.where(jnp.logical_and(is_bwd, pos == 0.0), 0.0, h)


def _filters(L, fw1, fb1, ff1, fw2, fb2, ff2, fw3, fb3):
    z, dl = _filter_feats(L)
    tm = min(WIDE_ROW_TILE, L)
    Hf = fw2.shape[0]
    w1 = jnp.pad(fw1, ((0, LANES - HY_EMB), (0, 0)))
    NC = fw3.shape[1]
    return pl.pallas_call(
        functools.partial(_filter_body, L=L),
        grid=(L // tm,),
        in_specs=[pl.BlockSpec((tm, LANES), lambda i: (i, 0)), _full((LANES, Hf)), _full((1, Hf)), _full((1, Hf)),
                  _full((Hf, Hf)), _full((1, Hf)), _full((1, Hf)), _full((Hf, NC)), _full((1, NC)), _full((1, NC))],
        out_specs=[pl.BlockSpec((tm, NC), lambda i: (i, 0)), _full((1, NC))],
        out_shape=[jax.ShapeDtypeStruct((L, NC), F32), jax.ShapeDtypeStruct((1, NC), F32)],
        compiler_params=_cp("arbitrary"),
        name="hy_filters",
    )(z, w1, fb1[None], ff1[None], fw2, fb2[None], ff2[None], fw3, fb3[None], dl)


def _dft_consts(L):
    N = 2 * L
    N2 = DFT_N2
    N1 = N // N2
    half = N1 // 2
    k1 = np.arange(N1)[:, None].astype(np.float64)
    n1 = np.arange(N1)[None, :].astype(np.float64)
    a1 = 2.0 * np.pi * k1 * n1 / N1
    f1r, f1i = np.cos(a1), -np.sin(a1)
    fa = np.concatenate([f1r[:, :half], f1i[:, :half]], axis=0)
    fb = np.concatenate([f1r[:half, :], f1i[:half, :]], axis=1) / N
    k2 = np.arange(N2)[:, None].astype(np.float64)
    n2 = np.arange(N2)[None, :].astype(np.float64)
    a2 = 2.0 * np.pi * k2 * n2 / N2
    f2r, f2i = np.cos(a2), -np.sin(a2)
    g = np.block([[f2r, -f2i], [f2i, f2r]])
    gc = np.block([[f2r, f2i], [-f2i, f2r]])
    at = 2.0 * np.pi * (np.arange(N1)[:, None] * np.arange(N2)[None, :] % N) / N
    twr, twi = np.cos(at), -np.sin(at)
    c = lambda a: jnp.asarray(a, dtype=F32)
    bc = lambda a: jnp.broadcast_to(c(a)[:, :, None], (N1, N2, LANES))
    eye = np.eye(SUBLANES)
    return dict(N1=N1, N2=N2, half=half, fa=c(np.kron(fa, eye)), fb=c(np.kron(fb, eye)), g=c(g), gc=c(gc),
                twr=bc(twr), twi=bc(twi))


def _lanes(t, width):
    return jnp.concatenate([t] * (width // LANES), axis=-1)


def _dft1_body(f_ref, x_ref, o_ref):
    x = x_ref[0]
    x2 = x.reshape(x.shape[0] * SUBLANES, x.shape[2]).astype(BF16)
    y = jnp.dot(f_ref[...], x2, preferred_element_type=F32)
    o_ref[0] = y.reshape(o_ref.shape[1], SUBLANES, y.shape[1])


def _dft_stage1(fa, x):
    B, half, N2, W = x.shape
    R = fa.shape[0] // SUBLANES
    return pl.pallas_call(
        _dft1_body,
        grid=(B, N2 // SUBLANES),
        in_specs=[_full(fa.shape), pl.BlockSpec((1, half, SUBLANES, W), lambda b, j: (b, 0, j, 0))],
        out_specs=pl.BlockSpec((1, R, SUBLANES, W), lambda b, j: (b, 0, j, 0)),
        out_shape=jax.ShapeDtypeStruct((B, R, N2, W), F32),
        compiler_params=_cp("parallel", "parallel"),
        name="hy_dft1",
    )(fa.astype(BF16), x)


def _filter_spec_body(a_ref, twr_ref, twi_ref, g_ref, ss_ref, hf_ref):
    W = a_ref.shape[-1]
    ar, ai = a_ref[0, 0], a_ref[1, 0]
    twr, twi = _lanes(twr_ref[0], W), _lanes(twi_ref[0], W)
    xr = ar * twr - ai * twi
    xi = ar * twi + ai * twr
    z = jnp.dot(g_ref[...], jnp.concatenate([xr, xi], axis=0).astype(BF16), preferred_element_type=F32)
    n2 = z.shape[0] // 2
    zr, zi = z[:n2], z[n2:]
    ss = ss_ref[...]
    for o in range(2):
        f0, b0 = (2 * o) * HY_CH, (2 * o + 1) * HY_CH
        sc = lax.rsqrt(ss[:, f0:f0 + HY_CH] + ss[:, b0:b0 + HY_CH] + 1e-6)
        hf_ref[o, 0, 0] = ((zr[:, f0:f0 + HY_CH] + zr[:, b0:b0 + HY_CH]) * sc).astype(hf_ref.dtype)
        hf_ref[o, 0, 1] = ((zi[:, f0:f0 + HY_CH] - zi[:, b0:b0 + HY_CH]) * sc).astype(hf_ref.dtype)


def _filter_spectrum(h, ss, dc):
    L, NC = h.shape
    N1, N2, half = dc["N1"], dc["N2"], dc["half"]
    a = _dft_stage1(dc["fa"], h.reshape(1, half, N2, NC))
    a = a.reshape(2, N1, N2, NC)
    return pl.pallas_call(
        _filter_spec_body,
        grid=(N1,),
        in_specs=[pl.BlockSpec((2, 1, N2, NC), lambda k: (0, k, 0, 0)),
                  pl.BlockSpec((1, N2, LANES), lambda k: (k, 0, 0)), pl.BlockSpec((1, N2, LANES), lambda k: (k, 0, 0)),
                  _full((2 * N2, 2 * N2)), _full((1, NC))],
        out_specs=pl.BlockSpec((2, 1, 2, N2, HY_CH), lambda k: (0, k, 0, 0, 0)),
        out_shape=jax.ShapeDtypeStruct((2, N1, 2, N2, HY_CH), BF16),
        compiler_params=_cp("parallel"),
        name="hy_filter_spec",
    )(a, dc["twr"], dc["twi"], dc["g"].astype(BF16), ss)


SPEC_K1 = 8


def _spec_mul_body(a_ref, twr_ref, twi_ref, g_ref, gc_ref, hf_ref, o_ref):
    W = a_ref.shape[-1]
    ks = range(a_ref.shape[2])
    n2 = g_ref.shape[0] // 2
    x = []
    for k in ks:
        ar, ai = a_ref[0, 0, k], a_ref[0, 1, k]
        twr, twi = _lanes(twr_ref[k], W), _lanes(twi_ref[k], W)
        x.append(jnp.concatenate([ar * twr - ai * twi, ar * twi + ai * twr], axis=0).astype(BF16))
    z = [jnp.dot(g_ref[...], x_, preferred_element_type=F32) for x_ in x]
    y = []
    for k, z_ in zip(ks, z):
        zr, zi = z_[:n2], z_[n2:]
        hr, hi = hf_ref[0, k, 0].astype(F32), hf_ref[0, k, 1].astype(F32)
        y.append(jnp.concatenate([zr * hr - zi * hi, zr * hi + zi * hr], axis=0).astype(BF16))
    b = [jnp.dot(gc_ref[...], y_, preferred_element_type=F32) for y_ in y]
    for k, b_ in zip(ks, b):
        br, bi = b_[:n2], b_[n2:]
        twr, twi = _lanes(twr_ref[k], W), _lanes(twi_ref[k], W)
        o_ref[0, 0, k] = br * twr + bi * twi
        o_ref[0, 1, k] = bi * twr - br * twi


def _spec_mul(a, hf, order, dc):
    B = a.shape[0]
    N1, N2 = dc["N1"], dc["N2"]
    C = a.shape[-1]
    kb = min(SPEC_K1, N1)
    blk = pl.BlockSpec((1, 2, kb, N2, C), lambda k, b: (b, 0, k, 0, 0))
    tw = pl.BlockSpec((kb, N2, LANES), lambda k, b: (k, 0, 0))
    return pl.pallas_call(
        _spec_mul_body,
        grid=(N1 // kb, B),
        in_specs=[blk, tw, tw, _full((2 * N2, 2 * N2)), _full((2 * N2, 2 * N2)),
                  pl.BlockSpec((1, kb, 2, N2, C), lambda k, b: (order, k, 0, 0, 0))],
        out_specs=blk,
        out_shape=jax.ShapeDtypeStruct(a.shape, F32),
        compiler_params=_cp("parallel", "parallel"),
        name="hy_spec_mul",
    )(a, dc["twr"], dc["twi"], dc["g"].astype(BF16), dc["gc"].astype(BF16), hf)


def _dft3_body(f_ref, b_ref, u_ref, gate_ref, skip_ref, o_ref):
    bm = b_ref[0]
    b2 = bm.reshape(bm.shape[0] * SUBLANES, bm.shape[2]).astype(BF16)
    y = jnp.dot(f_ref[...], b2, preferred_element_type=F32)
    rows, C = y.shape
    u = u_ref[0].reshape(rows, C)
    gate = gate_ref[0].reshape(rows, C)
    o_ref[0] = (gate * (y + u * skip_ref[...])).reshape(o_ref.shape[1], SUBLANES, C)


def _dft_stage3(fb, bm, u, gate, skip):
    B, R, N2, C = bm.shape
    half = u.shape[1]
    row = pl.BlockSpec((1, half, SUBLANES, C), lambda b, j: (b, 0, j, 0))
    return pl.pallas_call(
        _dft3_body,
        grid=(B, N2 // SUBLANES),
        in_specs=[_full(fb.shape), pl.BlockSpec((1, R, SUBLANES, C), lambda b, j: (b, 0, j, 0)), row, row, _full((1, C))],
        out_specs=row,
        out_shape=jax.ShapeDtypeStruct((B, half, N2, C), F32),
        compiler_params=_cp("parallel", "parallel"),
        name="hy_dft3",
    )(fb.astype(BF16), bm, u, gate, skip[None, :])


def _longconv_gated(u, gate, hf, order, skip, dc):
    B, L, C = u.shape
    N1, N2, half = dc["N1"], dc["N2"], dc["half"]
    u4 = u.reshape(B, half, N2, C)
    a = _dft_stage1(dc["fa"], u4).reshape(B, 2, N1, N2, C)
    bm = _spec_mul(a, hf, order, dc).reshape(B, 2 * N1, N2, C)
    return _dft_stage3(dc["fb"], bm, u4, gate.reshape(B, half, N2, C), skip).reshape(B, L, C)


def _hyena(hyu, conv_w, conv_b, filt, skip):
    B, L, _ = hyu.shape
    v, x1, x2 = _shortconv(hyu, conv_w, conv_b)
    h, ss = _filters(L, *filt)
    dc = _dft_consts(L)
    hf = _filter_spectrum(h, ss, dc)
    z1 = _longconv_gated(v, x1, hf, 0, skip[0], dc)
    return _longconv_gated(z1, x2, hf, 1, skip[1], dc)


def _hyena_ctx_body(v_ref, x1_ref, x2_ref, h_ref, ss_ref, skip_ref, fc_ref, gc_ref, o_ref):
    fc, gc = fc_ref[...], gc_ref[...]
    n = fc.shape[0] // 2
    ss = ss_ref[...]
    h = h_ref[...]

    def conv(u, o):
        f0, b0 = (2 * o) * HY_CH, (2 * o + 1) * HY_CH
        sc = lax.rsqrt(ss[:, f0:f0 + HY_CH] + ss[:, b0:b0 + HY_CH] + 1e-6)
        x = jnp.dot(fc, u, precision=HI, preferred_element_type=F32)
        hf = jnp.dot(fc, h[:, f0:f0 + HY_CH], precision=HI, preferred_element_type=F32)
        hb = jnp.dot(fc, h[:, b0:b0 + HY_CH], precision=HI, preferred_element_type=F32)
        hr = (hf[:n] + hb[:n]) * sc
        hi = (hf[n:] - hb[n:]) * sc
        yr = x[:n] * hr - x[n:] * hi
        yi = x[:n] * hi + x[n:] * hr
        y = jnp.dot(gc, jnp.concatenate([yr, yi], axis=0), precision=HI, preferred_element_type=F32)
        return y + u * skip_ref[o:o + 1, :]

    z1 = x1_ref[0] * conv(v_ref[0], 0)
    o_ref[0] = x2_ref[0] * conv(z1, 1)


def _hyena_ctx(hyu, conv_w, conv_b, filt, skip):
    B, L, _ = hyu.shape
    v, x1, x2 = _shortconv(hyu, conv_w, conv_b)
    h, ss = _filters(L, *filt)
    N = 2 * L
    ang = 2.0 * np.pi * (np.arange(N)[:, None] * np.arange(L)[None, :] % N) / N
    fr, fi = np.cos(ang), -np.sin(ang)
    fc = jnp.asarray(np.concatenate([fr, fi], axis=0), dtype=F32)
    gc = jnp.asarray(np.concatenate([fr.T, fi.T], axis=1) / N, dtype=F32)
    row = pl.BlockSpec((1, L, HY_CH), lambda b: (b, 0, 0))
    return pl.pallas_call(
        _hyena_ctx_body,
        grid=(B,),
        in_specs=[row, row, row, _full(h.shape), _full(ss.shape), _full(skip.shape), _full(fc.shape), _full(gc.shape)],
        out_specs=row,
        out_shape=jax.ShapeDtypeStruct((B, L, HY_CH), F32),
        compiler_params=_cp("parallel"),
        name="hyena_ctx",
    )(v, x1, x2, h, ss, skip, fc, gc)


HEAD_PAD = 128


def _rope_swap(w):
    a, b, c, d = w[..., 0:8], w[..., 8:16], w[..., 16:24], w[..., 24:32]
    return jnp.concatenate([-b, a, -d, c], axis=-1)


def _arrange_wq(w_uq):
    R = w_uq.shape[0]
    w = w_uq.reshape(R, MLA_HEADS, MLA_NOPE + MLA_ROPE)
    rope = w[..., MLA_NOPE:]
    out = jnp.concatenate([w[..., :MLA_NOPE], rope, _rope_swap(rope)], axis=-1)
    return out.reshape(R, MLA_HEADS * HEAD_PAD).astype(BF16)


def _arrange_wkv(w_ukv):
    R = w_ukv.shape[0]
    w = w_ukv.reshape(R, MLA_HEADS, MLA_NOPE + MLA_V)
    wk = jnp.concatenate([w[..., :MLA_NOPE], jnp.zeros((R, MLA_HEADS, HEAD_PAD - MLA_NOPE), w.dtype)], axis=-1)
    wv = w[..., MLA_NOPE:]
    return wk.reshape(R, MLA_HEADS * HEAD_PAD).astype(BF16), wv.reshape(R, MLA_HEADS * MLA_V).astype(BF16)


def _kr_place():
    e = np.zeros((LANES, MLA_HEADS * HEAD_PAD), np.float32)
    es = np.zeros((LANES, MLA_HEADS * HEAD_PAD), np.float32)
    for h in range(MLA_HEADS):
        base = h * HEAD_PAD + MLA_NOPE
        for j in range(MLA_ROPE):
            e[j, base + j] = 1.0
            blk, r = divmod(j, 16)
            if r < 8:
                es[16 * blk + r + 8, base + j] = -1.0
            else:
                es[16 * blk + r - 8, base + j] = 1.0
    return jnp.asarray(e).astype(BF16), jnp.asarray(es).astype(BF16)


def _rope_tables(L, rope):
    if rope:
        t = np.arange(L)
        row, col = (t // GRID_W).astype(np.float64), (t % GRID_W).astype(np.float64)
        half = MLA_ROPE // 2
        inv = ROPE_BASE ** (-np.arange(0, half, 2, dtype=np.float64) / half)
        ar, ac = row[:, None] * inv, col[:, None] * inv
        cos = jnp.asarray(np.concatenate([np.cos(ar), np.cos(ar), np.cos(ac), np.cos(ac)], axis=-1), dtype=F32)
        sin = jnp.asarray(np.concatenate([np.sin(ar), np.sin(ar), np.sin(ac), np.sin(ac)], axis=-1), dtype=F32)
    else:
        cos, sin = jnp.ones((L, MLA_ROPE), F32), jnp.zeros((L, MLA_ROPE), F32)
    return cos, sin


def _rms_rows(x, g, eps=1e-6):
    return x * lax.rsqrt(jnp.mean(x * x, axis=-1, keepdims=True) + eps) * g


def _qproj_body(cq_ref, g_ref, w_ref, t1_ref, t2_ref, q_ref):
    xn = _rms_rows(cq_ref[0].astype(F32), g_ref[...])
    acc = jnp.dot(xn.astype(BF16), w_ref[...], preferred_element_type=F32)
    W = acc.shape[1]
    t1, t2 = _lanes(t1_ref[...], W), _lanes(t2_ref[...], W)
    q_ref[0] = (acc * t1 + pltpu.roll(acc, W - MLA_ROPE, 1) * t2).astype(q_ref.dtype)


def _qproj(cq, g, wq, cos, sin):
    B, L, R = cq.shape
    tm = min(ROW_TILE, L)
    W = wq.shape[1]
    ones, zeros = jnp.ones((L, MLA_NOPE), F32), jnp.zeros((L, MLA_ROPE), F32)
    qs = MLA_SCALE * math.log2(math.e)
    t1 = jnp.concatenate([ones, cos, zeros], axis=-1) * qs
    t2 = jnp.concatenate([jnp.zeros((L, MLA_NOPE), F32), sin, zeros], axis=-1) * qs
    tab = pl.BlockSpec((tm, HEAD_PAD), lambda b, i: (i, 0))
    return pl.pallas_call(
        _qproj_body,
        grid=(B, L // tm),
        in_specs=[pl.BlockSpec((1, tm, R), lambda b, i: (b, i, 0)), _full((1, R)), _full((R, W)), tab, tab],
        out_specs=pl.BlockSpec((1, tm, W), lambda b, i: (b, i, 0)),
        out_shape=jax.ShapeDtypeStruct((B, L, W), BF16),
        compiler_params=_cp("parallel", "parallel"),
        name="mla_qproj",
    )(cq, g[None, :], wq, t1, t2)


def _kvproj_body(c_ref, g_ref, wk_ref, wv_ref, e_ref, es_ref, cos_ref, sin_ref, k_ref, v_ref):
    c = c_ref[0].astype(F32)
    R = MLA_KV_RANK
    xn = _rms_rows(c[:, :R], g_ref[...]).astype(BF16)
    kr = c[:, R:]
    acc = jnp.dot(xn, wk_ref[...], preferred_element_type=F32)
    acc += jnp.dot((kr * cos_ref[...]).astype(BF16), e_ref[...], preferred_element_type=F32)
    acc += jnp.dot((kr * sin_ref[...]).astype(BF16), es_ref[...], preferred_element_type=F32)
    k_ref[0] = acc.astype(k_ref.dtype)
    v_ref[0] = jnp.dot(xn, wv_ref[...], preferred_element_type=F32).astype(v_ref.dtype)


def _kvproj(ckvr, g, wk, wv, cos, sin):
    B, L, Wc = ckvr.shape
    tm = next(t for t in (1280, 512, 256, L) if L % t == 0)
    pad = jnp.zeros((L, LANES - MLA_ROPE), F32)
    cos_p, sin_p = jnp.concatenate([cos, pad], axis=-1), jnp.concatenate([sin, pad], axis=-1)
    e, es = _kr_place()
    tab = pl.BlockSpec((tm, LANES), lambda b, i: (i, 0))
    Wk, Wv = wk.shape[1], wv.shape[1]
    return pl.pallas_call(
        _kvproj_body,
        grid=(B, L // tm),
        in_specs=[pl.BlockSpec((1, tm, Wc), lambda b, i: (b, i, 0)), _full((1, MLA_KV_RANK)),
                  _full(wk.shape), _full(wv.shape), _full(e.shape), _full(es.shape), tab, tab],
        out_specs=[pl.BlockSpec((1, tm, Wk), lambda b, i: (b, i, 0)), pl.BlockSpec((1, tm, Wv), lambda b, i: (b, i, 0))],
        out_shape=[jax.ShapeDtypeStruct((B, L, Wk), BF16), jax.ShapeDtypeStruct((B, L, Wv), BF16)],
        compiler_params=_cp("parallel", "parallel"),
        name="mla_kvproj",
    )(ckvr, g[None, :], wk, wv, e, es, cos_p, sin_p)


FLASH_Q_TILE = 2048
FLASH_ROWS = 256
FLASH_KEYS = 256


def _flash_body(q_ref, k_ref, v_ref, o_ref, m_ref, l_ref, acc_ref, s_ref, *, R):
    j = pl.program_id(3)
    tq, tk = q_ref.shape[1], k_ref.shape[1]
    CK = FLASH_KEYS
    npc = CK // LANES

    @pl.when(j == 0)
    def _():
        m_ref[...] = jnp.full_like(m_ref, -jnp.inf)
        l_ref[...] = jnp.zeros_like(l_ref)
        acc_ref[...] = jnp.zeros_like(acc_ref)

    def pass1(a, r):
        lo, r0 = a * HEAD_PAD, r * R
        q = q_ref[0, r0:r0 + R, lo:lo + HEAD_PAD]
        mp = None
        for c in range(tk // CK):
            kc = k_ref[0, c * CK:(c + 1) * CK, lo:lo + HEAD_PAD]
            s = lax.dot_general(q, kc, (((1,), (1,)), ((), ())), preferred_element_type=F32)
            s_ref[r0:r0 + R, c * CK:(c + 1) * CK] = s
            for w in range(npc):
                pc = s[:, w * LANES:(w + 1) * LANES]
                mp = pc if mp is None else jnp.maximum(mp, pc)
        m_old = m_ref[a, r0:r0 + R, :]
        return m_old, jnp.maximum(m_old, jnp.max(mp, axis=1, keepdims=True))

    def pass2(a, r, m_old, m_new):
        r0 = r * R
        alpha = jnp.exp2(m_old - m_new)
        lp = jnp.zeros((R, LANES), F32)
        pv = jnp.zeros((R, 2 * MLA_V), F32)
        for c in range(tk // CK):
            s = s_ref[r0:r0 + R, c * CK:(c + 1) * CK]
            ps = [jnp.exp2(s[:, w * LANES:(w + 1) * LANES] - m_new) for w in range(npc)]
            for p_ in ps:
                lp = lp + p_
            p = jnp.concatenate(ps, axis=1).astype(BF16)
            pv = pv + jnp.dot(p, v_ref[0, c * CK:(c + 1) * CK, :], preferred_element_type=F32)
        l_ref[a, r0:r0 + R, :] = alpha * l_ref[a, r0:r0 + R, :] + jnp.sum(lp, axis=1, keepdims=True)
        acc_ref[a, r0:r0 + R, :] = alpha * acc_ref[a, r0:r0 + R, :] + pv
        m_ref[a, r0:r0 + R, :] = m_new

    assert tq // R >= 2
    blocks = [(a, r) for a in range(2) for r in range(tq // R)]
    pend = pass1(*blocks[0])
    for i, blk in enumerate(blocks):
        nxt = pass1(*blocks[i + 1]) if i + 1 < len(blocks) else None
        pass2(*blk, *pend)
        pend = nxt

    @pl.when(j == pl.num_programs(3) - 1)
    def _():
        lane = lax.broadcasted_iota(jnp.int32, acc_ref.shape[1:], 1)
        o_ref[0] = jnp.where(lane < MLA_V, acc_ref[0] / l_ref[0], acc_ref[1] / l_ref[1]).astype(o_ref.dtype)


def _flash_tiles(Lq, Lk):
    tq = min(FLASH_Q_TILE, Lq)
    tk = next(t for t in (3328, 1280, 1024, 512, 256, Lk) if Lk % t == 0)
    return tq, tk


def _flash(q, k, v):
    B, Lq, _ = q.shape
    Lk = k.shape[1]
    tq, tk = _flash_tiles(Lq, Lk)
    hp = MLA_HEADS // 2
    return pl.pallas_call(
        functools.partial(_flash_body, R=min(FLASH_ROWS, tq // 2)),
        grid=(B, hp, Lq // tq, Lk // tk),
        in_specs=[pl.BlockSpec((1, tq, 2 * HEAD_PAD), lambda b, h, i, j: (b, i, h)),
                  pl.BlockSpec((1, tk, 2 * HEAD_PAD), lambda b, h, i, j: (b, j, h)),
                  pl.BlockSpec((1, tk, 2 * MLA_V), lambda b, h, i, j: (b, j, h))],
        out_specs=pl.BlockSpec((1, tq, 2 * MLA_V), lambda b, h, i, j: (b, i, h)),
        out_shape=jax.ShapeDtypeStruct((B, Lq, MLA_HEADS * MLA_V), BF16),
        scratch_shapes=[pltpu.VMEM((2, tq, LANES), F32), pltpu.VMEM((2, tq, LANES), F32),
                        pltpu.VMEM((2, tq, 2 * MLA_V), F32), pltpu.VMEM((tq, tk), F32)],
        compiler_params=_cp("parallel", "parallel", "parallel", "arbitrary"),
        name="mla_flash",
    )(q, k, v)


def _layernorm_rows(x, g, b, eps=1e-5):
    mu = jnp.mean(x, axis=-1, keepdims=True)
    xc = x - mu
    var = jnp.mean(xc * xc, axis=-1, keepdims=True)
    return xc * lax.rsqrt(var + eps) * g + b


def _outproj_body(of_ref, ob_ref, g_ref, hy_ref, om_ref, x_ref, gate_ref, gg_ref, hg_ref, mg_ref,
                  w_ref, lg_ref, lb_ref, o_ref, *, alpha):
    VD = GLA_HEADS * GLA_DV
    tm = x_ref.shape[1]
    r = _idiv(lax.broadcasted_iota(jnp.int32, (VD, VD), 0), GLA_DV)
    c = _idiv(lax.broadcasted_iota(jnp.int32, (VD, VD), 1), GLA_DV)
    grp = (r == c).astype(F32)
    parts = [slice(0, tm // 2), slice(tm // 2, tm)] if tm % 32 == 0 else [slice(0, tm)]
    o = [of_ref[0, p, :].astype(F32) + ob_ref[0, p, :].astype(F32) for p in parts]
    ms = [jnp.dot(o_ * o_, grp, precision=HI, preferred_element_type=F32) * (1.0 / GLA_DV) for o_ in o]
    ys = []
    for p, o_, ms_ in zip(parts, o, ms):
        g = g_ref[0, p, :].astype(F32)
        ya = o_ * lax.rsqrt(ms_ + 1e-6) * gg_ref[...] * (g * jax.nn.sigmoid(g))
        yb = _rms_rows(hy_ref[0, p, :], hg_ref[...])
        yc = _rms_rows(om_ref[0, p, :].astype(F32), mg_ref[...])
        ys.append((ya.astype(BF16), yb.astype(BF16), yc.astype(BF16)))
    accs = []
    for ya, yb, yc in ys:
        acc = jnp.dot(ya, w_ref[0:VD, :], preferred_element_type=F32)
        acc += jnp.dot(yb, w_ref[VD:VD + HY_CH, :], preferred_element_type=F32)
        acc += jnp.dot(yc, w_ref[VD + HY_CH:, :], preferred_element_type=F32)
        accs.append(acc)
    for p, acc in zip(parts, accs):
        o_ref[0, p, :] = _layernorm_rows(alpha * x_ref[0, p, :] + gate_ref[0] * acc, lg_ref[...], lb_ref[...])


def _outproj(of, ob, vg, hy, om, x, gate, gla_g, hy_g, mla_g, w_out, ln_g, ln_b, alpha):
    B, L, D = x.shape
    tm = min(ROW_TILE, L)
    VD = GLA_HEADS * GLA_DV
    MD = MLA_HEADS * MLA_V
    row = lambda w: pl.BlockSpec((1, tm, w), lambda b, i: (b, i, 0))
    return pl.pallas_call(
        functools.partial(_outproj_body, alpha=alpha),
        grid=(B, L // tm),
        in_specs=[row(VD), row(VD), pl.BlockSpec((1, tm, VD), lambda b, i: (b, i, 1)), row(HY_CH), row(MD), row(D),
                  pl.BlockSpec((1, 1, D), lambda b, i: (b, 0, 0)), _full((1, VD)), _full((1, HY_CH)), _full((1, MD)),
                  _full(w_out.shape), _full((1, D)), _full((1, D))],
        out_specs=row(D),
        out_shape=jax.ShapeDtypeStruct((B, L, D), F32),
        compiler_params=_cp("parallel", "parallel"),
        name="outproj",
    )(of, ob, vg, hy, om, x, gate, jnp.tile(gla_g, GLA_HEADS)[None, :], hy_g[None, :], mla_g[None, :],
      w_out.astype(BF16), ln_g[None, :], ln_b[None, :])


def _ffn_body(x_ref, sh_ref, sc_ref, gate_ref, w1_ref, w3_ref, w2_ref, lg_ref, lb_ref, o_ref, h_ref, acc_ref, *, alpha):
    j = pl.program_id(2)

    @pl.when(j == 0)
    def _():
        h_ref[...] = (x_ref[0] * (1.0 + sc_ref[0]) + sh_ref[0]).astype(BF16)
        acc_ref[...] = jnp.zeros_like(acc_ref)

    h = h_ref[...]
    a = jnp.dot(h, w1_ref[...], preferred_element_type=F32)
    b = jnp.dot(h, w3_ref[...], preferred_element_type=F32)
    t = (a * jax.nn.sigmoid(a) * b).astype(BF16)
    acc_ref[...] += jnp.dot(t, w2_ref[...], preferred_element_type=F32)

    @pl.when(j == pl.num_programs(2) - 1)
    def _():
        o_ref[0] = _layernorm_rows(alpha * x_ref[0] + gate_ref[0] * acc_ref[...], lg_ref[...], lb_ref[...])


def _ffn(x, shift, scale, gate, w1, w3, w2, ln_g, ln_b, alpha):
    B, L, D = x.shape
    F = w1.shape[1]
    tf = next(t for t in (1408, 512, 256, 128, F) if F % t == 0)
    tm = min(ROW_TILE if tf > 512 else WIDE_ROW_TILE, L)
    row = pl.BlockSpec((1, tm, D), lambda b, i, j: (b, i, 0))
    vec = pl.BlockSpec((1, 1, D), lambda b, i, j: (b, 0, 0))
    return pl.pallas_call(
        functools.partial(_ffn_body, alpha=alpha),
        grid=(B, L // tm, F // tf),
        in_specs=[row, vec, vec, vec,
                  pl.BlockSpec((D, tf), lambda b, i, j: (0, j)), pl.BlockSpec((D, tf), lambda b, i, j: (0, j)),
                  pl.BlockSpec((tf, D), lambda b, i, j: (j, 0)), _full((1, D)), _full((1, D))],
        out_specs=row,
        out_shape=jax.ShapeDtypeStruct((B, L, D), F32),
        scratch_shapes=[pltpu.VMEM((tm, D), BF16), pltpu.VMEM((tm, D), F32)],
        compiler_params=_cp("parallel", "parallel", "arbitrary"),
        name="ffn",
    )(x, shift, scale, gate, w1.astype(BF16), w3.astype(BF16), w2.astype(BF16), ln_g[None, :], ln_b[None, :])


MOE_TOKENS = 2048
MOE_ROWS = 256
RANK_CHUNK = 256


def _router_body(x_ref, sh_ref, sc_ref, wr_ref, h_ref, g_ref, rk_ref, rkt_ref, cnt_ref):
    h = x_ref[0] * (1.0 + sc_ref[0]) + sh_ref[0]
    h_ref[0] = h.astype(BF16)
    logits = jnp.dot(h, wr_ref[...], precision=HI, preferred_element_type=F32)
    lane = lax.broadcasted_iota(jnp.int32, logits.shape, 1).astype(F32)
    logits = jnp.where(lane < N_EXPERTS, logits, -jnp.inf)
    m1 = jnp.max(logits, axis=1, keepdims=True)
    i1 = jnp.min(jnp.where(logits == m1, lane, float(LANES)), axis=1, keepdims=True)
    rest = jnp.where(lane == i1, -jnp.inf, logits)
    m2 = jnp.max(rest, axis=1, keepdims=True)
    i2 = jnp.min(jnp.where(rest == m2, lane, float(LANES)), axis=1, keepdims=True)
    e2 = jnp.exp(m2 - m1)
    w1 = 1.0 / (1.0 + e2)
    w2 = e2 / (1.0 + e2)
    g_ref[0] = jnp.where(lane == i1, w1, 0.0) + jnp.where(lane == i2, w2, 0.0)
    sel = jnp.logical_or(lane == i1, lane == i2)
    self_ = sel.astype(F32)
    tm = h.shape[0]
    C = min(RANK_CHUNK, tm)
    r = lax.broadcasted_iota(jnp.int32, (C, C), 0)
    c = lax.broadcasted_iota(jnp.int32, (C, C), 1)
    tri = (c < r).astype(BF16)
    carry = jnp.zeros((1, LANES), F32)
    parts = []
    for k in range(tm // C):
        sk = self_[k * C:(k + 1) * C]
        parts.append(jnp.dot(tri, sk.astype(BF16), preferred_element_type=F32) + carry)
        carry = carry + jnp.sum(sk, axis=0, keepdims=True)
    rank = jnp.where(sel, jnp.concatenate(parts, axis=0), -1.0)
    rk_ref[0] = rank
    rkt_ref[0] = rank.T[:8]
    cnt_ref[0, 0] = carry


def _router(x, shift, scale, w_router):
    B, L, D = x.shape
    tm = min(MOE_TOKENS, L)
    nt = L // tm
    wr = jnp.pad(w_router, ((0, 0), (0, LANES - N_EXPERTS)))
    vec = pl.BlockSpec((1, 1, D), lambda b, i: (b, 0, 0))
    col = pl.BlockSpec((1, tm, LANES), lambda b, i: (b, i, 0))
    return pl.pallas_call(
        _router_body,
        grid=(B, nt),
        in_specs=[pl.BlockSpec((1, tm, D), lambda b, i: (b, i, 0)), vec, vec, _full((D, LANES))],
        out_specs=[pl.BlockSpec((1, tm, D), lambda b, i: (b, i, 0)), col, col,
                   pl.BlockSpec((1, 8, tm), lambda b, i: (b, 0, i)), pl.BlockSpec((1, 1, 1, LANES), lambda b, i: (b, i, 0, 0))],
        out_shape=[jax.ShapeDtypeStruct((B, L, D), BF16), jax.ShapeDtypeStruct((B, L, LANES), F32),
                   jax.ShapeDtypeStruct((B, L, LANES), F32), jax.ShapeDtypeStruct((B, 8, L), F32),
                   jax.ShapeDtypeStruct((B, nt, 1, LANES), F32)],
        compiler_params=_cp("parallel", "parallel"),
        name="moe_router",
    )(x, shift, scale, wr)


def _moe_body(cnt_ref, h_ref, g_ref, rk_ref, rkt_ref, w1_ref, w3_ref, w2_ref, o_ref, xg_ref, y_ref, *, M, P):
    b, i, e, j = pl.program_id(0), pl.program_id(1), pl.program_id(2), pl.program_id(3)
    nt, ne, nj = pl.num_programs(1), pl.num_programs(2), pl.num_programs(3)
    tm = h_ref.shape[1]
    cnt = cnt_ref[(b * nt + i) * ne + e]
    n_ch = lax.div(cnt + (M - 1), M)

    @pl.when(jnp.logical_and(e == 0, j == 0))
    def _():
        o_ref[...] = jnp.zeros_like(o_ref)

    @pl.when(j == 0)
    def _():
        rkt = rkt_ref[0, pl.ds(e, 1), :]

        def gather(c, carry):
            r0 = pl.multiple_of(c * M, 16)
            rows = (lax.broadcasted_iota(jnp.int32, (M, 1), 0) + c * M).astype(F32)
            onehot = (rkt == rows).astype(BF16)
            xg_ref[pl.ds(r0, M), :] = jnp.dot(onehot, h_ref[0], preferred_element_type=F32).astype(BF16)
            return carry

        lax.fori_loop(0, n_ch, gather, 0)

    def expert(chunks):
        r0 = [pl.multiple_of(c * M, 16) for c in chunks]
        xg = [xg_ref[pl.ds(r, M), :] for r in r0]
        a = [jnp.dot(x_, w1_ref[0], preferred_element_type=F32) for x_ in xg]
        g = [jnp.dot(x_, w3_ref[0], preferred_element_type=F32) for x_ in xg]
        t = [(a_ * jax.nn.sigmoid(a_) * g_).astype(BF16) for a_, g_ in zip(a, g)]
        yv = [jnp.dot(t_, w2_ref[0], preferred_element_type=F32) for t_ in t]

        @pl.when(j == 0)
        def _():
            for r, y_ in zip(r0, yv):
                y_ref[pl.ds(r, M), :] = y_

        @pl.when(j > 0)
        def _():
            for r, y_ in zip(r0, yv):
                y_ref[pl.ds(r, M), :] += y_

    def expert_pair(c2, carry):
        expert([2 * c2, 2 * c2 + 1])
        return carry

    lax.fori_loop(0, lax.div(n_ch, 2), expert_pair, 0)

    @pl.when(lax.rem(n_ch, 2) == 1)
    def _():
        expert([n_ch - 1])

    @pl.when(j == nj - 1)
    def _():
        for p in range(tm // P):
            lane = lax.broadcasted_iota(jnp.int32, (P, LANES), 1)
            rke = jnp.sum(jnp.where(lane == e, rk_ref[0, p * P:(p + 1) * P, :], 0.0), axis=1, keepdims=True)
            ge = jnp.sum(jnp.where(lane == e, g_ref[0, p * P:(p + 1) * P, :], 0.0), axis=1, keepdims=True)

            def scatter(c, carry):
                r0 = pl.multiple_of(c * M, 16)
                cols = (lax.broadcasted_iota(jnp.int32, (1, M), 1) + c * M).astype(F32)
                onehot = (rke == cols).astype(BF16)
                yb = y_ref[pl.ds(r0, M), :].astype(BF16)
                o_ref[0, p * P:(p + 1) * P, :] += ge * jnp.dot(onehot, yb, preferred_element_type=F32)
                return carry

            lax.fori_loop(0, n_ch, scatter, 0)


def _res_ln_body(x_ref, y_ref, gate_ref, lg_ref, lb_ref, o_ref, *, alpha):
    o_ref[0] = _layernorm_rows(alpha * x_ref[0] + gate_ref[0] * y_ref[0], lg_ref[...], lb_ref[...])


def _res_ln(x, y, gate, ln_g, ln_b, alpha):
    B, L, D = x.shape
    tm = min(WIDE_ROW_TILE, L)
    row = pl.BlockSpec((1, tm, D), lambda b, i: (b, i, 0))
    return pl.pallas_call(
        functools.partial(_res_ln_body, alpha=alpha),
        grid=(B, L // tm),
        in_specs=[row, row, pl.BlockSpec((1, 1, D), lambda b, i: (b, 0, 0)), _full((1, D)), _full((1, D))],
        out_specs=row,
        out_shape=jax.ShapeDtypeStruct((B, L, D), F32),
        compiler_params=_cp("parallel", "parallel"),
        name="res_ln",
    )(x, y, gate, ln_g[None, :], ln_b[None, :])


def _moe(x, shift, scale, gate, w_router, w1, w3, w2, ln_g, ln_b, alpha):
    B, L, D = x.shape
    E, _, F = w1.shape
    hb, gts, rk, rkt, cnt = _router(x, shift, scale, w_router)
    tm = min(MOE_TOKENS, L)
    nt = L // tm
    M = MOE_ROWS
    rows_max = -(-tm // M) * M
    tf = next(t for t in (896, 512, 256, 128, F) if F % t == 0)
    counts = cnt[:, :, 0, :E].astype(jnp.int32).reshape(-1)
    row = lambda w: pl.BlockSpec((1, tm, w), lambda b, i, e, j, c: (b, i, 0))
    y = pl.pallas_call(
        functools.partial(_moe_body, M=M, P=min(512, tm)),
        grid_spec=pltpu.PrefetchScalarGridSpec(
            num_scalar_prefetch=1,
            grid=(B, nt, E, F // tf),
            in_specs=[row(D), row(LANES), row(LANES), pl.BlockSpec((1, 8, tm), lambda b, i, e, j, c: (b, 0, i)),
                      pl.BlockSpec((1, D, tf), lambda b, i, e, j, c: (e, 0, j)),
                      pl.BlockSpec((1, D, tf), lambda b, i, e, j, c: (e, 0, j)),
                      pl.BlockSpec((1, tf, D), lambda b, i, e, j, c: (e, j, 0))],
            out_specs=row(D),
            scratch_shapes=[pltpu.VMEM((rows_max, D), BF16), pltpu.VMEM((rows_max, D), F32)],
        ),
        out_shape=jax.ShapeDtypeStruct((B, L, D), F32),
        compiler_params=_cp("parallel", "parallel", "arbitrary", "arbitrary"),
        name="moe",
    )(counts, hb, gts, rk, rkt, w1.astype(BF16), w3.astype(BF16), w2.astype(BF16))
    return _res_ln(x, y, gate, ln_g, ln_b, alpha)


def _mod_body(c_ref, w_ref, b_ref, o_ref):
    c = c_ref[...]
    s = c * jax.nn.sigmoid(c)
    o_ref[...] = jnp.dot(s, w_ref[...], precision=HI, preferred_element_type=F32) + b_ref[...]


def _modulation(cc, w_mod, b_mod):
    R, D = cc.shape
    N = w_mod.shape[1]
    tn = 1024
    return pl.pallas_call(
        _mod_body,
        grid=(N // tn,),
        in_specs=[_full((R, D)), pl.BlockSpec((D, tn), lambda j: (0, j)), pl.BlockSpec((1, tn), lambda j: (0, j))],
        out_specs=pl.BlockSpec((R, tn), lambda j: (0, j)),
        out_shape=jax.ShapeDtypeStruct((R, N), F32),
        compiler_params=_cp("parallel"),
        name="modulation",
    )(cc, w_mod, b_mod[None, :])


def _streams(x, c, ctx, c_ctx, w_mod, b_mod, w_in, gla_w_gate, gla_b_gate, gla_norm_g, hy_conv_w, hy_conv_b, hy_f_w1, hy_f_b1, hy_f_freq1, hy_f_w2, hy_f_b2, hy_f_freq2, hy_f_w3, hy_f_b3, hy_skip, hy_norm_g, mla_q_norm_g, mla_w_uq, mla_kv_norm_g, mla_w_ukv, mla_norm_g, w_out, ln_g, ln_b, ffn_w1, ffn_w3, ffn_w2, moe_router, moe_w1, moe_w3, moe_w2):
    B, L, D = x.shape
    Lc = ctx.shape[1]
    depth = w_mod.shape[0]
    alpha = (2.0 * depth) ** 0.25
    cc = jnp.zeros((8, D), F32).at[:B].set(c).at[B].set(c_ctx)
    cos, sin = _rope_tables(L, True)
    cos_c, sin_c = _rope_tables(Lc, False)
    cos_all, sin_all = jnp.concatenate([cos_c, cos], axis=0), jnp.concatenate([sin_c, sin], axis=0)
    KD, VD = GLA_HEADS * GLA_DK, GLA_HEADS * GLA_DV
    xc = ctx
    for l in range(depth):
        need_ctx = l < depth - 1
        mods = _modulation(cc, w_mod[l], b_mod[l])
        m = [mods[:B, k * D:(k + 1) * D][:, None, :] for k in range(6)]
        mc = [jnp.broadcast_to(mods[B, k * D:(k + 1) * D][None, None, :], (B, 1, D)) for k in range(6)]
        w_arr = _arrange_w_in(w_in[l])
        wg, bg = _arrange_gate(gla_w_gate[l], gla_b_gate[l])
        filt = (hy_f_w1[l], hy_f_b1[l], hy_f_freq1[l], hy_f_w2[l], hy_f_b2[l], hy_f_freq2[l], hy_f_w3[l], hy_f_b3[l])
        wq = _arrange_wq(mla_w_uq[l])
        wk, wv = _arrange_wkv(mla_w_ukv[l])

        hyu, qk, vg, alr, cq, ckvr = _inproj(x, m[0], m[1], w_arr)
        hyu_c, qk_c, vg_c, alr_c, cq_c, ckvr_c = _inproj(xc, mc[0], mc[1], w_arr)

        of_c, ob_c, s_c = _gla(qk_c, vg_c, alr_c, wg, bg, jnp.zeros((B, 2, KD, VD), F32))
        of, ob, _ = _gla(qk, vg, alr, wg, bg, s_c)
        hy = _hyena(hyu, hy_conv_w[l], hy_conv_b[l], filt, hy_skip[l])
        k_all, v_all = _kvproj(jnp.concatenate([ckvr_c, ckvr], axis=1), mla_kv_norm_g[l], wk, wv, cos_all, sin_all)
        k_c, v_c = k_all[:, :Lc], v_all[:, :Lc]
        q_m = _qproj(cq, mla_q_norm_g[l], wq, cos, sin)
        om = _flash(q_m, k_all, v_all)

        x = _outproj(of, ob, vg, hy, om, x, m[2], gla_norm_g[l], hy_norm_g[l], mla_norm_g[l], w_out[l],
                     ln_g[l, 0], ln_b[l, 0], alpha)
        if need_ctx:
            hy_c = _hyena_ctx(hyu_c, hy_conv_w[l], hy_conv_b[l], filt, hy_skip[l])
            q_c = _qproj(cq_c, mla_q_norm_g[l], wq, cos_c, sin_c)
            om_c = _flash(q_c, k_c, v_c)
            xc = _outproj(of_c, ob_c, vg_c, hy_c, om_c, xc, mc[2], gla_norm_g[l], hy_norm_g[l], mla_norm_g[l],
                          w_out[l], ln_g[l, 0], ln_b[l, 0], alpha)

        i = l // 2
        if l % 2 == 0:
            x = _ffn(x, m[3], m[4], m[5], ffn_w1[i], ffn_w3[i], ffn_w2[i], ln_g[l, 1], ln_b[l, 1], alpha)
            if need_ctx:
                xc = _ffn(xc, mc[3], mc[4], mc[5], ffn_w1[i], ffn_w3[i], ffn_w2[i], ln_g[l, 1], ln_b[l, 1], alpha)
        else:
            x = _moe(x, m[3], m[4], m[5], moe_router[i], moe_w1[i], moe_w3[i], moe_w2[i], ln_g[l, 1], ln_b[l, 1], alpha)
            if need_ctx:
                xc = _moe(xc, mc[3], mc[4], mc[5], moe_router[i], moe_w1[i], moe_w3[i], moe_w2[i], ln_g[l, 1],
                          ln_b[l, 1], alpha)
    return x, xc


def kernel(x, c, ctx, c_ctx, w_mod, b_mod, w_in, gla_w_gate, gla_b_gate, gla_norm_g, hy_conv_w, hy_conv_b, hy_f_w1, hy_f_b1, hy_f_freq1, hy_f_w2, hy_f_b2, hy_f_freq2, hy_f_w3, hy_f_b3, hy_skip, hy_norm_g, mla_q_norm_g, mla_w_uq, mla_kv_norm_g, mla_w_ukv, mla_norm_g, w_out, ln_g, ln_b, ffn_w1, ffn_w3, ffn_w2, moe_router, moe_w1, moe_w3, moe_w2):
    return _streams(x, c, ctx, c_ctx, w_mod, b_mod, w_in, gla_w_gate, gla_b_gate, gla_norm_g, hy_conv_w, hy_conv_b, hy_f_w1, hy_f_b1, hy_f_freq1, hy_f_w2, hy_f_b2, hy_f_freq2, hy_f_w3, hy_f_b3, hy_skip, hy_norm_g, mla_q_norm_g, mla_w_uq, mla_kv_norm_g, mla_w_ukv, mla_norm_g, w_out, ln_g, ln_b, ffn_w1, ffn_w3, ffn_w2, moe_router, moe_w1, moe_w3, moe_w2)[0]
```

```python
import functools
import math

import numpy as np
import jax
import jax.numpy as jnp
from jax import lax
from jax.experimental import pallas as pl
from jax.experimental.pallas import tpu as pltpu

F32 = jnp.float32
BF16 = jnp.bfloat16
HI = lax.Precision.HIGHEST

GRID_W = 64
GLA_HEADS, GLA_DK, GLA_DV, GLA_RANK, GLA_TAU = 4, 32, 64, 16, 16.0
HY_CH, HY_EMB = 256, 33
HY_DECAY_TARGET, HY_FAST_DECAY, HY_SLOW_DECAY = 1e-2, 0.3, 1.5
MLA_HEADS, MLA_Q_RANK, MLA_KV_RANK, MLA_NOPE, MLA_ROPE, MLA_V = 8, 256, 128, 64, 32, 64
MLA_SCALE = (MLA_NOPE + MLA_ROPE) ** -0.5
ROPE_BASE = 10000.0
N_EXPERTS = 8
IN_SPLITS = (128, 128, 256, 256, 32, 768, 256, 128, 32)

LANES = 128
SUBLANES = 8
VMEM_LIMIT = 56 * 1024 * 1024

ROW_TILE = 512
WIDE_ROW_TILE = 1024
GLA_CHUNK = 128
DFT_N2 = 256


def _cp(*sem):
    return pltpu.CompilerParams(dimension_semantics=sem, vmem_limit_bytes=VMEM_LIMIT)


def _full(shape):
    n = len(shape)
    return pl.BlockSpec(shape, lambda *_: (0,) * n)


def _idiv(x, d):
    assert d & (d - 1) == 0
    return lax.shift_right_logical(x, int(math.log2(d)))


INPROJ_WIDTHS = (768, 256, 512, 128, 256, 256)


def _arrange_w_in(w):
    cuts = np.cumsum(IN_SPLITS)[:-1]
    qa, ka, va, ga, alr, hyu, cq, ckv, kr = jnp.split(w, [int(c) for c in cuts], axis=1)
    z96 = jnp.zeros((w.shape[0], 96), w.dtype)
    return jnp.concatenate([hyu, qa, ka, va, ga, alr, z96, cq, ckv, kr, z96], axis=1).astype(BF16)


def _inproj_body(x_ref, sh_ref, sc_ref, w_ref, *out_refs):
    h = x_ref[0] * (1.0 + sc_ref[0]) + sh_ref[0]
    acc = jnp.dot(h.astype(BF16), w_ref[...], preferred_element_type=F32)
    off = 0
    for r in out_refs:
        w = r.shape[-1]
        r[0] = acc[:, off:off + w].astype(r.dtype)
        off += w


def _inproj(x, shift, scale, w_arr):
    B, L, D = x.shape
    tm = min(ROW_TILE, L)
    n = w_arr.shape[1]
    row = lambda w: pl.BlockSpec((1, tm, w), lambda b, i: (b, i, 0))
    vec = pl.BlockSpec((1, 1, D), lambda b, i: (b, 0, 0))
    return pl.pallas_call(
        _inproj_body,
        grid=(B, L // tm),
        in_specs=[row(D), vec, vec, _full((D, n))],
        out_specs=[row(w) for w in INPROJ_WIDTHS],
        out_shape=[jax.ShapeDtypeStruct((B, L, w), BF16) for w in INPROJ_WIDTHS],
        compiler_params=_cp("parallel", "parallel"),
        name="inproj",
    )(x, shift, scale, w_arr)


def _log_sigmoid(z):
    return jnp.minimum(z, 0.0) - jnp.log1p(jnp.exp(-jnp.abs(z)))


def _gla_body(qkf_ref, vf_ref, af_ref, qkb_ref, vb_ref, ab_ref, wg_ref, bg_ref, s0_ref,
              of_ref, ob_ref, sout_ref, s_ref):
    i = pl.program_id(0)
    C = qkf_ref.shape[1]
    KD = GLA_HEADS * GLA_DK
    VD = GLA_HEADS * GLA_DV

    @pl.when(i == 0)
    def _():
        s_ref[...] = s0_ref[...]

    r = lax.broadcasted_iota(jnp.int32, (C, C), 0)
    c = lax.broadcasted_iota(jnp.int32, (C, C), 1)
    tris = ((c <= r).astype(F32), (c >= r).astype(F32))
    lane_k = _idiv(lax.broadcasted_iota(jnp.int32, (1, KD), 1), GLA_DK)
    lane_v = _idiv(lax.broadcasted_iota(jnp.int32, (1, VD), 1), GLA_DV)
    rk = _idiv(lax.broadcasted_iota(jnp.int32, (KD, VD), 0), GLA_DK)
    cv = _idiv(lax.broadcasted_iota(jnp.int32, (KD, VD), 1), GLA_DV)
    ones = jnp.ones((C, VD), F32)
    refs = ((qkf_ref, vf_ref, af_ref, of_ref), (qkb_ref, vb_ref, ab_ref, ob_ref))
    chains = [(b, d) for b in range(qkf_ref.shape[0]) for d in range(2)]

    z = [jnp.dot(refs[d][2][b].astype(F32), wg_ref[...], precision=HI, preferred_element_type=F32) + bg_ref[...]
         for b, d in chains]
    la = [_log_sigmoid(zz[:, d * KD:(d + 1) * KD]) / GLA_TAU for zz, (b, d) in zip(z, chains)]
    bb = [jnp.dot(tris[d], l_, precision=HI, preferred_element_type=F32) for l_, (b, d) in zip(la, chains)]
    tot_b = [lax.dot_general(l_, ones, (((0,), (0,)), ((), ())), precision=HI, preferred_element_type=F32) for l_ in la]
    qe, ke, kl, vb, s_old = [], [], [], [], []
    for n, (b, d) in enumerate(chains):
        qk = refs[d][0][b].astype(F32)
        q = qk[:, :KD] * (GLA_DK ** -0.5)
        k = qk[:, KD:]
        tot = jnp.sum(la[n], axis=0, keepdims=True)
        qe.append(q * jnp.exp(bb[n]))
        ke.append((k * jnp.exp(-bb[n])).astype(BF16))
        kl.append((k * jnp.exp(tot - bb[n])).astype(BF16))
        vb.append(refs[d][1][b].astype(BF16))
        s_old.append(s_ref[2 * b + d])
    o = [jnp.dot(qe[n].astype(BF16), s_old[n].astype(BF16), preferred_element_type=F32) for n in range(len(chains))]
    att = [[lax.dot_general(jnp.where(lane_k == h, qe[n], 0.0).astype(BF16), ke[n], (((1,), (1,)), ((), ())),
                            preferred_element_type=F32) for h in range(GLA_HEADS)] for n in range(len(chains))]
    kv = [lax.dot_general(kl[n], vb[n], (((0,), (0,)), ((), ())), preferred_element_type=F32) for n in range(len(chains))]
    for n, (b, d) in enumerate(chains):
        on = o[n]
        for h in range(GLA_HEADS):
            oh = jnp.dot((att[n][h] * tris[d]).astype(BF16), vb[n], preferred_element_type=F32)
            on = on + jnp.where(lane_v == h, oh, 0.0)
        refs[d][3][b] = on.astype(refs[d][3].dtype)
        s_ref[2 * b + d] = jnp.exp(tot_b[n]) * s_old[n] + jnp.where(rk == cv, kv[n], 0.0)

    @pl.when(i == pl.num_programs(0) - 1)
    def _():
        sout_ref[...] = s_ref[...]


def _gla(qk, vg, alr, wg, bg, s0):
    B, L, _ = qk.shape
    C = min(GLA_CHUNK, L)
    n = L // C
    KD, VD = GLA_HEADS * GLA_DK, GLA_HEADS * GLA_DV
    fwd = lambda w: pl.BlockSpec((B, C, w), lambda i: (0, i, 0))
    bwd = lambda w: pl.BlockSpec((B, C, w), lambda i: (0, n - 1 - i, 0))
    st = _full((2 * B, KD, VD))
    of, ob, s_out = pl.pallas_call(
        _gla_body,
        grid=(n,),
        in_specs=[fwd(2 * KD), fwd(VD), fwd(LANES), bwd(2 * KD), bwd(VD), bwd(LANES),
                  _full((LANES, 2 * KD)), _full((1, 2 * KD)), st],
        out_specs=[fwd(VD), bwd(VD), st],
        out_shape=[jax.ShapeDtypeStruct((B, L, VD), BF16), jax.ShapeDtypeStruct((B, L, VD), BF16),
                   jax.ShapeDtypeStruct((2 * B, KD, VD), F32)],
        scratch_shapes=[pltpu.VMEM((2 * B, KD, VD), F32)],
        compiler_params=_cp("arbitrary"),
        name="gla",
    )(qk, vg, alr, qk, vg, alr, wg, bg, s0.reshape(2 * B, KD, VD))
    return of, ob, s_out.reshape(B, 2, KD, VD)


def _arrange_gate(w_gate, b_gate):
    KD = GLA_HEADS * GLA_DK
    wg = jnp.zeros((LANES, 2 * KD), F32)
    wg = wg.at[:GLA_RANK, :KD].set(w_gate[0]).at[GLA_RANK:2 * GLA_RANK, KD:].set(w_gate[1])
    return wg, jnp.concatenate([b_gate[0], b_gate[1]])[None, :]


def _shortconv_body(x_ref, p_ref, n_ref, w_ref, b_ref, v_ref, x1_ref, x2_ref):
    i = pl.program_id(1)
    last = pl.num_programs(1) - 1
    x = x_ref[0].astype(F32)
    tm = x.shape[0]
    hb = p_ref.shape[1]
    prev = jnp.where(i > 0, p_ref[0].astype(F32)[hb - 1:hb, :], 0.0)
    nxt = jnp.where(i < last, n_ref[0].astype(F32)[0:1, :], 0.0)
    rid = lax.broadcasted_iota(jnp.int32, x.shape, 0)
    dn = jnp.where(rid == 0, prev, pltpu.roll(x, 1, 0))
    up = jnp.where(rid == tm - 1, nxt, pltpu.roll(x, tm - 1, 0))
    w = w_ref[...]
    y = b_ref[...] + dn * w[0:1] + x * w[1:2] + up * w[2:3]
    v_ref[0] = y[:, :HY_CH]
    x1_ref[0] = y[:, HY_CH:2 * HY_CH]
    x2_ref[0] = y[:, 2 * HY_CH:]


def _shortconv(u, w, b):
    B, L, W = u.shape
    tm = min(ROW_TILE, L)
    hb = 2 * SUBLANES
    nb = tm // hb
    row = pl.BlockSpec((1, tm, W), lambda b_, i: (b_, i, 0))
    prev = pl.BlockSpec((1, hb, W), lambda b_, i: (b_, jnp.maximum(i * nb - 1, 0), 0))
    nxt = pl.BlockSpec((1, hb, W), lambda b_, i: (b_, jnp.minimum((i + 1) * nb, L // hb - 1), 0))
    o = pl.BlockSpec((1, tm, HY_CH), lambda b_, i: (b_, i, 0))
    return pl.pallas_call(
        _shortconv_body,
        grid=(B, L // tm),
        in_specs=[row, prev, nxt, _full((3, W)), _full((1, W))],
        out_specs=[o, o, o],
        out_shape=[jax.ShapeDtypeStruct((B, L, HY_CH), F32)] * 3,
        compiler_params=_cp("parallel", "parallel"),
        name="shortconv",
    )(u, u, u, w, b[None, :])


def _filter_feats(L):
    pos = np.arange(L, dtype=np.float64)
    t = pos / (L - 1)
    bands = (HY_EMB - 1) // 2
    freqs = np.linspace(1e-4, bands - 1, bands)
    ang = (2.0 * math.pi * pos / L)[:, None] * freqs
    z = jnp.asarray(np.concatenate([t[:, None], np.cos(ang), -np.sin(ang)], axis=-1), dtype=F32)
    z = jnp.pad(z, ((0, 0), (0, LANES - HY_EMB)))
    deltas = np.abs(np.linspace(math.log(HY_DECAY_TARGET) / HY_SLOW_DECAY,
                                math.log(HY_DECAY_TARGET) / HY_FAST_DECAY, HY_CH))
    return z, jnp.asarray(np.tile(deltas, 4)[None, :], dtype=F32)


def _filter_body(z_ref, w1_ref, b1_ref, f1_ref, w2_ref, b2_ref, f2_ref, w3_ref, b3_ref, dl_ref,
                 h_ref, ss_ref, *, L):
    i = pl.program_id(0)
    z = z_ref[...]
    tm = z.shape[0]
    hid = jnp.sin(f1_ref[...] * (jnp.dot(z, w1_ref[...], precision=HI, preferred_element_type=F32) + b1_ref[...]))
    hid = jnp.sin(f2_ref[...] * (jnp.dot(hid, w2_ref[...], precision=HI, preferred_element_type=F32) + b2_ref[...]))
    h = jnp.dot(hid, w3_ref[...], precision=HI, preferred_element_type=F32) + b3_ref[...]
    pos = (lax.broadcasted_iota(jnp.int32, (tm, 1), 0) + i * tm).astype(F32)
    t = pos / (L - 1)
    h = h * jnp.exp(-t * dl_ref[...])

    @pl.when(i == 0)
    def _():
        ss_ref[...] = jnp.zeros_like(ss_ref)

    ss_ref[...] += jnp.sum(h * h, axis=0, keepdims=True)
    col = lax.broadcasted_iota(jnp.int32, h.shape, 1)
    is_bwd = (_idiv(col, HY_CH) & 1) == 1
    h_ref[...] = jnp.where(jnp.logical_and(is_bwd, pos == 0.0), 0.0, h)


def _filters(L, fw1, fb1, ff1, fw2, fb2, ff2, fw3, fb3):
    z, dl = _filter_feats(L)
    tm = min(WIDE_ROW_TILE, L)
    Hf = fw2.shape[0]
    w1 = jnp.pad(fw1, ((0, LANES - HY_EMB), (0, 0)))
    NC = fw3.shape[1]
    return pl.pallas_call(
        functools.partial(_filter_body, L=L),
        grid=(L // tm,),
        in_specs=[pl.BlockSpec((tm, LANES), lambda i: (i, 0)), _full((LANES, Hf)), _full((1, Hf)), _full((1, Hf)),
                  _full((Hf, Hf)), _full((1, Hf)), _full((1, Hf)), _full((Hf, NC)), _full((1, NC)), _full((1, NC))],
        out_specs=[pl.BlockSpec((tm, NC), lambda i: (i, 0)), _full((1, NC))],
        out_shape=[jax.ShapeDtypeStruct((L, NC), F32), jax.ShapeDtypeStruct((1, NC), F32)],
        compiler_params=_cp("arbitrary"),
        name="hy_filters",
    )(z, w1, fb1[None], ff1[None], fw2, fb2[None], ff2[None], fw3, fb3[None], dl)


def _dft_consts(L):
    N = 2 * L
    N2 = DFT_N2
    N1 = N // N2
    half = N1 // 2
    k1 = np.arange(N1)[:, None].astype(np.float64)
    n1 = np.arange(N1)[None, :].astype(np.float64)
    a1 = 2.0 * np.pi * k1 * n1 / N1
    f1r, f1i = np.cos(a1), -np.sin(a1)
    fa = np.concatenate([f1r[:, :half], f1i[:, :half]], axis=0)
    fb = np.concatenate([f1r[:half, :], f1i[:half, :]], axis=1) / N
    k2 = np.arange(N2)[:, None].astype(np.float64)
    n2 = np.arange(N2)[None, :].astype(np.float64)
    a2 = 2.0 * np.pi * k2 * n2 / N2
    f2r, f2i = np.cos(a2), -np.sin(a2)
    g = np.block([[f2r, -f2i], [f2i, f2r]])
    gc = np.block([[f2r, f2i], [-f2i, f2r]])
    at = 2.0 * np.pi * (np.arange(N1)[:, None] * np.arange(N2)[None, :] % N) / N
    twr, twi = np.cos(at), -np.sin(at)
    c = lambda a: jnp.asarray(a, dtype=F32)
    bc = lambda a: jnp.broadcast_to(c(a)[:, :, None], (N1, N2, LANES))
    eye = np.eye(SUBLANES)
    return dict(N1=N1, N2=N2, half=half, fa=c(np.kron(fa, eye)), fb=c(np.kron(fb, eye)), g=c(g), gc=c(gc),
                twr=bc(twr), twi=bc(twi))


def _lanes(t, width):
    return jnp.concatenate([t] * (width // LANES), axis=-1)


def _dft1_body(f_ref, x_ref, o_ref):
    x = x_ref[0]
    x2 = x.reshape(x.shape[0] * SUBLANES, x.shape[2]).astype(BF16)
    y = jnp.dot(f_ref[...], x2, preferred_element_type=F32)
    o_ref[0] = y.reshape(o_ref.shape[1], SUBLANES, y.shape[1])


def _dft_stage1(fa, x):
    B, half, N2, W = x.shape
    R = fa.shape[0] // SUBLANES
    return pl.pallas_call(
        _dft1_body,
        grid=(B, N2 // SUBLANES),
        in_specs=[_full(fa.shape), pl.BlockSpec((1, half, SUBLANES, W), lambda b, j: (b, 0, j, 0))],
        out_specs=pl.BlockSpec((1, R, SUBLANES, W), lambda b, j: (b, 0, j, 0)),
        out_shape=jax.ShapeDtypeStruct((B, R, N2, W), F32),
        compiler_params=_cp("parallel", "parallel"),
        name="hy_dft1",
    )(fa.astype(BF16), x)


def _filter_spec_body(a_ref, twr_ref, twi_ref, g_ref, ss_ref, hf_ref):
    W = a_ref.shape[-1]
    ar, ai = a_ref[0, 0], a_ref[1, 0]
    twr, twi = _lanes(twr_ref[0], W), _lanes(twi_ref[0], W)
    xr = ar * twr - ai * twi
    xi = ar * twi + ai * twr
    z = jnp.dot(g_ref[...], jnp.concatenate([xr, xi], axis=0).astype(BF16), preferred_element_type=F32)
    n2 = z.shape[0] // 2
    zr, zi = z[:n2], z[n2:]
    ss = ss_ref[...]
    for o in range(2):
        f0, b0 = (2 * o) * HY_CH, (2 * o + 1) * HY_CH
        sc = lax.rsqrt(ss[:, f0:f0 + HY_CH] + ss[:, b0:b0 + HY_CH] + 1e-6)
        hf_ref[o, 0, 0] = ((zr[:, f0:f0 + HY_CH] + zr[:, b0:b0 + HY_CH]) * sc).astype(hf_ref.dtype)
        hf_ref[o, 0, 1] = ((zi[:, f0:f0 + HY_CH] - zi[:, b0:b0 + HY_CH]) * sc).astype(hf_ref.dtype)


def _filter_spectrum(h, ss, dc):
    L, NC = h.shape
    N1, N2, half = dc["N1"], dc["N2"], dc["half"]
    a = _dft_stage1(dc["fa"], h.reshape(1, half, N2, NC))
    a = a.reshape(2, N1, N2, NC)
    return pl.pallas_call(
        _filter_spec_body,
        grid=(N1,),
        in_specs=[pl.BlockSpec((2, 1, N2, NC), lambda k: (0, k, 0, 0)),
                  pl.BlockSpec((1, N2, LANES), lambda k: (k, 0, 0)), pl.BlockSpec((1, N2, LANES), lambda k: (k, 0, 0)),
                  _full((2 * N2, 2 * N2)), _full((1, NC))],
        out_specs=pl.BlockSpec((2, 1, 2, N2, HY_CH), lambda k: (0, k, 0, 0, 0)),
        out_shape=jax.ShapeDtypeStruct((2, N1, 2, N2, HY_CH), BF16),
        compiler_params=_cp("parallel"),
        name="hy_filter_spec",
    )(a, dc["twr"], dc["twi"], dc["g"].astype(BF16), ss)


SPEC_K1 = 8


def _spec_mul_body(a_ref, twr_ref, twi_ref, g_ref, gc_ref, hf_ref, o_ref):
    W = a_ref.shape[-1]
    ks = range(a_ref.shape[2])
    n2 = g_ref.shape[0] // 2
    x = []
    for k in ks:
        ar, ai = a_ref[0, 0, k], a_ref[0, 1, k]
        twr, twi = _lanes(twr_ref[k], W), _lanes(twi_ref[k], W)
        x.append(jnp.concatenate([ar * twr - ai * twi, ar * twi + ai * twr], axis=0).astype(BF16))
    z = [jnp.dot(g_ref[...], x_, preferred_element_type=F32) for x_ in x]
    y = []
    for k, z_ in zip(ks, z):
        zr, zi = z_[:n2], z_[n2:]
        hr, hi = hf_ref[0, k, 0].astype(F32), hf_ref[0, k, 1].astype(F32)
        y.append(jnp.concatenate([zr * hr - zi * hi, zr * hi + zi * hr], axis=0).astype(BF16))
    b = [jnp.dot(gc_ref[...], y_, preferred_element_type=F32) for y_ in y]
    for k, b_ in zip(ks, b):
        br, bi = b_[:n2], b_[n2:]
        twr, twi = _lanes(twr_ref[k], W), _lanes(twi_ref[k], W)
        o_ref[0, 0, k] = br * twr + bi * twi
        o_ref[0, 1, k] = bi * twr - br * twi


def _spec_mul(a, hf, order, dc):
    B = a.shape[0]
    N1, N2 = dc["N1"], dc["N2"]
    C = a.shape[-1]
    kb = min(SPEC_K1, N1)
    blk = pl.BlockSpec((1, 2, kb, N2, C), lambda k, b: (b, 0, k, 0, 0))
    tw = pl.BlockSpec((kb, N2, LANES), lambda k, b: (k, 0, 0))
    return pl.pallas_call(
        _spec_mul_body,
        grid=(N1 // kb, B),
        in_specs=[blk, tw, tw, _full((2 * N2, 2 * N2)), _full((2 * N2, 2 * N2)),
                  pl.BlockSpec((1, kb, 2, N2, C), lambda k, b: (order, k, 0, 0, 0))],
        out_specs=blk,
        out_shape=jax.ShapeDtypeStruct(a.shape, F32),
        compiler_params=_cp("parallel", "parallel"),
        name="hy_spec_mul",
    )(a, dc["twr"], dc["twi"], dc["g"].astype(BF16), dc["gc"].astype(BF16), hf)


DFT3_ROWS = 2 * SUBLANES


def _dft3_body(f_ref, b_ref, u_ref, gate_ref, skip_ref, o_ref):
    R, half, C = b_ref.shape[1], o_ref.shape[1], b_ref.shape[3]
    groups = [slice(s, s + SUBLANES) for s in range(0, b_ref.shape[2], SUBLANES)]
    b2 = [b_ref[0, :, g, :].reshape(R * SUBLANES, C).astype(BF16) for g in groups]
    y = [jnp.dot(f_ref[...], b_, preferred_element_type=F32) for b_ in b2]
    for g, y_ in zip(groups, y):
        u = u_ref[0, :, g, :].reshape(half * SUBLANES, C)
        gate = gate_ref[0, :, g, :].reshape(half * SUBLANES, C)
        o_ref[0, :, g, :] = (gate * (y_ + u * skip_ref[...])).reshape(half, SUBLANES, C)


def _dft_stage3(fb, bm, u, gate, skip):
    B, R, N2, C = bm.shape
    half = u.shape[1]
    qb = min(DFT3_ROWS, N2)
    row = pl.BlockSpec((1, half, qb, C), lambda b, j: (b, 0, j, 0))
    return pl.pallas_call(
        _dft3_body,
        grid=(B, N2 // qb),
        in_specs=[_full(fb.shape), pl.BlockSpec((1, R, qb, C), lambda b, j: (b, 0, j, 0)), row, row, _full((1, C))],
        out_specs=row,
        out_shape=jax.ShapeDtypeStruct((B, half, N2, C), F32),
        compiler_params=_cp("parallel", "parallel"),
        name="hy_dft3",
    )(fb.astype(BF16), bm, u, gate, skip[None, :])


def _longconv_gated(u, gate, hf, order, skip, dc):
    B, L, C = u.shape
    N1, N2, half = dc["N1"], dc["N2"], dc["half"]
    u4 = u.reshape(B, half, N2, C)
    a = _dft_stage1(dc["fa"], u4).reshape(B, 2, N1, N2, C)
    bm = _spec_mul(a, hf, order, dc).reshape(B, 2 * N1, N2, C)
    return _dft_stage3(dc["fb"], bm, u4, gate.reshape(B, half, N2, C), skip).reshape(B, L, C)


def _hyena(hyu, conv_w, conv_b, filt, skip):
    B, L, _ = hyu.shape
    v, x1, x2 = _shortconv(hyu, conv_w, conv_b)
    h, ss = _filters(L, *filt)
    dc = _dft_consts(L)
    hf = _filter_spectrum(h, ss, dc)
    z1 = _longconv_gated(v, x1, hf, 0, skip[0], dc)
    return _longconv_gated(z1, x2, hf, 1, skip[1], dc)


def _hyena_ctx_body(v_ref, x1_ref, x2_ref, h_ref, ss_ref, skip_ref, fc_ref, gc_ref, o_ref):
    fc, gc = fc_ref[...], gc_ref[...]
    n = fc.shape[0] // 2
    ss = ss_ref[...]
    h = h_ref[...]

    def conv(u, o):
        f0, b0 = (2 * o) * HY_CH, (2 * o + 1) * HY_CH
        sc = lax.rsqrt(ss[:, f0:f0 + HY_CH] + ss[:, b0:b0 + HY_CH] + 1e-6)
        x = jnp.dot(fc, u, precision=HI, preferred_element_type=F32)
        hf = jnp.dot(fc, h[:, f0:f0 + HY_CH], precision=HI, preferred_element_type=F32)
        hb = jnp.dot(fc, h[:, b0:b0 + HY_CH], precision=HI, preferred_element_type=F32)
        hr = (hf[:n] + hb[:n]) * sc
        hi = (hf[n:] - hb[n:]) * sc
        yr = x[:n] * hr - x[n:] * hi
        yi = x[:n] * hi + x[n:] * hr
        y = jnp.dot(gc, jnp.concatenate([yr, yi], axis=0), precision=HI, preferred_element_type=F32)
        return y + u * skip_ref[o:o + 1, :]

    z1 = x1_ref[0] * conv(v_ref[0], 0)
    o_ref[0] = x2_ref[0] * conv(z1, 1)


def _hyena_ctx(hyu, conv_w, conv_b, filt, skip):
    B, L, _ = hyu.shape
    v, x1, x2 = _shortconv(hyu, conv_w, conv_b)
    h, ss = _filters(L, *filt)
    N = 2 * L
    ang = 2.0 * np.pi * (np.arange(N)[:, None] * np.arange(L)[None, :] % N) / N
    fr, fi = np.cos(ang), -np.sin(ang)
    fc = jnp.asarray(np.concatenate([fr, fi], axis=0), dtype=F32)
    gc = jnp.asarray(np.concatenate([fr.T, fi.T], axis=1) / N, dtype=F32)
    row = pl.BlockSpec((1, L, HY_CH), lambda b: (b, 0, 0))
    return pl.pallas_call(
        _hyena_ctx_body,
        grid=(B,),
        in_specs=[row, row, row, _full(h.shape), _full(ss.shape), _full(skip.shape), _full(fc.shape), _full(gc.shape)],
        out_specs=row,
        out_shape=jax.ShapeDtypeStruct((B, L, HY_CH), F32),
        compiler_params=_cp("parallel"),
        name="hyena_ctx",
    )(v, x1, x2, h, ss, skip, fc, gc)


HEAD_PAD = 128


def _rope_swap(w):
    a, b, c, d = w[..., 0:8], w[..., 8:16], w[..., 16:24], w[..., 24:32]
    return jnp.concatenate([-b, a, -d, c], axis=-1)


def _arrange_wq(w_uq):
    R = w_uq.shape[0]
    w = w_uq.reshape(R, MLA_HEADS, MLA_NOPE + MLA_ROPE)
    rope = w[..., MLA_NOPE:]
    out = jnp.concatenate([w[..., :MLA_NOPE], rope, _rope_swap(rope)], axis=-1)
    return out.reshape(R, MLA_HEADS * HEAD_PAD).astype(BF16)


def _arrange_wkv(w_ukv):
    R = w_ukv.shape[0]
    w = w_ukv.reshape(R, MLA_HEADS, MLA_NOPE + MLA_V)
    wk = jnp.concatenate([w[..., :MLA_NOPE], jnp.zeros((R, MLA_HEADS, HEAD_PAD - MLA_NOPE), w.dtype)], axis=-1)
    wv = w[..., MLA_NOPE:]
    return wk.reshape(R, MLA_HEADS * HEAD_PAD).astype(BF16), wv.reshape(R, MLA_HEADS * MLA_V).astype(BF16)


def _kr_place():
    e = np.zeros((LANES, MLA_HEADS * HEAD_PAD), np.float32)
    es = np.zeros((LANES, MLA_HEADS * HEAD_PAD), np.float32)
    for h in range(MLA_HEADS):
        base = h * HEAD_PAD + MLA_NOPE
        for j in range(MLA_ROPE):
            e[j, base + j] = 1.0
            blk, r = divmod(j, 16)
            if r < 8:
                es[16 * blk + r + 8, base + j] = -1.0
            else:
                es[16 * blk + r - 8, base + j] = 1.0
    return jnp.asarray(e).astype(BF16), jnp.asarray(es).astype(BF16)


def _rope_tables(L, rope):
    if rope:
        t = np.arange(L)
        row, col = (t // GRID_W).astype(np.float64), (t % GRID_W).astype(np.float64)
        half = MLA_ROPE // 2
        inv = ROPE_BASE ** (-np.arange(0, half, 2, dtype=np.float64) / half)
        ar, ac = row[:, None] * inv, col[:, None] * inv
        cos = jnp.asarray(np.concatenate([np.cos(ar), np.cos(ar), np.cos(ac), np.cos(ac)], axis=-1), dtype=F32)
        sin = jnp.asarray(np.concatenate([np.sin(ar), np.sin(ar), np.sin(ac), np.sin(ac)], axis=-1), dtype=F32)
    else:
        cos, sin = jnp.ones((L, MLA_ROPE), F32), jnp.zeros((L, MLA_ROPE), F32)
    return cos, sin


def _rms_rows(x, g, eps=1e-6):
    return x * lax.rsqrt(jnp.mean(x * x, axis=-1, keepdims=True) + eps) * g


def _qproj_body(cq_ref, g_ref, w_ref, t1_ref, t2_ref, q_ref):
    xn = _rms_rows(cq_ref[0].astype(F32), g_ref[...])
    acc = jnp.dot(xn.astype(BF16), w_ref[...], preferred_element_type=F32)
    W = acc.shape[1]
    t1, t2 = _lanes(t1_ref[...], W), _lanes(t2_ref[...], W)
    q_ref[0] = (acc * t1 + pltpu.roll(acc, W - MLA_ROPE, 1) * t2).astype(q_ref.dtype)


def _qproj(cq, g, wq, cos, sin):
    B, L, R = cq.shape
    tm = min(ROW_TILE, L)
    W = wq.shape[1]
    ones, zeros = jnp.ones((L, MLA_NOPE), F32), jnp.zeros((L, MLA_ROPE), F32)
    qs = MLA_SCALE * math.log2(math.e)
    t1 = jnp.concatenate([ones, cos, zeros], axis=-1) * qs
    t2 = jnp.concatenate([jnp.zeros((L, MLA_NOPE), F32), sin, zeros], axis=-1) * qs
    tab = pl.BlockSpec((tm, HEAD_PAD), lambda b, i: (i, 0))
    return pl.pallas_call(
        _qproj_body,
        grid=(B, L // tm),
        in_specs=[pl.BlockSpec((1, tm, R), lambda b, i: (b, i, 0)), _full((1, R)), _full((R, W)), tab, tab],
        out_specs=pl.BlockSpec((1, tm, W), lambda b, i: (b, i, 0)),
        out_shape=jax.ShapeDtypeStruct((B, L, W), BF16),
        compiler_params=_cp("parallel", "parallel"),
        name="mla_qproj",
    )(cq, g[None, :], wq, t1, t2)


def _kvproj_body(c_ref, g_ref, wk_ref, wv_ref, e_ref, es_ref, cos_ref, sin_ref, k_ref, v_ref):
    c = c_ref[0].astype(F32)
    R = MLA_KV_RANK
    xn = _rms_rows(c[:, :R], g_ref[...]).astype(BF16)
    kr = c[:, R:]
    acc = jnp.dot(xn, wk_ref[...], preferred_element_type=F32)
    acc += jnp.dot((kr * cos_ref[...]).astype(BF16), e_ref[...], preferred_element_type=F32)
    acc += jnp.dot((kr * sin_ref[...]).astype(BF16), es_ref[...], preferred_element_type=F32)
    k_ref[0] = acc.astype(k_ref.dtype)
    v_ref[0] = jnp.dot(xn, wv_ref[...], preferred_element_type=F32).astype(v_ref.dtype)


def _kvproj(ckvr, g, wk, wv, cos, sin):
    B, L, Wc = ckvr.shape
    tm = next(t for t in (1280, 512, 256, L) if L % t == 0)
    pad = jnp.zeros((L, LANES - MLA_ROPE), F32)
    cos_p, sin_p = jnp.concatenate([cos, pad], axis=-1), jnp.concatenate([sin, pad], axis=-1)
    e, es = _kr_place()
    tab = pl.BlockSpec((tm, LANES), lambda b, i: (i, 0))
    Wk, Wv = wk.shape[1], wv.shape[1]
    return pl.pallas_call(
        _kvproj_body,
        grid=(B, L // tm),
        in_specs=[pl.BlockSpec((1, tm, Wc), lambda b, i: (b, i, 0)), _full((1, MLA_KV_RANK)),
                  _full(wk.shape), _full(wv.shape), _full(e.shape), _full(es.shape), tab, tab],
        out_specs=[pl.BlockSpec((1, tm, Wk), lambda b, i: (b, i, 0)), pl.BlockSpec((1, tm, Wv), lambda b, i: (b, i, 0))],
        out_shape=[jax.ShapeDtypeStruct((B, L, Wk), BF16), jax.ShapeDtypeStruct((B, L, Wv), BF16)],
        compiler_params=_cp("parallel", "parallel"),
        name="mla_kvproj",
    )(ckvr, g[None, :], wk, wv, e, es, cos_p, sin_p)


FLASH_Q_TILE = 2048
FLASH_ROWS = 256
FLASH_KEYS = 256


def _flash_body(q_ref, k_ref, v_ref, o_ref, m_ref, l_ref, acc_ref, s_ref, *, R):
    j = pl.program_id(3)
    tq, tk = q_ref.shape[1], k_ref.shape[1]
    CK = FLASH_KEYS
    npc = CK // LANES

    @pl.when(j == 0)
    def _():
        m_ref[...] = jnp.full_like(m_ref, -jnp.inf)
        l_ref[...] = jnp.zeros_like(l_ref)
        acc_ref[...] = jnp.zeros_like(acc_ref)

    def pass1(a, r):
        lo, r0 = a * HEAD_PAD, r * R
        q = q_ref[0, r0:r0 + R, lo:lo + HEAD_PAD]
        mp = None
        for c in range(tk // CK):
            kc = k_ref[0, c * CK:(c + 1) * CK, lo:lo + HEAD_PAD]
            s = lax.dot_general(q, kc, (((1,), (1,)), ((), ())), preferred_element_type=F32)
            s_ref[r0:r0 + R, c * CK:(c + 1) * CK] = s
            for w in range(npc):
                pc = s[:, w * LANES:(w + 1) * LANES]
                mp = pc if mp is None else jnp.maximum(mp, pc)
        m_old = m_ref[a, r0:r0 + R, :]
        return m_old, jnp.maximum(m_old, jnp.max(mp, axis=1, keepdims=True))

    def pass2(a, r, m_old, m_new):
        r0 = r * R
        alpha = jnp.exp2(m_old - m_new)
        lp = jnp.zeros((R, LANES), F32)
        pv = jnp.zeros((R, 2 * MLA_V), F32)
        for c in range(tk // CK):
            s = s_ref[r0:r0 + R, c * CK:(c + 1) * CK]
            ps = [jnp.exp2(s[:, w * LANES:(w + 1) * LANES] - m_new) for w in range(npc)]
            for p_ in ps:
                lp = lp + p_
            p = jnp.concatenate(ps, axis=1).astype(BF16)
            pv = pv + jnp.dot(p, v_ref[0, c * CK:(c + 1) * CK, :], preferred_element_type=F32)
        l_ref[a, r0:r0 + R, :] = alpha * l_ref[a, r0:r0 + R, :] + jnp.sum(lp, axis=1, keepdims=True)
        acc_ref[a, r0:r0 + R, :] = alpha * acc_ref[a, r0:r0 + R, :] + pv
        m_ref[a, r0:r0 + R, :] = m_new

    assert tq // R >= 2
    blocks = [(a, r) for a in range(2) for r in range(tq // R)]
    pend = pass1(*blocks[0])
    for i, blk in enumerate(blocks):
        nxt = pass1(*blocks[i + 1]) if i + 1 < len(blocks) else None
        pass2(*blk, *pend)
        pend = nxt

    @pl.when(j == pl.num_programs(3) - 1)
    def _():
        lane = lax.broadcasted_iota(jnp.int32, acc_ref.shape[1:], 1)
        o_ref[0] = jnp.where(lane < MLA_V, acc_ref[0] / l_ref[0], acc_ref[1] / l_ref[1]).astype(o_ref.dtype)


def _flash_tiles(Lq, Lk):
    tq = min(FLASH_Q_TILE, Lq)
    tk = next(t for t in (3328, 1280, 1024, 512, 256, Lk) if Lk % t == 0)
    return tq, tk


def _flash(q, k, v):
    B, Lq, _ = q.shape
    Lk = k.shape[1]
    tq, tk = _flash_tiles(Lq, Lk)
    hp = MLA_HEADS // 2
    return pl.pallas_call(
        functools.partial(_flash_body, R=min(FLASH_ROWS, tq // 2)),
        grid=(B, hp, Lq // tq, Lk // tk),
        in_specs=[pl.BlockSpec((1, tq, 2 * HEAD_PAD), lambda b, h, i, j: (b, i, h)),
                  pl.BlockSpec((1, tk, 2 * HEAD_PAD), lambda b, h, i, j: (b, j, h)),
                  pl.BlockSpec((1, tk, 2 * MLA_V), lambda b, h, i, j: (b, j, h))],
        out_specs=pl.BlockSpec((1, tq, 2 * MLA_V), lambda b, h, i, j: (b, i, h)),
        out_shape=jax.ShapeDtypeStruct((B, Lq, MLA_HEADS * MLA_V), BF16),
        scratch_shapes=[pltpu.VMEM((2, tq, LANES), F32), pltpu.VMEM((2, tq, LANES), F32),
                        pltpu.VMEM((2, tq, 2 * MLA_V), F32), pltpu.VMEM((tq, tk), F32)],
        compiler_params=_cp("parallel", "parallel", "parallel", "arbitrary"),
        name="mla_flash",
    )(q, k, v)


def _layernorm_rows(x, g, b, eps=1e-5):
    mu = jnp.mean(x, axis=-1, keepdims=True)
    xc = x - mu
    var = jnp.mean(xc * xc, axis=-1, keepdims=True)
    return xc * lax.rsqrt(var + eps) * g + b


def _outproj_body(of_ref, ob_ref, g_ref, hy_ref, om_ref, x_ref, gate_ref, gg_ref, hg_ref, mg_ref,
                  w_ref, lg_ref, lb_ref, o_ref, *, alpha):
    VD = GLA_HEADS * GLA_DV
    tm = x_ref.shape[1]
    r = _idiv(lax.broadcasted_iota(jnp.int32, (VD, VD), 0), GLA_DV)
    c = _idiv(lax.broadcasted_iota(jnp.int32, (VD, VD), 1), GLA_DV)
    grp = (r == c).astype(F32)
    parts = [slice(0, tm // 2), slice(tm // 2, tm)] if tm % 32 == 0 else [slice(0, tm)]
    o = [of_ref[0, p, :].astype(F32) + ob_ref[0, p, :].astype(F32) for p in parts]
    ms = [jnp.dot(o_ * o_, grp, precision=HI, preferred_element_type=F32) * (1.0 / GLA_DV) for o_ in o]
    ys = []
    for p, o_, ms_ in zip(parts, o, ms):
        g = g_ref[0, p, :].astype(F32)
        ya = o_ * lax.rsqrt(ms_ + 1e-6) * gg_ref[...] * (g * jax.nn.sigmoid(g))
        yb = _rms_rows(hy_ref[0, p, :], hg_ref[...])
        yc = _rms_rows(om_ref[0, p, :].astype(F32), mg_ref[...])
        ys.append((ya.astype(BF16), yb.astype(BF16), yc.astype(BF16)))
    accs = []
    for ya, yb, yc in ys:
        acc = jnp.dot(ya, w_ref[0:VD, :], preferred_element_type=F32)
        acc += jnp.dot(yb, w_ref[VD:VD + HY_CH, :], preferred_element_type=F32)
        acc += jnp.dot(yc, w_ref[VD + HY_CH:, :], preferred_element_type=F32)
        accs.append(acc)
    for p, acc in zip(parts, accs):
        o_ref[0, p, :] = _layernorm_rows(alpha * x_ref[0, p, :] + gate_ref[0] * acc, lg_ref[...], lb_ref[...])


def _outproj(of, ob, vg, hy, om, x, gate, gla_g, hy_g, mla_g, w_out, ln_g, ln_b, alpha):
    B, L, D = x.shape
    tm = min(ROW_TILE, L)
    VD = GLA_HEADS * GLA_DV
    MD = MLA_HEADS * MLA_V
    row = lambda w: pl.BlockSpec((1, tm, w), lambda b, i: (b, i, 0))
    return pl.pallas_call(
        functools.partial(_outproj_body, alpha=alpha),
        grid=(B, L // tm),
        in_specs=[row(VD), row(VD), pl.BlockSpec((1, tm, VD), lambda b, i: (b, i, 1)), row(HY_CH), row(MD), row(D),
                  pl.BlockSpec((1, 1, D), lambda b, i: (b, 0, 0)), _full((1, VD)), _full((1, HY_CH)), _full((1, MD)),
                  _full(w_out.shape), _full((1, D)), _full((1, D))],
        out_specs=row(D),
        out_shape=jax.ShapeDtypeStruct((B, L, D), F32),
        compiler_params=_cp("parallel", "parallel"),
        name="outproj",
    )(of, ob, vg, hy, om, x, gate, jnp.tile(gla_g, GLA_HEADS)[None, :], hy_g[None, :], mla_g[None, :],
      w_out.astype(BF16), ln_g[None, :], ln_b[None, :])


def _ffn_body(x_ref, sh_ref, sc_ref, gate_ref, w1_ref, w3_ref, w2_ref, lg_ref, lb_ref, o_ref, h_ref, acc_ref, *, alpha):
    j = pl.program_id(2)

    @pl.when(j == 0)
    def _():
        h_ref[...] = (x_ref[0] * (1.0 + sc_ref[0]) + sh_ref[0]).astype(BF16)
        acc_ref[...] = jnp.zeros_like(acc_ref)

    h = h_ref[...]
    a = jnp.dot(h, w1_ref[...], preferred_element_type=F32)
    b = jnp.dot(h, w3_ref[...], preferred_element_type=F32)
    t = (a * jax.nn.sigmoid(a) * b).astype(BF16)
    acc_ref[...] += jnp.dot(t, w2_ref[...], preferred_element_type=F32)

    @pl.when(j == pl.num_programs(2) - 1)
    def _():
        o_ref[0] = _layernorm_rows(alpha * x_ref[0] + gate_ref[0] * acc_ref[...], lg_ref[...], lb_ref[...])


def _ffn(x, shift, scale, gate, w1, w3, w2, ln_g, ln_b, alpha):
    B, L, D = x.shape
    F = w1.shape[1]
    tf = next(t for t in (1408, 512, 256, 128, F) if F % t == 0)
    tm = min(ROW_TILE if tf > 512 else WIDE_ROW_TILE, L)
    row = pl.BlockSpec((1, tm, D), lambda b, i, j: (b, i, 0))
    vec = pl.BlockSpec((1, 1, D), lambda b, i, j: (b, 0, 0))
    return pl.pallas_call(
        functools.partial(_ffn_body, alpha=alpha),
        grid=(B, L // tm, F // tf),
        in_specs=[row, vec, vec, vec,
                  pl.BlockSpec((D, tf), lambda b, i, j: (0, j)), pl.BlockSpec((D, tf), lambda b, i, j: (0, j)),
                  pl.BlockSpec((tf, D), lambda b, i, j: (j, 0)), _full((1, D)), _full((1, D))],
        out_specs=row,
        out_shape=jax.ShapeDtypeStruct((B, L, D), F32),
        scratch_shapes=[pltpu.VMEM((tm, D), BF16), pltpu.VMEM((tm, D), F32)],
        compiler_params=_cp("parallel", "parallel", "arbitrary"),
        name="ffn",
    )(x, shift, scale, gate, w1.astype(BF16), w3.astype(BF16), w2.astype(BF16), ln_g[None, :], ln_b[None, :])


MOE_TOKENS = 2048
MOE_ROWS = 256
RANK_CHUNK = 256


def _router_body(x_ref, sh_ref, sc_ref, wr_ref, h_ref, g_ref, rk_ref, rkt_ref, cnt_ref):
    h = x_ref[0] * (1.0 + sc_ref[0]) + sh_ref[0]
    h_ref[0] = h.astype(BF16)
    logits = jnp.dot(h, wr_ref[...], precision=HI, preferred_element_type=F32)
    lane = lax.broadcasted_iota(jnp.int32, logits.shape, 1).astype(F32)
    logits = jnp.where(lane < N_EXPERTS, logits, -jnp.inf)
    m1 = jnp.max(logits, axis=1, keepdims=True)
    i1 = jnp.min(jnp.where(logits == m1, lane, float(LANES)), axis=1, keepdims=True)
    rest = jnp.where(lane == i1, -jnp.inf, logits)
    m2 = jnp.max(rest, axis=1, keepdims=True)
    i2 = jnp.min(jnp.where(rest == m2, lane, float(LANES)), axis=1, keepdims=True)
    e2 = jnp.exp(m2 - m1)
    w1 = 1.0 / (1.0 + e2)
    w2 = e2 / (1.0 + e2)
    g_ref[0] = jnp.where(lane == i1, w1, 0.0) + jnp.where(lane == i2, w2, 0.0)
    sel = jnp.logical_or(lane == i1, lane == i2)
    self_ = sel.astype(F32)
    tm = h.shape[0]
    C = min(RANK_CHUNK, tm)
    r = lax.broadcasted_iota(jnp.int32, (C, C), 0)
    c = lax.broadcasted_iota(jnp.int32, (C, C), 1)
    tri = (c < r).astype(BF16)
    carry = jnp.zeros((1, LANES), F32)
    parts = []
    for k in range(tm // C):
        sk = self_[k * C:(k + 1) * C]
        parts.append(jnp.dot(tri, sk.astype(BF16), preferred_element_type=F32) + carry)
        carry = carry + jnp.sum(sk, axis=0, keepdims=True)
    rank = jnp.where(sel, jnp.concatenate(parts, axis=0), -1.0)
    rk_ref[0] = rank
    rkt_ref[0] = rank.T[:8]
    cnt_ref[0, 0] = carry


def _router(x, shift, scale, w_router):
    B, L, D = x.shape
    tm = min(MOE_TOKENS, L)
    nt = L // tm
    wr = jnp.pad(w_router, ((0, 0), (0, LANES - N_EXPERTS)))
    vec = pl.BlockSpec((1, 1, D), lambda b, i: (b, 0, 0))
    col = pl.BlockSpec((1, tm, LANES), lambda b, i: (b, i, 0))
    return pl.pallas_call(
        _router_body,
        grid=(B, nt),
        in_specs=[pl.BlockSpec((1, tm, D), lambda b, i: (b, i, 0)), vec, vec, _full((D, LANES))],
        out_specs=[pl.BlockSpec((1, tm, D), lambda b, i: (b, i, 0)), col, col,
                   pl.BlockSpec((1, 8, tm), lambda b, i: (b, 0, i)), pl.BlockSpec((1, 1, 1, LANES), lambda b, i: (b, i, 0, 0))],
        out_shape=[jax.ShapeDtypeStruct((B, L, D), BF16), jax.ShapeDtypeStruct((B, L, LANES), F32),
                   jax.ShapeDtypeStruct((B, L, LANES), F32), jax.ShapeDtypeStruct((B, 8, L), F32),
                   jax.ShapeDtypeStruct((B, nt, 1, LANES), F32)],
        compiler_params=_cp("parallel", "parallel"),
        name="moe_router",
    )(x, shift, scale, wr)


def _moe_body(cnt_ref, h_ref, g_ref, rk_ref, rkt_ref, w1_ref, w3_ref, w2_ref, o_ref, xg_ref, y_ref, *, M, P):
    b, i, e, j = pl.program_id(0), pl.program_id(1), pl.program_id(2), pl.program_id(3)
    nt, ne, nj = pl.num_programs(1), pl.num_programs(2), pl.num_programs(3)
    tm = h_ref.shape[1]
    cnt = cnt_ref[(b * nt + i) * ne + e]
    n_ch = lax.div(cnt + (M - 1), M)

    @pl.when(jnp.logical_and(e == 0, j == 0))
    def _():
        o_ref[...] = jnp.zeros_like(o_ref)

    @pl.when(j == 0)
    def _():
        rkt = rkt_ref[0, pl.ds(e, 1), :]

        def gather(c, carry):
            r0 = pl.multiple_of(c * M, 16)
            rows = (lax.broadcasted_iota(jnp.int32, (M, 1), 0) + c * M).astype(F32)
            onehot = (rkt == rows).astype(BF16)
            xg_ref[pl.ds(r0, M), :] = jnp.dot(onehot, h_ref[0], preferred_element_type=F32).astype(BF16)
            return carry

        lax.fori_loop(0, n_ch, gather, 0)

    def expert(chunks):
        r0 = [pl.multiple_of(c * M, 16) for c in chunks]
        xg = [xg_ref[pl.ds(r, M), :] for r in r0]
        a = [jnp.dot(x_, w1_ref[0], preferred_element_type=F32) for x_ in xg]
        g = [jnp.dot(x_, w3_ref[0], preferred_element_type=F32) for x_ in xg]
        t = [(a_ * jax.nn.sigmoid(a_) * g_).astype(BF16) for a_, g_ in zip(a, g)]
        yv = [jnp.dot(t_, w2_ref[0], preferred_element_type=F32) for t_ in t]

        @pl.when(j == 0)
        def _():
            for r, y_ in zip(r0, yv):
                y_ref[pl.ds(r, M), :] = y_

        @pl.when(j > 0)
        def _():
            for r, y_ in zip(r0, yv):
                y_ref[pl.ds(r, M), :] += y_

    def expert_pair(c2, carry):
        expert([2 * c2, 2 * c2 + 1])
        return carry

    lax.fori_loop(0, lax.div(n_ch, 2), expert_pair, 0)

    @pl.when(lax.rem(n_ch, 2) == 1)
    def _():
        expert([n_ch - 1])

    @pl.when(j == nj - 1)
    def _():
        for p in range(tm // P):
            lane = lax.broadcasted_iota(jnp.int32, (P, LANES), 1)
            rke = jnp.sum(jnp.where(lane == e, rk_ref[0, p * P:(p + 1) * P, :], 0.0), axis=1, keepdims=True)
            ge = jnp.sum(jnp.where(lane == e, g_ref[0, p * P:(p + 1) * P, :], 0.0), axis=1, keepdims=True)

            def scatter(c, carry):
                r0 = pl.multiple_of(c * M, 16)
                cols = (lax.broadcasted_iota(jnp.int32, (1, M), 1) + c * M).astype(F32)
                onehot = (rke == cols).astype(BF16)
                yb = y_ref[pl.ds(r0, M), :].astype(BF16)
                o_ref[0, p * P:(p + 1) * P, :] += ge * jnp.dot(onehot, yb, preferred_element_type=F32)
                return carry

            lax.fori_loop(0, n_ch, scatter, 0)


def _res_ln_body(x_ref, y_ref, gate_ref, lg_ref, lb_ref, o_ref, *, alpha):
    o_ref[0] = _layernorm_rows(alpha * x_ref[0] + gate_ref[0] * y_ref[0], lg_ref[...], lb_ref[...])


def _res_ln(x, y, gate, ln_g, ln_b, alpha):
    B, L, D = x.shape
    tm = min(WIDE_ROW_TILE, L)
    row = pl.BlockSpec((1, tm, D), lambda b, i: (b, i, 0))
    return pl.pallas_call(
        functools.partial(_res_ln_body, alpha=alpha),
        grid=(B, L // tm),
        in_specs=[row, row, pl.BlockSpec((1, 1, D), lambda b, i: (b, 0, 0)), _full((1, D)), _full((1, D))],
        out_specs=row,
        out_shape=jax.ShapeDtypeStruct((B, L, D), F32),
        compiler_params=_cp("parallel", "parallel"),
        name="res_ln",
    )(x, y, gate, ln_g[None, :], ln_b[None, :])


def _moe(x, shift, scale, gate, w_router, w1, w3, w2, ln_g, ln_b, alpha):
    B, L, D = x.shape
    E, _, F = w1.shape
    hb, gts, rk, rkt, cnt = _router(x, shift, scale, w_router)
    tm = min(MOE_TOKENS, L)
    nt = L // tm
    M = MOE_ROWS
    rows_max = -(-tm // M) * M
    tf = next(t for t in (896, 512, 256, 128, F) if F % t == 0)
    counts = cnt[:, :, 0, :E].astype(jnp.int32).reshape(-1)
    row = lambda w: pl.BlockSpec((1, tm, w), lambda b, i, e, j, c: (b, i, 0))
    y = pl.pallas_call(
        functools.partial(_moe_body, M=M, P=min(512, tm)),
        grid_spec=pltpu.PrefetchScalarGridSpec(
            num_scalar_prefetch=1,
            grid=(B, nt, E, F // tf),
            in_specs=[row(D), row(LANES), row(LANES), pl.BlockSpec((1, 8, tm), lambda b, i, e, j, c: (b, 0, i)),
                      pl.BlockSpec((1, D, tf), lambda b, i, e, j, c: (e, 0, j)),
                      pl.BlockSpec((1, D, tf), lambda b, i, e, j, c: (e, 0, j)),
                      pl.BlockSpec((1, tf, D), lambda b, i, e, j, c: (e, j, 0))],
            out_specs=row(D),
            scratch_shapes=[pltpu.VMEM((rows_max, D), BF16), pltpu.VMEM((rows_max, D), F32)],
        ),
        out_shape=jax.ShapeDtypeStruct((B, L, D), F32),
        compiler_params=_cp("parallel", "parallel", "arbitrary", "arbitrary"),
        name="moe",
    )(counts, hb, gts, rk, rkt, w1.astype(BF16), w3.astype(BF16), w2.astype(BF16))
    return _res_ln(x, y, gate, ln_g, ln_b, alpha)


def _mod_body(c_ref, w_ref, b_ref, o_ref):
    c = c_ref[...]
    s = c * jax.nn.sigmoid(c)
    o_ref[...] = jnp.dot(s, w_ref[...], precision=HI, preferred_element_type=F32) + b_ref[...]


def _modulation(cc, w_mod, b_mod):
    R, D = cc.shape
    N = w_mod.shape[1]
    tn = 1024
    return pl.pallas_call(
        _mod_body,
        grid=(N // tn,),
        in_specs=[_full((R, D)), pl.BlockSpec((D, tn), lambda j: (0, j)), pl.BlockSpec((1, tn), lambda j: (0, j))],
        out_specs=pl.BlockSpec((R, tn), lambda j: (0, j)),
        out_shape=jax.ShapeDtypeStruct((R, N), F32),
        compiler_params=_cp("parallel"),
        name="modulation",
    )(cc, w_mod, b_mod[None, :])


def _streams(x, c, ctx, c_ctx, w_mod, b_mod, w_in, gla_w_gate, gla_b_gate, gla_norm_g, hy_conv_w, hy_conv_b, hy_f_w1, hy_f_b1, hy_f_freq1, hy_f_w2, hy_f_b2, hy_f_freq2, hy_f_w3, hy_f_b3, hy_skip, hy_norm_g, mla_q_norm_g, mla_w_uq, mla_kv_norm_g, mla_w_ukv, mla_norm_g, w_out, ln_g, ln_b, ffn_w1, ffn_w3, ffn_w2, moe_router, moe_w1, moe_w3, moe_w2):
    B, L, D = x.shape
    Lc = ctx.shape[1]
    depth = w_mod.shape[0]
    alpha = (2.0 * depth) ** 0.25
    cc = jnp.zeros((8, D), F32).at[:B].set(c).at[B].set(c_ctx)
    cos, sin = _rope_tables(L, True)
    cos_c, sin_c = _rope_tables(Lc, False)
    cos_all, sin_all = jnp.concatenate([cos_c, cos], axis=0), jnp.concatenate([sin_c, sin], axis=0)
    KD, VD = GLA_HEADS * GLA_DK, GLA_HEADS * GLA_DV
    xc = ctx
    for l in range(depth):
        need_ctx = l < depth - 1
        mods = _modulation(cc, w_mod[l], b_mod[l])
        m = [mods[:B, k * D:(k + 1) * D][:, None, :] for k in range(6)]
        mc = [jnp.broadcast_to(mods[B, k * D:(k + 1) * D][None, None, :], (B, 1, D)) for k in range(6)]
        w_arr = _arrange_w_in(w_in[l])
        wg, bg = _arrange_gate(gla_w_gate[l], gla_b_gate[l])
        filt = (hy_f_w1[l], hy_f_b1[l], hy_f_freq1[l], hy_f_w2[l], hy_f_b2[l], hy_f_freq2[l], hy_f_w3[l], hy_f_b3[l])
        wq = _arrange_wq(mla_w_uq[l])
        wk, wv = _arrange_wkv(mla_w_ukv[l])

        hyu, qk, vg, alr, cq, ckvr = _inproj(x, m[0], m[1], w_arr)
        hyu_c, qk_c, vg_c, alr_c, cq_c, ckvr_c = _inproj(xc, mc[0], mc[1], w_arr)

        of_c, ob_c, s_c = _gla(qk_c, vg_c, alr_c, wg, bg, jnp.zeros((B, 2, KD, VD), F32))
        of, ob, _ = _gla(qk, vg, alr, wg, bg, s_c)
        hy = _hyena(hyu, hy_conv_w[l], hy_conv_b[l], filt, hy_skip[l])
        k_all, v_all = _kvproj(jnp.concatenate([ckvr_c, ckvr], axis=1), mla_kv_norm_g[l], wk, wv, cos_all, sin_all)
        k_c, v_c = k_all[:, :Lc], v_all[:, :Lc]
        q_m = _qproj(cq, mla_q_norm_g[l], wq, cos, sin)
        om = _flash(q_m, k_all, v_all)

        x = _outproj(of, ob, vg, hy, om, x, m[2], gla_norm_g[l], hy_norm_g[l], mla_norm_g[l], w_out[l],
                     ln_g[l, 0], ln_b[l, 0], alpha)
        if need_ctx:
            hy_c = _hyena_ctx(hyu_c, hy_conv_w[l], hy_conv_b[l], filt, hy_skip[l])
            q_c = _qproj(cq_c, mla_q_norm_g[l], wq, cos_c, sin_c)
            om_c = _flash(q_c, k_c, v_c)
            xc = _outproj(of_c, ob_c, vg_c, hy_c, om_c, xc, mc[2], gla_norm_g[l], hy_norm_g[l], mla_norm_g[l],
                          w_out[l], ln_g[l, 0], ln_b[l, 0], alpha)

        i = l // 2
        if l % 2 == 0:
            x = _ffn(x, m[3], m[4], m[5], ffn_w1[i], ffn_w3[i], ffn_w2[i], ln_g[l, 1], ln_b[l, 1], alpha)
            if need_ctx:
                xc = _ffn(xc, mc[3], mc[4], mc[5], ffn_w1[i], ffn_w3[i], ffn_w2[i], ln_g[l, 1], ln_b[l, 1], alpha)
        else:
            x = _moe(x, m[3], m[4], m[5], moe_router[i], moe_w1[i], moe_w3[i], moe_w2[i], ln_g[l, 1], ln_b[l, 1], alpha)
            if need_ctx:
                xc = _moe(xc, mc[3], mc[4], mc[5], moe_router[i], moe_w1[i], moe_w3[i], moe_w2[i], ln_g[l, 1],
                          ln_b[l, 1], alpha)
    return x, xc


def kernel(x, c, ctx, c_ctx, w_mod, b_mod, w_in, gla_w_gate, gla_b_gate, gla_norm_g, hy_conv_w, hy_conv_b, hy_f_w1, hy_f_b1, hy_f_freq1, hy_f_w2, hy_f_b2, hy_f_freq2, hy_f_w3, hy_f_b3, hy_skip, hy_norm_g, mla_q_norm_g, mla_w_uq, mla_kv_norm_g, mla_w_ukv, mla_norm_g, w_out, ln_g, ln_b, ffn_w1, ffn_w3, ffn_w2, moe_router, moe_w1, moe_w3, moe_w2):
    return _streams(x, c, ctx, c_ctx, w_mod, b_mod, w_in, gla_w_gate, gla_b_gate, gla_norm_g, hy_conv_w, hy_conv_b, hy_f_w1, hy_f_b1, hy_f_freq1, hy_f_w2, hy_f_b2, hy_f_freq2, hy_f_w3, hy_f_b3, hy_skip, hy_norm_g, mla_q_norm_g, mla_w_uq, mla_kv_norm_g, mla_w_ukv, mla_norm_g, w_out, ln_g, ln_b, ffn_w1, ffn_w3, ffn_w2, moe_router, moe_w1, moe_w3, moe_w2)[0]
```

```python
import functools
import math

import numpy as np
import jax
import jax.numpy as jnp
from jax import lax
from jax.experimental import pallas as pl
from jax.experimental.pallas import tpu as pltpu

F32 = jnp.float32
BF16 = jnp.bfloat16
HI = lax.Precision.HIGHEST

GRID_W = 64
GLA_HEADS, GLA_DK, GLA_DV, GLA_RANK, GLA_TAU = 4, 32, 64, 16, 16.0
HY_CH, HY_EMB = 256, 33
HY_DECAY_TARGET, HY_FAST_DECAY, HY_SLOW_DECAY = 1e-2, 0.3, 1.5
MLA_HEADS, MLA_Q_RANK, MLA_KV_RANK, MLA_NOPE, MLA_ROPE, MLA_V = 8, 256, 128, 64, 32, 64
MLA_SCALE = (MLA_NOPE + MLA_ROPE) ** -0.5
ROPE_BASE = 10000.0
N_EXPERTS = 8
IN_SPLITS = (128, 128, 256, 256, 32, 768, 256, 128, 32)

LANES = 128
SUBLANES = 8
VMEM_LIMIT = 56 * 1024 * 1024

ROW_TILE = 512
WIDE_ROW_TILE = 1024
GLA_CHUNK = 128
DFT_N2 = 256


def _cp(*sem):
    return pltpu.CompilerParams(dimension_semantics=sem, vmem_limit_bytes=VMEM_LIMIT)


def _full(shape):
    n = len(shape)
    return pl.BlockSpec(shape, lambda *_: (0,) * n)


def _idiv(x, d):
    assert d & (d - 1) == 0
    return lax.shift_right_logical(x, int(math.log2(d)))


INPROJ_WIDTHS = (768, 256, 512, 128, 256, 256)


def _arrange_w_in(w):
    cuts = np.cumsum(IN_SPLITS)[:-1]
    qa, ka, va, ga, alr, hyu, cq, ckv, kr = jnp.split(w, [int(c) for c in cuts], axis=1)
    z96 = jnp.zeros((w.shape[0], 96), w.dtype)
    return jnp.concatenate([hyu, qa, ka, va, ga, alr, z96, cq, ckv, kr, z96], axis=1).astype(BF16)


def _inproj_body(x_ref, sh_ref, sc_ref, w_ref, *out_refs):
    h = x_ref[0] * (1.0 + sc_ref[0]) + sh_ref[0]
    acc = jnp.dot(h.astype(BF16), w_ref[...], preferred_element_type=F32)
    off = 0
    for r in out_refs:
        w = r.shape[-1]
        r[0] = acc[:, off:off + w].astype(r.dtype)
        off += w


def _inproj(x, shift, scale, w_arr):
    B, L, D = x.shape
    tm = min(ROW_TILE, L)
    n = w_arr.shape[1]
    row = lambda w: pl.BlockSpec((1, tm, w), lambda b, i: (b, i, 0))
    vec = pl.BlockSpec((1, 1, D), lambda b, i: (b, 0, 0))
    return pl.pallas_call(
        _inproj_body,
        grid=(B, L // tm),
        in_specs=[row(D), vec, vec, _full((D, n))],
        out_specs=[row(w) for w in INPROJ_WIDTHS],
        out_shape=[jax.ShapeDtypeStruct((B, L, w), BF16) for w in INPROJ_WIDTHS],
        compiler_params=_cp("parallel", "parallel"),
        name="inproj",
    )(x, shift, scale, w_arr)


def _log_sigmoid(z):
    return jnp.minimum(z, 0.0) - jnp.log1p(jnp.exp(-jnp.abs(z)))


def _gla_body(qkf_ref, vf_ref, af_ref, qkb_ref, vb_ref, ab_ref, wg_ref, bg_ref, s0_ref,
              of_ref, ob_ref, sout_ref, s_ref):
    i = pl.program_id(0)
    C = qkf_ref.shape[1]
    KD = GLA_HEADS * GLA_DK
    VD = GLA_HEADS * GLA_DV

    @pl.when(i == 0)
    def _():
        s_ref[...] = s0_ref[...]

    r = lax.broadcasted_iota(jnp.int32, (C, C), 0)
    c = lax.broadcasted_iota(jnp.int32, (C, C), 1)
    tris = ((c <= r).astype(F32), (c >= r).astype(F32))
    lane_k = _idiv(lax.broadcasted_iota(jnp.int32, (1, KD), 1), GLA_DK)
    lane_v = _idiv(lax.broadcasted_iota(jnp.int32, (1, VD), 1), GLA_DV)
    rk = _idiv(lax.broadcasted_iota(jnp.int32, (KD, VD), 0), GLA_DK)
    cv = _idiv(lax.broadcasted_iota(jnp.int32, (KD, VD), 1), GLA_DV)
    ones = jnp.ones((C, VD), F32)
    refs = ((qkf_ref, vf_ref, af_ref, of_ref), (qkb_ref, vb_ref, ab_ref, ob_ref))
    chains = [(b, d) for b in range(qkf_ref.shape[0]) for d in range(2)]

    z = [jnp.dot(refs[d][2][b].astype(F32), wg_ref[...], precision=HI, preferred_element_type=F32) + bg_ref[...]
         for b, d in chains]
    la = [_log_sigmoid(zz[:, d * KD:(d + 1) * KD]) / GLA_TAU for zz, (b, d) in zip(z, chains)]
    bb = [jnp.dot(tris[d], l_, precision=HI, preferred_element_type=F32) for l_, (b, d) in zip(la, chains)]
    tot_b = [lax.dot_general(l_, ones, (((0,), (0,)), ((), ())), precision=HI, preferred_element_type=F32) for l_ in la]
    qe, ke, kl, vb, s_old = [], [], [], [], []
    for n, (b, d) in enumerate(chains):
        qk = refs[d][0][b].astype(F32)
        q = qk[:, :KD] * (GLA_DK ** -0.5)
        k = qk[:, KD:]
        tot = jnp.sum(la[n], axis=0, keepdims=True)
        qe.append(q * jnp.exp(bb[n]))
        ke.append((k * jnp.exp(-bb[n])).astype(BF16))
        kl.append((k * jnp.exp(tot - bb[n])).astype(BF16))
        vb.append(refs[d][1][b].astype(BF16))
        s_old.append(s_ref[2 * b + d])
    o = [jnp.dot(qe[n].astype(BF16), s_old[n].astype(BF16), preferred_element_type=F32) for n in range(len(chains))]
    att = [[lax.dot_general(jnp.where(lane_k == h, qe[n], 0.0).astype(BF16), ke[n], (((1,), (1,)), ((), ())),
                            preferred_element_type=F32) for h in range(GLA_HEADS)] for n in range(len(chains))]
    kv = [lax.dot_general(kl[n], vb[n], (((0,), (0,)), ((), ())), preferred_element_type=F32) for n in range(len(chains))]
    for n, (b, d) in enumerate(chains):
        on = o[n]
        for h in range(GLA_HEADS):
            oh = jnp.dot((att[n][h] * tris[d]).astype(BF16), vb[n], preferred_element_type=F32)
            on = on + jnp.where(lane_v == h, oh, 0.0)
        refs[d][3][b] = on.astype(refs[d][3].dtype)
        s_ref[2 * b + d] = jnp.exp(tot_b[n]) * s_old[n] + jnp.where(rk == cv, kv[n], 0.0)

    @pl.when(i == pl.num_programs(0) - 1)
    def _():
        sout_ref[...] = s_ref[...]


def _gla(qk, vg, alr, wg, bg, s0):
    B, L, _ = qk.shape
    C = min(GLA_CHUNK, L)
    n = L // C
    KD, VD = GLA_HEADS * GLA_DK, GLA_HEADS * GLA_DV
    fwd = lambda w: pl.BlockSpec((B, C, w), lambda i: (0, i, 0))
    bwd = lambda w: pl.BlockSpec((B, C, w), lambda i: (0, n - 1 - i, 0))
    st = _full((2 * B, KD, VD))
    of, ob, s_out = pl.pallas_call(
        _gla_body,
        grid=(n,),
        in_specs=[fwd(2 * KD), fwd(VD), fwd(LANES), bwd(2 * KD), bwd(VD), bwd(LANES),
                  _full((LANES, 2 * KD)), _full((1, 2 * KD)), st],
        out_specs=[fwd(VD), bwd(VD), st],
        out_shape=[jax.ShapeDtypeStruct((B, L, VD), BF16), jax.ShapeDtypeStruct((B, L, VD), BF16),
                   jax.ShapeDtypeStruct((2 * B, KD, VD), F32)],
        scratch_shapes=[pltpu.VMEM((2 * B, KD, VD), F32)],
        compiler_params=_cp("arbitrary"),
        name="gla",
    )(qk, vg, alr, qk, vg, alr, wg, bg, s0.reshape(2 * B, KD, VD))
    return of, ob, s_out.reshape(B, 2, KD, VD)


def _arrange_gate(w_gate, b_gate):
    KD = GLA_HEADS * GLA_DK
    wg = jnp.zeros((LANES, 2 * KD), F32)
    wg = wg.at[:GLA_RANK, :KD].set(w_gate[0]).at[GLA_RANK:2 * GLA_RANK, KD:].set(w_gate[1])
    return wg, jnp.concatenate([b_gate[0], b_gate[1]])[None, :]


def _shortconv_body(x_ref, p_ref, n_ref, w_ref, b_ref, v_ref, x1_ref, x2_ref):
    i = pl.program_id(1)
    last = pl.num_programs(1) - 1
    x = x_ref[0].astype(F32)
    tm = x.shape[0]
    hb = p_ref.shape[1]
    prev = jnp.where(i > 0, p_ref[0].astype(F32)[hb - 1:hb, :], 0.0)
    nxt = jnp.where(i < last, n_ref[0].astype(F32)[0:1, :], 0.0)
    rid = lax.broadcasted_iota(jnp.int32, x.shape, 0)
    dn = jnp.where(rid == 0, prev, pltpu.roll(x, 1, 0))
    up = jnp.where(rid == tm - 1, nxt, pltpu.roll(x, tm - 1, 0))
    w = w_ref[...]
    y = b_ref[...] + dn * w[0:1] + x * w[1:2] + up * w[2:3]
    v_ref[0] = y[:, :HY_CH]
    x1_ref[0] = y[:, HY_CH:2 * HY_CH]
    x2_ref[0] = y[:, 2 * HY_CH:]


def _shortconv(u, w, b):
    B, L, W = u.shape
    tm = min(ROW_TILE, L)
    hb = 2 * SUBLANES
    nb = tm // hb
    row = pl.BlockSpec((1, tm, W), lambda b_, i: (b_, i, 0))
    prev = pl.BlockSpec((1, hb, W), lambda b_, i: (b_, jnp.maximum(i * nb - 1, 0), 0))
    nxt = pl.BlockSpec((1, hb, W), lambda b_, i: (b_, jnp.minimum((i + 1) * nb, L // hb - 1), 0))
    o = pl.BlockSpec((1, tm, HY_CH), lambda b_, i: (b_, i, 0))
    return pl.pallas_call(
        _shortconv_body,
        grid=(B, L // tm),
        in_specs=[row, prev, nxt, _full((3, W)), _full((1, W))],
        out_specs=[o, o, o],
        out_shape=[jax.ShapeDtypeStruct((B, L, HY_CH), F32)] * 3,
        compiler_params=_cp("parallel", "parallel"),
        name="shortconv",
    )(u, u, u, w, b[None, :])


def _filter_feats(L):
    pos = np.arange(L, dtype=np.float64)
    t = pos / (L - 1)
    bands = (HY_EMB - 1) // 2
    freqs = np.linspace(1e-4, bands - 1, bands)
    ang = (2.0 * math.pi * pos / L)[:, None] * freqs
    z = jnp.asarray(np.concatenate([t[:, None], np.cos(ang), -np.sin(ang)], axis=-1), dtype=F32)
    z = jnp.pad(z, ((0, 0), (0, LANES - HY_EMB)))
    deltas = np.abs(np.linspace(math.log(HY_DECAY_TARGET) / HY_SLOW_DECAY,
                                math.log(HY_DECAY_TARGET) / HY_FAST_DECAY, HY_CH))
    return z, jnp.asarray(np.tile(deltas, 4)[None, :], dtype=F32)


def _filter_body(z_ref, w1_ref, b1_ref, f1_ref, w2_ref, b2_ref, f2_ref, w3_ref, b3_ref, dl_ref,
                 h_ref, ss_ref, *, L):
    i = pl.program_id(0)
    z = z_ref[...]
    tm = z.shape[0]
    hid = jnp.sin(f1_ref[...] * (jnp.dot(z, w1_ref[...], precision=HI, preferred_element_type=F32) + b1_ref[...]))
    hid = jnp.sin(f2_ref[...] * (jnp.dot(hid, w2_ref[...], precision=HI, preferred_element_type=F32) + b2_ref[...]))
    h = jnp.dot(hid, w3_ref[...], precision=HI, preferred_element_type=F32) + b3_ref[...]
    pos = (lax.broadcasted_iota(jnp.int32, (tm, 1), 0) + i * tm).astype(F32)
    t = pos / (L - 1)
    h = h * jnp.exp(-t * dl_ref[...])

    @pl.when(i == 0)
    def _():
        ss_ref[...] = jnp.zeros_like(ss_ref)

    ss_ref[...] += jnp.sum(h * h, axis=0, keepdims=True)
    col = lax.broadcasted_iota(jnp.int32, h.shape, 1)
    is_bwd = (_idiv(col, HY_CH) & 1) == 1
    h_ref[...] = jnp.where(jnp.logical_and(is_bwd, pos == 0.0), 0.0, h)


def _filters(L, fw1, fb1, ff1, fw2, fb2, ff2, fw3, fb3):
    z, dl = _filter_feats(L)
    tm = min(WIDE_ROW_TILE, L)
    Hf = fw2.shape[0]
    w1 = jnp.pad(fw1, ((0, LANES - HY_EMB), (0, 0)))
    NC = fw3.shape[1]
    return pl.pallas_call(
        functools.partial(_filter_body, L=L),
        grid=(L // tm,),
        in_specs=[pl.BlockSpec((tm, LANES), lambda i: (i, 0)), _full((LANES, Hf)), _full((1, Hf)), _full((1, Hf)),
                  _full((Hf, Hf)), _full((1, Hf)), _full((1, Hf)), _full((Hf, NC)), _full((1, NC)), _full((1, NC))],
        out_specs=[pl.BlockSpec((tm, NC), lambda i: (i, 0)), _full((1, NC))],
        out_shape=[jax.ShapeDtypeStruct((L, NC), F32), jax.ShapeDtypeStruct((1, NC), F32)],
        compiler_params=_cp("arbitrary"),
        name="hy_filters",
    )(z, w1, fb1[None], ff1[None], fw2, fb2[None], ff2[None], fw3, fb3[None], dl)


def _dft_consts(L):
    N = 2 * L
    N2 = DFT_N2
    N1 = N // N2
    half = N1 // 2
    k1 = np.arange(N1)[:, None].astype(np.float64)
    n1 = np.arange(N1)[None, :].astype(np.float64)
    a1 = 2.0 * np.pi * k1 * n1 / N1
    f1r, f1i = np.cos(a1), -np.sin(a1)
    fa = np.concatenate([f1r[:, :half], f1i[:, :half]], axis=0)
    fb = np.concatenate([f1r[:half, :], f1i[:half, :]], axis=1) / N
    k2 = np.arange(N2)[:, None].astype(np.float64)
    n2 = np.arange(N2)[None, :].astype(np.float64)
    a2 = 2.0 * np.pi * k2 * n2 / N2
    f2r, f2i = np.cos(a2), -np.sin(a2)
    g = np.block([[f2r, -f2i], [f2i, f2r]])
    gc = np.block([[f2r, f2i], [-f2i, f2r]])
    at = 2.0 * np.pi * (np.arange(N1)[:, None] * np.arange(N2)[None, :] % N) / N
    twr, twi = np.cos(at), -np.sin(at)
    c = lambda a: jnp.asarray(a, dtype=F32)
    bc = lambda a: jnp.broadcast_to(c(a)[:, :, None], (N1, N2, LANES))
    eye = np.eye(SUBLANES)
    return dict(N1=N1, N2=N2, half=half, fa=c(np.kron(fa, eye)), fb=c(np.kron(fb, eye)), g=c(g), gc=c(gc),
                twr=bc(twr), twi=bc(twi))


def _lanes(t, width):
    return jnp.concatenate([t] * (width // LANES), axis=-1)


def _dft1_body(f_ref, x_ref, o_ref):
    half, R, W = x_ref.shape[1], o_ref.shape[1], x_ref.shape[3]
    groups = [slice(s, s + SUBLANES) for s in range(0, x_ref.shape[2], SUBLANES)]
    x2 = [x_ref[0, :, g, :].reshape(half * SUBLANES, W).astype(BF16) for g in groups]
    y = [jnp.dot(f_ref[...], x_, preferred_element_type=F32) for x_ in x2]
    for g, y_ in zip(groups, y):
        o_ref[0, :, g, :] = y_.reshape(R, SUBLANES, W)


def _dft_stage1(fa, x):
    B, half, N2, W = x.shape
    R = fa.shape[0] // SUBLANES
    qb = min(DFT3_ROWS, N2) if W <= HY_CH else SUBLANES
    return pl.pallas_call(
        _dft1_body,
        grid=(B, N2 // qb),
        in_specs=[_full(fa.shape), pl.BlockSpec((1, half, qb, W), lambda b, j: (b, 0, j, 0))],
        out_specs=pl.BlockSpec((1, R, qb, W), lambda b, j: (b, 0, j, 0)),
        out_shape=jax.ShapeDtypeStruct((B, R, N2, W), F32),
        compiler_params=_cp("parallel", "parallel"),
        name="hy_dft1",
    )(fa.astype(BF16), x)


def _filter_spec_body(a_ref, twr_ref, twi_ref, g_ref, ss_ref, hf_ref):
    W = a_ref.shape[-1]
    ar, ai = a_ref[0, 0], a_ref[1, 0]
    twr, twi = _lanes(twr_ref[0], W), _lanes(twi_ref[0], W)
    xr = ar * twr - ai * twi
    xi = ar * twi + ai * twr
    z = jnp.dot(g_ref[...], jnp.concatenate([xr, xi], axis=0).astype(BF16), preferred_element_type=F32)
    n2 = z.shape[0] // 2
    zr, zi = z[:n2], z[n2:]
    ss = ss_ref[...]
    for o in range(2):
        f0, b0 = (2 * o) * HY_CH, (2 * o + 1) * HY_CH
        sc = lax.rsqrt(ss[:, f0:f0 + HY_CH] + ss[:, b0:b0 + HY_CH] + 1e-6)
        hf_ref[o, 0, 0] = ((zr[:, f0:f0 + HY_CH] + zr[:, b0:b0 + HY_CH]) * sc).astype(hf_ref.dtype)
        hf_ref[o, 0, 1] = ((zi[:, f0:f0 + HY_CH] - zi[:, b0:b0 + HY_CH]) * sc).astype(hf_ref.dtype)


def _filter_spectrum(h, ss, dc):
    L, NC = h.shape
    N1, N2, half = dc["N1"], dc["N2"], dc["half"]
    a = _dft_stage1(dc["fa"], h.reshape(1, half, N2, NC))
    a = a.reshape(2, N1, N2, NC)
    return pl.pallas_call(
        _filter_spec_body,
        grid=(N1,),
        in_specs=[pl.BlockSpec((2, 1, N2, NC), lambda k: (0, k, 0, 0)),
                  pl.BlockSpec((1, N2, LANES), lambda k: (k, 0, 0)), pl.BlockSpec((1, N2, LANES), lambda k: (k, 0, 0)),
                  _full((2 * N2, 2 * N2)), _full((1, NC))],
        out_specs=pl.BlockSpec((2, 1, 2, N2, HY_CH), lambda k: (0, k, 0, 0, 0)),
        out_shape=jax.ShapeDtypeStruct((2, N1, 2, N2, HY_CH), BF16),
        compiler_params=_cp("parallel"),
        name="hy_filter_spec",
    )(a, dc["twr"], dc["twi"], dc["g"].astype(BF16), ss)


SPEC_K1 = 8


def _spec_mul_body(a_ref, twr_ref, twi_ref, g_ref, gc_ref, hf_ref, o_ref):
    W = a_ref.shape[-1]
    ks = range(a_ref.shape[2])
    n2 = g_ref.shape[0] // 2
    x = []
    for k in ks:
        ar, ai = a_ref[0, 0, k], a_ref[0, 1, k]
        twr, twi = _lanes(twr_ref[k], W), _lanes(twi_ref[k], W)
        x.append(jnp.concatenate([ar * twr - ai * twi, ar * twi + ai * twr], axis=0).astype(BF16))
    z = [jnp.dot(g_ref[...], x_, preferred_element_type=F32) for x_ in x]
    y = []
    for k, z_ in zip(ks, z):
        zr, zi = z_[:n2], z_[n2:]
        hr, hi = hf_ref[0, k, 0].astype(F32), hf_ref[0, k, 1].astype(F32)
        y.append(jnp.concatenate([zr * hr - zi * hi, zr * hi + zi * hr], axis=0).astype(BF16))
    b = [jnp.dot(gc_ref[...], y_, preferred_element_type=F32) for y_ in y]
    for k, b_ in zip(ks, b):
        br, bi = b_[:n2], b_[n2:]
        twr, twi = _lanes(twr_ref[k], W), _lanes(twi_ref[k], W)
        o_ref[0, 0, k] = br * twr + bi * twi
        o_ref[0, 1, k] = bi * twr - br * twi


def _spec_mul(a, hf, order, dc):
    B = a.shape[0]
    N1, N2 = dc["N1"], dc["N2"]
    C = a.shape[-1]
    kb = min(SPEC_K1, N1)
    blk = pl.BlockSpec((1, 2, kb, N2, C), lambda k, b: (b, 0, k, 0, 0))
    tw = pl.BlockSpec((kb, N2, LANES), lambda k, b: (k, 0, 0))
    return pl.pallas_call(
        _spec_mul_body,
        grid=(N1 // kb, B),
        in_specs=[blk, tw, tw, _full((2 * N2, 2 * N2)), _full((2 * N2, 2 * N2)),
                  pl.BlockSpec((1, kb, 2, N2, C), lambda k, b: (order, k, 0, 0, 0))],
        out_specs=blk,
        out_shape=jax.ShapeDtypeStruct(a.shape, F32),
        compiler_params=_cp("parallel", "parallel"),
        name="hy_spec_mul",
    )(a, dc["twr"], dc["twi"], dc["g"].astype(BF16), dc["gc"].astype(BF16), hf)


DFT3_ROWS = 2 * SUBLANES


def _dft3_body(f_ref, b_ref, u_ref, gate_ref, skip_ref, o_ref):
    R, half, C = b_ref.shape[1], o_ref.shape[1], b_ref.shape[3]
    groups = [slice(s, s + SUBLANES) for s in range(0, b_ref.shape[2], SUBLANES)]
    b2 = [b_ref[0, :, g, :].reshape(R * SUBLANES, C).astype(BF16) for g in groups]
    y = [jnp.dot(f_ref[...], b_, preferred_element_type=F32) for b_ in b2]
    for g, y_ in zip(groups, y):
        u = u_ref[0, :, g, :].reshape(half * SUBLANES, C)
        gate = gate_ref[0, :, g, :].reshape(half * SUBLANES, C)
        o_ref[0, :, g, :] = (gate * (y_ + u * skip_ref[...])).reshape(half, SUBLANES, C)


def _dft_stage3(fb, bm, u, gate, skip):
    B, R, N2, C = bm.shape
    half = u.shape[1]
    qb = min(DFT3_ROWS, N2)
    row = pl.BlockSpec((1, half, qb, C), lambda b, j: (b, 0, j, 0))
    return pl.pallas_call(
        _dft3_body,
        grid=(B, N2 // qb),
        in_specs=[_full(fb.shape), pl.BlockSpec((1, R, qb, C), lambda b, j: (b, 0, j, 0)), row, row, _full((1, C))],
        out_specs=row,
        out_shape=jax.ShapeDtypeStruct((B, half, N2, C), F32),
        compiler_params=_cp("parallel", "parallel"),
        name="hy_dft3",
    )(fb.astype(BF16), bm, u, gate, skip[None, :])


def _longconv_gated(u, gate, hf, order, skip, dc):
    B, L, C = u.shape
    N1, N2, half = dc["N1"], dc["N2"], dc["half"]
    u4 = u.reshape(B, half, N2, C)
    a = _dft_stage1(dc["fa"], u4).reshape(B, 2, N1, N2, C)
    bm = _spec_mul(a, hf, order, dc).reshape(B, 2 * N1, N2, C)
    return _dft_stage3(dc["fb"], bm, u4, gate.reshape(B, half, N2, C), skip).reshape(B, L, C)


def _hyena(hyu, conv_w, conv_b, filt, skip):
    B, L, _ = hyu.shape
    v, x1, x2 = _shortconv(hyu, conv_w, conv_b)
    h, ss = _filters(L, *filt)
    dc = _dft_consts(L)
    hf = _filter_spectrum(h, ss, dc)
    z1 = _longconv_gated(v, x1, hf, 0, skip[0], dc)
    return _longconv_gated(z1, x2, hf, 1, skip[1], dc)


def _hyena_ctx_body(v_ref, x1_ref, x2_ref, h_ref, ss_ref, skip_ref, fc_ref, gc_ref, o_ref):
    fc, gc = fc_ref[...], gc_ref[...]
    n = fc.shape[0] // 2
    ss = ss_ref[...]
    h = h_ref[...]

    def conv(u, o):
        f0, b0 = (2 * o) * HY_CH, (2 * o + 1) * HY_CH
        sc = lax.rsqrt(ss[:, f0:f0 + HY_CH] + ss[:, b0:b0 + HY_CH] + 1e-6)
        x = jnp.dot(fc, u, precision=HI, preferred_element_type=F32)
        hf = jnp.dot(fc, h[:, f0:f0 + HY_CH], precision=HI, preferred_element_type=F32)
        hb = jnp.dot(fc, h[:, b0:b0 + HY_CH], precision=HI, preferred_element_type=F32)
        hr = (hf[:n] + hb[:n]) * sc
        hi = (hf[n:] - hb[n:]) * sc
        yr = x[:n] * hr - x[n:] * hi
        yi = x[:n] * hi + x[n:] * hr
        y = jnp.dot(gc, jnp.concatenate([yr, yi], axis=0), precision=HI, preferred_element_type=F32)
        return y + u * skip_ref[o:o + 1, :]

    z1 = x1_ref[0] * conv(v_ref[0], 0)
    o_ref[0] = x2_ref[0] * conv(z1, 1)


def _hyena_ctx(hyu, conv_w, conv_b, filt, skip):
    B, L, _ = hyu.shape
    v, x1, x2 = _shortconv(hyu, conv_w, conv_b)
    h, ss = _filters(L, *filt)
    N = 2 * L
    ang = 2.0 * np.pi * (np.arange(N)[:, None] * np.arange(L)[None, :] % N) / N
    fr, fi = np.cos(ang), -np.sin(ang)
    fc = jnp.asarray(np.concatenate([fr, fi], axis=0), dtype=F32)
    gc = jnp.asarray(np.concatenate([fr.T, fi.T], axis=1) / N, dtype=F32)
    row = pl.BlockSpec((1, L, HY_CH), lambda b: (b, 0, 0))
    return pl.pallas_call(
        _hyena_ctx_body,
        grid=(B,),
        in_specs=[row, row, row, _full(h.shape), _full(ss.shape), _full(skip.shape), _full(fc.shape), _full(gc.shape)],
        out_specs=row,
        out_shape=jax.ShapeDtypeStruct((B, L, HY_CH), F32),
        compiler_params=_cp("parallel"),
        name="hyena_ctx",
    )(v, x1, x2, h, ss, skip, fc, gc)


HEAD_PAD = 128


def _rope_swap(w):
    a, b, c, d = w[..., 0:8], w[..., 8:16], w[..., 16:24], w[..., 24:32]
    return jnp.concatenate([-b, a, -d, c], axis=-1)


def _arrange_wq(w_uq):
    R = w_uq.shape[0]
    w = w_uq.reshape(R, MLA_HEADS, MLA_NOPE + MLA_ROPE)
    rope = w[..., MLA_NOPE:]
    out = jnp.concatenate([w[..., :MLA_NOPE], rope, _rope_swap(rope)], axis=-1)
    return out.reshape(R, MLA_HEADS * HEAD_PAD).astype(BF16)


def _arrange_wkv(w_ukv):
    R = w_ukv.shape[0]
    w = w_ukv.reshape(R, MLA_HEADS, MLA_NOPE + MLA_V)
    wk = jnp.concatenate([w[..., :MLA_NOPE], jnp.zeros((R, MLA_HEADS, HEAD_PAD - MLA_NOPE), w.dtype)], axis=-1)
    wv = w[..., MLA_NOPE:]
    return wk.reshape(R, MLA_HEADS * HEAD_PAD).astype(BF16), wv.reshape(R, MLA_HEADS * MLA_V).astype(BF16)


def _kr_place():
    e = np.zeros((LANES, MLA_HEADS * HEAD_PAD), np.float32)
    es = np.zeros((LANES, MLA_HEADS * HEAD_PAD), np.float32)
    for h in range(MLA_HEADS):
        base = h * HEAD_PAD + MLA_NOPE
        for j in range(MLA_ROPE):
            e[j, base + j] = 1.0
            blk, r = divmod(j, 16)
            if r < 8:
                es[16 * blk + r + 8, base + j] = -1.0
            else:
                es[16 * blk + r - 8, base + j] = 1.0
    return jnp.asarray(e).astype(BF16), jnp.asarray(es).astype(BF16)


def _rope_tables(L, rope):
    if rope:
        t = np.arange(L)
        row, col = (t // GRID_W).astype(np.float64), (t % GRID_W).astype(np.float64)
        half = MLA_ROPE // 2
        inv = ROPE_BASE ** (-np.arange(0, half, 2, dtype=np.float64) / half)
        ar, ac = row[:, None] * inv, col[:, None] * inv
        cos = jnp.asarray(np.concatenate([np.cos(ar), np.cos(ar), np.cos(ac), np.cos(ac)], axis=-1), dtype=F32)
        sin = jnp.asarray(np.concatenate([np.sin(ar), np.sin(ar), np.sin(ac), np.sin(ac)], axis=-1), dtype=F32)
    else:
        cos, sin = jnp.ones((L, MLA_ROPE), F32), jnp.zeros((L, MLA_ROPE), F32)
    return cos, sin


def _rms_rows(x, g, eps=1e-6):
    return x * lax.rsqrt(jnp.mean(x * x, axis=-1, keepdims=True) + eps) * g


def _qproj_body(cq_ref, g_ref, w_ref, t1_ref, t2_ref, q_ref):
    xn = _rms_rows(cq_ref[0].astype(F32), g_ref[...])
    acc = jnp.dot(xn.astype(BF16), w_ref[...], preferred_element_type=F32)
    W = acc.shape[1]
    t1, t2 = _lanes(t1_ref[...], W), _lanes(t2_ref[...], W)
    q_ref[0] = (acc * t1 + pltpu.roll(acc, W - MLA_ROPE, 1) * t2).astype(q_ref.dtype)


def _qproj(cq, g, wq, cos, sin):
    B, L, R = cq.shape
    tm = min(ROW_TILE, L)
    W = wq.shape[1]
    ones, zeros = jnp.ones((L, MLA_NOPE), F32), jnp.zeros((L, MLA_ROPE), F32)
    qs = MLA_SCALE * math.log2(math.e)
    t1 = jnp.concatenate([ones, cos, zeros], axis=-1) * qs
    t2 = jnp.concatenate([jnp.zeros((L, MLA_NOPE), F32), sin, zeros], axis=-1) * qs
    tab = pl.BlockSpec((tm, HEAD_PAD), lambda b, i: (i, 0))
    return pl.pallas_call(
        _qproj_body,
        grid=(B, L // tm),
        in_specs=[pl.BlockSpec((1, tm, R), lambda b, i: (b, i, 0)), _full((1, R)), _full((R, W)), tab, tab],
        out_specs=pl.BlockSpec((1, tm, W), lambda b, i: (b, i, 0)),
        out_shape=jax.ShapeDtypeStruct((B, L, W), BF16),
        compiler_params=_cp("parallel", "parallel"),
        name="mla_qproj",
    )(cq, g[None, :], wq, t1, t2)


def _kvproj_body(c_ref, g_ref, wk_ref, wv_ref, e_ref, es_ref, cos_ref, sin_ref, k_ref, v_ref):
    c = c_ref[0].astype(F32)
    R = MLA_KV_RANK
    xn = _rms_rows(c[:, :R], g_ref[...]).astype(BF16)
    kr = c[:, R:]
    acc = jnp.dot(xn, wk_ref[...], preferred_element_type=F32)
    acc += jnp.dot((kr * cos_ref[...]).astype(BF16), e_ref[...], preferred_element_type=F32)
    acc += jnp.dot((kr * sin_ref[...]).astype(BF16), es_ref[...], preferred_element_type=F32)
    k_ref[0] = acc.astype(k_ref.dtype)
    v_ref[0] = jnp.dot(xn, wv_ref[...], preferred_element_type=F32).astype(v_ref.dtype)


def _kvproj(ckvr, g, wk, wv, cos, sin):
    B, L, Wc = ckvr.shape
    tm = next(t for t in (1280, 512, 256, L) if L % t == 0)
    pad = jnp.zeros((L, LANES - MLA_ROPE), F32)
    cos_p, sin_p = jnp.concatenate([cos, pad], axis=-1), jnp.concatenate([sin, pad], axis=-1)
    e, es = _kr_place()
    tab = pl.BlockSpec((tm, LANES), lambda b, i: (i, 0))
    Wk, Wv = wk.shape[1], wv.shape[1]
    return pl.pallas_call(
        _kvproj_body,
        grid=(B, L // tm),
        in_specs=[pl.BlockSpec((1, tm, Wc), lambda b, i: (b, i, 0)), _full((1, MLA_KV_RANK)),
                  _full(wk.shape), _full(wv.shape), _full(e.shape), _full(es.shape), tab, tab],
        out_specs=[pl.BlockSpec((1, tm, Wk), lambda b, i: (b, i, 0)), pl.BlockSpec((1, tm, Wv), lambda b, i: (b, i, 0))],
        out_shape=[jax.ShapeDtypeStruct((B, L, Wk), BF16), jax.ShapeDtypeStruct((B, L, Wv), BF16)],
        compiler_params=_cp("parallel", "parallel"),
        name="mla_kvproj",
    )(ckvr, g[None, :], wk, wv, e, es, cos_p, sin_p)


FLASH_Q_TILE = 2048
FLASH_ROWS = 256
FLASH_KEYS = 256


def _flash_body(q_ref, k_ref, v_ref, o_ref, m_ref, l_ref, acc_ref, s_ref, *, R):
    j = pl.program_id(3)
    tq, tk = q_ref.shape[1], k_ref.shape[1]
    CK = FLASH_KEYS
    npc = CK // LANES

    @pl.when(j == 0)
    def _():
        m_ref[...] = jnp.full_like(m_ref, -jnp.inf)
        l_ref[...] = jnp.zeros_like(l_ref)
        acc_ref[...] = jnp.zeros_like(acc_ref)

    def pass1(a, r):
        lo, r0 = a * HEAD_PAD, r * R
        q = q_ref[0, r0:r0 + R, lo:lo + HEAD_PAD]
        mp = None
        for c in range(tk // CK):
            kc = k_ref[0, c * CK:(c + 1) * CK, lo:lo + HEAD_PAD]
            s = lax.dot_general(q, kc, (((1,), (1,)), ((), ())), preferred_element_type=F32)
            s_ref[r0:r0 + R, c * CK:(c + 1) * CK] = s
            for w in range(npc):
                pc = s[:, w * LANES:(w + 1) * LANES]
                mp = pc if mp is None else jnp.maximum(mp, pc)
        m_old = m_ref[a, r0:r0 + R, :]
        return m_old, jnp.maximum(m_old, jnp.max(mp, axis=1, keepdims=True))

    def pass2(a, r, m_old, m_new):
        r0 = r * R
        alpha = jnp.exp2(m_old - m_new)
        lp = jnp.zeros((R, LANES), F32)
        pv = jnp.zeros((R, 2 * MLA_V), F32)
        for c in range(tk // CK):
            s = s_ref[r0:r0 + R, c * CK:(c + 1) * CK]
            ps = [jnp.exp2(s[:, w * LANES:(w + 1) * LANES] - m_new) for w in range(npc)]
            for p_ in ps:
                lp = lp + p_
            p = jnp.concatenate(ps, axis=1).astype(BF16)
            pv = pv + jnp.dot(p, v_ref[0, c * CK:(c + 1) * CK, :], preferred_element_type=F32)
        l_ref[a, r0:r0 + R, :] = alpha * l_ref[a, r0:r0 + R, :] + jnp.sum(lp, axis=1, keepdims=True)
        acc_ref[a, r0:r0 + R, :] = alpha * acc_ref[a, r0:r0 + R, :] + pv
        m_ref[a, r0:r0 + R, :] = m_new

    assert tq // R >= 2
    blocks = [(a, r) for a in range(2) for r in range(tq // R)]
    pend = pass1(*blocks[0])
    for i, blk in enumerate(blocks):
        nxt = pass1(*blocks[i + 1]) if i + 1 < len(blocks) else None
        pass2(*blk, *pend)
        pend = nxt

    @pl.when(j == pl.num_programs(3) - 1)
    def _():
        lane = lax.broadcasted_iota(jnp.int32, acc_ref.shape[1:], 1)
        o_ref[0] = jnp.where(lane < MLA_V, acc_ref[0] / l_ref[0], acc_ref[1] / l_ref[1]).astype(o_ref.dtype)


def _flash_tiles(Lq, Lk):
    tq = min(FLASH_Q_TILE, Lq)
    tk = next(t for t in (3328, 1280, 1024, 512, 256, Lk) if Lk % t == 0)
    return tq, tk


def _flash(q, k, v):
    B, Lq, _ = q.shape
    Lk = k.shape[1]
    tq, tk = _flash_tiles(Lq, Lk)
    hp = MLA_HEADS // 2
    return pl.pallas_call(
        functools.partial(_flash_body, R=min(FLASH_ROWS, tq // 2)),
        grid=(B, hp, Lq // tq, Lk // tk),
        in_specs=[pl.BlockSpec((1, tq, 2 * HEAD_PAD), lambda b, h, i, j: (b, i, h)),
                  pl.BlockSpec((1, tk, 2 * HEAD_PAD), lambda b, h, i, j: (b, j, h)),
                  pl.BlockSpec((1, tk, 2 * MLA_V), lambda b, h, i, j: (b, j, h))],
        out_specs=pl.BlockSpec((1, tq, 2 * MLA_V), lambda b, h, i, j: (b, i, h)),
        out_shape=jax.ShapeDtypeStruct((B, Lq, MLA_HEADS * MLA_V), BF16),
        scratch_shapes=[pltpu.VMEM((2, tq, LANES), F32), pltpu.VMEM((2, tq, LANES), F32),
                        pltpu.VMEM((2, tq, 2 * MLA_V), F32), pltpu.VMEM((tq, tk), F32)],
        compiler_params=_cp("parallel", "parallel", "parallel", "arbitrary"),
        name="mla_flash",
    )(q, k, v)


def _layernorm_rows(x, g, b, eps=1e-5):
    mu = jnp.mean(x, axis=-1, keepdims=True)
    xc = x - mu
    var = jnp.mean(xc * xc, axis=-1, keepdims=True)
    return xc * lax.rsqrt(var + eps) * g + b


def _outproj_body(of_ref, ob_ref, g_ref, hy_ref, om_ref, x_ref, gate_ref, gg_ref, hg_ref, mg_ref,
                  w_ref, lg_ref, lb_ref, o_ref, *, alpha):
    VD = GLA_HEADS * GLA_DV
    tm = x_ref.shape[1]
    r = _idiv(lax.broadcasted_iota(jnp.int32, (VD, VD), 0), GLA_DV)
    c = _idiv(lax.broadcasted_iota(jnp.int32, (VD, VD), 1), GLA_DV)
    grp = (r == c).astype(F32)
    parts = [slice(0, tm // 2), slice(tm // 2, tm)] if tm % 32 == 0 else [slice(0, tm)]
    o = [of_ref[0, p, :].astype(F32) + ob_ref[0, p, :].astype(F32) for p in parts]
    ms = [jnp.dot(o_ * o_, grp, precision=HI, preferred_element_type=F32) * (1.0 / GLA_DV) for o_ in o]
    ys = []
    for p, o_, ms_ in zip(parts, o, ms):
        g = g_ref[0, p, :].astype(F32)
        ya = o_ * lax.rsqrt(ms_ + 1e-6) * gg_ref[...] * (g * jax.nn.sigmoid(g))
        yb = _rms_rows(hy_ref[0, p, :], hg_ref[...])
        yc = _rms_rows(om_ref[0, p, :].astype(F32), mg_ref[...])
        ys.append((ya.astype(BF16), yb.astype(BF16), yc.astype(BF16)))
    accs = []
    for ya, yb, yc in ys:
        acc = jnp.dot(ya, w_ref[0:VD, :], preferred_element_type=F32)
        acc += jnp.dot(yb, w_ref[VD:VD + HY_CH, :], preferred_element_type=F32)
        acc += jnp.dot(yc, w_ref[VD + HY_CH:, :], preferred_element_type=F32)
        accs.append(acc)
    for p, acc in zip(parts, accs):
        o_ref[0, p, :] = _layernorm_rows(alpha * x_ref[0, p, :] + gate_ref[0] * acc, lg_ref[...], lb_ref[...])


def _outproj(of, ob, vg, hy, om, x, gate, gla_g, hy_g, mla_g, w_out, ln_g, ln_b, alpha):
    B, L, D = x.shape
    tm = min(ROW_TILE, L)
    VD = GLA_HEADS * GLA_DV
    MD = MLA_HEADS * MLA_V
    row = lambda w: pl.BlockSpec((1, tm, w), lambda b, i: (b, i, 0))
    return pl.pallas_call(
        functools.partial(_outproj_body, alpha=alpha),
        grid=(B, L // tm),
        in_specs=[row(VD), row(VD), pl.BlockSpec((1, tm, VD), lambda b, i: (b, i, 1)), row(HY_CH), row(MD), row(D),
                  pl.BlockSpec((1, 1, D), lambda b, i: (b, 0, 0)), _full((1, VD)), _full((1, HY_CH)), _full((1, MD)),
                  _full(w_out.shape), _full((1, D)), _full((1, D))],
        out_specs=row(D),
        out_shape=jax.ShapeDtypeStruct((B, L, D), F32),
        compiler_params=_cp("parallel", "parallel"),
        name="outproj",
    )(of, ob, vg, hy, om, x, gate, jnp.tile(gla_g, GLA_HEADS)[None, :], hy_g[None, :], mla_g[None, :],
      w_out.astype(BF16), ln_g[None, :], ln_b[None, :])


def _ffn_body(x_ref, sh_ref, sc_ref, gate_ref, w1_ref, w3_ref, w2_ref, lg_ref, lb_ref, o_ref, h_ref, acc_ref, *, alpha):
    j = pl.program_id(2)

    @pl.when(j == 0)
    def _():
        h_ref[...] = (x_ref[0] * (1.0 + sc_ref[0]) + sh_ref[0]).astype(BF16)
        acc_ref[...] = jnp.zeros_like(acc_ref)

    h = h_ref[...]
    a = jnp.dot(h, w1_ref[...], preferred_element_type=F32)
    b = jnp.dot(h, w3_ref[...], preferred_element_type=F32)
    t = (a * jax.nn.sigmoid(a) * b).astype(BF16)
    acc_ref[...] += jnp.dot(t, w2_ref[...], preferred_element_type=F32)

    @pl.when(j == pl.num_programs(2) - 1)
    def _():
        o_ref[0] = _layernorm_rows(alpha * x_ref[0] + gate_ref[0] * acc_ref[...], lg_ref[...], lb_ref[...])


def _ffn(x, shift, scale, gate, w1, w3, w2, ln_g, ln_b, alpha):
    B, L, D = x.shape
    F = w1.shape[1]
    tf = next(t for t in (1408, 512, 256, 128, F) if F % t == 0)
    tm = min(ROW_TILE if tf > 512 else WIDE_ROW_TILE, L)
    row = pl.BlockSpec((1, tm, D), lambda b, i, j: (b, i, 0))
    vec = pl.BlockSpec((1, 1, D), lambda b, i, j: (b, 0, 0))
    return pl.pallas_call(
        functools.partial(_ffn_body, alpha=alpha),
        grid=(B, L // tm, F // tf),
        in_specs=[row, vec, vec, vec,
                  pl.BlockSpec((D, tf), lambda b, i, j: (0, j)), pl.BlockSpec((D, tf), lambda b, i, j: (0, j)),
                  pl.BlockSpec((tf, D), lambda b, i, j: (j, 0)), _full((1, D)), _full((1, D))],
        out_specs=row,
        out_shape=jax.ShapeDtypeStruct((B, L, D), F32),
        scratch_shapes=[pltpu.VMEM((tm, D), BF16), pltpu.VMEM((tm, D), F32)],
        compiler_params=_cp("parallel", "parallel", "arbitrary"),
        name="ffn",
    )(x, shift, scale, gate, w1.astype(BF16), w3.astype(BF16), w2.astype(BF16), ln_g[None, :], ln_b[None, :])


MOE_TOKENS = 2048
MOE_ROWS = 256
RANK_CHUNK = 256


def _router_body(x_ref, sh_ref, sc_ref, wr_ref, h_ref, g_ref, rk_ref, rkt_ref, cnt_ref):
    h = x_ref[0] * (1.0 + sc_ref[0]) + sh_ref[0]
    h_ref[0] = h.astype(BF16)
    logits = jnp.dot(h, wr_ref[...], precision=HI, preferred_element_type=F32)
    lane = lax.broadcasted_iota(jnp.int32, logits.shape, 1).astype(F32)
    logits = jnp.where(lane < N_EXPERTS, logits, -jnp.inf)
    m1 = jnp.max(logits, axis=1, keepdims=True)
    i1 = jnp.min(jnp.where(logits == m1, lane, float(LANES)), axis=1, keepdims=True)
    rest = jnp.where(lane == i1, -jnp.inf, logits)
    m2 = jnp.max(rest, axis=1, keepdims=True)
    i2 = jnp.min(jnp.where(rest == m2, lane, float(LANES)), axis=1, keepdims=True)
    e2 = jnp.exp(m2 - m1)
    w1 = 1.0 / (1.0 + e2)
    w2 = e2 / (1.0 + e2)
    g_ref[0] = jnp.where(lane == i1, w1, 0.0) + jnp.where(lane == i2, w2, 0.0)
    sel = jnp.logical_or(lane == i1, lane == i2)
    self_ = sel.astype(F32)
    tm = h.shape[0]
    C = min(RANK_CHUNK, tm)
    r = lax.broadcasted_iota(jnp.int32, (C, C), 0)
    c = lax.broadcasted_iota(jnp.int32, (C, C), 1)
    tri = (c < r).astype(BF16)
    carry = jnp.zeros((1, LANES), F32)
    parts = []
    for k in range(tm // C):
        sk = self_[k * C:(k + 1) * C]
        parts.append(jnp.dot(tri, sk.astype(BF16), preferred_element_type=F32) + carry)
        carry = carry + jnp.sum(sk, axis=0, keepdims=True)
    rank = jnp.where(sel, jnp.concatenate(parts, axis=0), -1.0)
    rk_ref[0] = rank
    rkt_ref[0] = rank.T[:8]
    cnt_ref[0, 0] = carry


def _router(x, shift, scale, w_router):
    B, L, D = x.shape
    tm = min(MOE_TOKENS, L)
    nt = L // tm
    wr = jnp.pad(w_router, ((0, 0), (0, LANES - N_EXPERTS)))
    vec = pl.BlockSpec((1, 1, D), lambda b, i: (b, 0, 0))
    col = pl.BlockSpec((1, tm, LANES), lambda b, i: (b, i, 0))
    return pl.pallas_call(
        _router_body,
        grid=(B, nt),
        in_specs=[pl.BlockSpec((1, tm, D), lambda b, i: (b, i, 0)), vec, vec, _full((D, LANES))],
        out_specs=[pl.BlockSpec((1, tm, D), lambda b, i: (b, i, 0)), col, col,
                   pl.BlockSpec((1, 8, tm), lambda b, i: (b, 0, i)), pl.BlockSpec((1, 1, 1, LANES), lambda b, i: (b, i, 0, 0))],
        out_shape=[jax.ShapeDtypeStruct((B, L, D), BF16), jax.ShapeDtypeStruct((B, L, LANES), F32),
                   jax.ShapeDtypeStruct((B, L, LANES), F32), jax.ShapeDtypeStruct((B, 8, L), F32),
                   jax.ShapeDtypeStruct((B, nt, 1, LANES), F32)],
        compiler_params=_cp("parallel", "parallel"),
        name="moe_router",
    )(x, shift, scale, wr)


def _moe_body(cnt_ref, h_ref, g_ref, rk_ref, rkt_ref, w1_ref, w3_ref, w2_ref, o_ref, xg_ref, y_ref, *, M, P):
    b, i, e, j = pl.program_id(0), pl.program_id(1), pl.program_id(2), pl.program_id(3)
    nt, ne, nj = pl.num_programs(1), pl.num_programs(2), pl.num_programs(3)
    tm = h_ref.shape[1]
    cnt = cnt_ref[(b * nt + i) * ne + e]
    n_ch = lax.div(cnt + (M - 1), M)

    @pl.when(jnp.logical_and(e == 0, j == 0))
    def _():
        o_ref[...] = jnp.zeros_like(o_ref)

    @pl.when(j == 0)
    def _():
        rkt = rkt_ref[0, pl.ds(e, 1), :]

        def gather(c, carry):
            r0 = pl.multiple_of(c * M, 16)
            rows = (lax.broadcasted_iota(jnp.int32, (M, 1), 0) + c * M).astype(F32)
            onehot = (rkt == rows).astype(BF16)
            xg_ref[pl.ds(r0, M), :] = jnp.dot(onehot, h_ref[0], preferred_element_type=F32).astype(BF16)
            return carry

        lax.fori_loop(0, n_ch, gather, 0)

    def expert(chunks):
        r0 = [pl.multiple_of(c * M, 16) for c in chunks]
        xg = [xg_ref[pl.ds(r, M), :] for r in r0]
        a = [jnp.dot(x_, w1_ref[0], preferred_element_type=F32) for x_ in xg]
        g = [jnp.dot(x_, w3_ref[0], preferred_element_type=F32) for x_ in xg]
        t = [(a_ * jax.nn.sigmoid(a_) * g_).astype(BF16) for a_, g_ in zip(a, g)]
        yv = [jnp.dot(t_, w2_ref[0], preferred_element_type=F32) for t_ in t]

        @pl.when(j == 0)
        def _():
            for r, y_ in zip(r0, yv):
                y_ref[pl.ds(r, M), :] = y_

        @pl.when(j > 0)
        def _():
            for r, y_ in zip(r0, yv):
                y_ref[pl.ds(r, M), :] += y_

    def expert_pair(c2, carry):
        expert([2 * c2, 2 * c2 + 1])
        return carry

    lax.fori_loop(0, lax.div(n_ch, 2), expert_pair, 0)

    @pl.when(lax.rem(n_ch, 2) == 1)
    def _():
        expert([n_ch - 1])

    @pl.when(j == nj - 1)
    def _():
        for p in range(tm // P):
            lane = lax.broadcasted_iota(jnp.int32, (P, LANES), 1)
            rke = jnp.sum(jnp.where(lane == e, rk_ref[0, p * P:(p + 1) * P, :], 0.0), axis=1, keepdims=True)
            ge = jnp.sum(jnp.where(lane == e, g_ref[0, p * P:(p + 1) * P, :], 0.0), axis=1, keepdims=True)

            def scatter(c, carry):
                r0 = pl.multiple_of(c * M, 16)
                cols = (lax.broadcasted_iota(jnp.int32, (1, M), 1) + c * M).astype(F32)
                onehot = (rke == cols).astype(BF16)
                yb = y_ref[pl.ds(r0, M), :].astype(BF16)
                o_ref[0, p * P:(p + 1) * P, :] += ge * jnp.dot(onehot, yb, preferred_element_type=F32)
                return carry

            lax.fori_loop(0, n_ch, scatter, 0)


def _res_ln_body(x_ref, y_ref, gate_ref, lg_ref, lb_ref, o_ref, *, alpha):
    o_ref[0] = _layernorm_rows(alpha * x_ref[0] + gate_ref[0] * y_ref[0], lg_ref[...], lb_ref[...])


def _res_ln(x, y, gate, ln_g, ln_b, alpha):
    B, L, D = x.shape
    tm = min(WIDE_ROW_TILE, L)
    row = pl.BlockSpec((1, tm, D), lambda b, i: (b, i, 0))
    return pl.pallas_call(
        functools.partial(_res_ln_body, alpha=alpha),
        grid=(B, L // tm),
        in_specs=[row, row, pl.BlockSpec((1, 1, D), lambda b, i: (b, 0, 0)), _full((1, D)), _full((1, D))],
        out_specs=row,
        out_shape=jax.ShapeDtypeStruct((B, L, D), F32),
        compiler_params=_cp("parallel", "parallel"),
        name="res_ln",
    )(x, y, gate, ln_g[None, :], ln_b[None, :])


def _moe(x, shift, scale, gate, w_router, w1, w3, w2, ln_g, ln_b, alpha):
    B, L, D = x.shape
    E, _, F = w1.shape
    hb, gts, rk, rkt, cnt = _router(x, shift, scale, w_router)
    tm = min(MOE_TOKENS, L)
    nt = L // tm
    M = MOE_ROWS
    rows_max = -(-tm // M) * M
    tf = next(t for t in (896, 512, 256, 128, F) if F % t == 0)
    counts = cnt[:, :, 0, :E].astype(jnp.int32).reshape(-1)
    row = lambda w: pl.BlockSpec((1, tm, w), lambda b, i, e, j, c: (b, i, 0))
    y = pl.pallas_call(
        functools.partial(_moe_body, M=M, P=min(512, tm)),
        grid_spec=pltpu.PrefetchScalarGridSpec(
            num_scalar_prefetch=1,
            grid=(B, nt, E, F // tf),
            in_specs=[row(D), row(LANES), row(LANES), pl.BlockSpec((1, 8, tm), lambda b, i, e, j, c: (b, 0, i)),
                      pl.BlockSpec((1, D, tf), lambda b, i, e, j, c: (e, 0, j)),
                      pl.BlockSpec((1, D, tf), lambda b, i, e, j, c: (e, 0, j)),
                      pl.BlockSpec((1, tf, D), lambda b, i, e, j, c: (e, j, 0))],
            out_specs=row(D),
            scratch_shapes=[pltpu.VMEM((rows_max, D), BF16), pltpu.VMEM((rows_max, D), F32)],
        ),
        out_shape=jax.ShapeDtypeStruct((B, L, D), F32),
        compiler_params=_cp("parallel", "parallel", "arbitrary", "arbitrary"),
        name="moe",
    )(counts, hb, gts, rk, rkt, w1.astype(BF16), w3.astype(BF16), w2.astype(BF16))
    return _res_ln(x, y, gate, ln_g, ln_b, alpha)


def _mod_body(c_ref, w_ref, b_ref, o_ref):
    c = c_ref[...]
    s = c * jax.nn.sigmoid(c)
    o_ref[...] = jnp.dot(s, w_ref[...], precision=HI, preferred_element_type=F32) + b_ref[...]


def _modulation(cc, w_mod, b_mod):
    R, D = cc.shape
    N = w_mod.shape[1]
    tn = 1024
    return pl.pallas_call(
        _mod_body,
        grid=(N // tn,),
        in_specs=[_full((R, D)), pl.BlockSpec((D, tn), lambda j: (0, j)), pl.BlockSpec((1, tn), lambda j: (0, j))],
        out_specs=pl.BlockSpec((R, tn), lambda j: (0, j)),
        out_shape=jax.ShapeDtypeStruct((R, N), F32),
        compiler_params=_cp("parallel"),
        name="modulation",
    )(cc, w_mod, b_mod[None, :])


def _streams(x, c, ctx, c_ctx, w_mod, b_mod, w_in, gla_w_gate, gla_b_gate, gla_norm_g, hy_conv_w, hy_conv_b, hy_f_w1, hy_f_b1, hy_f_freq1, hy_f_w2, hy_f_b2, hy_f_freq2, hy_f_w3, hy_f_b3, hy_skip, hy_norm_g, mla_q_norm_g, mla_w_uq, mla_kv_norm_g, mla_w_ukv, mla_norm_g, w_out, ln_g, ln_b, ffn_w1, ffn_w3, ffn_w2, moe_router, moe_w1, moe_w3, moe_w2):
    B, L, D = x.shape
    Lc = ctx.shape[1]
    depth = w_mod.shape[0]
    alpha = (2.0 * depth) ** 0.25
    cc = jnp.zeros((8, D), F32).at[:B].set(c).at[B].set(c_ctx)
    cos, sin = _rope_tables(L, True)
    cos_c, sin_c = _rope_tables(Lc, False)
    cos_all, sin_all = jnp.concatenate([cos_c, cos], axis=0), jnp.concatenate([sin_c, sin], axis=0)
    KD, VD = GLA_HEADS * GLA_DK, GLA_HEADS * GLA_DV
    xc = ctx
    for l in range(depth):
        need_ctx = l < depth - 1
        mods = _modulation(cc, w_mod[l], b_mod[l])
        m = [mods[:B, k * D:(k + 1) * D][:, None, :] for k in range(6)]
        mc = [jnp.broadcast_to(mods[B, k * D:(k + 1) * D][None, None, :], (B, 1, D)) for k in range(6)]
        w_arr = _arrange_w_in(w_in[l])
        wg, bg = _arrange_gate(gla_w_gate[l], gla_b_gate[l])
        filt = (hy_f_w1[l], hy_f_b1[l], hy_f_freq1[l], hy_f_w2[l], hy_f_b2[l], hy_f_freq2[l], hy_f_w3[l], hy_f_b3[l])
        wq = _arrange_wq(mla_w_uq[l])
        wk, wv = _arrange_wkv(mla_w_ukv[l])

        hyu, qk, vg, alr, cq, ckvr = _inproj(x, m[0], m[1], w_arr)
        hyu_c, qk_c, vg_c, alr_c, cq_c, ckvr_c = _inproj(xc, mc[0], mc[1], w_arr)

        of_c, ob_c, s_c = _gla(qk_c, vg_c, alr_c, wg, bg, jnp.zeros((B, 2, KD, VD), F32))
        of, ob, _ = _gla(qk, vg, alr, wg, bg, s_c)
        hy = _hyena(hyu, hy_conv_w[l], hy_conv_b[l], filt, hy_skip[l])
        k_all, v_all = _kvproj(jnp.concatenate([ckvr_c, ckvr], axis=1), mla_kv_norm_g[l], wk, wv, cos_all, sin_all)
        k_c, v_c = k_all[:, :Lc], v_all[:, :Lc]
        q_m = _qproj(cq, mla_q_norm_g[l], wq, cos, sin)
        om = _flash(q_m, k_all, v_all)

        x = _outproj(of, ob, vg, hy, om, x, m[2], gla_norm_g[l], hy_norm_g[l], mla_norm_g[l], w_out[l],
                     ln_g[l, 0], ln_b[l, 0], alpha)
        if need_ctx:
            hy_c = _hyena_ctx(hyu_c, hy_conv_w[l], hy_conv_b[l], filt, hy_skip[l])
            q_c = _qproj(cq_c, mla_q_norm_g[l], wq, cos_c, sin_c)
            om_c = _flash(q_c, k_c, v_c)
            xc = _outproj(of_c, ob_c, vg_c, hy_c, om_c, xc, mc[2], gla_norm_g[l], hy_norm_g[l], mla_norm_g[l],
                          w_out[l], ln_g[l, 0], ln_b[l, 0], alpha)

        i = l // 2
        if l % 2 == 0:
            x = _ffn(x, m[3], m[4], m[5], ffn_w1[i], ffn_w3[i], ffn_w2[i], ln_g[l, 1], ln_b[l, 1], alpha)
            if need_ctx:
                xc = _ffn(xc, mc[3], mc[4], mc[5], ffn_w1[i], ffn_w3[i], ffn_w2[i], ln_g[l, 1], ln_b[l, 1], alpha)
        else:
            x = _moe(x, m[3], m[4], m[5], moe_router[i], moe_w1[i], moe_w3[i], moe_w2[i], ln_g[l, 1], ln_b[l, 1], alpha)
            if need_ctx:
                xc = _moe(xc, mc[3], mc[4], mc[5], moe_router[i], moe_w1[i], moe_w3[i], moe_w2[i], ln_g[l, 1],
                          ln_b[l, 1], alpha)
    return x, xc


def kernel(x, c, ctx, c_ctx, w_mod, b_mod, w_in, gla_w_gate, gla_b_gate, gla_norm_g, hy_conv_w, hy_conv_b, hy_f_w1, hy_f_b1, hy_f_freq1, hy_f_w2, hy_f_b2, hy_f_freq2, hy_f_w3, hy_f_b3, hy_skip, hy_norm_g, mla_q_norm_g, mla_w_uq, mla_kv_norm_g, mla_w_ukv, mla_norm_g, w_out, ln_g, ln_b, ffn_w1, ffn_w3, ffn_w2, moe_router, moe_w1, moe_w3, moe_w2):
    return _streams(x, c, ctx, c_ctx, w_mod, b_mod, w_in, gla_w_gate, gla_b_gate, gla_norm_g, hy_conv_w, hy_conv_b, hy_f_w1, hy_f_b1, hy_f_freq1, hy_f_w2, hy_f_b2, hy_f_freq2, hy_f_w3, hy_f_b3, hy_skip, hy_norm_g, mla_q_norm_g, mla_w_uq, mla_kv_norm_g, mla_w_ukv, mla_norm_g, w_out, ln_g, ln_b, ffn_w1, ffn_w3, ffn_w2, moe_router, moe_w1, moe_w3, moe_w2)[0]
```
